```python
import math
import jax, jax.numpy as jnp
from jax import lax
import numpy as np

D_MODEL = 2048
BATCH = 8
SEQ = 4096
DEPTH = 1

CONV_WIDTH = D_MODEL // 2
SSM_WIDTH = D_MODEL // 2
CONV_KERNEL = 31
SSM_GROUP = 16
SSM_GROUPS = SSM_WIDTH // SSM_GROUP
SSM_STATE = 64
D_FF = 4 * D_MODEL
N_MOD = 6
IN_COLS = 2 * CONV_WIDTH + SSM_WIDTH + 2 * D_MODEL
EPS = 1e-6
DT_MIN = 1e-3
DT_MAX = 1e-1

kernel_name = "hybrid_conformer_s5_gated_block"


def rmsnorm(x, g):
    xf = x.astype(jnp.float32)
    y = xf * lax.rsqrt(jnp.mean(jnp.square(xf), axis=-1, keepdims=True) + EPS)
    return (y * g.astype(jnp.float32)).astype(x.dtype)


def layernorm(x, g, b):
    xf = x.astype(jnp.float32)
    mu = jnp.mean(xf, axis=-1, keepdims=True)
    var = jnp.mean(jnp.square(xf - mu), axis=-1, keepdims=True)
    y = (xf - mu) * lax.rsqrt(var + EPS)
    return (y * g.astype(jnp.float32) + b.astype(jnp.float32)).astype(x.dtype)


def conformer_conv(v_glu, w_dw, b_dw, ln_g, ln_b, w_conv_out):
    a, gate = jnp.split(v_glu, 2, axis=-1)
    v = a * jax.nn.sigmoid(gate)
    v = lax.conv_general_dilated(
        v, w_dw[:, None, :], window_strides=(1,), padding=[(CONV_KERNEL - 1, 0)],
        dimension_numbers=("NWC", "WIO", "NWC"), feature_group_count=CONV_WIDTH,
    ) + b_dw
    v = jax.nn.silu(layernorm(v, ln_g, ln_b))
    return v @ w_conv_out


def s5_ssm(u, a_re, a_im, log_dt, b_re, b_im, c_re, c_im, d_skip, w_glu_a, w_glu_b):
    bsz, seq, _ = u.shape
    uf = u.astype(jnp.float32)
    ug = uf.reshape(bsz, seq, SSM_GROUPS, SSM_GROUP)
    lam = lax.complex(a_re.astype(jnp.float32), a_im.astype(jnp.float32))
    dt = jnp.exp(log_dt.astype(jnp.float32))[:, None]
    lam_bar = jnp.exp(lam * dt)
    b_c = lax.complex(b_re.astype(jnp.float32), b_im.astype(jnp.float32))
    b_bar = ((lam_bar - 1.0) / lam)[..., None] * b_c
    bu = jnp.einsum("blgh,gph->blgp", ug.astype(b_bar.dtype), b_bar)
    a_seq = jnp.broadcast_to(lam_bar, bu.shape)

    def combine(e1, e2):
        a1, s1 = e1
        a2, s2 = e2
        return a1 * a2, a2 * s1 + s2

    _, states = lax.associative_scan(combine, (a_seq, bu), axis=1)
    c_c = lax.complex(c_re.astype(jnp.float32), c_im.astype(jnp.float32))
    y = jnp.real(jnp.einsum("blgp,ghp->blgh", states, c_c))
    y = y.reshape(bsz, seq, SSM_WIDTH) + d_skip.astype(jnp.float32) * uf
    y = jax.nn.gelu(y).astype(u.dtype)
    return (y @ w_glu_a) * jax.nn.sigmoid(y @ w_glu_b)


def _fwd_setup_inputs(seed: int = 0) -> dict:
    key = jax.random.key(seed)
    ks = jax.random.split(key, 32)
    f32 = jnp.float32

    def nrm(k, shape, scale):
        return jax.random.normal(k, shape, f32) * scale

    x = jax.random.normal(ks[0], (BATCH, SEQ, D_MODEL), f32)
    c = jax.random.normal(ks[1], (BATCH, D_MODEL), f32)
    w_ada = nrm(ks[2], (DEPTH, D_MODEL, N_MOD * D_MODEL), 0.5 * D_MODEL ** -0.5)
    b_ada = nrm(ks[3], (DEPTH, N_MOD * D_MODEL), 0.02)
    norm1_g = 1.0 + nrm(ks[4], (DEPTH, D_MODEL), 0.02)
    w_in = nrm(ks[5], (DEPTH, D_MODEL, IN_COLS), D_MODEL ** -0.5)
    w_dw = nrm(ks[6], (DEPTH, CONV_KERNEL, CONV_WIDTH), CONV_KERNEL ** -0.5)
    b_dw = nrm(ks[7], (DEPTH, CONV_WIDTH), 0.02)
    ln_g = 1.0 + nrm(ks[8], (DEPTH, CONV_WIDTH), 0.02)
    ln_b = nrm(ks[9], (DEPTH, CONV_WIDTH), 0.02)
    w_conv_out = nrm(ks[10], (DEPTH, CONV_WIDTH, D_MODEL), CONV_WIDTH ** -0.5)
    n_idx = jnp.arange(SSM_STATE, dtype=f32)
    a_re = -0.5 + nrm(ks[11], (DEPTH, SSM_GROUPS, SSM_STATE), 0.01)
    a_im = math.pi * n_idx + nrm(ks[12], (DEPTH, SSM_GROUPS, SSM_STATE), 0.01)
    log_dt = jax.random.uniform(ks[13], (DEPTH, SSM_GROUPS), f32, math.log(DT_MIN), math.log(DT_MAX))
    b_scale = (2.0 * SSM_GROUP) ** -0.5
    b_re = nrm(ks[14], (DEPTH, SSM_GROUPS, SSM_STATE, SSM_GROUP), b_scale)
    b_im = nrm(ks[15], (DEPTH, SSM_GROUPS, SSM_STATE, SSM_GROUP), b_scale)
    c_scale = (2.0 * SSM_STATE) ** -0.5
    c_re = nrm(ks[16], (DEPTH, SSM_GROUPS, SSM_GROUP, SSM_STATE), c_scale)
    c_im = nrm(ks[17], (DEPTH, SSM_GROUPS, SSM_GROUP, SSM_STATE), c_scale)
    d_skip = nrm(ks[18], (DEPTH, SSM_WIDTH), 1.0)
    w_glu_a = nrm(ks[19], (DEPTH, SSM_WIDTH, D_MODEL), SSM_WIDTH ** -0.5)
    w_glu_b = nrm(ks[20], (DEPTH, SSM_WIDTH, D_MODEL), SSM_WIDTH ** -0.5)
    w_out = nrm(ks[21], (DEPTH, D_MODEL, D_MODEL), D_MODEL ** -0.5)
    norm2_g = 1.0 + nrm(ks[22], (DEPTH, D_MODEL), 0.02)
    w_ff1 = nrm(ks[23], (DEPTH, D_MODEL, D_FF), D_MODEL ** -0.5)
    w_ff2 = nrm(ks[24], (DEPTH, D_FF, D_MODEL), D_FF ** -0.5)
    final_g = 1.0 + nrm(ks[25], (D_MODEL,), 0.02)
    return {
        "x": x, "c": c, "w_ada": w_ada, "b_ada": b_ada, "norm1_g": norm1_g, "w_in": w_in,
        "w_dw": w_dw, "b_dw": b_dw, "ln_g": ln_g, "ln_b": ln_b, "w_conv_out": w_conv_out,
        "a_re": a_re, "a_im": a_im, "log_dt": log_dt, "b_re": b_re, "b_im": b_im,
        "c_re": c_re, "c_im": c_im, "d_skip": d_skip, "w_glu_a": w_glu_a, "w_glu_b": w_glu_b,
        "w_out": w_out, "norm2_g": norm2_g, "w_ff1": w_ff1, "w_ff2": w_ff2, "final_g": final_g,
    }


def _fwd_reference(x, c, w_ada, b_ada, norm1_g, w_in, w_dw, b_dw, ln_g, ln_b, w_conv_out,
              a_re, a_im, log_dt, b_re, b_im, c_re, c_im, d_skip, w_glu_a, w_glu_b,
              w_out, norm2_g, w_ff1, w_ff2, final_g):
    h = x
    c_act = jax.nn.silu(c)
    split_pts = [2 * CONV_WIDTH, 2 * CONV_WIDTH + SSM_WIDTH, 2 * CONV_WIDTH + SSM_WIDTH + D_MODEL]
    for l in range(DEPTH):
        mod = c_act @ w_ada[l] + b_ada[l]
        shift1, scale1, gate1, shift2, scale2, gate2 = [m[:, None, :] for m in jnp.split(mod, N_MOD, axis=-1)]

        u = rmsnorm(h, norm1_g[l]) * (1.0 + scale1) + shift1
        proj = u @ w_in[l]
        v_conv, v_ssm, g_conv, g_ssm = jnp.split(proj, split_pts, axis=-1)
        y_conv = conformer_conv(v_conv, w_dw[l], b_dw[l], ln_g[l], ln_b[l], w_conv_out[l])
        y_ssm = s5_ssm(v_ssm, a_re[l], a_im[l], log_dt[l], b_re[l], b_im[l], c_re[l], c_im[l],
                       d_skip[l], w_glu_a[l], w_glu_b[l])
        merged = jax.nn.sigmoid(g_conv) * y_conv + jax.nn.sigmoid(g_ssm) * y_ssm
        h = h + gate1 * (merged @ w_out[l])

        z = rmsnorm(h, norm2_g[l]) * (1.0 + scale2) + shift2
        ff = jnp.square(jax.nn.relu(z @ w_ff1[l])) @ w_ff2[l]
        h = h + gate2 * ff
    return rmsnorm(h, final_g)


import jax as _jax
import jax.numpy as _jnp

TWIN_FORMAT = 'train_step'
FWD_PARAMS = ['x', 'c', 'w_ada', 'b_ada', 'norm1_g', 'w_in', 'w_dw', 'b_dw', 'ln_g', 'ln_b', 'w_conv_out', 'a_re', 'a_im', 'log_dt', 'b_re', 'b_im', 'c_re', 'c_im', 'd_skip', 'w_glu_a', 'w_glu_b', 'w_out', 'norm2_g', 'w_ff1', 'w_ff2', 'final_g']
TWIN_WEIGHTS = ['w_ada', 'b_ada', 'norm1_g', 'w_in', 'w_dw', 'b_dw', 'ln_g', 'ln_b', 'w_conv_out', 'a_re', 'a_im', 'log_dt', 'b_re', 'b_im', 'c_re', 'c_im', 'd_skip', 'w_glu_a', 'w_glu_b', 'w_out', 'norm2_g', 'w_ff1', 'w_ff2', 'final_g']
TWIN_DIFF_INPUT = 'x'
TWIN_INPUTS = ['x', 'c', 'w_ada', 'b_ada', 'norm1_g', 'w_in', 'w_dw', 'b_dw', 'ln_g', 'ln_b', 'w_conv_out', 'a_re', 'a_im', 'log_dt', 'b_re', 'b_im', 'c_re', 'c_im', 'd_skip', 'w_glu_a', 'w_glu_b', 'w_out', 'norm2_g', 'w_ff1', 'w_ff2', 'final_g', 'loss_target', 'm_w_ada', 'm_b_ada', 'm_norm1_g', 'm_w_in', 'm_w_dw', 'm_b_dw', 'm_ln_g', 'm_ln_b', 'm_w_conv_out', 'm_a_re', 'm_a_im', 'm_log_dt', 'm_b_re', 'm_b_im', 'm_c_re', 'm_c_im', 'm_d_skip', 'm_w_glu_a', 'm_w_glu_b', 'm_w_out', 'm_norm2_g', 'm_w_ff1', 'm_w_ff2', 'm_final_g', 'v_w_ada', 'v_b_ada', 'v_norm1_g', 'v_w_in', 'v_w_dw', 'v_b_dw', 'v_ln_g', 'v_ln_b', 'v_w_conv_out', 'v_a_re', 'v_a_im', 'v_log_dt', 'v_b_re', 'v_b_im', 'v_c_re', 'v_c_im', 'v_d_skip', 'v_w_glu_a', 'v_w_glu_b', 'v_w_out', 'v_norm2_g', 'v_w_ff1', 'v_w_ff2', 'v_final_g']
TWIN_OUTPUTS = ['loss', 'grad_x', 'grad_w_ada', 'grad_b_ada', 'grad_norm1_g', 'grad_w_in', 'grad_w_dw', 'grad_b_dw', 'grad_ln_g', 'grad_ln_b', 'grad_w_conv_out', 'grad_a_re', 'grad_a_im', 'grad_log_dt', 'grad_b_re', 'grad_b_im', 'grad_c_re', 'grad_c_im', 'grad_d_skip', 'grad_w_glu_a', 'grad_w_glu_b', 'grad_w_out', 'grad_norm2_g', 'grad_w_ff1', 'grad_w_ff2', 'grad_final_g', 'delta_w_ada', 'delta_b_ada', 'delta_norm1_g', 'delta_w_in', 'delta_w_dw', 'delta_b_dw', 'delta_ln_g', 'delta_ln_b', 'delta_w_conv_out', 'delta_a_re', 'delta_a_im', 'delta_log_dt', 'delta_b_re', 'delta_b_im', 'delta_c_re', 'delta_c_im', 'delta_d_skip', 'delta_w_glu_a', 'delta_w_glu_b', 'delta_w_out', 'delta_norm2_g', 'delta_w_ff1', 'delta_w_ff2', 'delta_final_g', 'new_m_w_ada', 'new_m_b_ada', 'new_m_norm1_g', 'new_m_w_in', 'new_m_w_dw', 'new_m_b_dw', 'new_m_ln_g', 'new_m_ln_b', 'new_m_w_conv_out', 'new_m_a_re', 'new_m_a_im', 'new_m_log_dt', 'new_m_b_re', 'new_m_b_im', 'new_m_c_re', 'new_m_c_im', 'new_m_d_skip', 'new_m_w_glu_a', 'new_m_w_glu_b', 'new_m_w_out', 'new_m_norm2_g', 'new_m_w_ff1', 'new_m_w_ff2', 'new_m_final_g', 'new_v_w_ada', 'new_v_b_ada', 'new_v_norm1_g', 'new_v_w_in', 'new_v_w_dw', 'new_v_b_dw', 'new_v_ln_g', 'new_v_ln_b', 'new_v_w_conv_out', 'new_v_a_re', 'new_v_a_im', 'new_v_log_dt', 'new_v_b_re', 'new_v_b_im', 'new_v_c_re', 'new_v_c_im', 'new_v_d_skip', 'new_v_w_glu_a', 'new_v_w_glu_b', 'new_v_w_out', 'new_v_norm2_g', 'new_v_w_ff1', 'new_v_w_ff2', 'new_v_final_g']
TWIN_LEAF_KINDS = {'loss': 'loss', 'grad_x': 'grad_x', 'grad_w_ada': 'grad_w', 'grad_b_ada': 'grad_w', 'grad_norm1_g': 'grad_w', 'grad_w_in': 'grad_w', 'grad_w_dw': 'grad_w', 'grad_b_dw': 'grad_w', 'grad_ln_g': 'grad_w', 'grad_ln_b': 'grad_w', 'grad_w_conv_out': 'grad_w', 'grad_a_re': 'grad_w', 'grad_a_im': 'grad_w', 'grad_log_dt': 'grad_w', 'grad_b_re': 'grad_w', 'grad_b_im': 'grad_w', 'grad_c_re': 'grad_w', 'grad_c_im': 'grad_w', 'grad_d_skip': 'grad_w', 'grad_w_glu_a': 'grad_w', 'grad_w_glu_b': 'grad_w', 'grad_w_out': 'grad_w', 'grad_norm2_g': 'grad_w', 'grad_w_ff1': 'grad_w', 'grad_w_ff2': 'grad_w', 'grad_final_g': 'grad_w', 'delta_w_ada': 'delta_w', 'delta_b_ada': 'delta_w', 'delta_norm1_g': 'delta_w', 'delta_w_in': 'delta_w', 'delta_w_dw': 'delta_w', 'delta_b_dw': 'delta_w', 'delta_ln_g': 'delta_w', 'delta_ln_b': 'delta_w', 'delta_w_conv_out': 'delta_w', 'delta_a_re': 'delta_w', 'delta_a_im': 'delta_w', 'delta_log_dt': 'delta_w', 'delta_b_re': 'delta_w', 'delta_b_im': 'delta_w', 'delta_c_re': 'delta_w', 'delta_c_im': 'delta_w', 'delta_d_skip': 'delta_w', 'delta_w_glu_a': 'delta_w', 'delta_w_glu_b': 'delta_w', 'delta_w_out': 'delta_w', 'delta_norm2_g': 'delta_w', 'delta_w_ff1': 'delta_w', 'delta_w_ff2': 'delta_w', 'delta_final_g': 'delta_w', 'new_m_w_ada': 'new_m', 'new_m_b_ada': 'new_m', 'new_m_norm1_g': 'new_m', 'new_m_w_in': 'new_m', 'new_m_w_dw': 'new_m', 'new_m_b_dw': 'new_m', 'new_m_ln_g': 'new_m', 'new_m_ln_b': 'new_m', 'new_m_w_conv_out': 'new_m', 'new_m_a_re': 'new_m', 'new_m_a_im': 'new_m', 'new_m_log_dt': 'new_m', 'new_m_b_re': 'new_m', 'new_m_b_im': 'new_m', 'new_m_c_re': 'new_m', 'new_m_c_im': 'new_m', 'new_m_d_skip': 'new_m', 'new_m_w_glu_a': 'new_m', 'new_m_w_glu_b': 'new_m', 'new_m_w_out': 'new_m', 'new_m_norm2_g': 'new_m', 'new_m_w_ff1': 'new_m', 'new_m_w_ff2': 'new_m', 'new_m_final_g': 'new_m', 'new_v_w_ada': 'new_v', 'new_v_b_ada': 'new_v', 'new_v_norm1_g': 'new_v', 'new_v_w_in': 'new_v', 'new_v_w_dw': 'new_v', 'new_v_b_dw': 'new_v', 'new_v_ln_g': 'new_v', 'new_v_ln_b': 'new_v', 'new_v_w_conv_out': 'new_v', 'new_v_a_re': 'new_v', 'new_v_a_im': 'new_v', 'new_v_log_dt': 'new_v', 'new_v_b_re': 'new_v', 'new_v_b_im': 'new_v', 'new_v_c_re': 'new_v', 'new_v_c_im': 'new_v', 'new_v_d_skip': 'new_v', 'new_v_w_glu_a': 'new_v', 'new_v_w_glu_b': 'new_v', 'new_v_w_out': 'new_v', 'new_v_norm2_g': 'new_v', 'new_v_w_ff1': 'new_v', 'new_v_w_ff2': 'new_v', 'new_v_final_g': 'new_v'}


def _forward(args):
    return _fwd_reference(*[args[k] for k in FWD_PARAMS])


def _output_shape():
    def fwd():
        inp = _fwd_setup_inputs(0)
        return _fwd_reference(*[inp[k] for k in FWD_PARAMS])
    out = _jax.eval_shape(fwd)
    return out.shape, out.dtype

N_MICROBATCH = 1
ADAM_LR = 0.001
ADAM_B1 = 0.9
ADAM_B2 = 0.999
ADAM_EPS = 1e-08
ADAM_WD = 0.01
ADAM_STEP = 10
PER_EXAMPLE_BATCH_AXIS = {'x': 0, 'c': 0, 'loss_target': 0}
SHARED_INPUTS = []
_WEIGHT_DTYPES = {'w_ada': _jnp.float32, 'b_ada': _jnp.float32, 'norm1_g': _jnp.float32, 'w_in': _jnp.float32, 'w_dw': _jnp.float32, 'b_dw': _jnp.float32, 'ln_g': _jnp.float32, 'ln_b': _jnp.float32, 'w_conv_out': _jnp.float32, 'a_re': _jnp.float32, 'a_im': _jnp.float32, 'log_dt': _jnp.float32, 'b_re': _jnp.float32, 'b_im': _jnp.float32, 'c_re': _jnp.float32, 'c_im': _jnp.float32, 'd_skip': _jnp.float32, 'w_glu_a': _jnp.float32, 'w_glu_b': _jnp.float32, 'w_out': _jnp.float32, 'norm2_g': _jnp.float32, 'w_ff1': _jnp.float32, 'w_ff2': _jnp.float32, 'final_g': _jnp.float32}
MOMENT_SCALE = {'w_ada': 3.994320e-02, 'b_ada': 7.516844e-02, 'norm1_g': 1.189404e-02, 'w_in': 6.637727e-03, 'w_dw': 1.304986e-02, 'b_dw': 2.458042e-02, 'ln_g': 1.672848e-02, 'ln_b': 1.453161e-02, 'w_conv_out': 8.938688e-03, 'a_re': 5.895284e-04, 'a_im': 5.055427e-04, 'log_dt': 2.330574e-01, 'b_re': 3.271449e-04, 'b_im': 3.373532e-04, 'c_re': 6.844372e-04, 'c_im': 6.676144e-04, 'd_skip': 8.733821e-03, 'w_glu_a': 5.634198e-03, 'w_glu_b': 1.766321e-03, 'w_out': 1.057877e-02, 'norm2_g': 3.910201e-02, 'w_ff1': 2.049348e-02, 'w_ff2': 3.845225e-02, 'final_g': 1.605123e+01}


def _to_microbatches(a, axis):
    t = _jnp.moveaxis(a, axis, 0)
    t = t.reshape((N_MICROBATCH, t.shape[0] // N_MICROBATCH) + t.shape[1:])
    return _jnp.moveaxis(t, 1, axis + 1)


def setup_inputs(seed: int = 0) -> dict:
    inp = _fwd_setup_inputs(seed)
    key = _jax.random.fold_in(_jax.random.key(seed), 7919)
    shape, _ = _output_shape()
    out = dict(inp)
    out["loss_target"] = _jax.random.normal(_jax.random.fold_in(key, 0), shape, _jnp.float32)
    for i, name in enumerate(TWIN_WEIGHTS):
        w = inp[name].astype(_jnp.float32)
        if MOMENT_SCALE is None:
            s = _jnp.sqrt(_jnp.mean(_jnp.square(w)) + 1e-30)
        else:
            s = MOMENT_SCALE[name]
        km, kv = _jax.random.split(_jax.random.fold_in(key, i + 1))
        out[name] = w
        out["m_" + name] = s * _jax.random.normal(km, w.shape, _jnp.float32)
        out["v_" + name] = (s * s) * _jax.random.uniform(kv, w.shape, _jnp.float32, 0.5, 1.5)
    if N_MICROBATCH > 1:
        for name, axis in PER_EXAMPLE_BATCH_AXIS.items():
            out[name] = _to_microbatches(out[name], axis)
    return {'x': out['x'], 'c': out['c'], 'w_ada': out['w_ada'], 'b_ada': out['b_ada'], 'norm1_g': out['norm1_g'], 'w_in': out['w_in'], 'w_dw': out['w_dw'], 'b_dw': out['b_dw'], 'ln_g': out['ln_g'], 'ln_b': out['ln_b'], 'w_conv_out': out['w_conv_out'], 'a_re': out['a_re'], 'a_im': out['a_im'], 'log_dt': out['log_dt'], 'b_re': out['b_re'], 'b_im': out['b_im'], 'c_re': out['c_re'], 'c_im': out['c_im'], 'd_skip': out['d_skip'], 'w_glu_a': out['w_glu_a'], 'w_glu_b': out['w_glu_b'], 'w_out': out['w_out'], 'norm2_g': out['norm2_g'], 'w_ff1': out['w_ff1'], 'w_ff2': out['w_ff2'], 'final_g': out['final_g'], 'loss_target': out['loss_target'], 'm_w_ada': out['m_w_ada'], 'm_b_ada': out['m_b_ada'], 'm_norm1_g': out['m_norm1_g'], 'm_w_in': out['m_w_in'], 'm_w_dw': out['m_w_dw'], 'm_b_dw': out['m_b_dw'], 'm_ln_g': out['m_ln_g'], 'm_ln_b': out['m_ln_b'], 'm_w_conv_out': out['m_w_conv_out'], 'm_a_re': out['m_a_re'], 'm_a_im': out['m_a_im'], 'm_log_dt': out['m_log_dt'], 'm_b_re': out['m_b_re'], 'm_b_im': out['m_b_im'], 'm_c_re': out['m_c_re'], 'm_c_im': out['m_c_im'], 'm_d_skip': out['m_d_skip'], 'm_w_glu_a': out['m_w_glu_a'], 'm_w_glu_b': out['m_w_glu_b'], 'm_w_out': out['m_w_out'], 'm_norm2_g': out['m_norm2_g'], 'm_w_ff1': out['m_w_ff1'], 'm_w_ff2': out['m_w_ff2'], 'm_final_g': out['m_final_g'], 'v_w_ada': out['v_w_ada'], 'v_b_ada': out['v_b_ada'], 'v_norm1_g': out['v_norm1_g'], 'v_w_in': out['v_w_in'], 'v_w_dw': out['v_w_dw'], 'v_b_dw': out['v_b_dw'], 'v_ln_g': out['v_ln_g'], 'v_ln_b': out['v_ln_b'], 'v_w_conv_out': out['v_w_conv_out'], 'v_a_re': out['v_a_re'], 'v_a_im': out['v_a_im'], 'v_log_dt': out['v_log_dt'], 'v_b_re': out['v_b_re'], 'v_b_im': out['v_b_im'], 'v_c_re': out['v_c_re'], 'v_c_im': out['v_c_im'], 'v_d_skip': out['v_d_skip'], 'v_w_glu_a': out['v_w_glu_a'], 'v_w_glu_b': out['v_w_glu_b'], 'v_w_out': out['v_w_out'], 'v_norm2_g': out['v_norm2_g'], 'v_w_ff1': out['v_w_ff1'], 'v_w_ff2': out['v_w_ff2'], 'v_final_g': out['v_final_g']}


def _loss(weights, diff, rest, loss_target):
    with _jax.named_scope("forward"):
        args = {**rest, TWIN_DIFF_INPUT: diff, **{k: w.astype(_WEIGHT_DTYPES[k]) for k, w in weights.items()}}
        y = _forward(args)
    with _jax.named_scope("loss_head"):
        err = _jnp.square(y.astype(_jnp.float32) - loss_target)
        return 0.5 * _jnp.sum(_jnp.mean(err, axis=-1)) if err.ndim else 0.5 * err


def _adamw(w, g, m, v):
    m = ADAM_B1 * m + (1.0 - ADAM_B1) * g
    v = ADAM_B2 * v + (1.0 - ADAM_B2) * _jnp.square(g)
    m_hat = m / (1.0 - ADAM_B1 ** ADAM_STEP)
    v_hat = v / (1.0 - ADAM_B2 ** ADAM_STEP)
    delta = -ADAM_LR * (m_hat / (_jnp.sqrt(v_hat) + ADAM_EPS) + ADAM_WD * w)
    return delta, m, v


def reference(x, c, w_ada, b_ada, norm1_g, w_in, w_dw, b_dw, ln_g, ln_b, w_conv_out, a_re, a_im, log_dt, b_re, b_im, c_re, c_im, d_skip, w_glu_a, w_glu_b, w_out, norm2_g, w_ff1, w_ff2, final_g, loss_target, m_w_ada, m_b_ada, m_norm1_g, m_w_in, m_w_dw, m_b_dw, m_ln_g, m_ln_b, m_w_conv_out, m_a_re, m_a_im, m_log_dt, m_b_re, m_b_im, m_c_re, m_c_im, m_d_skip, m_w_glu_a, m_w_glu_b, m_w_out, m_norm2_g, m_w_ff1, m_w_ff2, m_final_g, v_w_ada, v_b_ada, v_norm1_g, v_w_in, v_w_dw, v_b_dw, v_ln_g, v_ln_b, v_w_conv_out, v_a_re, v_a_im, v_log_dt, v_b_re, v_b_im, v_c_re, v_c_im, v_d_skip, v_w_glu_a, v_w_glu_b, v_w_out, v_norm2_g, v_w_ff1, v_w_ff2, v_final_g):
    given = dict(x=x, c=c, w_ada=w_ada, b_ada=b_ada, norm1_g=norm1_g, w_in=w_in, w_dw=w_dw, b_dw=b_dw, ln_g=ln_g, ln_b=ln_b, w_conv_out=w_conv_out, a_re=a_re, a_im=a_im, log_dt=log_dt, b_re=b_re, b_im=b_im, c_re=c_re, c_im=c_im, d_skip=d_skip, w_glu_a=w_glu_a, w_glu_b=w_glu_b, w_out=w_out, norm2_g=norm2_g, w_ff1=w_ff1, w_ff2=w_ff2, final_g=final_g, loss_target=loss_target, m_w_ada=m_w_ada, m_b_ada=m_b_ada, m_norm1_g=m_norm1_g, m_w_in=m_w_in, m_w_dw=m_w_dw, m_b_dw=m_b_dw, m_ln_g=m_ln_g, m_ln_b=m_ln_b, m_w_conv_out=m_w_conv_out, m_a_re=m_a_re, m_a_im=m_a_im, m_log_dt=m_log_dt, m_b_re=m_b_re, m_b_im=m_b_im, m_c_re=m_c_re, m_c_im=m_c_im, m_d_skip=m_d_skip, m_w_glu_a=m_w_glu_a, m_w_glu_b=m_w_glu_b, m_w_out=m_w_out, m_norm2_g=m_norm2_g, m_w_ff1=m_w_ff1, m_w_ff2=m_w_ff2, m_final_g=m_final_g, v_w_ada=v_w_ada, v_b_ada=v_b_ada, v_norm1_g=v_norm1_g, v_w_in=v_w_in, v_w_dw=v_w_dw, v_b_dw=v_b_dw, v_ln_g=v_ln_g, v_ln_b=v_ln_b, v_w_conv_out=v_w_conv_out, v_a_re=v_a_re, v_a_im=v_a_im, v_log_dt=v_log_dt, v_b_re=v_b_re, v_b_im=v_b_im, v_c_re=v_c_re, v_c_im=v_c_im, v_d_skip=v_d_skip, v_w_glu_a=v_w_glu_a, v_w_glu_b=v_w_glu_b, v_w_out=v_w_out, v_norm2_g=v_norm2_g, v_w_ff1=v_w_ff1, v_w_ff2=v_w_ff2, v_final_g=v_final_g)
    weights = {n: given[n] for n in TWIN_WEIGHTS}
    shared = {n: given[n] for n in SHARED_INPUTS}
    per_example = {n: given[n] for n in ['x', 'c']}
    grad_fn = _jax.value_and_grad(_loss, argnums=(0, 1))

    def one_microbatch(ex, loss_target):
        ex = dict(ex)
        diff = ex.pop(TWIN_DIFF_INPUT)
        return grad_fn(weights, diff, {**shared, **ex}, loss_target)

    if N_MICROBATCH == 1:
        loss, (grad_w, grad_x) = one_microbatch(per_example, given["loss_target"])
    else:
        def body(carry, xs):
            loss_sum, grad_sum = carry
            l_k, (gw_k, gx_k) = one_microbatch(xs[0], xs[1])
            with _jax.named_scope("update"):
                return (loss_sum + l_k, _jax.tree.map(_jnp.add, grad_sum, gw_k)), gx_k

        init = (_jnp.zeros((), _jnp.float32), _jax.tree.map(_jnp.zeros_like, weights))
        (loss, grad_w), grad_x = _jax.lax.scan(body, init, (per_example, given["loss_target"]))
    with _jax.named_scope("update"):
        delta_w, new_m, new_v = {}, {}, {}
        for n in TWIN_WEIGHTS:
            delta_w[n], new_m[n], new_v[n] = _adamw(weights[n], grad_w[n], given["m_" + n], given["v_" + n])
    return (loss, grad_x, *[grad_w[n] for n in TWIN_WEIGHTS], *[delta_w[n] for n in TWIN_WEIGHTS],
            *[new_m[n] for n in TWIN_WEIGHTS], *[new_v[n] for n in TWIN_WEIGHTS])
```

```python
import functools
import math

import jax
import jax.numpy as jnp
from jax import lax
from jax.experimental import pallas as pl
from jax.experimental.pallas import tpu as pltpu

F32 = jnp.float32
BF16 = jnp.bfloat16
NDEV = 8
EPS = 1e-6
ADAM_LR, ADAM_B1, ADAM_B2, ADAM_EPS, ADAM_WD, ADAM_STEP = 0.001, 0.9, 0.999, 1e-08, 0.01, 10
CONV_K = 31
HALO = 32
GROUP = 16
STATE = 64
GB = 8
HI = lax.Precision.HIGHEST
MESH = pl.DeviceIdType.MESH
VMEM_LIMIT = 56 * 1024 * 1024
PACK_ROWS = 64
PACK = PACK_ROWS * 1024
ANY = pl.BlockSpec(memory_space=pl.ANY)


def _params(sem=None):
    if sem is None:
        return pltpu.CompilerParams(vmem_limit_bytes=VMEM_LIMIT)
    return pltpu.CompilerParams(dimension_semantics=sem, vmem_limit_bytes=VMEM_LIMIT)


def _sigmoid(v):
    return 1.0 / (1.0 + jnp.exp(-v))


def _me():
    return 4 * lax.axis_index("x") + 2 * lax.axis_index("y") + lax.axis_index("c")


def _peer(k):
    x, y, c = lax.axis_index("x"), lax.axis_index("y"), lax.axis_index("c")
    px = 1 - x if (k >> 2) & 1 else x
    py = 1 - y if (k >> 1) & 1 else y
    pc = 1 - c if k & 1 else c
    return (px, py, pc), 4 * px + 2 * py + pc


def _exchange(arrays, name, gather):
    n = len(arrays)
    out_shape = []
    for a in arrays:
        shp = (NDEV,) + a.shape if gather else a.shape
        out_shape.append(jax.ShapeDtypeStruct(shp, a.dtype))

    def body(*refs):
        ins, outs = refs[:n], refs[n:2 * n]
        send, recv, lsem = refs[2 * n:]
        me = _me()
        local = []
        for a in range(n):
            src = ins[a] if gather else ins[a].at[me]
            cp = pltpu.make_async_copy(src, outs[a].at[me], lsem.at[a])
            cp.start()
            local.append(cp)
        sends = []
        for a in range(n):
            for k in range(1, NDEV):
                dev, pidx = _peer(k)
                src = ins[a] if gather else ins[a].at[pidx]
                cp = pltpu.make_async_remote_copy(
                    src_ref=src, dst_ref=outs[a].at[me], send_sem=send.at[a * (NDEV - 1) + k - 1], recv_sem=recv.at[a * (NDEV - 1) + k - 1],
                    device_id=dev, device_id_type=MESH)
                cp.start()
                sends.append(cp)
        for a in range(n):
            for k in range(1, NDEV):
                dev, pidx = _peer(k)
                src = ins[a] if gather else ins[a].at[pidx]
                pltpu.make_async_remote_copy(
                    src_ref=src, dst_ref=outs[a].at[pidx], send_sem=send.at[a * (NDEV - 1) + k - 1], recv_sem=recv.at[a * (NDEV - 1) + k - 1],
                    device_id=dev, device_id_type=MESH).wait_recv()
        for cp in sends:
            cp.wait_send()
        for cp in local:
            cp.wait()

    return pl.pallas_call(
        body, name=name, out_shape=tuple(out_shape),
        in_specs=[ANY] * n, out_specs=tuple([ANY] * n),
        scratch_shapes=[pltpu.SemaphoreType.DMA((n * (NDEV - 1),)), pltpu.SemaphoreType.DMA((n * (NDEV - 1),)),
                        pltpu.SemaphoreType.DMA((n,))],
    )(*arrays)


def _acc_steps(p, acc, k, nk, finish):
    if nk == 1:
        finish(p)
        return

    @pl.when(k == 0)
    def _():
        acc[...] = p

    @pl.when(k > 0)
    def _():
        acc[...] += p

    @pl.when(k == nk - 1)
    def _():
        finish(acc[...])


def mm_nn(a, w3, name, out_dtype=F32, epi=None, extras=()):
    M, K = a.shape
    J, _, n = w3.shape
    tm, tn, tk = min(1024, M), min(1024, n), min(2048, K)
    q, nk, ne = n // tn, K // tk, len(extras)

    def body(*refs):
        a_ref, w_ref = refs[:2]
        ex, o_ref, acc = refs[2:2 + ne], refs[2 + ne], refs[-1]
        p = jnp.dot(a_ref[...], w_ref[...], preferred_element_type=F32)

        def finish(r):
            if epi is not None:
                r = epi(r, *[e[...] for e in ex])
            o_ref[...] = r.astype(out_dtype)

        _acc_steps(p, acc, pl.program_id(2), nk, finish)

    return pl.pallas_call(
        body, name=name, grid=(M // tm, J * q, nk),
        in_specs=[pl.BlockSpec((tm, tk), lambda i, j, k: (i, k)),
                  pl.BlockSpec((None, tk, tn), lambda i, j, k: (j // q, k, j % q))]
        + [pl.BlockSpec((tm, tn), lambda i, j, k: (i, j))] * ne,
        out_specs=pl.BlockSpec((tm, tn), lambda i, j, k: (i, j)),
        out_shape=jax.ShapeDtypeStruct((M, J * n), out_dtype),
        scratch_shapes=[pltpu.VMEM((tm, tn), F32)],
        compiler_params=_params(("parallel", "parallel", "arbitrary")),
    )(a, w3, *extras)


def mm_nt(dy, w3, name, out_dtype=F32, epi=None, extras=()):
    M, _ = dy.shape
    J, K, n = w3.shape
    tm, tn, tkk = min(1024, M), min(1024, n), min(1024, K)
    q, ne = n // tn, len(extras)
    nk = J * q

    def body(*refs):
        d_ref, w_ref = refs[:2]
        ex, o_ref, acc = refs[2:2 + ne], refs[2 + ne], refs[-1]
        p = lax.dot_general(d_ref[...], w_ref[...], (((1,), (1,)), ((), ())), preferred_element_type=F32)

        def finish(r):
            if epi is not None:
                r = epi(r, *[e[...] for e in ex])
            o_ref[...] = r.astype(out_dtype)

        _acc_steps(p, acc, pl.program_id(2), nk, finish)

    return pl.pallas_call(
        body, name=name, grid=(M // tm, K // tkk, nk),
        in_specs=[pl.BlockSpec((tm, tn), lambda i, kk, c: (i, c)),
                  pl.BlockSpec((None, tkk, tn), lambda i, kk, c: (c // q, kk, c % q))]
        + [pl.BlockSpec((tm, tkk), lambda i, kk, c: (i, kk))] * ne,
        out_specs=pl.BlockSpec((tm, tkk), lambda i, kk, c: (i, kk)),
        out_shape=jax.ShapeDtypeStruct((M, K), out_dtype),
        scratch_shapes=[pltpu.VMEM((tm, tkk), F32)],
        compiler_params=_params(("parallel", "parallel", "arbitrary")),
    )(dy, w3, *extras)


def mm_tn(a, dy, J, name, out_dtype=F32):
    M, K = a.shape
    n = dy.shape[1] // J
    tm, tn, tkk = min(1024, M), min(1024, n), min(1024, K)
    q, nk = n // tn, M // tm

    def body(a_ref, d_ref, o_ref, acc):
        p = lax.dot_general(a_ref[...], d_ref[...], (((0,), (0,)), ((), ())), preferred_element_type=F32)

        def finish(r):
            o_ref[...] = r.astype(out_dtype)

        _acc_steps(p, acc, pl.program_id(2), nk, finish)

    return pl.pallas_call(
        body, name=name, grid=(K // tkk, J * q, nk),
        in_specs=[pl.BlockSpec((tm, tkk), lambda kk, c, m: (m, kk)),
                  pl.BlockSpec((tm, tn), lambda kk, c, m: (m, c))],
        out_specs=pl.BlockSpec((None, tkk, tn), lambda kk, c, m: (c // q, kk, c % q)),
        out_shape=jax.ShapeDtypeStruct((J, K, n), out_dtype),
        scratch_shapes=[pltpu.VMEM((tkk, tn), F32)],
        compiler_params=_params(("parallel", "parallel", "arbitrary")),
    )(a, dy)


def _tm(L):
    return min(256, L)


def _row(w, cb=0, tm=None):
    return pl.BlockSpec((tm, w), lambda i: (i, cb))


def _vec(w, cb=0):
    return pl.BlockSpec((1, w), lambda i: (0, cb))


def _accum(ref, val, i):
    @pl.when(i == 0)
    def _():
        ref[...] = val

    @pl.when(i > 0)
    def _():
        ref[...] += val


def _colsum(v):
    return jnp.sum(v, axis=0, keepdims=True)


def _rms(v):
    return lax.rsqrt(jnp.mean(v * v, axis=-1, keepdims=True) + EPS)


def adaln_mod(c_all, w_ada, b_cols):
    B, D = c_all.shape
    n = w_ada.shape[1]
    tn = 512

    def body(c_ref, w_ref, b_ref, o_ref, ca_ref):
        cv = c_ref[...]
        ca = cv * _sigmoid(cv)
        ca_ref[...] = ca
        o_ref[...] = jnp.dot(ca.astype(BF16), w_ref[...].astype(BF16), preferred_element_type=F32) + b_ref[...]

    return pl.pallas_call(
        body, name="adaln_mod", grid=(n // tn,),
        in_specs=[pl.BlockSpec((B, D), lambda j: (0, 0)), pl.BlockSpec((D, tn), lambda j: (0, j)),
                  pl.BlockSpec((1, tn), lambda j: (0, j))],
        out_specs=(pl.BlockSpec((B, tn), lambda j: (0, j)), pl.BlockSpec((B, D), lambda j: (0, 0))),
        out_shape=(jax.ShapeDtypeStruct((B, n), F32), jax.ShapeDtypeStruct((B, D), F32)),
        compiler_params=_params(("arbitrary",)),
    )(c_all, w_ada, b_cols)


def prenorm(x, g, scale, shift, name):
    L, D = x.shape
    tm = _tm(L)

    def body(x_ref, g_ref, sc_ref, sh_ref, u_ref):
        xv = x_ref[...]
        u_ref[...] = (xv * _rms(xv) * g_ref[...] * (1.0 + sc_ref[...]) + sh_ref[...]).astype(BF16)

    return pl.pallas_call(
        body, name=name, grid=(L // tm,),
        in_specs=[_row(D, tm=tm), _vec(D), _vec(D), _vec(D)],
        out_specs=_row(D, tm=tm), out_shape=jax.ShapeDtypeStruct((L, D), BF16),
        compiler_params=_params(("parallel",)),
    )(x, g, scale, shift)


def conv_fwd(proj, w_dw, b_dw, ln_g, ln_b):
    L = proj.shape[0]
    C = w_dw.shape[1]
    tm = _tm(L)
    hb = tm // HALO

    def body(a_ref, g_ref, ah_ref, gh_ref, w_ref, b_ref, lg_ref, lb_ref, vs_ref, vc_ref, buf):
        i = pl.program_id(0)
        halo = ah_ref[...] * _sigmoid(gh_ref[...])
        buf[0:HALO, :] = halo * jnp.where(i > 0, 1.0, 0.0)
        buf[HALO:HALO + tm, :] = a_ref[...] * _sigmoid(g_ref[...])
        acc = jnp.zeros((tm, C), F32) + b_ref[...]
        for k in range(CONV_K):
            acc = acc + w_ref[k:k + 1, :] * buf[pl.ds(HALO - (CONV_K - 1) + k, tm), :]
        vc_ref[...] = acc
        mu = jnp.mean(acc, axis=-1, keepdims=True)
        d = acc - mu
        var = jnp.mean(d * d, axis=-1, keepdims=True)
        ln = d * lax.rsqrt(var + EPS) * lg_ref[...] + lb_ref[...]
        vs_ref[...] = (ln * _sigmoid(ln)).astype(BF16)

    prev = lambda cb: pl.BlockSpec((HALO, C), lambda i: (jnp.maximum(i * hb - 1, 0), cb))
    return pl.pallas_call(
        body, name="conv_fwd", grid=(L // tm,),
        in_specs=[_row(C, 0, tm), _row(C, 1, tm), prev(0), prev(1),
                  pl.BlockSpec((HALO, C), lambda i: (0, 0)), _vec(C), _vec(C), _vec(C)],
        out_specs=(_row(C, tm=tm), _row(C, tm=tm)),
        out_shape=(jax.ShapeDtypeStruct((L, C), BF16), jax.ShapeDtypeStruct((L, C), F32)),
        scratch_shapes=[pltpu.VMEM((HALO + tm, C), F32)],
        compiler_params=_params(("parallel",)),
    )(proj, proj, proj, proj, w_dw, b_dw, ln_g, ln_b)


def _gelu(v):
    return 0.5 * v * (1.0 + jnp.tanh(math.sqrt(2.0 / math.pi) * (v + 0.044715 * v * v * v)))


def _gelu_grad(v):
    k = math.sqrt(2.0 / math.pi)
    t = jnp.tanh(k * (v + 0.044715 * v * v * v))
    return 0.5 * (1.0 + t) + 0.5 * v * (1.0 - t * t) * k * (1.0 + 3.0 * 0.044715 * v * v)


def s5_param_fn(ar, ai, ldt, br, bi, expand):
    dt = jnp.exp(ldt)
    er = jnp.exp(ar * dt)
    th = ai * dt
    lbr, lbi = er * jnp.cos(th), er * jnp.sin(th)
    nr, ni = lbr - 1.0, lbi
    den = ar * ar + ai * ai
    qr, qi = (nr * ar + ni * ai) / den, (ni * ar - nr * ai) / den
    qre = jnp.dot(expand, qr, precision=HI, preferred_element_type=F32)
    qie = jnp.dot(expand, qi, precision=HI, preferred_element_type=F32)
    return lbr, lbi, qre * br - qie * bi, qre * bi + qie * br


def s5_params(ar, ai, ldt, br2, bi2, expand):
    def body(ar_ref, ai_ref, ld_ref, br_ref, bi_ref, e_ref, o1, o2, o3, o4):
        r = s5_param_fn(ar_ref[...], ai_ref[...], ld_ref[...], br_ref[...], bi_ref[...], e_ref[...])
        o1[...], o2[...], o3[...], o4[...] = r

    s2, s3 = jax.ShapeDtypeStruct(ar.shape, F32), jax.ShapeDtypeStruct(br2.shape, F32)
    return pl.pallas_call(body, name="s5_params", out_shape=(s2, s2, s3, s3), compiler_params=_params())(
        ar, ai, ldt, br2, bi2, expand)


def s5_params_bwd(ar, ai, ldt, br2, bi2, expand, dlr, dli, dbr, dbi):
    def body(ar_ref, ai_ref, ld_ref, br_ref, bi_ref, e_ref, c1, c2, c3, c4, o1, o2, o3, o4, o5):
        e = e_ref[...]
        fn = lambda a, b, c, d, f: s5_param_fn(a, b, c, d, f, e)
        _, vjp = jax.vjp(fn, ar_ref[...], ai_ref[...], ld_ref[...], br_ref[...], bi_ref[...])
        r = vjp((c1[...], c2[...], c3[...], c4[...]))
        o1[...], o2[...], o3[...], o4[...], o5[...] = r

    shapes = tuple(jax.ShapeDtypeStruct(v.shape, F32) for v in (ar, ai, ldt, br2, bi2))
    return pl.pallas_call(body, name="s5_params_bwd", out_shape=shapes, compiler_params=_params())(
        ar, ai, ldt, br2, bi2, expand, dlr, dli, dbr, dbi)


def s5_tables(lr, li):
    C = lr.shape[1]

    def body(lr_ref, li_ref, o_ref):
        row = lax.broadcasted_iota(jnp.int32, (8, C), 0)
        for rev in (0, 1):
            pr = jnp.broadcast_to(lr_ref[...], (8, C))
            pi = jnp.broadcast_to(-li_ref[...] if rev else li_ref[...], (8, C))
            br, bi = pr, pi
            pows = [(pr, pi)]
            for _ in range(7):
                pr, pi = pr * br - pi * bi, pr * bi + pi * br
                pows.append((pr, pi))
            base = 8 * rev
            for s, d in enumerate((1, 2, 4)):
                keep = (row + d <= 7) if rev else (row >= d)
                o_ref[base + 2 * s] = jnp.where(keep, pows[d - 1][0], 0.0)
                o_ref[base + 2 * s + 1] = jnp.where(keep, pows[d - 1][1], 0.0)
            cr, ci = jnp.zeros((8, C), F32), jnp.zeros((8, C), F32)
            for j in range(8):
                e = (8 - j) if rev else (j + 1)
                cr = jnp.where(row == j, pows[e - 1][0], cr)
                ci = jnp.where(row == j, pows[e - 1][1], ci)
            o_ref[base + 6] = cr
            o_ref[base + 7] = ci

    return pl.pallas_call(body, name="s5_tables", out_shape=jax.ShapeDtypeStruct((16, 8, C), F32),
                          compiler_params=_params())(lr, li)


def _scan_tile(xr, xi, tabs, cr, ci, rev):
    for s, d in enumerate((1, 2, 4)):
        tr, ti = tabs[2 * s], tabs[2 * s + 1]
        sh = (8 - d) if rev else d
        sr, si = pltpu.roll(xr, sh, 0), pltpu.roll(xi, sh, 0)
        xr, xi = xr + tr * sr - ti * si, xi + tr * si + ti * sr
    tr, ti = tabs[6], tabs[7]
    xr, xi = xr + tr * cr - ti * ci, xi + tr * ci + ti * cr
    return xr, xi


def s5_fwd(proj, col0, bdr, bdi, cdr, cdi, tabs, d_skip):
    L = proj.shape[0]
    nb, cw, sw = bdr.shape
    tl = min(256, L)
    cb0 = col0 // cw

    def body(u_ref, bdr_ref, bdi_ref, cdr_ref, cdi_ref, t_ref, dk_ref, sr_ref, si_ref, yp_ref, yg_ref, car):
        l = pl.program_id(1)

        @pl.when(l == 0)
        def _():
            car[...] = jnp.zeros_like(car)

        u = u_ref[...]
        sr_ref[...] = jnp.dot(u, bdr_ref[...], precision=HI, preferred_element_type=F32)
        si_ref[...] = jnp.dot(u, bdi_ref[...], precision=HI, preferred_element_type=F32)
        def tile(i, c):
            tabs = [t_ref[j] for j in range(8)]
            r0 = pl.multiple_of(i * 8, 8)
            xr, xi = _scan_tile(sr_ref[pl.ds(r0, 8), :], si_ref[pl.ds(r0, 8), :], tabs, c[0], c[1], False)
            sr_ref[pl.ds(r0, 8), :] = xr
            si_ref[pl.ds(r0, 8), :] = xi
            return xr[7:8, :], xi[7:8, :]

        c = lax.fori_loop(0, tl // 8, tile, (car[0:1, :], car[1:2, :]))
        car[0:1, :] = c[0]
        car[1:2, :] = c[1]
        y = (jnp.dot(sr_ref[...], cdr_ref[...], precision=HI, preferred_element_type=F32)
             - jnp.dot(si_ref[...], cdi_ref[...], precision=HI, preferred_element_type=F32)
             + dk_ref[...] * u)
        yp_ref[...] = y
        yg_ref[...] = _gelu(y).astype(BF16)

    blk = lambda r, c: pl.BlockSpec((None, r, c), lambda b, l: (b, 0, 0))
    return pl.pallas_call(
        body, name="s5_fwd", grid=(nb, L // tl),
        in_specs=[pl.BlockSpec((tl, cw), lambda b, l: (l, cb0 + b)), blk(cw, sw), blk(cw, sw), blk(sw, cw), blk(sw, cw),
                  pl.BlockSpec((8, 8, sw), lambda b, l: (0, 0, b)), pl.BlockSpec((1, cw), lambda b, l: (0, b))],
        out_specs=(pl.BlockSpec((tl, sw), lambda b, l: (l, b)), pl.BlockSpec((tl, sw), lambda b, l: (l, b)),
                   pl.BlockSpec((tl, cw), lambda b, l: (l, b)), pl.BlockSpec((tl, cw), lambda b, l: (l, b))),
        out_shape=(jax.ShapeDtypeStruct((L, nb * sw), F32), jax.ShapeDtypeStruct((L, nb * sw), F32),
                   jax.ShapeDtypeStruct((L, nb * cw), F32), jax.ShapeDtypeStruct((L, nb * cw), BF16)),
        scratch_shapes=[pltpu.VMEM((8, sw), F32)],
        compiler_params=_params(("parallel", "arbitrary")),
    )(proj, bdr, bdi, cdr, cdi, tabs, d_skip)


def s5_bwd(dyg_a, dyg_b, yp, proj, col0, s_re, s_im, bdrt, bdit, cdrt, cdit, tabs, d_skip):
    L = proj.shape[0]
    nb, sw, cw = bdrt.shape
    tl = min(256, L)
    nl = L // tl
    cb0 = col0 // cw
    tb = tl // 8

    def body(da_ref, db_ref, yp_ref, u_ref, sr_ref, si_ref, hr_ref, hi_ref, bdrt_ref, bdit_ref, cdrt_ref, cdit_ref,
             t_ref, dk_ref, du_ref, ddk_ref, dcr_ref, dci_ref, dbr_ref, dbi_ref, dlr_ref, dli_ref,
             gr, gi, pr, pi, car):
        l = pl.program_id(1)
        first = l == nl - 1

        @pl.when(l == 0)
        def _():
            car[...] = jnp.zeros_like(car)

        u = u_ref[...]
        dy = (da_ref[...] + db_ref[...]) * _gelu_grad(yp_ref[...])
        gr[...] = jnp.dot(dy, cdrt_ref[...], precision=HI, preferred_element_type=F32)
        gi[...] = -jnp.dot(dy, cdit_ref[...], precision=HI, preferred_element_type=F32)
        inner = jnp.where(first, 0.0, 1.0)
        pr[0:8, :] = hr_ref[...] * inner
        pi[0:8, :] = hi_ref[...] * inner
        pr[8:8 + tl, :] = sr_ref[...]
        pi[8:8 + tl, :] = si_ref[...]
        row = lax.broadcasted_iota(jnp.int32, (8, sw), 0)

        def tile(j, c):
            tabs = [t_ref[8 + k] for k in range(8)]
            r0 = pl.multiple_of((tb - 1 - j) * 8, 8)
            xr, xi = _scan_tile(gr[pl.ds(r0, 8), :], gi[pl.ds(r0, 8), :], tabs, c[0], c[1], True)
            gr[pl.ds(r0, 8), :] = xr
            gi[pl.ds(r0, 8), :] = xi
            qr = jnp.where(row == 0, pltpu.roll(pr[pl.ds(r0, 8), :], 1, 0), pltpu.roll(pr[pl.ds(r0 + 8, 8), :], 1, 0))
            qi = jnp.where(row == 0, pltpu.roll(pi[pl.ds(r0, 8), :], 1, 0), pltpu.roll(pi[pl.ds(r0 + 8, 8), :], 1, 0))
            return xr[0:1, :], xi[0:1, :], c[2] + xr * qr + xi * qi, c[3] + xi * qr - xr * qi

        z = jnp.zeros((8, sw), F32)
        c = lax.fori_loop(0, tb, tile, (car[0:1, :], car[1:2, :], z, z))
        car[0:1, :] = c[0]
        car[1:2, :] = c[1]
        g_re, g_im = gr[...], gi[...]
        du_ref[...] = (dy * dk_ref[...] + jnp.dot(g_re, bdrt_ref[...], precision=HI, preferred_element_type=F32)
                       + jnp.dot(g_im, bdit_ref[...], precision=HI, preferred_element_type=F32)).astype(BF16)
        tn = (((0,), (0,)), ((), ()))
        _accum(ddk_ref, _colsum(dy * u), l)
        _accum(dcr_ref, lax.dot_general(dy, sr_ref[...], tn, precision=HI, preferred_element_type=F32), l)
        _accum(dci_ref, -lax.dot_general(dy, si_ref[...], tn, precision=HI, preferred_element_type=F32), l)
        _accum(dbr_ref, lax.dot_general(u, g_re, tn, precision=HI, preferred_element_type=F32), l)
        _accum(dbi_ref, lax.dot_general(u, g_im, tn, precision=HI, preferred_element_type=F32), l)
        _accum(dlr_ref, c[2], l)
        _accum(dli_ref, c[3], l)

    rl = lambda l: nl - 1 - l
    cblk = lambda w, off=0: pl.BlockSpec((tl, w), lambda b, l: (rl(l), off + b))
    halo = pl.BlockSpec((8, sw), lambda b, l: (jnp.maximum(rl(l) * tb - 1, 0), b))
    mat = lambda r, c: pl.BlockSpec((None, r, c), lambda b, l: (b, 0, 0))
    return pl.pallas_call(
        body, name="s5_bwd", grid=(nb, nl),
        in_specs=[cblk(cw), cblk(cw), cblk(cw), cblk(cw, cb0), cblk(sw), cblk(sw), halo, halo,
                  mat(sw, cw), mat(sw, cw), mat(cw, sw), mat(cw, sw),
                  pl.BlockSpec((16, 8, sw), lambda b, l: (0, 0, b)), pl.BlockSpec((1, cw), lambda b, l: (0, b))],
        out_specs=(cblk(cw), pl.BlockSpec((1, cw), lambda b, l: (0, b)), mat(cw, sw), mat(cw, sw), mat(cw, sw), mat(cw, sw),
                   pl.BlockSpec((8, sw), lambda b, l: (0, b)), pl.BlockSpec((8, sw), lambda b, l: (0, b))),
        out_shape=(jax.ShapeDtypeStruct((L, nb * cw), BF16), jax.ShapeDtypeStruct((1, nb * cw), F32),
                   jax.ShapeDtypeStruct((nb, cw, sw), F32), jax.ShapeDtypeStruct((nb, cw, sw), F32),
                   jax.ShapeDtypeStruct((nb, cw, sw), F32), jax.ShapeDtypeStruct((nb, cw, sw), F32),
                   jax.ShapeDtypeStruct((8, nb * sw), F32), jax.ShapeDtypeStruct((8, nb * sw), F32)),
        scratch_shapes=[pltpu.VMEM((tl, sw), F32), pltpu.VMEM((tl, sw), F32),
                        pltpu.VMEM((tl + 8, sw), F32), pltpu.VMEM((tl + 8, sw), F32), pltpu.VMEM((8, sw), F32)],
        compiler_params=_params(("parallel", "arbitrary")),
    )(dyg_a, dyg_b, yp, proj, s_re, s_im, s_re, s_im, bdrt, bdit, cdrt, cdit, tabs, d_skip)


def merge_fwd(proj, col_gc, y_conv, ga, gb):
    L, D = y_conv.shape
    tm = _tm(L)
    h = D // 2
    c0 = col_gc // h

    def body(p0, p1, p2, p3, yc_ref, ga_ref, gb_ref, o_ref):
        gc, gs = (p0, p1), (p2, p3)
        for s in range(2):
            cols = slice(s * h, (s + 1) * h)
            y_ssm = ga_ref[:, cols] * _sigmoid(gb_ref[:, cols])
            o_ref[:, cols] = (_sigmoid(gc[s][...]) * yc_ref[:, cols] + _sigmoid(gs[s][...]) * y_ssm).astype(BF16)

    return pl.pallas_call(
        body, name="merge_fwd", grid=(L // tm,),
        in_specs=[_row(h, c0 + s, tm) for s in range(4)] + [_row(D, tm=tm)] * 3,
        out_specs=_row(D, tm=tm), out_shape=jax.ShapeDtypeStruct((L, D), BF16),
        compiler_params=_params(("parallel",)),
    )(proj, proj, proj, proj, y_conv, ga, gb)


def residual_norm(x, m_out, gate, g, scale, shift):
    L, D = x.shape
    tm = _tm(L)

    def body(x_ref, m_ref, gt_ref, g_ref, sc_ref, sh_ref, h_ref, z_ref):
        h = x_ref[...] + gt_ref[...] * m_ref[...]
        h_ref[...] = h
        z_ref[...] = (h * _rms(h) * g_ref[...] * (1.0 + sc_ref[...]) + sh_ref[...]).astype(BF16)

    return pl.pallas_call(
        body, name="residual_norm", grid=(L // tm,),
        in_specs=[_row(D, tm=tm), _row(D, tm=tm), _vec(D), _vec(D), _vec(D), _vec(D)],
        out_specs=(_row(D, tm=tm), _row(D, tm=tm)),
        out_shape=(jax.ShapeDtypeStruct((L, D), F32), jax.ShapeDtypeStruct((L, D), BF16)),
        compiler_params=_params(("parallel",)),
    )(x, m_out, gate, g, scale, shift)


def loss_bwd(h1, ff, gate2, final_g, target):
    L, D = h1.shape
    tm = _tm(L)

    def body(h_ref, f_ref, gt_ref, g_ref, t_ref, dh_ref, dff_ref, loss_ref, dg_ref, dgt_ref):
        i = pl.program_id(0)
        ffv = f_ref[...]
        h2 = h_ref[...] + gt_ref[...] * ffv
        r = _rms(h2)
        n = h2 * r
        err = n * g_ref[...] - t_ref[...]
        per_tok = jnp.mean(err * err, axis=-1, keepdims=True)
        _accum(loss_ref, 0.5 * jnp.sum(per_tok, axis=0, keepdims=True), i)
        dy = err * (1.0 / D)
        _accum(dg_ref, _colsum(dy * n), i)
        dn = dy * g_ref[...]
        dh2 = r * (dn - n * jnp.mean(dn * n, axis=-1, keepdims=True))
        dh_ref[...] = dh2
        dff_ref[...] = (gt_ref[...] * dh2).astype(BF16)
        _accum(dgt_ref, _colsum(dh2 * ffv), i)

    return pl.pallas_call(
        body, name="loss_bwd", grid=(L // tm,),
        in_specs=[_row(D, tm=tm), _row(D, tm=tm), _vec(D), _vec(D), _row(D, tm=tm)],
        out_specs=(_row(D, tm=tm), _row(D, tm=tm), pl.BlockSpec((1, 1), lambda i: (0, 0)), _vec(D), _vec(D)),
        out_shape=(jax.ShapeDtypeStruct((L, D), F32), jax.ShapeDtypeStruct((L, D), BF16),
                   jax.ShapeDtypeStruct((1, 1), F32), jax.ShapeDtypeStruct((1, D), F32), jax.ShapeDtypeStruct((1, D), F32)),
        compiler_params=_params(("arbitrary",)),
    )(h1, ff, gate2, final_g, target)


def norm_bwd(dz, h, dh_in, g, scale, name, gate=None, m_out=None):
    L, D = h.shape
    tm = _tm(L)
    tail = gate is not None

    def body(*refs):
        dz_ref, h_ref, di_ref, g_ref, sc_ref = refs[:5]
        rest = refs[5:]
        if tail:
            gt_ref, m_ref = rest[:2]
            rest = rest[2:]
        dh_ref, dsc_ref, dsh_ref, dg_ref = rest[:4]
        i = pl.program_id(0)
        hv, dzv = h_ref[...], dz_ref[...]
        r = _rms(hv)
        n = hv * r
        _accum(dsc_ref, _colsum(dzv * n * g_ref[...]), i)
        _accum(dsh_ref, _colsum(dzv), i)
        dzn = dzv * (1.0 + sc_ref[...])
        _accum(dg_ref, _colsum(dzn * n), i)
        dn = dzn * g_ref[...]
        dh = di_ref[...] + r * (dn - n * jnp.mean(dn * n, axis=-1, keepdims=True))
        dh_ref[...] = dh
        if tail:
            dmo_ref, dgt_ref = rest[4:]
            dmo_ref[...] = (gt_ref[...] * dh).astype(BF16)
            _accum(dgt_ref, _colsum(dh * m_ref[...]), i)

    ins = [dz, h, dh_in, g, scale]
    in_specs = [_row(D, tm=tm)] * 3 + [_vec(D)] * 2
    out_specs = [_row(D, tm=tm), _vec(D), _vec(D), _vec(D)]
    out_shape = [jax.ShapeDtypeStruct((L, D), F32)] + [jax.ShapeDtypeStruct((1, D), F32)] * 3
    if tail:
        ins += [gate, m_out]
        in_specs += [_vec(D), _row(D, tm=tm)]
        out_specs += [_row(D, tm=tm), _vec(D)]
        out_shape += [jax.ShapeDtypeStruct((L, D), BF16), jax.ShapeDtypeStruct((1, D), F32)]
    return pl.pallas_call(
        body, name=name, grid=(L // tm,), in_specs=in_specs, out_specs=tuple(out_specs), out_shape=tuple(out_shape),
        compiler_params=_params(("arbitrary",)),
    )(*ins)


def merge_bwd(dmerged, proj, col_gc, y_conv, ga, gb):
    L, D = y_conv.shape
    tm = _tm(L)
    h = D // 2
    c0 = col_gc // h

    def body(dm_ref, p0, p1, p2, p3, yc_ref, ga_ref, gb_ref, dyc_ref, dga_ref, dgb_ref, dg_ref):
        gc, gs = (p0, p1), (p2, p3)
        for s in range(2):
            cols = slice(s * h, (s + 1) * h)
            dm = dm_ref[:, cols]
            sc, ss, sb = _sigmoid(gc[s][...]), _sigmoid(gs[s][...]), _sigmoid(gb_ref[:, cols])
            gav = ga_ref[:, cols]
            dyc_ref[:, cols] = (dm * sc).astype(BF16)
            dg_ref[:, cols] = (dm * yc_ref[:, cols] * sc * (1.0 - sc)).astype(BF16)
            dg_ref[:, D + s * h:D + (s + 1) * h] = (dm * gav * sb * ss * (1.0 - ss)).astype(BF16)
            dys = dm * ss
            dga_ref[:, cols] = (dys * sb).astype(BF16)
            dgb_ref[:, cols] = (dys * gav * sb * (1.0 - sb)).astype(BF16)

    return pl.pallas_call(
        body, name="merge_bwd", grid=(L // tm,),
        in_specs=[_row(D, tm=tm)] + [_row(h, c0 + s, tm) for s in range(4)] + [_row(D, tm=tm)] * 3,
        out_specs=(_row(D, tm=tm), _row(D, tm=tm), _row(D, tm=tm), _row(2 * D, tm=tm)),
        out_shape=(jax.ShapeDtypeStruct((L, D), BF16),) * 3 + (jax.ShapeDtypeStruct((L, 2 * D), BF16),),
        compiler_params=_params(("parallel",)),
    )(dmerged, proj, proj, proj, proj, y_conv, ga, gb)


def conv_ln_bwd(dvs, vc, ln_g, ln_b):
    L, C = vc.shape
    tm = _tm(L)

    def body(d_ref, v_ref, g_ref, b_ref, o_ref, dg_ref, db_ref):
        i = pl.program_id(0)
        v = v_ref[...]
        mu = jnp.mean(v, axis=-1, keepdims=True)
        d = v - mu
        rstd = lax.rsqrt(jnp.mean(d * d, axis=-1, keepdims=True) + EPS)
        xh = d * rstd
        ln = xh * g_ref[...] + b_ref[...]
        sg = _sigmoid(ln)
        dln = d_ref[...] * sg * (1.0 + ln * (1.0 - sg))
        _accum(dg_ref, _colsum(dln * xh), i)
        _accum(db_ref, _colsum(dln), i)
        dxh = dln * g_ref[...]
        o_ref[...] = rstd * (dxh - jnp.mean(dxh, axis=-1, keepdims=True)
                             - xh * jnp.mean(dxh * xh, axis=-1, keepdims=True))

    return pl.pallas_call(
        body, name="conv_ln_bwd", grid=(L // tm,),
        in_specs=[_row(C, tm=tm), _row(C, tm=tm), _vec(C), _vec(C)],
        out_specs=(_row(C, tm=tm), _vec(C), _vec(C)),
        out_shape=(jax.ShapeDtypeStruct((L, C), F32), jax.ShapeDtypeStruct((1, C), F32), jax.ShapeDtypeStruct((1, C), F32)),
        compiler_params=_params(("arbitrary",)),
    )(dvs, vc, ln_g, ln_b)


def conv_bwd(dvc, proj, w_dw):
    L, C = dvc.shape
    tm = _tm(L)
    hb = tm // HALO
    last = L // HALO - 1
    nt = L // tm

    def body(d_ref, dn_ref, a_ref, g_ref, ah_ref, gh_ref, w_ref, o_ref, dw_ref, db_ref, dbuf, vbuf):
        i = pl.program_id(0)
        dcur = d_ref[...]
        dbuf[0:tm, :] = dcur
        dbuf[tm:tm + HALO, :] = dn_ref[...] * jnp.where(i < nt - 1, 1.0, 0.0)
        av, sg = a_ref[...], _sigmoid(g_ref[...])
        vbuf[0:HALO, :] = ah_ref[...] * _sigmoid(gh_ref[...]) * jnp.where(i > 0, 1.0, 0.0)
        vbuf[HALO:HALO + tm, :] = av * sg
        dv = jnp.zeros((tm, C), F32)
        for k in range(CONV_K):
            dv = dv + w_ref[k:k + 1, :] * dbuf[pl.ds(CONV_K - 1 - k, tm), :]
        o_ref[:, 0:C] = (dv * sg).astype(BF16)
        o_ref[:, C:2 * C] = (dv * av * sg * (1.0 - sg)).astype(BF16)

        @pl.when(i == 0)
        def _():
            dw_ref[...] = jnp.zeros_like(dw_ref)

        for k in range(CONV_K):
            dw_ref[k:k + 1, :] += _colsum(dcur * vbuf[pl.ds(HALO - (CONV_K - 1) + k, tm), :])
        _accum(db_ref, _colsum(dcur), i)

    prev = lambda cb: pl.BlockSpec((HALO, C), lambda i: (jnp.maximum(i * hb - 1, 0), cb))
    return pl.pallas_call(
        body, name="conv_bwd", grid=(nt,),
        in_specs=[_row(C, tm=tm), pl.BlockSpec((HALO, C), lambda i: (jnp.minimum((i + 1) * hb, last), 0)),
                  _row(C, 0, tm), _row(C, 1, tm), prev(0), prev(1), pl.BlockSpec((HALO, C), lambda i: (0, 0))],
        out_specs=(_row(2 * C, tm=tm), pl.BlockSpec((HALO, C), lambda i: (0, 0)), _vec(C)),
        out_shape=(jax.ShapeDtypeStruct((L, 2 * C), BF16), jax.ShapeDtypeStruct((HALO, C), F32),
                   jax.ShapeDtypeStruct((1, C), F32)),
        scratch_shapes=[pltpu.VMEM((tm + HALO, C), F32), pltpu.VMEM((HALO + tm, C), F32)],
        compiler_params=_params(("arbitrary",)),
    )(dvc, dvc, proj, proj, proj, proj, w_dw)


def _adamw(w, g, m, v):
    m = ADAM_B1 * m + (1.0 - ADAM_B1) * g
    v = ADAM_B2 * v + (1.0 - ADAM_B2) * (g * g)
    m_hat = m / (1.0 - ADAM_B1 ** ADAM_STEP)
    v_hat = v / (1.0 - ADAM_B2 ** ADAM_STEP)
    delta = -ADAM_LR * (m_hat / (jnp.sqrt(v_hat) + ADAM_EPS) + ADAM_WD * w)
    return delta, m, v


def _tile_rows(R, C):
    tr = 8
    while tr * 2 * C <= 128 * 1024 and R % (tr * 2) == 0:
        tr *= 2
    assert R % tr == 0, (R, C)
    return tr


def sum_devices(parts, name):
    _, R, C = parts.shape
    tr = _tile_rows(R, C)

    def body(p_ref, o_ref):
        s = p_ref[0]
        for j in range(1, NDEV):
            s = s + p_ref[j]
        o_ref[...] = s

    return pl.pallas_call(
        body, name=name, grid=(R // tr,),
        in_specs=[pl.BlockSpec((NDEV, tr, C), lambda i: (0, i, 0))],
        out_specs=pl.BlockSpec((tr, C), lambda i: (i, 0)), out_shape=jax.ShapeDtypeStruct((R, C), F32),
        compiler_params=_params(("parallel",)),
    )(parts)


def adam_update(w, g, m, v, name):
    R, C = w.shape
    tr = _tile_rows(R, C)

    def body(w_ref, g_ref, m_ref, v_ref, d_ref, mo_ref, vo_ref):
        d, mm, vv = _adamw(w_ref[...], g_ref[...], m_ref[...], v_ref[...])
        d_ref[...], mo_ref[...], vo_ref[...] = d, mm, vv

    spec = pl.BlockSpec((tr, C), lambda i: (i, 0))
    return pl.pallas_call(
        body, name=name, grid=(R // tr,), in_specs=[spec] * 4, out_specs=(spec,) * 3,
        out_shape=(jax.ShapeDtypeStruct((R, C), F32),) * 3, compiler_params=_params(("parallel",)),
    )(w, g, m, v)


def adam_reduce(parts, w, m, v, name):
    R, C = w.shape
    tr = _tile_rows(R, C)

    def body(p_ref, w_ref, m_ref, v_ref, g_ref, d_ref, mo_ref, vo_ref):
        g = p_ref[0].astype(F32)
        for j in range(1, NDEV):
            g = g + p_ref[j].astype(F32)
        g_ref[...] = g
        d, mm, vv = _adamw(w_ref[...], g, m_ref[...], v_ref[...])
        d_ref[...], mo_ref[...], vo_ref[...] = d, mm, vv

    spec = pl.BlockSpec((tr, C), lambda i: (i, 0))
    return pl.pallas_call(
        body, name=name, grid=(R // tr,),
        in_specs=[pl.BlockSpec((NDEV, tr, C), lambda i: (0, i, 0)), spec, spec, spec], out_specs=(spec,) * 4,
        out_shape=(jax.ShapeDtypeStruct((R, C), F32),) * 4, compiler_params=_params(("parallel",)),
    )(parts, w, m, v)


def adam_w_ada(c_act, dmod_cols, w, m, v):
    D, n = w.shape
    tn = 256

    def body(c_ref, dm_ref, w_ref, m_ref, v_ref, g_ref, d_ref, mo_ref, vo_ref):
        g = lax.dot_general(c_ref[...].astype(BF16), dm_ref[...].astype(BF16), (((0,), (0,)), ((), ())),
                            preferred_element_type=F32)
        g_ref[...] = g
        d, mm, vv = _adamw(w_ref[...], g, m_ref[...], v_ref[...])
        d_ref[...], mo_ref[...], vo_ref[...] = d, mm, vv

    spec = pl.BlockSpec((D, tn), lambda j: (0, j))
    return pl.pallas_call(
        body, name="adam_w_ada", grid=(n // tn,),
        in_specs=[pl.BlockSpec((NDEV, D), lambda j: (0, 0)), pl.BlockSpec((NDEV, tn), lambda j: (0, j)), spec, spec, spec],
        out_specs=(spec,) * 4, out_shape=(jax.ShapeDtypeStruct((D, n), F32),) * 4,
        compiler_params=_params(("parallel",)),
    )(c_act, dmod_cols, w, m, v)


def _block_diag(m):
    G, a, b = m.shape
    m4 = m.reshape(G // GB, GB, a, b)
    eye = jnp.eye(GB, dtype=m.dtype)
    return (m4[:, :, :, None, :] * eye[None, :, None, :, None]).reshape(G // GB, GB * a, GB * b)


def _diag_blocks(m, a, b):
    nb = m.shape[0]
    m5 = m.reshape(nb, GB, a, GB, b)
    idx = jnp.arange(GB)
    return m5[:, idx, :, idx, :].transpose(1, 0, 2, 3).reshape(nb * GB, a, b)


def _flat_pad(parts, mult):
    flat = jnp.concatenate([p.reshape(-1) for p in parts])
    pad = (-flat.shape[0]) % mult
    return jnp.pad(flat, (0, pad))


def _split(flat, like):
    out, off = [], 0
    for p in like:
        out.append(flat[off:off + p.size].reshape(p.shape))
        off += p.size
    return out


def kernel(x, c, w_ada, b_ada, norm1_g, w_in, w_dw, b_dw, ln_g, ln_b, w_conv_out, a_re, a_im, log_dt, b_re, b_im, c_re, c_im, d_skip, w_glu_a, w_glu_b, w_out, norm2_g, w_ff1, w_ff2, final_g, loss_target, m_w_ada, m_b_ada, m_norm1_g, m_w_in, m_w_dw, m_b_dw, m_ln_g, m_ln_b, m_w_conv_out, m_a_re, m_a_im, m_log_dt, m_b_re, m_b_im, m_c_re, m_c_im, m_d_skip, m_w_glu_a, m_w_glu_b, m_w_out, m_norm2_g, m_w_ff1, m_w_ff2, m_final_g, v_w_ada, v_b_ada, v_norm1_g, v_w_in, v_w_dw, v_b_dw, v_ln_g, v_ln_b, v_w_conv_out, v_a_re, v_a_im, v_log_dt, v_b_re, v_b_im, v_c_re, v_c_im, v_d_skip, v_w_glu_a, v_w_glu_b, v_w_out, v_norm2_g, v_w_ff1, v_w_ff2, v_final_g):
    W = dict(w_ada=w_ada, b_ada=b_ada, norm1_g=norm1_g, w_in=w_in, w_dw=w_dw, b_dw=b_dw, ln_g=ln_g, ln_b=ln_b,
             w_conv_out=w_conv_out, a_re=a_re, a_im=a_im, log_dt=log_dt, b_re=b_re, b_im=b_im, c_re=c_re, c_im=c_im,
             d_skip=d_skip, w_glu_a=w_glu_a, w_glu_b=w_glu_b, w_out=w_out, norm2_g=norm2_g, w_ff1=w_ff1, w_ff2=w_ff2,
             final_g=final_g)
    Mo = dict(w_ada=m_w_ada, b_ada=m_b_ada, norm1_g=m_norm1_g, w_in=m_w_in, w_dw=m_w_dw, b_dw=m_b_dw, ln_g=m_ln_g,
              ln_b=m_ln_b, w_conv_out=m_w_conv_out, a_re=m_a_re, a_im=m_a_im, log_dt=m_log_dt, b_re=m_b_re, b_im=m_b_im,
              c_re=m_c_re, c_im=m_c_im, d_skip=m_d_skip, w_glu_a=m_w_glu_a, w_glu_b=m_w_glu_b, w_out=m_w_out,
              norm2_g=m_norm2_g, w_ff1=m_w_ff1, w_ff2=m_w_ff2, final_g=m_final_g)
    Vo = dict(w_ada=v_w_ada, b_ada=v_b_ada, norm1_g=v_norm1_g, w_in=v_w_in, w_dw=v_w_dw, b_dw=v_b_dw, ln_g=v_ln_g,
              ln_b=v_ln_b, w_conv_out=v_w_conv_out, a_re=v_a_re, a_im=v_a_im, log_dt=v_log_dt, b_re=v_b_re, b_im=v_b_im,
              c_re=v_c_re, c_im=v_c_im, d_skip=v_d_skip, w_glu_a=v_w_glu_a, w_glu_b=v_w_glu_b, w_out=v_w_out,
              norm2_g=v_norm2_g, w_ff1=v_w_ff1, w_ff2=v_w_ff2, final_g=v_final_g)
    names = list(W)

    me = _me()
    xs, tgt = x[0], loss_target[0]
    L, D = xs.shape
    CW = w_dw.shape[2] * NDEV
    G, P = a_re.shape[1], a_re.shape[2]
    H = b_re.shape[3]
    n_ada = w_ada.shape[2]

    w_dw_pad = jnp.pad(w_dw[0], ((0, HALO - CONV_K), (0, 0)))
    c_all, w_dw_all = _exchange([c, w_dw_pad], "gather_small", True)
    c_all = c_all.reshape(NDEV, D)
    w_dw_full = w_dw_all.transpose(1, 0, 2).reshape(HALO, CW)
    big = ["w_in", "w_conv_out", "w_glu_a", "w_glu_b", "w_out", "w_ff1", "w_ff2"]
    gathered = _exchange([W[k][0].astype(BF16) for k in big], "gather_weights", True)
    wg = dict(zip(big, gathered))
    for k in ("w_out", "w_ff2"):
        s = wg[k].shape
        wg[k] = wg[k].reshape(1, s[0] * s[1], s[2])

    b_cols = lax.dynamic_slice(b_ada, (0, me * n_ada), (1, n_ada))
    mod_cols, c_act = adaln_mod(c_all, w_ada[0], b_cols)
    (mod_all,) = _exchange([mod_cols], "gather_mod", True)
    mod = lax.dynamic_slice(mod_all, (0, me, 0), (NDEV, 1, n_ada)).reshape(6, 1, D)
    shift1, scale1, gate1, shift2, scale2, gate2 = [mod[j] for j in range(6)]

    u = prenorm(xs, norm1_g, scale1, shift1, "prenorm1")
    proj = mm_nn(u, wg["w_in"], "in_proj")
    vs, vc = conv_fwd(proj, w_dw_full, b_dw, ln_g, ln_b)
    y_conv = mm_nn(vs, wg["w_conv_out"], "conv_out")

    br2 = b_re[0].transpose(0, 2, 1).reshape(G * H, P)
    bi2 = b_im[0].transpose(0, 2, 1).reshape(G * H, P)
    ldt = log_dt[0].reshape(G, 1)
    expand = jnp.repeat(jnp.eye(G, dtype=F32), H, axis=0)
    lbr, lbi, bbr, bbi = s5_params(a_re[0], a_im[0], ldt, br2, bi2, expand)
    tabs = s5_tables(lbr.reshape(1, G * P), lbi.reshape(1, G * P))
    bdr, bdi = _block_diag(bbr.reshape(G, H, P)), _block_diag(bbi.reshape(G, H, P))
    cdr = _block_diag(c_re[0].transpose(0, 2, 1))
    cdi = _block_diag(c_im[0].transpose(0, 2, 1))
    s_re, s_im, y_pre, yg = s5_fwd(proj, 2 * CW, bdr, bdi, cdr, cdi, tabs, d_skip)
    ga = mm_nn(yg, wg["w_glu_a"], "glu_a")
    gb = mm_nn(yg, wg["w_glu_b"], "glu_b")
    merged = merge_fwd(proj, 3 * CW, y_conv, ga, gb)
    m_out = mm_nn(merged, wg["w_out"], "out_proj")
    h1, z = residual_norm(xs, m_out, gate1, norm2_g, scale2, shift2)
    act = mm_nn(z, wg["w_ff1"], "ff1", out_dtype=BF16, epi=lambda r: jnp.square(jnp.maximum(r, 0.0)))
    ff = mm_nn(act, wg["w_ff2"], "ff2")

    dh2, dff, loss_part, d_final_g, d_gate2 = loss_bwd(h1, ff, gate2, final_g.reshape(1, D), tgt)
    df = mm_nt(dff, wg["w_ff2"], "ff2_dx", out_dtype=BF16,
               epi=lambda r, a: r * (2.0 * jnp.sqrt(a.astype(F32))), extras=(act,))
    g_ff2 = mm_tn(act, dff, 1, "ff2_dw")
    dz = mm_nt(df, wg["w_ff1"], "ff1_dx")
    g_ff1 = mm_tn(z, df, NDEV, "ff1_dw")
    dh1, d_scale2, d_shift2, d_norm2_g, dmo, d_gate1 = norm_bwd(dz, h1, dh2, norm2_g, scale2, "norm2_bwd", gate1, m_out)
    dmerged = mm_nt(dmo, wg["w_out"], "out_dx")
    g_out = mm_tn(merged, dmo, 1, "out_dw")
    dyc, dga, dgb, dproj_g = merge_bwd(dmerged, proj, 3 * CW, y_conv, ga, gb)
    dvs = mm_nt(dyc, wg["w_conv_out"], "conv_out_dx")
    g_conv_out = mm_tn(vs, dyc, NDEV, "conv_out_dw")
    dyg_a = mm_nt(dga, wg["w_glu_a"], "glu_a_dx")
    dyg_b = mm_nt(dgb, wg["w_glu_b"], "glu_b_dx")
    g_glu_a = mm_tn(yg, dga, NDEV, "glu_a_dw")
    g_glu_b = mm_tn(yg, dgb, NDEV, "glu_b_dw")
    dvc, d_ln_g, d_ln_b = conv_ln_bwd(dvs, vc, ln_g, ln_b)
    dproj_c, d_w_dw, d_b_dw = conv_bwd(dvc, proj, w_dw_full)
    bdrt, bdit = bdr.transpose(0, 2, 1), bdi.transpose(0, 2, 1)
    cdrt, cdit = cdr.transpose(0, 2, 1), cdi.transpose(0, 2, 1)
    dproj_s, d_d_skip, dcdr, dcdi, dbdr, dbdi, dlr8, dli8 = s5_bwd(
        dyg_a, dyg_b, y_pre, proj, 2 * CW, s_re, s_im, bdrt, bdit, cdrt, cdit, tabs, d_skip)
    dproj = jnp.concatenate([dproj_c, dproj_s, dproj_g], axis=1)
    du = mm_nt(dproj, wg["w_in"], "in_dx")
    g_in = mm_tn(u, dproj, NDEV, "in_dw")
    grad_x, d_scale1, d_shift1, d_norm1_g = norm_bwd(du, xs, dh1, norm1_g, scale1, "norm1_bwd")

    dmod = jnp.concatenate([d_shift1, d_scale1, d_gate1, d_shift2, d_scale2, d_gate2], axis=1)
    d_c_re = _diag_blocks(dcdr, H, P)
    d_c_im = _diag_blocks(dcdi, H, P)
    d_bbr = _diag_blocks(dbdr, H, P)
    d_bbi = _diag_blocks(dbdi, H, P)
    dlr = jnp.sum(dlr8, axis=0).reshape(G, P)
    dli = jnp.sum(dli8, axis=0).reshape(G, P)
    small_parts = [dmod, d_norm1_g, d_b_dw, d_ln_g, d_ln_b, dlr, dli, d_bbr, d_bbi, d_c_re, d_c_im, d_d_skip,
                   d_norm2_g, d_final_g, d_w_dw]
    pack = _flat_pad(small_parts, PACK).reshape(-1, 1024)
    (pack_all,) = _exchange([pack], "gather_small_grads", True)
    tot = sum_devices(pack_all, "sum_small_grads").reshape(-1)
    (g_b_ada, g_norm1_g, g_b_dw, g_ln_g, g_ln_b, t_lr, t_li, t_bbr, t_bbi, g_c_re_t, g_c_im_t, g_d_skip,
     g_norm2_g, g_final_g, g_w_dw_full) = _split(tot, small_parts)
    g_a_re, g_a_im, g_ldt, g_br2, g_bi2 = s5_params_bwd(
        a_re[0], a_im[0], ldt, br2, bi2, expand, t_lr, t_li, t_bbr.reshape(G * H, P), t_bbi.reshape(G * H, P))
    g_brt, g_bit = g_br2.reshape(G, H, P), g_bi2.reshape(G, H, P)
    dmod_all = pack_all[:, :, :].reshape(NDEV, -1)[:, :6 * D]
    dmod_cols = lax.dynamic_slice(dmod_all, (0, me * n_ada), (NDEV, n_ada))

    grads = {
        "b_ada": g_b_ada, "norm1_g": g_norm1_g, "b_dw": g_b_dw, "ln_g": g_ln_g, "ln_b": g_ln_b,
        "a_re": g_a_re[None], "a_im": g_a_im[None], "log_dt": g_ldt.reshape(1, G),
        "b_re": g_brt.transpose(0, 2, 1)[None], "b_im": g_bit.transpose(0, 2, 1)[None],
        "c_re": g_c_re_t[None], "c_im": g_c_im_t[None], "d_skip": g_d_skip, "norm2_g": g_norm2_g,
        "final_g": g_final_g.reshape(D),
        "w_dw": lax.dynamic_slice(g_w_dw_full, (0, me * (CW // NDEV)), (CONV_K, CW // NDEV))[None],
    }
    small = [k for k in names if k in grads]
    wf = _flat_pad([W[k] for k in small], PACK).reshape(-1, 1024)
    gf = _flat_pad([grads[k] for k in small], PACK).reshape(-1, 1024)
    mf = _flat_pad([Mo[k] for k in small], PACK).reshape(-1, 1024)
    vf = _flat_pad([Vo[k] for k in small], PACK).reshape(-1, 1024)
    d_s, m_s, v_s = adam_update(wf, gf, mf, vf, "adam_small")
    like = [W[k] for k in small]
    delta = dict(zip(small, _split(d_s.reshape(-1), like)))
    new_m = dict(zip(small, _split(m_s.reshape(-1), like)))
    new_v = dict(zip(small, _split(v_s.reshape(-1), like)))

    send = {"w_in": g_in, "w_conv_out": g_conv_out, "w_glu_a": g_glu_a, "w_glu_b": g_glu_b,
            "w_out": g_out.reshape(NDEV, -1, D), "w_ff1": g_ff1, "w_ff2": g_ff2.reshape(NDEV, -1, D)}
    parts = dict(zip(big, _exchange([send[k] for k in big], "scatter_grads", False)))
    for k in big:
        g, d, mm, vv = adam_reduce(parts[k], W[k][0], Mo[k][0], Vo[k][0], "adam_" + k)
        grads[k], delta[k], new_m[k], new_v[k] = g[None], d[None], mm[None], vv[None]
    g, d, mm, vv = adam_w_ada(c_act, dmod_cols, w_ada[0], m_w_ada[0], v_w_ada[0])
    grads["w_ada"], delta["w_ada"], new_m["w_ada"], new_v["w_ada"] = g[None], d[None], mm[None], vv[None]

    loss = lax.psum(loss_part[0, 0], ("x", "y", "c"))
    return (loss, grad_x[None], *[grads[k] for k in names], *[delta[k] for k in names],
            *[new_m[k] for k in names], *[new_v[k] for k in names])
```

```python
import functools
import math

import jax
import jax.numpy as jnp
from jax import lax
from jax.experimental import pallas as pl
from jax.experimental.pallas import tpu as pltpu

F32 = jnp.float32
BF16 = jnp.bfloat16
NDEV = 8
EPS = 1e-6
ADAM_LR, ADAM_B1, ADAM_B2, ADAM_EPS, ADAM_WD, ADAM_STEP = 0.001, 0.9, 0.999, 1e-08, 0.01, 10
CONV_K = 31
HALO = 32
GROUP = 16
STATE = 64
GB = 8
HI = lax.Precision.HIGHEST
MESH = pl.DeviceIdType.MESH
VMEM_LIMIT = 56 * 1024 * 1024
PACK_ROWS = 64
PACK = PACK_ROWS * 1024
ANY = pl.BlockSpec(memory_space=pl.ANY)


def _params(sem=None):
    if sem is None:
        return pltpu.CompilerParams(vmem_limit_bytes=VMEM_LIMIT)
    return pltpu.CompilerParams(dimension_semantics=sem, vmem_limit_bytes=VMEM_LIMIT)


def _sigmoid(v):
    return 1.0 / (1.0 + jnp.exp(-v))


def _me():
    return 4 * lax.axis_index("x") + 2 * lax.axis_index("y") + lax.axis_index("c")


def _peer(k):
    x, y, c = lax.axis_index("x"), lax.axis_index("y"), lax.axis_index("c")
    px = 1 - x if (k >> 2) & 1 else x
    py = 1 - y if (k >> 1) & 1 else y
    pc = 1 - c if k & 1 else c
    return (px, py, pc), 4 * px + 2 * py + pc


def _exchange(arrays, name, gather):
    n = len(arrays)
    out_shape = []
    for a in arrays:
        shp = (NDEV,) + a.shape if gather else a.shape
        out_shape.append(jax.ShapeDtypeStruct(shp, a.dtype))

    def body(*refs):
        ins, outs = refs[:n], refs[n:2 * n]
        send, recv, lsem = refs[2 * n:]
        me = _me()
        local = []
        for a in range(n):
            src = ins[a] if gather else ins[a].at[me]
            cp = pltpu.make_async_copy(src, outs[a].at[me], lsem.at[a])
            cp.start()
            local.append(cp)
        sends = []
        for a in range(n):
            for k in range(1, NDEV):
                dev, pidx = _peer(k)
                src = ins[a] if gather else ins[a].at[pidx]
                cp = pltpu.make_async_remote_copy(
                    src_ref=src, dst_ref=outs[a].at[me], send_sem=send.at[a * (NDEV - 1) + k - 1], recv_sem=recv.at[a * (NDEV - 1) + k - 1],
                    device_id=dev, device_id_type=MESH)
                cp.start()
                sends.append(cp)
        for a in range(n):
            for k in range(1, NDEV):
                dev, pidx = _peer(k)
                src = ins[a] if gather else ins[a].at[pidx]
                pltpu.make_async_remote_copy(
                    src_ref=src, dst_ref=outs[a].at[pidx], send_sem=send.at[a * (NDEV - 1) + k - 1], recv_sem=recv.at[a * (NDEV - 1) + k - 1],
                    device_id=dev, device_id_type=MESH).wait_recv()
        for cp in sends:
            cp.wait_send()
        for cp in local:
            cp.wait()

    return pl.pallas_call(
        body, name=name, out_shape=tuple(out_shape),
        in_specs=[ANY] * n, out_specs=tuple([ANY] * n),
        scratch_shapes=[pltpu.SemaphoreType.DMA((n * (NDEV - 1),)), pltpu.SemaphoreType.DMA((n * (NDEV - 1),)),
                        pltpu.SemaphoreType.DMA((n,))],
    )(*arrays)


HBM = pl.BlockSpec(memory_space=pltpu.HBM)
SEM = pl.BlockSpec(memory_space=pltpu.SEMAPHORE)
EFFECT = pltpu.SideEffectType.DATAFLOW_SIDE_EFFECTING
NPEER = NDEV - 1


def _landing(block_of_me, shape, dtype):
    land = lax.empty((NDEV,) + tuple(shape), dtype)
    start = (_me(),) + (0,) * len(shape)
    return pltpu.with_memory_space_constraint(lax.dynamic_update_slice(land, block_of_me[None], start), pltpu.HBM)


def exchange_start(arrays, name, gather):
    n = len(arrays)
    me = _me()
    lands = []
    for a in arrays:
        if gather:
            lands.append(_landing(a, a.shape, a.dtype))
        else:
            mine = lax.dynamic_slice(a, (me,) + (0,) * (a.ndim - 1), (1,) + a.shape[1:])[0]
            lands.append(_landing(mine, a.shape[1:], a.dtype))
    srcs = [pltpu.with_memory_space_constraint(a, pltpu.HBM) for a in arrays]

    def body(*refs):
        ins, lnd = refs[:n], refs[n:2 * n]
        outs = refs[2 * n:]
        sends, recvs, token = outs[:n], outs[n:2 * n], outs[-1]
        my = _me()
        for a in range(n):
            for k in range(1, NDEV):
                dev, pidx = _peer(k)
                src = ins[a] if gather else ins[a].at[pidx]
                pltpu.make_async_remote_copy(
                    src_ref=src, dst_ref=lnd[a].at[my], send_sem=sends[a].at[k - 1], recv_sem=recvs[a].at[k - 1],
                    device_id=dev, device_id_type=MESH).start()
        token[...] = jnp.zeros_like(token)

    out_shape = ([pltpu.SemaphoreType.DMA((NPEER,))] * (2 * n)
                 + [pltpu.HBM(a.shape, a.dtype) for a in srcs] + [pltpu.HBM(l.shape, l.dtype) for l in lands]
                 + [jax.ShapeDtypeStruct((8, 128), F32)])
    res = pl.pallas_call(
        body, name=name, out_shape=tuple(out_shape),
        in_specs=[HBM] * (2 * n), out_specs=tuple([SEM] * (2 * n) + [HBM] * (2 * n) + [pl.BlockSpec(memory_space=pltpu.VMEM)]),
        input_output_aliases={i: 2 * n + i for i in range(2 * n)},
        compiler_params=pltpu.CompilerParams(has_side_effects=EFFECT),
    )(*srcs, *lands)
    handles = [(res[a], res[n + a], res[2 * n + a], res[3 * n + a]) for a in range(n)]
    return handles, res[-1]


def exchange_wait(handle, after, name, gather):
    send_sem, recv_sem, src, land = handle

    def body(src_ref, land_ref, s_ref, r_ref, after_ref, src_out, land_out):
        for k in range(1, NDEV):
            dev, pidx = _peer(k)
            s = src_ref if gather else src_ref.at[pidx]
            cp = pltpu.make_async_remote_copy(
                src_ref=s, dst_ref=land_ref.at[pidx], send_sem=s_ref.at[k - 1], recv_sem=r_ref.at[k - 1],
                device_id=dev, device_id_type=MESH)
            cp.wait_send()
            cp.wait_recv()

    return pl.pallas_call(
        body, name=name, out_shape=(pltpu.HBM(src.shape, src.dtype), pltpu.HBM(land.shape, land.dtype)),
        in_specs=(HBM, HBM, SEM, SEM, ANY), out_specs=(HBM, HBM), input_output_aliases={0: 0, 1: 1},
        compiler_params=pltpu.CompilerParams(has_side_effects=EFFECT),
    )(src, land, send_sem, recv_sem, after)[1]


def _acc_steps(p, acc, k, nk, finish):
    if nk == 1:
        finish(p)
        return

    @pl.when(k == 0)
    def _():
        acc[...] = p

    @pl.when(k > 0)
    def _():
        acc[...] += p

    @pl.when(k == nk - 1)
    def _():
        finish(acc[...])


def mm_nn(a, w3, name, out_dtype=F32, epi=None, extras=()):
    M, K = a.shape
    J, _, n = w3.shape
    tm, tn, tk = min(1024, M), min(1024, n), min(2048, K)
    q, nk, ne = n // tn, K // tk, len(extras)

    def body(*refs):
        a_ref, w_ref = refs[:2]
        ex, o_ref, acc = refs[2:2 + ne], refs[2 + ne], refs[-1]
        p = jnp.dot(a_ref[...], w_ref[...], preferred_element_type=F32)

        def finish(r):
            if epi is not None:
                r = epi(r, *[e[...] for e in ex])
            o_ref[...] = r.astype(out_dtype)

        _acc_steps(p, acc, pl.program_id(2), nk, finish)

    return pl.pallas_call(
        body, name=name, grid=(M // tm, J * q, nk),
        in_specs=[pl.BlockSpec((tm, tk), lambda i, j, k: (i, k)),
                  pl.BlockSpec((None, tk, tn), lambda i, j, k: (j // q, k, j % q))]
        + [pl.BlockSpec((tm, tn), lambda i, j, k: (i, j))] * ne,
        out_specs=pl.BlockSpec((tm, tn), lambda i, j, k: (i, j)),
        out_shape=jax.ShapeDtypeStruct((M, J * n), out_dtype),
        scratch_shapes=[pltpu.VMEM((tm, tn), F32)],
        compiler_params=_params(("parallel", "parallel", "arbitrary")),
    )(a, w3, *extras)


def mm_nt(dy, w3, name, out_dtype=F32, epi=None, extras=()):
    M, _ = dy.shape
    J, K, n = w3.shape
    tm, tn, tkk = min(1024, M), min(1024, n), min(1024, K)
    q, ne = n // tn, len(extras)
    nk = J * q

    def body(*refs):
        d_ref, w_ref = refs[:2]
        ex, o_ref, acc = refs[2:2 + ne], refs[2 + ne], refs[-1]
        p = lax.dot_general(d_ref[...], w_ref[...], (((1,), (1,)), ((), ())), preferred_element_type=F32)

        def finish(r):
            if epi is not None:
                r = epi(r, *[e[...] for e in ex])
            o_ref[...] = r.astype(out_dtype)

        _acc_steps(p, acc, pl.program_id(2), nk, finish)

    return pl.pallas_call(
        body, name=name, grid=(M // tm, K // tkk, nk),
        in_specs=[pl.BlockSpec((tm, tn), lambda i, kk, c: (i, c)),
                  pl.BlockSpec((None, tkk, tn), lambda i, kk, c: (c // q, kk, c % q))]
        + [pl.BlockSpec((tm, tkk), lambda i, kk, c: (i, kk))] * ne,
        out_specs=pl.BlockSpec((tm, tkk), lambda i, kk, c: (i, kk)),
        out_shape=jax.ShapeDtypeStruct((M, K), out_dtype),
        scratch_shapes=[pltpu.VMEM((tm, tkk), F32)],
        compiler_params=_params(("parallel", "parallel", "arbitrary")),
    )(dy, w3, *extras)


def mm_tn(a, dy, J, name, out_dtype=F32):
    M, K = a.shape
    n = dy.shape[1] // J
    tm, tn, tkk = min(1024, M), min(1024, n), min(1024, K)
    q, nk = n // tn, M // tm

    def body(a_ref, d_ref, o_ref, acc):
        p = lax.dot_general(a_ref[...], d_ref[...], (((0,), (0,)), ((), ())), preferred_element_type=F32)

        def finish(r):
            o_ref[...] = r.astype(out_dtype)

        _acc_steps(p, acc, pl.program_id(2), nk, finish)

    return pl.pallas_call(
        body, name=name, grid=(K // tkk, J * q, nk),
        in_specs=[pl.BlockSpec((tm, tkk), lambda kk, c, m: (m, kk)),
                  pl.BlockSpec((tm, tn), lambda kk, c, m: (m, c))],
        out_specs=pl.BlockSpec((None, tkk, tn), lambda kk, c, m: (c // q, kk, c % q)),
        out_shape=jax.ShapeDtypeStruct((J, K, n), out_dtype),
        scratch_shapes=[pltpu.VMEM((tkk, tn), F32)],
        compiler_params=_params(("parallel", "parallel", "arbitrary")),
    )(a, dy)


def _tm(L):
    return min(256, L)


def _row(w, cb=0, tm=None):
    return pl.BlockSpec((tm, w), lambda i: (i, cb))


def _vec(w, cb=0):
    return pl.BlockSpec((1, w), lambda i: (0, cb))


def _accum(ref, val, i):
    @pl.when(i == 0)
    def _():
        ref[...] = val

    @pl.when(i > 0)
    def _():
        ref[...] += val


def _colsum(v):
    return jnp.sum(v, axis=0, keepdims=True)


def _rms(v):
    return lax.rsqrt(jnp.mean(v * v, axis=-1, keepdims=True) + EPS)


def adaln_mod(c_all, w_ada, b_cols):
    B, D = c_all.shape
    n = w_ada.shape[1]
    tn = 512

    def body(c_ref, w_ref, b_ref, o_ref, ca_ref):
        cv = c_ref[...]
        ca = cv * _sigmoid(cv)
        ca_ref[...] = ca
        o_ref[...] = jnp.dot(ca.astype(BF16), w_ref[...].astype(BF16), preferred_element_type=F32) + b_ref[...]

    return pl.pallas_call(
        body, name="adaln_mod", grid=(n // tn,),
        in_specs=[pl.BlockSpec((B, D), lambda j: (0, 0)), pl.BlockSpec((D, tn), lambda j: (0, j)),
                  pl.BlockSpec((1, tn), lambda j: (0, j))],
        out_specs=(pl.BlockSpec((B, tn), lambda j: (0, j)), pl.BlockSpec((B, D), lambda j: (0, 0))),
        out_shape=(jax.ShapeDtypeStruct((B, n), F32), jax.ShapeDtypeStruct((B, D), F32)),
        compiler_params=_params(("arbitrary",)),
    )(c_all, w_ada, b_cols)


def prenorm(x, g, scale, shift, name):
    L, D = x.shape
    tm = _tm(L)

    def body(x_ref, g_ref, sc_ref, sh_ref, u_ref):
        xv = x_ref[...]
        u_ref[...] = (xv * _rms(xv) * g_ref[...] * (1.0 + sc_ref[...]) + sh_ref[...]).astype(BF16)

    return pl.pallas_call(
        body, name=name, grid=(L // tm,),
        in_specs=[_row(D, tm=tm), _vec(D), _vec(D), _vec(D)],
        out_specs=_row(D, tm=tm), out_shape=jax.ShapeDtypeStruct((L, D), BF16),
        compiler_params=_params(("parallel",)),
    )(x, g, scale, shift)


def conv_fwd(proj, w_dw, b_dw, ln_g, ln_b):
    L = proj.shape[0]
    C = w_dw.shape[1]
    tm = _tm(L)
    hb = tm // HALO

    def body(a_ref, g_ref, ah_ref, gh_ref, w_ref, b_ref, lg_ref, lb_ref, vs_ref, vc_ref, buf):
        i = pl.program_id(0)
        halo = ah_ref[...] * _sigmoid(gh_ref[...])
        buf[0:HALO, :] = halo * jnp.where(i > 0, 1.0, 0.0)
        buf[HALO:HALO + tm, :] = a_ref[...] * _sigmoid(g_ref[...])
        acc = jnp.zeros((tm, C), F32) + b_ref[...]
        for k in range(CONV_K):
            acc = acc + w_ref[k:k + 1, :] * buf[pl.ds(HALO - (CONV_K - 1) + k, tm), :]
        vc_ref[...] = acc
        mu = jnp.mean(acc, axis=-1, keepdims=True)
        d = acc - mu
        var = jnp.mean(d * d, axis=-1, keepdims=True)
        ln = d * lax.rsqrt(var + EPS) * lg_ref[...] + lb_ref[...]
        vs_ref[...] = (ln * _sigmoid(ln)).astype(BF16)

    prev = lambda cb: pl.BlockSpec((HALO, C), lambda i: (jnp.maximum(i * hb - 1, 0), cb))
    return pl.pallas_call(
        body, name="conv_fwd", grid=(L // tm,),
        in_specs=[_row(C, 0, tm), _row(C, 1, tm), prev(0), prev(1),
                  pl.BlockSpec((HALO, C), lambda i: (0, 0)), _vec(C), _vec(C), _vec(C)],
        out_specs=(_row(C, tm=tm), _row(C, tm=tm)),
        out_shape=(jax.ShapeDtypeStruct((L, C), BF16), jax.ShapeDtypeStruct((L, C), F32)),
        scratch_shapes=[pltpu.VMEM((HALO + tm, C), F32)],
        compiler_params=_params(("parallel",)),
    )(proj, proj, proj, proj, w_dw, b_dw, ln_g, ln_b)


def _gelu(v):
    return 0.5 * v * (1.0 + jnp.tanh(math.sqrt(2.0 / math.pi) * (v + 0.044715 * v * v * v)))


def _gelu_grad(v):
    k = math.sqrt(2.0 / math.pi)
    t = jnp.tanh(k * (v + 0.044715 * v * v * v))
    return 0.5 * (1.0 + t) + 0.5 * v * (1.0 - t * t) * k * (1.0 + 3.0 * 0.044715 * v * v)


def s5_param_fn(ar, ai, ldt, br, bi, expand):
    dt = jnp.exp(ldt)
    er = jnp.exp(ar * dt)
    th = ai * dt
    lbr, lbi = er * jnp.cos(th), er * jnp.sin(th)
    nr, ni = lbr - 1.0, lbi
    den = ar * ar + ai * ai
    qr, qi = (nr * ar + ni * ai) / den, (ni * ar - nr * ai) / den
    qre = jnp.dot(expand, qr, precision=HI, preferred_element_type=F32)
    qie = jnp.dot(expand, qi, precision=HI, preferred_element_type=F32)
    return lbr, lbi, qre * br - qie * bi, qre * bi + qie * br


def s5_params(ar, ai, ldt, br2, bi2, expand):
    def body(ar_ref, ai_ref, ld_ref, br_ref, bi_ref, e_ref, o1, o2, o3, o4):
        r = s5_param_fn(ar_ref[...], ai_ref[...], ld_ref[...], br_ref[...], bi_ref[...], e_ref[...])
        o1[...], o2[...], o3[...], o4[...] = r

    s2, s3 = jax.ShapeDtypeStruct(ar.shape, F32), jax.ShapeDtypeStruct(br2.shape, F32)
    return pl.pallas_call(body, name="s5_params", out_shape=(s2, s2, s3, s3), compiler_params=_params())(
        ar, ai, ldt, br2, bi2, expand)


def s5_params_bwd(ar, ai, ldt, br2, bi2, expand, dlr, dli, dbr, dbi):
    def body(ar_ref, ai_ref, ld_ref, br_ref, bi_ref, e_ref, c1, c2, c3, c4, o1, o2, o3, o4, o5):
        e = e_ref[...]
        fn = lambda a, b, c, d, f: s5_param_fn(a, b, c, d, f, e)
        _, vjp = jax.vjp(fn, ar_ref[...], ai_ref[...], ld_ref[...], br_ref[...], bi_ref[...])
        r = vjp((c1[...], c2[...], c3[...], c4[...]))
        o1[...], o2[...], o3[...], o4[...], o5[...] = r

    shapes = tuple(jax.ShapeDtypeStruct(v.shape, F32) for v in (ar, ai, ldt, br2, bi2))
    return pl.pallas_call(body, name="s5_params_bwd", out_shape=shapes, compiler_params=_params())(
        ar, ai, ldt, br2, bi2, expand, dlr, dli, dbr, dbi)


def s5_tables(lr, li):
    C = lr.shape[1]

    def body(lr_ref, li_ref, o_ref):
        row = lax.broadcasted_iota(jnp.int32, (8, C), 0)
        for rev in (0, 1):
            pr = jnp.broadcast_to(lr_ref[...], (8, C))
            pi = jnp.broadcast_to(-li_ref[...] if rev else li_ref[...], (8, C))
            br, bi = pr, pi
            pows = [(pr, pi)]
            for _ in range(7):
                pr, pi = pr * br - pi * bi, pr * bi + pi * br
                pows.append((pr, pi))
            base = 8 * rev
            for s, d in enumerate((1, 2, 4)):
                keep = (row + d <= 7) if rev else (row >= d)
                o_ref[base + 2 * s] = jnp.where(keep, pows[d - 1][0], 0.0)
                o_ref[base + 2 * s + 1] = jnp.where(keep, pows[d - 1][1], 0.0)
            cr, ci = jnp.zeros((8, C), F32), jnp.zeros((8, C), F32)
            for j in range(8):
                e = (8 - j) if rev else (j + 1)
                cr = jnp.where(row == j, pows[e - 1][0], cr)
                ci = jnp.where(row == j, pows[e - 1][1], ci)
            o_ref[base + 6] = cr
            o_ref[base + 7] = ci

    return pl.pallas_call(body, name="s5_tables", out_shape=jax.ShapeDtypeStruct((16, 8, C), F32),
                          compiler_params=_params())(lr, li)


def _scan_tile(xr, xi, tabs, cr, ci, rev):
    for s, d in enumerate((1, 2, 4)):
        tr, ti = tabs[2 * s], tabs[2 * s + 1]
        sh = (8 - d) if rev else d
        sr, si = pltpu.roll(xr, sh, 0), pltpu.roll(xi, sh, 0)
        xr, xi = xr + tr * sr - ti * si, xi + tr * si + ti * sr
    tr, ti = tabs[6], tabs[7]
    xr, xi = xr + tr * cr - ti * ci, xi + tr * ci + ti * cr
    return xr, xi


def s5_fwd(proj, col0, bdr, bdi, cdr, cdi, tabs, d_skip):
    L = proj.shape[0]
    nb, cw, sw = bdr.shape
    tl = min(256, L)
    cb0 = col0 // cw

    def body(u_ref, bdr_ref, bdi_ref, cdr_ref, cdi_ref, t_ref, dk_ref, sr_ref, si_ref, yp_ref, yg_ref, car):
        l = pl.program_id(1)

        @pl.when(l == 0)
        def _():
            car[...] = jnp.zeros_like(car)

        u = u_ref[...]
        sr_ref[...] = jnp.dot(u, bdr_ref[...], precision=HI, preferred_element_type=F32)
        si_ref[...] = jnp.dot(u, bdi_ref[...], precision=HI, preferred_element_type=F32)
        def tile(i, c):
            tabs = [t_ref[j] for j in range(8)]
            r0 = pl.multiple_of(i * 8, 8)
            xr, xi = _scan_tile(sr_ref[pl.ds(r0, 8), :], si_ref[pl.ds(r0, 8), :], tabs, c[0], c[1], False)
            sr_ref[pl.ds(r0, 8), :] = xr
            si_ref[pl.ds(r0, 8), :] = xi
            return xr[7:8, :], xi[7:8, :]

        c = lax.fori_loop(0, tl // 8, tile, (car[0:1, :], car[1:2, :]))
        car[0:1, :] = c[0]
        car[1:2, :] = c[1]
        y = (jnp.dot(sr_ref[...], cdr_ref[...], precision=HI, preferred_element_type=F32)
             - jnp.dot(si_ref[...], cdi_ref[...], precision=HI, preferred_element_type=F32)
             + dk_ref[...] * u)
        yp_ref[...] = y
        yg_ref[...] = _gelu(y).astype(BF16)

    blk = lambda r, c: pl.BlockSpec((None, r, c), lambda b, l: (b, 0, 0))
    return pl.pallas_call(
        body, name="s5_fwd", grid=(nb, L // tl),
        in_specs=[pl.BlockSpec((tl, cw), lambda b, l: (l, cb0 + b)), blk(cw, sw), blk(cw, sw), blk(sw, cw), blk(sw, cw),
                  pl.BlockSpec((8, 8, sw), lambda b, l: (0, 0, b)), pl.BlockSpec((1, cw), lambda b, l: (0, b))],
        out_specs=(pl.BlockSpec((tl, sw), lambda b, l: (l, b)), pl.BlockSpec((tl, sw), lambda b, l: (l, b)),
                   pl.BlockSpec((tl, cw), lambda b, l: (l, b)), pl.BlockSpec((tl, cw), lambda b, l: (l, b))),
        out_shape=(jax.ShapeDtypeStruct((L, nb * sw), F32), jax.ShapeDtypeStruct((L, nb * sw), F32),
                   jax.ShapeDtypeStruct((L, nb * cw), F32), jax.ShapeDtypeStruct((L, nb * cw), BF16)),
        scratch_shapes=[pltpu.VMEM((8, sw), F32)],
        compiler_params=_params(("parallel", "arbitrary")),
    )(proj, bdr, bdi, cdr, cdi, tabs, d_skip)


def s5_bwd(dyg_a, dyg_b, yp, proj, col0, s_re, s_im, bdrt, bdit, cdrt, cdit, tabs, d_skip):
    L = proj.shape[0]
    nb, sw, cw = bdrt.shape
    tl = min(256, L)
    nl = L // tl
    cb0 = col0 // cw
    tb = tl // 8

    def body(da_ref, db_ref, yp_ref, u_ref, sr_ref, si_ref, hr_ref, hi_ref, bdrt_ref, bdit_ref, cdrt_ref, cdit_ref,
             t_ref, dk_ref, du_ref, ddk_ref, dcr_ref, dci_ref, dbr_ref, dbi_ref, dlr_ref, dli_ref,
             gr, gi, pr, pi, car):
        l = pl.program_id(1)
        first = l == nl - 1

        @pl.when(l == 0)
        def _():
            car[...] = jnp.zeros_like(car)

        u = u_ref[...]
        dy = (da_ref[...] + db_ref[...]) * _gelu_grad(yp_ref[...])
        gr[...] = jnp.dot(dy, cdrt_ref[...], precision=HI, preferred_element_type=F32)
        gi[...] = -jnp.dot(dy, cdit_ref[...], precision=HI, preferred_element_type=F32)
        inner = jnp.where(first, 0.0, 1.0)
        pr[0:8, :] = hr_ref[...] * inner
        pi[0:8, :] = hi_ref[...] * inner
        pr[8:8 + tl, :] = sr_ref[...]
        pi[8:8 + tl, :] = si_ref[...]
        row = lax.broadcasted_iota(jnp.int32, (8, sw), 0)

        def tile(j, c):
            tabs = [t_ref[8 + k] for k in range(8)]
            r0 = pl.multiple_of((tb - 1 - j) * 8, 8)
            xr, xi = _scan_tile(gr[pl.ds(r0, 8), :], gi[pl.ds(r0, 8), :], tabs, c[0], c[1], True)
            gr[pl.ds(r0, 8), :] = xr
            gi[pl.ds(r0, 8), :] = xi
            qr = jnp.where(row == 0, pltpu.roll(pr[pl.ds(r0, 8), :], 1, 0), pltpu.roll(pr[pl.ds(r0 + 8, 8), :], 1, 0))
            qi = jnp.where(row == 0, pltpu.roll(pi[pl.ds(r0, 8), :], 1, 0), pltpu.roll(pi[pl.ds(r0 + 8, 8), :], 1, 0))
            return xr[0:1, :], xi[0:1, :], c[2] + xr * qr + xi * qi, c[3] + xi * qr - xr * qi

        z = jnp.zeros((8, sw), F32)
        c = lax.fori_loop(0, tb, tile, (car[0:1, :], car[1:2, :], z, z))
        car[0:1, :] = c[0]
        car[1:2, :] = c[1]
        g_re, g_im = gr[...], gi[...]
        du_ref[...] = (dy * dk_ref[...] + jnp.dot(g_re, bdrt_ref[...], precision=HI, preferred_element_type=F32)
                       + jnp.dot(g_im, bdit_ref[...], precision=HI, preferred_element_type=F32)).astype(BF16)
        tn = (((0,), (0,)), ((), ()))
        _accum(ddk_ref, _colsum(dy * u), l)
        _accum(dcr_ref, lax.dot_general(dy, sr_ref[...], tn, precision=HI, preferred_element_type=F32), l)
        _accum(dci_ref, -lax.dot_general(dy, si_ref[...], tn, precision=HI, preferred_element_type=F32), l)
        _accum(dbr_ref, lax.dot_general(u, g_re, tn, precision=HI, preferred_element_type=F32), l)
        _accum(dbi_ref, lax.dot_general(u, g_im, tn, precision=HI, preferred_element_type=F32), l)
        _accum(dlr_ref, c[2], l)
        _accum(dli_ref, c[3], l)

    rl = lambda l: nl - 1 - l
    cblk = lambda w, off=0: pl.BlockSpec((tl, w), lambda b, l: (rl(l), off + b))
    halo = pl.BlockSpec((8, sw), lambda b, l: (jnp.maximum(rl(l) * tb - 1, 0), b))
    mat = lambda r, c: pl.BlockSpec((None, r, c), lambda b, l: (b, 0, 0))
    return pl.pallas_call(
        body, name="s5_bwd", grid=(nb, nl),
        in_specs=[cblk(cw), cblk(cw), cblk(cw), cblk(cw, cb0), cblk(sw), cblk(sw), halo, halo,
                  mat(sw, cw), mat(sw, cw), mat(cw, sw), mat(cw, sw),
                  pl.BlockSpec((16, 8, sw), lambda b, l: (0, 0, b)), pl.BlockSpec((1, cw), lambda b, l: (0, b))],
        out_specs=(cblk(cw), pl.BlockSpec((1, cw), lambda b, l: (0, b)), mat(cw, sw), mat(cw, sw), mat(cw, sw), mat(cw, sw),
                   pl.BlockSpec((8, sw), lambda b, l: (0, b)), pl.BlockSpec((8, sw), lambda b, l: (0, b))),
        out_shape=(jax.ShapeDtypeStruct((L, nb * cw), BF16), jax.ShapeDtypeStruct((1, nb * cw), F32),
                   jax.ShapeDtypeStruct((nb, cw, sw), F32), jax.ShapeDtypeStruct((nb, cw, sw), F32),
                   jax.ShapeDtypeStruct((nb, cw, sw), F32), jax.ShapeDtypeStruct((nb, cw, sw), F32),
                   jax.ShapeDtypeStruct((8, nb * sw), F32), jax.ShapeDtypeStruct((8, nb * sw), F32)),
        scratch_shapes=[pltpu.VMEM((tl, sw), F32), pltpu.VMEM((tl, sw), F32),
                        pltpu.VMEM((tl + 8, sw), F32), pltpu.VMEM((tl + 8, sw), F32), pltpu.VMEM((8, sw), F32)],
        compiler_params=_params(("parallel", "arbitrary")),
    )(dyg_a, dyg_b, yp, proj, s_re, s_im, s_re, s_im, bdrt, bdit, cdrt, cdit, tabs, d_skip)


def merge_fwd(proj, col_gc, y_conv, ga, gb):
    L, D = y_conv.shape
    tm = _tm(L)
    h = D // 2
    c0 = col_gc // h

    def body(p0, p1, p2, p3, yc_ref, ga_ref, gb_ref, o_ref):
        gc, gs = (p0, p1), (p2, p3)
        for s in range(2):
            cols = slice(s * h, (s + 1) * h)
            y_ssm = ga_ref[:, cols] * _sigmoid(gb_ref[:, cols])
            o_ref[:, cols] = (_sigmoid(gc[s][...]) * yc_ref[:, cols] + _sigmoid(gs[s][...]) * y_ssm).astype(BF16)

    return pl.pallas_call(
        body, name="merge_fwd", grid=(L // tm,),
        in_specs=[_row(h, c0 + s, tm) for s in range(4)] + [_row(D, tm=tm)] * 3,
        out_specs=_row(D, tm=tm), out_shape=jax.ShapeDtypeStruct((L, D), BF16),
        compiler_params=_params(("parallel",)),
    )(proj, proj, proj, proj, y_conv, ga, gb)


def residual_norm(x, m_out, gate, g, scale, shift):
    L, D = x.shape
    tm = _tm(L)

    def body(x_ref, m_ref, gt_ref, g_ref, sc_ref, sh_ref, h_ref, z_ref):
        h = x_ref[...] + gt_ref[...] * m_ref[...]
        h_ref[...] = h
        z_ref[...] = (h * _rms(h) * g_ref[...] * (1.0 + sc_ref[...]) + sh_ref[...]).astype(BF16)

    return pl.pallas_call(
        body, name="residual_norm", grid=(L // tm,),
        in_specs=[_row(D, tm=tm), _row(D, tm=tm), _vec(D), _vec(D), _vec(D), _vec(D)],
        out_specs=(_row(D, tm=tm), _row(D, tm=tm)),
        out_shape=(jax.ShapeDtypeStruct((L, D), F32), jax.ShapeDtypeStruct((L, D), BF16)),
        compiler_params=_params(("parallel",)),
    )(x, m_out, gate, g, scale, shift)


def loss_bwd(h1, ff, gate2, final_g, target):
    L, D = h1.shape
    tm = _tm(L)

    def body(h_ref, f_ref, gt_ref, g_ref, t_ref, dh_ref, dff_ref, loss_ref, dg_ref, dgt_ref):
        i = pl.program_id(0)
        ffv = f_ref[...]
        h2 = h_ref[...] + gt_ref[...] * ffv
        r = _rms(h2)
        n = h2 * r
        err = n * g_ref[...] - t_ref[...]
        per_tok = jnp.mean(err * err, axis=-1, keepdims=True)
        _accum(loss_ref, 0.5 * jnp.sum(per_tok, axis=0, keepdims=True), i)
        dy = err * (1.0 / D)
        _accum(dg_ref, _colsum(dy * n), i)
        dn = dy * g_ref[...]
        dh2 = r * (dn - n * jnp.mean(dn * n, axis=-1, keepdims=True))
        dh_ref[...] = dh2
        dff_ref[...] = (gt_ref[...] * dh2).astype(BF16)
        _accum(dgt_ref, _colsum(dh2 * ffv), i)

    return pl.pallas_call(
        body, name="loss_bwd", grid=(L // tm,),
        in_specs=[_row(D, tm=tm), _row(D, tm=tm), _vec(D), _vec(D), _row(D, tm=tm)],
        out_specs=(_row(D, tm=tm), _row(D, tm=tm), pl.BlockSpec((1, 1), lambda i: (0, 0)), _vec(D), _vec(D)),
        out_shape=(jax.ShapeDtypeStruct((L, D), F32), jax.ShapeDtypeStruct((L, D), BF16),
                   jax.ShapeDtypeStruct((1, 1), F32), jax.ShapeDtypeStruct((1, D), F32), jax.ShapeDtypeStruct((1, D), F32)),
        compiler_params=_params(("arbitrary",)),
    )(h1, ff, gate2, final_g, target)


def norm_bwd(dz, h, dh_in, g, scale, name, gate=None, m_out=None):
    L, D = h.shape
    tm = _tm(L)
    tail = gate is not None

    def body(*refs):
        dz_ref, h_ref, di_ref, g_ref, sc_ref = refs[:5]
        rest = refs[5:]
        if tail:
            gt_ref, m_ref = rest[:2]
            rest = rest[2:]
        dh_ref, dsc_ref, dsh_ref, dg_ref = rest[:4]
        i = pl.program_id(0)
        hv, dzv = h_ref[...], dz_ref[...]
        r = _rms(hv)
        n = hv * r
        _accum(dsc_ref, _colsum(dzv * n * g_ref[...]), i)
        _accum(dsh_ref, _colsum(dzv), i)
        dzn = dzv * (1.0 + sc_ref[...])
        _accum(dg_ref, _colsum(dzn * n), i)
        dn = dzn * g_ref[...]
        dh = di_ref[...] + r * (dn - n * jnp.mean(dn * n, axis=-1, keepdims=True))
        dh_ref[...] = dh
        if tail:
            dmo_ref, dgt_ref = rest[4:]
            dmo_ref[...] = (gt_ref[...] * dh).astype(BF16)
            _accum(dgt_ref, _colsum(dh * m_ref[...]), i)

    ins = [dz, h, dh_in, g, scale]
    in_specs = [_row(D, tm=tm)] * 3 + [_vec(D)] * 2
    out_specs = [_row(D, tm=tm), _vec(D), _vec(D), _vec(D)]
    out_shape = [jax.ShapeDtypeStruct((L, D), F32)] + [jax.ShapeDtypeStruct((1, D), F32)] * 3
    if tail:
        ins += [gate, m_out]
        in_specs += [_vec(D), _row(D, tm=tm)]
        out_specs += [_row(D, tm=tm), _vec(D)]
        out_shape += [jax.ShapeDtypeStruct((L, D), BF16), jax.ShapeDtypeStruct((1, D), F32)]
    return pl.pallas_call(
        body, name=name, grid=(L // tm,), in_specs=in_specs, out_specs=tuple(out_specs), out_shape=tuple(out_shape),
        compiler_params=_params(("arbitrary",)),
    )(*ins)


def merge_bwd(dmerged, proj, col_gc, y_conv, ga, gb):
    L, D = y_conv.shape
    tm = _tm(L)
    h = D // 2
    c0 = col_gc // h

    def body(dm_ref, p0, p1, p2, p3, yc_ref, ga_ref, gb_ref, dyc_ref, dga_ref, dgb_ref, dg_ref):
        gc, gs = (p0, p1), (p2, p3)
        for s in range(2):
            cols = slice(s * h, (s + 1) * h)
            dm = dm_ref[:, cols]
            sc, ss, sb = _sigmoid(gc[s][...]), _sigmoid(gs[s][...]), _sigmoid(gb_ref[:, cols])
            gav = ga_ref[:, cols]
            dyc_ref[:, cols] = (dm * sc).astype(BF16)
            dg_ref[:, cols] = (dm * yc_ref[:, cols] * sc * (1.0 - sc)).astype(BF16)
            dg_ref[:, D + s * h:D + (s + 1) * h] = (dm * gav * sb * ss * (1.0 - ss)).astype(BF16)
            dys = dm * ss
            dga_ref[:, cols] = (dys * sb).astype(BF16)
            dgb_ref[:, cols] = (dys * gav * sb * (1.0 - sb)).astype(BF16)

    return pl.pallas_call(
        body, name="merge_bwd", grid=(L // tm,),
        in_specs=[_row(D, tm=tm)] + [_row(h, c0 + s, tm) for s in range(4)] + [_row(D, tm=tm)] * 3,
        out_specs=(_row(D, tm=tm), _row(D, tm=tm), _row(D, tm=tm), _row(2 * D, tm=tm)),
        out_shape=(jax.ShapeDtypeStruct((L, D), BF16),) * 3 + (jax.ShapeDtypeStruct((L, 2 * D), BF16),),
        compiler_params=_params(("parallel",)),
    )(dmerged, proj, proj, proj, proj, y_conv, ga, gb)


def conv_ln_bwd(dvs, vc, ln_g, ln_b):
    L, C = vc.shape
    tm = _tm(L)

    def body(d_ref, v_ref, g_ref, b_ref, o_ref, dg_ref, db_ref):
        i = pl.program_id(0)
        v = v_ref[...]
        mu = jnp.mean(v, axis=-1, keepdims=True)
        d = v - mu
        rstd = lax.rsqrt(jnp.mean(d * d, axis=-1, keepdims=True) + EPS)
        xh = d * rstd
        ln = xh * g_ref[...] + b_ref[...]
        sg = _sigmoid(ln)
        dln = d_ref[...] * sg * (1.0 + ln * (1.0 - sg))
        _accum(dg_ref, _colsum(dln * xh), i)
        _accum(db_ref, _colsum(dln), i)
        dxh = dln * g_ref[...]
        o_ref[...] = rstd * (dxh - jnp.mean(dxh, axis=-1, keepdims=True)
                             - xh * jnp.mean(dxh * xh, axis=-1, keepdims=True))

    return pl.pallas_call(
        body, name="conv_ln_bwd", grid=(L // tm,),
        in_specs=[_row(C, tm=tm), _row(C, tm=tm), _vec(C), _vec(C)],
        out_specs=(_row(C, tm=tm), _vec(C), _vec(C)),
        out_shape=(jax.ShapeDtypeStruct((L, C), F32), jax.ShapeDtypeStruct((1, C), F32), jax.ShapeDtypeStruct((1, C), F32)),
        compiler_params=_params(("arbitrary",)),
    )(dvs, vc, ln_g, ln_b)


def conv_bwd(dvc, proj, w_dw):
    L, C = dvc.shape
    tm = _tm(L)
    hb = tm // HALO
    last = L // HALO - 1
    nt = L // tm

    def body(d_ref, dn_ref, a_ref, g_ref, ah_ref, gh_ref, w_ref, o_ref, dw_ref, db_ref, dbuf, vbuf):
        i = pl.program_id(0)
        dcur = d_ref[...]
        dbuf[0:tm, :] = dcur
        dbuf[tm:tm + HALO, :] = dn_ref[...] * jnp.where(i < nt - 1, 1.0, 0.0)
        av, sg = a_ref[...], _sigmoid(g_ref[...])
        vbuf[0:HALO, :] = ah_ref[...] * _sigmoid(gh_ref[...]) * jnp.where(i > 0, 1.0, 0.0)
        vbuf[HALO:HALO + tm, :] = av * sg
        dv = jnp.zeros((tm, C), F32)
        for k in range(CONV_K):
            dv = dv + w_ref[k:k + 1, :] * dbuf[pl.ds(CONV_K - 1 - k, tm), :]
        o_ref[:, 0:C] = (dv * sg).astype(BF16)
        o_ref[:, C:2 * C] = (dv * av * sg * (1.0 - sg)).astype(BF16)

        @pl.when(i == 0)
        def _():
            dw_ref[...] = jnp.zeros_like(dw_ref)

        for k in range(CONV_K):
            dw_ref[k:k + 1, :] += _colsum(dcur * vbuf[pl.ds(HALO - (CONV_K - 1) + k, tm), :])
        _accum(db_ref, _colsum(dcur), i)

    prev = lambda cb: pl.BlockSpec((HALO, C), lambda i: (jnp.maximum(i * hb - 1, 0), cb))
    return pl.pallas_call(
        body, name="conv_bwd", grid=(nt,),
        in_specs=[_row(C, tm=tm), pl.BlockSpec((HALO, C), lambda i: (jnp.minimum((i + 1) * hb, last), 0)),
                  _row(C, 0, tm), _row(C, 1, tm), prev(0), prev(1), pl.BlockSpec((HALO, C), lambda i: (0, 0))],
        out_specs=(_row(2 * C, tm=tm), pl.BlockSpec((HALO, C), lambda i: (0, 0)), _vec(C)),
        out_shape=(jax.ShapeDtypeStruct((L, 2 * C), BF16), jax.ShapeDtypeStruct((HALO, C), F32),
                   jax.ShapeDtypeStruct((1, C), F32)),
        scratch_shapes=[pltpu.VMEM((tm + HALO, C), F32), pltpu.VMEM((HALO + tm, C), F32)],
        compiler_params=_params(("arbitrary",)),
    )(dvc, dvc, proj, proj, proj, proj, w_dw)


def _adamw(w, g, m, v):
    m = ADAM_B1 * m + (1.0 - ADAM_B1) * g
    v = ADAM_B2 * v + (1.0 - ADAM_B2) * (g * g)
    m_hat = m / (1.0 - ADAM_B1 ** ADAM_STEP)
    v_hat = v / (1.0 - ADAM_B2 ** ADAM_STEP)
    delta = -ADAM_LR * (m_hat / (jnp.sqrt(v_hat) + ADAM_EPS) + ADAM_WD * w)
    return delta, m, v


def _tile_rows(R, C):
    tr = 8
    while tr * 2 * C <= 128 * 1024 and R % (tr * 2) == 0:
        tr *= 2
    assert R % tr == 0, (R, C)
    return tr


def sum_devices(parts, name):
    _, R, C = parts.shape
    tr = _tile_rows(R, C)

    def body(p_ref, o_ref):
        s = p_ref[0]
        for j in range(1, NDEV):
            s = s + p_ref[j]
        o_ref[...] = s

    return pl.pallas_call(
        body, name=name, grid=(R // tr,),
        in_specs=[pl.BlockSpec((NDEV, tr, C), lambda i: (0, i, 0))],
        out_specs=pl.BlockSpec((tr, C), lambda i: (i, 0)), out_shape=jax.ShapeDtypeStruct((R, C), F32),
        compiler_params=_params(("parallel",)),
    )(parts)


def adam_update(w, g, m, v, name):
    R, C = w.shape
    tr = _tile_rows(R, C)

    def body(w_ref, g_ref, m_ref, v_ref, d_ref, mo_ref, vo_ref):
        d, mm, vv = _adamw(w_ref[...], g_ref[...], m_ref[...], v_ref[...])
        d_ref[...], mo_ref[...], vo_ref[...] = d, mm, vv

    spec = pl.BlockSpec((tr, C), lambda i: (i, 0))
    return pl.pallas_call(
        body, name=name, grid=(R // tr,), in_specs=[spec] * 4, out_specs=(spec,) * 3,
        out_shape=(jax.ShapeDtypeStruct((R, C), F32),) * 3, compiler_params=_params(("parallel",)),
    )(w, g, m, v)


def adam_reduce(parts, w, m, v, name):
    R, C = w.shape
    tr = _tile_rows(R, C)

    def body(p_ref, w_ref, m_ref, v_ref, g_ref, d_ref, mo_ref, vo_ref):
        g = p_ref[0].astype(F32)
        for j in range(1, NDEV):
            g = g + p_ref[j].astype(F32)
        g_ref[...] = g
        d, mm, vv = _adamw(w_ref[...], g, m_ref[...], v_ref[...])
        d_ref[...], mo_ref[...], vo_ref[...] = d, mm, vv

    spec = pl.BlockSpec((tr, C), lambda i: (i, 0))
    return pl.pallas_call(
        body, name=name, grid=(R // tr,),
        in_specs=[pl.BlockSpec((NDEV, tr, C), lambda i: (0, i, 0)), spec, spec, spec], out_specs=(spec,) * 4,
        out_shape=(jax.ShapeDtypeStruct((R, C), F32),) * 4, compiler_params=_params(("parallel",)),
    )(parts, w, m, v)


def adam_w_ada(c_act, dmod_cols, w, m, v):
    D, n = w.shape
    tn = 256

    def body(c_ref, dm_ref, w_ref, m_ref, v_ref, g_ref, d_ref, mo_ref, vo_ref):
        g = lax.dot_general(c_ref[...].astype(BF16), dm_ref[...].astype(BF16), (((0,), (0,)), ((), ())),
                            preferred_element_type=F32)
        g_ref[...] = g
        d, mm, vv = _adamw(w_ref[...], g, m_ref[...], v_ref[...])
        d_ref[...], mo_ref[...], vo_ref[...] = d, mm, vv

    spec = pl.BlockSpec((D, tn), lambda j: (0, j))
    return pl.pallas_call(
        body, name="adam_w_ada", grid=(n // tn,),
        in_specs=[pl.BlockSpec((NDEV, D), lambda j: (0, 0)), pl.BlockSpec((NDEV, tn), lambda j: (0, j)), spec, spec, spec],
        out_specs=(spec,) * 4, out_shape=(jax.ShapeDtypeStruct((D, n), F32),) * 4,
        compiler_params=_params(("parallel",)),
    )(c_act, dmod_cols, w, m, v)


def _block_diag(m):
    G, a, b = m.shape
    m4 = m.reshape(G // GB, GB, a, b)
    eye = jnp.eye(GB, dtype=m.dtype)
    return (m4[:, :, :, None, :] * eye[None, :, None, :, None]).reshape(G // GB, GB * a, GB * b)


def _diag_blocks(m, a, b):
    nb = m.shape[0]
    m5 = m.reshape(nb, GB, a, GB, b)
    idx = jnp.arange(GB)
    return m5[:, idx, :, idx, :].transpose(1, 0, 2, 3).reshape(nb * GB, a, b)


def _flat_pad(parts, mult):
    flat = jnp.concatenate([p.reshape(-1) for p in parts])
    pad = (-flat.shape[0]) % mult
    return jnp.pad(flat, (0, pad))


def _split(flat, like):
    out, off = [], 0
    for p in like:
        out.append(flat[off:off + p.size].reshape(p.shape))
        off += p.size
    return out


def kernel(x, c, w_ada, b_ada, norm1_g, w_in, w_dw, b_dw, ln_g, ln_b, w_conv_out, a_re, a_im, log_dt, b_re, b_im, c_re, c_im, d_skip, w_glu_a, w_glu_b, w_out, norm2_g, w_ff1, w_ff2, final_g, loss_target, m_w_ada, m_b_ada, m_norm1_g, m_w_in, m_w_dw, m_b_dw, m_ln_g, m_ln_b, m_w_conv_out, m_a_re, m_a_im, m_log_dt, m_b_re, m_b_im, m_c_re, m_c_im, m_d_skip, m_w_glu_a, m_w_glu_b, m_w_out, m_norm2_g, m_w_ff1, m_w_ff2, m_final_g, v_w_ada, v_b_ada, v_norm1_g, v_w_in, v_w_dw, v_b_dw, v_ln_g, v_ln_b, v_w_conv_out, v_a_re, v_a_im, v_log_dt, v_b_re, v_b_im, v_c_re, v_c_im, v_d_skip, v_w_glu_a, v_w_glu_b, v_w_out, v_norm2_g, v_w_ff1, v_w_ff2, v_final_g):
    W = dict(w_ada=w_ada, b_ada=b_ada, norm1_g=norm1_g, w_in=w_in, w_dw=w_dw, b_dw=b_dw, ln_g=ln_g, ln_b=ln_b,
             w_conv_out=w_conv_out, a_re=a_re, a_im=a_im, log_dt=log_dt, b_re=b_re, b_im=b_im, c_re=c_re, c_im=c_im,
             d_skip=d_skip, w_glu_a=w_glu_a, w_glu_b=w_glu_b, w_out=w_out, norm2_g=norm2_g, w_ff1=w_ff1, w_ff2=w_ff2,
             final_g=final_g)
    Mo = dict(w_ada=m_w_ada, b_ada=m_b_ada, norm1_g=m_norm1_g, w_in=m_w_in, w_dw=m_w_dw, b_dw=m_b_dw, ln_g=m_ln_g,
              ln_b=m_ln_b, w_conv_out=m_w_conv_out, a_re=m_a_re, a_im=m_a_im, log_dt=m_log_dt, b_re=m_b_re, b_im=m_b_im,
              c_re=m_c_re, c_im=m_c_im, d_skip=m_d_skip, w_glu_a=m_w_glu_a, w_glu_b=m_w_glu_b, w_out=m_w_out,
              norm2_g=m_norm2_g, w_ff1=m_w_ff1, w_ff2=m_w_ff2, final_g=m_final_g)
    Vo = dict(w_ada=v_w_ada, b_ada=v_b_ada, norm1_g=v_norm1_g, w_in=v_w_in, w_dw=v_w_dw, b_dw=v_b_dw, ln_g=v_ln_g,
              ln_b=v_ln_b, w_conv_out=v_w_conv_out, a_re=v_a_re, a_im=v_a_im, log_dt=v_log_dt, b_re=v_b_re, b_im=v_b_im,
              c_re=v_c_re, c_im=v_c_im, d_skip=v_d_skip, w_glu_a=v_w_glu_a, w_glu_b=v_w_glu_b, w_out=v_w_out,
              norm2_g=v_norm2_g, w_ff1=v_w_ff1, w_ff2=v_w_ff2, final_g=v_final_g)
    names = list(W)

    me = _me()
    xs, tgt = x[0], loss_target[0]
    L, D = xs.shape
    CW = w_dw.shape[2] * NDEV
    G, P = a_re.shape[1], a_re.shape[2]
    H = b_re.shape[3]
    n_ada = w_ada.shape[2]

    w_dw_pad = jnp.pad(w_dw[0], ((0, HALO - CONV_K), (0, 0)))
    c_all, w_dw_all = _exchange([c, w_dw_pad], "gather_small", True)
    c_all = c_all.reshape(NDEV, D)
    w_dw_full = w_dw_all.transpose(1, 0, 2).reshape(HALO, CW)
    big = ["w_in", "w_conv_out", "w_glu_a", "w_glu_b", "w_out", "w_ff1", "w_ff2"]
    handles, _ = exchange_start([W[k][0].astype(BF16) for k in big], "gather_weights_start", True)
    gather_handle = dict(zip(big, handles))

    def weight(k, after):
        w = exchange_wait(gather_handle[k], after, "gather_wait_" + k, True)
        if k in ("w_out", "w_ff2"):
            w = w.reshape(1, w.shape[0] * w.shape[1], w.shape[2])
        return w

    scatter_handle = {}

    def scatter(k, g):
        (scatter_handle[k],), _ = exchange_start([g], "scatter_start_" + k, False)

    b_cols = lax.dynamic_slice(b_ada, (0, me * n_ada), (1, n_ada))
    mod_cols, c_act = adaln_mod(c_all, w_ada[0], b_cols)
    (mod_all,) = _exchange([mod_cols], "gather_mod", True)
    mod = lax.dynamic_slice(mod_all, (0, me, 0), (NDEV, 1, n_ada)).reshape(6, 1, D)
    shift1, scale1, gate1, shift2, scale2, gate2 = [mod[j] for j in range(6)]

    u = prenorm(xs, norm1_g, scale1, shift1, "prenorm1")
    wg = {"w_in": weight("w_in", u)}
    proj = mm_nn(u, wg["w_in"], "in_proj")
    vs, vc = conv_fwd(proj, w_dw_full, b_dw, ln_g, ln_b)
    wg["w_conv_out"] = weight("w_conv_out", vs)
    y_conv = mm_nn(vs, wg["w_conv_out"], "conv_out")

    br2 = b_re[0].transpose(0, 2, 1).reshape(G * H, P)
    bi2 = b_im[0].transpose(0, 2, 1).reshape(G * H, P)
    ldt = log_dt[0].reshape(G, 1)
    expand = jnp.repeat(jnp.eye(G, dtype=F32), H, axis=0)
    lbr, lbi, bbr, bbi = s5_params(a_re[0], a_im[0], ldt, br2, bi2, expand)
    tabs = s5_tables(lbr.reshape(1, G * P), lbi.reshape(1, G * P))
    bdr, bdi = _block_diag(bbr.reshape(G, H, P)), _block_diag(bbi.reshape(G, H, P))
    cdr = _block_diag(c_re[0].transpose(0, 2, 1))
    cdi = _block_diag(c_im[0].transpose(0, 2, 1))
    s_re, s_im, y_pre, yg = s5_fwd(proj, 2 * CW, bdr, bdi, cdr, cdi, tabs, d_skip)
    wg["w_glu_a"] = weight("w_glu_a", yg)
    wg["w_glu_b"] = weight("w_glu_b", yg)
    ga = mm_nn(yg, wg["w_glu_a"], "glu_a")
    gb = mm_nn(yg, wg["w_glu_b"], "glu_b")
    merged = merge_fwd(proj, 3 * CW, y_conv, ga, gb)
    wg["w_out"] = weight("w_out", merged)
    m_out = mm_nn(merged, wg["w_out"], "out_proj")
    h1, z = residual_norm(xs, m_out, gate1, norm2_g, scale2, shift2)
    wg["w_ff1"] = weight("w_ff1", z)
    act = mm_nn(z, wg["w_ff1"], "ff1", out_dtype=BF16, epi=lambda r: jnp.square(jnp.maximum(r, 0.0)))
    wg["w_ff2"] = weight("w_ff2", act)
    ff = mm_nn(act, wg["w_ff2"], "ff2")

    dh2, dff, loss_part, d_final_g, d_gate2 = loss_bwd(h1, ff, gate2, final_g.reshape(1, D), tgt)
    df = mm_nt(dff, wg["w_ff2"], "ff2_dx", out_dtype=BF16,
               epi=lambda r, a: r * (2.0 * jnp.sqrt(a.astype(F32))), extras=(act,))
    scatter("w_ff2", mm_tn(act, dff, 1, "ff2_dw", out_dtype=BF16).reshape(NDEV, -1, D))
    dz = mm_nt(df, wg["w_ff1"], "ff1_dx")
    scatter("w_ff1", mm_tn(z, df, NDEV, "ff1_dw", out_dtype=BF16))
    dh1, d_scale2, d_shift2, d_norm2_g, dmo, d_gate1 = norm_bwd(dz, h1, dh2, norm2_g, scale2, "norm2_bwd", gate1, m_out)
    dmerged = mm_nt(dmo, wg["w_out"], "out_dx")
    scatter("w_out", mm_tn(merged, dmo, 1, "out_dw", out_dtype=BF16).reshape(NDEV, -1, D))
    dyc, dga, dgb, dproj_g = merge_bwd(dmerged, proj, 3 * CW, y_conv, ga, gb)
    dvs = mm_nt(dyc, wg["w_conv_out"], "conv_out_dx")
    scatter("w_conv_out", mm_tn(vs, dyc, NDEV, "conv_out_dw", out_dtype=BF16))
    dyg_a = mm_nt(dga, wg["w_glu_a"], "glu_a_dx")
    dyg_b = mm_nt(dgb, wg["w_glu_b"], "glu_b_dx")
    scatter("w_glu_a", mm_tn(yg, dga, NDEV, "glu_a_dw", out_dtype=BF16))
    scatter("w_glu_b", mm_tn(yg, dgb, NDEV, "glu_b_dw", out_dtype=BF16))
    dvc, d_ln_g, d_ln_b = conv_ln_bwd(dvs, vc, ln_g, ln_b)
    dproj_c, d_w_dw, d_b_dw = conv_bwd(dvc, proj, w_dw_full)
    bdrt, bdit = bdr.transpose(0, 2, 1), bdi.transpose(0, 2, 1)
    cdrt, cdit = cdr.transpose(0, 2, 1), cdi.transpose(0, 2, 1)
    dproj_s, d_d_skip, dcdr, dcdi, dbdr, dbdi, dlr8, dli8 = s5_bwd(
        dyg_a, dyg_b, y_pre, proj, 2 * CW, s_re, s_im, bdrt, bdit, cdrt, cdit, tabs, d_skip)
    dproj = jnp.concatenate([dproj_c, dproj_s, dproj_g], axis=1)
    du = mm_nt(dproj, wg["w_in"], "in_dx")
    scatter("w_in", mm_tn(u, dproj, NDEV, "in_dw", out_dtype=BF16))
    grad_x, d_scale1, d_shift1, d_norm1_g = norm_bwd(du, xs, dh1, norm1_g, scale1, "norm1_bwd")

    dmod = jnp.concatenate([d_shift1, d_scale1, d_gate1, d_shift2, d_scale2, d_gate2], axis=1)
    d_c_re = _diag_blocks(dcdr, H, P)
    d_c_im = _diag_blocks(dcdi, H, P)
    d_bbr = _diag_blocks(dbdr, H, P)
    d_bbi = _diag_blocks(dbdi, H, P)
    dlr = jnp.sum(dlr8, axis=0).reshape(G, P)
    dli = jnp.sum(dli8, axis=0).reshape(G, P)
    small_parts = [dmod, d_norm1_g, d_b_dw, d_ln_g, d_ln_b, dlr, dli, d_bbr, d_bbi, d_c_re, d_c_im, d_d_skip,
                   d_norm2_g, d_final_g, d_w_dw]
    pack = _flat_pad(small_parts, PACK).reshape(-1, 1024)
    (pack_all,) = _exchange([pack], "gather_small_grads", True)
    tot = sum_devices(pack_all, "sum_small_grads").reshape(-1)
    (g_b_ada, g_norm1_g, g_b_dw, g_ln_g, g_ln_b, t_lr, t_li, t_bbr, t_bbi, g_c_re_t, g_c_im_t, g_d_skip,
     g_norm2_g, g_final_g, g_w_dw_full) = _split(tot, small_parts)
    g_a_re, g_a_im, g_ldt, g_br2, g_bi2 = s5_params_bwd(
        a_re[0], a_im[0], ldt, br2, bi2, expand, t_lr, t_li, t_bbr.reshape(G * H, P), t_bbi.reshape(G * H, P))
    g_brt, g_bit = g_br2.reshape(G, H, P), g_bi2.reshape(G, H, P)
    dmod_all = pack_all[:, :, :].reshape(NDEV, -1)[:, :6 * D]
    dmod_cols = lax.dynamic_slice(dmod_all, (0, me * n_ada), (NDEV, n_ada))

    grads = {
        "b_ada": g_b_ada, "norm1_g": g_norm1_g, "b_dw": g_b_dw, "ln_g": g_ln_g, "ln_b": g_ln_b,
        "a_re": g_a_re[None], "a_im": g_a_im[None], "log_dt": g_ldt.reshape(1, G),
        "b_re": g_brt.transpose(0, 2, 1)[None], "b_im": g_bit.transpose(0, 2, 1)[None],
        "c_re": g_c_re_t[None], "c_im": g_c_im_t[None], "d_skip": g_d_skip, "norm2_g": g_norm2_g,
        "final_g": g_final_g.reshape(D),
        "w_dw": lax.dynamic_slice(g_w_dw_full, (0, me * (CW // NDEV)), (CONV_K, CW // NDEV))[None],
    }
    small = [k for k in names if k in grads]
    wf = _flat_pad([W[k] for k in small], PACK).reshape(-1, 1024)
    gf = _flat_pad([grads[k] for k in small], PACK).reshape(-1, 1024)
    mf = _flat_pad([Mo[k] for k in small], PACK).reshape(-1, 1024)
    vf = _flat_pad([Vo[k] for k in small], PACK).reshape(-1, 1024)
    d_s, m_s, v_s = adam_update(wf, gf, mf, vf, "adam_small")
    like = [W[k] for k in small]
    delta = dict(zip(small, _split(d_s.reshape(-1), like)))
    new_m = dict(zip(small, _split(m_s.reshape(-1), like)))
    new_v = dict(zip(small, _split(v_s.reshape(-1), like)))

    g, d, mm, vv = adam_w_ada(c_act, dmod_cols, w_ada[0], m_w_ada[0], v_w_ada[0])
    grads["w_ada"], delta["w_ada"], new_m["w_ada"], new_v["w_ada"] = g[None], d[None], mm[None], vv[None]

    after = d
    for k in ("w_ff2", "w_ff1", "w_out", "w_conv_out", "w_glu_a", "w_glu_b", "w_in"):
        parts = exchange_wait(scatter_handle[k], after, "scatter_wait_" + k, False)
        g, d, mm, vv = adam_reduce(parts, W[k][0], Mo[k][0], Vo[k][0], "adam_" + k)
        grads[k], delta[k], new_m[k], new_v[k] = g[None], d[None], mm[None], vv[None]
        after = d

    loss = lax.psum(loss_part[0, 0], ("x", "y", "c"))
    return (loss, grad_x[None], *[grads[k] for k in names], *[delta[k] for k in names],
            *[new_m[k] for k in names], *[new_v[k] for k in names])
```

```python
import functools
import math

import jax
import jax.numpy as jnp
from jax import lax
from jax.experimental import pallas as pl
from jax.experimental.pallas import tpu as pltpu

F32 = jnp.float32
BF16 = jnp.bfloat16
NDEV = 8
EPS = 1e-6
ADAM_LR, ADAM_B1, ADAM_B2, ADAM_EPS, ADAM_WD, ADAM_STEP = 0.001, 0.9, 0.999, 1e-08, 0.01, 10
CONV_K = 31
HALO = 32
GROUP = 16
STATE = 64
GB = 8
HI = lax.Precision.HIGHEST
MESH = pl.DeviceIdType.MESH
VMEM_LIMIT = 56 * 1024 * 1024
PACK_ROWS = 64
PACK = PACK_ROWS * 1024
ANY = pl.BlockSpec(memory_space=pl.ANY)


def _params(sem=None):
    if sem is None:
        return pltpu.CompilerParams(vmem_limit_bytes=VMEM_LIMIT)
    return pltpu.CompilerParams(dimension_semantics=sem, vmem_limit_bytes=VMEM_LIMIT)


def _sigmoid(v):
    return 1.0 / (1.0 + jnp.exp(-v))


def _me():
    return 4 * lax.axis_index("x") + 2 * lax.axis_index("y") + lax.axis_index("c")


def _peer(k):
    x, y, c = lax.axis_index("x"), lax.axis_index("y"), lax.axis_index("c")
    px = 1 - x if (k >> 2) & 1 else x
    py = 1 - y if (k >> 1) & 1 else y
    pc = 1 - c if k & 1 else c
    return (px, py, pc), 4 * px + 2 * py + pc


def _exchange(arrays, name, gather):
    n = len(arrays)
    out_shape = []
    for a in arrays:
        shp = (NDEV,) + a.shape if gather else a.shape
        out_shape.append(jax.ShapeDtypeStruct(shp, a.dtype))

    def body(*refs):
        ins, outs = refs[:n], refs[n:2 * n]
        send, recv, lsem = refs[2 * n:]
        me = _me()
        local = []
        for a in range(n):
            src = ins[a] if gather else ins[a].at[me]
            cp = pltpu.make_async_copy(src, outs[a].at[me], lsem.at[a])
            cp.start()
            local.append(cp)
        sends = []
        for a in range(n):
            for k in range(1, NDEV):
                dev, pidx = _peer(k)
                src = ins[a] if gather else ins[a].at[pidx]
                cp = pltpu.make_async_remote_copy(
                    src_ref=src, dst_ref=outs[a].at[me], send_sem=send.at[a * (NDEV - 1) + k - 1], recv_sem=recv.at[a * (NDEV - 1) + k - 1],
                    device_id=dev, device_id_type=MESH)
                cp.start()
                sends.append(cp)
        for a in range(n):
            for k in range(1, NDEV):
                dev, pidx = _peer(k)
                src = ins[a] if gather else ins[a].at[pidx]
                pltpu.make_async_remote_copy(
                    src_ref=src, dst_ref=outs[a].at[pidx], send_sem=send.at[a * (NDEV - 1) + k - 1], recv_sem=recv.at[a * (NDEV - 1) + k - 1],
                    device_id=dev, device_id_type=MESH).wait_recv()
        for cp in sends:
            cp.wait_send()
        for cp in local:
            cp.wait()

    return pl.pallas_call(
        body, name=name, out_shape=tuple(out_shape),
        in_specs=[ANY] * n, out_specs=tuple([ANY] * n),
        scratch_shapes=[pltpu.SemaphoreType.DMA((n * (NDEV - 1),)), pltpu.SemaphoreType.DMA((n * (NDEV - 1),)),
                        pltpu.SemaphoreType.DMA((n,))],
    )(*arrays)


HBM = pl.BlockSpec(memory_space=pltpu.HBM)
SEM = pl.BlockSpec(memory_space=pltpu.SEMAPHORE)
EFFECT = pltpu.SideEffectType.DATAFLOW_SIDE_EFFECTING
NPEER = NDEV - 1


def _landing(block_of_me, shape, dtype):
    land = lax.empty((NDEV,) + tuple(shape), dtype)
    start = (_me(),) + (0,) * len(shape)
    return pltpu.with_memory_space_constraint(lax.dynamic_update_slice(land, block_of_me[None], start), pltpu.HBM)


def exchange_start(arrays, name, gather):
    n = len(arrays)
    me = _me()
    lands = []
    for a in arrays:
        if gather:
            lands.append(_landing(a, a.shape, a.dtype))
        else:
            mine = lax.dynamic_slice(a, (me,) + (0,) * (a.ndim - 1), (1,) + a.shape[1:])[0]
            lands.append(_landing(mine, a.shape[1:], a.dtype))
    srcs = [pltpu.with_memory_space_constraint(a, pltpu.HBM) for a in arrays]

    def body(*refs):
        ins, lnd = refs[:n], refs[n:2 * n]
        outs = refs[2 * n:]
        sends, recvs, token = outs[:n], outs[n:2 * n], outs[-1]
        my = _me()
        for a in range(n):
            for k in range(1, NDEV):
                dev, pidx = _peer(k)
                src = ins[a] if gather else ins[a].at[pidx]
                pltpu.make_async_remote_copy(
                    src_ref=src, dst_ref=lnd[a].at[my], send_sem=sends[a].at[k - 1], recv_sem=recvs[a].at[k - 1],
                    device_id=dev, device_id_type=MESH).start()
        token[...] = jnp.zeros_like(token)

    out_shape = ([pltpu.SemaphoreType.DMA((NPEER,))] * (2 * n)
                 + [pltpu.HBM(a.shape, a.dtype) for a in srcs] + [pltpu.HBM(l.shape, l.dtype) for l in lands]
                 + [jax.ShapeDtypeStruct((8, 128), F32)])
    res = pl.pallas_call(
        body, name=name, out_shape=tuple(out_shape),
        in_specs=[HBM] * (2 * n), out_specs=tuple([SEM] * (2 * n) + [HBM] * (2 * n) + [pl.BlockSpec(memory_space=pltpu.VMEM)]),
        input_output_aliases={i: 2 * n + i for i in range(2 * n)},
        compiler_params=pltpu.CompilerParams(has_side_effects=EFFECT),
    )(*srcs, *lands)
    handles = [(res[a], res[n + a], res[2 * n + a], res[3 * n + a]) for a in range(n)]
    return handles, res[-1]


def exchange_wait(handle, after, name, gather):
    send_sem, recv_sem, src, land = handle

    def body(src_ref, land_ref, s_ref, r_ref, after_ref, src_out, land_out):
        for k in range(1, NDEV):
            dev, pidx = _peer(k)
            s = src_ref if gather else src_ref.at[pidx]
            cp = pltpu.make_async_remote_copy(
                src_ref=s, dst_ref=land_ref.at[pidx], send_sem=s_ref.at[k - 1], recv_sem=r_ref.at[k - 1],
                device_id=dev, device_id_type=MESH)
            cp.wait_send()
            cp.wait_recv()

    return pl.pallas_call(
        body, name=name, out_shape=(pltpu.HBM(src.shape, src.dtype), pltpu.HBM(land.shape, land.dtype)),
        in_specs=(HBM, HBM, SEM, SEM, ANY), out_specs=(HBM, HBM), input_output_aliases={0: 0, 1: 1},
        compiler_params=pltpu.CompilerParams(has_side_effects=EFFECT),
    )(src, land, send_sem, recv_sem, after)[1]


def _acc_steps(p, acc, k, nk, finish):
    if nk == 1:
        finish(p)
        return

    @pl.when(k == 0)
    def _():
        acc[...] = p

    @pl.when(k > 0)
    def _():
        acc[...] += p

    @pl.when(k == nk - 1)
    def _():
        finish(acc[...])


def mm_nn(a, w3, name, out_dtype=F32, epi=None, extras=()):
    M, K = a.shape
    J, _, n = w3.shape
    tm, tn, tk = min(1024, M), min(1024, n), min(2048, K)
    q, nk, ne = n // tn, K // tk, len(extras)

    def body(*refs):
        a_ref, w_ref = refs[:2]
        ex, o_ref, acc = refs[2:2 + ne], refs[2 + ne], refs[-1]
        p = jnp.dot(a_ref[...], w_ref[...], preferred_element_type=F32)

        def finish(r):
            if epi is not None:
                r = epi(r, *[e[...] for e in ex])
            o_ref[...] = r.astype(out_dtype)

        _acc_steps(p, acc, pl.program_id(2), nk, finish)

    return pl.pallas_call(
        body, name=name, grid=(M // tm, J * q, nk),
        in_specs=[pl.BlockSpec((tm, tk), lambda i, j, k: (i, k)),
                  pl.BlockSpec((None, tk, tn), lambda i, j, k: (j // q, k, j % q))]
        + [pl.BlockSpec((tm, tn), lambda i, j, k: (i, j))] * ne,
        out_specs=pl.BlockSpec((tm, tn), lambda i, j, k: (i, j)),
        out_shape=jax.ShapeDtypeStruct((M, J * n), out_dtype),
        scratch_shapes=[pltpu.VMEM((tm, tn), F32)],
        compiler_params=_params(("parallel", "parallel", "arbitrary")),
    )(a, w3, *extras)


def mm_nt(dy, w3, name, out_dtype=F32, epi=None, extras=()):
    M, _ = dy.shape
    J, K, n = w3.shape
    tm, tn, tkk = min(1024, M), min(1024, n), min(1024, K)
    q, ne = n // tn, len(extras)
    nk = J * q

    def body(*refs):
        d_ref, w_ref = refs[:2]
        ex, o_ref, acc = refs[2:2 + ne], refs[2 + ne], refs[-1]
        p = lax.dot_general(d_ref[...], w_ref[...], (((1,), (1,)), ((), ())), preferred_element_type=F32)

        def finish(r):
            if epi is not None:
                r = epi(r, *[e[...] for e in ex])
            o_ref[...] = r.astype(out_dtype)

        _acc_steps(p, acc, pl.program_id(2), nk, finish)

    return pl.pallas_call(
        body, name=name, grid=(M // tm, K // tkk, nk),
        in_specs=[pl.BlockSpec((tm, tn), lambda i, kk, c: (i, c)),
                  pl.BlockSpec((None, tkk, tn), lambda i, kk, c: (c // q, kk, c % q))]
        + [pl.BlockSpec((tm, tkk), lambda i, kk, c: (i, kk))] * ne,
        out_specs=pl.BlockSpec((tm, tkk), lambda i, kk, c: (i, kk)),
        out_shape=jax.ShapeDtypeStruct((M, K), out_dtype),
        scratch_shapes=[pltpu.VMEM((tm, tkk), F32)],
        compiler_params=_params(("parallel", "parallel", "arbitrary")),
    )(dy, w3, *extras)


def mm_tn(a, dy, J, name, out_dtype=F32):
    M, K = a.shape
    n = dy.shape[1] // J
    tm, tn, tkk = min(1024, M), min(1024, n), min(1024, K)
    q, nk = n // tn, M // tm

    def body(a_ref, d_ref, o_ref, acc):
        p = lax.dot_general(a_ref[...], d_ref[...], (((0,), (0,)), ((), ())), preferred_element_type=F32)

        def finish(r):
            o_ref[...] = r.astype(out_dtype)

        _acc_steps(p, acc, pl.program_id(2), nk, finish)

    return pl.pallas_call(
        body, name=name, grid=(K // tkk, J * q, nk),
        in_specs=[pl.BlockSpec((tm, tkk), lambda kk, c, m: (m, kk)),
                  pl.BlockSpec((tm, tn), lambda kk, c, m: (m, c))],
        out_specs=pl.BlockSpec((None, tkk, tn), lambda kk, c, m: (c // q, kk, c % q)),
        out_shape=jax.ShapeDtypeStruct((J, K, n), out_dtype),
        scratch_shapes=[pltpu.VMEM((tkk, tn), F32)],
        compiler_params=_params(("parallel", "parallel", "arbitrary")),
    )(a, dy)


def _tm(L):
    return min(256, L)


def _row(w, cb=0, tm=None):
    return pl.BlockSpec((tm, w), lambda i: (i, cb))


def _vec(w, cb=0):
    return pl.BlockSpec((1, w), lambda i: (0, cb))


def _accum(ref, val, i):
    @pl.when(i == 0)
    def _():
        ref[...] = val

    @pl.when(i > 0)
    def _():
        ref[...] += val


def _colsum(v):
    return jnp.sum(v, axis=0, keepdims=True)


def _rms(v):
    return lax.rsqrt(jnp.mean(v * v, axis=-1, keepdims=True) + EPS)


def adaln_mod(c_all, w_ada, b_cols):
    B, D = c_all.shape
    n = w_ada.shape[1]
    tn = 512

    def body(c_ref, w_ref, b_ref, o_ref, ca_ref):
        cv = c_ref[...]
        ca = cv * _sigmoid(cv)
        ca_ref[...] = ca
        o_ref[...] = jnp.dot(ca.astype(BF16), w_ref[...].astype(BF16), preferred_element_type=F32) + b_ref[...]

    return pl.pallas_call(
        body, name="adaln_mod", grid=(n // tn,),
        in_specs=[pl.BlockSpec((B, D), lambda j: (0, 0)), pl.BlockSpec((D, tn), lambda j: (0, j)),
                  pl.BlockSpec((1, tn), lambda j: (0, j))],
        out_specs=(pl.BlockSpec((B, tn), lambda j: (0, j)), pl.BlockSpec((B, D), lambda j: (0, 0))),
        out_shape=(jax.ShapeDtypeStruct((B, n), F32), jax.ShapeDtypeStruct((B, D), F32)),
        compiler_params=_params(("arbitrary",)),
    )(c_all, w_ada, b_cols)


def prenorm(x, g, scale, shift, name):
    L, D = x.shape
    tm = _tm(L)

    def body(x_ref, g_ref, sc_ref, sh_ref, u_ref):
        xv = x_ref[...]
        u_ref[...] = (xv * _rms(xv) * g_ref[...] * (1.0 + sc_ref[...]) + sh_ref[...]).astype(BF16)

    return pl.pallas_call(
        body, name=name, grid=(L // tm,),
        in_specs=[_row(D, tm=tm), _vec(D), _vec(D), _vec(D)],
        out_specs=_row(D, tm=tm), out_shape=jax.ShapeDtypeStruct((L, D), BF16),
        compiler_params=_params(("parallel",)),
    )(x, g, scale, shift)


def conv_fwd(proj, w_dw, b_dw, ln_g, ln_b):
    L = proj.shape[0]
    C = w_dw.shape[1]
    tm = _tm(L)
    hb = tm // HALO

    def body(a_ref, g_ref, ah_ref, gh_ref, w_ref, b_ref, lg_ref, lb_ref, vs_ref, vc_ref, buf):
        i = pl.program_id(0)
        halo = ah_ref[...] * _sigmoid(gh_ref[...])
        buf[0:HALO, :] = halo * jnp.where(i > 0, 1.0, 0.0)
        buf[HALO:HALO + tm, :] = a_ref[...] * _sigmoid(g_ref[...])
        acc = jnp.zeros((tm, C), F32) + b_ref[...]
        for k in range(CONV_K):
            acc = acc + w_ref[k:k + 1, :] * buf[pl.ds(HALO - (CONV_K - 1) + k, tm), :]
        vc_ref[...] = acc
        mu = jnp.mean(acc, axis=-1, keepdims=True)
        d = acc - mu
        var = jnp.mean(d * d, axis=-1, keepdims=True)
        ln = d * lax.rsqrt(var + EPS) * lg_ref[...] + lb_ref[...]
        vs_ref[...] = (ln * _sigmoid(ln)).astype(BF16)

    prev = lambda cb: pl.BlockSpec((HALO, C), lambda i: (jnp.maximum(i * hb - 1, 0), cb))
    return pl.pallas_call(
        body, name="conv_fwd", grid=(L // tm,),
        in_specs=[_row(C, 0, tm), _row(C, 1, tm), prev(0), prev(1),
                  pl.BlockSpec((HALO, C), lambda i: (0, 0)), _vec(C), _vec(C), _vec(C)],
        out_specs=(_row(C, tm=tm), _row(C, tm=tm)),
        out_shape=(jax.ShapeDtypeStruct((L, C), BF16), jax.ShapeDtypeStruct((L, C), F32)),
        scratch_shapes=[pltpu.VMEM((HALO + tm, C), F32)],
        compiler_params=_params(("parallel",)),
    )(proj, proj, proj, proj, w_dw, b_dw, ln_g, ln_b)


def _gelu(v):
    return 0.5 * v * (1.0 + jnp.tanh(math.sqrt(2.0 / math.pi) * (v + 0.044715 * v * v * v)))


def _gelu_grad(v):
    k = math.sqrt(2.0 / math.pi)
    t = jnp.tanh(k * (v + 0.044715 * v * v * v))
    return 0.5 * (1.0 + t) + 0.5 * v * (1.0 - t * t) * k * (1.0 + 3.0 * 0.044715 * v * v)


def s5_param_fn(ar, ai, ldt, br, bi, expand):
    dt = jnp.exp(ldt)
    er = jnp.exp(ar * dt)
    th = ai * dt
    lbr, lbi = er * jnp.cos(th), er * jnp.sin(th)
    nr, ni = lbr - 1.0, lbi
    den = ar * ar + ai * ai
    qr, qi = (nr * ar + ni * ai) / den, (ni * ar - nr * ai) / den
    qre = jnp.dot(expand, qr, precision=HI, preferred_element_type=F32)
    qie = jnp.dot(expand, qi, precision=HI, preferred_element_type=F32)
    return lbr, lbi, qre * br - qie * bi, qre * bi + qie * br


def s5_params(ar, ai, ldt, br2, bi2, expand):
    def body(ar_ref, ai_ref, ld_ref, br_ref, bi_ref, e_ref, o1, o2, o3, o4):
        r = s5_param_fn(ar_ref[...], ai_ref[...], ld_ref[...], br_ref[...], bi_ref[...], e_ref[...])
        o1[...], o2[...], o3[...], o4[...] = r

    s2, s3 = jax.ShapeDtypeStruct(ar.shape, F32), jax.ShapeDtypeStruct(br2.shape, F32)
    return pl.pallas_call(body, name="s5_params", out_shape=(s2, s2, s3, s3), compiler_params=_params())(
        ar, ai, ldt, br2, bi2, expand)


def s5_params_bwd(ar, ai, ldt, br2, bi2, expand, dlr, dli, dbr, dbi):
    def body(ar_ref, ai_ref, ld_ref, br_ref, bi_ref, e_ref, c1, c2, c3, c4, o1, o2, o3, o4, o5):
        e = e_ref[...]
        fn = lambda a, b, c, d, f: s5_param_fn(a, b, c, d, f, e)
        _, vjp = jax.vjp(fn, ar_ref[...], ai_ref[...], ld_ref[...], br_ref[...], bi_ref[...])
        r = vjp((c1[...], c2[...], c3[...], c4[...]))
        o1[...], o2[...], o3[...], o4[...], o5[...] = r

    shapes = tuple(jax.ShapeDtypeStruct(v.shape, F32) for v in (ar, ai, ldt, br2, bi2))
    return pl.pallas_call(body, name="s5_params_bwd", out_shape=shapes, compiler_params=_params())(
        ar, ai, ldt, br2, bi2, expand, dlr, dli, dbr, dbi)


def s5_tables(lr, li):
    C = lr.shape[1]

    def body(lr_ref, li_ref, o_ref):
        row = lax.broadcasted_iota(jnp.int32, (8, C), 0)
        for rev in (0, 1):
            pr = jnp.broadcast_to(lr_ref[...], (8, C))
            pi = jnp.broadcast_to(-li_ref[...] if rev else li_ref[...], (8, C))
            br, bi = pr, pi
            pows = [(pr, pi)]
            for _ in range(7):
                pr, pi = pr * br - pi * bi, pr * bi + pi * br
                pows.append((pr, pi))
            base = 8 * rev
            for s, d in enumerate((1, 2, 4)):
                keep = (row + d <= 7) if rev else (row >= d)
                o_ref[base + 2 * s] = jnp.where(keep, pows[d - 1][0], 0.0)
                o_ref[base + 2 * s + 1] = jnp.where(keep, pows[d - 1][1], 0.0)
            cr, ci = jnp.zeros((8, C), F32), jnp.zeros((8, C), F32)
            for j in range(8):
                e = (8 - j) if rev else (j + 1)
                cr = jnp.where(row == j, pows[e - 1][0], cr)
                ci = jnp.where(row == j, pows[e - 1][1], ci)
            o_ref[base + 6] = cr
            o_ref[base + 7] = ci

    return pl.pallas_call(body, name="s5_tables", out_shape=jax.ShapeDtypeStruct((16, 8, C), F32),
                          compiler_params=_params())(lr, li)


def _scan_tile(xr, xi, tabs, cr, ci, rev):
    for s, d in enumerate((1, 2, 4)):
        tr, ti = tabs[2 * s], tabs[2 * s + 1]
        sh = (8 - d) if rev else d
        sr, si = pltpu.roll(xr, sh, 0), pltpu.roll(xi, sh, 0)
        xr, xi = xr + tr * sr - ti * si, xi + tr * si + ti * sr
    tr, ti = tabs[6], tabs[7]
    xr, xi = xr + tr * cr - ti * ci, xi + tr * ci + ti * cr
    return xr, xi


def _hi_lo(a):
    hi = a.astype(BF16)
    return hi, (a - hi.astype(F32)).astype(BF16)


def _lhs3(a):
    hi, lo = _hi_lo(a)
    return jnp.concatenate([hi, lo, hi], axis=1)


def _rhs3(m):
    hi, lo = _hi_lo(m)
    return jnp.concatenate([hi, hi, lo], axis=-2)


def s5_fwd(proj, col0, bdr3, bdi3, cd2, tabs, d_skip):
    L = proj.shape[0]
    nb, cw3, sw = bdr3.shape
    cw = cw3 // 3
    tl = min(256, L)
    cb0 = col0 // cw

    def body(u_ref, bdr_ref, bdi_ref, cd_ref, t_ref, dk_ref, sr_ref, si_ref, yp_ref, yg_ref, car):
        l = pl.program_id(1)

        @pl.when(l == 0)
        def _():
            car[...] = jnp.zeros_like(car)

        u = u_ref[...]
        u3 = _lhs3(u)
        sr_ref[...] = jnp.dot(u3, bdr_ref[...], preferred_element_type=F32)
        si_ref[...] = jnp.dot(u3, bdi_ref[...], preferred_element_type=F32)

        def tile(i, c):
            tabs = [t_ref[j] for j in range(8)]
            r0 = pl.multiple_of(i * 8, 8)
            xr, xi = _scan_tile(sr_ref[pl.ds(r0, 8), :], si_ref[pl.ds(r0, 8), :], tabs, c[0], c[1], False)
            sr_ref[pl.ds(r0, 8), :] = xr
            si_ref[pl.ds(r0, 8), :] = xi
            return xr[7:8, :], xi[7:8, :]

        c = lax.fori_loop(0, tl // 8, tile, (car[0:1, :], car[1:2, :]))
        car[0:1, :] = c[0]
        car[1:2, :] = c[1]
        s2 = jnp.concatenate([sr_ref[...].astype(BF16), si_ref[...].astype(BF16)], axis=1)
        y = jnp.dot(s2, cd_ref[...], preferred_element_type=F32) + dk_ref[...] * u
        yp_ref[...] = y
        yg_ref[...] = _gelu(y).astype(BF16)

    blk = lambda r, c: pl.BlockSpec((None, r, c), lambda b, l: (b, 0, 0))
    return pl.pallas_call(
        body, name="s5_fwd", grid=(nb, L // tl),
        in_specs=[pl.BlockSpec((tl, cw), lambda b, l: (l, cb0 + b)), blk(cw3, sw), blk(cw3, sw), blk(2 * sw, cw),
                  pl.BlockSpec((8, 8, sw), lambda b, l: (0, 0, b)), pl.BlockSpec((1, cw), lambda b, l: (0, b))],
        out_specs=(pl.BlockSpec((tl, sw), lambda b, l: (l, b)), pl.BlockSpec((tl, sw), lambda b, l: (l, b)),
                   pl.BlockSpec((tl, cw), lambda b, l: (l, b)), pl.BlockSpec((tl, cw), lambda b, l: (l, b))),
        out_shape=(jax.ShapeDtypeStruct((L, nb * sw), F32), jax.ShapeDtypeStruct((L, nb * sw), F32),
                   jax.ShapeDtypeStruct((L, nb * cw), F32), jax.ShapeDtypeStruct((L, nb * cw), BF16)),
        scratch_shapes=[pltpu.VMEM((8, sw), F32)],
        compiler_params=_params(("parallel", "arbitrary")),
    )(proj, bdr3, bdi3, cd2, tabs, d_skip)


def s5_bwd(dyg_a, dyg_b, yp, proj, col0, s_re, s_im, bdt2, cdrt3, cdit3, tabs, d_skip):
    L = proj.shape[0]
    nb, sw2, cw = bdt2.shape
    sw = sw2 // 2
    tl = min(256, L)
    nl = L // tl
    cb0 = col0 // cw
    tb = tl // 8

    def body(da_ref, db_ref, yp_ref, u_ref, sr_ref, si_ref, hr_ref, hi_ref, bdt_ref, cdrt_ref, cdit_ref,
             t_ref, dk_ref, du_ref, ddk_ref, dcr_ref, dci_ref, dbr_ref, dbi_ref, dlr_ref, dli_ref,
             gr, gi, pr, pi, car):
        l = pl.program_id(1)
        first = l == nl - 1

        @pl.when(l == 0)
        def _():
            car[...] = jnp.zeros_like(car)

        u = u_ref[...]
        dy = (da_ref[...] + db_ref[...]) * _gelu_grad(yp_ref[...])
        dy3 = _lhs3(dy)
        gr[...] = jnp.dot(dy3, cdrt_ref[...], preferred_element_type=F32)
        gi[...] = jnp.dot(dy3, cdit_ref[...], preferred_element_type=F32)
        inner = jnp.where(first, 0.0, 1.0)
        pr[0:8, :] = hr_ref[...] * inner
        pi[0:8, :] = hi_ref[...] * inner
        pr[8:8 + tl, :] = sr_ref[...]
        pi[8:8 + tl, :] = si_ref[...]
        row = lax.broadcasted_iota(jnp.int32, (8, sw), 0)

        def tile(j, c):
            tabs = [t_ref[8 + k] for k in range(8)]
            r0 = pl.multiple_of((tb - 1 - j) * 8, 8)
            xr, xi = _scan_tile(gr[pl.ds(r0, 8), :], gi[pl.ds(r0, 8), :], tabs, c[0], c[1], True)
            gr[pl.ds(r0, 8), :] = xr
            gi[pl.ds(r0, 8), :] = xi
            qr = jnp.where(row == 0, pltpu.roll(pr[pl.ds(r0, 8), :], 1, 0), pltpu.roll(pr[pl.ds(r0 + 8, 8), :], 1, 0))
            qi = jnp.where(row == 0, pltpu.roll(pi[pl.ds(r0, 8), :], 1, 0), pltpu.roll(pi[pl.ds(r0 + 8, 8), :], 1, 0))
            return xr[0:1, :], xi[0:1, :], c[2] + xr * qr + xi * qi, c[3] + xi * qr - xr * qi

        z = jnp.zeros((8, sw), F32)
        c = lax.fori_loop(0, tb, tile, (car[0:1, :], car[1:2, :], z, z))
        car[0:1, :] = c[0]
        car[1:2, :] = c[1]
        g_re, g_im = gr[...].astype(BF16), gi[...].astype(BF16)
        g2 = jnp.concatenate([g_re, g_im], axis=1)
        du_ref[...] = (dy * dk_ref[...] + jnp.dot(g2, bdt_ref[...], preferred_element_type=F32)).astype(BF16)
        tn = (((0,), (0,)), ((), ()))
        dyb, ub = dy.astype(BF16), u.astype(BF16)
        _accum(ddk_ref, _colsum(dy * u), l)
        _accum(dcr_ref, lax.dot_general(dyb, sr_ref[...].astype(BF16), tn, preferred_element_type=F32), l)
        _accum(dci_ref, -lax.dot_general(dyb, si_ref[...].astype(BF16), tn, preferred_element_type=F32), l)
        _accum(dbr_ref, lax.dot_general(ub, g_re, tn, preferred_element_type=F32), l)
        _accum(dbi_ref, lax.dot_general(ub, g_im, tn, preferred_element_type=F32), l)
        _accum(dlr_ref, c[2], l)
        _accum(dli_ref, c[3], l)

    rl = lambda l: nl - 1 - l
    cblk = lambda w, off=0: pl.BlockSpec((tl, w), lambda b, l: (rl(l), off + b))
    halo = pl.BlockSpec((8, sw), lambda b, l: (jnp.maximum(rl(l) * tb - 1, 0), b))
    mat = lambda r, c: pl.BlockSpec((None, r, c), lambda b, l: (b, 0, 0))
    return pl.pallas_call(
        body, name="s5_bwd", grid=(nb, nl),
        in_specs=[cblk(cw), cblk(cw), cblk(cw), cblk(cw, cb0), cblk(sw), cblk(sw), halo, halo,
                  mat(2 * sw, cw), mat(3 * cw, sw), mat(3 * cw, sw),
                  pl.BlockSpec((16, 8, sw), lambda b, l: (0, 0, b)), pl.BlockSpec((1, cw), lambda b, l: (0, b))],
        out_specs=(cblk(cw), pl.BlockSpec((1, cw), lambda b, l: (0, b)), mat(cw, sw), mat(cw, sw), mat(cw, sw), mat(cw, sw),
                   pl.BlockSpec((8, sw), lambda b, l: (0, b)), pl.BlockSpec((8, sw), lambda b, l: (0, b))),
        out_shape=(jax.ShapeDtypeStruct((L, nb * cw), BF16), jax.ShapeDtypeStruct((1, nb * cw), F32),
                   jax.ShapeDtypeStruct((nb, cw, sw), F32), jax.ShapeDtypeStruct((nb, cw, sw), F32),
                   jax.ShapeDtypeStruct((nb, cw, sw), F32), jax.ShapeDtypeStruct((nb, cw, sw), F32),
                   jax.ShapeDtypeStruct((8, nb * sw), F32), jax.ShapeDtypeStruct((8, nb * sw), F32)),
        scratch_shapes=[pltpu.VMEM((tl, sw), F32), pltpu.VMEM((tl, sw), F32),
                        pltpu.VMEM((tl + 8, sw), F32), pltpu.VMEM((tl + 8, sw), F32), pltpu.VMEM((8, sw), F32)],
        compiler_params=_params(("parallel", "arbitrary")),
    )(dyg_a, dyg_b, yp, proj, s_re, s_im, s_re, s_im, bdt2, cdrt3, cdit3, tabs, d_skip)


def merge_fwd(proj, col_gc, y_conv, ga, gb):
    L, D = y_conv.shape
    tm = _tm(L)
    h = D // 2
    c0 = col_gc // h

    def body(p0, p1, p2, p3, yc_ref, ga_ref, gb_ref, o_ref):
        gc, gs = (p0, p1), (p2, p3)
        for s in range(2):
            cols = slice(s * h, (s + 1) * h)
            y_ssm = ga_ref[:, cols] * _sigmoid(gb_ref[:, cols])
            o_ref[:, cols] = (_sigmoid(gc[s][...]) * yc_ref[:, cols] + _sigmoid(gs[s][...]) * y_ssm).astype(BF16)

    return pl.pallas_call(
        body, name="merge_fwd", grid=(L // tm,),
        in_specs=[_row(h, c0 + s, tm) for s in range(4)] + [_row(D, tm=tm)] * 3,
        out_specs=_row(D, tm=tm), out_shape=jax.ShapeDtypeStruct((L, D), BF16),
        compiler_params=_params(("parallel",)),
    )(proj, proj, proj, proj, y_conv, ga, gb)


def residual_norm(x, m_out, gate, g, scale, shift):
    L, D = x.shape
    tm = _tm(L)

    def body(x_ref, m_ref, gt_ref, g_ref, sc_ref, sh_ref, h_ref, z_ref):
        h = x_ref[...] + gt_ref[...] * m_ref[...]
        h_ref[...] = h
        z_ref[...] = (h * _rms(h) * g_ref[...] * (1.0 + sc_ref[...]) + sh_ref[...]).astype(BF16)

    return pl.pallas_call(
        body, name="residual_norm", grid=(L // tm,),
        in_specs=[_row(D, tm=tm), _row(D, tm=tm), _vec(D), _vec(D), _vec(D), _vec(D)],
        out_specs=(_row(D, tm=tm), _row(D, tm=tm)),
        out_shape=(jax.ShapeDtypeStruct((L, D), F32), jax.ShapeDtypeStruct((L, D), BF16)),
        compiler_params=_params(("parallel",)),
    )(x, m_out, gate, g, scale, shift)


def loss_bwd(h1, ff, gate2, final_g, target):
    L, D = h1.shape
    tm = _tm(L)

    def body(h_ref, f_ref, gt_ref, g_ref, t_ref, dh_ref, dff_ref, loss_ref, dg_ref, dgt_ref):
        i = pl.program_id(0)
        ffv = f_ref[...]
        h2 = h_ref[...] + gt_ref[...] * ffv
        r = _rms(h2)
        n = h2 * r
        err = n * g_ref[...] - t_ref[...]
        per_tok = jnp.mean(err * err, axis=-1, keepdims=True)
        _accum(loss_ref, 0.5 * jnp.sum(per_tok, axis=0, keepdims=True), i)
        dy = err * (1.0 / D)
        _accum(dg_ref, _colsum(dy * n), i)
        dn = dy * g_ref[...]
        dh2 = r * (dn - n * jnp.mean(dn * n, axis=-1, keepdims=True))
        dh_ref[...] = dh2
        dff_ref[...] = (gt_ref[...] * dh2).astype(BF16)
        _accum(dgt_ref, _colsum(dh2 * ffv), i)

    return pl.pallas_call(
        body, name="loss_bwd", grid=(L // tm,),
        in_specs=[_row(D, tm=tm), _row(D, tm=tm), _vec(D), _vec(D), _row(D, tm=tm)],
        out_specs=(_row(D, tm=tm), _row(D, tm=tm), pl.BlockSpec((1, 1), lambda i: (0, 0)), _vec(D), _vec(D)),
        out_shape=(jax.ShapeDtypeStruct((L, D), F32), jax.ShapeDtypeStruct((L, D), BF16),
                   jax.ShapeDtypeStruct((1, 1), F32), jax.ShapeDtypeStruct((1, D), F32), jax.ShapeDtypeStruct((1, D), F32)),
        compiler_params=_params(("arbitrary",)),
    )(h1, ff, gate2, final_g, target)


def norm_bwd(dz, h, dh_in, g, scale, name, gate=None, m_out=None):
    L, D = h.shape
    tm = _tm(L)
    tail = gate is not None

    def body(*refs):
        dz_ref, h_ref, di_ref, g_ref, sc_ref = refs[:5]
        rest = refs[5:]
        if tail:
            gt_ref, m_ref = rest[:2]
            rest = rest[2:]
        dh_ref, dsc_ref, dsh_ref, dg_ref = rest[:4]
        i = pl.program_id(0)
        hv, dzv = h_ref[...], dz_ref[...]
        r = _rms(hv)
        n = hv * r
        _accum(dsc_ref, _colsum(dzv * n * g_ref[...]), i)
        _accum(dsh_ref, _colsum(dzv), i)
        dzn = dzv * (1.0 + sc_ref[...])
        _accum(dg_ref, _colsum(dzn * n), i)
        dn = dzn * g_ref[...]
        dh = di_ref[...] + r * (dn - n * jnp.mean(dn * n, axis=-1, keepdims=True))
        dh_ref[...] = dh
        if tail:
            dmo_ref, dgt_ref = rest[4:]
            dmo_ref[...] = (gt_ref[...] * dh).astype(BF16)
            _accum(dgt_ref, _colsum(dh * m_ref[...]), i)

    ins = [dz, h, dh_in, g, scale]
    in_specs = [_row(D, tm=tm)] * 3 + [_vec(D)] * 2
    out_specs = [_row(D, tm=tm), _vec(D), _vec(D), _vec(D)]
    out_shape = [jax.ShapeDtypeStruct((L, D), F32)] + [jax.ShapeDtypeStruct((1, D), F32)] * 3
    if tail:
        ins += [gate, m_out]
        in_specs += [_vec(D), _row(D, tm=tm)]
        out_specs += [_row(D, tm=tm), _vec(D)]
        out_shape += [jax.ShapeDtypeStruct((L, D), BF16), jax.ShapeDtypeStruct((1, D), F32)]
    return pl.pallas_call(
        body, name=name, grid=(L // tm,), in_specs=in_specs, out_specs=tuple(out_specs), out_shape=tuple(out_shape),
        compiler_params=_params(("arbitrary",)),
    )(*ins)


def merge_bwd(dmerged, proj, col_gc, y_conv, ga, gb):
    L, D = y_conv.shape
    tm = _tm(L)
    h = D // 2
    c0 = col_gc // h

    def body(dm_ref, p0, p1, p2, p3, yc_ref, ga_ref, gb_ref, dyc_ref, dga_ref, dgb_ref, dg_ref):
        gc, gs = (p0, p1), (p2, p3)
        for s in range(2):
            cols = slice(s * h, (s + 1) * h)
            dm = dm_ref[:, cols]
            sc, ss, sb = _sigmoid(gc[s][...]), _sigmoid(gs[s][...]), _sigmoid(gb_ref[:, cols])
            gav = ga_ref[:, cols]
            dyc_ref[:, cols] = (dm * sc).astype(BF16)
            dg_ref[:, cols] = (dm * yc_ref[:, cols] * sc * (1.0 - sc)).astype(BF16)
            dg_ref[:, D + s * h:D + (s + 1) * h] = (dm * gav * sb * ss * (1.0 - ss)).astype(BF16)
            dys = dm * ss
            dga_ref[:, cols] = (dys * sb).astype(BF16)
            dgb_ref[:, cols] = (dys * gav * sb * (1.0 - sb)).astype(BF16)

    return pl.pallas_call(
        body, name="merge_bwd", grid=(L // tm,),
        in_specs=[_row(D, tm=tm)] + [_row(h, c0 + s, tm) for s in range(4)] + [_row(D, tm=tm)] * 3,
        out_specs=(_row(D, tm=tm), _row(D, tm=tm), _row(D, tm=tm), _row(2 * D, tm=tm)),
        out_shape=(jax.ShapeDtypeStruct((L, D), BF16),) * 3 + (jax.ShapeDtypeStruct((L, 2 * D), BF16),),
        compiler_params=_params(("parallel",)),
    )(dmerged, proj, proj, proj, proj, y_conv, ga, gb)


def conv_ln_bwd(dvs, vc, ln_g, ln_b):
    L, C = vc.shape
    tm = _tm(L)

    def body(d_ref, v_ref, g_ref, b_ref, o_ref, dg_ref, db_ref):
        i = pl.program_id(0)
        v = v_ref[...]
        mu = jnp.mean(v, axis=-1, keepdims=True)
        d = v - mu
        rstd = lax.rsqrt(jnp.mean(d * d, axis=-1, keepdims=True) + EPS)
        xh = d * rstd
        ln = xh * g_ref[...] + b_ref[...]
        sg = _sigmoid(ln)
        dln = d_ref[...] * sg * (1.0 + ln * (1.0 - sg))
        _accum(dg_ref, _colsum(dln * xh), i)
        _accum(db_ref, _colsum(dln), i)
        dxh = dln * g_ref[...]
        o_ref[...] = rstd * (dxh - jnp.mean(dxh, axis=-1, keepdims=True)
                             - xh * jnp.mean(dxh * xh, axis=-1, keepdims=True))

    return pl.pallas_call(
        body, name="conv_ln_bwd", grid=(L // tm,),
        in_specs=[_row(C, tm=tm), _row(C, tm=tm), _vec(C), _vec(C)],
        out_specs=(_row(C, tm=tm), _vec(C), _vec(C)),
        out_shape=(jax.ShapeDtypeStruct((L, C), F32), jax.ShapeDtypeStruct((1, C), F32), jax.ShapeDtypeStruct((1, C), F32)),
        compiler_params=_params(("arbitrary",)),
    )(dvs, vc, ln_g, ln_b)


def conv_bwd(dvc, proj, w_dw):
    L, C = dvc.shape
    tm = _tm(L)
    hb = tm // HALO
    last = L // HALO - 1
    nt = L // tm

    def body(d_ref, dn_ref, a_ref, g_ref, ah_ref, gh_ref, w_ref, o_ref, dw_ref, db_ref, dbuf, vbuf):
        i = pl.program_id(0)
        dcur = d_ref[...]
        dbuf[0:tm, :] = dcur
        dbuf[tm:tm + HALO, :] = dn_ref[...] * jnp.where(i < nt - 1, 1.0, 0.0)
        av, sg = a_ref[...], _sigmoid(g_ref[...])
        vbuf[0:HALO, :] = ah_ref[...] * _sigmoid(gh_ref[...]) * jnp.where(i > 0, 1.0, 0.0)
        vbuf[HALO:HALO + tm, :] = av * sg
        dv = jnp.zeros((tm, C), F32)
        for k in range(CONV_K):
            dv = dv + w_ref[k:k + 1, :] * dbuf[pl.ds(CONV_K - 1 - k, tm), :]
        o_ref[:, 0:C] = (dv * sg).astype(BF16)
        o_ref[:, C:2 * C] = (dv * av * sg * (1.0 - sg)).astype(BF16)

        @pl.when(i == 0)
        def _():
            dw_ref[...] = jnp.zeros_like(dw_ref)

        for k in range(CONV_K):
            dw_ref[k:k + 1, :] += _colsum(dcur * vbuf[pl.ds(HALO - (CONV_K - 1) + k, tm), :])
        _accum(db_ref, _colsum(dcur), i)

    prev = lambda cb: pl.BlockSpec((HALO, C), lambda i: (jnp.maximum(i * hb - 1, 0), cb))
    return pl.pallas_call(
        body, name="conv_bwd", grid=(nt,),
        in_specs=[_row(C, tm=tm), pl.BlockSpec((HALO, C), lambda i: (jnp.minimum((i + 1) * hb, last), 0)),
                  _row(C, 0, tm), _row(C, 1, tm), prev(0), prev(1), pl.BlockSpec((HALO, C), lambda i: (0, 0))],
        out_specs=(_row(2 * C, tm=tm), pl.BlockSpec((HALO, C), lambda i: (0, 0)), _vec(C)),
        out_shape=(jax.ShapeDtypeStruct((L, 2 * C), BF16), jax.ShapeDtypeStruct((HALO, C), F32),
                   jax.ShapeDtypeStruct((1, C), F32)),
        scratch_shapes=[pltpu.VMEM((tm + HALO, C), F32), pltpu.VMEM((HALO + tm, C), F32)],
        compiler_params=_params(("arbitrary",)),
    )(dvc, dvc, proj, proj, proj, proj, w_dw)


def _adamw(w, g, m, v):
    m = ADAM_B1 * m + (1.0 - ADAM_B1) * g
    v = ADAM_B2 * v + (1.0 - ADAM_B2) * (g * g)
    m_hat = m / (1.0 - ADAM_B1 ** ADAM_STEP)
    v_hat = v / (1.0 - ADAM_B2 ** ADAM_STEP)
    delta = -ADAM_LR * (m_hat / (jnp.sqrt(v_hat) + ADAM_EPS) + ADAM_WD * w)
    return delta, m, v


def _tile_rows(R, C):
    tr = 8
    while tr * 2 * C <= 128 * 1024 and R % (tr * 2) == 0:
        tr *= 2
    assert R % tr == 0, (R, C)
    return tr


def sum_devices(parts, name):
    _, R, C = parts.shape
    tr = _tile_rows(R, C)

    def body(p_ref, o_ref):
        s = p_ref[0]
        for j in range(1, NDEV):
            s = s + p_ref[j]
        o_ref[...] = s

    return pl.pallas_call(
        body, name=name, grid=(R // tr,),
        in_specs=[pl.BlockSpec((NDEV, tr, C), lambda i: (0, i, 0))],
        out_specs=pl.BlockSpec((tr, C), lambda i: (i, 0)), out_shape=jax.ShapeDtypeStruct((R, C), F32),
        compiler_params=_params(("parallel",)),
    )(parts)


def adam_update(w, g, m, v, name):
    R, C = w.shape
    tr = _tile_rows(R, C)

    def body(w_ref, g_ref, m_ref, v_ref, d_ref, mo_ref, vo_ref):
        d, mm, vv = _adamw(w_ref[...], g_ref[...], m_ref[...], v_ref[...])
        d_ref[...], mo_ref[...], vo_ref[...] = d, mm, vv

    spec = pl.BlockSpec((tr, C), lambda i: (i, 0))
    return pl.pallas_call(
        body, name=name, grid=(R // tr,), in_specs=[spec] * 4, out_specs=(spec,) * 3,
        out_shape=(jax.ShapeDtypeStruct((R, C), F32),) * 3, compiler_params=_params(("parallel",)),
    )(w, g, m, v)


def adam_reduce(parts, w, m, v, name):
    R, C = w.shape
    tr = _tile_rows(R, C)

    def body(p_ref, w_ref, m_ref, v_ref, g_ref, d_ref, mo_ref, vo_ref):
        g = p_ref[0].astype(F32)
        for j in range(1, NDEV):
            g = g + p_ref[j].astype(F32)
        g_ref[...] = g
        d, mm, vv = _adamw(w_ref[...], g, m_ref[...], v_ref[...])
        d_ref[...], mo_ref[...], vo_ref[...] = d, mm, vv

    spec = pl.BlockSpec((tr, C), lambda i: (i, 0))
    return pl.pallas_call(
        body, name=name, grid=(R // tr,),
        in_specs=[pl.BlockSpec((NDEV, tr, C), lambda i: (0, i, 0)), spec, spec, spec], out_specs=(spec,) * 4,
        out_shape=(jax.ShapeDtypeStruct((R, C), F32),) * 4, compiler_params=_params(("parallel",)),
    )(parts, w, m, v)


def adam_w_ada(c_act, dmod_cols, w, m, v):
    D, n = w.shape
    tn = 256

    def body(c_ref, dm_ref, w_ref, m_ref, v_ref, g_ref, d_ref, mo_ref, vo_ref):
        g = lax.dot_general(c_ref[...].astype(BF16), dm_ref[...].astype(BF16), (((0,), (0,)), ((), ())),
                            preferred_element_type=F32)
        g_ref[...] = g
        d, mm, vv = _adamw(w_ref[...], g, m_ref[...], v_ref[...])
        d_ref[...], mo_ref[...], vo_ref[...] = d, mm, vv

    spec = pl.BlockSpec((D, tn), lambda j: (0, j))
    return pl.pallas_call(
        body, name="adam_w_ada", grid=(n // tn,),
        in_specs=[pl.BlockSpec((NDEV, D), lambda j: (0, 0)), pl.BlockSpec((NDEV, tn), lambda j: (0, j)), spec, spec, spec],
        out_specs=(spec,) * 4, out_shape=(jax.ShapeDtypeStruct((D, n), F32),) * 4,
        compiler_params=_params(("parallel",)),
    )(c_act, dmod_cols, w, m, v)


def _block_diag(m):
    G, a, b = m.shape
    m4 = m.reshape(G // GB, GB, a, b)
    eye = jnp.eye(GB, dtype=m.dtype)
    return (m4[:, :, :, None, :] * eye[None, :, None, :, None]).reshape(G // GB, GB * a, GB * b)


def _diag_blocks(m, a, b):
    nb = m.shape[0]
    m5 = m.reshape(nb, GB, a, GB, b)
    idx = jnp.arange(GB)
    return m5[:, idx, :, idx, :].transpose(1, 0, 2, 3).reshape(nb * GB, a, b)


def _flat_pad(parts, mult):
    flat = jnp.concatenate([p.reshape(-1) for p in parts])
    pad = (-flat.shape[0]) % mult
    return jnp.pad(flat, (0, pad))


def _split(flat, like):
    out, off = [], 0
    for p in like:
        out.append(flat[off:off + p.size].reshape(p.shape))
        off += p.size
    return out


def kernel(x, c, w_ada, b_ada, norm1_g, w_in, w_dw, b_dw, ln_g, ln_b, w_conv_out, a_re, a_im, log_dt, b_re, b_im, c_re, c_im, d_skip, w_glu_a, w_glu_b, w_out, norm2_g, w_ff1, w_ff2, final_g, loss_target, m_w_ada, m_b_ada, m_norm1_g, m_w_in, m_w_dw, m_b_dw, m_ln_g, m_ln_b, m_w_conv_out, m_a_re, m_a_im, m_log_dt, m_b_re, m_b_im, m_c_re, m_c_im, m_d_skip, m_w_glu_a, m_w_glu_b, m_w_out, m_norm2_g, m_w_ff1, m_w_ff2, m_final_g, v_w_ada, v_b_ada, v_norm1_g, v_w_in, v_w_dw, v_b_dw, v_ln_g, v_ln_b, v_w_conv_out, v_a_re, v_a_im, v_log_dt, v_b_re, v_b_im, v_c_re, v_c_im, v_d_skip, v_w_glu_a, v_w_glu_b, v_w_out, v_norm2_g, v_w_ff1, v_w_ff2, v_final_g):
    W = dict(w_ada=w_ada, b_ada=b_ada, norm1_g=norm1_g, w_in=w_in, w_dw=w_dw, b_dw=b_dw, ln_g=ln_g, ln_b=ln_b,
             w_conv_out=w_conv_out, a_re=a_re, a_im=a_im, log_dt=log_dt, b_re=b_re, b_im=b_im, c_re=c_re, c_im=c_im,
             d_skip=d_skip, w_glu_a=w_glu_a, w_glu_b=w_glu_b, w_out=w_out, norm2_g=norm2_g, w_ff1=w_ff1, w_ff2=w_ff2,
             final_g=final_g)
    Mo = dict(w_ada=m_w_ada, b_ada=m_b_ada, norm1_g=m_norm1_g, w_in=m_w_in, w_dw=m_w_dw, b_dw=m_b_dw, ln_g=m_ln_g,
              ln_b=m_ln_b, w_conv_out=m_w_conv_out, a_re=m_a_re, a_im=m_a_im, log_dt=m_log_dt, b_re=m_b_re, b_im=m_b_im,
              c_re=m_c_re, c_im=m_c_im, d_skip=m_d_skip, w_glu_a=m_w_glu_a, w_glu_b=m_w_glu_b, w_out=m_w_out,
              norm2_g=m_norm2_g, w_ff1=m_w_ff1, w_ff2=m_w_ff2, final_g=m_final_g)
    Vo = dict(w_ada=v_w_ada, b_ada=v_b_ada, norm1_g=v_norm1_g, w_in=v_w_in, w_dw=v_w_dw, b_dw=v_b_dw, ln_g=v_ln_g,
              ln_b=v_ln_b, w_conv_out=v_w_conv_out, a_re=v_a_re, a_im=v_a_im, log_dt=v_log_dt, b_re=v_b_re, b_im=v_b_im,
              c_re=v_c_re, c_im=v_c_im, d_skip=v_d_skip, w_glu_a=v_w_glu_a, w_glu_b=v_w_glu_b, w_out=v_w_out,
              norm2_g=v_norm2_g, w_ff1=v_w_ff1, w_ff2=v_w_ff2, final_g=v_final_g)
    names = list(W)

    me = _me()
    xs, tgt = x[0], loss_target[0]
    L, D = xs.shape
    CW = w_dw.shape[2] * NDEV
    G, P = a_re.shape[1], a_re.shape[2]
    H = b_re.shape[3]
    n_ada = w_ada.shape[2]

    w_dw_pad = jnp.pad(w_dw[0], ((0, HALO - CONV_K), (0, 0)))
    c_all, w_dw_all = _exchange([c, w_dw_pad], "gather_small", True)
    c_all = c_all.reshape(NDEV, D)
    w_dw_full = w_dw_all.transpose(1, 0, 2).reshape(HALO, CW)
    big = ["w_in", "w_conv_out", "w_glu_a", "w_glu_b", "w_out", "w_ff1", "w_ff2"]
    handles, _ = exchange_start([W[k][0].astype(BF16) for k in big], "gather_weights_start", True)
    gather_handle = dict(zip(big, handles))

    def weight(k, after):
        w = exchange_wait(gather_handle[k], after, "gather_wait_" + k, True)
        if k in ("w_out", "w_ff2"):
            w = w.reshape(1, w.shape[0] * w.shape[1], w.shape[2])
        return w

    scatter_handle = {}

    def scatter(k, g, then):
        (scatter_handle[k],), token = exchange_start([g], "scatter_start_" + k, False)
        return lax.optimization_barrier((then, token))[0]

    b_cols = lax.dynamic_slice(b_ada, (0, me * n_ada), (1, n_ada))
    mod_cols, c_act = adaln_mod(c_all, w_ada[0], b_cols)
    (mod_all,) = _exchange([mod_cols], "gather_mod", True)
    mod = lax.dynamic_slice(mod_all, (0, me, 0), (NDEV, 1, n_ada)).reshape(6, 1, D)
    shift1, scale1, gate1, shift2, scale2, gate2 = [mod[j] for j in range(6)]

    u = prenorm(xs, norm1_g, scale1, shift1, "prenorm1")
    wg = {"w_in": weight("w_in", u)}
    proj = mm_nn(u, wg["w_in"], "in_proj")
    vs, vc = conv_fwd(proj, w_dw_full, b_dw, ln_g, ln_b)
    wg["w_conv_out"] = weight("w_conv_out", vs)
    y_conv = mm_nn(vs, wg["w_conv_out"], "conv_out")

    br2 = b_re[0].transpose(0, 2, 1).reshape(G * H, P)
    bi2 = b_im[0].transpose(0, 2, 1).reshape(G * H, P)
    ldt = log_dt[0].reshape(G, 1)
    expand = jnp.repeat(jnp.eye(G, dtype=F32), H, axis=0)
    lbr, lbi, bbr, bbi = s5_params(a_re[0], a_im[0], ldt, br2, bi2, expand)
    tabs = s5_tables(lbr.reshape(1, G * P), lbi.reshape(1, G * P))
    bdr, bdi = _block_diag(bbr.reshape(G, H, P)), _block_diag(bbi.reshape(G, H, P))
    cdr = _block_diag(c_re[0].transpose(0, 2, 1))
    cdi = _block_diag(c_im[0].transpose(0, 2, 1))
    cd2 = jnp.concatenate([cdr, -cdi], axis=1).astype(BF16)
    s_re, s_im, y_pre, yg = s5_fwd(proj, 2 * CW, _rhs3(bdr), _rhs3(bdi), cd2, tabs, d_skip)
    wg["w_glu_a"] = weight("w_glu_a", yg)
    wg["w_glu_b"] = weight("w_glu_b", yg)
    ga = mm_nn(yg, wg["w_glu_a"], "glu_a")
    gb = mm_nn(yg, wg["w_glu_b"], "glu_b")
    merged = merge_fwd(proj, 3 * CW, y_conv, ga, gb)
    wg["w_out"] = weight("w_out", merged)
    m_out = mm_nn(merged, wg["w_out"], "out_proj")
    h1, z = residual_norm(xs, m_out, gate1, norm2_g, scale2, shift2)
    wg["w_ff1"] = weight("w_ff1", z)
    act = mm_nn(z, wg["w_ff1"], "ff1", out_dtype=BF16, epi=lambda r: jnp.square(jnp.maximum(r, 0.0)))
    wg["w_ff2"] = weight("w_ff2", act)
    ff = mm_nn(act, wg["w_ff2"], "ff2")

    dh2, dff, loss_part, d_final_g, d_gate2 = loss_bwd(h1, ff, gate2, final_g.reshape(1, D), tgt)
    df = mm_nt(dff, wg["w_ff2"], "ff2_dx", out_dtype=BF16,
               epi=lambda r, a: r * (2.0 * jnp.sqrt(a.astype(F32))), extras=(act,))
    df = scatter("w_ff2", mm_tn(act, dff, 1, "ff2_dw", out_dtype=BF16).reshape(NDEV, -1, D), df)
    df = scatter("w_ff1", mm_tn(z, df, NDEV, "ff1_dw", out_dtype=BF16), df)
    dz = mm_nt(df, wg["w_ff1"], "ff1_dx")
    dh1, d_scale2, d_shift2, d_norm2_g, dmo, d_gate1 = norm_bwd(dz, h1, dh2, norm2_g, scale2, "norm2_bwd", gate1, m_out)
    dmo = scatter("w_out", mm_tn(merged, dmo, 1, "out_dw", out_dtype=BF16).reshape(NDEV, -1, D), dmo)
    dmerged = mm_nt(dmo, wg["w_out"], "out_dx")
    dyc, dga, dgb, dproj_g = merge_bwd(dmerged, proj, 3 * CW, y_conv, ga, gb)
    dyc = scatter("w_conv_out", mm_tn(vs, dyc, NDEV, "conv_out_dw", out_dtype=BF16), dyc)
    dga = scatter("w_glu_a", mm_tn(yg, dga, NDEV, "glu_a_dw", out_dtype=BF16), dga)
    dgb = scatter("w_glu_b", mm_tn(yg, dgb, NDEV, "glu_b_dw", out_dtype=BF16), dgb)
    dvs = mm_nt(dyc, wg["w_conv_out"], "conv_out_dx")
    dyg_a = mm_nt(dga, wg["w_glu_a"], "glu_a_dx")
    dyg_b = mm_nt(dgb, wg["w_glu_b"], "glu_b_dx")
    dvc, d_ln_g, d_ln_b = conv_ln_bwd(dvs, vc, ln_g, ln_b)
    dproj_c, d_w_dw, d_b_dw = conv_bwd(dvc, proj, w_dw_full)
    bdt2 = jnp.concatenate([bdr.transpose(0, 2, 1), bdi.transpose(0, 2, 1)], axis=1).astype(BF16)
    cdrt3, cdit3 = _rhs3(cdr.transpose(0, 2, 1)), _rhs3(-cdi.transpose(0, 2, 1))
    dproj_s, d_d_skip, dcdr, dcdi, dbdr, dbdi, dlr8, dli8 = s5_bwd(
        dyg_a, dyg_b, y_pre, proj, 2 * CW, s_re, s_im, bdt2, cdrt3, cdit3, tabs, d_skip)
    dproj = jnp.concatenate([dproj_c, dproj_s, dproj_g], axis=1)
    dproj = scatter("w_in", mm_tn(u, dproj, NDEV, "in_dw", out_dtype=BF16), dproj)
    du = mm_nt(dproj, wg["w_in"], "in_dx")
    grad_x, d_scale1, d_shift1, d_norm1_g = norm_bwd(du, xs, dh1, norm1_g, scale1, "norm1_bwd")

    dmod = jnp.concatenate([d_shift1, d_scale1, d_gate1, d_shift2, d_scale2, d_gate2], axis=1)
    d_c_re = _diag_blocks(dcdr, H, P)
    d_c_im = _diag_blocks(dcdi, H, P)
    d_bbr = _diag_blocks(dbdr, H, P)
    d_bbi = _diag_blocks(dbdi, H, P)
    dlr = jnp.sum(dlr8, axis=0).reshape(G, P)
    dli = jnp.sum(dli8, axis=0).reshape(G, P)
    small_parts = [dmod, d_norm1_g, d_b_dw, d_ln_g, d_ln_b, dlr, dli, d_bbr, d_bbi, d_c_re, d_c_im, d_d_skip,
                   d_norm2_g, d_final_g, d_w_dw]
    pack = _flat_pad(small_parts, PACK).reshape(-1, 1024)
    (pack_all,) = _exchange([pack], "gather_small_grads", True)
    tot = sum_devices(pack_all, "sum_small_grads").reshape(-1)
    (g_b_ada, g_norm1_g, g_b_dw, g_ln_g, g_ln_b, t_lr, t_li, t_bbr, t_bbi, g_c_re_t, g_c_im_t, g_d_skip,
     g_norm2_g, g_final_g, g_w_dw_full) = _split(tot, small_parts)
    g_a_re, g_a_im, g_ldt, g_br2, g_bi2 = s5_params_bwd(
        a_re[0], a_im[0], ldt, br2, bi2, expand, t_lr, t_li, t_bbr.reshape(G * H, P), t_bbi.reshape(G * H, P))
    g_brt, g_bit = g_br2.reshape(G, H, P), g_bi2.reshape(G, H, P)
    dmod_all = pack_all[:, :, :].reshape(NDEV, -1)[:, :6 * D]
    dmod_cols = lax.dynamic_slice(dmod_all, (0, me * n_ada), (NDEV, n_ada))

    grads = {
        "b_ada": g_b_ada, "norm1_g": g_norm1_g, "b_dw": g_b_dw, "ln_g": g_ln_g, "ln_b": g_ln_b,
        "a_re": g_a_re[None], "a_im": g_a_im[None], "log_dt": g_ldt.reshape(1, G),
        "b_re": g_brt.transpose(0, 2, 1)[None], "b_im": g_bit.transpose(0, 2, 1)[None],
        "c_re": g_c_re_t[None], "c_im": g_c_im_t[None], "d_skip": g_d_skip, "norm2_g": g_norm2_g,
        "final_g": g_final_g.reshape(D),
        "w_dw": lax.dynamic_slice(g_w_dw_full, (0, me * (CW // NDEV)), (CONV_K, CW // NDEV))[None],
    }
    small = [k for k in names if k in grads]
    wf = _flat_pad([W[k] for k in small], PACK).reshape(-1, 1024)
    gf = _flat_pad([grads[k] for k in small], PACK).reshape(-1, 1024)
    mf = _flat_pad([Mo[k] for k in small], PACK).reshape(-1, 1024)
    vf = _flat_pad([Vo[k] for k in small], PACK).reshape(-1, 1024)
    d_s, m_s, v_s = adam_update(wf, gf, mf, vf, "adam_small")
    like = [W[k] for k in small]
    delta = dict(zip(small, _split(d_s.reshape(-1), like)))
    new_m = dict(zip(small, _split(m_s.reshape(-1), like)))
    new_v = dict(zip(small, _split(v_s.reshape(-1), like)))

    g, d, mm, vv = adam_w_ada(c_act, dmod_cols, w_ada[0], m_w_ada[0], v_w_ada[0])
    grads["w_ada"], delta["w_ada"], new_m["w_ada"], new_v["w_ada"] = g[None], d[None], mm[None], vv[None]

    after = d
    for k in ("w_ff2", "w_ff1", "w_out", "w_conv_out", "w_glu_a", "w_glu_b", "w_in"):
        parts = exchange_wait(scatter_handle[k], after, "scatter_wait_" + k, False)
        g, d, mm, vv = adam_reduce(parts, W[k][0], Mo[k][0], Vo[k][0], "adam_" + k)
        grads[k], delta[k], new_m[k], new_v[k] = g[None], d[None], mm[None], vv[None]
        after = d

    loss = lax.psum(loss_part[0, 0], ("x", "y", "c"))
    return (loss, grad_x[None], *[grads[k] for k in names], *[delta[k] for k in names],
            *[new_m[k] for k in names], *[new_v[k] for k in names])
```

```python
import functools
import math

import jax
import jax.numpy as jnp
from jax import lax
from jax.experimental import pallas as pl
from jax.experimental.pallas import tpu as pltpu

F32 = jnp.float32
BF16 = jnp.bfloat16
NDEV = 8
EPS = 1e-6
ADAM_LR, ADAM_B1, ADAM_B2, ADAM_EPS, ADAM_WD, ADAM_STEP = 0.001, 0.9, 0.999, 1e-08, 0.01, 10
CONV_K = 31
HALO = 32
GROUP = 16
STATE = 64
GB = 8
HI = lax.Precision.HIGHEST
MESH = pl.DeviceIdType.MESH
VMEM_LIMIT = 56 * 1024 * 1024
PACK_ROWS = 64
PACK = PACK_ROWS * 1024
ANY = pl.BlockSpec(memory_space=pl.ANY)


def _params(sem=None):
    if sem is None:
        return pltpu.CompilerParams(vmem_limit_bytes=VMEM_LIMIT)
    return pltpu.CompilerParams(dimension_semantics=sem, vmem_limit_bytes=VMEM_LIMIT)


def _sigmoid(v):
    return 1.0 / (1.0 + jnp.exp(-v))


def _me():
    return 4 * lax.axis_index("x") + 2 * lax.axis_index("y") + lax.axis_index("c")


def _peer(k):
    x, y, c = lax.axis_index("x"), lax.axis_index("y"), lax.axis_index("c")
    px = 1 - x if (k >> 2) & 1 else x
    py = 1 - y if (k >> 1) & 1 else y
    pc = 1 - c if k & 1 else c
    return (px, py, pc), 4 * px + 2 * py + pc


def _exchange(arrays, name, gather):
    n = len(arrays)
    out_shape = []
    for a in arrays:
        shp = (NDEV,) + a.shape if gather else a.shape
        out_shape.append(jax.ShapeDtypeStruct(shp, a.dtype))

    def body(*refs):
        ins, outs = refs[:n], refs[n:2 * n]
        send, recv, lsem = refs[2 * n:]
        me = _me()
        local = []
        for a in range(n):
            src = ins[a] if gather else ins[a].at[me]
            cp = pltpu.make_async_copy(src, outs[a].at[me], lsem.at[a])
            cp.start()
            local.append(cp)
        sends = []
        for a in range(n):
            for k in range(1, NDEV):
                dev, pidx = _peer(k)
                src = ins[a] if gather else ins[a].at[pidx]
                cp = pltpu.make_async_remote_copy(
                    src_ref=src, dst_ref=outs[a].at[me], send_sem=send.at[a * (NDEV - 1) + k - 1], recv_sem=recv.at[a * (NDEV - 1) + k - 1],
                    device_id=dev, device_id_type=MESH)
                cp.start()
                sends.append(cp)
        for a in range(n):
            for k in range(1, NDEV):
                dev, pidx = _peer(k)
                src = ins[a] if gather else ins[a].at[pidx]
                pltpu.make_async_remote_copy(
                    src_ref=src, dst_ref=outs[a].at[pidx], send_sem=send.at[a * (NDEV - 1) + k - 1], recv_sem=recv.at[a * (NDEV - 1) + k - 1],
                    device_id=dev, device_id_type=MESH).wait_recv()
        for cp in sends:
            cp.wait_send()
        for cp in local:
            cp.wait()

    return pl.pallas_call(
        body, name=name, out_shape=tuple(out_shape),
        in_specs=[ANY] * n, out_specs=tuple([ANY] * n),
        scratch_shapes=[pltpu.SemaphoreType.DMA((n * (NDEV - 1),)), pltpu.SemaphoreType.DMA((n * (NDEV - 1),)),
                        pltpu.SemaphoreType.DMA((n,))],
    )(*arrays)


HBM = pl.BlockSpec(memory_space=pltpu.HBM)
SEM = pl.BlockSpec(memory_space=pltpu.SEMAPHORE)
EFFECT = pltpu.SideEffectType.DATAFLOW_SIDE_EFFECTING
NPEER = NDEV - 1


def _landing(block_of_me, shape, dtype):
    land = lax.empty((NDEV,) + tuple(shape), dtype)
    start = (_me(),) + (0,) * len(shape)
    return pltpu.with_memory_space_constraint(lax.dynamic_update_slice(land, block_of_me[None], start), pltpu.HBM)


def exchange_start(arrays, name, gather, after=None):
    n = len(arrays)
    me = _me()
    deps = () if after is None else (after,)
    lands = []
    for a in arrays:
        if gather:
            lands.append(_landing(a, a.shape, a.dtype))
        else:
            mine = lax.dynamic_slice(a, (me,) + (0,) * (a.ndim - 1), (1,) + a.shape[1:])[0]
            lands.append(_landing(mine, a.shape[1:], a.dtype))
    srcs = [pltpu.with_memory_space_constraint(a, pltpu.HBM) for a in arrays]

    def body(*refs):
        ins, lnd = refs[:n], refs[n:2 * n]
        outs = refs[2 * n + len(deps):]
        sends, recvs, token = outs[:n], outs[n:2 * n], outs[-1]
        my = _me()
        for a in range(n):
            for k in range(1, NDEV):
                dev, pidx = _peer(k)
                src = ins[a] if gather else ins[a].at[pidx]
                pltpu.make_async_remote_copy(
                    src_ref=src, dst_ref=lnd[a].at[my], send_sem=sends[a].at[k - 1], recv_sem=recvs[a].at[k - 1],
                    device_id=dev, device_id_type=MESH).start()
        token[...] = jnp.zeros_like(token)

    out_shape = ([pltpu.SemaphoreType.DMA((NPEER,))] * (2 * n)
                 + [pltpu.HBM(a.shape, a.dtype) for a in srcs] + [pltpu.HBM(l.shape, l.dtype) for l in lands]
                 + [jax.ShapeDtypeStruct((8, 128), F32)])
    res = pl.pallas_call(
        body, name=name, out_shape=tuple(out_shape),
        in_specs=[HBM] * (2 * n) + [ANY] * len(deps),
        out_specs=tuple([SEM] * (2 * n) + [HBM] * (2 * n) + [pl.BlockSpec(memory_space=pltpu.VMEM)]),
        input_output_aliases={i: 2 * n + i for i in range(2 * n)},
        compiler_params=pltpu.CompilerParams(has_side_effects=EFFECT),
    )(*srcs, *lands, *deps)
    handles = [(res[a], res[n + a], res[2 * n + a], res[3 * n + a]) for a in range(n)]
    return handles, res[-1]


def exchange_wait(handle, after, name, gather):
    send_sem, recv_sem, src, land = handle

    def body(src_ref, land_ref, s_ref, r_ref, after_ref, src_out, land_out):
        for k in range(1, NDEV):
            dev, pidx = _peer(k)
            s = src_ref if gather else src_ref.at[pidx]
            cp = pltpu.make_async_remote_copy(
                src_ref=s, dst_ref=land_ref.at[pidx], send_sem=s_ref.at[k - 1], recv_sem=r_ref.at[k - 1],
                device_id=dev, device_id_type=MESH)
            cp.wait_send()
            cp.wait_recv()

    return pl.pallas_call(
        body, name=name, out_shape=(pltpu.HBM(src.shape, src.dtype), pltpu.HBM(land.shape, land.dtype)),
        in_specs=(HBM, HBM, SEM, SEM, ANY), out_specs=(HBM, HBM), input_output_aliases={0: 0, 1: 1},
        compiler_params=pltpu.CompilerParams(has_side_effects=EFFECT),
    )(src, land, send_sem, recv_sem, after)[1]


def _acc_steps(p, acc, k, nk, finish):
    if nk == 1:
        finish(p)
        return

    @pl.when(k == 0)
    def _():
        acc[...] = p

    @pl.when(k > 0)
    def _():
        acc[...] += p

    @pl.when(k == nk - 1)
    def _():
        finish(acc[...])


def mm_nn(a, w3, name, out_dtype=F32, epi=None, extras=()):
    M, K = a.shape
    J, _, n = w3.shape
    tm, tn, tk = min(1024, M), min(1024, n), min(2048, K)
    q, nk, ne = n // tn, K // tk, len(extras)

    def body(*refs):
        a_ref, w_ref = refs[:2]
        ex, o_ref, acc = refs[2:2 + ne], refs[2 + ne], refs[-1]
        p = jnp.dot(a_ref[...], w_ref[...], preferred_element_type=F32)

        def finish(r):
            if epi is not None:
                r = epi(r, *[e[...] for e in ex])
            o_ref[...] = r.astype(out_dtype)

        _acc_steps(p, acc, pl.program_id(2), nk, finish)

    return pl.pallas_call(
        body, name=name, grid=(M // tm, J * q, nk),
        in_specs=[pl.BlockSpec((tm, tk), lambda i, j, k: (i, k)),
                  pl.BlockSpec((None, tk, tn), lambda i, j, k: (j // q, k, j % q))]
        + [pl.BlockSpec((tm, tn), lambda i, j, k: (i, j))] * ne,
        out_specs=pl.BlockSpec((tm, tn), lambda i, j, k: (i, j)),
        out_shape=jax.ShapeDtypeStruct((M, J * n), out_dtype),
        scratch_shapes=[pltpu.VMEM((tm, tn), F32)],
        compiler_params=_params(("parallel", "parallel", "arbitrary")),
    )(a, w3, *extras)


def mm_nt(dy, w3, name, out_dtype=F32, epi=None, extras=(), dep=None):
    M, _ = dy.shape
    J, K, n = w3.shape
    tm, tn, tkk = min(1024, M), min(1024, n), min(1024, K)
    q, ne = n // tn, len(extras)
    nk = J * q
    deps = () if dep is None else (dep,)

    def body(*refs):
        d_ref, w_ref = refs[:2]
        ex, o_ref, acc = refs[2:2 + ne], refs[-2], refs[-1]
        p = lax.dot_general(d_ref[...], w_ref[...], (((1,), (1,)), ((), ())), preferred_element_type=F32)

        def finish(r):
            if epi is not None:
                r = epi(r, *[e[...] for e in ex])
            o_ref[...] = r.astype(out_dtype)

        _acc_steps(p, acc, pl.program_id(2), nk, finish)

    return pl.pallas_call(
        body, name=name, grid=(M // tm, K // tkk, nk),
        in_specs=[pl.BlockSpec((tm, tn), lambda i, kk, c: (i, c)),
                  pl.BlockSpec((None, tkk, tn), lambda i, kk, c: (c // q, kk, c % q))]
        + [pl.BlockSpec((tm, tkk), lambda i, kk, c: (i, kk))] * ne + [ANY] * len(deps),
        out_specs=pl.BlockSpec((tm, tkk), lambda i, kk, c: (i, kk)),
        out_shape=jax.ShapeDtypeStruct((M, K), out_dtype),
        scratch_shapes=[pltpu.VMEM((tm, tkk), F32)],
        compiler_params=_params(("parallel", "parallel", "arbitrary")),
    )(dy, w3, *extras, *deps)


def mm_tn(a, dy, J, name, out_dtype=F32, dep=None):
    M, K = a.shape
    n = dy.shape[1] // J
    tm, tn, tkk = min(1024, M), min(1024, n), min(1024, K)
    q, nk = n // tn, M // tm
    deps = () if dep is None else (dep,)

    def body(a_ref, d_ref, *rest):
        o_ref, acc = rest[-2:]
        p = lax.dot_general(a_ref[...], d_ref[...], (((0,), (0,)), ((), ())), preferred_element_type=F32)

        def finish(r):
            o_ref[...] = r.astype(out_dtype)

        _acc_steps(p, acc, pl.program_id(2), nk, finish)

    return pl.pallas_call(
        body, name=name, grid=(K // tkk, J * q, nk),
        in_specs=[pl.BlockSpec((tm, tkk), lambda kk, c, m: (m, kk)),
                  pl.BlockSpec((tm, tn), lambda kk, c, m: (m, c))] + [ANY] * len(deps),
        out_specs=pl.BlockSpec((None, tkk, tn), lambda kk, c, m: (c // q, kk, c % q)),
        out_shape=jax.ShapeDtypeStruct((J, K, n), out_dtype),
        scratch_shapes=[pltpu.VMEM((tkk, tn), F32)],
        compiler_params=_params(("parallel", "parallel", "arbitrary")),
    )(a, dy, *deps)


def _tm(L):
    return min(256, L)


def _row(w, cb=0, tm=None):
    return pl.BlockSpec((tm, w), lambda i: (i, cb))


def _vec(w, cb=0):
    return pl.BlockSpec((1, w), lambda i: (0, cb))


def _accum(ref, val, i):
    @pl.when(i == 0)
    def _():
        ref[...] = val

    @pl.when(i > 0)
    def _():
        ref[...] += val


def _colsum(v):
    return jnp.sum(v, axis=0, keepdims=True)


def _rms(v):
    return lax.rsqrt(jnp.mean(v * v, axis=-1, keepdims=True) + EPS)


def adaln_mod(c_all, w_ada, b_cols):
    B, D = c_all.shape
    n = w_ada.shape[1]
    tn = 512

    def body(c_ref, w_ref, b_ref, o_ref, ca_ref):
        cv = c_ref[...]
        ca = cv * _sigmoid(cv)
        ca_ref[...] = ca
        o_ref[...] = jnp.dot(ca.astype(BF16), w_ref[...].astype(BF16), preferred_element_type=F32) + b_ref[...]

    return pl.pallas_call(
        body, name="adaln_mod", grid=(n // tn,),
        in_specs=[pl.BlockSpec((B, D), lambda j: (0, 0)), pl.BlockSpec((D, tn), lambda j: (0, j)),
                  pl.BlockSpec((1, tn), lambda j: (0, j))],
        out_specs=(pl.BlockSpec((B, tn), lambda j: (0, j)), pl.BlockSpec((B, D), lambda j: (0, 0))),
        out_shape=(jax.ShapeDtypeStruct((B, n), F32), jax.ShapeDtypeStruct((B, D), F32)),
        compiler_params=_params(("arbitrary",)),
    )(c_all, w_ada, b_cols)


def prenorm(x, g, scale, shift, name):
    L, D = x.shape
    tm = _tm(L)

    def body(x_ref, g_ref, sc_ref, sh_ref, u_ref):
        xv = x_ref[...]
        u_ref[...] = (xv * _rms(xv) * g_ref[...] * (1.0 + sc_ref[...]) + sh_ref[...]).astype(BF16)

    return pl.pallas_call(
        body, name=name, grid=(L // tm,),
        in_specs=[_row(D, tm=tm), _vec(D), _vec(D), _vec(D)],
        out_specs=_row(D, tm=tm), out_shape=jax.ShapeDtypeStruct((L, D), BF16),
        compiler_params=_params(("parallel",)),
    )(x, g, scale, shift)


def conv_fwd(proj, w_dw, b_dw, ln_g, ln_b):
    L = proj.shape[0]
    C = w_dw.shape[1]
    tm = _tm(L)
    hb = tm // HALO

    def body(a_ref, g_ref, ah_ref, gh_ref, w_ref, b_ref, lg_ref, lb_ref, vs_ref, vc_ref, buf):
        i = pl.program_id(0)
        halo = ah_ref[...] * _sigmoid(gh_ref[...])
        buf[0:HALO, :] = halo * jnp.where(i > 0, 1.0, 0.0)
        buf[HALO:HALO + tm, :] = a_ref[...] * _sigmoid(g_ref[...])
        acc = jnp.zeros((tm, C), F32) + b_ref[...]
        for k in range(CONV_K):
            acc = acc + w_ref[k:k + 1, :] * buf[pl.ds(HALO - (CONV_K - 1) + k, tm), :]
        vc_ref[...] = acc
        mu = jnp.mean(acc, axis=-1, keepdims=True)
        d = acc - mu
        var = jnp.mean(d * d, axis=-1, keepdims=True)
        ln = d * lax.rsqrt(var + EPS) * lg_ref[...] + lb_ref[...]
        vs_ref[...] = (ln * _sigmoid(ln)).astype(BF16)

    prev = lambda cb: pl.BlockSpec((HALO, C), lambda i: (jnp.maximum(i * hb - 1, 0), cb))
    return pl.pallas_call(
        body, name="conv_fwd", grid=(L // tm,),
        in_specs=[_row(C, 0, tm), _row(C, 1, tm), prev(0), prev(1),
                  pl.BlockSpec((HALO, C), lambda i: (0, 0)), _vec(C), _vec(C), _vec(C)],
        out_specs=(_row(C, tm=tm), _row(C, tm=tm)),
        out_shape=(jax.ShapeDtypeStruct((L, C), BF16), jax.ShapeDtypeStruct((L, C), F32)),
        scratch_shapes=[pltpu.VMEM((HALO + tm, C), F32)],
        compiler_params=_params(("parallel",)),
    )(proj, proj, proj, proj, w_dw, b_dw, ln_g, ln_b)


def _gelu(v):
    return 0.5 * v * (1.0 + jnp.tanh(math.sqrt(2.0 / math.pi) * (v + 0.044715 * v * v * v)))


def _gelu_grad(v):
    k = math.sqrt(2.0 / math.pi)
    t = jnp.tanh(k * (v + 0.044715 * v * v * v))
    return 0.5 * (1.0 + t) + 0.5 * v * (1.0 - t * t) * k * (1.0 + 3.0 * 0.044715 * v * v)


def s5_param_fn(ar, ai, ldt, br, bi, expand):
    dt = jnp.exp(ldt)
    er = jnp.exp(ar * dt)
    th = ai * dt
    lbr, lbi = er * jnp.cos(th), er * jnp.sin(th)
    nr, ni = lbr - 1.0, lbi
    den = ar * ar + ai * ai
    qr, qi = (nr * ar + ni * ai) / den, (ni * ar - nr * ai) / den
    qre = jnp.dot(expand, qr, precision=HI, preferred_element_type=F32)
    qie = jnp.dot(expand, qi, precision=HI, preferred_element_type=F32)
    return lbr, lbi, qre * br - qie * bi, qre * bi + qie * br


def s5_params(ar, ai, ldt, br2, bi2, expand):
    def body(ar_ref, ai_ref, ld_ref, br_ref, bi_ref, e_ref, o1, o2, o3, o4):
        r = s5_param_fn(ar_ref[...], ai_ref[...], ld_ref[...], br_ref[...], bi_ref[...], e_ref[...])
        o1[...], o2[...], o3[...], o4[...] = r

    s2, s3 = jax.ShapeDtypeStruct(ar.shape, F32), jax.ShapeDtypeStruct(br2.shape, F32)
    return pl.pallas_call(body, name="s5_params", out_shape=(s2, s2, s3, s3), compiler_params=_params())(
        ar, ai, ldt, br2, bi2, expand)


def s5_params_bwd(ar, ai, ldt, br2, bi2, expand, dlr, dli, dbr, dbi):
    def body(ar_ref, ai_ref, ld_ref, br_ref, bi_ref, e_ref, c1, c2, c3, c4, o1, o2, o3, o4, o5):
        e = e_ref[...]
        fn = lambda a, b, c, d, f: s5_param_fn(a, b, c, d, f, e)
        _, vjp = jax.vjp(fn, ar_ref[...], ai_ref[...], ld_ref[...], br_ref[...], bi_ref[...])
        r = vjp((c1[...], c2[...], c3[...], c4[...]))
        o1[...], o2[...], o3[...], o4[...], o5[...] = r

    shapes = tuple(jax.ShapeDtypeStruct(v.shape, F32) for v in (ar, ai, ldt, br2, bi2))
    return pl.pallas_call(body, name="s5_params_bwd", out_shape=shapes, compiler_params=_params())(
        ar, ai, ldt, br2, bi2, expand, dlr, dli, dbr, dbi)


def s5_tables(lr, li):
    C = lr.shape[1]

    def body(lr_ref, li_ref, o_ref):
        row = lax.broadcasted_iota(jnp.int32, (8, C), 0)
        for rev in (0, 1):
            pr = jnp.broadcast_to(lr_ref[...], (8, C))
            pi = jnp.broadcast_to(-li_ref[...] if rev else li_ref[...], (8, C))
            br, bi = pr, pi
            pows = [(pr, pi)]
            for _ in range(7):
                pr, pi = pr * br - pi * bi, pr * bi + pi * br
                pows.append((pr, pi))
            base = 8 * rev
            for s, d in enumerate((1, 2, 4)):
                keep = (row + d <= 7) if rev else (row >= d)
                o_ref[base + 2 * s] = jnp.where(keep, pows[d - 1][0], 0.0)
                o_ref[base + 2 * s + 1] = jnp.where(keep, pows[d - 1][1], 0.0)
            cr, ci = jnp.zeros((8, C), F32), jnp.zeros((8, C), F32)
            for j in range(8):
                e = (8 - j) if rev else (j + 1)
                cr = jnp.where(row == j, pows[e - 1][0], cr)
                ci = jnp.where(row == j, pows[e - 1][1], ci)
            o_ref[base + 6] = cr
            o_ref[base + 7] = ci

    return pl.pallas_call(body, name="s5_tables", out_shape=jax.ShapeDtypeStruct((16, 8, C), F32),
                          compiler_params=_params())(lr, li)


def _scan_tile(xr, xi, tabs, cr, ci, rev):
    for s, d in enumerate((1, 2, 4)):
        tr, ti = tabs[2 * s], tabs[2 * s + 1]
        sh = (8 - d) if rev else d
        sr, si = pltpu.roll(xr, sh, 0), pltpu.roll(xi, sh, 0)
        xr, xi = xr + tr * sr - ti * si, xi + tr * si + ti * sr
    tr, ti = tabs[6], tabs[7]
    xr, xi = xr + tr * cr - ti * ci, xi + tr * ci + ti * cr
    return xr, xi


def _hi_lo(a):
    hi = a.astype(BF16)
    return hi, (a - hi.astype(F32)).astype(BF16)


def _lhs3(a):
    hi, lo = _hi_lo(a)
    return jnp.concatenate([hi, lo, hi], axis=1)


def _rhs3(m):
    hi, lo = _hi_lo(m)
    return jnp.concatenate([hi, hi, lo], axis=-2)


def s5_fwd(proj, col0, bdr3, bdi3, cd2, tabs, d_skip):
    L = proj.shape[0]
    nb, cw3, sw = bdr3.shape
    cw = cw3 // 3
    tl = min(256, L)
    cb0 = col0 // cw

    def body(u_ref, bdr_ref, bdi_ref, cd_ref, t_ref, dk_ref, sr_ref, si_ref, yp_ref, yg_ref, car):
        l = pl.program_id(1)

        @pl.when(l == 0)
        def _():
            car[...] = jnp.zeros_like(car)

        u = u_ref[...]
        u3 = _lhs3(u)
        sr_ref[...] = jnp.dot(u3, bdr_ref[...], preferred_element_type=F32)
        si_ref[...] = jnp.dot(u3, bdi_ref[...], preferred_element_type=F32)

        def tile(i, c):
            tabs = [t_ref[j] for j in range(8)]
            r0 = pl.multiple_of(i * 8, 8)
            xr, xi = _scan_tile(sr_ref[pl.ds(r0, 8), :], si_ref[pl.ds(r0, 8), :], tabs, c[0], c[1], False)
            sr_ref[pl.ds(r0, 8), :] = xr
            si_ref[pl.ds(r0, 8), :] = xi
            return xr[7:8, :], xi[7:8, :]

        c = lax.fori_loop(0, tl // 8, tile, (car[0:1, :], car[1:2, :]))
        car[0:1, :] = c[0]
        car[1:2, :] = c[1]
        s2 = jnp.concatenate([sr_ref[...].astype(BF16), si_ref[...].astype(BF16)], axis=1)
        y = jnp.dot(s2, cd_ref[...], preferred_element_type=F32) + dk_ref[...] * u
        yp_ref[...] = y
        yg_ref[...] = _gelu(y).astype(BF16)

    blk = lambda r, c: pl.BlockSpec((None, r, c), lambda b, l: (b, 0, 0))
    return pl.pallas_call(
        body, name="s5_fwd", grid=(nb, L // tl),
        in_specs=[pl.BlockSpec((tl, cw), lambda b, l: (l, cb0 + b)), blk(cw3, sw), blk(cw3, sw), blk(2 * sw, cw),
                  pl.BlockSpec((8, 8, sw), lambda b, l: (0, 0, b)), pl.BlockSpec((1, cw), lambda b, l: (0, b))],
        out_specs=(pl.BlockSpec((tl, sw), lambda b, l: (l, b)), pl.BlockSpec((tl, sw), lambda b, l: (l, b)),
                   pl.BlockSpec((tl, cw), lambda b, l: (l, b)), pl.BlockSpec((tl, cw), lambda b, l: (l, b))),
        out_shape=(jax.ShapeDtypeStruct((L, nb * sw), F32), jax.ShapeDtypeStruct((L, nb * sw), F32),
                   jax.ShapeDtypeStruct((L, nb * cw), F32), jax.ShapeDtypeStruct((L, nb * cw), BF16)),
        scratch_shapes=[pltpu.VMEM((8, sw), F32)],
        compiler_params=_params(("parallel", "arbitrary")),
    )(proj, bdr3, bdi3, cd2, tabs, d_skip)


def s5_bwd(dyg_a, dyg_b, yp, proj, col0, s_re, s_im, bdt2, cdrt3, cdit3, tabs, d_skip):
    L = proj.shape[0]
    nb, sw2, cw = bdt2.shape
    sw = sw2 // 2
    tl = min(256, L)
    nl = L // tl
    cb0 = col0 // cw
    tb = tl // 8

    def body(da_ref, db_ref, yp_ref, u_ref, sr_ref, si_ref, hr_ref, hi_ref, bdt_ref, cdrt_ref, cdit_ref,
             t_ref, dk_ref, du_ref, ddk_ref, dcr_ref, dci_ref, dbr_ref, dbi_ref, dlr_ref, dli_ref,
             gr, gi, pr, pi, car):
        l = pl.program_id(1)
        first = l == nl - 1

        @pl.when(l == 0)
        def _():
            car[...] = jnp.zeros_like(car)

        u = u_ref[...]
        dy = (da_ref[...] + db_ref[...]) * _gelu_grad(yp_ref[...])
        dy3 = _lhs3(dy)
        gr[...] = jnp.dot(dy3, cdrt_ref[...], preferred_element_type=F32)
        gi[...] = jnp.dot(dy3, cdit_ref[...], preferred_element_type=F32)
        inner = jnp.where(first, 0.0, 1.0)
        pr[0:8, :] = hr_ref[...] * inner
        pi[0:8, :] = hi_ref[...] * inner
        pr[8:8 + tl, :] = sr_ref[...]
        pi[8:8 + tl, :] = si_ref[...]
        row = lax.broadcasted_iota(jnp.int32, (8, sw), 0)

        def tile(j, c):
            tabs = [t_ref[8 + k] for k in range(8)]
            r0 = pl.multiple_of((tb - 1 - j) * 8, 8)
            xr, xi = _scan_tile(gr[pl.ds(r0, 8), :], gi[pl.ds(r0, 8), :], tabs, c[0], c[1], True)
            gr[pl.ds(r0, 8), :] = xr
            gi[pl.ds(r0, 8), :] = xi
            qr = jnp.where(row == 0, pltpu.roll(pr[pl.ds(r0, 8), :], 1, 0), pltpu.roll(pr[pl.ds(r0 + 8, 8), :], 1, 0))
            qi = jnp.where(row == 0, pltpu.roll(pi[pl.ds(r0, 8), :], 1, 0), pltpu.roll(pi[pl.ds(r0 + 8, 8), :], 1, 0))
            return xr[0:1, :], xi[0:1, :], c[2] + xr * qr + xi * qi, c[3] + xi * qr - xr * qi

        z = jnp.zeros((8, sw), F32)
        c = lax.fori_loop(0, tb, tile, (car[0:1, :], car[1:2, :], z, z))
        car[0:1, :] = c[0]
        car[1:2, :] = c[1]
        g_re, g_im = gr[...].astype(BF16), gi[...].astype(BF16)
        g2 = jnp.concatenate([g_re, g_im], axis=1)
        du_ref[...] = (dy * dk_ref[...] + jnp.dot(g2, bdt_ref[...], preferred_element_type=F32)).astype(BF16)
        tn = (((0,), (0,)), ((), ()))
        dyb, ub = dy.astype(BF16), u.astype(BF16)
        _accum(ddk_ref, _colsum(dy * u), l)
        _accum(dcr_ref, lax.dot_general(dyb, sr_ref[...].astype(BF16), tn, preferred_element_type=F32), l)
        _accum(dci_ref, -lax.dot_general(dyb, si_ref[...].astype(BF16), tn, preferred_element_type=F32), l)
        _accum(dbr_ref, lax.dot_general(ub, g_re, tn, preferred_element_type=F32), l)
        _accum(dbi_ref, lax.dot_general(ub, g_im, tn, preferred_element_type=F32), l)
        _accum(dlr_ref, c[2], l)
        _accum(dli_ref, c[3], l)

    rl = lambda l: nl - 1 - l
    cblk = lambda w, off=0: pl.BlockSpec((tl, w), lambda b, l: (rl(l), off + b))
    halo = pl.BlockSpec((8, sw), lambda b, l: (jnp.maximum(rl(l) * tb - 1, 0), b))
    mat = lambda r, c: pl.BlockSpec((None, r, c), lambda b, l: (b, 0, 0))
    return pl.pallas_call(
        body, name="s5_bwd", grid=(nb, nl),
        in_specs=[cblk(cw), cblk(cw), cblk(cw), cblk(cw, cb0), cblk(sw), cblk(sw), halo, halo,
                  mat(2 * sw, cw), mat(3 * cw, sw), mat(3 * cw, sw),
                  pl.BlockSpec((16, 8, sw), lambda b, l: (0, 0, b)), pl.BlockSpec((1, cw), lambda b, l: (0, b))],
        out_specs=(cblk(cw), pl.BlockSpec((1, cw), lambda b, l: (0, b)), mat(cw, sw), mat(cw, sw), mat(cw, sw), mat(cw, sw),
                   pl.BlockSpec((8, sw), lambda b, l: (0, b)), pl.BlockSpec((8, sw), lambda b, l: (0, b))),
        out_shape=(jax.ShapeDtypeStruct((L, nb * cw), BF16), jax.ShapeDtypeStruct((1, nb * cw), F32),
                   jax.ShapeDtypeStruct((nb, cw, sw), F32), jax.ShapeDtypeStruct((nb, cw, sw), F32),
                   jax.ShapeDtypeStruct((nb, cw, sw), F32), jax.ShapeDtypeStruct((nb, cw, sw), F32),
                   jax.ShapeDtypeStruct((8, nb * sw), F32), jax.ShapeDtypeStruct((8, nb * sw), F32)),
        scratch_shapes=[pltpu.VMEM((tl, sw), F32), pltpu.VMEM((tl, sw), F32),
                        pltpu.VMEM((tl + 8, sw), F32), pltpu.VMEM((tl + 8, sw), F32), pltpu.VMEM((8, sw), F32)],
        compiler_params=_params(("parallel", "arbitrary")),
    )(dyg_a, dyg_b, yp, proj, s_re, s_im, s_re, s_im, bdt2, cdrt3, cdit3, tabs, d_skip)


def merge_fwd(proj, col_gc, y_conv, ga, gb):
    L, D = y_conv.shape
    tm = _tm(L)
    h = D // 2
    c0 = col_gc // h

    def body(p0, p1, p2, p3, yc_ref, ga_ref, gb_ref, o_ref):
        gc, gs = (p0, p1), (p2, p3)
        for s in range(2):
            cols = slice(s * h, (s + 1) * h)
            y_ssm = ga_ref[:, cols] * _sigmoid(gb_ref[:, cols])
            o_ref[:, cols] = (_sigmoid(gc[s][...]) * yc_ref[:, cols] + _sigmoid(gs[s][...]) * y_ssm).astype(BF16)

    return pl.pallas_call(
        body, name="merge_fwd", grid=(L // tm,),
        in_specs=[_row(h, c0 + s, tm) for s in range(4)] + [_row(D, tm=tm)] * 3,
        out_specs=_row(D, tm=tm), out_shape=jax.ShapeDtypeStruct((L, D), BF16),
        compiler_params=_params(("parallel",)),
    )(proj, proj, proj, proj, y_conv, ga, gb)


def residual_norm(x, m_out, gate, g, scale, shift):
    L, D = x.shape
    tm = _tm(L)

    def body(x_ref, m_ref, gt_ref, g_ref, sc_ref, sh_ref, h_ref, z_ref):
        h = x_ref[...] + gt_ref[...] * m_ref[...]
        h_ref[...] = h
        z_ref[...] = (h * _rms(h) * g_ref[...] * (1.0 + sc_ref[...]) + sh_ref[...]).astype(BF16)

    return pl.pallas_call(
        body, name="residual_norm", grid=(L // tm,),
        in_specs=[_row(D, tm=tm), _row(D, tm=tm), _vec(D), _vec(D), _vec(D), _vec(D)],
        out_specs=(_row(D, tm=tm), _row(D, tm=tm)),
        out_shape=(jax.ShapeDtypeStruct((L, D), F32), jax.ShapeDtypeStruct((L, D), BF16)),
        compiler_params=_params(("parallel",)),
    )(x, m_out, gate, g, scale, shift)


def loss_bwd(h1, ff, gate2, final_g, target):
    L, D = h1.shape
    tm = _tm(L)

    def body(h_ref, f_ref, gt_ref, g_ref, t_ref, dh_ref, dff_ref, loss_ref, dg_ref, dgt_ref):
        i = pl.program_id(0)
        ffv = f_ref[...]
        h2 = h_ref[...] + gt_ref[...] * ffv
        r = _rms(h2)
        n = h2 * r
        err = n * g_ref[...] - t_ref[...]
        per_tok = jnp.mean(err * err, axis=-1, keepdims=True)
        _accum(loss_ref, 0.5 * jnp.sum(per_tok, axis=0, keepdims=True), i)
        dy = err * (1.0 / D)
        _accum(dg_ref, _colsum(dy * n), i)
        dn = dy * g_ref[...]
        dh2 = r * (dn - n * jnp.mean(dn * n, axis=-1, keepdims=True))
        dh_ref[...] = dh2
        dff_ref[...] = (gt_ref[...] * dh2).astype(BF16)
        _accum(dgt_ref, _colsum(dh2 * ffv), i)

    return pl.pallas_call(
        body, name="loss_bwd", grid=(L // tm,),
        in_specs=[_row(D, tm=tm), _row(D, tm=tm), _vec(D), _vec(D), _row(D, tm=tm)],
        out_specs=(_row(D, tm=tm), _row(D, tm=tm), pl.BlockSpec((1, 1), lambda i: (0, 0)), _vec(D), _vec(D)),
        out_shape=(jax.ShapeDtypeStruct((L, D), F32), jax.ShapeDtypeStruct((L, D), BF16),
                   jax.ShapeDtypeStruct((1, 1), F32), jax.ShapeDtypeStruct((1, D), F32), jax.ShapeDtypeStruct((1, D), F32)),
        compiler_params=_params(("arbitrary",)),
    )(h1, ff, gate2, final_g, target)


def norm_bwd(dz, h, dh_in, g, scale, name, gate=None, m_out=None):
    L, D = h.shape
    tm = _tm(L)
    tail = gate is not None

    def body(*refs):
        dz_ref, h_ref, di_ref, g_ref, sc_ref = refs[:5]
        rest = refs[5:]
        if tail:
            gt_ref, m_ref = rest[:2]
            rest = rest[2:]
        dh_ref, dsc_ref, dsh_ref, dg_ref = rest[:4]
        i = pl.program_id(0)
        hv, dzv = h_ref[...], dz_ref[...]
        r = _rms(hv)
        n = hv * r
        _accum(dsc_ref, _colsum(dzv * n * g_ref[...]), i)
        _accum(dsh_ref, _colsum(dzv), i)
        dzn = dzv * (1.0 + sc_ref[...])
        _accum(dg_ref, _colsum(dzn * n), i)
        dn = dzn * g_ref[...]
        dh = di_ref[...] + r * (dn - n * jnp.mean(dn * n, axis=-1, keepdims=True))
        dh_ref[...] = dh
        if tail:
            dmo_ref, dgt_ref = rest[4:]
            dmo_ref[...] = (gt_ref[...] * dh).astype(BF16)
            _accum(dgt_ref, _colsum(dh * m_ref[...]), i)

    ins = [dz, h, dh_in, g, scale]
    in_specs = [_row(D, tm=tm)] * 3 + [_vec(D)] * 2
    out_specs = [_row(D, tm=tm), _vec(D), _vec(D), _vec(D)]
    out_shape = [jax.ShapeDtypeStruct((L, D), F32)] + [jax.ShapeDtypeStruct((1, D), F32)] * 3
    if tail:
        ins += [gate, m_out]
        in_specs += [_vec(D), _row(D, tm=tm)]
        out_specs += [_row(D, tm=tm), _vec(D)]
        out_shape += [jax.ShapeDtypeStruct((L, D), BF16), jax.ShapeDtypeStruct((1, D), F32)]
    return pl.pallas_call(
        body, name=name, grid=(L // tm,), in_specs=in_specs, out_specs=tuple(out_specs), out_shape=tuple(out_shape),
        compiler_params=_params(("arbitrary",)),
    )(*ins)


def merge_bwd(dmerged, proj, col_gc, y_conv, ga, gb):
    L, D = y_conv.shape
    tm = _tm(L)
    h = D // 2
    c0 = col_gc // h

    def body(dm_ref, p0, p1, p2, p3, yc_ref, ga_ref, gb_ref, dyc_ref, dga_ref, dgb_ref, dg_ref):
        gc, gs = (p0, p1), (p2, p3)
        for s in range(2):
            cols = slice(s * h, (s + 1) * h)
            dm = dm_ref[:, cols]
            sc, ss, sb = _sigmoid(gc[s][...]), _sigmoid(gs[s][...]), _sigmoid(gb_ref[:, cols])
            gav = ga_ref[:, cols]
            dyc_ref[:, cols] = (dm * sc).astype(BF16)
            dg_ref[:, cols] = (dm * yc_ref[:, cols] * sc * (1.0 - sc)).astype(BF16)
            dg_ref[:, D + s * h:D + (s + 1) * h] = (dm * gav * sb * ss * (1.0 - ss)).astype(BF16)
            dys = dm * ss
            dga_ref[:, cols] = (dys * sb).astype(BF16)
            dgb_ref[:, cols] = (dys * gav * sb * (1.0 - sb)).astype(BF16)

    return pl.pallas_call(
        body, name="merge_bwd", grid=(L // tm,),
        in_specs=[_row(D, tm=tm)] + [_row(h, c0 + s, tm) for s in range(4)] + [_row(D, tm=tm)] * 3,
        out_specs=(_row(D, tm=tm), _row(D, tm=tm), _row(D, tm=tm), _row(2 * D, tm=tm)),
        out_shape=(jax.ShapeDtypeStruct((L, D), BF16),) * 3 + (jax.ShapeDtypeStruct((L, 2 * D), BF16),),
        compiler_params=_params(("parallel",)),
    )(dmerged, proj, proj, proj, proj, y_conv, ga, gb)


def conv_ln_bwd(dvs, vc, ln_g, ln_b):
    L, C = vc.shape
    tm = _tm(L)

    def body(d_ref, v_ref, g_ref, b_ref, o_ref, dg_ref, db_ref):
        i = pl.program_id(0)
        v = v_ref[...]
        mu = jnp.mean(v, axis=-1, keepdims=True)
        d = v - mu
        rstd = lax.rsqrt(jnp.mean(d * d, axis=-1, keepdims=True) + EPS)
        xh = d * rstd
        ln = xh * g_ref[...] + b_ref[...]
        sg = _sigmoid(ln)
        dln = d_ref[...] * sg * (1.0 + ln * (1.0 - sg))
        _accum(dg_ref, _colsum(dln * xh), i)
        _accum(db_ref, _colsum(dln), i)
        dxh = dln * g_ref[...]
        o_ref[...] = rstd * (dxh - jnp.mean(dxh, axis=-1, keepdims=True)
                             - xh * jnp.mean(dxh * xh, axis=-1, keepdims=True))

    return pl.pallas_call(
        body, name="conv_ln_bwd", grid=(L // tm,),
        in_specs=[_row(C, tm=tm), _row(C, tm=tm), _vec(C), _vec(C)],
        out_specs=(_row(C, tm=tm), _vec(C), _vec(C)),
        out_shape=(jax.ShapeDtypeStruct((L, C), F32), jax.ShapeDtypeStruct((1, C), F32), jax.ShapeDtypeStruct((1, C), F32)),
        compiler_params=_params(("arbitrary",)),
    )(dvs, vc, ln_g, ln_b)


def conv_bwd(dvc, proj, w_dw):
    L, C = dvc.shape
    tm = _tm(L)
    hb = tm // HALO
    last = L // HALO - 1
    nt = L // tm

    def body(d_ref, dn_ref, a_ref, g_ref, ah_ref, gh_ref, w_ref, o_ref, dw_ref, db_ref, dbuf, vbuf):
        i = pl.program_id(0)
        dcur = d_ref[...]
        dbuf[0:tm, :] = dcur
        dbuf[tm:tm + HALO, :] = dn_ref[...] * jnp.where(i < nt - 1, 1.0, 0.0)
        av, sg = a_ref[...], _sigmoid(g_ref[...])
        vbuf[0:HALO, :] = ah_ref[...] * _sigmoid(gh_ref[...]) * jnp.where(i > 0, 1.0, 0.0)
        vbuf[HALO:HALO + tm, :] = av * sg
        dv = jnp.zeros((tm, C), F32)
        for k in range(CONV_K):
            dv = dv + w_ref[k:k + 1, :] * dbuf[pl.ds(CONV_K - 1 - k, tm), :]
        o_ref[:, 0:C] = (dv * sg).astype(BF16)
        o_ref[:, C:2 * C] = (dv * av * sg * (1.0 - sg)).astype(BF16)

        @pl.when(i == 0)
        def _():
            dw_ref[...] = jnp.zeros_like(dw_ref)

        for k in range(CONV_K):
            dw_ref[k:k + 1, :] += _colsum(dcur * vbuf[pl.ds(HALO - (CONV_K - 1) + k, tm), :])
        _accum(db_ref, _colsum(dcur), i)

    prev = lambda cb: pl.BlockSpec((HALO, C), lambda i: (jnp.maximum(i * hb - 1, 0), cb))
    return pl.pallas_call(
        body, name="conv_bwd", grid=(nt,),
        in_specs=[_row(C, tm=tm), pl.BlockSpec((HALO, C), lambda i: (jnp.minimum((i + 1) * hb, last), 0)),
                  _row(C, 0, tm), _row(C, 1, tm), prev(0), prev(1), pl.BlockSpec((HALO, C), lambda i: (0, 0))],
        out_specs=(_row(2 * C, tm=tm), pl.BlockSpec((HALO, C), lambda i: (0, 0)), _vec(C)),
        out_shape=(jax.ShapeDtypeStruct((L, 2 * C), BF16), jax.ShapeDtypeStruct((HALO, C), F32),
                   jax.ShapeDtypeStruct((1, C), F32)),
        scratch_shapes=[pltpu.VMEM((tm + HALO, C), F32), pltpu.VMEM((HALO + tm, C), F32)],
        compiler_params=_params(("arbitrary",)),
    )(dvc, dvc, proj, proj, proj, proj, w_dw)


def _adamw(w, g, m, v):
    m = ADAM_B1 * m + (1.0 - ADAM_B1) * g
    v = ADAM_B2 * v + (1.0 - ADAM_B2) * (g * g)
    m_hat = m / (1.0 - ADAM_B1 ** ADAM_STEP)
    v_hat = v / (1.0 - ADAM_B2 ** ADAM_STEP)
    delta = -ADAM_LR * (m_hat / (jnp.sqrt(v_hat) + ADAM_EPS) + ADAM_WD * w)
    return delta, m, v


def _tile_rows(R, C):
    tr = 8
    while tr * 2 * C <= 128 * 1024 and R % (tr * 2) == 0:
        tr *= 2
    assert R % tr == 0, (R, C)
    return tr


def sum_devices(parts, name):
    _, R, C = parts.shape
    tr = _tile_rows(R, C)

    def body(p_ref, o_ref):
        s = p_ref[0]
        for j in range(1, NDEV):
            s = s + p_ref[j]
        o_ref[...] = s

    return pl.pallas_call(
        body, name=name, grid=(R // tr,),
        in_specs=[pl.BlockSpec((NDEV, tr, C), lambda i: (0, i, 0))],
        out_specs=pl.BlockSpec((tr, C), lambda i: (i, 0)), out_shape=jax.ShapeDtypeStruct((R, C), F32),
        compiler_params=_params(("parallel",)),
    )(parts)


def adam_update(w, g, m, v, name):
    R, C = w.shape
    tr = _tile_rows(R, C)

    def body(w_ref, g_ref, m_ref, v_ref, d_ref, mo_ref, vo_ref):
        d, mm, vv = _adamw(w_ref[...], g_ref[...], m_ref[...], v_ref[...])
        d_ref[...], mo_ref[...], vo_ref[...] = d, mm, vv

    spec = pl.BlockSpec((tr, C), lambda i: (i, 0))
    return pl.pallas_call(
        body, name=name, grid=(R // tr,), in_specs=[spec] * 4, out_specs=(spec,) * 3,
        out_shape=(jax.ShapeDtypeStruct((R, C), F32),) * 3, compiler_params=_params(("parallel",)),
    )(w, g, m, v)


def adam_reduce(parts, w, m, v, name):
    R, C = w.shape
    tr = _tile_rows(R, C)

    def body(p_ref, w_ref, m_ref, v_ref, g_ref, d_ref, mo_ref, vo_ref):
        g = p_ref[0].astype(F32)
        for j in range(1, NDEV):
            g = g + p_ref[j].astype(F32)
        g_ref[...] = g
        d, mm, vv = _adamw(w_ref[...], g, m_ref[...], v_ref[...])
        d_ref[...], mo_ref[...], vo_ref[...] = d, mm, vv

    spec = pl.BlockSpec((tr, C), lambda i: (i, 0))
    return pl.pallas_call(
        body, name=name, grid=(R // tr,),
        in_specs=[pl.BlockSpec((NDEV, tr, C), lambda i: (0, i, 0)), spec, spec, spec], out_specs=(spec,) * 4,
        out_shape=(jax.ShapeDtypeStruct((R, C), F32),) * 4, compiler_params=_params(("parallel",)),
    )(parts, w, m, v)


def adam_w_ada(c_act, dmod_cols, w, m, v):
    D, n = w.shape
    tn = 256

    def body(c_ref, dm_ref, w_ref, m_ref, v_ref, g_ref, d_ref, mo_ref, vo_ref):
        g = lax.dot_general(c_ref[...].astype(BF16), dm_ref[...].astype(BF16), (((0,), (0,)), ((), ())),
                            preferred_element_type=F32)
        g_ref[...] = g
        d, mm, vv = _adamw(w_ref[...], g, m_ref[...], v_ref[...])
        d_ref[...], mo_ref[...], vo_ref[...] = d, mm, vv

    spec = pl.BlockSpec((D, tn), lambda j: (0, j))
    return pl.pallas_call(
        body, name="adam_w_ada", grid=(n // tn,),
        in_specs=[pl.BlockSpec((NDEV, D), lambda j: (0, 0)), pl.BlockSpec((NDEV, tn), lambda j: (0, j)), spec, spec, spec],
        out_specs=(spec,) * 4, out_shape=(jax.ShapeDtypeStruct((D, n), F32),) * 4,
        compiler_params=_params(("parallel",)),
    )(c_act, dmod_cols, w, m, v)


def _block_diag(m):
    G, a, b = m.shape
    m4 = m.reshape(G // GB, GB, a, b)
    eye = jnp.eye(GB, dtype=m.dtype)
    return (m4[:, :, :, None, :] * eye[None, :, None, :, None]).reshape(G // GB, GB * a, GB * b)


def _diag_blocks(m, a, b):
    nb = m.shape[0]
    m5 = m.reshape(nb, GB, a, GB, b)
    idx = jnp.arange(GB)
    return m5[:, idx, :, idx, :].transpose(1, 0, 2, 3).reshape(nb * GB, a, b)


def _flat_pad(parts, mult):
    flat = jnp.concatenate([p.reshape(-1) for p in parts])
    pad = (-flat.shape[0]) % mult
    return jnp.pad(flat, (0, pad))


def _split(flat, like):
    out, off = [], 0
    for p in like:
        out.append(flat[off:off + p.size].reshape(p.shape))
        off += p.size
    return out


def kernel(x, c, w_ada, b_ada, norm1_g, w_in, w_dw, b_dw, ln_g, ln_b, w_conv_out, a_re, a_im, log_dt, b_re, b_im, c_re, c_im, d_skip, w_glu_a, w_glu_b, w_out, norm2_g, w_ff1, w_ff2, final_g, loss_target, m_w_ada, m_b_ada, m_norm1_g, m_w_in, m_w_dw, m_b_dw, m_ln_g, m_ln_b, m_w_conv_out, m_a_re, m_a_im, m_log_dt, m_b_re, m_b_im, m_c_re, m_c_im, m_d_skip, m_w_glu_a, m_w_glu_b, m_w_out, m_norm2_g, m_w_ff1, m_w_ff2, m_final_g, v_w_ada, v_b_ada, v_norm1_g, v_w_in, v_w_dw, v_b_dw, v_ln_g, v_ln_b, v_w_conv_out, v_a_re, v_a_im, v_log_dt, v_b_re, v_b_im, v_c_re, v_c_im, v_d_skip, v_w_glu_a, v_w_glu_b, v_w_out, v_norm2_g, v_w_ff1, v_w_ff2, v_final_g):
    W = dict(w_ada=w_ada, b_ada=b_ada, norm1_g=norm1_g, w_in=w_in, w_dw=w_dw, b_dw=b_dw, ln_g=ln_g, ln_b=ln_b,
             w_conv_out=w_conv_out, a_re=a_re, a_im=a_im, log_dt=log_dt, b_re=b_re, b_im=b_im, c_re=c_re, c_im=c_im,
             d_skip=d_skip, w_glu_a=w_glu_a, w_glu_b=w_glu_b, w_out=w_out, norm2_g=norm2_g, w_ff1=w_ff1, w_ff2=w_ff2,
             final_g=final_g)
    Mo = dict(w_ada=m_w_ada, b_ada=m_b_ada, norm1_g=m_norm1_g, w_in=m_w_in, w_dw=m_w_dw, b_dw=m_b_dw, ln_g=m_ln_g,
              ln_b=m_ln_b, w_conv_out=m_w_conv_out, a_re=m_a_re, a_im=m_a_im, log_dt=m_log_dt, b_re=m_b_re, b_im=m_b_im,
              c_re=m_c_re, c_im=m_c_im, d_skip=m_d_skip, w_glu_a=m_w_glu_a, w_glu_b=m_w_glu_b, w_out=m_w_out,
              norm2_g=m_norm2_g, w_ff1=m_w_ff1, w_ff2=m_w_ff2, final_g=m_final_g)
    Vo = dict(w_ada=v_w_ada, b_ada=v_b_ada, norm1_g=v_norm1_g, w_in=v_w_in, w_dw=v_w_dw, b_dw=v_b_dw, ln_g=v_ln_g,
              ln_b=v_ln_b, w_conv_out=v_w_conv_out, a_re=v_a_re, a_im=v_a_im, log_dt=v_log_dt, b_re=v_b_re, b_im=v_b_im,
              c_re=v_c_re, c_im=v_c_im, d_skip=v_d_skip, w_glu_a=v_w_glu_a, w_glu_b=v_w_glu_b, w_out=v_w_out,
              norm2_g=v_norm2_g, w_ff1=v_w_ff1, w_ff2=v_w_ff2, final_g=v_final_g)
    names = list(W)

    me = _me()
    xs, tgt = x[0], loss_target[0]
    L, D = xs.shape
    CW = w_dw.shape[2] * NDEV
    G, P = a_re.shape[1], a_re.shape[2]
    H = b_re.shape[3]
    n_ada = w_ada.shape[2]

    (c_all,) = _exchange([c], "gather_c", True)
    b_cols = lax.dynamic_slice(b_ada, (0, me * n_ada), (1, n_ada))
    mod_cols, c_act = adaln_mod(c_all.reshape(NDEV, D), w_ada[0], b_cols)
    (mod_all,) = _exchange([mod_cols], "gather_mod", True)
    mod = lax.dynamic_slice(mod_all, (0, me, 0), (NDEV, 1, n_ada)).reshape(6, 1, D)
    shift1, scale1, gate1, shift2, scale2, gate2 = [mod[j] for j in range(6)]

    big = ["w_in", "w_conv_out", "w_glu_a", "w_glu_b", "w_out", "w_ff1", "w_ff2"]
    order = ["w_in", "w_dw"] + big[1:]
    shards = {k: W[k][0].astype(BF16) for k in big}
    shards["w_dw"] = jnp.pad(w_dw[0], ((0, HALO - CONV_K), (0, 0)))
    handles, _ = exchange_start([shards[k] for k in order], "gather_weights_start", True, after=mod_all)
    gather_handle = dict(zip(order, handles))

    def weight(k, after):
        w = exchange_wait(gather_handle[k], after, "gather_wait_" + k, True)
        if k in ("w_out", "w_ff2"):
            w = w.reshape(1, w.shape[0] * w.shape[1], w.shape[2])
        elif k in narrow:
            w = w.transpose(1, 0, 2).reshape(1, w.shape[1], NDEV * w.shape[2])
        elif k == "w_dw":
            w = w.transpose(1, 0, 2).reshape(HALO, CW)
        return w

    narrow = ("w_conv_out", "w_glu_a", "w_glu_b")
    scatter_handle = {}

    def scatter(k, g):
        if k in narrow:
            g = g.reshape(g.shape[1], NDEV, -1).transpose(1, 0, 2)
        elif g.shape[0] == 1:
            g = g.reshape(NDEV, -1, g.shape[2])
        (scatter_handle[k],), token = exchange_start([g], "scatter_start_" + k, False)
        return token

    u = prenorm(xs, norm1_g, scale1, shift1, "prenorm1")
    wg = {"w_in": weight("w_in", u)}
    proj = mm_nn(u, wg["w_in"], "in_proj")
    w_dw_full = weight("w_dw", proj)
    vs, vc = conv_fwd(proj, w_dw_full, b_dw, ln_g, ln_b)
    wg["w_conv_out"] = weight("w_conv_out", vs)
    y_conv = mm_nn(vs, wg["w_conv_out"], "conv_out")

    br2 = b_re[0].transpose(0, 2, 1).reshape(G * H, P)
    bi2 = b_im[0].transpose(0, 2, 1).reshape(G * H, P)
    ldt = log_dt[0].reshape(G, 1)
    expand = jnp.repeat(jnp.eye(G, dtype=F32), H, axis=0)
    lbr, lbi, bbr, bbi = s5_params(a_re[0], a_im[0], ldt, br2, bi2, expand)
    tabs = s5_tables(lbr.reshape(1, G * P), lbi.reshape(1, G * P))
    bdr, bdi = _block_diag(bbr.reshape(G, H, P)), _block_diag(bbi.reshape(G, H, P))
    cdr = _block_diag(c_re[0].transpose(0, 2, 1))
    cdi = _block_diag(c_im[0].transpose(0, 2, 1))
    cd2 = jnp.concatenate([cdr, -cdi], axis=1).astype(BF16)
    s_re, s_im, y_pre, yg = s5_fwd(proj, 2 * CW, _rhs3(bdr), _rhs3(bdi), cd2, tabs, d_skip)
    wg["w_glu_a"] = weight("w_glu_a", yg)
    wg["w_glu_b"] = weight("w_glu_b", yg)
    ga = mm_nn(yg, wg["w_glu_a"], "glu_a")
    gb = mm_nn(yg, wg["w_glu_b"], "glu_b")
    merged = merge_fwd(proj, 3 * CW, y_conv, ga, gb)
    wg["w_out"] = weight("w_out", merged)
    m_out = mm_nn(merged, wg["w_out"], "out_proj")
    h1, z = residual_norm(xs, m_out, gate1, norm2_g, scale2, shift2)
    wg["w_ff1"] = weight("w_ff1", z)
    act = mm_nn(z, wg["w_ff1"], "ff1", out_dtype=BF16, epi=lambda r: jnp.square(jnp.maximum(r, 0.0)))
    wg["w_ff2"] = weight("w_ff2", act)
    ff = mm_nn(act, wg["w_ff2"], "ff2")

    dh2, dff, loss_part, d_final_g, d_gate2 = loss_bwd(h1, ff, gate2, final_g.reshape(1, D), tgt)
    df = mm_nt(dff, wg["w_ff2"], "ff2_dx", out_dtype=BF16,
               epi=lambda r, a: r * (2.0 * jnp.sqrt(a.astype(F32))), extras=(act,))
    t = scatter("w_ff2", mm_tn(act, dff, 1, "ff2_dw", out_dtype=BF16))
    t = scatter("w_ff1", mm_tn(z, df, NDEV, "ff1_dw", out_dtype=BF16, dep=t))
    dz = mm_nt(df, wg["w_ff1"], "ff1_dx", dep=t)
    dh1, d_scale2, d_shift2, d_norm2_g, dmo, d_gate1 = norm_bwd(dz, h1, dh2, norm2_g, scale2, "norm2_bwd", gate1, m_out)
    t = scatter("w_out", mm_tn(merged, dmo, 1, "out_dw", out_dtype=BF16))
    dmerged = mm_nt(dmo, wg["w_out"], "out_dx", dep=t)
    dyc, dga, dgb, dproj_g = merge_bwd(dmerged, proj, 3 * CW, y_conv, ga, gb)
    t = scatter("w_conv_out", mm_tn(vs, dyc, 1, "conv_out_dw", out_dtype=BF16))
    t = scatter("w_glu_a", mm_tn(yg, dga, 1, "glu_a_dw", out_dtype=BF16, dep=t))
    t = scatter("w_glu_b", mm_tn(yg, dgb, 1, "glu_b_dw", out_dtype=BF16, dep=t))
    dvs = mm_nt(dyc, wg["w_conv_out"], "conv_out_dx", dep=t)
    dyg_a = mm_nt(dga, wg["w_glu_a"], "glu_a_dx", dep=t)
    dyg_b = mm_nt(dgb, wg["w_glu_b"], "glu_b_dx", dep=t)
    dvc, d_ln_g, d_ln_b = conv_ln_bwd(dvs, vc, ln_g, ln_b)
    dproj_c, d_w_dw, d_b_dw = conv_bwd(dvc, proj, w_dw_full)
    bdt2 = jnp.concatenate([bdr.transpose(0, 2, 1), bdi.transpose(0, 2, 1)], axis=1).astype(BF16)
    cdrt3, cdit3 = _rhs3(cdr.transpose(0, 2, 1)), _rhs3(-cdi.transpose(0, 2, 1))
    dproj_s, d_d_skip, dcdr, dcdi, dbdr, dbdi, dlr8, dli8 = s5_bwd(
        dyg_a, dyg_b, y_pre, proj, 2 * CW, s_re, s_im, bdt2, cdrt3, cdit3, tabs, d_skip)
    dproj = jnp.concatenate([dproj_c, dproj_s, dproj_g], axis=1)
    t = scatter("w_in", mm_tn(u, dproj, NDEV, "in_dw", out_dtype=BF16))
    du = mm_nt(dproj, wg["w_in"], "in_dx", dep=t)
    grad_x, d_scale1, d_shift1, d_norm1_g = norm_bwd(du, xs, dh1, norm1_g, scale1, "norm1_bwd")

    dmod = jnp.concatenate([d_shift1, d_scale1, d_gate1, d_shift2, d_scale2, d_gate2], axis=1)
    d_c_re = _diag_blocks(dcdr, H, P)
    d_c_im = _diag_blocks(dcdi, H, P)
    d_bbr = _diag_blocks(dbdr, H, P)
    d_bbi = _diag_blocks(dbdi, H, P)
    dlr = jnp.sum(dlr8, axis=0).reshape(G, P)
    dli = jnp.sum(dli8, axis=0).reshape(G, P)
    small_parts = [dmod, d_norm1_g, d_b_dw, d_ln_g, d_ln_b, dlr, dli, d_bbr, d_bbi, d_c_re, d_c_im, d_d_skip,
                   d_norm2_g, d_final_g, d_w_dw]
    pack8 = _flat_pad(small_parts, PACK).reshape(NDEV, -1, 1024)
    parts8, dmod_from = _exchange([pack8, dmod.reshape(NDEV, 1, n_ada)], "scatter_small_grads", False)
    (tot8,) = _exchange([sum_devices(parts8, "sum_small_grads")], "gather_small_sums", True)
    tot = tot8.reshape(-1)
    (g_b_ada, g_norm1_g, g_b_dw, g_ln_g, g_ln_b, t_lr, t_li, t_bbr, t_bbi, g_c_re_t, g_c_im_t, g_d_skip,
     g_norm2_g, g_final_g, g_w_dw_full) = _split(tot, small_parts)
    g_a_re, g_a_im, g_ldt, g_br2, g_bi2 = s5_params_bwd(
        a_re[0], a_im[0], ldt, br2, bi2, expand, t_lr, t_li, t_bbr.reshape(G * H, P), t_bbi.reshape(G * H, P))
    g_brt, g_bit = g_br2.reshape(G, H, P), g_bi2.reshape(G, H, P)
    dmod_cols = dmod_from.reshape(NDEV, n_ada)

    grads = {
        "b_ada": g_b_ada, "norm1_g": g_norm1_g, "b_dw": g_b_dw, "ln_g": g_ln_g, "ln_b": g_ln_b,
        "a_re": g_a_re[None], "a_im": g_a_im[None], "log_dt": g_ldt.reshape(1, G),
        "b_re": g_brt.transpose(0, 2, 1)[None], "b_im": g_bit.transpose(0, 2, 1)[None],
        "c_re": g_c_re_t[None], "c_im": g_c_im_t[None], "d_skip": g_d_skip, "norm2_g": g_norm2_g,
        "final_g": g_final_g.reshape(D),
        "w_dw": lax.dynamic_slice(g_w_dw_full, (0, me * (CW // NDEV)), (CONV_K, CW // NDEV))[None],
    }
    small = [k for k in names if k in grads]
    wf = _flat_pad([W[k] for k in small], PACK).reshape(-1, 1024)
    gf = _flat_pad([grads[k] for k in small], PACK).reshape(-1, 1024)
    mf = _flat_pad([Mo[k] for k in small], PACK).reshape(-1, 1024)
    vf = _flat_pad([Vo[k] for k in small], PACK).reshape(-1, 1024)
    d_s, m_s, v_s = adam_update(wf, gf, mf, vf, "adam_small")
    like = [W[k] for k in small]
    delta = dict(zip(small, _split(d_s.reshape(-1), like)))
    new_m = dict(zip(small, _split(m_s.reshape(-1), like)))
    new_v = dict(zip(small, _split(v_s.reshape(-1), like)))

    g, d, mm, vv = adam_w_ada(c_act, dmod_cols, w_ada[0], m_w_ada[0], v_w_ada[0])
    grads["w_ada"], delta["w_ada"], new_m["w_ada"], new_v["w_ada"] = g[None], d[None], mm[None], vv[None]

    after = d
    for k in ("w_ff2", "w_ff1", "w_out", "w_conv_out", "w_glu_a", "w_glu_b", "w_in"):
        parts = exchange_wait(scatter_handle[k], after, "scatter_wait_" + k, False)
        g, d, mm, vv = adam_reduce(parts, W[k][0], Mo[k][0], Vo[k][0], "adam_" + k)
        grads[k], delta[k], new_m[k], new_v[k] = g[None], d[None], mm[None], vv[None]
        after = d

    loss = lax.psum(loss_part[0, 0], ("x", "y", "c"))
    return (loss, grad_x[None], *[grads[k] for k in names], *[delta[k] for k in names],
            *[new_m[k] for k in names], *[new_v[k] for k in names])
```

```python
import functools
import math

import jax
import jax.numpy as jnp
from jax import lax
from jax.experimental import pallas as pl
from jax.experimental.pallas import tpu as pltpu

F32 = jnp.float32
BF16 = jnp.bfloat16
NDEV = 8
EPS = 1e-6
ADAM_LR, ADAM_B1, ADAM_B2, ADAM_EPS, ADAM_WD, ADAM_STEP = 0.001, 0.9, 0.999, 1e-08, 0.01, 10
CONV_K = 31
HALO = 32
GROUP = 16
STATE = 64
GB = 8
S5_ROWS = 512
HI = lax.Precision.HIGHEST
MESH = pl.DeviceIdType.MESH
VMEM_LIMIT = 56 * 1024 * 1024
MAX_CONTRACT = 2048
PACK_ROWS = 64
PACK = PACK_ROWS * 1024
ANY = pl.BlockSpec(memory_space=pl.ANY)


def _params(sem=None):
    if sem is None:
        return pltpu.CompilerParams(vmem_limit_bytes=VMEM_LIMIT)
    return pltpu.CompilerParams(dimension_semantics=sem, vmem_limit_bytes=VMEM_LIMIT)


def _sigmoid(v):
    return 1.0 / (1.0 + jnp.exp(-v))


def _me():
    return 4 * lax.axis_index("x") + 2 * lax.axis_index("y") + lax.axis_index("c")


def _peer(k):
    x, y, c = lax.axis_index("x"), lax.axis_index("y"), lax.axis_index("c")
    px = 1 - x if (k >> 2) & 1 else x
    py = 1 - y if (k >> 1) & 1 else y
    pc = 1 - c if k & 1 else c
    return (px, py, pc), 4 * px + 2 * py + pc


def _exchange(arrays, name, gather):
    n = len(arrays)
    out_shape = []
    for a in arrays:
        shp = (NDEV,) + a.shape if gather else a.shape
        out_shape.append(jax.ShapeDtypeStruct(shp, a.dtype))

    def body(*refs):
        ins, outs = refs[:n], refs[n:2 * n]
        send, recv, lsem = refs[2 * n:]
        me = _me()
        local = []
        for a in range(n):
            src = ins[a] if gather else ins[a].at[me]
            cp = pltpu.make_async_copy(src, outs[a].at[me], lsem.at[a])
            cp.start()
            local.append(cp)
        sends = []
        for a in range(n):
            for k in range(1, NDEV):
                dev, pidx = _peer(k)
                src = ins[a] if gather else ins[a].at[pidx]
                cp = pltpu.make_async_remote_copy(
                    src_ref=src, dst_ref=outs[a].at[me], send_sem=send.at[a * (NDEV - 1) + k - 1], recv_sem=recv.at[a * (NDEV - 1) + k - 1],
                    device_id=dev, device_id_type=MESH)
                cp.start()
                sends.append(cp)
        for a in range(n):
            for k in range(1, NDEV):
                dev, pidx = _peer(k)
                src = ins[a] if gather else ins[a].at[pidx]
                pltpu.make_async_remote_copy(
                    src_ref=src, dst_ref=outs[a].at[pidx], send_sem=send.at[a * (NDEV - 1) + k - 1], recv_sem=recv.at[a * (NDEV - 1) + k - 1],
                    device_id=dev, device_id_type=MESH).wait_recv()
        for cp in sends:
            cp.wait_send()
        for cp in local:
            cp.wait()

    return pl.pallas_call(
        body, name=name, out_shape=tuple(out_shape),
        in_specs=[ANY] * n, out_specs=tuple([ANY] * n),
        scratch_shapes=[pltpu.SemaphoreType.DMA((n * (NDEV - 1),)), pltpu.SemaphoreType.DMA((n * (NDEV - 1),)),
                        pltpu.SemaphoreType.DMA((n,))],
    )(*arrays)


HBM = pl.BlockSpec(memory_space=pltpu.HBM)
SEM = pl.BlockSpec(memory_space=pltpu.SEMAPHORE)
EFFECT = pltpu.SideEffectType.DATAFLOW_SIDE_EFFECTING
NPEER = NDEV - 1


def _landing(block_of_me, shape, dtype):
    land = lax.empty((NDEV,) + tuple(shape), dtype)
    start = (_me(),) + (0,) * len(shape)
    return pltpu.with_memory_space_constraint(lax.dynamic_update_slice(land, block_of_me[None], start), pltpu.HBM)


def exchange_start(arrays, name, gather, after=None):
    n = len(arrays)
    me = _me()
    deps = () if after is None else (after,)
    lands = []
    for a in arrays:
        if gather:
            lands.append(_landing(a, a.shape, a.dtype))
        else:
            mine = lax.dynamic_slice(a, (me,) + (0,) * (a.ndim - 1), (1,) + a.shape[1:])[0]
            lands.append(_landing(mine, a.shape[1:], a.dtype))
    srcs = [pltpu.with_memory_space_constraint(a, pltpu.HBM) for a in arrays]

    def body(*refs):
        ins, lnd = refs[:n], refs[n:2 * n]
        outs = refs[2 * n + len(deps):]
        sends, recvs, token = outs[:n], outs[n:2 * n], outs[-1]
        my = _me()
        for a in range(n):
            for k in range(1, NDEV):
                dev, pidx = _peer(k)
                src = ins[a] if gather else ins[a].at[pidx]
                pltpu.make_async_remote_copy(
                    src_ref=src, dst_ref=lnd[a].at[my], send_sem=sends[a].at[k - 1], recv_sem=recvs[a].at[k - 1],
                    device_id=dev, device_id_type=MESH).start()
        token[...] = jnp.zeros_like(token)

    out_shape = ([pltpu.SemaphoreType.DMA((NPEER,))] * (2 * n)
                 + [pltpu.HBM(a.shape, a.dtype) for a in srcs] + [pltpu.HBM(l.shape, l.dtype) for l in lands]
                 + [jax.ShapeDtypeStruct((8, 128), F32)])
    res = pl.pallas_call(
        body, name=name, out_shape=tuple(out_shape),
        in_specs=[HBM] * (2 * n) + [ANY] * len(deps),
        out_specs=tuple([SEM] * (2 * n) + [HBM] * (2 * n) + [pl.BlockSpec(memory_space=pltpu.VMEM)]),
        input_output_aliases={i: 2 * n + i for i in range(2 * n)},
        compiler_params=pltpu.CompilerParams(has_side_effects=EFFECT),
    )(*srcs, *lands, *deps)
    handles = [(res[a], res[n + a], res[2 * n + a], res[3 * n + a]) for a in range(n)]
    return handles, res[-1]


def exchange_wait(handle, after, name, gather):
    send_sem, recv_sem, src, land = handle

    def body(src_ref, land_ref, s_ref, r_ref, after_ref, src_out, land_out):
        for k in range(1, NDEV):
            dev, pidx = _peer(k)
            s = src_ref if gather else src_ref.at[pidx]
            cp = pltpu.make_async_remote_copy(
                src_ref=s, dst_ref=land_ref.at[pidx], send_sem=s_ref.at[k - 1], recv_sem=r_ref.at[k - 1],
                device_id=dev, device_id_type=MESH)
            cp.wait_send()
            cp.wait_recv()

    return pl.pallas_call(
        body, name=name, out_shape=(pltpu.HBM(src.shape, src.dtype), pltpu.HBM(land.shape, land.dtype)),
        in_specs=(HBM, HBM, SEM, SEM, ANY), out_specs=(HBM, HBM), input_output_aliases={0: 0, 1: 1},
        compiler_params=pltpu.CompilerParams(has_side_effects=EFFECT),
    )(src, land, send_sem, recv_sem, after)[1]


ICI_PEERS = (2, 4, 6)
SIBLING = 1


def gather2_start(block, name, after):
    land = _landing(block, block.shape, block.dtype)
    src = pltpu.with_memory_space_constraint(block, pltpu.HBM)

    def body(src_ref, land_ref, after_ref, send, recv_sib, recv_ici, src_out, land_out, token):
        my = _me()
        dev, _ = _peer(SIBLING)
        pltpu.make_async_remote_copy(src_ref=src_ref, dst_ref=land_ref.at[my], send_sem=send.at[0],
                                     recv_sem=recv_sib.at[0], device_id=dev, device_id_type=MESH).start()
        for j, k in enumerate(ICI_PEERS):
            dev, _ = _peer(k)
            pltpu.make_async_remote_copy(src_ref=src_ref, dst_ref=land_ref.at[my], send_sem=send.at[1 + j],
                                         recv_sem=recv_ici.at[j], device_id=dev, device_id_type=MESH).start()
        token[...] = jnp.zeros_like(token)

    n = len(ICI_PEERS)
    res = pl.pallas_call(
        body, name=name,
        out_shape=(pltpu.SemaphoreType.DMA((1 + n,)), pltpu.SemaphoreType.DMA((1,)), pltpu.SemaphoreType.DMA((n,)),
                   pltpu.HBM(src.shape, src.dtype), pltpu.HBM(land.shape, land.dtype),
                   jax.ShapeDtypeStruct((8, 128), F32)),
        in_specs=(HBM, HBM, ANY), out_specs=(SEM, SEM, SEM, HBM, HBM, pl.BlockSpec(memory_space=pltpu.VMEM)),
        input_output_aliases={0: 3, 1: 4},
        compiler_params=pltpu.CompilerParams(has_side_effects=EFFECT),
    )(src, land, after)
    return res[:5], res[5]


def gather2_forward(handle, name, after):
    send, recv_sib, recv_ici, src, land = handle
    n = len(ICI_PEERS)

    def body(src_ref, land_ref, recv_ici_ref, after_ref, fsend, frecv, src_out, land_out):
        sib, _ = _peer(SIBLING)
        for j, k in enumerate(ICI_PEERS):
            dev, pidx = _peer(k)
            pltpu.make_async_remote_copy(src_ref=src_ref, dst_ref=land_ref.at[pidx], send_sem=fsend.at[j],
                                         recv_sem=recv_ici_ref.at[j], device_id=dev, device_id_type=MESH).wait_recv()
            pltpu.make_async_remote_copy(src_ref=land_ref.at[pidx], dst_ref=land_ref.at[pidx], send_sem=fsend.at[j],
                                         recv_sem=frecv.at[j], device_id=sib, device_id_type=MESH).start()

    res = pl.pallas_call(
        body, name=name,
        out_shape=(pltpu.SemaphoreType.DMA((n,)), pltpu.SemaphoreType.DMA((n,)),
                   pltpu.HBM(src.shape, src.dtype), pltpu.HBM(land.shape, land.dtype)),
        in_specs=(HBM, HBM, SEM, ANY), out_specs=(SEM, SEM, HBM, HBM), input_output_aliases={0: 2, 1: 3},
        compiler_params=pltpu.CompilerParams(has_side_effects=EFFECT),
    )(src, land, recv_ici, after)
    return (send, recv_sib, res[0], res[1], res[2], res[3])


def gather2_wait(handle, name):
    send, recv_sib, fsend, frecv, src, land = handle
    n = len(ICI_PEERS)

    def body(src_ref, land_ref, send_ref, recv_sib_ref, fsend_ref, frecv_ref, src_out, land_out):
        sib, sib_idx = _peer(SIBLING)
        own = pltpu.make_async_remote_copy(src_ref=src_ref, dst_ref=land_ref.at[sib_idx], send_sem=send_ref.at[0],
                                           recv_sem=recv_sib_ref.at[0], device_id=sib, device_id_type=MESH)
        own.wait_send()
        own.wait_recv()
        for j, k in enumerate(ICI_PEERS):
            dev, pidx = _peer(k)
            pltpu.make_async_remote_copy(src_ref=src_ref, dst_ref=land_ref.at[pidx], send_sem=send_ref.at[1 + j],
                                         recv_sem=frecv_ref.at[j], device_id=dev, device_id_type=MESH).wait_send()
            _, fidx = _peer(k ^ SIBLING)
            fwd = pltpu.make_async_remote_copy(src_ref=land_ref.at[pidx], dst_ref=land_ref.at[fidx],
                                               send_sem=fsend_ref.at[j], recv_sem=frecv_ref.at[j],
                                               device_id=sib, device_id_type=MESH)
            fwd.wait_send()
            fwd.wait_recv()

    return pl.pallas_call(
        body, name=name, out_shape=(pltpu.HBM(src.shape, src.dtype), pltpu.HBM(land.shape, land.dtype)),
        in_specs=(HBM, HBM, SEM, SEM, SEM, SEM), out_specs=(HBM, HBM), input_output_aliases={0: 0, 1: 1},
        compiler_params=pltpu.CompilerParams(has_side_effects=EFFECT),
    )(src, land, send, recv_sib, fsend, frecv)[1]


def _acc_steps(p, acc, k, nk, finish):
    if nk == 1:
        finish(p)
        return

    @pl.when(k == 0)
    def _():
        acc[...] = p

    @pl.when(k > 0)
    def _():
        acc[...] += p

    @pl.when(k == nk - 1)
    def _():
        finish(acc[...])


def mm_nn(a, w3, name, out_dtype=F32, epi=None, extras=()):
    M, K = a.shape
    J, _, n = w3.shape
    tm, tn, tk = min(1024, M), min(1024, n), min(2048, K)
    q, nk, ne = n // tn, K // tk, len(extras)

    def body(*refs):
        a_ref, w_ref = refs[:2]
        ex, o_ref, acc = refs[2:2 + ne], refs[2 + ne], refs[-1]
        p = jnp.dot(a_ref[...], w_ref[...], preferred_element_type=F32)

        def finish(r):
            if epi is not None:
                r = epi(r, *[e[...] for e in ex])
            o_ref[...] = r.astype(out_dtype)

        _acc_steps(p, acc, pl.program_id(2), nk, finish)

    return pl.pallas_call(
        body, name=name, grid=(M // tm, J * q, nk),
        in_specs=[pl.BlockSpec((tm, tk), lambda i, j, k: (i, k)),
                  pl.BlockSpec((None, tk, tn), lambda i, j, k: (j // q, k, j % q))]
        + [pl.BlockSpec((tm, tn), lambda i, j, k: (i, j))] * ne,
        out_specs=pl.BlockSpec((tm, tn), lambda i, j, k: (i, j)),
        out_shape=jax.ShapeDtypeStruct((M, J * n), out_dtype),
        scratch_shapes=[pltpu.VMEM((tm, tn), F32)],
        compiler_params=_params(("parallel", "parallel", "arbitrary")),
    )(a, w3, *extras)


def mm_nt(dy, w3, name, out_dtype=F32, epi=None, extras=(), dep=None):
    M, _ = dy.shape
    J, K, n = w3.shape
    tm, tn, tkk = min(1024, M), min(MAX_CONTRACT, n), min(1024, K)
    q, ne = n // tn, len(extras)
    s = 1
    while q == 1 and J % (2 * s) == 0 and 2 * s * tn <= MAX_CONTRACT:
        s *= 2
    nk = (J // s) * q
    deps = () if dep is None else (dep,)

    def body(*refs):
        d_ref, w_ref = refs[:2]
        ex, o_ref, acc = refs[2:2 + ne], refs[-2], refs[-1]
        nt = (((1,), (1,)), ((), ()))
        p = lax.dot_general(d_ref[:, 0:tn], w_ref[0], nt, preferred_element_type=F32)
        for j in range(1, s):
            p = p + lax.dot_general(d_ref[:, j * tn:(j + 1) * tn], w_ref[j], nt, preferred_element_type=F32)

        def finish(r):
            if epi is not None:
                r = epi(r, *[e[...] for e in ex])
            o_ref[...] = r.astype(out_dtype)

        _acc_steps(p, acc, pl.program_id(2), nk, finish)

    return pl.pallas_call(
        body, name=name, grid=(M // tm, K // tkk, nk),
        in_specs=[pl.BlockSpec((tm, s * tn), lambda i, kk, c: (i, c)),
                  pl.BlockSpec((s, tkk, tn), lambda i, kk, c: (c // q, kk, c % q))]
        + [pl.BlockSpec((tm, tkk), lambda i, kk, c: (i, kk))] * ne + [ANY] * len(deps),
        out_specs=pl.BlockSpec((tm, tkk), lambda i, kk, c: (i, kk)),
        out_shape=jax.ShapeDtypeStruct((M, K), out_dtype),
        scratch_shapes=[pltpu.VMEM((tm, tkk), F32)],
        compiler_params=_params(("parallel", "parallel", "arbitrary")),
    )(dy, w3, *extras, *deps)


def mm_tn(a, dy, J, name, out_dtype=F32, dep=None):
    M, K = a.shape
    n = dy.shape[1] // J
    tm, tn, tkk = min(MAX_CONTRACT, M), min(1024, n), min(1024, K)
    q, nk = n // tn, M // tm
    deps = () if dep is None else (dep,)

    def body(a_ref, d_ref, *rest):
        o_ref, acc = rest[-2:]
        p = lax.dot_general(a_ref[...], d_ref[...], (((0,), (0,)), ((), ())), preferred_element_type=F32)

        def finish(r):
            o_ref[...] = r.astype(out_dtype)

        _acc_steps(p, acc, pl.program_id(2), nk, finish)

    return pl.pallas_call(
        body, name=name, grid=(K // tkk, J * q, nk),
        in_specs=[pl.BlockSpec((tm, tkk), lambda kk, c, m: (m, kk)),
                  pl.BlockSpec((tm, tn), lambda kk, c, m: (m, c))] + [ANY] * len(deps),
        out_specs=pl.BlockSpec((None, tkk, tn), lambda kk, c, m: (c // q, kk, c % q)),
        out_shape=jax.ShapeDtypeStruct((J, K, n), out_dtype),
        scratch_shapes=[pltpu.VMEM((tkk, tn), F32)],
        compiler_params=_params(("parallel", "parallel", "arbitrary")),
    )(a, dy, *deps)


def _tm(L):
    return min(256, L)


def _row(w, cb=0, tm=None):
    return pl.BlockSpec((tm, w), lambda i: (i, cb))


def _vec(w, cb=0):
    return pl.BlockSpec((1, w), lambda i: (0, cb))


def _accum(ref, val, i):
    @pl.when(i == 0)
    def _():
        ref[...] = val

    @pl.when(i > 0)
    def _():
        ref[...] += val


def _colsum(v):
    return jnp.sum(v, axis=0, keepdims=True)


def _rms(v):
    return lax.rsqrt(jnp.mean(v * v, axis=-1, keepdims=True) + EPS)


def adaln_mod(c_all, w_ada, b_cols):
    B, D = c_all.shape
    n = w_ada.shape[1]
    tn = 512

    def body(c_ref, w_ref, b_ref, o_ref, ca_ref):
        cv = c_ref[...]
        ca = cv * _sigmoid(cv)
        ca_ref[...] = ca
        o_ref[...] = jnp.dot(ca.astype(BF16), w_ref[...].astype(BF16), preferred_element_type=F32) + b_ref[...]

    return pl.pallas_call(
        body, name="adaln_mod", grid=(n // tn,),
        in_specs=[pl.BlockSpec((B, D), lambda j: (0, 0)), pl.BlockSpec((D, tn), lambda j: (0, j)),
                  pl.BlockSpec((1, tn), lambda j: (0, j))],
        out_specs=(pl.BlockSpec((B, tn), lambda j: (0, j)), pl.BlockSpec((B, D), lambda j: (0, 0))),
        out_shape=(jax.ShapeDtypeStruct((B, n), F32), jax.ShapeDtypeStruct((B, D), F32)),
        compiler_params=_params(("arbitrary",)),
    )(c_all, w_ada, b_cols)


def prenorm(x, g, scale, shift, name):
    L, D = x.shape
    tm = _tm(L)

    def body(x_ref, g_ref, sc_ref, sh_ref, u_ref):
        xv = x_ref[...]
        u_ref[...] = (xv * _rms(xv) * g_ref[...] * (1.0 + sc_ref[...]) + sh_ref[...]).astype(BF16)

    return pl.pallas_call(
        body, name=name, grid=(L // tm,),
        in_specs=[_row(D, tm=tm), _vec(D), _vec(D), _vec(D)],
        out_specs=_row(D, tm=tm), out_shape=jax.ShapeDtypeStruct((L, D), BF16),
        compiler_params=_params(("parallel",)),
    )(x, g, scale, shift)


def _shifted_copies(buf, shifted, tm):
    n = HALO + tm - 8
    for r in range(1, 8):
        shifted[r - 1] = buf[pl.ds(r, n), :]


def _window(buf, shifted, off, tm):
    r, base = off % 8, off - off % 8
    if r == 0:
        return buf[pl.ds(base, tm), :]
    return shifted[r - 1, pl.ds(base, tm), :]


def conv_fwd(proj, w_dw, b_dw, ln_g, ln_b):
    L = proj.shape[0]
    C = w_dw.shape[1]
    tm = _tm(L)
    hb = tm // HALO

    def body(a_ref, g_ref, ah_ref, gh_ref, w_ref, b_ref, lg_ref, lb_ref, vs_ref, vc_ref, buf, shifted):
        i = pl.program_id(0)
        halo = ah_ref[...] * _sigmoid(gh_ref[...])
        buf[0:HALO, :] = halo * jnp.where(i > 0, 1.0, 0.0)
        buf[HALO:HALO + tm, :] = a_ref[...] * _sigmoid(g_ref[...])
        _shifted_copies(buf, shifted, tm)
        acc = jnp.zeros((tm, C), F32) + b_ref[...]
        for k in range(CONV_K):
            acc = acc + w_ref[k:k + 1, :] * _window(buf, shifted, HALO - (CONV_K - 1) + k, tm)
        vc_ref[...] = acc
        mu = jnp.mean(acc, axis=-1, keepdims=True)
        d = acc - mu
        var = jnp.mean(d * d, axis=-1, keepdims=True)
        ln = d * lax.rsqrt(var + EPS) * lg_ref[...] + lb_ref[...]
        vs_ref[...] = (ln * _sigmoid(ln)).astype(BF16)

    prev = lambda cb: pl.BlockSpec((HALO, C), lambda i: (jnp.maximum(i * hb - 1, 0), cb))
    return pl.pallas_call(
        body, name="conv_fwd", grid=(L // tm,),
        in_specs=[_row(C, 0, tm), _row(C, 1, tm), prev(0), prev(1),
                  pl.BlockSpec((HALO, C), lambda i: (0, 0)), _vec(C), _vec(C), _vec(C)],
        out_specs=(_row(C, tm=tm), _row(C, tm=tm)),
        out_shape=(jax.ShapeDtypeStruct((L, C), BF16), jax.ShapeDtypeStruct((L, C), F32)),
        scratch_shapes=[pltpu.VMEM((HALO + tm, C), F32), pltpu.VMEM((7, HALO + tm - 8, C), F32)],
        compiler_params=_params(("parallel",)),
    )(proj, proj, proj, proj, w_dw, b_dw, ln_g, ln_b)


def _gelu(v):
    return 0.5 * v * (1.0 + jnp.tanh(math.sqrt(2.0 / math.pi) * (v + 0.044715 * v * v * v)))


def _gelu_grad(v):
    k = math.sqrt(2.0 / math.pi)
    t = jnp.tanh(k * (v + 0.044715 * v * v * v))
    return 0.5 * (1.0 + t) + 0.5 * v * (1.0 - t * t) * k * (1.0 + 3.0 * 0.044715 * v * v)


def s5_param_fn(ar, ai, ldt, br, bi, expand):
    dt = jnp.exp(ldt)
    er = jnp.exp(ar * dt)
    th = ai * dt
    lbr, lbi = er * jnp.cos(th), er * jnp.sin(th)
    nr, ni = lbr - 1.0, lbi
    den = ar * ar + ai * ai
    qr, qi = (nr * ar + ni * ai) / den, (ni * ar - nr * ai) / den
    qre = jnp.dot(expand, qr, precision=HI, preferred_element_type=F32)
    qie = jnp.dot(expand, qi, precision=HI, preferred_element_type=F32)
    return lbr, lbi, qre * br - qie * bi, qre * bi + qie * br


def s5_params(ar, ai, ldt, br2, bi2, expand):
    def body(ar_ref, ai_ref, ld_ref, br_ref, bi_ref, e_ref, o1, o2, o3, o4):
        r = s5_param_fn(ar_ref[...], ai_ref[...], ld_ref[...], br_ref[...], bi_ref[...], e_ref[...])
        o1[...], o2[...], o3[...], o4[...] = r

    s2, s3 = jax.ShapeDtypeStruct(ar.shape, F32), jax.ShapeDtypeStruct(br2.shape, F32)
    return pl.pallas_call(body, name="s5_params", out_shape=(s2, s2, s3, s3), compiler_params=_params())(
        ar, ai, ldt, br2, bi2, expand)


def s5_params_bwd(ar, ai, ldt, br2, bi2, expand, dlr, dli, dbr, dbi):
    def body(ar_ref, ai_ref, ld_ref, br_ref, bi_ref, e_ref, c1, c2, c3, c4, o1, o2, o3, o4, o5):
        e = e_ref[...]
        fn = lambda a, b, c, d, f: s5_param_fn(a, b, c, d, f, e)
        _, vjp = jax.vjp(fn, ar_ref[...], ai_ref[...], ld_ref[...], br_ref[...], bi_ref[...])
        r = vjp((c1[...], c2[...], c3[...], c4[...]))
        o1[...], o2[...], o3[...], o4[...], o5[...] = r

    shapes = tuple(jax.ShapeDtypeStruct(v.shape, F32) for v in (ar, ai, ldt, br2, bi2))
    return pl.pallas_call(body, name="s5_params_bwd", out_shape=shapes, compiler_params=_params())(
        ar, ai, ldt, br2, bi2, expand, dlr, dli, dbr, dbi)


def s5_tables(lr, li):
    C = lr.shape[1]

    def body(lr_ref, li_ref, o_ref):
        row = lax.broadcasted_iota(jnp.int32, (8, C), 0)
        for rev in (0, 1):
            pr = jnp.broadcast_to(lr_ref[...], (8, C))
            pi = jnp.broadcast_to(-li_ref[...] if rev else li_ref[...], (8, C))
            br, bi = pr, pi
            pows = [(pr, pi)]
            for _ in range(7):
                pr, pi = pr * br - pi * bi, pr * bi + pi * br
                pows.append((pr, pi))
            base = 8 * rev
            for s, d in enumerate((1, 2, 4)):
                keep = (row + d <= 7) if rev else (row >= d)
                o_ref[base + 2 * s] = jnp.where(keep, pows[d - 1][0], 0.0)
                o_ref[base + 2 * s + 1] = jnp.where(keep, pows[d - 1][1], 0.0)
            cr, ci = jnp.zeros((8, C), F32), jnp.zeros((8, C), F32)
            for j in range(8):
                e = (8 - j) if rev else (j + 1)
                cr = jnp.where(row == j, pows[e - 1][0], cr)
                ci = jnp.where(row == j, pows[e - 1][1], ci)
            o_ref[base + 6] = cr
            o_ref[base + 7] = ci

    return pl.pallas_call(body, name="s5_tables", out_shape=jax.ShapeDtypeStruct((16, 8, C), F32),
                          compiler_params=_params())(lr, li)


def _scan_tile(xr, xi, tabs, cr, ci, rev):
    for s, d in enumerate((1, 2, 4)):
        tr, ti = tabs[2 * s], tabs[2 * s + 1]
        sh = (8 - d) if rev else d
        sr, si = pltpu.roll(xr, sh, 0), pltpu.roll(xi, sh, 0)
        xr, xi = xr + tr * sr - ti * si, xi + tr * si + ti * sr
    tr, ti = tabs[6], tabs[7]
    xr, xi = xr + tr * cr - ti * ci, xi + tr * ci + ti * cr
    return xr, xi


def _hi_lo(a):
    hi = a.astype(BF16)
    return hi, (a - hi.astype(F32)).astype(BF16)


def _lhs3(a):
    hi, lo = _hi_lo(a)
    return jnp.concatenate([hi, lo, hi], axis=1)


def _rhs3(m):
    hi, lo = _hi_lo(m)
    return jnp.concatenate([hi, hi, lo], axis=-2)


def s5_fwd(proj, col0, bdr3, bdi3, cd2, tabs, d_skip):
    L = proj.shape[0]
    nb, cw3, sw = bdr3.shape
    cw = cw3 // 3
    tl = min(S5_ROWS, L)
    cb0 = col0 // cw

    def body(u_ref, bdr_ref, bdi_ref, cd_ref, t_ref, dk_ref, sr_ref, si_ref, yp_ref, yg_ref, car):
        l = pl.program_id(1)

        @pl.when(l == 0)
        def _():
            car[...] = jnp.zeros_like(car)

        u = u_ref[...]
        u3 = _lhs3(u)
        sr_ref[...] = jnp.dot(u3, bdr_ref[...], preferred_element_type=F32)
        si_ref[...] = jnp.dot(u3, bdi_ref[...], preferred_element_type=F32)

        def tile(i, c):
            tabs = [t_ref[j] for j in range(8)]
            r0 = pl.multiple_of(i * 8, 8)
            xr, xi = _scan_tile(sr_ref[pl.ds(r0, 8), :], si_ref[pl.ds(r0, 8), :], tabs, c[0], c[1], False)
            sr_ref[pl.ds(r0, 8), :] = xr
            si_ref[pl.ds(r0, 8), :] = xi
            return xr[7:8, :], xi[7:8, :]

        c = lax.fori_loop(0, tl // 8, tile, (car[0:1, :], car[1:2, :]))
        car[0:1, :] = c[0]
        car[1:2, :] = c[1]
        s2 = jnp.concatenate([sr_ref[...].astype(BF16), si_ref[...].astype(BF16)], axis=1)
        y = jnp.dot(s2, cd_ref[...], preferred_element_type=F32) + dk_ref[...] * u
        yp_ref[...] = y
        yg_ref[...] = _gelu(y).astype(BF16)

    blk = lambda r, c: pl.BlockSpec((None, r, c), lambda b, l: (b, 0, 0))
    return pl.pallas_call(
        body, name="s5_fwd", grid=(nb, L // tl),
        in_specs=[pl.BlockSpec((tl, cw), lambda b, l: (l, cb0 + b)), blk(cw3, sw), blk(cw3, sw), blk(2 * sw, cw),
                  pl.BlockSpec((8, 8, sw), lambda b, l: (0, 0, b)), pl.BlockSpec((1, cw), lambda b, l: (0, b))],
        out_specs=(pl.BlockSpec((tl, sw), lambda b, l: (l, b)), pl.BlockSpec((tl, sw), lambda b, l: (l, b)),
                   pl.BlockSpec((tl, cw), lambda b, l: (l, b)), pl.BlockSpec((tl, cw), lambda b, l: (l, b))),
        out_shape=(jax.ShapeDtypeStruct((L, nb * sw), F32), jax.ShapeDtypeStruct((L, nb * sw), F32),
                   jax.ShapeDtypeStruct((L, nb * cw), F32), jax.ShapeDtypeStruct((L, nb * cw), BF16)),
        scratch_shapes=[pltpu.VMEM((8, sw), F32)],
        compiler_params=_params(("parallel", "arbitrary")),
    )(proj, bdr3, bdi3, cd2, tabs, d_skip)


def s5_bwd(dyg_a, dyg_b, yp, proj, col0, s_re, s_im, bdt2, cdrt3, cdit3, tabs, d_skip):
    L = proj.shape[0]
    nb, sw2, cw = bdt2.shape
    sw = sw2 // 2
    tl = min(S5_ROWS, L)
    nl = L // tl
    cb0 = col0 // cw
    tb = tl // 8

    def body(da_ref, db_ref, yp_ref, u_ref, sr_ref, si_ref, hr_ref, hi_ref, bdt_ref, cdrt_ref, cdit_ref,
             t_ref, dk_ref, du_ref, ddk_ref, dcr_ref, dci_ref, dbr_ref, dbi_ref, dlr_ref, dli_ref,
             gr, gi, pr, pi, car):
        l = pl.program_id(1)
        first = l == nl - 1

        @pl.when(l == 0)
        def _():
            car[...] = jnp.zeros_like(car)

        u = u_ref[...]
        dy = (da_ref[...] + db_ref[...]) * _gelu_grad(yp_ref[...])
        dy3 = _lhs3(dy)
        gr[...] = jnp.dot(dy3, cdrt_ref[...], preferred_element_type=F32)
        gi[...] = jnp.dot(dy3, cdit_ref[...], preferred_element_type=F32)
        inner = jnp.where(first, 0.0, 1.0)
        pr[0:8, :] = hr_ref[...] * inner
        pi[0:8, :] = hi_ref[...] * inner
        pr[8:8 + tl, :] = sr_ref[...]
        pi[8:8 + tl, :] = si_ref[...]
        row = lax.broadcasted_iota(jnp.int32, (8, sw), 0)

        def tile(j, c):
            tabs = [t_ref[8 + k] for k in range(8)]
            r0 = pl.multiple_of((tb - 1 - j) * 8, 8)
            xr, xi = _scan_tile(gr[pl.ds(r0, 8), :], gi[pl.ds(r0, 8), :], tabs, c[0], c[1], True)
            gr[pl.ds(r0, 8), :] = xr
            gi[pl.ds(r0, 8), :] = xi
            qr = jnp.where(row == 0, pltpu.roll(pr[pl.ds(r0, 8), :], 1, 0), pltpu.roll(pr[pl.ds(r0 + 8, 8), :], 1, 0))
            qi = jnp.where(row == 0, pltpu.roll(pi[pl.ds(r0, 8), :], 1, 0), pltpu.roll(pi[pl.ds(r0 + 8, 8), :], 1, 0))
            return xr[0:1, :], xi[0:1, :], c[2] + xr * qr + xi * qi, c[3] + xi * qr - xr * qi

        z = jnp.zeros((8, sw), F32)
        c = lax.fori_loop(0, tb, tile, (car[0:1, :], car[1:2, :], z, z))
        car[0:1, :] = c[0]
        car[1:2, :] = c[1]
        g_re, g_im = gr[...].astype(BF16), gi[...].astype(BF16)
        g2 = jnp.concatenate([g_re, g_im], axis=1)
        du_ref[...] = (dy * dk_ref[...] + jnp.dot(g2, bdt_ref[...], preferred_element_type=F32)).astype(BF16)
        tn = (((0,), (0,)), ((), ()))
        dyb, ub = dy.astype(BF16), u.astype(BF16)
        _accum(ddk_ref, _colsum(dy * u), l)
        _accum(dcr_ref, lax.dot_general(dyb, sr_ref[...].astype(BF16), tn, preferred_element_type=F32), l)
        _accum(dci_ref, -lax.dot_general(dyb, si_ref[...].astype(BF16), tn, preferred_element_type=F32), l)
        _accum(dbr_ref, lax.dot_general(ub, g_re, tn, preferred_element_type=F32), l)
        _accum(dbi_ref, lax.dot_general(ub, g_im, tn, preferred_element_type=F32), l)
        _accum(dlr_ref, c[2], l)
        _accum(dli_ref, c[3], l)

    rl = lambda l: nl - 1 - l
    cblk = lambda w, off=0: pl.BlockSpec((tl, w), lambda b, l: (rl(l), off + b))
    halo = pl.BlockSpec((8, sw), lambda b, l: (jnp.maximum(rl(l) * tb - 1, 0), b))
    mat = lambda r, c: pl.BlockSpec((None, r, c), lambda b, l: (b, 0, 0))
    return pl.pallas_call(
        body, name="s5_bwd", grid=(nb, nl),
        in_specs=[cblk(cw), cblk(cw), cblk(cw), cblk(cw, cb0), cblk(sw), cblk(sw), halo, halo,
                  mat(2 * sw, cw), mat(3 * cw, sw), mat(3 * cw, sw),
                  pl.BlockSpec((16, 8, sw), lambda b, l: (0, 0, b)), pl.BlockSpec((1, cw), lambda b, l: (0, b))],
        out_specs=(cblk(cw), pl.BlockSpec((1, cw), lambda b, l: (0, b)), mat(cw, sw), mat(cw, sw), mat(cw, sw), mat(cw, sw),
                   pl.BlockSpec((8, sw), lambda b, l: (0, b)), pl.BlockSpec((8, sw), lambda b, l: (0, b))),
        out_shape=(jax.ShapeDtypeStruct((L, nb * cw), BF16), jax.ShapeDtypeStruct((1, nb * cw), F32),
                   jax.ShapeDtypeStruct((nb, cw, sw), F32), jax.ShapeDtypeStruct((nb, cw, sw), F32),
                   jax.ShapeDtypeStruct((nb, cw, sw), F32), jax.ShapeDtypeStruct((nb, cw, sw), F32),
                   jax.ShapeDtypeStruct((8, nb * sw), F32), jax.ShapeDtypeStruct((8, nb * sw), F32)),
        scratch_shapes=[pltpu.VMEM((tl, sw), F32), pltpu.VMEM((tl, sw), F32),
                        pltpu.VMEM((tl + 8, sw), F32), pltpu.VMEM((tl + 8, sw), F32), pltpu.VMEM((8, sw), F32)],
        compiler_params=_params(("parallel", "arbitrary")),
    )(dyg_a, dyg_b, yp, proj, s_re, s_im, s_re, s_im, bdt2, cdrt3, cdit3, tabs, d_skip)


def merge_fwd(proj, col_gc, y_conv, ga, gb):
    L, D = y_conv.shape
    tm = _tm(L)
    h = D // 2
    c0 = col_gc // h

    def body(p0, p1, p2, p3, yc_ref, ga_ref, gb_ref, o_ref):
        gc, gs = (p0, p1), (p2, p3)
        for s in range(2):
            cols = slice(s * h, (s + 1) * h)
            y_ssm = ga_ref[:, cols] * _sigmoid(gb_ref[:, cols])
            o_ref[:, cols] = (_sigmoid(gc[s][...]) * yc_ref[:, cols] + _sigmoid(gs[s][...]) * y_ssm).astype(BF16)

    return pl.pallas_call(
        body, name="merge_fwd", grid=(L // tm,),
        in_specs=[_row(h, c0 + s, tm) for s in range(4)] + [_row(D, tm=tm)] * 3,
        out_specs=_row(D, tm=tm), out_shape=jax.ShapeDtypeStruct((L, D), BF16),
        compiler_params=_params(("parallel",)),
    )(proj, proj, proj, proj, y_conv, ga, gb)


def residual_norm(x, m_out, gate, g, scale, shift):
    L, D = x.shape
    tm = _tm(L)

    def body(x_ref, m_ref, gt_ref, g_ref, sc_ref, sh_ref, h_ref, z_ref):
        h = x_ref[...] + gt_ref[...] * m_ref[...]
        h_ref[...] = h
        z_ref[...] = (h * _rms(h) * g_ref[...] * (1.0 + sc_ref[...]) + sh_ref[...]).astype(BF16)

    return pl.pallas_call(
        body, name="residual_norm", grid=(L // tm,),
        in_specs=[_row(D, tm=tm), _row(D, tm=tm), _vec(D), _vec(D), _vec(D), _vec(D)],
        out_specs=(_row(D, tm=tm), _row(D, tm=tm)),
        out_shape=(jax.ShapeDtypeStruct((L, D), F32), jax.ShapeDtypeStruct((L, D), BF16)),
        compiler_params=_params(("parallel",)),
    )(x, m_out, gate, g, scale, shift)


def loss_bwd(h1, ff, gate2, final_g, target):
    L, D = h1.shape
    tm = _tm(L)

    def body(h_ref, f_ref, gt_ref, g_ref, t_ref, dh_ref, dff_ref, loss_ref, dg_ref, dgt_ref):
        i = pl.program_id(0)
        ffv = f_ref[...]
        h2 = h_ref[...] + gt_ref[...] * ffv
        r = _rms(h2)
        n = h2 * r
        err = n * g_ref[...] - t_ref[...]
        per_tok = jnp.mean(err * err, axis=-1, keepdims=True)
        _accum(loss_ref, 0.5 * jnp.sum(per_tok, axis=0, keepdims=True), i)
        dy = err * (1.0 / D)
        _accum(dg_ref, _colsum(dy * n), i)
        dn = dy * g_ref[...]
        dh2 = r * (dn - n * jnp.mean(dn * n, axis=-1, keepdims=True))
        dh_ref[...] = dh2
        dff_ref[...] = (gt_ref[...] * dh2).astype(BF16)
        _accum(dgt_ref, _colsum(dh2 * ffv), i)

    return pl.pallas_call(
        body, name="loss_bwd", grid=(L // tm,),
        in_specs=[_row(D, tm=tm), _row(D, tm=tm), _vec(D), _vec(D), _row(D, tm=tm)],
        out_specs=(_row(D, tm=tm), _row(D, tm=tm), pl.BlockSpec((1, 1), lambda i: (0, 0)), _vec(D), _vec(D)),
        out_shape=(jax.ShapeDtypeStruct((L, D), F32), jax.ShapeDtypeStruct((L, D), BF16),
                   jax.ShapeDtypeStruct((1, 1), F32), jax.ShapeDtypeStruct((1, D), F32), jax.ShapeDtypeStruct((1, D), F32)),
        compiler_params=_params(("arbitrary",)),
    )(h1, ff, gate2, final_g, target)


def norm_bwd(dz, h, dh_in, g, scale, name, gate=None, m_out=None):
    L, D = h.shape
    tm = _tm(L)
    tail = gate is not None

    def body(*refs):
        dz_ref, h_ref, di_ref, g_ref, sc_ref = refs[:5]
        rest = refs[5:]
        if tail:
            gt_ref, m_ref = rest[:2]
            rest = rest[2:]
        dh_ref, dsc_ref, dsh_ref, dg_ref = rest[:4]
        i = pl.program_id(0)
        hv, dzv = h_ref[...], dz_ref[...]
        r = _rms(hv)
        n = hv * r
        _accum(dsc_ref, _colsum(dzv * n * g_ref[...]), i)
        _accum(dsh_ref, _colsum(dzv), i)
        dzn = dzv * (1.0 + sc_ref[...])
        _accum(dg_ref, _colsum(dzn * n), i)
        dn = dzn * g_ref[...]
        dh = di_ref[...] + r * (dn - n * jnp.mean(dn * n, axis=-1, keepdims=True))
        dh_ref[...] = dh
        if tail:
            dmo_ref, dgt_ref = rest[4:]
            dmo_ref[...] = (gt_ref[...] * dh).astype(BF16)
            _accum(dgt_ref, _colsum(dh * m_ref[...]), i)

    ins = [dz, h, dh_in, g, scale]
    in_specs = [_row(D, tm=tm)] * 3 + [_vec(D)] * 2
    out_specs = [_row(D, tm=tm), _vec(D), _vec(D), _vec(D)]
    out_shape = [jax.ShapeDtypeStruct((L, D), F32)] + [jax.ShapeDtypeStruct((1, D), F32)] * 3
    if tail:
        ins += [gate, m_out]
        in_specs += [_vec(D), _row(D, tm=tm)]
        out_specs += [_row(D, tm=tm), _vec(D)]
        out_shape += [jax.ShapeDtypeStruct((L, D), BF16), jax.ShapeDtypeStruct((1, D), F32)]
    return pl.pallas_call(
        body, name=name, grid=(L // tm,), in_specs=in_specs, out_specs=tuple(out_specs), out_shape=tuple(out_shape),
        compiler_params=_params(("arbitrary",)),
    )(*ins)


def merge_bwd(dmerged, proj, col_gc, y_conv, ga, gb):
    L, D = y_conv.shape
    tm = _tm(L)
    h = D // 2
    c0 = col_gc // h

    def body(dm_ref, p0, p1, p2, p3, yc_ref, ga_ref, gb_ref, dyc_ref, dga_ref, dgb_ref, dg_ref):
        gc, gs = (p0, p1), (p2, p3)
        for s in range(2):
            cols = slice(s * h, (s + 1) * h)
            dm = dm_ref[:, cols]
            sc, ss, sb = _sigmoid(gc[s][...]), _sigmoid(gs[s][...]), _sigmoid(gb_ref[:, cols])
            gav = ga_ref[:, cols]
            dyc_ref[:, cols] = (dm * sc).astype(BF16)
            dg_ref[:, cols] = (dm * yc_ref[:, cols] * sc * (1.0 - sc)).astype(BF16)
            dg_ref[:, D + s * h:D + (s + 1) * h] = (dm * gav * sb * ss * (1.0 - ss)).astype(BF16)
            dys = dm * ss
            dga_ref[:, cols] = (dys * sb).astype(BF16)
            dgb_ref[:, cols] = (dys * gav * sb * (1.0 - sb)).astype(BF16)

    return pl.pallas_call(
        body, name="merge_bwd", grid=(L // tm,),
        in_specs=[_row(D, tm=tm)] + [_row(h, c0 + s, tm) for s in range(4)] + [_row(D, tm=tm)] * 3,
        out_specs=(_row(D, tm=tm), _row(D, tm=tm), _row(D, tm=tm), _row(2 * D, tm=tm)),
        out_shape=(jax.ShapeDtypeStruct((L, D), BF16),) * 3 + (jax.ShapeDtypeStruct((L, 2 * D), BF16),),
        compiler_params=_params(("parallel",)),
    )(dmerged, proj, proj, proj, proj, y_conv, ga, gb)


def conv_ln_bwd(dvs, vc, ln_g, ln_b):
    L, C = vc.shape
    tm = _tm(L)

    def body(d_ref, v_ref, g_ref, b_ref, o_ref, dg_ref, db_ref):
        i = pl.program_id(0)
        v = v_ref[...]
        mu = jnp.mean(v, axis=-1, keepdims=True)
        d = v - mu
        rstd = lax.rsqrt(jnp.mean(d * d, axis=-1, keepdims=True) + EPS)
        xh = d * rstd
        ln = xh * g_ref[...] + b_ref[...]
        sg = _sigmoid(ln)
        dln = d_ref[...] * sg * (1.0 + ln * (1.0 - sg))
        _accum(dg_ref, _colsum(dln * xh), i)
        _accum(db_ref, _colsum(dln), i)
        dxh = dln * g_ref[...]
        o_ref[...] = rstd * (dxh - jnp.mean(dxh, axis=-1, keepdims=True)
                             - xh * jnp.mean(dxh * xh, axis=-1, keepdims=True))

    return pl.pallas_call(
        body, name="conv_ln_bwd", grid=(L // tm,),
        in_specs=[_row(C, tm=tm), _row(C, tm=tm), _vec(C), _vec(C)],
        out_specs=(_row(C, tm=tm), _vec(C), _vec(C)),
        out_shape=(jax.ShapeDtypeStruct((L, C), F32), jax.ShapeDtypeStruct((1, C), F32), jax.ShapeDtypeStruct((1, C), F32)),
        compiler_params=_params(("arbitrary",)),
    )(dvs, vc, ln_g, ln_b)


def conv_bwd(dvc, proj, w_dw):
    L, C = dvc.shape
    tm = _tm(L)
    hb = tm // HALO
    last = L // HALO - 1
    nt = L // tm

    def body(d_ref, dn_ref, a_ref, g_ref, ah_ref, gh_ref, w_ref, o_ref, dw_ref, db_ref, dbuf, vbuf, dsh, vsh, dw8):
        i = pl.program_id(0)
        dcur = d_ref[...]
        dbuf[0:tm, :] = dcur
        dbuf[tm:tm + HALO, :] = dn_ref[...] * jnp.where(i < nt - 1, 1.0, 0.0)
        av, sg = a_ref[...], _sigmoid(g_ref[...])
        vbuf[0:HALO, :] = ah_ref[...] * _sigmoid(gh_ref[...]) * jnp.where(i > 0, 1.0, 0.0)
        vbuf[HALO:HALO + tm, :] = av * sg
        _shifted_copies(dbuf, dsh, tm)
        _shifted_copies(vbuf, vsh, tm)
        dv = jnp.zeros((tm, C), F32)
        for k in range(CONV_K):
            dv = dv + w_ref[k:k + 1, :] * _window(dbuf, dsh, CONV_K - 1 - k, tm)
        o_ref[:, 0:C] = (dv * sg).astype(BF16)
        o_ref[:, C:2 * C] = (dv * av * sg * (1.0 - sg)).astype(BF16)

        @pl.when(i == 0)
        def _():
            dw8[...] = jnp.zeros_like(dw8)

        for k in range(CONV_K):
            prod = dcur * _window(vbuf, vsh, HALO - (CONV_K - 1) + k, tm)
            part = prod[0:8, :]
            for j in range(1, tm // 8):
                part = part + prod[8 * j:8 * j + 8, :]
            dw8[k] += part
        _accum(db_ref, _colsum(dcur), i)

        @pl.when(i == nt - 1)
        def _():
            for k in range(CONV_K):
                dw_ref[k:k + 1, :] = _colsum(dw8[k])
            dw_ref[CONV_K:HALO, :] = jnp.zeros((HALO - CONV_K, C), F32)

    prev = lambda cb: pl.BlockSpec((HALO, C), lambda i: (jnp.maximum(i * hb - 1, 0), cb))
    return pl.pallas_call(
        body, name="conv_bwd", grid=(nt,),
        in_specs=[_row(C, tm=tm), pl.BlockSpec((HALO, C), lambda i: (jnp.minimum((i + 1) * hb, last), 0)),
                  _row(C, 0, tm), _row(C, 1, tm), prev(0), prev(1), pl.BlockSpec((HALO, C), lambda i: (0, 0))],
        out_specs=(_row(2 * C, tm=tm), pl.BlockSpec((HALO, C), lambda i: (0, 0)), _vec(C)),
        out_shape=(jax.ShapeDtypeStruct((L, 2 * C), BF16), jax.ShapeDtypeStruct((HALO, C), F32),
                   jax.ShapeDtypeStruct((1, C), F32)),
        scratch_shapes=[pltpu.VMEM((tm + HALO, C), F32), pltpu.VMEM((HALO + tm, C), F32),
                        pltpu.VMEM((7, HALO + tm - 8, C), F32), pltpu.VMEM((7, HALO + tm - 8, C), F32),
                        pltpu.VMEM((CONV_K, 8, C), F32)],
        compiler_params=_params(("arbitrary",)),
    )(dvc, dvc, proj, proj, proj, proj, w_dw)


def _adamw(w, g, m, v):
    m = ADAM_B1 * m + (1.0 - ADAM_B1) * g
    v = ADAM_B2 * v + (1.0 - ADAM_B2) * (g * g)
    m_hat = m / (1.0 - ADAM_B1 ** ADAM_STEP)
    v_hat = v / (1.0 - ADAM_B2 ** ADAM_STEP)
    delta = -ADAM_LR * (m_hat / (jnp.sqrt(v_hat) + ADAM_EPS) + ADAM_WD * w)
    return delta, m, v


def _tile_rows(R, C):
    tr = 8
    while tr * 2 * C <= 128 * 1024 and R % (tr * 2) == 0:
        tr *= 2
    assert R % tr == 0, (R, C)
    return tr


def sum_devices(parts, name):
    _, R, C = parts.shape
    tr = _tile_rows(R, C)

    def body(p_ref, o_ref):
        s = p_ref[0]
        for j in range(1, NDEV):
            s = s + p_ref[j]
        o_ref[...] = s

    return pl.pallas_call(
        body, name=name, grid=(R // tr,),
        in_specs=[pl.BlockSpec((NDEV, tr, C), lambda i: (0, i, 0))],
        out_specs=pl.BlockSpec((tr, C), lambda i: (i, 0)), out_shape=jax.ShapeDtypeStruct((R, C), F32),
        compiler_params=_params(("parallel",)),
    )(parts)


def adam_update(w, g, m, v, name):
    R, C = w.shape
    tr = _tile_rows(R, C)

    def body(w_ref, g_ref, m_ref, v_ref, d_ref, mo_ref, vo_ref):
        d, mm, vv = _adamw(w_ref[...], g_ref[...], m_ref[...], v_ref[...])
        d_ref[...], mo_ref[...], vo_ref[...] = d, mm, vv

    spec = pl.BlockSpec((tr, C), lambda i: (i, 0))
    return pl.pallas_call(
        body, name=name, grid=(R // tr,), in_specs=[spec] * 4, out_specs=(spec,) * 3,
        out_shape=(jax.ShapeDtypeStruct((R, C), F32),) * 3, compiler_params=_params(("parallel",)),
    )(w, g, m, v)


def adam_reduce(parts, w, m, v, name):
    R, C = w.shape
    tr = _tile_rows(R, C)

    def body(p_ref, w_ref, m_ref, v_ref, g_ref, d_ref, mo_ref, vo_ref):
        g = p_ref[0].astype(F32)
        for j in range(1, NDEV):
            g = g + p_ref[j].astype(F32)
        g_ref[...] = g
        d, mm, vv = _adamw(w_ref[...], g, m_ref[...], v_ref[...])
        d_ref[...], mo_ref[...], vo_ref[...] = d, mm, vv

    spec = pl.BlockSpec((tr, C), lambda i: (i, 0))
    return pl.pallas_call(
        body, name=name, grid=(R // tr,),
        in_specs=[pl.BlockSpec((NDEV, tr, C), lambda i: (0, i, 0)), spec, spec, spec], out_specs=(spec,) * 4,
        out_shape=(jax.ShapeDtypeStruct((R, C), F32),) * 4, compiler_params=_params(("parallel",)),
    )(parts, w, m, v)


def adam_w_ada(c_act, dmod_cols, w, m, v):
    D, n = w.shape
    tn = 256

    def body(c_ref, dm_ref, w_ref, m_ref, v_ref, g_ref, d_ref, mo_ref, vo_ref):
        g = lax.dot_general(c_ref[...].astype(BF16), dm_ref[...].astype(BF16), (((0,), (0,)), ((), ())),
                            preferred_element_type=F32)
        g_ref[...] = g
        d, mm, vv = _adamw(w_ref[...], g, m_ref[...], v_ref[...])
        d_ref[...], mo_ref[...], vo_ref[...] = d, mm, vv

    spec = pl.BlockSpec((D, tn), lambda j: (0, j))
    return pl.pallas_call(
        body, name="adam_w_ada", grid=(n // tn,),
        in_specs=[pl.BlockSpec((NDEV, D), lambda j: (0, 0)), pl.BlockSpec((NDEV, tn), lambda j: (0, j)), spec, spec, spec],
        out_specs=(spec,) * 4, out_shape=(jax.ShapeDtypeStruct((D, n), F32),) * 4,
        compiler_params=_params(("parallel",)),
    )(c_act, dmod_cols, w, m, v)


def _block_diag(m):
    G, a, b = m.shape
    m4 = m.reshape(G // GB, GB, a, b)
    eye = jnp.eye(GB, dtype=m.dtype)
    return (m4[:, :, :, None, :] * eye[None, :, None, :, None]).reshape(G // GB, GB * a, GB * b)


def _diag_blocks(m, a, b):
    nb = m.shape[0]
    m5 = m.reshape(nb, GB, a, GB, b)
    idx = jnp.arange(GB)
    return m5[:, idx, :, idx, :].transpose(1, 0, 2, 3).reshape(nb * GB, a, b)


def _flat_pad(parts, mult):
    flat = jnp.concatenate([p.reshape(-1) for p in parts])
    pad = (-flat.shape[0]) % mult
    return jnp.pad(flat, (0, pad))


def _split(flat, like):
    out, off = [], 0
    for p in like:
        out.append(flat[off:off + p.size].reshape(p.shape))
        off += p.size
    return out


def kernel(x, c, w_ada, b_ada, norm1_g, w_in, w_dw, b_dw, ln_g, ln_b, w_conv_out, a_re, a_im, log_dt, b_re, b_im, c_re, c_im, d_skip, w_glu_a, w_glu_b, w_out, norm2_g, w_ff1, w_ff2, final_g, loss_target, m_w_ada, m_b_ada, m_norm1_g, m_w_in, m_w_dw, m_b_dw, m_ln_g, m_ln_b, m_w_conv_out, m_a_re, m_a_im, m_log_dt, m_b_re, m_b_im, m_c_re, m_c_im, m_d_skip, m_w_glu_a, m_w_glu_b, m_w_out, m_norm2_g, m_w_ff1, m_w_ff2, m_final_g, v_w_ada, v_b_ada, v_norm1_g, v_w_in, v_w_dw, v_b_dw, v_ln_g, v_ln_b, v_w_conv_out, v_a_re, v_a_im, v_log_dt, v_b_re, v_b_im, v_c_re, v_c_im, v_d_skip, v_w_glu_a, v_w_glu_b, v_w_out, v_norm2_g, v_w_ff1, v_w_ff2, v_final_g):
    W = dict(w_ada=w_ada, b_ada=b_ada, norm1_g=norm1_g, w_in=w_in, w_dw=w_dw, b_dw=b_dw, ln_g=ln_g, ln_b=ln_b,
             w_conv_out=w_conv_out, a_re=a_re, a_im=a_im, log_dt=log_dt, b_re=b_re, b_im=b_im, c_re=c_re, c_im=c_im,
             d_skip=d_skip, w_glu_a=w_glu_a, w_glu_b=w_glu_b, w_out=w_out, norm2_g=norm2_g, w_ff1=w_ff1, w_ff2=w_ff2,
             final_g=final_g)
    Mo = dict(w_ada=m_w_ada, b_ada=m_b_ada, norm1_g=m_norm1_g, w_in=m_w_in, w_dw=m_w_dw, b_dw=m_b_dw, ln_g=m_ln_g,
              ln_b=m_ln_b, w_conv_out=m_w_conv_out, a_re=m_a_re, a_im=m_a_im, log_dt=m_log_dt, b_re=m_b_re, b_im=m_b_im,
              c_re=m_c_re, c_im=m_c_im, d_skip=m_d_skip, w_glu_a=m_w_glu_a, w_glu_b=m_w_glu_b, w_out=m_w_out,
              norm2_g=m_norm2_g, w_ff1=m_w_ff1, w_ff2=m_w_ff2, final_g=m_final_g)
    Vo = dict(w_ada=v_w_ada, b_ada=v_b_ada, norm1_g=v_norm1_g, w_in=v_w_in, w_dw=v_w_dw, b_dw=v_b_dw, ln_g=v_ln_g,
              ln_b=v_ln_b, w_conv_out=v_w_conv_out, a_re=v_a_re, a_im=v_a_im, log_dt=v_log_dt, b_re=v_b_re, b_im=v_b_im,
              c_re=v_c_re, c_im=v_c_im, d_skip=v_d_skip, w_glu_a=v_w_glu_a, w_glu_b=v_w_glu_b, w_out=v_w_out,
              norm2_g=v_norm2_g, w_ff1=v_w_ff1, w_ff2=v_w_ff2, final_g=v_final_g)
    names = list(W)

    me = _me()
    xs, tgt = x[0], loss_target[0]
    L, D = xs.shape
    CW = w_dw.shape[2] * NDEV
    G, P = a_re.shape[1], a_re.shape[2]
    H = b_re.shape[3]
    n_ada = w_ada.shape[2]

    (c_all,) = _exchange([c], "gather_c", True)
    b_cols = lax.dynamic_slice(b_ada, (0, me * n_ada), (1, n_ada))
    mod_cols, c_act = adaln_mod(c_all.reshape(NDEV, D), w_ada[0], b_cols)
    (mod_all,) = _exchange([mod_cols], "gather_mod", True)
    mod = lax.dynamic_slice(mod_all, (0, me, 0), (NDEV, 1, n_ada)).reshape(6, 1, D)
    shift1, scale1, gate1, shift2, scale2, gate2 = [mod[j] for j in range(6)]

    big = ["w_in", "w_conv_out", "w_glu_a", "w_glu_b", "w_out", "w_ff1", "w_ff2"]
    order = ["w_dw"] + big[1:]
    shards = {k: W[k][0].astype(BF16) for k in big}
    shards["w_dw"] = jnp.pad(w_dw[0], ((0, HALO - CONV_K), (0, 0)))
    w_in_handle, w_in_token = gather2_start(shards["w_in"], "gather_w_in_start", mod_all)
    handles, _ = exchange_start([shards[k] for k in order], "gather_weights_start", True, after=w_in_token)
    gather_handle = dict(zip(order, handles))

    def weight(k, after):
        if k == "w_in":
            return gather2_wait(gather2_forward(w_in_handle, "gather_w_in_forward", after), "gather_w_in_wait")
        w = exchange_wait(gather_handle[k], after, "gather_wait_" + k, True)
        if k in ("w_out", "w_ff2"):
            w = w.reshape(1, w.shape[0] * w.shape[1], w.shape[2])
        elif k in narrow:
            w = w.transpose(1, 0, 2).reshape(1, w.shape[1], NDEV * w.shape[2])
        elif k == "w_dw":
            w = w.transpose(1, 0, 2).reshape(HALO, CW)
        return w

    narrow = ("w_conv_out", "w_glu_a", "w_glu_b")
    scatter_handle = {}

    def scatter(k, g):
        if k in narrow:
            g = g.reshape(g.shape[1], NDEV, -1).transpose(1, 0, 2)
        elif g.shape[0] == 1:
            g = g.reshape(NDEV, -1, g.shape[2])
        (scatter_handle[k],), token = exchange_start([g], "scatter_start_" + k, False)
        return token

    u = prenorm(xs, norm1_g, scale1, shift1, "prenorm1")
    wg = {"w_in": weight("w_in", u)}
    proj = mm_nn(u, wg["w_in"], "in_proj")
    w_dw_full = weight("w_dw", proj)
    vs, vc = conv_fwd(proj, w_dw_full, b_dw, ln_g, ln_b)
    wg["w_conv_out"] = weight("w_conv_out", vs)
    y_conv = mm_nn(vs, wg["w_conv_out"], "conv_out")

    br2 = b_re[0].transpose(0, 2, 1).reshape(G * H, P)
    bi2 = b_im[0].transpose(0, 2, 1).reshape(G * H, P)
    ldt = log_dt[0].reshape(G, 1)
    expand = jnp.repeat(jnp.eye(G, dtype=F32), H, axis=0)
    lbr, lbi, bbr, bbi = s5_params(a_re[0], a_im[0], ldt, br2, bi2, expand)
    tabs = s5_tables(lbr.reshape(1, G * P), lbi.reshape(1, G * P))
    bdr, bdi = _block_diag(bbr.reshape(G, H, P)), _block_diag(bbi.reshape(G, H, P))
    cdr = _block_diag(c_re[0].transpose(0, 2, 1))
    cdi = _block_diag(c_im[0].transpose(0, 2, 1))
    cd2 = jnp.concatenate([cdr, -cdi], axis=1).astype(BF16)
    s_re, s_im, y_pre, yg = s5_fwd(proj, 2 * CW, _rhs3(bdr), _rhs3(bdi), cd2, tabs, d_skip)
    wg["w_glu_a"] = weight("w_glu_a", yg)
    wg["w_glu_b"] = weight("w_glu_b", yg)
    ga = mm_nn(yg, wg["w_glu_a"], "glu_a")
    gb = mm_nn(yg, wg["w_glu_b"], "glu_b")
    merged = merge_fwd(proj, 3 * CW, y_conv, ga, gb)
    wg["w_out"] = weight("w_out", merged)
    m_out = mm_nn(merged, wg["w_out"], "out_proj")
    h1, z = residual_norm(xs, m_out, gate1, norm2_g, scale2, shift2)
    wg["w_ff1"] = weight("w_ff1", z)
    act = mm_nn(z, wg["w_ff1"], "ff1", out_dtype=BF16, epi=lambda r: jnp.square(jnp.maximum(r, 0.0)))
    wg["w_ff2"] = weight("w_ff2", act)
    ff = mm_nn(act, wg["w_ff2"], "ff2")

    dh2, dff, loss_part, d_final_g, d_gate2 = loss_bwd(h1, ff, gate2, final_g.reshape(1, D), tgt)
    df = mm_nt(dff, wg["w_ff2"], "ff2_dx", out_dtype=BF16,
               epi=lambda r, a: r * (2.0 * jnp.sqrt(a.astype(F32))), extras=(act,))
    t = scatter("w_ff2", mm_tn(act, dff, 1, "ff2_dw", out_dtype=BF16))
    t = scatter("w_ff1", mm_tn(z, df, NDEV, "ff1_dw", out_dtype=BF16, dep=t))
    dz = mm_nt(df, wg["w_ff1"], "ff1_dx", dep=t)
    dh1, d_scale2, d_shift2, d_norm2_g, dmo, d_gate1 = norm_bwd(dz, h1, dh2, norm2_g, scale2, "norm2_bwd", gate1, m_out)
    t = scatter("w_out", mm_tn(merged, dmo, 1, "out_dw", out_dtype=BF16))
    dmerged = mm_nt(dmo, wg["w_out"], "out_dx", dep=t)
    dyc, dga, dgb, dproj_g = merge_bwd(dmerged, proj, 3 * CW, y_conv, ga, gb)
    t = scatter("w_conv_out", mm_tn(vs, dyc, 1, "conv_out_dw", out_dtype=BF16))
    t = scatter("w_glu_a", mm_tn(yg, dga, 1, "glu_a_dw", out_dtype=BF16, dep=t))
    t = scatter("w_glu_b", mm_tn(yg, dgb, 1, "glu_b_dw", out_dtype=BF16, dep=t))
    dvs = mm_nt(dyc, wg["w_conv_out"], "conv_out_dx", dep=t)
    dyg_a = mm_nt(dga, wg["w_glu_a"], "glu_a_dx", dep=t)
    dyg_b = mm_nt(dgb, wg["w_glu_b"], "glu_b_dx", dep=t)
    dvc, d_ln_g, d_ln_b = conv_ln_bwd(dvs, vc, ln_g, ln_b)
    dproj_c, d_w_dw, d_b_dw = conv_bwd(dvc, proj, w_dw_full)
    bdt2 = jnp.concatenate([bdr.transpose(0, 2, 1), bdi.transpose(0, 2, 1)], axis=1).astype(BF16)
    cdrt3, cdit3 = _rhs3(cdr.transpose(0, 2, 1)), _rhs3(-cdi.transpose(0, 2, 1))
    dproj_s, d_d_skip, dcdr, dcdi, dbdr, dbdi, dlr8, dli8 = s5_bwd(
        dyg_a, dyg_b, y_pre, proj, 2 * CW, s_re, s_im, bdt2, cdrt3, cdit3, tabs, d_skip)
    dproj = jnp.concatenate([dproj_c, dproj_s, dproj_g], axis=1)
    t = scatter("w_in", mm_tn(u, dproj, NDEV, "in_dw", out_dtype=BF16))
    du = mm_nt(dproj, wg["w_in"], "in_dx", dep=t)
    grad_x, d_scale1, d_shift1, d_norm1_g = norm_bwd(du, xs, dh1, norm1_g, scale1, "norm1_bwd")

    dmod = jnp.concatenate([d_shift1, d_scale1, d_gate1, d_shift2, d_scale2, d_gate2], axis=1)
    d_c_re = _diag_blocks(dcdr, H, P)
    d_c_im = _diag_blocks(dcdi, H, P)
    d_bbr = _diag_blocks(dbdr, H, P)
    d_bbi = _diag_blocks(dbdi, H, P)
    dlr = jnp.sum(dlr8, axis=0).reshape(G, P)
    dli = jnp.sum(dli8, axis=0).reshape(G, P)
    small_parts = [dmod, d_norm1_g, d_b_dw, d_ln_g, d_ln_b, dlr, dli, d_bbr, d_bbi, d_c_re, d_c_im, d_d_skip,
                   d_norm2_g, d_final_g, d_w_dw]
    pack8 = _flat_pad(small_parts, PACK).reshape(NDEV, -1, 1024)
    parts8, dmod_from = _exchange([pack8, dmod.reshape(NDEV, 1, n_ada)], "scatter_small_grads", False)
    (tot8,) = _exchange([sum_devices(parts8, "sum_small_grads")], "gather_small_sums", True)
    tot = tot8.reshape(-1)
    (g_b_ada, g_norm1_g, g_b_dw, g_ln_g, g_ln_b, t_lr, t_li, t_bbr, t_bbi, g_c_re_t, g_c_im_t, g_d_skip,
     g_norm2_g, g_final_g, g_w_dw_full) = _split(tot, small_parts)
    g_a_re, g_a_im, g_ldt, g_br2, g_bi2 = s5_params_bwd(
        a_re[0], a_im[0], ldt, br2, bi2, expand, t_lr, t_li, t_bbr.reshape(G * H, P), t_bbi.reshape(G * H, P))
    g_brt, g_bit = g_br2.reshape(G, H, P), g_bi2.reshape(G, H, P)
    dmod_cols = dmod_from.reshape(NDEV, n_ada)

    grads = {
        "b_ada": g_b_ada, "norm1_g": g_norm1_g, "b_dw": g_b_dw, "ln_g": g_ln_g, "ln_b": g_ln_b,
        "a_re": g_a_re[None], "a_im": g_a_im[None], "log_dt": g_ldt.reshape(1, G),
        "b_re": g_brt.transpose(0, 2, 1)[None], "b_im": g_bit.transpose(0, 2, 1)[None],
        "c_re": g_c_re_t[None], "c_im": g_c_im_t[None], "d_skip": g_d_skip, "norm2_g": g_norm2_g,
        "final_g": g_final_g.reshape(D),
        "w_dw": lax.dynamic_slice(g_w_dw_full, (0, me * (CW // NDEV)), (CONV_K, CW // NDEV))[None],
    }
    small = [k for k in names if k in grads]
    wf = _flat_pad([W[k] for k in small], PACK).reshape(-1, 1024)
    gf = _flat_pad([grads[k] for k in small], PACK).reshape(-1, 1024)
    mf = _flat_pad([Mo[k] for k in small], PACK).reshape(-1, 1024)
    vf = _flat_pad([Vo[k] for k in small], PACK).reshape(-1, 1024)
    d_s, m_s, v_s = adam_update(wf, gf, mf, vf, "adam_small")
    like = [W[k] for k in small]
    delta = dict(zip(small, _split(d_s.reshape(-1), like)))
    new_m = dict(zip(small, _split(m_s.reshape(-1), like)))
    new_v = dict(zip(small, _split(v_s.reshape(-1), like)))

    g, d, mm, vv = adam_w_ada(c_act, dmod_cols, w_ada[0], m_w_ada[0], v_w_ada[0])
    grads["w_ada"], delta["w_ada"], new_m["w_ada"], new_v["w_ada"] = g[None], d[None], mm[None], vv[None]

    after = d
    for k in ("w_ff2", "w_ff1", "w_out", "w_conv_out", "w_glu_a", "w_glu_b", "w_in"):
        parts = exchange_wait(scatter_handle[k], after, "scatter_wait_" + k, False)
        g, d, mm, vv = adam_reduce(parts, W[k][0], Mo[k][0], Vo[k][0], "adam_" + k)
        grads[k], delta[k], new_m[k], new_v[k] = g[None], d[None], mm[None], vv[None]
        after = d

    loss = lax.psum(loss_part[0, 0], ("x", "y", "c"))
    return (loss, grad_x[None], *[grads[k] for k in names], *[delta[k] for k in names],
            *[new_m[k] for k in names], *[new_v[k] for k in names])
```

```python
import functools
import math

import jax
import jax.numpy as jnp
from jax import lax
from jax.experimental import pallas as pl
from jax.experimental.pallas import tpu as pltpu

F32 = jnp.float32
BF16 = jnp.bfloat16
NDEV = 8
EPS = 1e-6
ADAM_LR, ADAM_B1, ADAM_B2, ADAM_EPS, ADAM_WD, ADAM_STEP = 0.001, 0.9, 0.999, 1e-08, 0.01, 10
CONV_K = 31
HALO = 32
GROUP = 16
STATE = 64
GB = 8
S5_ROWS = 512
HI = lax.Precision.HIGHEST
MESH = pl.DeviceIdType.MESH
VMEM_LIMIT = 56 * 1024 * 1024
MAX_CONTRACT = 2048
PACK_ROWS = 64
PACK = PACK_ROWS * 1024
ANY = pl.BlockSpec(memory_space=pl.ANY)


def _params(sem=None):
    if sem is None:
        return pltpu.CompilerParams(vmem_limit_bytes=VMEM_LIMIT)
    return pltpu.CompilerParams(dimension_semantics=sem, vmem_limit_bytes=VMEM_LIMIT)


def _sigmoid(v):
    return 1.0 / (1.0 + jnp.exp(-v))


def _me():
    return 4 * lax.axis_index("x") + 2 * lax.axis_index("y") + lax.axis_index("c")


def _peer(k):
    x, y, c = lax.axis_index("x"), lax.axis_index("y"), lax.axis_index("c")
    px = 1 - x if (k >> 2) & 1 else x
    py = 1 - y if (k >> 1) & 1 else y
    pc = 1 - c if k & 1 else c
    return (px, py, pc), 4 * px + 2 * py + pc


def _exchange(arrays, name, gather):
    n = len(arrays)
    out_shape = []
    for a in arrays:
        shp = (NDEV,) + a.shape if gather else a.shape
        out_shape.append(jax.ShapeDtypeStruct(shp, a.dtype))

    def body(*refs):
        ins, outs = refs[:n], refs[n:2 * n]
        send, recv, lsem = refs[2 * n:]
        me = _me()
        local = []
        for a in range(n):
            src = ins[a] if gather else ins[a].at[me]
            cp = pltpu.make_async_copy(src, outs[a].at[me], lsem.at[a])
            cp.start()
            local.append(cp)
        sends = []
        for a in range(n):
            for k in range(1, NDEV):
                dev, pidx = _peer(k)
                src = ins[a] if gather else ins[a].at[pidx]
                cp = pltpu.make_async_remote_copy(
                    src_ref=src, dst_ref=outs[a].at[me], send_sem=send.at[a * (NDEV - 1) + k - 1], recv_sem=recv.at[a * (NDEV - 1) + k - 1],
                    device_id=dev, device_id_type=MESH)
                cp.start()
                sends.append(cp)
        for a in range(n):
            for k in range(1, NDEV):
                dev, pidx = _peer(k)
                src = ins[a] if gather else ins[a].at[pidx]
                pltpu.make_async_remote_copy(
                    src_ref=src, dst_ref=outs[a].at[pidx], send_sem=send.at[a * (NDEV - 1) + k - 1], recv_sem=recv.at[a * (NDEV - 1) + k - 1],
                    device_id=dev, device_id_type=MESH).wait_recv()
        for cp in sends:
            cp.wait_send()
        for cp in local:
            cp.wait()

    return pl.pallas_call(
        body, name=name, out_shape=tuple(out_shape),
        in_specs=[ANY] * n, out_specs=tuple([ANY] * n),
        scratch_shapes=[pltpu.SemaphoreType.DMA((n * (NDEV - 1),)), pltpu.SemaphoreType.DMA((n * (NDEV - 1),)),
                        pltpu.SemaphoreType.DMA((n,))],
    )(*arrays)


HBM = pl.BlockSpec(memory_space=pltpu.HBM)
SEM = pl.BlockSpec(memory_space=pltpu.SEMAPHORE)
EFFECT = pltpu.SideEffectType.DATAFLOW_SIDE_EFFECTING
NPEER = NDEV - 1


def _landing(block_of_me, shape, dtype):
    land = lax.empty((NDEV,) + tuple(shape), dtype)
    start = (_me(),) + (0,) * len(shape)
    return pltpu.with_memory_space_constraint(lax.dynamic_update_slice(land, block_of_me[None], start), pltpu.HBM)


def exchange_start(arrays, name, gather, after=None):
    n = len(arrays)
    me = _me()
    deps = () if after is None else (after,)
    lands = []
    for a in arrays:
        if gather:
            lands.append(_landing(a, a.shape, a.dtype))
        else:
            mine = lax.dynamic_slice(a, (me,) + (0,) * (a.ndim - 1), (1,) + a.shape[1:])[0]
            lands.append(_landing(mine, a.shape[1:], a.dtype))
    srcs = [pltpu.with_memory_space_constraint(a, pltpu.HBM) for a in arrays]

    def body(*refs):
        ins, lnd = refs[:n], refs[n:2 * n]
        outs = refs[2 * n + len(deps):]
        sends, recvs, token = outs[:n], outs[n:2 * n], outs[-1]
        my = _me()
        for a in range(n):
            for k in range(1, NDEV):
                dev, pidx = _peer(k)
                src = ins[a] if gather else ins[a].at[pidx]
                pltpu.make_async_remote_copy(
                    src_ref=src, dst_ref=lnd[a].at[my], send_sem=sends[a].at[k - 1], recv_sem=recvs[a].at[k - 1],
                    device_id=dev, device_id_type=MESH).start()
        token[...] = jnp.zeros_like(token)

    out_shape = ([pltpu.SemaphoreType.DMA((NPEER,))] * (2 * n)
                 + [pltpu.HBM(a.shape, a.dtype) for a in srcs] + [pltpu.HBM(l.shape, l.dtype) for l in lands]
                 + [jax.ShapeDtypeStruct((8, 128), F32)])
    res = pl.pallas_call(
        body, name=name, out_shape=tuple(out_shape),
        in_specs=[HBM] * (2 * n) + [ANY] * len(deps),
        out_specs=tuple([SEM] * (2 * n) + [HBM] * (2 * n) + [pl.BlockSpec(memory_space=pltpu.VMEM)]),
        input_output_aliases={i: 2 * n + i for i in range(2 * n)},
        compiler_params=pltpu.CompilerParams(has_side_effects=EFFECT),
    )(*srcs, *lands, *deps)
    handles = [(res[a], res[n + a], res[2 * n + a], res[3 * n + a]) for a in range(n)]
    return handles, res[-1]


def exchange_wait(handle, after, name, gather):
    send_sem, recv_sem, src, land = handle

    def body(src_ref, land_ref, s_ref, r_ref, after_ref, src_out, land_out):
        for k in range(1, NDEV):
            dev, pidx = _peer(k)
            s = src_ref if gather else src_ref.at[pidx]
            cp = pltpu.make_async_remote_copy(
                src_ref=s, dst_ref=land_ref.at[pidx], send_sem=s_ref.at[k - 1], recv_sem=r_ref.at[k - 1],
                device_id=dev, device_id_type=MESH)
            cp.wait_send()
            cp.wait_recv()

    return pl.pallas_call(
        body, name=name, out_shape=(pltpu.HBM(src.shape, src.dtype), pltpu.HBM(land.shape, land.dtype)),
        in_specs=(HBM, HBM, SEM, SEM, ANY), out_specs=(HBM, HBM), input_output_aliases={0: 0, 1: 1},
        compiler_params=pltpu.CompilerParams(has_side_effects=EFFECT),
    )(src, land, send_sem, recv_sem, after)[1]


ICI_PEERS = (2, 4, 6)
SIBLING = 1


def gather2_start(blocks, name, after):
    m = len(blocks)
    lands = [_landing(b, b.shape, b.dtype) for b in blocks]
    srcs = [pltpu.with_memory_space_constraint(b, pltpu.HBM) for b in blocks]
    n = len(ICI_PEERS)

    def body(*refs):
        src, lnd = refs[:m], refs[m:2 * m]
        outs = refs[2 * m + 1:]
        send, recv_sib, recv_ici = outs[:m], outs[m:2 * m], outs[2 * m:3 * m]
        my = _me()
        for a in range(m):
            dev, _ = _peer(SIBLING)
            pltpu.make_async_remote_copy(src_ref=src[a], dst_ref=lnd[a].at[my], send_sem=send[a].at[0],
                                         recv_sem=recv_sib[a].at[0], device_id=dev, device_id_type=MESH).start()
            for j, k in enumerate(ICI_PEERS):
                dev, _ = _peer(k)
                pltpu.make_async_remote_copy(src_ref=src[a], dst_ref=lnd[a].at[my], send_sem=send[a].at[1 + j],
                                             recv_sem=recv_ici[a].at[j], device_id=dev, device_id_type=MESH).start()

    out_shape = ([pltpu.SemaphoreType.DMA((1 + n,))] * m + [pltpu.SemaphoreType.DMA((1,))] * m
                 + [pltpu.SemaphoreType.DMA((n,))] * m
                 + [pltpu.HBM(s.shape, s.dtype) for s in srcs] + [pltpu.HBM(l.shape, l.dtype) for l in lands])
    res = pl.pallas_call(
        body, name=name, out_shape=tuple(out_shape),
        in_specs=[HBM] * (2 * m) + [ANY], out_specs=tuple([SEM] * (3 * m) + [HBM] * (2 * m)),
        input_output_aliases={i: 3 * m + i for i in range(2 * m)},
        compiler_params=pltpu.CompilerParams(has_side_effects=EFFECT),
    )(*srcs, *lands, after)
    return [tuple(res[g * m + a] for g in range(5)) for a in range(m)]


def gather2_forward(handles, name, after):
    m = len(handles)
    n = len(ICI_PEERS)
    srcs, lands = [h[3] for h in handles], [h[4] for h in handles]

    def body(*refs):
        src, lnd, recv_ici = refs[:m], refs[m:2 * m], refs[2 * m:3 * m]
        outs = refs[3 * m + 1:]
        fsend, frecv = outs[:m], outs[m:2 * m]
        sib, _ = _peer(SIBLING)
        for a in range(m):
            for j, k in enumerate(ICI_PEERS):
                dev, pidx = _peer(k)
                pltpu.make_async_remote_copy(
                    src_ref=src[a], dst_ref=lnd[a].at[pidx], send_sem=fsend[a].at[j], recv_sem=recv_ici[a].at[j],
                    device_id=dev, device_id_type=MESH).wait_recv()
                pltpu.make_async_remote_copy(
                    src_ref=lnd[a].at[pidx], dst_ref=lnd[a].at[pidx], send_sem=fsend[a].at[j], recv_sem=frecv[a].at[j],
                    device_id=sib, device_id_type=MESH).start()

    out_shape = ([pltpu.SemaphoreType.DMA((n,))] * (2 * m)
                 + [pltpu.HBM(s.shape, s.dtype) for s in srcs] + [pltpu.HBM(l.shape, l.dtype) for l in lands])
    res = pl.pallas_call(
        body, name=name, out_shape=tuple(out_shape),
        in_specs=[HBM] * (2 * m) + [SEM] * m + [ANY], out_specs=tuple([SEM] * (2 * m) + [HBM] * (2 * m)),
        input_output_aliases={i: 2 * m + i for i in range(2 * m)},
        compiler_params=pltpu.CompilerParams(has_side_effects=EFFECT),
    )(*srcs, *lands, *[h[2] for h in handles], after)
    return [(handles[a][0], handles[a][1], res[a], res[m + a], res[2 * m + a], res[3 * m + a]) for a in range(m)]


def gather2_wait(handle, name, after):
    send, recv_sib, fsend, frecv, src, land = handle

    def body(src_ref, land_ref, send_ref, recv_sib_ref, fsend_ref, frecv_ref, after_ref, src_out, land_out):
        sib, sib_idx = _peer(SIBLING)
        own = pltpu.make_async_remote_copy(src_ref=src_ref, dst_ref=land_ref.at[sib_idx], send_sem=send_ref.at[0],
                                           recv_sem=recv_sib_ref.at[0], device_id=sib, device_id_type=MESH)
        own.wait_send()
        own.wait_recv()
        for j, k in enumerate(ICI_PEERS):
            dev, pidx = _peer(k)
            pltpu.make_async_remote_copy(src_ref=src_ref, dst_ref=land_ref.at[pidx], send_sem=send_ref.at[1 + j],
                                         recv_sem=frecv_ref.at[j], device_id=dev, device_id_type=MESH).wait_send()
            _, fidx = _peer(k ^ SIBLING)
            fwd = pltpu.make_async_remote_copy(src_ref=land_ref.at[pidx], dst_ref=land_ref.at[fidx],
                                               send_sem=fsend_ref.at[j], recv_sem=frecv_ref.at[j],
                                               device_id=sib, device_id_type=MESH)
            fwd.wait_send()
            fwd.wait_recv()

    return pl.pallas_call(
        body, name=name, out_shape=(pltpu.HBM(src.shape, src.dtype), pltpu.HBM(land.shape, land.dtype)),
        in_specs=(HBM, HBM, SEM, SEM, SEM, SEM, ANY), out_specs=(HBM, HBM), input_output_aliases={0: 0, 1: 1},
        compiler_params=pltpu.CompilerParams(has_side_effects=EFFECT),
    )(src, land, send, recv_sib, fsend, frecv, after)[1]


def _acc_steps(p, acc, k, nk, finish):
    if nk == 1:
        finish(p)
        return

    @pl.when(k == 0)
    def _():
        acc[...] = p

    @pl.when(k > 0)
    def _():
        acc[...] += p

    @pl.when(k == nk - 1)
    def _():
        finish(acc[...])


def mm_nn(a, w3, name, out_dtype=F32, epi=None, extras=()):
    M, K = a.shape
    J, _, n = w3.shape
    tm, tn, tk = min(1024, M), min(1024, n), min(2048, K)
    q, nk, ne = n // tn, K // tk, len(extras)

    def body(*refs):
        a_ref, w_ref = refs[:2]
        ex, o_ref, acc = refs[2:2 + ne], refs[2 + ne], refs[-1]
        p = jnp.dot(a_ref[...], w_ref[...], preferred_element_type=F32)

        def finish(r):
            if epi is not None:
                r = epi(r, *[e[...] for e in ex])
            o_ref[...] = r.astype(out_dtype)

        _acc_steps(p, acc, pl.program_id(2), nk, finish)

    return pl.pallas_call(
        body, name=name, grid=(M // tm, J * q, nk),
        in_specs=[pl.BlockSpec((tm, tk), lambda i, j, k: (i, k)),
                  pl.BlockSpec((None, tk, tn), lambda i, j, k: (j // q, k, j % q))]
        + [pl.BlockSpec((tm, tn), lambda i, j, k: (i, j))] * ne,
        out_specs=pl.BlockSpec((tm, tn), lambda i, j, k: (i, j)),
        out_shape=jax.ShapeDtypeStruct((M, J * n), out_dtype),
        scratch_shapes=[pltpu.VMEM((tm, tn), F32)],
        compiler_params=_params(("parallel", "parallel", "arbitrary")),
    )(a, w3, *extras)


def mm_nt(dy, w3, name, out_dtype=F32, epi=None, extras=(), dep=None):
    M, _ = dy.shape
    J, K, n = w3.shape
    tm, tn, tkk = min(1024, M), min(MAX_CONTRACT, n), min(1024, K)
    q, ne = n // tn, len(extras)
    s = 1
    while q == 1 and J % (2 * s) == 0 and 2 * s * tn <= MAX_CONTRACT:
        s *= 2
    nk = (J // s) * q
    deps = () if dep is None else (dep,)

    def body(*refs):
        d_ref, w_ref = refs[:2]
        ex, o_ref, acc = refs[2:2 + ne], refs[-2], refs[-1]
        nt = (((1,), (1,)), ((), ()))
        p = lax.dot_general(d_ref[:, 0:tn], w_ref[0], nt, preferred_element_type=F32)
        for j in range(1, s):
            p = p + lax.dot_general(d_ref[:, j * tn:(j + 1) * tn], w_ref[j], nt, preferred_element_type=F32)

        def finish(r):
            if epi is not None:
                r = epi(r, *[e[...] for e in ex])
            o_ref[...] = r.astype(out_dtype)

        _acc_steps(p, acc, pl.program_id(2), nk, finish)

    return pl.pallas_call(
        body, name=name, grid=(M // tm, K // tkk, nk),
        in_specs=[pl.BlockSpec((tm, s * tn), lambda i, kk, c: (i, c)),
                  pl.BlockSpec((s, tkk, tn), lambda i, kk, c: (c // q, kk, c % q))]
        + [pl.BlockSpec((tm, tkk), lambda i, kk, c: (i, kk))] * ne + [ANY] * len(deps),
        out_specs=pl.BlockSpec((tm, tkk), lambda i, kk, c: (i, kk)),
        out_shape=jax.ShapeDtypeStruct((M, K), out_dtype),
        scratch_shapes=[pltpu.VMEM((tm, tkk), F32)],
        compiler_params=_params(("parallel", "parallel", "arbitrary")),
    )(dy, w3, *extras, *deps)


def mm_tn(a, dy, J, name, out_dtype=F32, dep=None):
    M, K = a.shape
    n = dy.shape[1] // J
    tm, tn, tkk = min(MAX_CONTRACT, M), min(1024, n), min(1024, K)
    q, nk = n // tn, M // tm
    deps = () if dep is None else (dep,)

    def body(a_ref, d_ref, *rest):
        o_ref, acc = rest[-2:]
        p = lax.dot_general(a_ref[...], d_ref[...], (((0,), (0,)), ((), ())), preferred_element_type=F32)

        def finish(r):
            o_ref[...] = r.astype(out_dtype)

        _acc_steps(p, acc, pl.program_id(2), nk, finish)

    return pl.pallas_call(
        body, name=name, grid=(K // tkk, J * q, nk),
        in_specs=[pl.BlockSpec((tm, tkk), lambda kk, c, m: (m, kk)),
                  pl.BlockSpec((tm, tn), lambda kk, c, m: (m, c))] + [ANY] * len(deps),
        out_specs=pl.BlockSpec((None, tkk, tn), lambda kk, c, m: (c // q, kk, c % q)),
        out_shape=jax.ShapeDtypeStruct((J, K, n), out_dtype),
        scratch_shapes=[pltpu.VMEM((tkk, tn), F32)],
        compiler_params=_params(("parallel", "parallel", "arbitrary")),
    )(a, dy, *deps)


def _tm(L):
    return min(256, L)


def _row(w, cb=0, tm=None):
    return pl.BlockSpec((tm, w), lambda i: (i, cb))


def _vec(w, cb=0):
    return pl.BlockSpec((1, w), lambda i: (0, cb))


def _accum(ref, val, i):
    @pl.when(i == 0)
    def _():
        ref[...] = val

    @pl.when(i > 0)
    def _():
        ref[...] += val


def _colsum(v):
    return jnp.sum(v, axis=0, keepdims=True)


def _rms(v):
    return lax.rsqrt(jnp.mean(v * v, axis=-1, keepdims=True) + EPS)


def adaln_mod(c_all, w_ada, b_cols):
    B, D = c_all.shape
    n = w_ada.shape[1]
    tn = 512

    def body(c_ref, w_ref, b_ref, o_ref, ca_ref):
        cv = c_ref[...]
        ca = cv * _sigmoid(cv)
        ca_ref[...] = ca
        o_ref[...] = jnp.dot(ca.astype(BF16), w_ref[...].astype(BF16), preferred_element_type=F32) + b_ref[...]

    return pl.pallas_call(
        body, name="adaln_mod", grid=(n // tn,),
        in_specs=[pl.BlockSpec((B, D), lambda j: (0, 0)), pl.BlockSpec((D, tn), lambda j: (0, j)),
                  pl.BlockSpec((1, tn), lambda j: (0, j))],
        out_specs=(pl.BlockSpec((B, tn), lambda j: (0, j)), pl.BlockSpec((B, D), lambda j: (0, 0))),
        out_shape=(jax.ShapeDtypeStruct((B, n), F32), jax.ShapeDtypeStruct((B, D), F32)),
        compiler_params=_params(("arbitrary",)),
    )(c_all, w_ada, b_cols)


def prenorm(x, g, scale, shift, name):
    L, D = x.shape
    tm = _tm(L)

    def body(x_ref, g_ref, sc_ref, sh_ref, u_ref):
        xv = x_ref[...]
        u_ref[...] = (xv * _rms(xv) * g_ref[...] * (1.0 + sc_ref[...]) + sh_ref[...]).astype(BF16)

    return pl.pallas_call(
        body, name=name, grid=(L // tm,),
        in_specs=[_row(D, tm=tm), _vec(D), _vec(D), _vec(D)],
        out_specs=_row(D, tm=tm), out_shape=jax.ShapeDtypeStruct((L, D), BF16),
        compiler_params=_params(("parallel",)),
    )(x, g, scale, shift)


def _shifted_copies(buf, shifted, tm):
    n = HALO + tm - 8
    for r in range(1, 8):
        shifted[r - 1] = buf[pl.ds(r, n), :]


def _window(buf, shifted, off, tm):
    r, base = off % 8, off - off % 8
    if r == 0:
        return buf[pl.ds(base, tm), :]
    return shifted[r - 1, pl.ds(base, tm), :]


def conv_fwd(proj, w_dw, b_dw, ln_g, ln_b):
    L = proj.shape[0]
    C = w_dw.shape[1]
    tm = _tm(L)
    hb = tm // HALO

    def body(a_ref, g_ref, ah_ref, gh_ref, w_ref, b_ref, lg_ref, lb_ref, vs_ref, vc_ref, buf, shifted):
        i = pl.program_id(0)
        halo = ah_ref[...] * _sigmoid(gh_ref[...])
        buf[0:HALO, :] = halo * jnp.where(i > 0, 1.0, 0.0)
        buf[HALO:HALO + tm, :] = a_ref[...] * _sigmoid(g_ref[...])
        _shifted_copies(buf, shifted, tm)
        acc = jnp.zeros((tm, C), F32) + b_ref[...]
        for k in range(CONV_K):
            acc = acc + w_ref[k:k + 1, :] * _window(buf, shifted, HALO - (CONV_K - 1) + k, tm)
        vc_ref[...] = acc
        mu = jnp.mean(acc, axis=-1, keepdims=True)
        d = acc - mu
        var = jnp.mean(d * d, axis=-1, keepdims=True)
        ln = d * lax.rsqrt(var + EPS) * lg_ref[...] + lb_ref[...]
        vs_ref[...] = (ln * _sigmoid(ln)).astype(BF16)

    prev = lambda cb: pl.BlockSpec((HALO, C), lambda i: (jnp.maximum(i * hb - 1, 0), cb))
    return pl.pallas_call(
        body, name="conv_fwd", grid=(L // tm,),
        in_specs=[_row(C, 0, tm), _row(C, 1, tm), prev(0), prev(1),
                  pl.BlockSpec((HALO, C), lambda i: (0, 0)), _vec(C), _vec(C), _vec(C)],
        out_specs=(_row(C, tm=tm), _row(C, tm=tm)),
        out_shape=(jax.ShapeDtypeStruct((L, C), BF16), jax.ShapeDtypeStruct((L, C), F32)),
        scratch_shapes=[pltpu.VMEM((HALO + tm, C), F32), pltpu.VMEM((7, HALO + tm - 8, C), F32)],
        compiler_params=_params(("parallel",)),
    )(proj, proj, proj, proj, w_dw, b_dw, ln_g, ln_b)


def _gelu(v):
    return 0.5 * v * (1.0 + jnp.tanh(math.sqrt(2.0 / math.pi) * (v + 0.044715 * v * v * v)))


def _gelu_grad(v):
    k = math.sqrt(2.0 / math.pi)
    t = jnp.tanh(k * (v + 0.044715 * v * v * v))
    return 0.5 * (1.0 + t) + 0.5 * v * (1.0 - t * t) * k * (1.0 + 3.0 * 0.044715 * v * v)


def s5_param_fn(ar, ai, ldt, br, bi, expand):
    dt = jnp.exp(ldt)
    er = jnp.exp(ar * dt)
    th = ai * dt
    lbr, lbi = er * jnp.cos(th), er * jnp.sin(th)
    nr, ni = lbr - 1.0, lbi
    den = ar * ar + ai * ai
    qr, qi = (nr * ar + ni * ai) / den, (ni * ar - nr * ai) / den
    qre = jnp.dot(expand, qr, precision=HI, preferred_element_type=F32)
    qie = jnp.dot(expand, qi, precision=HI, preferred_element_type=F32)
    return lbr, lbi, qre * br - qie * bi, qre * bi + qie * br


def s5_params(ar, ai, ldt, br2, bi2, expand):
    def body(ar_ref, ai_ref, ld_ref, br_ref, bi_ref, e_ref, o1, o2, o3, o4):
        r = s5_param_fn(ar_ref[...], ai_ref[...], ld_ref[...], br_ref[...], bi_ref[...], e_ref[...])
        o1[...], o2[...], o3[...], o4[...] = r

    s2, s3 = jax.ShapeDtypeStruct(ar.shape, F32), jax.ShapeDtypeStruct(br2.shape, F32)
    return pl.pallas_call(body, name="s5_params", out_shape=(s2, s2, s3, s3), compiler_params=_params())(
        ar, ai, ldt, br2, bi2, expand)


def s5_params_bwd(ar, ai, ldt, br2, bi2, expand, dlr, dli, dbr, dbi):
    def body(ar_ref, ai_ref, ld_ref, br_ref, bi_ref, e_ref, c1, c2, c3, c4, o1, o2, o3, o4, o5):
        e = e_ref[...]
        fn = lambda a, b, c, d, f: s5_param_fn(a, b, c, d, f, e)
        _, vjp = jax.vjp(fn, ar_ref[...], ai_ref[...], ld_ref[...], br_ref[...], bi_ref[...])
        r = vjp((c1[...], c2[...], c3[...], c4[...]))
        o1[...], o2[...], o3[...], o4[...], o5[...] = r

    shapes = tuple(jax.ShapeDtypeStruct(v.shape, F32) for v in (ar, ai, ldt, br2, bi2))
    return pl.pallas_call(body, name="s5_params_bwd", out_shape=shapes, compiler_params=_params())(
        ar, ai, ldt, br2, bi2, expand, dlr, dli, dbr, dbi)


def s5_tables(lr, li):
    C = lr.shape[1]

    def body(lr_ref, li_ref, o_ref):
        row = lax.broadcasted_iota(jnp.int32, (8, C), 0)
        for rev in (0, 1):
            pr = jnp.broadcast_to(lr_ref[...], (8, C))
            pi = jnp.broadcast_to(-li_ref[...] if rev else li_ref[...], (8, C))
            br, bi = pr, pi
            pows = [(pr, pi)]
            for _ in range(7):
                pr, pi = pr * br - pi * bi, pr * bi + pi * br
                pows.append((pr, pi))
            base = 8 * rev
            for s, d in enumerate((1, 2, 4)):
                keep = (row + d <= 7) if rev else (row >= d)
                o_ref[base + 2 * s] = jnp.where(keep, pows[d - 1][0], 0.0)
                o_ref[base + 2 * s + 1] = jnp.where(keep, pows[d - 1][1], 0.0)
            cr, ci = jnp.zeros((8, C), F32), jnp.zeros((8, C), F32)
            for j in range(8):
                e = (8 - j) if rev else (j + 1)
                cr = jnp.where(row == j, pows[e - 1][0], cr)
                ci = jnp.where(row == j, pows[e - 1][1], ci)
            o_ref[base + 6] = cr
            o_ref[base + 7] = ci

    return pl.pallas_call(body, name="s5_tables", out_shape=jax.ShapeDtypeStruct((16, 8, C), F32),
                          compiler_params=_params())(lr, li)


def _scan_tile(xr, xi, tabs, cr, ci, rev):
    for s, d in enumerate((1, 2, 4)):
        tr, ti = tabs[2 * s], tabs[2 * s + 1]
        sh = (8 - d) if rev else d
        sr, si = pltpu.roll(xr, sh, 0), pltpu.roll(xi, sh, 0)
        xr, xi = xr + tr * sr - ti * si, xi + tr * si + ti * sr
    tr, ti = tabs[6], tabs[7]
    xr, xi = xr + tr * cr - ti * ci, xi + tr * ci + ti * cr
    return xr, xi


def _hi_lo(a):
    hi = a.astype(BF16)
    return hi, (a - hi.astype(F32)).astype(BF16)


def _lhs3(a):
    hi, lo = _hi_lo(a)
    return jnp.concatenate([hi, lo, hi], axis=1)


def _rhs3(m):
    hi, lo = _hi_lo(m)
    return jnp.concatenate([hi, hi, lo], axis=-2)


def s5_fwd(proj, col0, bdr3, bdi3, cd2, tabs, d_skip):
    L = proj.shape[0]
    nb, cw3, sw = bdr3.shape
    cw = cw3 // 3
    tl = min(S5_ROWS, L)
    cb0 = col0 // cw

    def body(u_ref, bdr_ref, bdi_ref, cd_ref, t_ref, dk_ref, sr_ref, si_ref, yp_ref, yg_ref, car):
        l = pl.program_id(1)

        @pl.when(l == 0)
        def _():
            car[...] = jnp.zeros_like(car)

        u = u_ref[...]
        u3 = _lhs3(u)
        sr_ref[...] = jnp.dot(u3, bdr_ref[...], preferred_element_type=F32)
        si_ref[...] = jnp.dot(u3, bdi_ref[...], preferred_element_type=F32)

        def tile(i, c):
            tabs = [t_ref[j] for j in range(8)]
            r0 = pl.multiple_of(i * 8, 8)
            xr, xi = _scan_tile(sr_ref[pl.ds(r0, 8), :], si_ref[pl.ds(r0, 8), :], tabs, c[0], c[1], False)
            sr_ref[pl.ds(r0, 8), :] = xr
            si_ref[pl.ds(r0, 8), :] = xi
            return xr[7:8, :], xi[7:8, :]

        c = lax.fori_loop(0, tl // 8, tile, (car[0:1, :], car[1:2, :]))
        car[0:1, :] = c[0]
        car[1:2, :] = c[1]
        s2 = jnp.concatenate([sr_ref[...].astype(BF16), si_ref[...].astype(BF16)], axis=1)
        y = jnp.dot(s2, cd_ref[...], preferred_element_type=F32) + dk_ref[...] * u
        yp_ref[...] = y
        yg_ref[...] = _gelu(y).astype(BF16)

    blk = lambda r, c: pl.BlockSpec((None, r, c), lambda b, l: (b, 0, 0))
    return pl.pallas_call(
        body, name="s5_fwd", grid=(nb, L // tl),
        in_specs=[pl.BlockSpec((tl, cw), lambda b, l: (l, cb0 + b)), blk(cw3, sw), blk(cw3, sw), blk(2 * sw, cw),
                  pl.BlockSpec((8, 8, sw), lambda b, l: (0, 0, b)), pl.BlockSpec((1, cw), lambda b, l: (0, b))],
        out_specs=(pl.BlockSpec((tl, sw), lambda b, l: (l, b)), pl.BlockSpec((tl, sw), lambda b, l: (l, b)),
                   pl.BlockSpec((tl, cw), lambda b, l: (l, b)), pl.BlockSpec((tl, cw), lambda b, l: (l, b))),
        out_shape=(jax.ShapeDtypeStruct((L, nb * sw), F32), jax.ShapeDtypeStruct((L, nb * sw), F32),
                   jax.ShapeDtypeStruct((L, nb * cw), F32), jax.ShapeDtypeStruct((L, nb * cw), BF16)),
        scratch_shapes=[pltpu.VMEM((8, sw), F32)],
        compiler_params=_params(("parallel", "arbitrary")),
    )(proj, bdr3, bdi3, cd2, tabs, d_skip)


def s5_bwd(dyg_a, dyg_b, yp, proj, col0, s_re, s_im, bdt2, cdrt3, cdit3, tabs, d_skip):
    L = proj.shape[0]
    nb, sw2, cw = bdt2.shape
    sw = sw2 // 2
    tl = min(S5_ROWS, L)
    nl = L // tl
    cb0 = col0 // cw
    tb = tl // 8

    def body(da_ref, db_ref, yp_ref, u_ref, sr_ref, si_ref, hr_ref, hi_ref, bdt_ref, cdrt_ref, cdit_ref,
             t_ref, dk_ref, du_ref, ddk_ref, dcr_ref, dci_ref, dbr_ref, dbi_ref, dlr_ref, dli_ref,
             gr, gi, pr, pi, car):
        l = pl.program_id(1)
        first = l == nl - 1

        @pl.when(l == 0)
        def _():
            car[...] = jnp.zeros_like(car)

        u = u_ref[...]
        dy = (da_ref[...] + db_ref[...]) * _gelu_grad(yp_ref[...])
        dy3 = _lhs3(dy)
        gr[...] = jnp.dot(dy3, cdrt_ref[...], preferred_element_type=F32)
        gi[...] = jnp.dot(dy3, cdit_ref[...], preferred_element_type=F32)
        inner = jnp.where(first, 0.0, 1.0)
        pr[0:8, :] = hr_ref[...] * inner
        pi[0:8, :] = hi_ref[...] * inner
        pr[8:8 + tl, :] = sr_ref[...]
        pi[8:8 + tl, :] = si_ref[...]
        row = lax.broadcasted_iota(jnp.int32, (8, sw), 0)

        def tile(j, c):
            tabs = [t_ref[8 + k] for k in range(8)]
            r0 = pl.multiple_of((tb - 1 - j) * 8, 8)
            xr, xi = _scan_tile(gr[pl.ds(r0, 8), :], gi[pl.ds(r0, 8), :], tabs, c[0], c[1], True)
            gr[pl.ds(r0, 8), :] = xr
            gi[pl.ds(r0, 8), :] = xi
            qr = jnp.where(row == 0, pltpu.roll(pr[pl.ds(r0, 8), :], 1, 0), pltpu.roll(pr[pl.ds(r0 + 8, 8), :], 1, 0))
            qi = jnp.where(row == 0, pltpu.roll(pi[pl.ds(r0, 8), :], 1, 0), pltpu.roll(pi[pl.ds(r0 + 8, 8), :], 1, 0))
            return xr[0:1, :], xi[0:1, :], c[2] + xr * qr + xi * qi, c[3] + xi * qr - xr * qi

        z = jnp.zeros((8, sw), F32)
        c = lax.fori_loop(0, tb, tile, (car[0:1, :], car[1:2, :], z, z))
        car[0:1, :] = c[0]
        car[1:2, :] = c[1]
        g_re, g_im = gr[...].astype(BF16), gi[...].astype(BF16)
        g2 = jnp.concatenate([g_re, g_im], axis=1)
        du_ref[...] = (dy * dk_ref[...] + jnp.dot(g2, bdt_ref[...], preferred_element_type=F32)).astype(BF16)
        tn = (((0,), (0,)), ((), ()))
        dyb, ub = dy.astype(BF16), u.astype(BF16)
        _accum(ddk_ref, _colsum(dy * u), l)
        _accum(dcr_ref, lax.dot_general(dyb, sr_ref[...].astype(BF16), tn, preferred_element_type=F32), l)
        _accum(dci_ref, -lax.dot_general(dyb, si_ref[...].astype(BF16), tn, preferred_element_type=F32), l)
        _accum(dbr_ref, lax.dot_general(ub, g_re, tn, preferred_element_type=F32), l)
        _accum(dbi_ref, lax.dot_general(ub, g_im, tn, preferred_element_type=F32), l)
        _accum(dlr_ref, c[2], l)
        _accum(dli_ref, c[3], l)

    rl = lambda l: nl - 1 - l
    cblk = lambda w, off=0: pl.BlockSpec((tl, w), lambda b, l: (rl(l), off + b))
    halo = pl.BlockSpec((8, sw), lambda b, l: (jnp.maximum(rl(l) * tb - 1, 0), b))
    mat = lambda r, c: pl.BlockSpec((None, r, c), lambda b, l: (b, 0, 0))
    return pl.pallas_call(
        body, name="s5_bwd", grid=(nb, nl),
        in_specs=[cblk(cw), cblk(cw), cblk(cw), cblk(cw, cb0), cblk(sw), cblk(sw), halo, halo,
                  mat(2 * sw, cw), mat(3 * cw, sw), mat(3 * cw, sw),
                  pl.BlockSpec((16, 8, sw), lambda b, l: (0, 0, b)), pl.BlockSpec((1, cw), lambda b, l: (0, b))],
        out_specs=(cblk(cw), pl.BlockSpec((1, cw), lambda b, l: (0, b)), mat(cw, sw), mat(cw, sw), mat(cw, sw), mat(cw, sw),
                   pl.BlockSpec((8, sw), lambda b, l: (0, b)), pl.BlockSpec((8, sw), lambda b, l: (0, b))),
        out_shape=(jax.ShapeDtypeStruct((L, nb * cw), BF16), jax.ShapeDtypeStruct((1, nb * cw), F32),
                   jax.ShapeDtypeStruct((nb, cw, sw), F32), jax.ShapeDtypeStruct((nb, cw, sw), F32),
                   jax.ShapeDtypeStruct((nb, cw, sw), F32), jax.ShapeDtypeStruct((nb, cw, sw), F32),
                   jax.ShapeDtypeStruct((8, nb * sw), F32), jax.ShapeDtypeStruct((8, nb * sw), F32)),
        scratch_shapes=[pltpu.VMEM((tl, sw), F32), pltpu.VMEM((tl, sw), F32),
                        pltpu.VMEM((tl + 8, sw), F32), pltpu.VMEM((tl + 8, sw), F32), pltpu.VMEM((8, sw), F32)],
        compiler_params=_params(("parallel", "arbitrary")),
    )(dyg_a, dyg_b, yp, proj, s_re, s_im, s_re, s_im, bdt2, cdrt3, cdit3, tabs, d_skip)


def merge_fwd(proj, col_gc, y_conv, ga, gb):
    L, D = y_conv.shape
    tm = _tm(L)
    h = D // 2
    c0 = col_gc // h

    def body(p0, p1, p2, p3, yc_ref, ga_ref, gb_ref, o_ref):
        gc, gs = (p0, p1), (p2, p3)
        for s in range(2):
            cols = slice(s * h, (s + 1) * h)
            y_ssm = ga_ref[:, cols] * _sigmoid(gb_ref[:, cols])
            o_ref[:, cols] = (_sigmoid(gc[s][...]) * yc_ref[:, cols] + _sigmoid(gs[s][...]) * y_ssm).astype(BF16)

    return pl.pallas_call(
        body, name="merge_fwd", grid=(L // tm,),
        in_specs=[_row(h, c0 + s, tm) for s in range(4)] + [_row(D, tm=tm)] * 3,
        out_specs=_row(D, tm=tm), out_shape=jax.ShapeDtypeStruct((L, D), BF16),
        compiler_params=_params(("parallel",)),
    )(proj, proj, proj, proj, y_conv, ga, gb)


def residual_norm(x, m_out, gate, g, scale, shift):
    L, D = x.shape
    tm = _tm(L)

    def body(x_ref, m_ref, gt_ref, g_ref, sc_ref, sh_ref, h_ref, z_ref):
        h = x_ref[...] + gt_ref[...] * m_ref[...]
        h_ref[...] = h
        z_ref[...] = (h * _rms(h) * g_ref[...] * (1.0 + sc_ref[...]) + sh_ref[...]).astype(BF16)

    return pl.pallas_call(
        body, name="residual_norm", grid=(L // tm,),
        in_specs=[_row(D, tm=tm), _row(D, tm=tm), _vec(D), _vec(D), _vec(D), _vec(D)],
        out_specs=(_row(D, tm=tm), _row(D, tm=tm)),
        out_shape=(jax.ShapeDtypeStruct((L, D), F32), jax.ShapeDtypeStruct((L, D), BF16)),
        compiler_params=_params(("parallel",)),
    )(x, m_out, gate, g, scale, shift)


def loss_bwd(h1, ff, gate2, final_g, target):
    L, D = h1.shape
    tm = _tm(L)

    def body(h_ref, f_ref, gt_ref, g_ref, t_ref, dh_ref, dff_ref, loss_ref, dg_ref, dgt_ref):
        i = pl.program_id(0)
        ffv = f_ref[...]
        h2 = h_ref[...] + gt_ref[...] * ffv
        r = _rms(h2)
        n = h2 * r
        err = n * g_ref[...] - t_ref[...]
        per_tok = jnp.mean(err * err, axis=-1, keepdims=True)
        _accum(loss_ref, 0.5 * jnp.sum(per_tok, axis=0, keepdims=True), i)
        dy = err * (1.0 / D)
        _accum(dg_ref, _colsum(dy * n), i)
        dn = dy * g_ref[...]
        dh2 = r * (dn - n * jnp.mean(dn * n, axis=-1, keepdims=True))
        dh_ref[...] = dh2
        dff_ref[...] = (gt_ref[...] * dh2).astype(BF16)
        _accum(dgt_ref, _colsum(dh2 * ffv), i)

    return pl.pallas_call(
        body, name="loss_bwd", grid=(L // tm,),
        in_specs=[_row(D, tm=tm), _row(D, tm=tm), _vec(D), _vec(D), _row(D, tm=tm)],
        out_specs=(_row(D, tm=tm), _row(D, tm=tm), pl.BlockSpec((1, 1), lambda i: (0, 0)), _vec(D), _vec(D)),
        out_shape=(jax.ShapeDtypeStruct((L, D), F32), jax.ShapeDtypeStruct((L, D), BF16),
                   jax.ShapeDtypeStruct((1, 1), F32), jax.ShapeDtypeStruct((1, D), F32), jax.ShapeDtypeStruct((1, D), F32)),
        compiler_params=_params(("arbitrary",)),
    )(h1, ff, gate2, final_g, target)


def norm_bwd(dz, h, dh_in, g, scale, name, gate=None, m_out=None):
    L, D = h.shape
    tm = _tm(L)
    tail = gate is not None

    def body(*refs):
        dz_ref, h_ref, di_ref, g_ref, sc_ref = refs[:5]
        rest = refs[5:]
        if tail:
            gt_ref, m_ref = rest[:2]
            rest = rest[2:]
        dh_ref, dsc_ref, dsh_ref, dg_ref = rest[:4]
        i = pl.program_id(0)
        hv, dzv = h_ref[...], dz_ref[...]
        r = _rms(hv)
        n = hv * r
        _accum(dsc_ref, _colsum(dzv * n * g_ref[...]), i)
        _accum(dsh_ref, _colsum(dzv), i)
        dzn = dzv * (1.0 + sc_ref[...])
        _accum(dg_ref, _colsum(dzn * n), i)
        dn = dzn * g_ref[...]
        dh = di_ref[...] + r * (dn - n * jnp.mean(dn * n, axis=-1, keepdims=True))
        dh_ref[...] = dh
        if tail:
            dmo_ref, dgt_ref = rest[4:]
            dmo_ref[...] = (gt_ref[...] * dh).astype(BF16)
            _accum(dgt_ref, _colsum(dh * m_ref[...]), i)

    ins = [dz, h, dh_in, g, scale]
    in_specs = [_row(D, tm=tm)] * 3 + [_vec(D)] * 2
    out_specs = [_row(D, tm=tm), _vec(D), _vec(D), _vec(D)]
    out_shape = [jax.ShapeDtypeStruct((L, D), F32)] + [jax.ShapeDtypeStruct((1, D), F32)] * 3
    if tail:
        ins += [gate, m_out]
        in_specs += [_vec(D), _row(D, tm=tm)]
        out_specs += [_row(D, tm=tm), _vec(D)]
        out_shape += [jax.ShapeDtypeStruct((L, D), BF16), jax.ShapeDtypeStruct((1, D), F32)]
    return pl.pallas_call(
        body, name=name, grid=(L // tm,), in_specs=in_specs, out_specs=tuple(out_specs), out_shape=tuple(out_shape),
        compiler_params=_params(("arbitrary",)),
    )(*ins)


def merge_bwd(dmerged, proj, col_gc, y_conv, ga, gb):
    L, D = y_conv.shape
    tm = _tm(L)
    h = D // 2
    c0 = col_gc // h

    def body(dm_ref, p0, p1, p2, p3, yc_ref, ga_ref, gb_ref, dyc_ref, dga_ref, dgb_ref, dg_ref):
        gc, gs = (p0, p1), (p2, p3)
        for s in range(2):
            cols = slice(s * h, (s + 1) * h)
            dm = dm_ref[:, cols]
            sc, ss, sb = _sigmoid(gc[s][...]), _sigmoid(gs[s][...]), _sigmoid(gb_ref[:, cols])
            gav = ga_ref[:, cols]
            dyc_ref[:, cols] = (dm * sc).astype(BF16)
            dg_ref[:, cols] = (dm * yc_ref[:, cols] * sc * (1.0 - sc)).astype(BF16)
            dg_ref[:, D + s * h:D + (s + 1) * h] = (dm * gav * sb * ss * (1.0 - ss)).astype(BF16)
            dys = dm * ss
            dga_ref[:, cols] = (dys * sb).astype(BF16)
            dgb_ref[:, cols] = (dys * gav * sb * (1.0 - sb)).astype(BF16)

    return pl.pallas_call(
        body, name="merge_bwd", grid=(L // tm,),
        in_specs=[_row(D, tm=tm)] + [_row(h, c0 + s, tm) for s in range(4)] + [_row(D, tm=tm)] * 3,
        out_specs=(_row(D, tm=tm), _row(D, tm=tm), _row(D, tm=tm), _row(2 * D, tm=tm)),
        out_shape=(jax.ShapeDtypeStruct((L, D), BF16),) * 3 + (jax.ShapeDtypeStruct((L, 2 * D), BF16),),
        compiler_params=_params(("parallel",)),
    )(dmerged, proj, proj, proj, proj, y_conv, ga, gb)


def conv_ln_bwd(dvs, vc, ln_g, ln_b):
    L, C = vc.shape
    tm = _tm(L)

    def body(d_ref, v_ref, g_ref, b_ref, o_ref, dg_ref, db_ref):
        i = pl.program_id(0)
        v = v_ref[...]
        mu = jnp.mean(v, axis=-1, keepdims=True)
        d = v - mu
        rstd = lax.rsqrt(jnp.mean(d * d, axis=-1, keepdims=True) + EPS)
        xh = d * rstd
        ln = xh * g_ref[...] + b_ref[...]
        sg = _sigmoid(ln)
        dln = d_ref[...] * sg * (1.0 + ln * (1.0 - sg))
        _accum(dg_ref, _colsum(dln * xh), i)
        _accum(db_ref, _colsum(dln), i)
        dxh = dln * g_ref[...]
        o_ref[...] = rstd * (dxh - jnp.mean(dxh, axis=-1, keepdims=True)
                             - xh * jnp.mean(dxh * xh, axis=-1, keepdims=True))

    return pl.pallas_call(
        body, name="conv_ln_bwd", grid=(L // tm,),
        in_specs=[_row(C, tm=tm), _row(C, tm=tm), _vec(C), _vec(C)],
        out_specs=(_row(C, tm=tm), _vec(C), _vec(C)),
        out_shape=(jax.ShapeDtypeStruct((L, C), F32), jax.ShapeDtypeStruct((1, C), F32), jax.ShapeDtypeStruct((1, C), F32)),
        compiler_params=_params(("arbitrary",)),
    )(dvs, vc, ln_g, ln_b)


def conv_bwd(dvc, proj, w_dw):
    L, C = dvc.shape
    tm = _tm(L)
    hb = tm // HALO
    last = L // HALO - 1
    nt = L // tm

    def body(d_ref, dn_ref, a_ref, g_ref, ah_ref, gh_ref, w_ref, o_ref, dw_ref, db_ref, dbuf, vbuf, dsh, vsh, dw8):
        i = pl.program_id(0)
        dcur = d_ref[...]
        dbuf[0:tm, :] = dcur
        dbuf[tm:tm + HALO, :] = dn_ref[...] * jnp.where(i < nt - 1, 1.0, 0.0)
        av, sg = a_ref[...], _sigmoid(g_ref[...])
        vbuf[0:HALO, :] = ah_ref[...] * _sigmoid(gh_ref[...]) * jnp.where(i > 0, 1.0, 0.0)
        vbuf[HALO:HALO + tm, :] = av * sg
        _shifted_copies(dbuf, dsh, tm)
        _shifted_copies(vbuf, vsh, tm)
        dv = jnp.zeros((tm, C), F32)
        for k in range(CONV_K):
            dv = dv + w_ref[k:k + 1, :] * _window(dbuf, dsh, CONV_K - 1 - k, tm)
        o_ref[:, 0:C] = (dv * sg).astype(BF16)
        o_ref[:, C:2 * C] = (dv * av * sg * (1.0 - sg)).astype(BF16)

        @pl.when(i == 0)
        def _():
            dw8[...] = jnp.zeros_like(dw8)

        for k in range(CONV_K):
            prod = dcur * _window(vbuf, vsh, HALO - (CONV_K - 1) + k, tm)
            part = prod[0:8, :]
            for j in range(1, tm // 8):
                part = part + prod[8 * j:8 * j + 8, :]
            dw8[k] += part
        _accum(db_ref, _colsum(dcur), i)

        @pl.when(i == nt - 1)
        def _():
            for k in range(CONV_K):
                dw_ref[k:k + 1, :] = _colsum(dw8[k])
            dw_ref[CONV_K:HALO, :] = jnp.zeros((HALO - CONV_K, C), F32)

    prev = lambda cb: pl.BlockSpec((HALO, C), lambda i: (jnp.maximum(i * hb - 1, 0), cb))
    return pl.pallas_call(
        body, name="conv_bwd", grid=(nt,),
        in_specs=[_row(C, tm=tm), pl.BlockSpec((HALO, C), lambda i: (jnp.minimum((i + 1) * hb, last), 0)),
                  _row(C, 0, tm), _row(C, 1, tm), prev(0), prev(1), pl.BlockSpec((HALO, C), lambda i: (0, 0))],
        out_specs=(_row(2 * C, tm=tm), pl.BlockSpec((HALO, C), lambda i: (0, 0)), _vec(C)),
        out_shape=(jax.ShapeDtypeStruct((L, 2 * C), BF16), jax.ShapeDtypeStruct((HALO, C), F32),
                   jax.ShapeDtypeStruct((1, C), F32)),
        scratch_shapes=[pltpu.VMEM((tm + HALO, C), F32), pltpu.VMEM((HALO + tm, C), F32),
                        pltpu.VMEM((7, HALO + tm - 8, C), F32), pltpu.VMEM((7, HALO + tm - 8, C), F32),
                        pltpu.VMEM((CONV_K, 8, C), F32)],
        compiler_params=_params(("arbitrary",)),
    )(dvc, dvc, proj, proj, proj, proj, w_dw)


def _adamw(w, g, m, v):
    m = ADAM_B1 * m + (1.0 - ADAM_B1) * g
    v = ADAM_B2 * v + (1.0 - ADAM_B2) * (g * g)
    m_hat = m / (1.0 - ADAM_B1 ** ADAM_STEP)
    v_hat = v / (1.0 - ADAM_B2 ** ADAM_STEP)
    delta = -ADAM_LR * (m_hat / (jnp.sqrt(v_hat) + ADAM_EPS) + ADAM_WD * w)
    return delta, m, v


def _tile_rows(R, C):
    tr = 8
    while tr * 2 * C <= 128 * 1024 and R % (tr * 2) == 0:
        tr *= 2
    assert R % tr == 0, (R, C)
    return tr


def sum_devices(parts, name):
    _, R, C = parts.shape
    tr = _tile_rows(R, C)

    def body(p_ref, o_ref):
        s = p_ref[0]
        for j in range(1, NDEV):
            s = s + p_ref[j]
        o_ref[...] = s

    return pl.pallas_call(
        body, name=name, grid=(R // tr,),
        in_specs=[pl.BlockSpec((NDEV, tr, C), lambda i: (0, i, 0))],
        out_specs=pl.BlockSpec((tr, C), lambda i: (i, 0)), out_shape=jax.ShapeDtypeStruct((R, C), F32),
        compiler_params=_params(("parallel",)),
    )(parts)


def adam_update(w, g, m, v, name):
    R, C = w.shape
    tr = _tile_rows(R, C)

    def body(w_ref, g_ref, m_ref, v_ref, d_ref, mo_ref, vo_ref):
        d, mm, vv = _adamw(w_ref[...], g_ref[...], m_ref[...], v_ref[...])
        d_ref[...], mo_ref[...], vo_ref[...] = d, mm, vv

    spec = pl.BlockSpec((tr, C), lambda i: (i, 0))
    return pl.pallas_call(
        body, name=name, grid=(R // tr,), in_specs=[spec] * 4, out_specs=(spec,) * 3,
        out_shape=(jax.ShapeDtypeStruct((R, C), F32),) * 3, compiler_params=_params(("parallel",)),
    )(w, g, m, v)


def adam_reduce(parts, w, m, v, name):
    R, C = w.shape
    tr = _tile_rows(R, C)

    def body(p_ref, w_ref, m_ref, v_ref, g_ref, d_ref, mo_ref, vo_ref):
        g = p_ref[0].astype(F32)
        for j in range(1, NDEV):
            g = g + p_ref[j].astype(F32)
        g_ref[...] = g
        d, mm, vv = _adamw(w_ref[...], g, m_ref[...], v_ref[...])
        d_ref[...], mo_ref[...], vo_ref[...] = d, mm, vv

    spec = pl.BlockSpec((tr, C), lambda i: (i, 0))
    return pl.pallas_call(
        body, name=name, grid=(R // tr,),
        in_specs=[pl.BlockSpec((NDEV, tr, C), lambda i: (0, i, 0)), spec, spec, spec], out_specs=(spec,) * 4,
        out_shape=(jax.ShapeDtypeStruct((R, C), F32),) * 4, compiler_params=_params(("parallel",)),
    )(parts, w, m, v)


def adam_w_ada(c_act, dmod_cols, w, m, v):
    D, n = w.shape
    tn = 256

    def body(c_ref, dm_ref, w_ref, m_ref, v_ref, g_ref, d_ref, mo_ref, vo_ref):
        g = lax.dot_general(c_ref[...].astype(BF16), dm_ref[...].astype(BF16), (((0,), (0,)), ((), ())),
                            preferred_element_type=F32)
        g_ref[...] = g
        d, mm, vv = _adamw(w_ref[...], g, m_ref[...], v_ref[...])
        d_ref[...], mo_ref[...], vo_ref[...] = d, mm, vv

    spec = pl.BlockSpec((D, tn), lambda j: (0, j))
    return pl.pallas_call(
        body, name="adam_w_ada", grid=(n // tn,),
        in_specs=[pl.BlockSpec((NDEV, D), lambda j: (0, 0)), pl.BlockSpec((NDEV, tn), lambda j: (0, j)), spec, spec, spec],
        out_specs=(spec,) * 4, out_shape=(jax.ShapeDtypeStruct((D, n), F32),) * 4,
        compiler_params=_params(("parallel",)),
    )(c_act, dmod_cols, w, m, v)


def _block_diag(m):
    G, a, b = m.shape
    m4 = m.reshape(G // GB, GB, a, b)
    eye = jnp.eye(GB, dtype=m.dtype)
    return (m4[:, :, :, None, :] * eye[None, :, None, :, None]).reshape(G // GB, GB * a, GB * b)


def _diag_blocks(m, a, b):
    nb = m.shape[0]
    m5 = m.reshape(nb, GB, a, GB, b)
    idx = jnp.arange(GB)
    return m5[:, idx, :, idx, :].transpose(1, 0, 2, 3).reshape(nb * GB, a, b)


def _flat_pad(parts, mult):
    flat = jnp.concatenate([p.reshape(-1) for p in parts])
    pad = (-flat.shape[0]) % mult
    return jnp.pad(flat, (0, pad))


def _split(flat, like):
    out, off = [], 0
    for p in like:
        out.append(flat[off:off + p.size].reshape(p.shape))
        off += p.size
    return out


def kernel(x, c, w_ada, b_ada, norm1_g, w_in, w_dw, b_dw, ln_g, ln_b, w_conv_out, a_re, a_im, log_dt, b_re, b_im, c_re, c_im, d_skip, w_glu_a, w_glu_b, w_out, norm2_g, w_ff1, w_ff2, final_g, loss_target, m_w_ada, m_b_ada, m_norm1_g, m_w_in, m_w_dw, m_b_dw, m_ln_g, m_ln_b, m_w_conv_out, m_a_re, m_a_im, m_log_dt, m_b_re, m_b_im, m_c_re, m_c_im, m_d_skip, m_w_glu_a, m_w_glu_b, m_w_out, m_norm2_g, m_w_ff1, m_w_ff2, m_final_g, v_w_ada, v_b_ada, v_norm1_g, v_w_in, v_w_dw, v_b_dw, v_ln_g, v_ln_b, v_w_conv_out, v_a_re, v_a_im, v_log_dt, v_b_re, v_b_im, v_c_re, v_c_im, v_d_skip, v_w_glu_a, v_w_glu_b, v_w_out, v_norm2_g, v_w_ff1, v_w_ff2, v_final_g):
    W = dict(w_ada=w_ada, b_ada=b_ada, norm1_g=norm1_g, w_in=w_in, w_dw=w_dw, b_dw=b_dw, ln_g=ln_g, ln_b=ln_b,
             w_conv_out=w_conv_out, a_re=a_re, a_im=a_im, log_dt=log_dt, b_re=b_re, b_im=b_im, c_re=c_re, c_im=c_im,
             d_skip=d_skip, w_glu_a=w_glu_a, w_glu_b=w_glu_b, w_out=w_out, norm2_g=norm2_g, w_ff1=w_ff1, w_ff2=w_ff2,
             final_g=final_g)
    Mo = dict(w_ada=m_w_ada, b_ada=m_b_ada, norm1_g=m_norm1_g, w_in=m_w_in, w_dw=m_w_dw, b_dw=m_b_dw, ln_g=m_ln_g,
              ln_b=m_ln_b, w_conv_out=m_w_conv_out, a_re=m_a_re, a_im=m_a_im, log_dt=m_log_dt, b_re=m_b_re, b_im=m_b_im,
              c_re=m_c_re, c_im=m_c_im, d_skip=m_d_skip, w_glu_a=m_w_glu_a, w_glu_b=m_w_glu_b, w_out=m_w_out,
              norm2_g=m_norm2_g, w_ff1=m_w_ff1, w_ff2=m_w_ff2, final_g=m_final_g)
    Vo = dict(w_ada=v_w_ada, b_ada=v_b_ada, norm1_g=v_norm1_g, w_in=v_w_in, w_dw=v_w_dw, b_dw=v_b_dw, ln_g=v_ln_g,
              ln_b=v_ln_b, w_conv_out=v_w_conv_out, a_re=v_a_re, a_im=v_a_im, log_dt=v_log_dt, b_re=v_b_re, b_im=v_b_im,
              c_re=v_c_re, c_im=v_c_im, d_skip=v_d_skip, w_glu_a=v_w_glu_a, w_glu_b=v_w_glu_b, w_out=v_w_out,
              norm2_g=v_norm2_g, w_ff1=v_w_ff1, w_ff2=v_w_ff2, final_g=v_final_g)
    names = list(W)

    me = _me()
    xs, tgt = x[0], loss_target[0]
    L, D = xs.shape
    CW = w_dw.shape[2] * NDEV
    G, P = a_re.shape[1], a_re.shape[2]
    H = b_re.shape[3]
    n_ada = w_ada.shape[2]

    (c_all,) = _exchange([c], "gather_c", True)
    b_cols = lax.dynamic_slice(b_ada, (0, me * n_ada), (1, n_ada))
    mod_cols, c_act = adaln_mod(c_all.reshape(NDEV, D), w_ada[0], b_cols)
    (mod_all,) = _exchange([mod_cols], "gather_mod", True)
    mod = lax.dynamic_slice(mod_all, (0, me, 0), (NDEV, 1, n_ada)).reshape(6, 1, D)
    shift1, scale1, gate1, shift2, scale2, gate2 = [mod[j] for j in range(6)]

    big = ["w_in", "w_conv_out", "w_glu_a", "w_glu_b", "w_out", "w_ff1", "w_ff2"]
    order = ["w_in", "w_dw"] + big[1:]
    shards = {k: W[k][0].astype(BF16) for k in big}
    shards["w_dw"] = jnp.pad(w_dw[0], ((0, HALO - CONV_K), (0, 0)))
    gather_handle = dict(zip(order, gather2_start([shards[k] for k in order], "gather_weights_start", mod_all)))

    def forward(ks, name, after):
        gather_handle.update(zip(ks, gather2_forward([gather_handle[k] for k in ks], name, after)))

    def weight(k, after):
        w = gather2_wait(gather_handle[k], "gather_wait_" + k, after)
        if k in ("w_out", "w_ff2"):
            w = w.reshape(1, w.shape[0] * w.shape[1], w.shape[2])
        elif k in narrow:
            w = w.transpose(1, 0, 2).reshape(1, w.shape[1], NDEV * w.shape[2])
        elif k == "w_dw":
            w = w.transpose(1, 0, 2).reshape(HALO, CW)
        return w

    narrow = ("w_conv_out", "w_glu_a", "w_glu_b")
    scatter_handle = {}

    def scatter(k, g):
        if k in narrow:
            g = g.reshape(g.shape[1], NDEV, -1).transpose(1, 0, 2)
        elif g.shape[0] == 1:
            g = g.reshape(NDEV, -1, g.shape[2])
        (scatter_handle[k],), token = exchange_start([g], "scatter_start_" + k, False)
        return token

    u = prenorm(xs, norm1_g, scale1, shift1, "prenorm1")
    forward(["w_in"], "gather_forward_in", u)
    wg = {"w_in": weight("w_in", u)}
    proj = mm_nn(u, wg["w_in"], "in_proj")
    forward(["w_dw", "w_conv_out", "w_glu_a", "w_glu_b", "w_out"], "gather_forward_mix", proj)
    w_dw_full = weight("w_dw", proj)
    vs, vc = conv_fwd(proj, w_dw_full, b_dw, ln_g, ln_b)
    wg["w_conv_out"] = weight("w_conv_out", vs)
    y_conv = mm_nn(vs, wg["w_conv_out"], "conv_out")

    br2 = b_re[0].transpose(0, 2, 1).reshape(G * H, P)
    bi2 = b_im[0].transpose(0, 2, 1).reshape(G * H, P)
    ldt = log_dt[0].reshape(G, 1)
    expand = jnp.repeat(jnp.eye(G, dtype=F32), H, axis=0)
    lbr, lbi, bbr, bbi = s5_params(a_re[0], a_im[0], ldt, br2, bi2, expand)
    tabs = s5_tables(lbr.reshape(1, G * P), lbi.reshape(1, G * P))
    bdr, bdi = _block_diag(bbr.reshape(G, H, P)), _block_diag(bbi.reshape(G, H, P))
    cdr = _block_diag(c_re[0].transpose(0, 2, 1))
    cdi = _block_diag(c_im[0].transpose(0, 2, 1))
    cd2 = jnp.concatenate([cdr, -cdi], axis=1).astype(BF16)
    s_re, s_im, y_pre, yg = s5_fwd(proj, 2 * CW, _rhs3(bdr), _rhs3(bdi), cd2, tabs, d_skip)
    forward(["w_ff1"], "gather_forward_ff1", yg)
    wg["w_glu_a"] = weight("w_glu_a", yg)
    wg["w_glu_b"] = weight("w_glu_b", yg)
    ga = mm_nn(yg, wg["w_glu_a"], "glu_a")
    gb = mm_nn(yg, wg["w_glu_b"], "glu_b")
    merged = merge_fwd(proj, 3 * CW, y_conv, ga, gb)
    forward(["w_ff2"], "gather_forward_ff2", merged)
    wg["w_out"] = weight("w_out", merged)
    m_out = mm_nn(merged, wg["w_out"], "out_proj")
    h1, z = residual_norm(xs, m_out, gate1, norm2_g, scale2, shift2)
    wg["w_ff1"] = weight("w_ff1", z)
    act = mm_nn(z, wg["w_ff1"], "ff1", out_dtype=BF16, epi=lambda r: jnp.square(jnp.maximum(r, 0.0)))
    wg["w_ff2"] = weight("w_ff2", act)
    ff = mm_nn(act, wg["w_ff2"], "ff2")

    dh2, dff, loss_part, d_final_g, d_gate2 = loss_bwd(h1, ff, gate2, final_g.reshape(1, D), tgt)
    df = mm_nt(dff, wg["w_ff2"], "ff2_dx", out_dtype=BF16,
               epi=lambda r, a: r * (2.0 * jnp.sqrt(a.astype(F32))), extras=(act,))
    t = scatter("w_ff2", mm_tn(act, dff, 1, "ff2_dw", out_dtype=BF16))
    t = scatter("w_ff1", mm_tn(z, df, NDEV, "ff1_dw", out_dtype=BF16, dep=t))
    dz = mm_nt(df, wg["w_ff1"], "ff1_dx", dep=t)
    dh1, d_scale2, d_shift2, d_norm2_g, dmo, d_gate1 = norm_bwd(dz, h1, dh2, norm2_g, scale2, "norm2_bwd", gate1, m_out)
    t = scatter("w_out", mm_tn(merged, dmo, 1, "out_dw", out_dtype=BF16))
    dmerged = mm_nt(dmo, wg["w_out"], "out_dx", dep=t)
    dyc, dga, dgb, dproj_g = merge_bwd(dmerged, proj, 3 * CW, y_conv, ga, gb)
    t = scatter("w_conv_out", mm_tn(vs, dyc, 1, "conv_out_dw", out_dtype=BF16))
    t = scatter("w_glu_a", mm_tn(yg, dga, 1, "glu_a_dw", out_dtype=BF16, dep=t))
    t = scatter("w_glu_b", mm_tn(yg, dgb, 1, "glu_b_dw", out_dtype=BF16, dep=t))
    dvs = mm_nt(dyc, wg["w_conv_out"], "conv_out_dx", dep=t)
    dyg_a = mm_nt(dga, wg["w_glu_a"], "glu_a_dx", dep=t)
    dyg_b = mm_nt(dgb, wg["w_glu_b"], "glu_b_dx", dep=t)
    dvc, d_ln_g, d_ln_b = conv_ln_bwd(dvs, vc, ln_g, ln_b)
    dproj_c, d_w_dw, d_b_dw = conv_bwd(dvc, proj, w_dw_full)
    bdt2 = jnp.concatenate([bdr.transpose(0, 2, 1), bdi.transpose(0, 2, 1)], axis=1).astype(BF16)
    cdrt3, cdit3 = _rhs3(cdr.transpose(0, 2, 1)), _rhs3(-cdi.transpose(0, 2, 1))
    dproj_s, d_d_skip, dcdr, dcdi, dbdr, dbdi, dlr8, dli8 = s5_bwd(
        dyg_a, dyg_b, y_pre, proj, 2 * CW, s_re, s_im, bdt2, cdrt3, cdit3, tabs, d_skip)
    dproj = jnp.concatenate([dproj_c, dproj_s, dproj_g], axis=1)
    t = scatter("w_in", mm_tn(u, dproj, NDEV, "in_dw", out_dtype=BF16))
    du = mm_nt(dproj, wg["w_in"], "in_dx", dep=t)
    grad_x, d_scale1, d_shift1, d_norm1_g = norm_bwd(du, xs, dh1, norm1_g, scale1, "norm1_bwd")

    dmod = jnp.concatenate([d_shift1, d_scale1, d_gate1, d_shift2, d_scale2, d_gate2], axis=1)
    d_c_re = _diag_blocks(dcdr, H, P)
    d_c_im = _diag_blocks(dcdi, H, P)
    d_bbr = _diag_blocks(dbdr, H, P)
    d_bbi = _diag_blocks(dbdi, H, P)
    dlr = jnp.sum(dlr8, axis=0).reshape(G, P)
    dli = jnp.sum(dli8, axis=0).reshape(G, P)
    small_parts = [dmod, d_norm1_g, d_b_dw, d_ln_g, d_ln_b, dlr, dli, d_bbr, d_bbi, d_c_re, d_c_im, d_d_skip,
                   d_norm2_g, d_final_g, d_w_dw]
    pack8 = _flat_pad(small_parts, PACK).reshape(NDEV, -1, 1024)
    parts8, dmod_from = _exchange([pack8, dmod.reshape(NDEV, 1, n_ada)], "scatter_small_grads", False)
    (tot8,) = _exchange([sum_devices(parts8, "sum_small_grads")], "gather_small_sums", True)
    tot = tot8.reshape(-1)
    (g_b_ada, g_norm1_g, g_b_dw, g_ln_g, g_ln_b, t_lr, t_li, t_bbr, t_bbi, g_c_re_t, g_c_im_t, g_d_skip,
     g_norm2_g, g_final_g, g_w_dw_full) = _split(tot, small_parts)
    g_a_re, g_a_im, g_ldt, g_br2, g_bi2 = s5_params_bwd(
        a_re[0], a_im[0], ldt, br2, bi2, expand, t_lr, t_li, t_bbr.reshape(G * H, P), t_bbi.reshape(G * H, P))
    g_brt, g_bit = g_br2.reshape(G, H, P), g_bi2.reshape(G, H, P)
    dmod_cols = dmod_from.reshape(NDEV, n_ada)

    grads = {
        "b_ada": g_b_ada, "norm1_g": g_norm1_g, "b_dw": g_b_dw, "ln_g": g_ln_g, "ln_b": g_ln_b,
        "a_re": g_a_re[None], "a_im": g_a_im[None], "log_dt": g_ldt.reshape(1, G),
        "b_re": g_brt.transpose(0, 2, 1)[None], "b_im": g_bit.transpose(0, 2, 1)[None],
        "c_re": g_c_re_t[None], "c_im": g_c_im_t[None], "d_skip": g_d_skip, "norm2_g": g_norm2_g,
        "final_g": g_final_g.reshape(D),
        "w_dw": lax.dynamic_slice(g_w_dw_full, (0, me * (CW // NDEV)), (CONV_K, CW // NDEV))[None],
    }
    small = [k for k in names if k in grads]
    wf = _flat_pad([W[k] for k in small], PACK).reshape(-1, 1024)
    gf = _flat_pad([grads[k] for k in small], PACK).reshape(-1, 1024)
    mf = _flat_pad([Mo[k] for k in small], PACK).reshape(-1, 1024)
    vf = _flat_pad([Vo[k] for k in small], PACK).reshape(-1, 1024)
    d_s, m_s, v_s = adam_update(wf, gf, mf, vf, "adam_small")
    like = [W[k] for k in small]
    delta = dict(zip(small, _split(d_s.reshape(-1), like)))
    new_m = dict(zip(small, _split(m_s.reshape(-1), like)))
    new_v = dict(zip(small, _split(v_s.reshape(-1), like)))

    g, d, mm, vv = adam_w_ada(c_act, dmod_cols, w_ada[0], m_w_ada[0], v_w_ada[0])
    grads["w_ada"], delta["w_ada"], new_m["w_ada"], new_v["w_ada"] = g[None], d[None], mm[None], vv[None]

    after = d
    for k in ("w_ff2", "w_ff1", "w_out", "w_conv_out", "w_glu_a", "w_glu_b", "w_in"):
        parts = exchange_wait(scatter_handle[k], after, "scatter_wait_" + k, False)
        g, d, mm, vv = adam_reduce(parts, W[k][0], Mo[k][0], Vo[k][0], "adam_" + k)
        grads[k], delta[k], new_m[k], new_v[k] = g[None], d[None], mm[None], vv[None]
        after = d

    loss = lax.psum(loss_part[0, 0], ("x", "y", "c"))
    return (loss, grad_x[None], *[grads[k] for k in names], *[delta[k] for k in names],
            *[new_m[k] for k in names], *[new_v[k] for k in names])
```

```python
import functools
import math

import jax
import jax.numpy as jnp
from jax import lax
from jax.experimental import pallas as pl
from jax.experimental.pallas import tpu as pltpu

F32 = jnp.float32
BF16 = jnp.bfloat16
NDEV = 8
EPS = 1e-6
ADAM_LR, ADAM_B1, ADAM_B2, ADAM_EPS, ADAM_WD, ADAM_STEP = 0.001, 0.9, 0.999, 1e-08, 0.01, 10
CONV_K = 31
HALO = 32
GROUP = 16
STATE = 64
GB = 8
S5_ROWS = 512
HI = lax.Precision.HIGHEST
MESH = pl.DeviceIdType.MESH
VMEM_LIMIT = 56 * 1024 * 1024
MAX_CONTRACT = 2048
PACK_ROWS = 64
PACK = PACK_ROWS * 1024
ANY = pl.BlockSpec(memory_space=pl.ANY)


def _params(sem=None):
    if sem is None:
        return pltpu.CompilerParams(vmem_limit_bytes=VMEM_LIMIT)
    return pltpu.CompilerParams(dimension_semantics=sem, vmem_limit_bytes=VMEM_LIMIT)


def _sigmoid(v):
    return 1.0 / (1.0 + jnp.exp(-v))


def _me():
    return 4 * lax.axis_index("x") + 2 * lax.axis_index("y") + lax.axis_index("c")


def _peer(k):
    x, y, c = lax.axis_index("x"), lax.axis_index("y"), lax.axis_index("c")
    px = 1 - x if (k >> 2) & 1 else x
    py = 1 - y if (k >> 1) & 1 else y
    pc = 1 - c if k & 1 else c
    return (px, py, pc), 4 * px + 2 * py + pc


def _exchange(arrays, name, gather):
    n = len(arrays)
    out_shape = []
    for a in arrays:
        shp = (NDEV,) + a.shape if gather else a.shape
        out_shape.append(jax.ShapeDtypeStruct(shp, a.dtype))

    def body(*refs):
        ins, outs = refs[:n], refs[n:2 * n]
        send, recv, lsem = refs[2 * n:]
        me = _me()
        local = []
        for a in range(n):
            src = ins[a] if gather else ins[a].at[me]
            cp = pltpu.make_async_copy(src, outs[a].at[me], lsem.at[a])
            cp.start()
            local.append(cp)
        sends = []
        for a in range(n):
            for k in range(1, NDEV):
                dev, pidx = _peer(k)
                src = ins[a] if gather else ins[a].at[pidx]
                cp = pltpu.make_async_remote_copy(
                    src_ref=src, dst_ref=outs[a].at[me], send_sem=send.at[a * (NDEV - 1) + k - 1], recv_sem=recv.at[a * (NDEV - 1) + k - 1],
                    device_id=dev, device_id_type=MESH)
                cp.start()
                sends.append(cp)
        for a in range(n):
            for k in range(1, NDEV):
                dev, pidx = _peer(k)
                src = ins[a] if gather else ins[a].at[pidx]
                pltpu.make_async_remote_copy(
                    src_ref=src, dst_ref=outs[a].at[pidx], send_sem=send.at[a * (NDEV - 1) + k - 1], recv_sem=recv.at[a * (NDEV - 1) + k - 1],
                    device_id=dev, device_id_type=MESH).wait_recv()
        for cp in sends:
            cp.wait_send()
        for cp in local:
            cp.wait()

    return pl.pallas_call(
        body, name=name, out_shape=tuple(out_shape),
        in_specs=[ANY] * n, out_specs=tuple([ANY] * n),
        scratch_shapes=[pltpu.SemaphoreType.DMA((n * (NDEV - 1),)), pltpu.SemaphoreType.DMA((n * (NDEV - 1),)),
                        pltpu.SemaphoreType.DMA((n,))],
    )(*arrays)


HBM = pl.BlockSpec(memory_space=pltpu.HBM)
SEM = pl.BlockSpec(memory_space=pltpu.SEMAPHORE)
EFFECT = pltpu.SideEffectType.DATAFLOW_SIDE_EFFECTING
NPEER = NDEV - 1


def _landing(block_of_me, shape, dtype):
    land = lax.empty((NDEV,) + tuple(shape), dtype)
    start = (_me(),) + (0,) * len(shape)
    return pltpu.with_memory_space_constraint(lax.dynamic_update_slice(land, block_of_me[None], start), pltpu.HBM)


def exchange_start(arrays, name, gather, after=None):
    n = len(arrays)
    me = _me()
    deps = () if after is None else (after,)
    lands = []
    for a in arrays:
        if gather:
            lands.append(_landing(a, a.shape, a.dtype))
        else:
            mine = lax.dynamic_slice(a, (me,) + (0,) * (a.ndim - 1), (1,) + a.shape[1:])[0]
            lands.append(_landing(mine, a.shape[1:], a.dtype))
    srcs = [pltpu.with_memory_space_constraint(a, pltpu.HBM) for a in arrays]

    def body(*refs):
        ins, lnd = refs[:n], refs[n:2 * n]
        outs = refs[2 * n + len(deps):]
        sends, recvs, token = outs[:n], outs[n:2 * n], outs[-1]
        my = _me()
        for a in range(n):
            for k in range(1, NDEV):
                dev, pidx = _peer(k)
                src = ins[a] if gather else ins[a].at[pidx]
                pltpu.make_async_remote_copy(
                    src_ref=src, dst_ref=lnd[a].at[my], send_sem=sends[a].at[k - 1], recv_sem=recvs[a].at[k - 1],
                    device_id=dev, device_id_type=MESH).start()
        token[...] = jnp.zeros_like(token)

    out_shape = ([pltpu.SemaphoreType.DMA((NPEER,))] * (2 * n)
                 + [pltpu.HBM(a.shape, a.dtype) for a in srcs] + [pltpu.HBM(l.shape, l.dtype) for l in lands]
                 + [jax.ShapeDtypeStruct((8, 128), F32)])
    res = pl.pallas_call(
        body, name=name, out_shape=tuple(out_shape),
        in_specs=[HBM] * (2 * n) + [ANY] * len(deps),
        out_specs=tuple([SEM] * (2 * n) + [HBM] * (2 * n) + [pl.BlockSpec(memory_space=pltpu.VMEM)]),
        input_output_aliases={i: 2 * n + i for i in range(2 * n)},
        compiler_params=pltpu.CompilerParams(has_side_effects=EFFECT),
    )(*srcs, *lands, *deps)
    handles = [(res[a], res[n + a], res[2 * n + a], res[3 * n + a]) for a in range(n)]
    return handles, res[-1]


def exchange_wait(handle, after, name, gather):
    send_sem, recv_sem, src, land = handle

    def body(src_ref, land_ref, s_ref, r_ref, after_ref, src_out, land_out):
        for k in range(1, NDEV):
            dev, pidx = _peer(k)
            s = src_ref if gather else src_ref.at[pidx]
            cp = pltpu.make_async_remote_copy(
                src_ref=s, dst_ref=land_ref.at[pidx], send_sem=s_ref.at[k - 1], recv_sem=r_ref.at[k - 1],
                device_id=dev, device_id_type=MESH)
            cp.wait_send()
            cp.wait_recv()

    return pl.pallas_call(
        body, name=name, out_shape=(pltpu.HBM(src.shape, src.dtype), pltpu.HBM(land.shape, land.dtype)),
        in_specs=(HBM, HBM, SEM, SEM, ANY), out_specs=(HBM, HBM), input_output_aliases={0: 0, 1: 1},
        compiler_params=pltpu.CompilerParams(has_side_effects=EFFECT),
    )(src, land, send_sem, recv_sem, after)[1]


ICI_PEERS = (2, 4, 6)
SIBLING = 1


def gather2_start(blocks, name, after):
    m = len(blocks)
    lands = [_landing(b, b.shape, b.dtype) for b in blocks]
    srcs = [pltpu.with_memory_space_constraint(b, pltpu.HBM) for b in blocks]
    n = len(ICI_PEERS)

    def body(*refs):
        src, lnd = refs[:m], refs[m:2 * m]
        outs = refs[2 * m + 1:]
        send, recv_sib, recv_ici = outs[:m], outs[m:2 * m], outs[2 * m:3 * m]
        my = _me()
        for a in range(m):
            dev, _ = _peer(SIBLING)
            pltpu.make_async_remote_copy(src_ref=src[a], dst_ref=lnd[a].at[my], send_sem=send[a].at[0],
                                         recv_sem=recv_sib[a].at[0], device_id=dev, device_id_type=MESH).start()
            for j, k in enumerate(ICI_PEERS):
                dev, _ = _peer(k)
                pltpu.make_async_remote_copy(src_ref=src[a], dst_ref=lnd[a].at[my], send_sem=send[a].at[1 + j],
                                             recv_sem=recv_ici[a].at[j], device_id=dev, device_id_type=MESH).start()

    out_shape = ([pltpu.SemaphoreType.DMA((1 + n,))] * m + [pltpu.SemaphoreType.DMA((1,))] * m
                 + [pltpu.SemaphoreType.DMA((n,))] * m
                 + [pltpu.HBM(s.shape, s.dtype) for s in srcs] + [pltpu.HBM(l.shape, l.dtype) for l in lands])
    res = pl.pallas_call(
        body, name=name, out_shape=tuple(out_shape),
        in_specs=[HBM] * (2 * m) + [ANY], out_specs=tuple([SEM] * (3 * m) + [HBM] * (2 * m)),
        input_output_aliases={i: 3 * m + i for i in range(2 * m)},
        compiler_params=pltpu.CompilerParams(has_side_effects=EFFECT),
    )(*srcs, *lands, after)
    return [tuple(res[g * m + a] for g in range(5)) for a in range(m)]


def gather2_forward(handles, name, after):
    m = len(handles)
    n = len(ICI_PEERS)
    srcs, lands = [h[3] for h in handles], [h[4] for h in handles]

    def body(*refs):
        src, lnd, recv_ici = refs[:m], refs[m:2 * m], refs[2 * m:3 * m]
        outs = refs[3 * m + 1:]
        fsend, frecv = outs[:m], outs[m:2 * m]
        sib, _ = _peer(SIBLING)
        for a in range(m):
            for j, k in enumerate(ICI_PEERS):
                dev, pidx = _peer(k)
                pltpu.make_async_remote_copy(
                    src_ref=src[a], dst_ref=lnd[a].at[pidx], send_sem=fsend[a].at[j], recv_sem=recv_ici[a].at[j],
                    device_id=dev, device_id_type=MESH).wait_recv()
                pltpu.make_async_remote_copy(
                    src_ref=lnd[a].at[pidx], dst_ref=lnd[a].at[pidx], send_sem=fsend[a].at[j], recv_sem=frecv[a].at[j],
                    device_id=sib, device_id_type=MESH).start()

    out_shape = ([pltpu.SemaphoreType.DMA((n,))] * (2 * m)
                 + [pltpu.HBM(s.shape, s.dtype) for s in srcs] + [pltpu.HBM(l.shape, l.dtype) for l in lands])
    res = pl.pallas_call(
        body, name=name, out_shape=tuple(out_shape),
        in_specs=[HBM] * (2 * m) + [SEM] * m + [ANY], out_specs=tuple([SEM] * (2 * m) + [HBM] * (2 * m)),
        input_output_aliases={i: 2 * m + i for i in range(2 * m)},
        compiler_params=pltpu.CompilerParams(has_side_effects=EFFECT),
    )(*srcs, *lands, *[h[2] for h in handles], after)
    return [(handles[a][0], handles[a][1], res[a], res[m + a], res[2 * m + a], res[3 * m + a]) for a in range(m)]


def gather2_wait(handle, name, after):
    send, recv_sib, fsend, frecv, src, land = handle

    def body(src_ref, land_ref, send_ref, recv_sib_ref, fsend_ref, frecv_ref, after_ref, src_out, land_out):
        sib, sib_idx = _peer(SIBLING)
        own = pltpu.make_async_remote_copy(src_ref=src_ref, dst_ref=land_ref.at[sib_idx], send_sem=send_ref.at[0],
                                           recv_sem=recv_sib_ref.at[0], device_id=sib, device_id_type=MESH)
        own.wait_send()
        own.wait_recv()
        for j, k in enumerate(ICI_PEERS):
            dev, pidx = _peer(k)
            pltpu.make_async_remote_copy(src_ref=src_ref, dst_ref=land_ref.at[pidx], send_sem=send_ref.at[1 + j],
                                         recv_sem=frecv_ref.at[j], device_id=dev, device_id_type=MESH).wait_send()
            _, fidx = _peer(k ^ SIBLING)
            fwd = pltpu.make_async_remote_copy(src_ref=land_ref.at[pidx], dst_ref=land_ref.at[fidx],
                                               send_sem=fsend_ref.at[j], recv_sem=frecv_ref.at[j],
                                               device_id=sib, device_id_type=MESH)
            fwd.wait_send()
            fwd.wait_recv()

    return pl.pallas_call(
        body, name=name, out_shape=(pltpu.HBM(src.shape, src.dtype), pltpu.HBM(land.shape, land.dtype)),
        in_specs=(HBM, HBM, SEM, SEM, SEM, SEM, ANY), out_specs=(HBM, HBM), input_output_aliases={0: 0, 1: 1},
        compiler_params=pltpu.CompilerParams(has_side_effects=EFFECT),
    )(src, land, send, recv_sib, fsend, frecv, after)[1]


def _acc_steps(p, acc, k, nk, finish):
    if nk == 1:
        finish(p)
        return

    @pl.when(k == 0)
    def _():
        acc[...] = p

    @pl.when(k > 0)
    def _():
        acc[...] += p

    @pl.when(k == nk - 1)
    def _():
        finish(acc[...])


def mm_nn(a, w3, name, out_dtype=F32, epi=None, extras=()):
    M, K = a.shape
    J, _, n = w3.shape
    tm, tn, tk = min(1024, M), min(1024, n), min(2048, K)
    q, nk, ne = n // tn, K // tk, len(extras)

    def body(*refs):
        a_ref, w_ref = refs[:2]
        ex, o_ref, acc = refs[2:2 + ne], refs[2 + ne], refs[-1]
        p = jnp.dot(a_ref[...], w_ref[...], preferred_element_type=F32)

        def finish(r):
            if epi is not None:
                r = epi(r, *[e[...] for e in ex])
            o_ref[...] = r.astype(out_dtype)

        _acc_steps(p, acc, pl.program_id(2), nk, finish)

    return pl.pallas_call(
        body, name=name, grid=(M // tm, J * q, nk),
        in_specs=[pl.BlockSpec((tm, tk), lambda i, j, k: (i, k)),
                  pl.BlockSpec((None, tk, tn), lambda i, j, k: (j // q, k, j % q))]
        + [pl.BlockSpec((tm, tn), lambda i, j, k: (i, j))] * ne,
        out_specs=pl.BlockSpec((tm, tn), lambda i, j, k: (i, j)),
        out_shape=jax.ShapeDtypeStruct((M, J * n), out_dtype),
        scratch_shapes=[pltpu.VMEM((tm, tn), F32)],
        compiler_params=_params(("parallel", "parallel", "arbitrary")),
    )(a, w3, *extras)


def mm_nt(dy, w3, name, out_dtype=F32, epi=None, extras=(), dep=None):
    M, _ = dy.shape
    J, K, n = w3.shape
    tm, tn, tkk = min(1024, M), min(MAX_CONTRACT, n), min(1024, K)
    q, ne = n // tn, len(extras)
    s = 1
    while q == 1 and J % (2 * s) == 0 and 2 * s * tn <= MAX_CONTRACT:
        s *= 2
    nk = (J // s) * q
    deps = () if dep is None else (dep,)

    def body(*refs):
        d_ref, w_ref = refs[:2]
        ex, o_ref, acc = refs[2:2 + ne], refs[-2], refs[-1]
        nt = (((1,), (1,)), ((), ()))
        p = lax.dot_general(d_ref[:, 0:tn], w_ref[0], nt, preferred_element_type=F32)
        for j in range(1, s):
            p = p + lax.dot_general(d_ref[:, j * tn:(j + 1) * tn], w_ref[j], nt, preferred_element_type=F32)

        def finish(r):
            if epi is not None:
                r = epi(r, *[e[...] for e in ex])
            o_ref[...] = r.astype(out_dtype)

        _acc_steps(p, acc, pl.program_id(2), nk, finish)

    return pl.pallas_call(
        body, name=name, grid=(M // tm, K // tkk, nk),
        in_specs=[pl.BlockSpec((tm, s * tn), lambda i, kk, c: (i, c)),
                  pl.BlockSpec((s, tkk, tn), lambda i, kk, c: (c // q, kk, c % q))]
        + [pl.BlockSpec((tm, tkk), lambda i, kk, c: (i, kk))] * ne + [ANY] * len(deps),
        out_specs=pl.BlockSpec((tm, tkk), lambda i, kk, c: (i, kk)),
        out_shape=jax.ShapeDtypeStruct((M, K), out_dtype),
        scratch_shapes=[pltpu.VMEM((tm, tkk), F32)],
        compiler_params=_params(("parallel", "parallel", "arbitrary")),
    )(dy, w3, *extras, *deps)


def mm_tn(a, dy, J, name, out_dtype=F32, dep=None):
    M, K = a.shape
    n = dy.shape[1] // J
    tm, tn, tkk = min(MAX_CONTRACT, M), min(1024, n), min(1024, K)
    q, nk = n // tn, M // tm
    deps = () if dep is None else (dep,)

    def body(a_ref, d_ref, *rest):
        o_ref, acc = rest[-2:]
        p = lax.dot_general(a_ref[...], d_ref[...], (((0,), (0,)), ((), ())), preferred_element_type=F32)

        def finish(r):
            o_ref[...] = r.astype(out_dtype)

        _acc_steps(p, acc, pl.program_id(2), nk, finish)

    return pl.pallas_call(
        body, name=name, grid=(K // tkk, J * q, nk),
        in_specs=[pl.BlockSpec((tm, tkk), lambda kk, c, m: (m, kk)),
                  pl.BlockSpec((tm, tn), lambda kk, c, m: (m, c))] + [ANY] * len(deps),
        out_specs=pl.BlockSpec((None, tkk, tn), lambda kk, c, m: (c // q, kk, c % q)),
        out_shape=jax.ShapeDtypeStruct((J, K, n), out_dtype),
        scratch_shapes=[pltpu.VMEM((tkk, tn), F32)],
        compiler_params=_params(("parallel", "parallel", "arbitrary")),
    )(a, dy, *deps)


def _tm(L):
    return min(256, L)


def _row(w, cb=0, tm=None):
    return pl.BlockSpec((tm, w), lambda i: (i, cb))


def _vec(w, cb=0):
    return pl.BlockSpec((1, w), lambda i: (0, cb))


def _accum(ref, val, i):
    @pl.when(i == 0)
    def _():
        ref[...] = val

    @pl.when(i > 0)
    def _():
        ref[...] += val


def _colsum(v):
    return jnp.sum(v, axis=0, keepdims=True)


def _rms(v):
    return lax.rsqrt(jnp.mean(v * v, axis=-1, keepdims=True) + EPS)


def adaln_mod(c_all, w_ada, b_cols):
    B, D = c_all.shape
    n = w_ada.shape[1]
    tn = 512

    def body(c_ref, w_ref, b_ref, o_ref, ca_ref):
        cv = c_ref[...]
        ca = cv * _sigmoid(cv)
        ca_ref[...] = ca
        o_ref[...] = jnp.dot(ca.astype(BF16), w_ref[...].astype(BF16), preferred_element_type=F32) + b_ref[...]

    return pl.pallas_call(
        body, name="adaln_mod", grid=(n // tn,),
        in_specs=[pl.BlockSpec((B, D), lambda j: (0, 0)), pl.BlockSpec((D, tn), lambda j: (0, j)),
                  pl.BlockSpec((1, tn), lambda j: (0, j))],
        out_specs=(pl.BlockSpec((B, tn), lambda j: (0, j)), pl.BlockSpec((B, D), lambda j: (0, 0))),
        out_shape=(jax.ShapeDtypeStruct((B, n), F32), jax.ShapeDtypeStruct((B, D), F32)),
        compiler_params=_params(("arbitrary",)),
    )(c_all, w_ada, b_cols)


def prenorm(x, g, scale, shift, name):
    L, D = x.shape
    tm = _tm(L)

    def body(x_ref, g_ref, sc_ref, sh_ref, u_ref):
        xv = x_ref[...]
        u_ref[...] = (xv * _rms(xv) * g_ref[...] * (1.0 + sc_ref[...]) + sh_ref[...]).astype(BF16)

    return pl.pallas_call(
        body, name=name, grid=(L // tm,),
        in_specs=[_row(D, tm=tm), _vec(D), _vec(D), _vec(D)],
        out_specs=_row(D, tm=tm), out_shape=jax.ShapeDtypeStruct((L, D), BF16),
        compiler_params=_params(("parallel",)),
    )(x, g, scale, shift)


def _shifted_copies(buf, shifted, tm):
    n = HALO + tm - 8
    for r in range(1, 8):
        shifted[r - 1] = buf[pl.ds(r, n), :]


def _window(buf, shifted, off, tm):
    r, base = off % 8, off - off % 8
    if r == 0:
        return buf[pl.ds(base, tm), :]
    return shifted[r - 1, pl.ds(base, tm), :]


def conv_fwd(proj, w_dw, b_dw, ln_g, ln_b):
    L = proj.shape[0]
    C = w_dw.shape[1]
    tm = _tm(L)
    hb = tm // HALO

    def body(a_ref, g_ref, ah_ref, gh_ref, w_ref, b_ref, lg_ref, lb_ref, vs_ref, vc_ref, buf, shifted):
        i = pl.program_id(0)
        halo = ah_ref[...] * _sigmoid(gh_ref[...])
        buf[0:HALO, :] = halo * jnp.where(i > 0, 1.0, 0.0)
        buf[HALO:HALO + tm, :] = a_ref[...] * _sigmoid(g_ref[...])
        _shifted_copies(buf, shifted, tm)
        acc = jnp.zeros((tm, C), F32) + b_ref[...]
        for k in range(CONV_K):
            acc = acc + w_ref[k:k + 1, :] * _window(buf, shifted, HALO - (CONV_K - 1) + k, tm)
        vc_ref[...] = acc
        mu = jnp.mean(acc, axis=-1, keepdims=True)
        d = acc - mu
        var = jnp.mean(d * d, axis=-1, keepdims=True)
        ln = d * lax.rsqrt(var + EPS) * lg_ref[...] + lb_ref[...]
        vs_ref[...] = (ln * _sigmoid(ln)).astype(BF16)

    prev = lambda cb: pl.BlockSpec((HALO, C), lambda i: (jnp.maximum(i * hb - 1, 0), cb))
    return pl.pallas_call(
        body, name="conv_fwd", grid=(L // tm,),
        in_specs=[_row(C, 0, tm), _row(C, 1, tm), prev(0), prev(1),
                  pl.BlockSpec((HALO, C), lambda i: (0, 0)), _vec(C), _vec(C), _vec(C)],
        out_specs=(_row(C, tm=tm), _row(C, tm=tm)),
        out_shape=(jax.ShapeDtypeStruct((L, C), BF16), jax.ShapeDtypeStruct((L, C), F32)),
        scratch_shapes=[pltpu.VMEM((HALO + tm, C), F32), pltpu.VMEM((7, HALO + tm - 8, C), F32)],
        compiler_params=_params(("parallel",)),
    )(proj, proj, proj, proj, w_dw, b_dw, ln_g, ln_b)


def _gelu(v):
    return 0.5 * v * (1.0 + jnp.tanh(math.sqrt(2.0 / math.pi) * (v + 0.044715 * v * v * v)))


def _gelu_grad(v):
    k = math.sqrt(2.0 / math.pi)
    t = jnp.tanh(k * (v + 0.044715 * v * v * v))
    return 0.5 * (1.0 + t) + 0.5 * v * (1.0 - t * t) * k * (1.0 + 3.0 * 0.044715 * v * v)


def s5_param_fn(ar, ai, ldt, br, bi, expand):
    dt = jnp.exp(ldt)
    er = jnp.exp(ar * dt)
    th = ai * dt
    lbr, lbi = er * jnp.cos(th), er * jnp.sin(th)
    nr, ni = lbr - 1.0, lbi
    den = ar * ar + ai * ai
    qr, qi = (nr * ar + ni * ai) / den, (ni * ar - nr * ai) / den
    qre = jnp.dot(expand, qr, precision=HI, preferred_element_type=F32)
    qie = jnp.dot(expand, qi, precision=HI, preferred_element_type=F32)
    return lbr, lbi, qre * br - qie * bi, qre * bi + qie * br


def s5_params(ar, ai, ldt, br2, bi2, expand):
    def body(ar_ref, ai_ref, ld_ref, br_ref, bi_ref, e_ref, o1, o2, o3, o4):
        r = s5_param_fn(ar_ref[...], ai_ref[...], ld_ref[...], br_ref[...], bi_ref[...], e_ref[...])
        o1[...], o2[...], o3[...], o4[...] = r

    s2, s3 = jax.ShapeDtypeStruct(ar.shape, F32), jax.ShapeDtypeStruct(br2.shape, F32)
    return pl.pallas_call(body, name="s5_params", out_shape=(s2, s2, s3, s3), compiler_params=_params())(
        ar, ai, ldt, br2, bi2, expand)


def s5_params_bwd(ar, ai, ldt, br2, bi2, expand, dlr, dli, dbr, dbi):
    def body(ar_ref, ai_ref, ld_ref, br_ref, bi_ref, e_ref, c1, c2, c3, c4, o1, o2, o3, o4, o5):
        e = e_ref[...]
        fn = lambda a, b, c, d, f: s5_param_fn(a, b, c, d, f, e)
        _, vjp = jax.vjp(fn, ar_ref[...], ai_ref[...], ld_ref[...], br_ref[...], bi_ref[...])
        r = vjp((c1[...], c2[...], c3[...], c4[...]))
        o1[...], o2[...], o3[...], o4[...], o5[...] = r

    shapes = tuple(jax.ShapeDtypeStruct(v.shape, F32) for v in (ar, ai, ldt, br2, bi2))
    return pl.pallas_call(body, name="s5_params_bwd", out_shape=shapes, compiler_params=_params())(
        ar, ai, ldt, br2, bi2, expand, dlr, dli, dbr, dbi)


def s5_tables(lr, li):
    C = lr.shape[1]

    def body(lr_ref, li_ref, o_ref):
        row = lax.broadcasted_iota(jnp.int32, (8, C), 0)
        for rev in (0, 1):
            pr = jnp.broadcast_to(lr_ref[...], (8, C))
            pi = jnp.broadcast_to(-li_ref[...] if rev else li_ref[...], (8, C))
            br, bi = pr, pi
            pows = [(pr, pi)]
            for _ in range(7):
                pr, pi = pr * br - pi * bi, pr * bi + pi * br
                pows.append((pr, pi))
            base = 8 * rev
            for s, d in enumerate((1, 2, 4)):
                keep = (row + d <= 7) if rev else (row >= d)
                o_ref[base + 2 * s] = jnp.where(keep, pows[d - 1][0], 0.0)
                o_ref[base + 2 * s + 1] = jnp.where(keep, pows[d - 1][1], 0.0)
            cr, ci = jnp.zeros((8, C), F32), jnp.zeros((8, C), F32)
            for j in range(8):
                e = (8 - j) if rev else (j + 1)
                cr = jnp.where(row == j, pows[e - 1][0], cr)
                ci = jnp.where(row == j, pows[e - 1][1], ci)
            o_ref[base + 6] = cr
            o_ref[base + 7] = ci

    return pl.pallas_call(body, name="s5_tables", out_shape=jax.ShapeDtypeStruct((16, 8, C), F32),
                          compiler_params=_params())(lr, li)


def _scan_tile(xr, xi, tabs, cr, ci, rev):
    for s, d in enumerate((1, 2, 4)):
        tr, ti = tabs[2 * s], tabs[2 * s + 1]
        sh = (8 - d) if rev else d
        sr, si = pltpu.roll(xr, sh, 0), pltpu.roll(xi, sh, 0)
        xr, xi = xr + tr * sr - ti * si, xi + tr * si + ti * sr
    tr, ti = tabs[6], tabs[7]
    xr, xi = xr + tr * cr - ti * ci, xi + tr * ci + ti * cr
    return xr, xi


def _hi_lo(a):
    hi = a.astype(BF16)
    return hi, (a - hi.astype(F32)).astype(BF16)


def _lhs3(a):
    hi, lo = _hi_lo(a)
    return jnp.concatenate([hi, lo, hi], axis=1)


def _rhs3(m):
    hi, lo = _hi_lo(m)
    return jnp.concatenate([hi, hi, lo], axis=-2)


def s5_fwd(proj, col0, bdr3, bdi3, cd2, tabs, d_skip):
    L = proj.shape[0]
    nb, cw3, sw = bdr3.shape
    cw = cw3 // 3
    tl = min(S5_ROWS, L)
    cb0 = col0 // cw

    def body(u_ref, bdr_ref, bdi_ref, cd_ref, t_ref, dk_ref, sr_ref, si_ref, yp_ref, yg_ref, car):
        l = pl.program_id(1)

        @pl.when(l == 0)
        def _():
            car[...] = jnp.zeros_like(car)

        u = u_ref[...]
        u3 = _lhs3(u)
        sr_ref[...] = jnp.dot(u3, bdr_ref[...], preferred_element_type=F32)
        si_ref[...] = jnp.dot(u3, bdi_ref[...], preferred_element_type=F32)

        def tile(i, c):
            tabs = [t_ref[j] for j in range(8)]
            r0 = pl.multiple_of(i * 8, 8)
            xr, xi = _scan_tile(sr_ref[pl.ds(r0, 8), :], si_ref[pl.ds(r0, 8), :], tabs, c[0], c[1], False)
            sr_ref[pl.ds(r0, 8), :] = xr
            si_ref[pl.ds(r0, 8), :] = xi
            return xr[7:8, :], xi[7:8, :]

        c = lax.fori_loop(0, tl // 8, tile, (car[0:1, :], car[1:2, :]))
        car[0:1, :] = c[0]
        car[1:2, :] = c[1]
        s2 = jnp.concatenate([sr_ref[...].astype(BF16), si_ref[...].astype(BF16)], axis=1)
        y = jnp.dot(s2, cd_ref[...], preferred_element_type=F32) + dk_ref[...] * u
        yp_ref[...] = y
        yg_ref[...] = _gelu(y).astype(BF16)

    blk = lambda r, c: pl.BlockSpec((None, r, c), lambda b, l: (b, 0, 0))
    return pl.pallas_call(
        body, name="s5_fwd", grid=(nb, L // tl),
        in_specs=[pl.BlockSpec((tl, cw), lambda b, l: (l, cb0 + b)), blk(cw3, sw), blk(cw3, sw), blk(2 * sw, cw),
                  pl.BlockSpec((8, 8, sw), lambda b, l: (0, 0, b)), pl.BlockSpec((1, cw), lambda b, l: (0, b))],
        out_specs=(pl.BlockSpec((tl, sw), lambda b, l: (l, b)), pl.BlockSpec((tl, sw), lambda b, l: (l, b)),
                   pl.BlockSpec((tl, cw), lambda b, l: (l, b)), pl.BlockSpec((tl, cw), lambda b, l: (l, b))),
        out_shape=(jax.ShapeDtypeStruct((L, nb * sw), F32), jax.ShapeDtypeStruct((L, nb * sw), F32),
                   jax.ShapeDtypeStruct((L, nb * cw), F32), jax.ShapeDtypeStruct((L, nb * cw), BF16)),
        scratch_shapes=[pltpu.VMEM((8, sw), F32)],
        compiler_params=_params(("parallel", "arbitrary")),
    )(proj, bdr3, bdi3, cd2, tabs, d_skip)


def s5_bwd(dyg_a, dyg_b, yp, proj, col0, s_re, s_im, bdt2, cdrt3, cdit3, tabs, d_skip):
    L = proj.shape[0]
    nb, sw2, cw = bdt2.shape
    sw = sw2 // 2
    tl = min(S5_ROWS, L)
    nl = L // tl
    cb0 = col0 // cw
    tb = tl // 8

    def body(da_ref, db_ref, yp_ref, u_ref, sr_ref, si_ref, hr_ref, hi_ref, bdt_ref, cdrt_ref, cdit_ref,
             t_ref, dk_ref, du_ref, ddk_ref, dcr_ref, dci_ref, dbr_ref, dbi_ref, dlr_ref, dli_ref,
             gr, gi, pr, pi, car):
        l = pl.program_id(1)
        first = l == nl - 1

        @pl.when(l == 0)
        def _():
            car[...] = jnp.zeros_like(car)

        u = u_ref[...]
        dy = (da_ref[...].astype(F32) + db_ref[...].astype(F32)) * _gelu_grad(yp_ref[...])
        dy3 = _lhs3(dy)
        gr[...] = jnp.dot(dy3, cdrt_ref[...], preferred_element_type=F32)
        gi[...] = jnp.dot(dy3, cdit_ref[...], preferred_element_type=F32)
        inner = jnp.where(first, 0.0, 1.0)
        pr[0:8, :] = hr_ref[...] * inner
        pi[0:8, :] = hi_ref[...] * inner
        pr[8:8 + tl, :] = sr_ref[...]
        pi[8:8 + tl, :] = si_ref[...]
        row = lax.broadcasted_iota(jnp.int32, (8, sw), 0)

        def tile(j, c):
            tabs = [t_ref[8 + k] for k in range(8)]
            r0 = pl.multiple_of((tb - 1 - j) * 8, 8)
            xr, xi = _scan_tile(gr[pl.ds(r0, 8), :], gi[pl.ds(r0, 8), :], tabs, c[0], c[1], True)
            gr[pl.ds(r0, 8), :] = xr
            gi[pl.ds(r0, 8), :] = xi
            qr = jnp.where(row == 0, pltpu.roll(pr[pl.ds(r0, 8), :], 1, 0), pltpu.roll(pr[pl.ds(r0 + 8, 8), :], 1, 0))
            qi = jnp.where(row == 0, pltpu.roll(pi[pl.ds(r0, 8), :], 1, 0), pltpu.roll(pi[pl.ds(r0 + 8, 8), :], 1, 0))
            return xr[0:1, :], xi[0:1, :], c[2] + xr * qr + xi * qi, c[3] + xi * qr - xr * qi

        z = jnp.zeros((8, sw), F32)
        c = lax.fori_loop(0, tb, tile, (car[0:1, :], car[1:2, :], z, z))
        car[0:1, :] = c[0]
        car[1:2, :] = c[1]
        g_re, g_im = gr[...].astype(BF16), gi[...].astype(BF16)
        g2 = jnp.concatenate([g_re, g_im], axis=1)
        du_ref[...] = (dy * dk_ref[...] + jnp.dot(g2, bdt_ref[...], preferred_element_type=F32)).astype(BF16)
        tn = (((0,), (0,)), ((), ()))
        dyb, ub = dy.astype(BF16), u.astype(BF16)
        _accum(ddk_ref, _colsum(dy * u), l)
        _accum(dcr_ref, lax.dot_general(dyb, sr_ref[...].astype(BF16), tn, preferred_element_type=F32), l)
        _accum(dci_ref, -lax.dot_general(dyb, si_ref[...].astype(BF16), tn, preferred_element_type=F32), l)
        _accum(dbr_ref, lax.dot_general(ub, g_re, tn, preferred_element_type=F32), l)
        _accum(dbi_ref, lax.dot_general(ub, g_im, tn, preferred_element_type=F32), l)
        _accum(dlr_ref, c[2], l)
        _accum(dli_ref, c[3], l)

    rl = lambda l: nl - 1 - l
    cblk = lambda w, off=0: pl.BlockSpec((tl, w), lambda b, l: (rl(l), off + b))
    halo = pl.BlockSpec((8, sw), lambda b, l: (jnp.maximum(rl(l) * tb - 1, 0), b))
    mat = lambda r, c: pl.BlockSpec((None, r, c), lambda b, l: (b, 0, 0))
    return pl.pallas_call(
        body, name="s5_bwd", grid=(nb, nl),
        in_specs=[cblk(cw), cblk(cw), cblk(cw), cblk(cw, cb0), cblk(sw), cblk(sw), halo, halo,
                  mat(2 * sw, cw), mat(3 * cw, sw), mat(3 * cw, sw),
                  pl.BlockSpec((16, 8, sw), lambda b, l: (0, 0, b)), pl.BlockSpec((1, cw), lambda b, l: (0, b))],
        out_specs=(cblk(cw), pl.BlockSpec((1, cw), lambda b, l: (0, b)), mat(cw, sw), mat(cw, sw), mat(cw, sw), mat(cw, sw),
                   pl.BlockSpec((8, sw), lambda b, l: (0, b)), pl.BlockSpec((8, sw), lambda b, l: (0, b))),
        out_shape=(jax.ShapeDtypeStruct((L, nb * cw), BF16), jax.ShapeDtypeStruct((1, nb * cw), F32),
                   jax.ShapeDtypeStruct((nb, cw, sw), F32), jax.ShapeDtypeStruct((nb, cw, sw), F32),
                   jax.ShapeDtypeStruct((nb, cw, sw), F32), jax.ShapeDtypeStruct((nb, cw, sw), F32),
                   jax.ShapeDtypeStruct((8, nb * sw), F32), jax.ShapeDtypeStruct((8, nb * sw), F32)),
        scratch_shapes=[pltpu.VMEM((tl, sw), F32), pltpu.VMEM((tl, sw), F32),
                        pltpu.VMEM((tl + 8, sw), F32), pltpu.VMEM((tl + 8, sw), F32), pltpu.VMEM((8, sw), F32)],
        compiler_params=_params(("parallel", "arbitrary")),
    )(dyg_a, dyg_b, yp, proj, s_re, s_im, s_re, s_im, bdt2, cdrt3, cdit3, tabs, d_skip)


def merge_fwd(proj, col_gc, y_conv, ga, gb):
    L, D = y_conv.shape
    tm = _tm(L)
    h = D // 2
    c0 = col_gc // h

    def body(p0, p1, p2, p3, yc_ref, ga_ref, gb_ref, o_ref):
        gc, gs = (p0, p1), (p2, p3)
        for s in range(2):
            cols = slice(s * h, (s + 1) * h)
            y_ssm = ga_ref[:, cols].astype(F32) * _sigmoid(gb_ref[:, cols].astype(F32))
            o_ref[:, cols] = (_sigmoid(gc[s][...]) * yc_ref[:, cols].astype(F32)
                              + _sigmoid(gs[s][...]) * y_ssm).astype(BF16)

    return pl.pallas_call(
        body, name="merge_fwd", grid=(L // tm,),
        in_specs=[_row(h, c0 + s, tm) for s in range(4)] + [_row(D, tm=tm)] * 3,
        out_specs=_row(D, tm=tm), out_shape=jax.ShapeDtypeStruct((L, D), BF16),
        compiler_params=_params(("parallel",)),
    )(proj, proj, proj, proj, y_conv, ga, gb)


def residual_norm(x, m_out, gate, g, scale, shift):
    L, D = x.shape
    tm = _tm(L)

    def body(x_ref, m_ref, gt_ref, g_ref, sc_ref, sh_ref, h_ref, z_ref):
        h = x_ref[...] + gt_ref[...] * m_ref[...]
        h_ref[...] = h
        z_ref[...] = (h * _rms(h) * g_ref[...] * (1.0 + sc_ref[...]) + sh_ref[...]).astype(BF16)

    return pl.pallas_call(
        body, name="residual_norm", grid=(L // tm,),
        in_specs=[_row(D, tm=tm), _row(D, tm=tm), _vec(D), _vec(D), _vec(D), _vec(D)],
        out_specs=(_row(D, tm=tm), _row(D, tm=tm)),
        out_shape=(jax.ShapeDtypeStruct((L, D), F32), jax.ShapeDtypeStruct((L, D), BF16)),
        compiler_params=_params(("parallel",)),
    )(x, m_out, gate, g, scale, shift)


def loss_bwd(h1, ff, gate2, final_g, target):
    L, D = h1.shape
    tm = _tm(L)

    def body(h_ref, f_ref, gt_ref, g_ref, t_ref, dh_ref, dff_ref, loss_ref, dg_ref, dgt_ref):
        i = pl.program_id(0)
        ffv = f_ref[...]
        h2 = h_ref[...] + gt_ref[...] * ffv
        r = _rms(h2)
        n = h2 * r
        err = n * g_ref[...] - t_ref[...]
        per_tok = jnp.mean(err * err, axis=-1, keepdims=True)
        _accum(loss_ref, 0.5 * jnp.sum(per_tok, axis=0, keepdims=True), i)
        dy = err * (1.0 / D)
        _accum(dg_ref, _colsum(dy * n), i)
        dn = dy * g_ref[...]
        dh2 = r * (dn - n * jnp.mean(dn * n, axis=-1, keepdims=True))
        dh_ref[...] = dh2
        dff_ref[...] = (gt_ref[...] * dh2).astype(BF16)
        _accum(dgt_ref, _colsum(dh2 * ffv), i)

    return pl.pallas_call(
        body, name="loss_bwd", grid=(L // tm,),
        in_specs=[_row(D, tm=tm), _row(D, tm=tm), _vec(D), _vec(D), _row(D, tm=tm)],
        out_specs=(_row(D, tm=tm), _row(D, tm=tm), pl.BlockSpec((1, 1), lambda i: (0, 0)), _vec(D), _vec(D)),
        out_shape=(jax.ShapeDtypeStruct((L, D), F32), jax.ShapeDtypeStruct((L, D), BF16),
                   jax.ShapeDtypeStruct((1, 1), F32), jax.ShapeDtypeStruct((1, D), F32), jax.ShapeDtypeStruct((1, D), F32)),
        compiler_params=_params(("arbitrary",)),
    )(h1, ff, gate2, final_g, target)


def norm_bwd(dz, h, dh_in, g, scale, name, gate=None, m_out=None):
    L, D = h.shape
    tm = _tm(L)
    tail = gate is not None

    def body(*refs):
        dz_ref, h_ref, di_ref, g_ref, sc_ref = refs[:5]
        rest = refs[5:]
        if tail:
            gt_ref, m_ref = rest[:2]
            rest = rest[2:]
        dh_ref, dsc_ref, dsh_ref, dg_ref = rest[:4]
        i = pl.program_id(0)
        hv, dzv = h_ref[...], dz_ref[...].astype(F32)
        r = _rms(hv)
        n = hv * r
        _accum(dsc_ref, _colsum(dzv * n * g_ref[...]), i)
        _accum(dsh_ref, _colsum(dzv), i)
        dzn = dzv * (1.0 + sc_ref[...])
        _accum(dg_ref, _colsum(dzn * n), i)
        dn = dzn * g_ref[...]
        dh = di_ref[...] + r * (dn - n * jnp.mean(dn * n, axis=-1, keepdims=True))
        dh_ref[...] = dh
        if tail:
            dmo_ref, dgt_ref = rest[4:]
            dmo_ref[...] = (gt_ref[...] * dh).astype(BF16)
            _accum(dgt_ref, _colsum(dh * m_ref[...]), i)

    ins = [dz, h, dh_in, g, scale]
    in_specs = [_row(D, tm=tm)] * 3 + [_vec(D)] * 2
    out_specs = [_row(D, tm=tm), _vec(D), _vec(D), _vec(D)]
    out_shape = [jax.ShapeDtypeStruct((L, D), F32)] + [jax.ShapeDtypeStruct((1, D), F32)] * 3
    if tail:
        ins += [gate, m_out]
        in_specs += [_vec(D), _row(D, tm=tm)]
        out_specs += [_row(D, tm=tm), _vec(D)]
        out_shape += [jax.ShapeDtypeStruct((L, D), BF16), jax.ShapeDtypeStruct((1, D), F32)]
    return pl.pallas_call(
        body, name=name, grid=(L // tm,), in_specs=in_specs, out_specs=tuple(out_specs), out_shape=tuple(out_shape),
        compiler_params=_params(("arbitrary",)),
    )(*ins)


def merge_bwd(dmerged, proj, col_gc, y_conv, ga, gb):
    L, D = y_conv.shape
    tm = _tm(L)
    h = D // 2
    c0 = col_gc // h

    def body(dm_ref, p0, p1, p2, p3, yc_ref, ga_ref, gb_ref, dyc_ref, dga_ref, dgb_ref, dg_ref):
        gc, gs = (p0, p1), (p2, p3)
        for s in range(2):
            cols = slice(s * h, (s + 1) * h)
            dm = dm_ref[:, cols].astype(F32)
            sc, ss, sb = _sigmoid(gc[s][...]), _sigmoid(gs[s][...]), _sigmoid(gb_ref[:, cols].astype(F32))
            gav = ga_ref[:, cols].astype(F32)
            dyc_ref[:, cols] = (dm * sc).astype(BF16)
            dg_ref[:, cols] = (dm * yc_ref[:, cols].astype(F32) * sc * (1.0 - sc)).astype(BF16)
            dg_ref[:, D + s * h:D + (s + 1) * h] = (dm * gav * sb * ss * (1.0 - ss)).astype(BF16)
            dys = dm * ss
            dga_ref[:, cols] = (dys * sb).astype(BF16)
            dgb_ref[:, cols] = (dys * gav * sb * (1.0 - sb)).astype(BF16)

    return pl.pallas_call(
        body, name="merge_bwd", grid=(L // tm,),
        in_specs=[_row(D, tm=tm)] + [_row(h, c0 + s, tm) for s in range(4)] + [_row(D, tm=tm)] * 3,
        out_specs=(_row(D, tm=tm), _row(D, tm=tm), _row(D, tm=tm), _row(2 * D, tm=tm)),
        out_shape=(jax.ShapeDtypeStruct((L, D), BF16),) * 3 + (jax.ShapeDtypeStruct((L, 2 * D), BF16),),
        compiler_params=_params(("parallel",)),
    )(dmerged, proj, proj, proj, proj, y_conv, ga, gb)


def conv_ln_bwd(dvs, vc, ln_g, ln_b):
    L, C = vc.shape
    tm = _tm(L)

    def body(d_ref, v_ref, g_ref, b_ref, o_ref, dg_ref, db_ref):
        i = pl.program_id(0)
        v = v_ref[...]
        mu = jnp.mean(v, axis=-1, keepdims=True)
        d = v - mu
        rstd = lax.rsqrt(jnp.mean(d * d, axis=-1, keepdims=True) + EPS)
        xh = d * rstd
        ln = xh * g_ref[...] + b_ref[...]
        sg = _sigmoid(ln)
        dln = d_ref[...].astype(F32) * sg * (1.0 + ln * (1.0 - sg))
        _accum(dg_ref, _colsum(dln * xh), i)
        _accum(db_ref, _colsum(dln), i)
        dxh = dln * g_ref[...]
        o_ref[...] = rstd * (dxh - jnp.mean(dxh, axis=-1, keepdims=True)
                             - xh * jnp.mean(dxh * xh, axis=-1, keepdims=True))

    return pl.pallas_call(
        body, name="conv_ln_bwd", grid=(L // tm,),
        in_specs=[_row(C, tm=tm), _row(C, tm=tm), _vec(C), _vec(C)],
        out_specs=(_row(C, tm=tm), _vec(C), _vec(C)),
        out_shape=(jax.ShapeDtypeStruct((L, C), F32), jax.ShapeDtypeStruct((1, C), F32), jax.ShapeDtypeStruct((1, C), F32)),
        compiler_params=_params(("arbitrary",)),
    )(dvs, vc, ln_g, ln_b)


def conv_bwd(dvc, proj, w_dw):
    L, C = dvc.shape
    tm = _tm(L)
    hb = tm // HALO
    last = L // HALO - 1
    nt = L // tm

    def body(d_ref, dn_ref, a_ref, g_ref, ah_ref, gh_ref, w_ref, o_ref, dw_ref, db_ref, dbuf, vbuf, dsh, vsh, dw8):
        i = pl.program_id(0)
        dcur = d_ref[...]
        dbuf[0:tm, :] = dcur
        dbuf[tm:tm + HALO, :] = dn_ref[...] * jnp.where(i < nt - 1, 1.0, 0.0)
        av, sg = a_ref[...], _sigmoid(g_ref[...])
        vbuf[0:HALO, :] = ah_ref[...] * _sigmoid(gh_ref[...]) * jnp.where(i > 0, 1.0, 0.0)
        vbuf[HALO:HALO + tm, :] = av * sg
        _shifted_copies(dbuf, dsh, tm)
        _shifted_copies(vbuf, vsh, tm)
        dv = jnp.zeros((tm, C), F32)
        for k in range(CONV_K):
            dv = dv + w_ref[k:k + 1, :] * _window(dbuf, dsh, CONV_K - 1 - k, tm)
        o_ref[:, 0:C] = (dv * sg).astype(BF16)
        o_ref[:, C:2 * C] = (dv * av * sg * (1.0 - sg)).astype(BF16)

        @pl.when(i == 0)
        def _():
            dw8[...] = jnp.zeros_like(dw8)

        for k in range(CONV_K):
            prod = dcur * _window(vbuf, vsh, HALO - (CONV_K - 1) + k, tm)
            part = prod[0:8, :]
            for j in range(1, tm // 8):
                part = part + prod[8 * j:8 * j + 8, :]
            dw8[k] += part
        _accum(db_ref, _colsum(dcur), i)

        @pl.when(i == nt - 1)
        def _():
            for k in range(CONV_K):
                dw_ref[k:k + 1, :] = _colsum(dw8[k])
            dw_ref[CONV_K:HALO, :] = jnp.zeros((HALO - CONV_K, C), F32)

    prev = lambda cb: pl.BlockSpec((HALO, C), lambda i: (jnp.maximum(i * hb - 1, 0), cb))
    return pl.pallas_call(
        body, name="conv_bwd", grid=(nt,),
        in_specs=[_row(C, tm=tm), pl.BlockSpec((HALO, C), lambda i: (jnp.minimum((i + 1) * hb, last), 0)),
                  _row(C, 0, tm), _row(C, 1, tm), prev(0), prev(1), pl.BlockSpec((HALO, C), lambda i: (0, 0))],
        out_specs=(_row(2 * C, tm=tm), pl.BlockSpec((HALO, C), lambda i: (0, 0)), _vec(C)),
        out_shape=(jax.ShapeDtypeStruct((L, 2 * C), BF16), jax.ShapeDtypeStruct((HALO, C), F32),
                   jax.ShapeDtypeStruct((1, C), F32)),
        scratch_shapes=[pltpu.VMEM((tm + HALO, C), F32), pltpu.VMEM((HALO + tm, C), F32),
                        pltpu.VMEM((7, HALO + tm - 8, C), F32), pltpu.VMEM((7, HALO + tm - 8, C), F32),
                        pltpu.VMEM((CONV_K, 8, C), F32)],
        compiler_params=_params(("arbitrary",)),
    )(dvc, dvc, proj, proj, proj, proj, w_dw)


def _adamw(w, g, m, v):
    m = ADAM_B1 * m + (1.0 - ADAM_B1) * g
    v = ADAM_B2 * v + (1.0 - ADAM_B2) * (g * g)
    m_hat = m / (1.0 - ADAM_B1 ** ADAM_STEP)
    v_hat = v / (1.0 - ADAM_B2 ** ADAM_STEP)
    delta = -ADAM_LR * (m_hat / (jnp.sqrt(v_hat) + ADAM_EPS) + ADAM_WD * w)
    return delta, m, v


def _tile_rows(R, C):
    tr = 8
    while tr * 2 * C <= 128 * 1024 and R % (tr * 2) == 0:
        tr *= 2
    assert R % tr == 0, (R, C)
    return tr


def sum_devices(parts, name):
    _, R, C = parts.shape
    tr = _tile_rows(R, C)

    def body(p_ref, o_ref):
        s = p_ref[0]
        for j in range(1, NDEV):
            s = s + p_ref[j]
        o_ref[...] = s

    return pl.pallas_call(
        body, name=name, grid=(R // tr,),
        in_specs=[pl.BlockSpec((NDEV, tr, C), lambda i: (0, i, 0))],
        out_specs=pl.BlockSpec((tr, C), lambda i: (i, 0)), out_shape=jax.ShapeDtypeStruct((R, C), F32),
        compiler_params=_params(("parallel",)),
    )(parts)


def adam_update(w, g, m, v, name):
    R, C = w.shape
    tr = _tile_rows(R, C)

    def body(w_ref, g_ref, m_ref, v_ref, d_ref, mo_ref, vo_ref):
        d, mm, vv = _adamw(w_ref[...], g_ref[...], m_ref[...], v_ref[...])
        d_ref[...], mo_ref[...], vo_ref[...] = d, mm, vv

    spec = pl.BlockSpec((tr, C), lambda i: (i, 0))
    return pl.pallas_call(
        body, name=name, grid=(R // tr,), in_specs=[spec] * 4, out_specs=(spec,) * 3,
        out_shape=(jax.ShapeDtypeStruct((R, C), F32),) * 3, compiler_params=_params(("parallel",)),
    )(w, g, m, v)


def adam_reduce(parts, w, m, v, name):
    R, C = w.shape
    tr = _tile_rows(R, C)

    def body(p_ref, w_ref, m_ref, v_ref, g_ref, d_ref, mo_ref, vo_ref):
        g = p_ref[0].astype(F32)
        for j in range(1, NDEV):
            g = g + p_ref[j].astype(F32)
        g_ref[...] = g
        d, mm, vv = _adamw(w_ref[...], g, m_ref[...], v_ref[...])
        d_ref[...], mo_ref[...], vo_ref[...] = d, mm, vv

    spec = pl.BlockSpec((tr, C), lambda i: (i, 0))
    return pl.pallas_call(
        body, name=name, grid=(R // tr,),
        in_specs=[pl.BlockSpec((NDEV, tr, C), lambda i: (0, i, 0)), spec, spec, spec], out_specs=(spec,) * 4,
        out_shape=(jax.ShapeDtypeStruct((R, C), F32),) * 4, compiler_params=_params(("parallel",)),
    )(parts, w, m, v)


def adam_w_ada(c_act, dmod_cols, w, m, v):
    D, n = w.shape
    tn = 256

    def body(c_ref, dm_ref, w_ref, m_ref, v_ref, g_ref, d_ref, mo_ref, vo_ref):
        g = lax.dot_general(c_ref[...].astype(BF16), dm_ref[...].astype(BF16), (((0,), (0,)), ((), ())),
                            preferred_element_type=F32)
        g_ref[...] = g
        d, mm, vv = _adamw(w_ref[...], g, m_ref[...], v_ref[...])
        d_ref[...], mo_ref[...], vo_ref[...] = d, mm, vv

    spec = pl.BlockSpec((D, tn), lambda j: (0, j))
    return pl.pallas_call(
        body, name="adam_w_ada", grid=(n // tn,),
        in_specs=[pl.BlockSpec((NDEV, D), lambda j: (0, 0)), pl.BlockSpec((NDEV, tn), lambda j: (0, j)), spec, spec, spec],
        out_specs=(spec,) * 4, out_shape=(jax.ShapeDtypeStruct((D, n), F32),) * 4,
        compiler_params=_params(("parallel",)),
    )(c_act, dmod_cols, w, m, v)


def _block_diag(m):
    G, a, b = m.shape
    m4 = m.reshape(G // GB, GB, a, b)
    eye = jnp.eye(GB, dtype=m.dtype)
    return (m4[:, :, :, None, :] * eye[None, :, None, :, None]).reshape(G // GB, GB * a, GB * b)


def _diag_blocks(m, a, b):
    nb = m.shape[0]
    m5 = m.reshape(nb, GB, a, GB, b)
    idx = jnp.arange(GB)
    return m5[:, idx, :, idx, :].transpose(1, 0, 2, 3).reshape(nb * GB, a, b)


def _flat_pad(parts, mult):
    flat = jnp.concatenate([p.reshape(-1) for p in parts])
    pad = (-flat.shape[0]) % mult
    return jnp.pad(flat, (0, pad))


def _split(flat, like):
    out, off = [], 0
    for p in like:
        out.append(flat[off:off + p.size].reshape(p.shape))
        off += p.size
    return out


def kernel(x, c, w_ada, b_ada, norm1_g, w_in, w_dw, b_dw, ln_g, ln_b, w_conv_out, a_re, a_im, log_dt, b_re, b_im, c_re, c_im, d_skip, w_glu_a, w_glu_b, w_out, norm2_g, w_ff1, w_ff2, final_g, loss_target, m_w_ada, m_b_ada, m_norm1_g, m_w_in, m_w_dw, m_b_dw, m_ln_g, m_ln_b, m_w_conv_out, m_a_re, m_a_im, m_log_dt, m_b_re, m_b_im, m_c_re, m_c_im, m_d_skip, m_w_glu_a, m_w_glu_b, m_w_out, m_norm2_g, m_w_ff1, m_w_ff2, m_final_g, v_w_ada, v_b_ada, v_norm1_g, v_w_in, v_w_dw, v_b_dw, v_ln_g, v_ln_b, v_w_conv_out, v_a_re, v_a_im, v_log_dt, v_b_re, v_b_im, v_c_re, v_c_im, v_d_skip, v_w_glu_a, v_w_glu_b, v_w_out, v_norm2_g, v_w_ff1, v_w_ff2, v_final_g):
    W = dict(w_ada=w_ada, b_ada=b_ada, norm1_g=norm1_g, w_in=w_in, w_dw=w_dw, b_dw=b_dw, ln_g=ln_g, ln_b=ln_b,
             w_conv_out=w_conv_out, a_re=a_re, a_im=a_im, log_dt=log_dt, b_re=b_re, b_im=b_im, c_re=c_re, c_im=c_im,
             d_skip=d_skip, w_glu_a=w_glu_a, w_glu_b=w_glu_b, w_out=w_out, norm2_g=norm2_g, w_ff1=w_ff1, w_ff2=w_ff2,
             final_g=final_g)
    Mo = dict(w_ada=m_w_ada, b_ada=m_b_ada, norm1_g=m_norm1_g, w_in=m_w_in, w_dw=m_w_dw, b_dw=m_b_dw, ln_g=m_ln_g,
              ln_b=m_ln_b, w_conv_out=m_w_conv_out, a_re=m_a_re, a_im=m_a_im, log_dt=m_log_dt, b_re=m_b_re, b_im=m_b_im,
              c_re=m_c_re, c_im=m_c_im, d_skip=m_d_skip, w_glu_a=m_w_glu_a, w_glu_b=m_w_glu_b, w_out=m_w_out,
              norm2_g=m_norm2_g, w_ff1=m_w_ff1, w_ff2=m_w_ff2, final_g=m_final_g)
    Vo = dict(w_ada=v_w_ada, b_ada=v_b_ada, norm1_g=v_norm1_g, w_in=v_w_in, w_dw=v_w_dw, b_dw=v_b_dw, ln_g=v_ln_g,
              ln_b=v_ln_b, w_conv_out=v_w_conv_out, a_re=v_a_re, a_im=v_a_im, log_dt=v_log_dt, b_re=v_b_re, b_im=v_b_im,
              c_re=v_c_re, c_im=v_c_im, d_skip=v_d_skip, w_glu_a=v_w_glu_a, w_glu_b=v_w_glu_b, w_out=v_w_out,
              norm2_g=v_norm2_g, w_ff1=v_w_ff1, w_ff2=v_w_ff2, final_g=v_final_g)
    names = list(W)

    me = _me()
    xs, tgt = x[0], loss_target[0]
    L, D = xs.shape
    CW = w_dw.shape[2] * NDEV
    G, P = a_re.shape[1], a_re.shape[2]
    H = b_re.shape[3]
    n_ada = w_ada.shape[2]

    (c_all,) = _exchange([c], "gather_c", True)
    b_cols = lax.dynamic_slice(b_ada, (0, me * n_ada), (1, n_ada))
    mod_cols, c_act = adaln_mod(c_all.reshape(NDEV, D), w_ada[0], b_cols)
    (mod_all,) = _exchange([mod_cols], "gather_mod", True)
    mod = lax.dynamic_slice(mod_all, (0, me, 0), (NDEV, 1, n_ada)).reshape(6, 1, D)
    shift1, scale1, gate1, shift2, scale2, gate2 = [mod[j] for j in range(6)]

    big = ["w_in", "w_conv_out", "w_glu_a", "w_glu_b", "w_out", "w_ff1", "w_ff2"]
    order = ["w_in", "w_dw"] + big[1:]
    shards = {k: W[k][0].astype(BF16) for k in big}
    shards["w_dw"] = jnp.pad(w_dw[0], ((0, HALO - CONV_K), (0, 0)))
    gather_handle = dict(zip(order, gather2_start([shards[k] for k in order], "gather_weights_start", mod_all)))

    def forward(ks, name, after):
        gather_handle.update(zip(ks, gather2_forward([gather_handle[k] for k in ks], name, after)))

    def weight(k, after):
        w = gather2_wait(gather_handle[k], "gather_wait_" + k, after)
        if k in ("w_out", "w_ff2"):
            w = w.reshape(1, w.shape[0] * w.shape[1], w.shape[2])
        elif k in narrow:
            w = w.transpose(1, 0, 2).reshape(1, w.shape[1], NDEV * w.shape[2])
        elif k == "w_dw":
            w = w.transpose(1, 0, 2).reshape(HALO, CW)
        return w

    narrow = ("w_conv_out", "w_glu_a", "w_glu_b")
    scatter_handle = {}

    def scatter(k, g):
        if k in narrow:
            g = g.reshape(g.shape[1], NDEV, -1).transpose(1, 0, 2)
        elif g.shape[0] == 1:
            g = g.reshape(NDEV, -1, g.shape[2])
        (scatter_handle[k],), token = exchange_start([g], "scatter_start_" + k, False)
        return token

    big_out = {}

    def finish_weight(k, after):
        parts = exchange_wait(scatter_handle[k], after, "scatter_wait_" + k, False)
        big_out[k] = adam_reduce(parts, W[k][0], Mo[k][0], Vo[k][0], "adam_" + k)
        return big_out[k][1]

    u = prenorm(xs, norm1_g, scale1, shift1, "prenorm1")
    forward(["w_in"], "gather_forward_in", u)
    wg = {"w_in": weight("w_in", u)}
    proj = mm_nn(u, wg["w_in"], "in_proj")
    forward(["w_dw", "w_conv_out", "w_glu_a", "w_glu_b", "w_out"], "gather_forward_mix", proj)
    w_dw_full = weight("w_dw", proj)
    vs, vc = conv_fwd(proj, w_dw_full, b_dw, ln_g, ln_b)
    wg["w_conv_out"] = weight("w_conv_out", vs)
    y_conv = mm_nn(vs, wg["w_conv_out"], "conv_out", out_dtype=BF16)

    br2 = b_re[0].transpose(0, 2, 1).reshape(G * H, P)
    bi2 = b_im[0].transpose(0, 2, 1).reshape(G * H, P)
    ldt = log_dt[0].reshape(G, 1)
    expand = jnp.repeat(jnp.eye(G, dtype=F32), H, axis=0)
    lbr, lbi, bbr, bbi = s5_params(a_re[0], a_im[0], ldt, br2, bi2, expand)
    tabs = s5_tables(lbr.reshape(1, G * P), lbi.reshape(1, G * P))
    bdr, bdi = _block_diag(bbr.reshape(G, H, P)), _block_diag(bbi.reshape(G, H, P))
    cdr = _block_diag(c_re[0].transpose(0, 2, 1))
    cdi = _block_diag(c_im[0].transpose(0, 2, 1))
    cd2 = jnp.concatenate([cdr, -cdi], axis=1).astype(BF16)
    s_re, s_im, y_pre, yg = s5_fwd(proj, 2 * CW, _rhs3(bdr), _rhs3(bdi), cd2, tabs, d_skip)
    forward(["w_ff1"], "gather_forward_ff1", yg)
    wg["w_glu_a"] = weight("w_glu_a", yg)
    wg["w_glu_b"] = weight("w_glu_b", yg)
    ga = mm_nn(yg, wg["w_glu_a"], "glu_a", out_dtype=BF16)
    gb = mm_nn(yg, wg["w_glu_b"], "glu_b", out_dtype=BF16)
    merged = merge_fwd(proj, 3 * CW, y_conv, ga, gb)
    forward(["w_ff2"], "gather_forward_ff2", merged)
    wg["w_out"] = weight("w_out", merged)
    m_out = mm_nn(merged, wg["w_out"], "out_proj")
    h1, z = residual_norm(xs, m_out, gate1, norm2_g, scale2, shift2)
    wg["w_ff1"] = weight("w_ff1", z)
    act = mm_nn(z, wg["w_ff1"], "ff1", out_dtype=BF16, epi=lambda r: jnp.square(jnp.maximum(r, 0.0)))
    wg["w_ff2"] = weight("w_ff2", act)
    ff = mm_nn(act, wg["w_ff2"], "ff2")

    dh2, dff, loss_part, d_final_g, d_gate2 = loss_bwd(h1, ff, gate2, final_g.reshape(1, D), tgt)
    df = mm_nt(dff, wg["w_ff2"], "ff2_dx", out_dtype=BF16,
               epi=lambda r, a: r * (2.0 * jnp.sqrt(a.astype(F32))), extras=(act,))
    t = scatter("w_ff2", mm_tn(act, dff, 1, "ff2_dw", out_dtype=BF16))
    t = scatter("w_ff1", mm_tn(z, df, NDEV, "ff1_dw", out_dtype=BF16, dep=t))
    dz = mm_nt(df, wg["w_ff1"], "ff1_dx", out_dtype=BF16, dep=t)
    dh1, d_scale2, d_shift2, d_norm2_g, dmo, d_gate1 = norm_bwd(dz, h1, dh2, norm2_g, scale2, "norm2_bwd", gate1, m_out)
    t = scatter("w_out", mm_tn(merged, dmo, 1, "out_dw", out_dtype=BF16))
    dmerged = mm_nt(dmo, wg["w_out"], "out_dx", out_dtype=BF16, dep=t)
    dyc, dga, dgb, dproj_g = merge_bwd(dmerged, proj, 3 * CW, y_conv, ga, gb)
    t = scatter("w_conv_out", mm_tn(vs, dyc, 1, "conv_out_dw", out_dtype=BF16))
    t = scatter("w_glu_a", mm_tn(yg, dga, 1, "glu_a_dw", out_dtype=BF16, dep=t))
    t = scatter("w_glu_b", mm_tn(yg, dgb, 1, "glu_b_dw", out_dtype=BF16, dep=t))
    dvs = mm_nt(dyc, wg["w_conv_out"], "conv_out_dx", out_dtype=BF16, dep=t)
    dyg_a = mm_nt(dga, wg["w_glu_a"], "glu_a_dx", out_dtype=BF16, dep=t)
    dyg_b = mm_nt(dgb, wg["w_glu_b"], "glu_b_dx", out_dtype=BF16, dep=t)
    dvc, d_ln_g, d_ln_b = conv_ln_bwd(dvs, vc, ln_g, ln_b)
    dproj_c, d_w_dw, d_b_dw = conv_bwd(dvc, proj, w_dw_full)
    bdt2 = jnp.concatenate([bdr.transpose(0, 2, 1), bdi.transpose(0, 2, 1)], axis=1).astype(BF16)
    cdrt3, cdit3 = _rhs3(cdr.transpose(0, 2, 1)), _rhs3(-cdi.transpose(0, 2, 1))
    dproj_s, d_d_skip, dcdr, dcdi, dbdr, dbdi, dlr8, dli8 = s5_bwd(
        dyg_a, dyg_b, y_pre, proj, 2 * CW, s_re, s_im, bdt2, cdrt3, cdit3, tabs, d_skip)
    done = dproj_s
    for k in ("w_ff2", "w_ff1", "w_out"):
        done = finish_weight(k, done)
    dproj = jnp.concatenate([dproj_c, dproj_s, dproj_g], axis=1)
    t = scatter("w_in", mm_tn(u, dproj, NDEV, "in_dw", out_dtype=BF16, dep=done))
    du = mm_nt(dproj, wg["w_in"], "in_dx", out_dtype=BF16, dep=t)
    grad_x, d_scale1, d_shift1, d_norm1_g = norm_bwd(du, xs, dh1, norm1_g, scale1, "norm1_bwd")

    dmod = jnp.concatenate([d_shift1, d_scale1, d_gate1, d_shift2, d_scale2, d_gate2], axis=1)
    d_c_re = _diag_blocks(dcdr, H, P)
    d_c_im = _diag_blocks(dcdi, H, P)
    d_bbr = _diag_blocks(dbdr, H, P)
    d_bbi = _diag_blocks(dbdi, H, P)
    dlr = jnp.sum(dlr8, axis=0).reshape(G, P)
    dli = jnp.sum(dli8, axis=0).reshape(G, P)
    small_parts = [dmod, d_norm1_g, d_b_dw, d_ln_g, d_ln_b, dlr, dli, d_bbr, d_bbi, d_c_re, d_c_im, d_d_skip,
                   d_norm2_g, d_final_g, d_w_dw]
    pack8 = _flat_pad(small_parts, PACK).reshape(NDEV, -1, 1024)
    parts8, dmod_from = _exchange([pack8, dmod.reshape(NDEV, 1, n_ada)], "scatter_small_grads", False)
    (tot8,) = _exchange([sum_devices(parts8, "sum_small_grads")], "gather_small_sums", True)
    tot = tot8.reshape(-1)
    (g_b_ada, g_norm1_g, g_b_dw, g_ln_g, g_ln_b, t_lr, t_li, t_bbr, t_bbi, g_c_re_t, g_c_im_t, g_d_skip,
     g_norm2_g, g_final_g, g_w_dw_full) = _split(tot, small_parts)
    g_a_re, g_a_im, g_ldt, g_br2, g_bi2 = s5_params_bwd(
        a_re[0], a_im[0], ldt, br2, bi2, expand, t_lr, t_li, t_bbr.reshape(G * H, P), t_bbi.reshape(G * H, P))
    g_brt, g_bit = g_br2.reshape(G, H, P), g_bi2.reshape(G, H, P)
    dmod_cols = dmod_from.reshape(NDEV, n_ada)

    grads = {
        "b_ada": g_b_ada, "norm1_g": g_norm1_g, "b_dw": g_b_dw, "ln_g": g_ln_g, "ln_b": g_ln_b,
        "a_re": g_a_re[None], "a_im": g_a_im[None], "log_dt": g_ldt.reshape(1, G),
        "b_re": g_brt.transpose(0, 2, 1)[None], "b_im": g_bit.transpose(0, 2, 1)[None],
        "c_re": g_c_re_t[None], "c_im": g_c_im_t[None], "d_skip": g_d_skip, "norm2_g": g_norm2_g,
        "final_g": g_final_g.reshape(D),
        "w_dw": lax.dynamic_slice(g_w_dw_full, (0, me * (CW // NDEV)), (CONV_K, CW // NDEV))[None],
    }
    small = [k for k in names if k in grads]
    wf = _flat_pad([W[k] for k in small], PACK).reshape(-1, 1024)
    gf = _flat_pad([grads[k] for k in small], PACK).reshape(-1, 1024)
    mf = _flat_pad([Mo[k] for k in small], PACK).reshape(-1, 1024)
    vf = _flat_pad([Vo[k] for k in small], PACK).reshape(-1, 1024)
    d_s, m_s, v_s = adam_update(wf, gf, mf, vf, "adam_small")
    like = [W[k] for k in small]
    delta = dict(zip(small, _split(d_s.reshape(-1), like)))
    new_m = dict(zip(small, _split(m_s.reshape(-1), like)))
    new_v = dict(zip(small, _split(v_s.reshape(-1), like)))

    g, d, mm, vv = adam_w_ada(c_act, dmod_cols, w_ada[0], m_w_ada[0], v_w_ada[0])
    grads["w_ada"], delta["w_ada"], new_m["w_ada"], new_v["w_ada"] = g[None], d[None], mm[None], vv[None]

    after = d
    for k in ("w_conv_out", "w_glu_a", "w_glu_b", "w_in"):
        after = finish_weight(k, after)
    for k in big:
        g, d, mm, vv = big_out[k]
        grads[k], delta[k], new_m[k], new_v[k] = g[None], d[None], mm[None], vv[None]

    loss = lax.psum(loss_part[0, 0], ("x", "y", "c"))
    return (loss, grad_x[None], *[grads[k] for k in names], *[delta[k] for k in names],
            *[new_m[k] for k in names], *[new_v[k] for k in names])
```

```python
import functools
import math

import jax
import jax.numpy as jnp
from jax import lax
from jax.experimental import pallas as pl
from jax.experimental.pallas import tpu as pltpu

F32 = jnp.float32
BF16 = jnp.bfloat16
NDEV = 8
EPS = 1e-6
ADAM_LR, ADAM_B1, ADAM_B2, ADAM_EPS, ADAM_WD, ADAM_STEP = 0.001, 0.9, 0.999, 1e-08, 0.01, 10
CONV_K = 31
HALO = 32
GROUP = 16
STATE = 64
GB = 8
S5_ROWS = 512
HI = lax.Precision.HIGHEST
MESH = pl.DeviceIdType.MESH
VMEM_LIMIT = 56 * 1024 * 1024
MAX_CONTRACT = 2048
PACK_ROWS = 64
PACK = PACK_ROWS * 1024
ANY = pl.BlockSpec(memory_space=pl.ANY)


def _params(sem=None):
    if sem is None:
        return pltpu.CompilerParams(vmem_limit_bytes=VMEM_LIMIT)
    return pltpu.CompilerParams(dimension_semantics=sem, vmem_limit_bytes=VMEM_LIMIT)


def _sigmoid(v):
    return 1.0 / (1.0 + jnp.exp(-v))


def _me():
    return 4 * lax.axis_index("x") + 2 * lax.axis_index("y") + lax.axis_index("c")


def _peer(k):
    x, y, c = lax.axis_index("x"), lax.axis_index("y"), lax.axis_index("c")
    px = 1 - x if (k >> 2) & 1 else x
    py = 1 - y if (k >> 1) & 1 else y
    pc = 1 - c if k & 1 else c
    return (px, py, pc), 4 * px + 2 * py + pc


def _exchange(arrays, name, gather):
    n = len(arrays)
    out_shape = []
    for a in arrays:
        shp = (NDEV,) + a.shape if gather else a.shape
        out_shape.append(jax.ShapeDtypeStruct(shp, a.dtype))

    def body(*refs):
        ins, outs = refs[:n], refs[n:2 * n]
        send, recv, lsem = refs[2 * n:]
        me = _me()
        local = []
        for a in range(n):
            src = ins[a] if gather else ins[a].at[me]
            cp = pltpu.make_async_copy(src, outs[a].at[me], lsem.at[a])
            cp.start()
            local.append(cp)
        sends = []
        for a in range(n):
            for k in range(1, NDEV):
                dev, pidx = _peer(k)
                src = ins[a] if gather else ins[a].at[pidx]
                cp = pltpu.make_async_remote_copy(
                    src_ref=src, dst_ref=outs[a].at[me], send_sem=send.at[a * (NDEV - 1) + k - 1], recv_sem=recv.at[a * (NDEV - 1) + k - 1],
                    device_id=dev, device_id_type=MESH)
                cp.start()
                sends.append(cp)
        for a in range(n):
            for k in range(1, NDEV):
                dev, pidx = _peer(k)
                src = ins[a] if gather else ins[a].at[pidx]
                pltpu.make_async_remote_copy(
                    src_ref=src, dst_ref=outs[a].at[pidx], send_sem=send.at[a * (NDEV - 1) + k - 1], recv_sem=recv.at[a * (NDEV - 1) + k - 1],
                    device_id=dev, device_id_type=MESH).wait_recv()
        for cp in sends:
            cp.wait_send()
        for cp in local:
            cp.wait()

    return pl.pallas_call(
        body, name=name, out_shape=tuple(out_shape),
        in_specs=[ANY] * n, out_specs=tuple([ANY] * n),
        scratch_shapes=[pltpu.SemaphoreType.DMA((n * (NDEV - 1),)), pltpu.SemaphoreType.DMA((n * (NDEV - 1),)),
                        pltpu.SemaphoreType.DMA((n,))],
    )(*arrays)


HBM = pl.BlockSpec(memory_space=pltpu.HBM)
SEM = pl.BlockSpec(memory_space=pltpu.SEMAPHORE)
EFFECT = pltpu.SideEffectType.DATAFLOW_SIDE_EFFECTING
NPEER = NDEV - 1


def _landing(block_of_me, shape, dtype):
    land = lax.empty((NDEV,) + tuple(shape), dtype)
    start = (_me(),) + (0,) * len(shape)
    return pltpu.with_memory_space_constraint(lax.dynamic_update_slice(land, block_of_me[None], start), pltpu.HBM)


def exchange_start(arrays, name, gather, after=None):
    n = len(arrays)
    me = _me()
    deps = () if after is None else (after,)
    lands = []
    for a in arrays:
        if gather:
            lands.append(_landing(a, a.shape, a.dtype))
        else:
            mine = lax.dynamic_slice(a, (me,) + (0,) * (a.ndim - 1), (1,) + a.shape[1:])[0]
            lands.append(_landing(mine, a.shape[1:], a.dtype))
    srcs = [pltpu.with_memory_space_constraint(a, pltpu.HBM) for a in arrays]

    def body(*refs):
        ins, lnd = refs[:n], refs[n:2 * n]
        outs = refs[2 * n + len(deps):]
        sends, recvs, token = outs[:n], outs[n:2 * n], outs[-1]
        my = _me()
        for a in range(n):
            for k in range(1, NDEV):
                dev, pidx = _peer(k)
                src = ins[a] if gather else ins[a].at[pidx]
                pltpu.make_async_remote_copy(
                    src_ref=src, dst_ref=lnd[a].at[my], send_sem=sends[a].at[k - 1], recv_sem=recvs[a].at[k - 1],
                    device_id=dev, device_id_type=MESH).start()
        token[...] = jnp.zeros_like(token)

    out_shape = ([pltpu.SemaphoreType.DMA((NPEER,))] * (2 * n)
                 + [pltpu.HBM(a.shape, a.dtype) for a in srcs] + [pltpu.HBM(l.shape, l.dtype) for l in lands]
                 + [jax.ShapeDtypeStruct((8, 128), F32)])
    res = pl.pallas_call(
        body, name=name, out_shape=tuple(out_shape),
        in_specs=[HBM] * (2 * n) + [ANY] * len(deps),
        out_specs=tuple([SEM] * (2 * n) + [HBM] * (2 * n) + [pl.BlockSpec(memory_space=pltpu.VMEM)]),
        input_output_aliases={i: 2 * n + i for i in range(2 * n)},
        compiler_params=pltpu.CompilerParams(has_side_effects=EFFECT),
    )(*srcs, *lands, *deps)
    handles = [(res[a], res[n + a], res[2 * n + a], res[3 * n + a]) for a in range(n)]
    return handles, res[-1]


def exchange_wait(handle, after, name, gather):
    send_sem, recv_sem, src, land = handle

    def body(src_ref, land_ref, s_ref, r_ref, after_ref, src_out, land_out):
        for k in range(1, NDEV):
            dev, pidx = _peer(k)
            s = src_ref if gather else src_ref.at[pidx]
            cp = pltpu.make_async_remote_copy(
                src_ref=s, dst_ref=land_ref.at[pidx], send_sem=s_ref.at[k - 1], recv_sem=r_ref.at[k - 1],
                device_id=dev, device_id_type=MESH)
            cp.wait_send()
            cp.wait_recv()

    return pl.pallas_call(
        body, name=name, out_shape=(pltpu.HBM(src.shape, src.dtype), pltpu.HBM(land.shape, land.dtype)),
        in_specs=(HBM, HBM, SEM, SEM, ANY), out_specs=(HBM, HBM), input_output_aliases={0: 0, 1: 1},
        compiler_params=pltpu.CompilerParams(has_side_effects=EFFECT),
    )(src, land, send_sem, recv_sem, after)[1]


ICI_PEERS = (2, 4, 6)
SIBLING = 1


def gather2_start(blocks, name, after):
    m = len(blocks)
    lands = [_landing(b, b.shape, b.dtype) for b in blocks]
    srcs = [pltpu.with_memory_space_constraint(b, pltpu.HBM) for b in blocks]
    n = len(ICI_PEERS)

    def body(*refs):
        src, lnd = refs[:m], refs[m:2 * m]
        outs = refs[2 * m + 1:]
        send, recv_sib, recv_ici = outs[:m], outs[m:2 * m], outs[2 * m:3 * m]
        my = _me()
        for a in range(m):
            dev, _ = _peer(SIBLING)
            pltpu.make_async_remote_copy(src_ref=src[a], dst_ref=lnd[a].at[my], send_sem=send[a].at[0],
                                         recv_sem=recv_sib[a].at[0], device_id=dev, device_id_type=MESH).start()
            for j, k in enumerate(ICI_PEERS):
                dev, _ = _peer(k)
                pltpu.make_async_remote_copy(src_ref=src[a], dst_ref=lnd[a].at[my], send_sem=send[a].at[1 + j],
                                             recv_sem=recv_ici[a].at[j], device_id=dev, device_id_type=MESH).start()

    out_shape = ([pltpu.SemaphoreType.DMA((1 + n,))] * m + [pltpu.SemaphoreType.DMA((1,))] * m
                 + [pltpu.SemaphoreType.DMA((n,))] * m
                 + [pltpu.HBM(s.shape, s.dtype) for s in srcs] + [pltpu.HBM(l.shape, l.dtype) for l in lands])
    res = pl.pallas_call(
        body, name=name, out_shape=tuple(out_shape),
        in_specs=[HBM] * (2 * m) + [ANY], out_specs=tuple([SEM] * (3 * m) + [HBM] * (2 * m)),
        input_output_aliases={i: 3 * m + i for i in range(2 * m)},
        compiler_params=pltpu.CompilerParams(has_side_effects=EFFECT),
    )(*srcs, *lands, after)
    return [tuple(res[g * m + a] for g in range(5)) for a in range(m)]


def gather2_forward(handles, name, after):
    m = len(handles)
    n = len(ICI_PEERS)
    srcs, lands = [h[3] for h in handles], [h[4] for h in handles]
    deps = tuple(after) if isinstance(after, (tuple, list)) else (after,)

    def body(*refs):
        src, lnd, recv_ici = refs[:m], refs[m:2 * m], refs[2 * m:3 * m]
        outs = refs[3 * m + len(deps):]
        fsend, frecv = outs[:m], outs[m:2 * m]
        sib, _ = _peer(SIBLING)
        for a in range(m):
            for j, k in enumerate(ICI_PEERS):
                dev, pidx = _peer(k)
                pltpu.make_async_remote_copy(
                    src_ref=src[a], dst_ref=lnd[a].at[pidx], send_sem=fsend[a].at[j], recv_sem=recv_ici[a].at[j],
                    device_id=dev, device_id_type=MESH).wait_recv()
                pltpu.make_async_remote_copy(
                    src_ref=lnd[a].at[pidx], dst_ref=lnd[a].at[pidx], send_sem=fsend[a].at[j], recv_sem=frecv[a].at[j],
                    device_id=sib, device_id_type=MESH).start()

    out_shape = ([pltpu.SemaphoreType.DMA((n,))] * (2 * m)
                 + [pltpu.HBM(s.shape, s.dtype) for s in srcs] + [pltpu.HBM(l.shape, l.dtype) for l in lands])
    res = pl.pallas_call(
        body, name=name, out_shape=tuple(out_shape),
        in_specs=[HBM] * (2 * m) + [SEM] * m + [ANY] * len(deps),
        out_specs=tuple([SEM] * (2 * m) + [HBM] * (2 * m)),
        input_output_aliases={i: 2 * m + i for i in range(2 * m)},
        compiler_params=pltpu.CompilerParams(has_side_effects=EFFECT),
    )(*srcs, *lands, *[h[2] for h in handles], *deps)
    return [(handles[a][0], handles[a][1], res[a], res[m + a], res[2 * m + a], res[3 * m + a]) for a in range(m)]


def gather2_wait(handle, name, after):
    send, recv_sib, fsend, frecv, src, land = handle

    def body(src_ref, land_ref, send_ref, recv_sib_ref, fsend_ref, frecv_ref, after_ref, src_out, land_out):
        sib, sib_idx = _peer(SIBLING)
        own = pltpu.make_async_remote_copy(src_ref=src_ref, dst_ref=land_ref.at[sib_idx], send_sem=send_ref.at[0],
                                           recv_sem=recv_sib_ref.at[0], device_id=sib, device_id_type=MESH)
        own.wait_send()
        own.wait_recv()
        for j, k in enumerate(ICI_PEERS):
            dev, pidx = _peer(k)
            pltpu.make_async_remote_copy(src_ref=src_ref, dst_ref=land_ref.at[pidx], send_sem=send_ref.at[1 + j],
                                         recv_sem=frecv_ref.at[j], device_id=dev, device_id_type=MESH).wait_send()
            _, fidx = _peer(k ^ SIBLING)
            fwd = pltpu.make_async_remote_copy(src_ref=land_ref.at[pidx], dst_ref=land_ref.at[fidx],
                                               send_sem=fsend_ref.at[j], recv_sem=frecv_ref.at[j],
                                               device_id=sib, device_id_type=MESH)
            fwd.wait_send()
            fwd.wait_recv()

    return pl.pallas_call(
        body, name=name, out_shape=(pltpu.HBM(src.shape, src.dtype), pltpu.HBM(land.shape, land.dtype)),
        in_specs=(HBM, HBM, SEM, SEM, SEM, SEM, ANY), out_specs=(HBM, HBM), input_output_aliases={0: 0, 1: 1},
        compiler_params=pltpu.CompilerParams(has_side_effects=EFFECT),
    )(src, land, send, recv_sib, fsend, frecv, after)[1]


def _acc_steps(p, acc, k, nk, finish):
    if nk == 1:
        finish(p)
        return

    @pl.when(k == 0)
    def _():
        acc[...] = p

    @pl.when(k > 0)
    def _():
        acc[...] += p

    @pl.when(k == nk - 1)
    def _():
        finish(acc[...])


def mm_nn(a, w3, name, out_dtype=F32, epi=None, extras=()):
    M, K = a.shape
    J, _, n = w3.shape
    tm, tn, tk = min(1024, M), min(1024, n), min(2048, K)
    q, nk, ne = n // tn, K // tk, len(extras)

    def body(*refs):
        a_ref, w_ref = refs[:2]
        ex, o_ref, acc = refs[2:2 + ne], refs[2 + ne], refs[-1]
        p = jnp.dot(a_ref[...], w_ref[...], preferred_element_type=F32)

        def finish(r):
            if epi is not None:
                r = epi(r, *[e[...] for e in ex])
            o_ref[...] = r.astype(out_dtype)

        _acc_steps(p, acc, pl.program_id(2), nk, finish)

    return pl.pallas_call(
        body, name=name, grid=(M // tm, J * q, nk),
        in_specs=[pl.BlockSpec((tm, tk), lambda i, j, k: (i, k)),
                  pl.BlockSpec((None, tk, tn), lambda i, j, k: (j // q, k, j % q))]
        + [pl.BlockSpec((tm, tn), lambda i, j, k: (i, j))] * ne,
        out_specs=pl.BlockSpec((tm, tn), lambda i, j, k: (i, j)),
        out_shape=jax.ShapeDtypeStruct((M, J * n), out_dtype),
        scratch_shapes=[pltpu.VMEM((tm, tn), F32)],
        compiler_params=_params(("parallel", "parallel", "arbitrary")),
    )(a, w3, *extras)


def mm_nt(dy, w3, name, out_dtype=F32, epi=None, extras=(), dep=None):
    M, _ = dy.shape
    J, K, n = w3.shape
    tm, tn, tkk = min(1024, M), min(MAX_CONTRACT, n), min(1024, K)
    q, ne = n // tn, len(extras)
    s = 1
    while q == 1 and J % (2 * s) == 0 and 2 * s * tn <= MAX_CONTRACT:
        s *= 2
    nk = (J // s) * q
    deps = () if dep is None else (dep,)

    def body(*refs):
        d_ref, w_ref = refs[:2]
        ex, o_ref, acc = refs[2:2 + ne], refs[-2], refs[-1]
        nt = (((1,), (1,)), ((), ()))
        p = lax.dot_general(d_ref[:, 0:tn], w_ref[0], nt, preferred_element_type=F32)
        for j in range(1, s):
            p = p + lax.dot_general(d_ref[:, j * tn:(j + 1) * tn], w_ref[j], nt, preferred_element_type=F32)

        def finish(r):
            if epi is not None:
                r = epi(r, *[e[...] for e in ex])
            o_ref[...] = r.astype(out_dtype)

        _acc_steps(p, acc, pl.program_id(2), nk, finish)

    return pl.pallas_call(
        body, name=name, grid=(M // tm, K // tkk, nk),
        in_specs=[pl.BlockSpec((tm, s * tn), lambda i, kk, c: (i, c)),
                  pl.BlockSpec((s, tkk, tn), lambda i, kk, c: (c // q, kk, c % q))]
        + [pl.BlockSpec((tm, tkk), lambda i, kk, c: (i, kk))] * ne + [ANY] * len(deps),
        out_specs=pl.BlockSpec((tm, tkk), lambda i, kk, c: (i, kk)),
        out_shape=jax.ShapeDtypeStruct((M, K), out_dtype),
        scratch_shapes=[pltpu.VMEM((tm, tkk), F32)],
        compiler_params=_params(("parallel", "parallel", "arbitrary")),
    )(dy, w3, *extras, *deps)


def mm_tn(a, dy, J, name, out_dtype=F32, dep=None):
    M, K = a.shape
    n = dy.shape[1] // J
    tm, tn, tkk = min(MAX_CONTRACT, M), min(1024, n), min(1024, K)
    q, nk = n // tn, M // tm
    deps = () if dep is None else (dep,)

    def body(a_ref, d_ref, *rest):
        o_ref, acc = rest[-2:]
        p = lax.dot_general(a_ref[...], d_ref[...], (((0,), (0,)), ((), ())), preferred_element_type=F32)

        def finish(r):
            o_ref[...] = r.astype(out_dtype)

        _acc_steps(p, acc, pl.program_id(2), nk, finish)

    return pl.pallas_call(
        body, name=name, grid=(K // tkk, J * q, nk),
        in_specs=[pl.BlockSpec((tm, tkk), lambda kk, c, m: (m, kk)),
                  pl.BlockSpec((tm, tn), lambda kk, c, m: (m, c))] + [ANY] * len(deps),
        out_specs=pl.BlockSpec((None, tkk, tn), lambda kk, c, m: (c // q, kk, c % q)),
        out_shape=jax.ShapeDtypeStruct((J, K, n), out_dtype),
        scratch_shapes=[pltpu.VMEM((tkk, tn), F32)],
        compiler_params=_params(("parallel", "parallel", "arbitrary")),
    )(a, dy, *deps)


def _tm(L):
    return min(256, L)


def _row(w, cb=0, tm=None):
    return pl.BlockSpec((tm, w), lambda i: (i, cb))


def _vec(w, cb=0):
    return pl.BlockSpec((1, w), lambda i: (0, cb))


def _accum(ref, val, i):
    @pl.when(i == 0)
    def _():
        ref[...] = val

    @pl.when(i > 0)
    def _():
        ref[...] += val


def _colsum(v):
    return jnp.sum(v, axis=0, keepdims=True)


def _rms(v):
    return lax.rsqrt(jnp.mean(v * v, axis=-1, keepdims=True) + EPS)


def adaln_mod(c_all, w_ada, b_cols):
    B, D = c_all.shape
    n = w_ada.shape[1]
    tn = 512

    def body(c_ref, w_ref, b_ref, o_ref, ca_ref):
        cv = c_ref[...]
        ca = cv * _sigmoid(cv)
        ca_ref[...] = ca
        o_ref[...] = jnp.dot(ca.astype(BF16), w_ref[...].astype(BF16), preferred_element_type=F32) + b_ref[...]

    return pl.pallas_call(
        body, name="adaln_mod", grid=(n // tn,),
        in_specs=[pl.BlockSpec((B, D), lambda j: (0, 0)), pl.BlockSpec((D, tn), lambda j: (0, j)),
                  pl.BlockSpec((1, tn), lambda j: (0, j))],
        out_specs=(pl.BlockSpec((B, tn), lambda j: (0, j)), pl.BlockSpec((B, D), lambda j: (0, 0))),
        out_shape=(jax.ShapeDtypeStruct((B, n), F32), jax.ShapeDtypeStruct((B, D), F32)),
        compiler_params=_params(("arbitrary",)),
    )(c_all, w_ada, b_cols)


def prenorm(x, g, scale, shift, name):
    L, D = x.shape
    tm = _tm(L)

    def body(x_ref, g_ref, sc_ref, sh_ref, u_ref):
        xv = x_ref[...]
        u_ref[...] = (xv * _rms(xv) * g_ref[...] * (1.0 + sc_ref[...]) + sh_ref[...]).astype(BF16)

    return pl.pallas_call(
        body, name=name, grid=(L // tm,),
        in_specs=[_row(D, tm=tm), _vec(D), _vec(D), _vec(D)],
        out_specs=_row(D, tm=tm), out_shape=jax.ShapeDtypeStruct((L, D), BF16),
        compiler_params=_params(("parallel",)),
    )(x, g, scale, shift)


def _shifted_copies(buf, shifted, tm):
    n = HALO + tm - 8
    for r in range(1, 8):
        shifted[r - 1] = buf[pl.ds(r, n), :]


def _window(buf, shifted, off, tm):
    r, base = off % 8, off - off % 8
    if r == 0:
        return buf[pl.ds(base, tm), :]
    return shifted[r - 1, pl.ds(base, tm), :]


def conv_fwd(proj, w_dw, b_dw, ln_g, ln_b):
    L = proj.shape[0]
    C = w_dw.shape[1]
    tm = _tm(L)
    hb = tm // HALO

    def body(a_ref, g_ref, ah_ref, gh_ref, w_ref, b_ref, lg_ref, lb_ref, vs_ref, vc_ref, buf, shifted):
        i = pl.program_id(0)
        halo = ah_ref[...] * _sigmoid(gh_ref[...])
        buf[0:HALO, :] = halo * jnp.where(i > 0, 1.0, 0.0)
        buf[HALO:HALO + tm, :] = a_ref[...] * _sigmoid(g_ref[...])
        _shifted_copies(buf, shifted, tm)
        acc = jnp.zeros((tm, C), F32) + b_ref[...]
        for k in range(CONV_K):
            acc = acc + w_ref[k:k + 1, :] * _window(buf, shifted, HALO - (CONV_K - 1) + k, tm)
        vc_ref[...] = acc
        mu = jnp.mean(acc, axis=-1, keepdims=True)
        d = acc - mu
        var = jnp.mean(d * d, axis=-1, keepdims=True)
        ln = d * lax.rsqrt(var + EPS) * lg_ref[...] + lb_ref[...]
        vs_ref[...] = (ln * _sigmoid(ln)).astype(BF16)

    prev = lambda cb: pl.BlockSpec((HALO, C), lambda i: (jnp.maximum(i * hb - 1, 0), cb))
    return pl.pallas_call(
        body, name="conv_fwd", grid=(L // tm,),
        in_specs=[_row(C, 0, tm), _row(C, 1, tm), prev(0), prev(1),
                  pl.BlockSpec((HALO, C), lambda i: (0, 0)), _vec(C), _vec(C), _vec(C)],
        out_specs=(_row(C, tm=tm), _row(C, tm=tm)),
        out_shape=(jax.ShapeDtypeStruct((L, C), BF16), jax.ShapeDtypeStruct((L, C), F32)),
        scratch_shapes=[pltpu.VMEM((HALO + tm, C), F32), pltpu.VMEM((7, HALO + tm - 8, C), F32)],
        compiler_params=_params(("parallel",)),
    )(proj, proj, proj, proj, w_dw, b_dw, ln_g, ln_b)


def _gelu(v):
    return 0.5 * v * (1.0 + jnp.tanh(math.sqrt(2.0 / math.pi) * (v + 0.044715 * v * v * v)))


def _gelu_grad(v):
    k = math.sqrt(2.0 / math.pi)
    t = jnp.tanh(k * (v + 0.044715 * v * v * v))
    return 0.5 * (1.0 + t) + 0.5 * v * (1.0 - t * t) * k * (1.0 + 3.0 * 0.044715 * v * v)


def s5_param_fn(ar, ai, ldt, br, bi, expand):
    dt = jnp.exp(ldt)
    er = jnp.exp(ar * dt)
    th = ai * dt
    lbr, lbi = er * jnp.cos(th), er * jnp.sin(th)
    nr, ni = lbr - 1.0, lbi
    den = ar * ar + ai * ai
    qr, qi = (nr * ar + ni * ai) / den, (ni * ar - nr * ai) / den
    qre = jnp.dot(expand, qr, precision=HI, preferred_element_type=F32)
    qie = jnp.dot(expand, qi, precision=HI, preferred_element_type=F32)
    return lbr, lbi, qre * br - qie * bi, qre * bi + qie * br


def s5_params(ar, ai, ldt, br2, bi2, expand):
    def body(ar_ref, ai_ref, ld_ref, br_ref, bi_ref, e_ref, o1, o2, o3, o4):
        r = s5_param_fn(ar_ref[...], ai_ref[...], ld_ref[...], br_ref[...], bi_ref[...], e_ref[...])
        o1[...], o2[...], o3[...], o4[...] = r

    s2, s3 = jax.ShapeDtypeStruct(ar.shape, F32), jax.ShapeDtypeStruct(br2.shape, F32)
    return pl.pallas_call(body, name="s5_params", out_shape=(s2, s2, s3, s3), compiler_params=_params())(
        ar, ai, ldt, br2, bi2, expand)


def s5_params_bwd(ar, ai, ldt, br2, bi2, expand, dlr, dli, dbr, dbi):
    def body(ar_ref, ai_ref, ld_ref, br_ref, bi_ref, e_ref, c1, c2, c3, c4, o1, o2, o3, o4, o5):
        e = e_ref[...]
        fn = lambda a, b, c, d, f: s5_param_fn(a, b, c, d, f, e)
        _, vjp = jax.vjp(fn, ar_ref[...], ai_ref[...], ld_ref[...], br_ref[...], bi_ref[...])
        r = vjp((c1[...], c2[...], c3[...], c4[...]))
        o1[...], o2[...], o3[...], o4[...], o5[...] = r

    shapes = tuple(jax.ShapeDtypeStruct(v.shape, F32) for v in (ar, ai, ldt, br2, bi2))
    return pl.pallas_call(body, name="s5_params_bwd", out_shape=shapes, compiler_params=_params())(
        ar, ai, ldt, br2, bi2, expand, dlr, dli, dbr, dbi)


def s5_tables(lr, li):
    C = lr.shape[1]

    def body(lr_ref, li_ref, o_ref):
        row = lax.broadcasted_iota(jnp.int32, (8, C), 0)
        for rev in (0, 1):
            pr = jnp.broadcast_to(lr_ref[...], (8, C))
            pi = jnp.broadcast_to(-li_ref[...] if rev else li_ref[...], (8, C))
            br, bi = pr, pi
            pows = [(pr, pi)]
            for _ in range(7):
                pr, pi = pr * br - pi * bi, pr * bi + pi * br
                pows.append((pr, pi))
            base = 8 * rev
            for s, d in enumerate((1, 2, 4)):
                keep = (row + d <= 7) if rev else (row >= d)
                o_ref[base + 2 * s] = jnp.where(keep, pows[d - 1][0], 0.0)
                o_ref[base + 2 * s + 1] = jnp.where(keep, pows[d - 1][1], 0.0)
            cr, ci = jnp.zeros((8, C), F32), jnp.zeros((8, C), F32)
            for j in range(8):
                e = (8 - j) if rev else (j + 1)
                cr = jnp.where(row == j, pows[e - 1][0], cr)
                ci = jnp.where(row == j, pows[e - 1][1], ci)
            o_ref[base + 6] = cr
            o_ref[base + 7] = ci

    return pl.pallas_call(body, name="s5_tables", out_shape=jax.ShapeDtypeStruct((16, 8, C), F32),
                          compiler_params=_params())(lr, li)


def _scan_tile(xr, xi, tabs, cr, ci, rev):
    for s, d in enumerate((1, 2, 4)):
        tr, ti = tabs[2 * s], tabs[2 * s + 1]
        sh = (8 - d) if rev else d
        sr, si = pltpu.roll(xr, sh, 0), pltpu.roll(xi, sh, 0)
        xr, xi = xr + tr * sr - ti * si, xi + tr * si + ti * sr
    tr, ti = tabs[6], tabs[7]
    xr, xi = xr + tr * cr - ti * ci, xi + tr * ci + ti * cr
    return xr, xi


def _hi_lo(a):
    hi = a.astype(BF16)
    return hi, (a - hi.astype(F32)).astype(BF16)


def _lhs3(a):
    hi, lo = _hi_lo(a)
    return jnp.concatenate([hi, lo, hi], axis=1)


def _rhs3(m):
    hi, lo = _hi_lo(m)
    return jnp.concatenate([hi, hi, lo], axis=-2)


def s5_fwd(proj, col0, bdr3, bdi3, cd2, tabs, d_skip):
    L = proj.shape[0]
    nb, cw3, sw = bdr3.shape
    cw = cw3 // 3
    tl = min(S5_ROWS, L)
    cb0 = col0 // cw

    def body(u_ref, bdr_ref, bdi_ref, cd_ref, t_ref, dk_ref, sr_ref, si_ref, yp_ref, yg_ref, car):
        l = pl.program_id(1)

        @pl.when(l == 0)
        def _():
            car[...] = jnp.zeros_like(car)

        u = u_ref[...]
        u3 = _lhs3(u)
        sr_ref[...] = jnp.dot(u3, bdr_ref[...], preferred_element_type=F32)
        si_ref[...] = jnp.dot(u3, bdi_ref[...], preferred_element_type=F32)

        def tile(i, c):
            tabs = [t_ref[j] for j in range(8)]
            r0 = pl.multiple_of(i * 8, 8)
            xr, xi = _scan_tile(sr_ref[pl.ds(r0, 8), :], si_ref[pl.ds(r0, 8), :], tabs, c[0], c[1], False)
            sr_ref[pl.ds(r0, 8), :] = xr
            si_ref[pl.ds(r0, 8), :] = xi
            return xr[7:8, :], xi[7:8, :]

        c = lax.fori_loop(0, tl // 8, tile, (car[0:1, :], car[1:2, :]))
        car[0:1, :] = c[0]
        car[1:2, :] = c[1]
        s2 = jnp.concatenate([sr_ref[...].astype(BF16), si_ref[...].astype(BF16)], axis=1)
        y = jnp.dot(s2, cd_ref[...], preferred_element_type=F32) + dk_ref[...] * u
        yp_ref[...] = y
        yg_ref[...] = _gelu(y).astype(BF16)

    blk = lambda r, c: pl.BlockSpec((None, r, c), lambda b, l: (b, 0, 0))
    return pl.pallas_call(
        body, name="s5_fwd", grid=(nb, L // tl),
        in_specs=[pl.BlockSpec((tl, cw), lambda b, l: (l, cb0 + b)), blk(cw3, sw), blk(cw3, sw), blk(2 * sw, cw),
                  pl.BlockSpec((8, 8, sw), lambda b, l: (0, 0, b)), pl.BlockSpec((1, cw), lambda b, l: (0, b))],
        out_specs=(pl.BlockSpec((tl, sw), lambda b, l: (l, b)), pl.BlockSpec((tl, sw), lambda b, l: (l, b)),
                   pl.BlockSpec((tl, cw), lambda b, l: (l, b)), pl.BlockSpec((tl, cw), lambda b, l: (l, b))),
        out_shape=(jax.ShapeDtypeStruct((L, nb * sw), F32), jax.ShapeDtypeStruct((L, nb * sw), F32),
                   jax.ShapeDtypeStruct((L, nb * cw), F32), jax.ShapeDtypeStruct((L, nb * cw), BF16)),
        scratch_shapes=[pltpu.VMEM((8, sw), F32)],
        compiler_params=_params(("parallel", "arbitrary")),
    )(proj, bdr3, bdi3, cd2, tabs, d_skip)


def s5_bwd(dyg_a, dyg_b, yp, proj, col0, s_re, s_im, bdt2, cdrt3, cdit3, tabs, d_skip):
    L = proj.shape[0]
    nb, sw2, cw = bdt2.shape
    sw = sw2 // 2
    tl = min(S5_ROWS, L)
    nl = L // tl
    cb0 = col0 // cw
    tb = tl // 8

    def body(da_ref, db_ref, yp_ref, u_ref, sr_ref, si_ref, hr_ref, hi_ref, bdt_ref, cdrt_ref, cdit_ref,
             t_ref, dk_ref, du_ref, ddk_ref, dcr_ref, dci_ref, dbr_ref, dbi_ref, dlr_ref, dli_ref,
             gr, gi, pr, pi, car):
        l = pl.program_id(1)
        first = l == nl - 1

        @pl.when(l == 0)
        def _():
            car[...] = jnp.zeros_like(car)

        u = u_ref[...]
        dy = (da_ref[...].astype(F32) + db_ref[...].astype(F32)) * _gelu_grad(yp_ref[...])
        dy3 = _lhs3(dy)
        gr[...] = jnp.dot(dy3, cdrt_ref[...], preferred_element_type=F32)
        gi[...] = jnp.dot(dy3, cdit_ref[...], preferred_element_type=F32)
        inner = jnp.where(first, 0.0, 1.0)
        pr[0:8, :] = hr_ref[...] * inner
        pi[0:8, :] = hi_ref[...] * inner
        pr[8:8 + tl, :] = sr_ref[...]
        pi[8:8 + tl, :] = si_ref[...]
        row = lax.broadcasted_iota(jnp.int32, (8, sw), 0)

        def tile(j, c):
            tabs = [t_ref[8 + k] for k in range(8)]
            r0 = pl.multiple_of((tb - 1 - j) * 8, 8)
            xr, xi = _scan_tile(gr[pl.ds(r0, 8), :], gi[pl.ds(r0, 8), :], tabs, c[0], c[1], True)
            gr[pl.ds(r0, 8), :] = xr
            gi[pl.ds(r0, 8), :] = xi
            qr = jnp.where(row == 0, pltpu.roll(pr[pl.ds(r0, 8), :], 1, 0), pltpu.roll(pr[pl.ds(r0 + 8, 8), :], 1, 0))
            qi = jnp.where(row == 0, pltpu.roll(pi[pl.ds(r0, 8), :], 1, 0), pltpu.roll(pi[pl.ds(r0 + 8, 8), :], 1, 0))
            return xr[0:1, :], xi[0:1, :], c[2] + xr * qr + xi * qi, c[3] + xi * qr - xr * qi

        z = jnp.zeros((8, sw), F32)
        c = lax.fori_loop(0, tb, tile, (car[0:1, :], car[1:2, :], z, z))
        car[0:1, :] = c[0]
        car[1:2, :] = c[1]
        g_re, g_im = gr[...].astype(BF16), gi[...].astype(BF16)
        g2 = jnp.concatenate([g_re, g_im], axis=1)
        du_ref[...] = (dy * dk_ref[...] + jnp.dot(g2, bdt_ref[...], preferred_element_type=F32)).astype(BF16)
        tn = (((0,), (0,)), ((), ()))
        dyb, ub = dy.astype(BF16), u.astype(BF16)
        _accum(ddk_ref, _colsum(dy * u), l)
        _accum(dcr_ref, lax.dot_general(dyb, sr_ref[...].astype(BF16), tn, preferred_element_type=F32), l)
        _accum(dci_ref, -lax.dot_general(dyb, si_ref[...].astype(BF16), tn, preferred_element_type=F32), l)
        _accum(dbr_ref, lax.dot_general(ub, g_re, tn, preferred_element_type=F32), l)
        _accum(dbi_ref, lax.dot_general(ub, g_im, tn, preferred_element_type=F32), l)
        _accum(dlr_ref, c[2], l)
        _accum(dli_ref, c[3], l)

    rl = lambda l: nl - 1 - l
    cblk = lambda w, off=0: pl.BlockSpec((tl, w), lambda b, l: (rl(l), off + b))
    halo = pl.BlockSpec((8, sw), lambda b, l: (jnp.maximum(rl(l) * tb - 1, 0), b))
    mat = lambda r, c: pl.BlockSpec((None, r, c), lambda b, l: (b, 0, 0))
    return pl.pallas_call(
        body, name="s5_bwd", grid=(nb, nl),
        in_specs=[cblk(cw), cblk(cw), cblk(cw), cblk(cw, cb0), cblk(sw), cblk(sw), halo, halo,
                  mat(2 * sw, cw), mat(3 * cw, sw), mat(3 * cw, sw),
                  pl.BlockSpec((16, 8, sw), lambda b, l: (0, 0, b)), pl.BlockSpec((1, cw), lambda b, l: (0, b))],
        out_specs=(cblk(cw), pl.BlockSpec((1, cw), lambda b, l: (0, b)), mat(cw, sw), mat(cw, sw), mat(cw, sw), mat(cw, sw),
                   pl.BlockSpec((8, sw), lambda b, l: (0, b)), pl.BlockSpec((8, sw), lambda b, l: (0, b))),
        out_shape=(jax.ShapeDtypeStruct((L, nb * cw), BF16), jax.ShapeDtypeStruct((1, nb * cw), F32),
                   jax.ShapeDtypeStruct((nb, cw, sw), F32), jax.ShapeDtypeStruct((nb, cw, sw), F32),
                   jax.ShapeDtypeStruct((nb, cw, sw), F32), jax.ShapeDtypeStruct((nb, cw, sw), F32),
                   jax.ShapeDtypeStruct((8, nb * sw), F32), jax.ShapeDtypeStruct((8, nb * sw), F32)),
        scratch_shapes=[pltpu.VMEM((tl, sw), F32), pltpu.VMEM((tl, sw), F32),
                        pltpu.VMEM((tl + 8, sw), F32), pltpu.VMEM((tl + 8, sw), F32), pltpu.VMEM((8, sw), F32)],
        compiler_params=_params(("parallel", "arbitrary")),
    )(dyg_a, dyg_b, yp, proj, s_re, s_im, s_re, s_im, bdt2, cdrt3, cdit3, tabs, d_skip)


def merge_fwd(proj, col_gc, y_conv, ga, gb):
    L, D = y_conv.shape
    tm = _tm(L)
    h = D // 2
    c0 = col_gc // h

    def body(p0, p1, p2, p3, yc_ref, ga_ref, gb_ref, o_ref):
        gc, gs = (p0, p1), (p2, p3)
        for s in range(2):
            cols = slice(s * h, (s + 1) * h)
            y_ssm = ga_ref[:, cols].astype(F32) * _sigmoid(gb_ref[:, cols].astype(F32))
            o_ref[:, cols] = (_sigmoid(gc[s][...]) * yc_ref[:, cols].astype(F32)
                              + _sigmoid(gs[s][...]) * y_ssm).astype(BF16)

    return pl.pallas_call(
        body, name="merge_fwd", grid=(L // tm,),
        in_specs=[_row(h, c0 + s, tm) for s in range(4)] + [_row(D, tm=tm)] * 3,
        out_specs=_row(D, tm=tm), out_shape=jax.ShapeDtypeStruct((L, D), BF16),
        compiler_params=_params(("parallel",)),
    )(proj, proj, proj, proj, y_conv, ga, gb)


def residual_norm(x, m_out, gate, g, scale, shift):
    L, D = x.shape
    tm = _tm(L)

    def body(x_ref, m_ref, gt_ref, g_ref, sc_ref, sh_ref, h_ref, z_ref):
        h = x_ref[...] + gt_ref[...] * m_ref[...]
        h_ref[...] = h
        z_ref[...] = (h * _rms(h) * g_ref[...] * (1.0 + sc_ref[...]) + sh_ref[...]).astype(BF16)

    return pl.pallas_call(
        body, name="residual_norm", grid=(L // tm,),
        in_specs=[_row(D, tm=tm), _row(D, tm=tm), _vec(D), _vec(D), _vec(D), _vec(D)],
        out_specs=(_row(D, tm=tm), _row(D, tm=tm)),
        out_shape=(jax.ShapeDtypeStruct((L, D), F32), jax.ShapeDtypeStruct((L, D), BF16)),
        compiler_params=_params(("parallel",)),
    )(x, m_out, gate, g, scale, shift)


def loss_bwd(h1, ff, gate2, final_g, target):
    L, D = h1.shape
    tm = _tm(L)

    def body(h_ref, f_ref, gt_ref, g_ref, t_ref, dh_ref, dff_ref, loss_ref, dg_ref, dgt_ref):
        i = pl.program_id(0)
        ffv = f_ref[...]
        h2 = h_ref[...] + gt_ref[...] * ffv
        r = _rms(h2)
        n = h2 * r
        err = n * g_ref[...] - t_ref[...]
        per_tok = jnp.mean(err * err, axis=-1, keepdims=True)
        _accum(loss_ref, 0.5 * jnp.sum(per_tok, axis=0, keepdims=True), i)
        dy = err * (1.0 / D)
        _accum(dg_ref, _colsum(dy * n), i)
        dn = dy * g_ref[...]
        dh2 = r * (dn - n * jnp.mean(dn * n, axis=-1, keepdims=True))
        dh_ref[...] = dh2
        dff_ref[...] = (gt_ref[...] * dh2).astype(BF16)
        _accum(dgt_ref, _colsum(dh2 * ffv), i)

    return pl.pallas_call(
        body, name="loss_bwd", grid=(L // tm,),
        in_specs=[_row(D, tm=tm), _row(D, tm=tm), _vec(D), _vec(D), _row(D, tm=tm)],
        out_specs=(_row(D, tm=tm), _row(D, tm=tm), pl.BlockSpec((1, 1), lambda i: (0, 0)), _vec(D), _vec(D)),
        out_shape=(jax.ShapeDtypeStruct((L, D), F32), jax.ShapeDtypeStruct((L, D), BF16),
                   jax.ShapeDtypeStruct((1, 1), F32), jax.ShapeDtypeStruct((1, D), F32), jax.ShapeDtypeStruct((1, D), F32)),
        compiler_params=_params(("arbitrary",)),
    )(h1, ff, gate2, final_g, target)


def norm_bwd(dz, h, dh_in, g, scale, name, gate=None, m_out=None):
    L, D = h.shape
    tm = _tm(L)
    tail = gate is not None

    def body(*refs):
        dz_ref, h_ref, di_ref, g_ref, sc_ref = refs[:5]
        rest = refs[5:]
        if tail:
            gt_ref, m_ref = rest[:2]
            rest = rest[2:]
        dh_ref, dsc_ref, dsh_ref, dg_ref = rest[:4]
        i = pl.program_id(0)
        hv, dzv = h_ref[...], dz_ref[...].astype(F32)
        r = _rms(hv)
        n = hv * r
        _accum(dsc_ref, _colsum(dzv * n * g_ref[...]), i)
        _accum(dsh_ref, _colsum(dzv), i)
        dzn = dzv * (1.0 + sc_ref[...])
        _accum(dg_ref, _colsum(dzn * n), i)
        dn = dzn * g_ref[...]
        dh = di_ref[...] + r * (dn - n * jnp.mean(dn * n, axis=-1, keepdims=True))
        dh_ref[...] = dh
        if tail:
            dmo_ref, dgt_ref = rest[4:]
            dmo_ref[...] = (gt_ref[...] * dh).astype(BF16)
            _accum(dgt_ref, _colsum(dh * m_ref[...]), i)

    ins = [dz, h, dh_in, g, scale]
    in_specs = [_row(D, tm=tm)] * 3 + [_vec(D)] * 2
    out_specs = [_row(D, tm=tm), _vec(D), _vec(D), _vec(D)]
    out_shape = [jax.ShapeDtypeStruct((L, D), F32)] + [jax.ShapeDtypeStruct((1, D), F32)] * 3
    if tail:
        ins += [gate, m_out]
        in_specs += [_vec(D), _row(D, tm=tm)]
        out_specs += [_row(D, tm=tm), _vec(D)]
        out_shape += [jax.ShapeDtypeStruct((L, D), BF16), jax.ShapeDtypeStruct((1, D), F32)]
    return pl.pallas_call(
        body, name=name, grid=(L // tm,), in_specs=in_specs, out_specs=tuple(out_specs), out_shape=tuple(out_shape),
        compiler_params=_params(("arbitrary",)),
    )(*ins)


def merge_bwd(dmerged, proj, col_gc, y_conv, ga, gb):
    L, D = y_conv.shape
    tm = _tm(L)
    h = D // 2
    c0 = col_gc // h

    def body(dm_ref, p0, p1, p2, p3, yc_ref, ga_ref, gb_ref, dyc_ref, dga_ref, dgb_ref, dg_ref):
        gc, gs = (p0, p1), (p2, p3)
        for s in range(2):
            cols = slice(s * h, (s + 1) * h)
            dm = dm_ref[:, cols].astype(F32)
            sc, ss, sb = _sigmoid(gc[s][...]), _sigmoid(gs[s][...]), _sigmoid(gb_ref[:, cols].astype(F32))
            gav = ga_ref[:, cols].astype(F32)
            dyc_ref[:, cols] = (dm * sc).astype(BF16)
            dg_ref[:, cols] = (dm * yc_ref[:, cols].astype(F32) * sc * (1.0 - sc)).astype(BF16)
            dg_ref[:, D + s * h:D + (s + 1) * h] = (dm * gav * sb * ss * (1.0 - ss)).astype(BF16)
            dys = dm * ss
            dga_ref[:, cols] = (dys * sb).astype(BF16)
            dgb_ref[:, cols] = (dys * gav * sb * (1.0 - sb)).astype(BF16)

    return pl.pallas_call(
        body, name="merge_bwd", grid=(L // tm,),
        in_specs=[_row(D, tm=tm)] + [_row(h, c0 + s, tm) for s in range(4)] + [_row(D, tm=tm)] * 3,
        out_specs=(_row(D, tm=tm), _row(D, tm=tm), _row(D, tm=tm), _row(2 * D, tm=tm)),
        out_shape=(jax.ShapeDtypeStruct((L, D), BF16),) * 3 + (jax.ShapeDtypeStruct((L, 2 * D), BF16),),
        compiler_params=_params(("parallel",)),
    )(dmerged, proj, proj, proj, proj, y_conv, ga, gb)


def conv_ln_bwd(dvs, vc, ln_g, ln_b):
    L, C = vc.shape
    tm = _tm(L)

    def body(d_ref, v_ref, g_ref, b_ref, o_ref, dg_ref, db_ref):
        i = pl.program_id(0)
        v = v_ref[...]
        mu = jnp.mean(v, axis=-1, keepdims=True)
        d = v - mu
        rstd = lax.rsqrt(jnp.mean(d * d, axis=-1, keepdims=True) + EPS)
        xh = d * rstd
        ln = xh * g_ref[...] + b_ref[...]
        sg = _sigmoid(ln)
        dln = d_ref[...].astype(F32) * sg * (1.0 + ln * (1.0 - sg))
        _accum(dg_ref, _colsum(dln * xh), i)
        _accum(db_ref, _colsum(dln), i)
        dxh = dln * g_ref[...]
        o_ref[...] = rstd * (dxh - jnp.mean(dxh, axis=-1, keepdims=True)
                             - xh * jnp.mean(dxh * xh, axis=-1, keepdims=True))

    return pl.pallas_call(
        body, name="conv_ln_bwd", grid=(L // tm,),
        in_specs=[_row(C, tm=tm), _row(C, tm=tm), _vec(C), _vec(C)],
        out_specs=(_row(C, tm=tm), _vec(C), _vec(C)),
        out_shape=(jax.ShapeDtypeStruct((L, C), F32), jax.ShapeDtypeStruct((1, C), F32), jax.ShapeDtypeStruct((1, C), F32)),
        compiler_params=_params(("arbitrary",)),
    )(dvs, vc, ln_g, ln_b)


def conv_bwd(dvc, proj, w_dw):
    L, C = dvc.shape
    tm = _tm(L)
    hb = tm // HALO
    last = L // HALO - 1
    nt = L // tm

    def body(d_ref, dn_ref, a_ref, g_ref, ah_ref, gh_ref, w_ref, o_ref, dw_ref, db_ref, dbuf, vbuf, dsh, vsh, dw8):
        i = pl.program_id(0)
        dcur = d_ref[...]
        dbuf[0:tm, :] = dcur
        dbuf[tm:tm + HALO, :] = dn_ref[...] * jnp.where(i < nt - 1, 1.0, 0.0)
        av, sg = a_ref[...], _sigmoid(g_ref[...])
        vbuf[0:HALO, :] = ah_ref[...] * _sigmoid(gh_ref[...]) * jnp.where(i > 0, 1.0, 0.0)
        vbuf[HALO:HALO + tm, :] = av * sg
        _shifted_copies(dbuf, dsh, tm)
        _shifted_copies(vbuf, vsh, tm)
        dv = jnp.zeros((tm, C), F32)
        for k in range(CONV_K):
            dv = dv + w_ref[k:k + 1, :] * _window(dbuf, dsh, CONV_K - 1 - k, tm)
        o_ref[:, 0:C] = (dv * sg).astype(BF16)
        o_ref[:, C:2 * C] = (dv * av * sg * (1.0 - sg)).astype(BF16)

        @pl.when(i == 0)
        def _():
            dw8[...] = jnp.zeros_like(dw8)

        for k in range(CONV_K):
            prod = dcur * _window(vbuf, vsh, HALO - (CONV_K - 1) + k, tm)
            part = prod[0:8, :]
            for j in range(1, tm // 8):
                part = part + prod[8 * j:8 * j + 8, :]
            dw8[k] += part
        _accum(db_ref, _colsum(dcur), i)

        @pl.when(i == nt - 1)
        def _():
            for k in range(CONV_K):
                dw_ref[k:k + 1, :] = _colsum(dw8[k])
            dw_ref[CONV_K:HALO, :] = jnp.zeros((HALO - CONV_K, C), F32)

    prev = lambda cb: pl.BlockSpec((HALO, C), lambda i: (jnp.maximum(i * hb - 1, 0), cb))
    return pl.pallas_call(
        body, name="conv_bwd", grid=(nt,),
        in_specs=[_row(C, tm=tm), pl.BlockSpec((HALO, C), lambda i: (jnp.minimum((i + 1) * hb, last), 0)),
                  _row(C, 0, tm), _row(C, 1, tm), prev(0), prev(1), pl.BlockSpec((HALO, C), lambda i: (0, 0))],
        out_specs=(_row(2 * C, tm=tm), pl.BlockSpec((HALO, C), lambda i: (0, 0)), _vec(C)),
        out_shape=(jax.ShapeDtypeStruct((L, 2 * C), BF16), jax.ShapeDtypeStruct((HALO, C), F32),
                   jax.ShapeDtypeStruct((1, C), F32)),
        scratch_shapes=[pltpu.VMEM((tm + HALO, C), F32), pltpu.VMEM((HALO + tm, C), F32),
                        pltpu.VMEM((7, HALO + tm - 8, C), F32), pltpu.VMEM((7, HALO + tm - 8, C), F32),
                        pltpu.VMEM((CONV_K, 8, C), F32)],
        compiler_params=_params(("arbitrary",)),
    )(dvc, dvc, proj, proj, proj, proj, w_dw)


def _adamw(w, g, m, v):
    m = ADAM_B1 * m + (1.0 - ADAM_B1) * g
    v = ADAM_B2 * v + (1.0 - ADAM_B2) * (g * g)
    m_hat = m / (1.0 - ADAM_B1 ** ADAM_STEP)
    v_hat = v / (1.0 - ADAM_B2 ** ADAM_STEP)
    delta = -ADAM_LR * (m_hat / (jnp.sqrt(v_hat) + ADAM_EPS) + ADAM_WD * w)
    return delta, m, v


def _tile_rows(R, C):
    tr = 8
    while tr * 2 * C <= 128 * 1024 and R % (tr * 2) == 0:
        tr *= 2
    assert R % tr == 0, (R, C)
    return tr


def sum_devices(parts, name):
    _, R, C = parts.shape
    tr = _tile_rows(R, C)

    def body(p_ref, o_ref):
        s = p_ref[0]
        for j in range(1, NDEV):
            s = s + p_ref[j]
        o_ref[...] = s

    return pl.pallas_call(
        body, name=name, grid=(R // tr,),
        in_specs=[pl.BlockSpec((NDEV, tr, C), lambda i: (0, i, 0))],
        out_specs=pl.BlockSpec((tr, C), lambda i: (i, 0)), out_shape=jax.ShapeDtypeStruct((R, C), F32),
        compiler_params=_params(("parallel",)),
    )(parts)


def adam_update(w, g, m, v, name):
    R, C = w.shape
    tr = _tile_rows(R, C)

    def body(w_ref, g_ref, m_ref, v_ref, d_ref, mo_ref, vo_ref):
        d, mm, vv = _adamw(w_ref[...], g_ref[...], m_ref[...], v_ref[...])
        d_ref[...], mo_ref[...], vo_ref[...] = d, mm, vv

    spec = pl.BlockSpec((tr, C), lambda i: (i, 0))
    return pl.pallas_call(
        body, name=name, grid=(R // tr,), in_specs=[spec] * 4, out_specs=(spec,) * 3,
        out_shape=(jax.ShapeDtypeStruct((R, C), F32),) * 3, compiler_params=_params(("parallel",)),
    )(w, g, m, v)


def adam_many(items, name):
    n = len(items)

    def body(*refs):
        ins, outs = refs[:4 * n], refs[4 * n:]
        for i in range(n):
            w, g, m, v = [ins[4 * i + j][...] for j in range(4)]
            outs[3 * i][...], outs[3 * i + 1][...], outs[3 * i + 2][...] = _adamw(w, g, m, v)

    out_shape = [jax.ShapeDtypeStruct(w.shape, F32) for w, _, _, _ in items for _ in range(3)]
    res = pl.pallas_call(body, name=name, out_shape=tuple(out_shape), compiler_params=_params())(
        *[a for it in items for a in it])
    return [tuple(res[3 * i:3 * i + 3]) for i in range(n)]


def adam_reduce(parts, w, m, v, name):
    R, C = w.shape
    tr = _tile_rows(R, C)

    def body(p_ref, w_ref, m_ref, v_ref, g_ref, d_ref, mo_ref, vo_ref):
        g = p_ref[0].astype(F32)
        for j in range(1, NDEV):
            g = g + p_ref[j].astype(F32)
        g_ref[...] = g
        d, mm, vv = _adamw(w_ref[...], g, m_ref[...], v_ref[...])
        d_ref[...], mo_ref[...], vo_ref[...] = d, mm, vv

    spec = pl.BlockSpec((tr, C), lambda i: (i, 0))
    return pl.pallas_call(
        body, name=name, grid=(R // tr,),
        in_specs=[pl.BlockSpec((NDEV, tr, C), lambda i: (0, i, 0)), spec, spec, spec], out_specs=(spec,) * 4,
        out_shape=(jax.ShapeDtypeStruct((R, C), F32),) * 4, compiler_params=_params(("parallel",)),
    )(parts, w, m, v)


def adam_w_ada(c_act, dmod_cols, w, m, v):
    D, n = w.shape
    tn = 256

    def body(c_ref, dm_ref, w_ref, m_ref, v_ref, g_ref, d_ref, mo_ref, vo_ref):
        g = lax.dot_general(c_ref[...].astype(BF16), dm_ref[...].astype(BF16), (((0,), (0,)), ((), ())),
                            preferred_element_type=F32)
        g_ref[...] = g
        d, mm, vv = _adamw(w_ref[...], g, m_ref[...], v_ref[...])
        d_ref[...], mo_ref[...], vo_ref[...] = d, mm, vv

    spec = pl.BlockSpec((D, tn), lambda j: (0, j))
    return pl.pallas_call(
        body, name="adam_w_ada", grid=(n // tn,),
        in_specs=[pl.BlockSpec((NDEV, D), lambda j: (0, 0)), pl.BlockSpec((NDEV, tn), lambda j: (0, j)), spec, spec, spec],
        out_specs=(spec,) * 4, out_shape=(jax.ShapeDtypeStruct((D, n), F32),) * 4,
        compiler_params=_params(("parallel",)),
    )(c_act, dmod_cols, w, m, v)


def _block_diag(m):
    G, a, b = m.shape
    m4 = m.reshape(G // GB, GB, a, b)
    eye = jnp.eye(GB, dtype=m.dtype)
    return (m4[:, :, :, None, :] * eye[None, :, None, :, None]).reshape(G // GB, GB * a, GB * b)


def _diag_blocks(m, a, b):
    nb = m.shape[0]
    m5 = m.reshape(nb, GB, a, GB, b)
    idx = jnp.arange(GB)
    return m5[:, idx, :, idx, :].transpose(1, 0, 2, 3).reshape(nb * GB, a, b)


def _flat_pad(parts, mult):
    flat = jnp.concatenate([p.reshape(-1) for p in parts])
    pad = (-flat.shape[0]) % mult
    return jnp.pad(flat, (0, pad))


def _split(flat, like):
    out, off = [], 0
    for p in like:
        out.append(flat[off:off + p.size].reshape(p.shape))
        off += p.size
    return out


def kernel(x, c, w_ada, b_ada, norm1_g, w_in, w_dw, b_dw, ln_g, ln_b, w_conv_out, a_re, a_im, log_dt, b_re, b_im, c_re, c_im, d_skip, w_glu_a, w_glu_b, w_out, norm2_g, w_ff1, w_ff2, final_g, loss_target, m_w_ada, m_b_ada, m_norm1_g, m_w_in, m_w_dw, m_b_dw, m_ln_g, m_ln_b, m_w_conv_out, m_a_re, m_a_im, m_log_dt, m_b_re, m_b_im, m_c_re, m_c_im, m_d_skip, m_w_glu_a, m_w_glu_b, m_w_out, m_norm2_g, m_w_ff1, m_w_ff2, m_final_g, v_w_ada, v_b_ada, v_norm1_g, v_w_in, v_w_dw, v_b_dw, v_ln_g, v_ln_b, v_w_conv_out, v_a_re, v_a_im, v_log_dt, v_b_re, v_b_im, v_c_re, v_c_im, v_d_skip, v_w_glu_a, v_w_glu_b, v_w_out, v_norm2_g, v_w_ff1, v_w_ff2, v_final_g):
    W = dict(w_ada=w_ada, b_ada=b_ada, norm1_g=norm1_g, w_in=w_in, w_dw=w_dw, b_dw=b_dw, ln_g=ln_g, ln_b=ln_b,
             w_conv_out=w_conv_out, a_re=a_re, a_im=a_im, log_dt=log_dt, b_re=b_re, b_im=b_im, c_re=c_re, c_im=c_im,
             d_skip=d_skip, w_glu_a=w_glu_a, w_glu_b=w_glu_b, w_out=w_out, norm2_g=norm2_g, w_ff1=w_ff1, w_ff2=w_ff2,
             final_g=final_g)
    Mo = dict(w_ada=m_w_ada, b_ada=m_b_ada, norm1_g=m_norm1_g, w_in=m_w_in, w_dw=m_w_dw, b_dw=m_b_dw, ln_g=m_ln_g,
              ln_b=m_ln_b, w_conv_out=m_w_conv_out, a_re=m_a_re, a_im=m_a_im, log_dt=m_log_dt, b_re=m_b_re, b_im=m_b_im,
              c_re=m_c_re, c_im=m_c_im, d_skip=m_d_skip, w_glu_a=m_w_glu_a, w_glu_b=m_w_glu_b, w_out=m_w_out,
              norm2_g=m_norm2_g, w_ff1=m_w_ff1, w_ff2=m_w_ff2, final_g=m_final_g)
    Vo = dict(w_ada=v_w_ada, b_ada=v_b_ada, norm1_g=v_norm1_g, w_in=v_w_in, w_dw=v_w_dw, b_dw=v_b_dw, ln_g=v_ln_g,
              ln_b=v_ln_b, w_conv_out=v_w_conv_out, a_re=v_a_re, a_im=v_a_im, log_dt=v_log_dt, b_re=v_b_re, b_im=v_b_im,
              c_re=v_c_re, c_im=v_c_im, d_skip=v_d_skip, w_glu_a=v_w_glu_a, w_glu_b=v_w_glu_b, w_out=v_w_out,
              norm2_g=v_norm2_g, w_ff1=v_w_ff1, w_ff2=v_w_ff2, final_g=v_final_g)
    names = list(W)

    me = _me()
    xs, tgt = x[0], loss_target[0]
    L, D = xs.shape
    CW = w_dw.shape[2] * NDEV
    G, P = a_re.shape[1], a_re.shape[2]
    H = b_re.shape[3]
    n_ada = w_ada.shape[2]

    (c_all,) = _exchange([c], "gather_c", True)
    b_cols = lax.dynamic_slice(b_ada, (0, me * n_ada), (1, n_ada))
    mod_cols, c_act = adaln_mod(c_all.reshape(NDEV, D), w_ada[0], b_cols)
    (mod_all,) = _exchange([mod_cols], "gather_mod", True)
    mod = lax.dynamic_slice(mod_all, (0, me, 0), (NDEV, 1, n_ada)).reshape(6, 1, D)
    shift1, scale1, gate1, shift2, scale2, gate2 = [mod[j] for j in range(6)]

    big = ["w_in", "w_conv_out", "w_glu_a", "w_glu_b", "w_out", "w_ff1", "w_ff2"]
    order = ["w_in", "w_dw"] + big[1:]
    shards = {k: W[k][0].astype(BF16) for k in big}
    shards["w_dw"] = jnp.pad(w_dw[0], ((0, HALO - CONV_K), (0, 0)))
    gather_handle = dict(zip(order, gather2_start([shards[k] for k in order], "gather_weights_start", mod_all)))

    def forward(ks, name, after):
        gather_handle.update(zip(ks, gather2_forward([gather_handle[k] for k in ks], name, after)))

    def weight(k, after):
        w = gather2_wait(gather_handle[k], "gather_wait_" + k, after)
        if k in ("w_out", "w_ff2"):
            w = w.reshape(1, w.shape[0] * w.shape[1], w.shape[2])
        elif k in narrow:
            w = w.transpose(1, 0, 2).reshape(1, w.shape[1], NDEV * w.shape[2])
        elif k == "w_dw":
            w = w.transpose(1, 0, 2).reshape(HALO, CW)
        return w

    narrow = ("w_conv_out", "w_glu_a", "w_glu_b")
    scatter_handle = {}

    def scatter(k, g):
        if k in narrow:
            g = g.reshape(g.shape[1], NDEV, -1).transpose(1, 0, 2)
        elif g.shape[0] == 1:
            g = g.reshape(NDEV, -1, g.shape[2])
        (scatter_handle[k],), token = exchange_start([g], "scatter_start_" + k, False)
        return token

    big_out = {}

    def finish_weight(k, after):
        parts = exchange_wait(scatter_handle[k], after, "scatter_wait_" + k, False)
        big_out[k] = adam_reduce(parts, W[k][0], Mo[k][0], Vo[k][0], "adam_" + k)
        return big_out[k][1]

    u = prenorm(xs, norm1_g, scale1, shift1, "prenorm1")

    br2 = b_re[0].transpose(0, 2, 1).reshape(G * H, P)
    bi2 = b_im[0].transpose(0, 2, 1).reshape(G * H, P)
    ldt = log_dt[0].reshape(G, 1)
    expand = jnp.repeat(jnp.eye(G, dtype=F32), H, axis=0)
    lbr, lbi, bbr, bbi = s5_params(a_re[0], a_im[0], ldt, br2, bi2, expand)
    tabs = s5_tables(lbr.reshape(1, G * P), lbi.reshape(1, G * P))
    bdr, bdi = _block_diag(bbr.reshape(G, H, P)), _block_diag(bbi.reshape(G, H, P))
    cdr = _block_diag(c_re[0].transpose(0, 2, 1))
    cdi = _block_diag(c_im[0].transpose(0, 2, 1))
    cd2 = jnp.concatenate([cdr, -cdi], axis=1).astype(BF16)
    bdr3, bdi3 = _rhs3(bdr), _rhs3(bdi)
    bdt2 = jnp.concatenate([bdr.transpose(0, 2, 1), bdi.transpose(0, 2, 1)], axis=1).astype(BF16)
    cdrt3, cdit3 = _rhs3(cdr.transpose(0, 2, 1)), _rhs3(-cdi.transpose(0, 2, 1))

    forward(["w_in"], "gather_forward_in", (u, tabs, bdr3, bdi3, cd2, bdt2, cdrt3, cdit3))
    wg = {"w_in": weight("w_in", u)}
    proj = mm_nn(u, wg["w_in"], "in_proj")
    forward(["w_dw", "w_conv_out", "w_glu_a", "w_glu_b", "w_out"], "gather_forward_mix", proj)
    w_dw_full = weight("w_dw", proj)
    vs, vc = conv_fwd(proj, w_dw_full, b_dw, ln_g, ln_b)
    wg["w_conv_out"] = weight("w_conv_out", vs)
    y_conv = mm_nn(vs, wg["w_conv_out"], "conv_out", out_dtype=BF16)
    s_re, s_im, y_pre, yg = s5_fwd(proj, 2 * CW, bdr3, bdi3, cd2, tabs, d_skip)
    forward(["w_ff1"], "gather_forward_ff1", yg)
    wg["w_glu_a"] = weight("w_glu_a", yg)
    wg["w_glu_b"] = weight("w_glu_b", yg)
    ga = mm_nn(yg, wg["w_glu_a"], "glu_a", out_dtype=BF16)
    gb = mm_nn(yg, wg["w_glu_b"], "glu_b", out_dtype=BF16)
    merged = merge_fwd(proj, 3 * CW, y_conv, ga, gb)
    forward(["w_ff2"], "gather_forward_ff2", merged)
    wg["w_out"] = weight("w_out", merged)
    m_out = mm_nn(merged, wg["w_out"], "out_proj")
    h1, z = residual_norm(xs, m_out, gate1, norm2_g, scale2, shift2)
    wg["w_ff1"] = weight("w_ff1", z)
    act = mm_nn(z, wg["w_ff1"], "ff1", out_dtype=BF16, epi=lambda r: jnp.square(jnp.maximum(r, 0.0)))
    wg["w_ff2"] = weight("w_ff2", act)
    ff = mm_nn(act, wg["w_ff2"], "ff2")

    dh2, dff, loss_part, d_final_g, d_gate2 = loss_bwd(h1, ff, gate2, final_g.reshape(1, D), tgt)
    df = mm_nt(dff, wg["w_ff2"], "ff2_dx", out_dtype=BF16,
               epi=lambda r, a: r * (2.0 * jnp.sqrt(a.astype(F32))), extras=(act,))
    t = scatter("w_ff2", mm_tn(act, dff, 1, "ff2_dw", out_dtype=BF16))
    t = scatter("w_ff1", mm_tn(z, df, NDEV, "ff1_dw", out_dtype=BF16, dep=t))
    dz = mm_nt(df, wg["w_ff1"], "ff1_dx", out_dtype=BF16, dep=t)
    dh1, d_scale2, d_shift2, d_norm2_g, dmo, d_gate1 = norm_bwd(dz, h1, dh2, norm2_g, scale2, "norm2_bwd", gate1, m_out)
    t = scatter("w_out", mm_tn(merged, dmo, 1, "out_dw", out_dtype=BF16))
    dmerged = mm_nt(dmo, wg["w_out"], "out_dx", out_dtype=BF16, dep=t)
    dyc, dga, dgb, dproj_g = merge_bwd(dmerged, proj, 3 * CW, y_conv, ga, gb)
    t = scatter("w_conv_out", mm_tn(vs, dyc, 1, "conv_out_dw", out_dtype=BF16))
    t = scatter("w_glu_a", mm_tn(yg, dga, 1, "glu_a_dw", out_dtype=BF16, dep=t))
    t = scatter("w_glu_b", mm_tn(yg, dgb, 1, "glu_b_dw", out_dtype=BF16, dep=t))
    dvs = mm_nt(dyc, wg["w_conv_out"], "conv_out_dx", out_dtype=BF16, dep=t)
    dyg_a = mm_nt(dga, wg["w_glu_a"], "glu_a_dx", out_dtype=BF16, dep=t)
    dyg_b = mm_nt(dgb, wg["w_glu_b"], "glu_b_dx", out_dtype=BF16, dep=t)
    dvc, d_ln_g, d_ln_b = conv_ln_bwd(dvs, vc, ln_g, ln_b)
    dproj_c, d_w_dw, d_b_dw = conv_bwd(dvc, proj, w_dw_full)
    dproj_s, d_d_skip, dcdr, dcdi, dbdr, dbdi, dlr8, dli8 = s5_bwd(
        dyg_a, dyg_b, y_pre, proj, 2 * CW, s_re, s_im, bdt2, cdrt3, cdit3, tabs, d_skip)
    done = dproj_s
    for k in ("w_ff2", "w_ff1", "w_out"):
        done = finish_weight(k, done)
    dproj = jnp.concatenate([dproj_c, dproj_s, dproj_g], axis=1)
    t = scatter("w_in", mm_tn(u, dproj, NDEV, "in_dw", out_dtype=BF16, dep=done))
    du = mm_nt(dproj, wg["w_in"], "in_dx", out_dtype=BF16, dep=t)
    grad_x, d_scale1, d_shift1, d_norm1_g = norm_bwd(du, xs, dh1, norm1_g, scale1, "norm1_bwd")

    dmod = jnp.concatenate([d_shift1, d_scale1, d_gate1, d_shift2, d_scale2, d_gate2], axis=1)
    d_c_re = _diag_blocks(dcdr, H, P)
    d_c_im = _diag_blocks(dcdi, H, P)
    d_bbr = _diag_blocks(dbdr, H, P)
    d_bbi = _diag_blocks(dbdi, H, P)
    dlr = jnp.sum(dlr8, axis=0).reshape(G, P)
    dli = jnp.sum(dli8, axis=0).reshape(G, P)
    small_parts = [dmod, d_norm1_g, d_b_dw, d_ln_g, d_ln_b, dlr, dli, d_bbr, d_bbi, d_c_re, d_c_im, d_d_skip,
                   d_norm2_g, d_final_g, d_w_dw]
    pack8 = _flat_pad(small_parts, PACK).reshape(NDEV, -1, 1024)
    parts8, dmod_from = _exchange([pack8, dmod.reshape(NDEV, 1, n_ada)], "scatter_small_grads", False)
    (tot8,) = _exchange([sum_devices(parts8, "sum_small_grads")], "gather_small_sums", True)
    tot = tot8.reshape(-1)
    (g_b_ada, g_norm1_g, g_b_dw, g_ln_g, g_ln_b, t_lr, t_li, t_bbr, t_bbi, g_c_re_t, g_c_im_t, g_d_skip,
     g_norm2_g, g_final_g, g_w_dw_full) = _split(tot, small_parts)
    g_a_re, g_a_im, g_ldt, g_br2, g_bi2 = s5_params_bwd(
        a_re[0], a_im[0], ldt, br2, bi2, expand, t_lr, t_li, t_bbr.reshape(G * H, P), t_bbi.reshape(G * H, P))
    g_brt, g_bit = g_br2.reshape(G, H, P), g_bi2.reshape(G, H, P)
    dmod_cols = dmod_from.reshape(NDEV, n_ada)

    g2 = {
        "b_ada": g_b_ada, "norm1_g": g_norm1_g, "b_dw": g_b_dw, "ln_g": g_ln_g, "ln_b": g_ln_b,
        "a_re": g_a_re, "a_im": g_a_im, "log_dt": g_ldt.reshape(1, G),
        "b_re": g_brt.transpose(0, 2, 1).reshape(G * P, H), "b_im": g_bit.transpose(0, 2, 1).reshape(G * P, H),
        "c_re": g_c_re_t.reshape(G * H, P), "c_im": g_c_im_t.reshape(G * H, P), "d_skip": g_d_skip,
        "norm2_g": g_norm2_g, "final_g": g_final_g,
        "w_dw": lax.dynamic_slice(g_w_dw_full, (0, me * (CW // NDEV)), (CONV_K, CW // NDEV)),
    }
    small = [k for k in names if k in g2]
    as2d = lambda k, a: a.reshape(g2[k].shape)
    outs = adam_many([(as2d(k, W[k]), g2[k], as2d(k, Mo[k]), as2d(k, Vo[k])) for k in small], "adam_small")
    grads = {k: g2[k].reshape(W[k].shape) for k in small}
    delta = {k: o[0].reshape(W[k].shape) for k, o in zip(small, outs)}
    new_m = {k: o[1].reshape(W[k].shape) for k, o in zip(small, outs)}
    new_v = {k: o[2].reshape(W[k].shape) for k, o in zip(small, outs)}

    g, d, mm, vv = adam_w_ada(c_act, dmod_cols, w_ada[0], m_w_ada[0], v_w_ada[0])
    grads["w_ada"], delta["w_ada"], new_m["w_ada"], new_v["w_ada"] = g[None], d[None], mm[None], vv[None]

    after = d
    for k in ("w_conv_out", "w_glu_a", "w_glu_b", "w_in"):
        after = finish_weight(k, after)
    for k in big:
        g, d, mm, vv = big_out[k]
        grads[k], delta[k], new_m[k], new_v[k] = g[None], d[None], mm[None], vv[None]

    loss = lax.psum(loss_part[0, 0], ("x", "y", "c"))
    return (loss, grad_x[None], *[grads[k] for k in names], *[delta[k] for k in names],
            *[new_m[k] for k in names], *[new_v[k] for k in names])
```

```python
import functools
import math

import jax
import jax.numpy as jnp
from jax import lax
from jax.experimental import pallas as pl
from jax.experimental.pallas import tpu as pltpu

F32 = jnp.float32
BF16 = jnp.bfloat16
NDEV = 8
EPS = 1e-6
ADAM_LR, ADAM_B1, ADAM_B2, ADAM_EPS, ADAM_WD, ADAM_STEP = 0.001, 0.9, 0.999, 1e-08, 0.01, 10
CONV_K = 31
HALO = 32
GROUP = 16
STATE = 64
GB = 8
S5_ROWS = 512
HI = lax.Precision.HIGHEST
MESH = pl.DeviceIdType.MESH
VMEM_LIMIT = 56 * 1024 * 1024
MAX_CONTRACT = 2048
PACK_ROWS = 64
PACK = PACK_ROWS * 1024
ANY = pl.BlockSpec(memory_space=pl.ANY)


def _params(sem=None):
    if sem is None:
        return pltpu.CompilerParams(vmem_limit_bytes=VMEM_LIMIT)
    return pltpu.CompilerParams(dimension_semantics=sem, vmem_limit_bytes=VMEM_LIMIT)


def _sigmoid(v):
    return 1.0 / (1.0 + jnp.exp(-v))


def _me():
    return 4 * lax.axis_index("x") + 2 * lax.axis_index("y") + lax.axis_index("c")


def _peer(k):
    x, y, c = lax.axis_index("x"), lax.axis_index("y"), lax.axis_index("c")
    px = 1 - x if (k >> 2) & 1 else x
    py = 1 - y if (k >> 1) & 1 else y
    pc = 1 - c if k & 1 else c
    return (px, py, pc), 4 * px + 2 * py + pc


def _exchange(arrays, name, gather):
    n = len(arrays)
    out_shape = []
    for a in arrays:
        shp = (NDEV,) + a.shape if gather else a.shape
        out_shape.append(jax.ShapeDtypeStruct(shp, a.dtype))

    def body(*refs):
        ins, outs = refs[:n], refs[n:2 * n]
        send, recv, lsem = refs[2 * n:]
        me = _me()
        local = []
        for a in range(n):
            src = ins[a] if gather else ins[a].at[me]
            cp = pltpu.make_async_copy(src, outs[a].at[me], lsem.at[a])
            cp.start()
            local.append(cp)
        sends = []
        for a in range(n):
            for k in range(1, NDEV):
                dev, pidx = _peer(k)
                src = ins[a] if gather else ins[a].at[pidx]
                cp = pltpu.make_async_remote_copy(
                    src_ref=src, dst_ref=outs[a].at[me], send_sem=send.at[a * (NDEV - 1) + k - 1], recv_sem=recv.at[a * (NDEV - 1) + k - 1],
                    device_id=dev, device_id_type=MESH)
                cp.start()
                sends.append(cp)
        for a in range(n):
            for k in range(1, NDEV):
                dev, pidx = _peer(k)
                src = ins[a] if gather else ins[a].at[pidx]
                pltpu.make_async_remote_copy(
                    src_ref=src, dst_ref=outs[a].at[pidx], send_sem=send.at[a * (NDEV - 1) + k - 1], recv_sem=recv.at[a * (NDEV - 1) + k - 1],
                    device_id=dev, device_id_type=MESH).wait_recv()
        for cp in sends:
            cp.wait_send()
        for cp in local:
            cp.wait()

    return pl.pallas_call(
        body, name=name, out_shape=tuple(out_shape),
        in_specs=[ANY] * n, out_specs=tuple([ANY] * n),
        scratch_shapes=[pltpu.SemaphoreType.DMA((n * (NDEV - 1),)), pltpu.SemaphoreType.DMA((n * (NDEV - 1),)),
                        pltpu.SemaphoreType.DMA((n,))],
    )(*arrays)


HBM = pl.BlockSpec(memory_space=pltpu.HBM)
SEM = pl.BlockSpec(memory_space=pltpu.SEMAPHORE)
EFFECT = pltpu.SideEffectType.DATAFLOW_SIDE_EFFECTING
NPEER = NDEV - 1


def _landing(block_of_me, shape, dtype):
    land = lax.empty((NDEV,) + tuple(shape), dtype)
    start = (_me(),) + (0,) * len(shape)
    return pltpu.with_memory_space_constraint(lax.dynamic_update_slice(land, block_of_me[None], start), pltpu.HBM)


def exchange_start(arrays, name, gather, after=None):
    n = len(arrays)
    me = _me()
    deps = () if after is None else (after,)
    lands = []
    for a in arrays:
        if gather:
            lands.append(_landing(a, a.shape, a.dtype))
        else:
            mine = lax.dynamic_slice(a, (me,) + (0,) * (a.ndim - 1), (1,) + a.shape[1:])[0]
            lands.append(_landing(mine, a.shape[1:], a.dtype))
    srcs = [pltpu.with_memory_space_constraint(a, pltpu.HBM) for a in arrays]

    def body(*refs):
        ins, lnd = refs[:n], refs[n:2 * n]
        outs = refs[2 * n + len(deps):]
        sends, recvs, token = outs[:n], outs[n:2 * n], outs[-1]
        my = _me()
        for a in range(n):
            for k in range(1, NDEV):
                dev, pidx = _peer(k)
                src = ins[a] if gather else ins[a].at[pidx]
                pltpu.make_async_remote_copy(
                    src_ref=src, dst_ref=lnd[a].at[my], send_sem=sends[a].at[k - 1], recv_sem=recvs[a].at[k - 1],
                    device_id=dev, device_id_type=MESH).start()
        token[...] = jnp.zeros_like(token)

    out_shape = ([pltpu.SemaphoreType.DMA((NPEER,))] * (2 * n)
                 + [pltpu.HBM(a.shape, a.dtype) for a in srcs] + [pltpu.HBM(l.shape, l.dtype) for l in lands]
                 + [jax.ShapeDtypeStruct((8, 128), F32)])
    res = pl.pallas_call(
        body, name=name, out_shape=tuple(out_shape),
        in_specs=[HBM] * (2 * n) + [ANY] * len(deps),
        out_specs=tuple([SEM] * (2 * n) + [HBM] * (2 * n) + [pl.BlockSpec(memory_space=pltpu.VMEM)]),
        input_output_aliases={i: 2 * n + i for i in range(2 * n)},
        compiler_params=pltpu.CompilerParams(has_side_effects=EFFECT),
    )(*srcs, *lands, *deps)
    handles = [(res[a], res[n + a], res[2 * n + a], res[3 * n + a]) for a in range(n)]
    return handles, res[-1]


def exchange_wait(handle, after, name, gather):
    send_sem, recv_sem, src, land = handle

    def body(src_ref, land_ref, s_ref, r_ref, after_ref, src_out, land_out):
        for k in range(1, NDEV):
            dev, pidx = _peer(k)
            s = src_ref if gather else src_ref.at[pidx]
            cp = pltpu.make_async_remote_copy(
                src_ref=s, dst_ref=land_ref.at[pidx], send_sem=s_ref.at[k - 1], recv_sem=r_ref.at[k - 1],
                device_id=dev, device_id_type=MESH)
            cp.wait_send()
            cp.wait_recv()

    return pl.pallas_call(
        body, name=name, out_shape=(pltpu.HBM(src.shape, src.dtype), pltpu.HBM(land.shape, land.dtype)),
        in_specs=(HBM, HBM, SEM, SEM, ANY), out_specs=(HBM, HBM), input_output_aliases={0: 0, 1: 1},
        compiler_params=pltpu.CompilerParams(has_side_effects=EFFECT),
    )(src, land, send_sem, recv_sem, after)[1]


ICI_PEERS = (2, 4, 6)
SIBLING = 1


def gather2_start(blocks, name, after):
    m = len(blocks)
    lands = [_landing(b, b.shape, b.dtype) for b in blocks]
    srcs = [pltpu.with_memory_space_constraint(b, pltpu.HBM) for b in blocks]
    n = len(ICI_PEERS)

    def body(*refs):
        src, lnd = refs[:m], refs[m:2 * m]
        outs = refs[2 * m + 1:]
        send, recv_sib, recv_ici = outs[:m], outs[m:2 * m], outs[2 * m:3 * m]
        my = _me()
        for a in range(m):
            dev, _ = _peer(SIBLING)
            pltpu.make_async_remote_copy(src_ref=src[a], dst_ref=lnd[a].at[my], send_sem=send[a].at[0],
                                         recv_sem=recv_sib[a].at[0], device_id=dev, device_id_type=MESH).start()
            for j, k in enumerate(ICI_PEERS):
                dev, _ = _peer(k)
                pltpu.make_async_remote_copy(src_ref=src[a], dst_ref=lnd[a].at[my], send_sem=send[a].at[1 + j],
                                             recv_sem=recv_ici[a].at[j], device_id=dev, device_id_type=MESH).start()

    out_shape = ([pltpu.SemaphoreType.DMA((1 + n,))] * m + [pltpu.SemaphoreType.DMA((1,))] * m
                 + [pltpu.SemaphoreType.DMA((n,))] * m
                 + [pltpu.HBM(s.shape, s.dtype) for s in srcs] + [pltpu.HBM(l.shape, l.dtype) for l in lands])
    res = pl.pallas_call(
        body, name=name, out_shape=tuple(out_shape),
        in_specs=[HBM] * (2 * m) + [ANY], out_specs=tuple([SEM] * (3 * m) + [HBM] * (2 * m)),
        input_output_aliases={i: 3 * m + i for i in range(2 * m)},
        compiler_params=pltpu.CompilerParams(has_side_effects=EFFECT),
    )(*srcs, *lands, after)
    return [tuple(res[g * m + a] for g in range(5)) for a in range(m)]


def gather2_forward(handles, name, after):
    m = len(handles)
    n = len(ICI_PEERS)
    srcs, lands = [h[3] for h in handles], [h[4] for h in handles]
    deps = tuple(after) if isinstance(after, (tuple, list)) else (after,)

    def body(*refs):
        src, lnd, recv_ici = refs[:m], refs[m:2 * m], refs[2 * m:3 * m]
        outs = refs[3 * m + len(deps):]
        fsend, frecv = outs[:m], outs[m:2 * m]
        sib, _ = _peer(SIBLING)
        for a in range(m):
            for j, k in enumerate(ICI_PEERS):
                dev, pidx = _peer(k)
                pltpu.make_async_remote_copy(
                    src_ref=src[a], dst_ref=lnd[a].at[pidx], send_sem=fsend[a].at[j], recv_sem=recv_ici[a].at[j],
                    device_id=dev, device_id_type=MESH).wait_recv()
                pltpu.make_async_remote_copy(
                    src_ref=lnd[a].at[pidx], dst_ref=lnd[a].at[pidx], send_sem=fsend[a].at[j], recv_sem=frecv[a].at[j],
                    device_id=sib, device_id_type=MESH).start()

    out_shape = ([pltpu.SemaphoreType.DMA((n,))] * (2 * m)
                 + [pltpu.HBM(s.shape, s.dtype) for s in srcs] + [pltpu.HBM(l.shape, l.dtype) for l in lands])
    res = pl.pallas_call(
        body, name=name, out_shape=tuple(out_shape),
        in_specs=[HBM] * (2 * m) + [SEM] * m + [ANY] * len(deps),
        out_specs=tuple([SEM] * (2 * m) + [HBM] * (2 * m)),
        input_output_aliases={i: 2 * m + i for i in range(2 * m)},
        compiler_params=pltpu.CompilerParams(has_side_effects=EFFECT),
    )(*srcs, *lands, *[h[2] for h in handles], *deps)
    return [(handles[a][0], handles[a][1], res[a], res[m + a], res[2 * m + a], res[3 * m + a]) for a in range(m)]


def gather2_wait(handle, name, after):
    send, recv_sib, fsend, frecv, src, land = handle

    def body(src_ref, land_ref, send_ref, recv_sib_ref, fsend_ref, frecv_ref, after_ref, src_out, land_out):
        sib, sib_idx = _peer(SIBLING)
        own = pltpu.make_async_remote_copy(src_ref=src_ref, dst_ref=land_ref.at[sib_idx], send_sem=send_ref.at[0],
                                           recv_sem=recv_sib_ref.at[0], device_id=sib, device_id_type=MESH)
        own.wait_send()
        own.wait_recv()
        for j, k in enumerate(ICI_PEERS):
            dev, pidx = _peer(k)
            pltpu.make_async_remote_copy(src_ref=src_ref, dst_ref=land_ref.at[pidx], send_sem=send_ref.at[1 + j],
                                         recv_sem=frecv_ref.at[j], device_id=dev, device_id_type=MESH).wait_send()
            _, fidx = _peer(k ^ SIBLING)
            fwd = pltpu.make_async_remote_copy(src_ref=land_ref.at[pidx], dst_ref=land_ref.at[fidx],
                                               send_sem=fsend_ref.at[j], recv_sem=frecv_ref.at[j],
                                               device_id=sib, device_id_type=MESH)
            fwd.wait_send()
            fwd.wait_recv()

    return pl.pallas_call(
        body, name=name, out_shape=(pltpu.HBM(src.shape, src.dtype), pltpu.HBM(land.shape, land.dtype)),
        in_specs=(HBM, HBM, SEM, SEM, SEM, SEM, ANY), out_specs=(HBM, HBM), input_output_aliases={0: 0, 1: 1},
        compiler_params=pltpu.CompilerParams(has_side_effects=EFFECT),
    )(src, land, send, recv_sib, fsend, frecv, after)[1]


def _acc_steps(p, acc, k, nk, finish):
    if nk == 1:
        finish(p)
        return

    @pl.when(k == 0)
    def _():
        acc[...] = p

    @pl.when(k > 0)
    def _():
        acc[...] += p

    @pl.when(k == nk - 1)
    def _():
        finish(acc[...])


def mm_nn(a, w3, name, out_dtype=F32, epi=None, extras=()):
    M, K = a.shape
    J, _, n = w3.shape
    tm, tn, tk = min(1024, M), min(1024, n), min(2048, K)
    q, nk, ne = n // tn, K // tk, len(extras)

    def body(*refs):
        a_ref, w_ref = refs[:2]
        ex, o_ref, acc = refs[2:2 + ne], refs[2 + ne], refs[-1]
        p = jnp.dot(a_ref[...], w_ref[...], preferred_element_type=F32)

        def finish(r):
            if epi is not None:
                r = epi(r, *[e[...] for e in ex])
            o_ref[...] = r.astype(out_dtype)

        _acc_steps(p, acc, pl.program_id(2), nk, finish)

    return pl.pallas_call(
        body, name=name, grid=(M // tm, J * q, nk),
        in_specs=[pl.BlockSpec((tm, tk), lambda i, j, k: (i, k)),
                  pl.BlockSpec((None, tk, tn), lambda i, j, k: (j // q, k, j % q))]
        + [pl.BlockSpec((tm, tn), lambda i, j, k: (i, j))] * ne,
        out_specs=pl.BlockSpec((tm, tn), lambda i, j, k: (i, j)),
        out_shape=jax.ShapeDtypeStruct((M, J * n), out_dtype),
        scratch_shapes=[pltpu.VMEM((tm, tn), F32)],
        compiler_params=_params(("parallel", "parallel", "arbitrary")),
    )(a, w3, *extras)


def mm_nt(dy, w3, name, out_dtype=F32, epi=None, extras=(), dep=None):
    M, _ = dy.shape
    J, K, n = w3.shape
    tm, tn, tkk = min(1024, M), min(MAX_CONTRACT, n), min(1024, K)
    q, ne = n // tn, len(extras)
    s = 1
    while q == 1 and J % (2 * s) == 0 and 2 * s * tn <= MAX_CONTRACT:
        s *= 2
    nk = (J // s) * q
    deps = () if dep is None else (dep,)

    def body(*refs):
        d_ref, w_ref = refs[:2]
        ex, o_ref, acc = refs[2:2 + ne], refs[-2], refs[-1]
        nt = (((1,), (1,)), ((), ()))
        p = lax.dot_general(d_ref[:, 0:tn], w_ref[0], nt, preferred_element_type=F32)
        for j in range(1, s):
            p = p + lax.dot_general(d_ref[:, j * tn:(j + 1) * tn], w_ref[j], nt, preferred_element_type=F32)

        def finish(r):
            if epi is not None:
                r = epi(r, *[e[...] for e in ex])
            o_ref[...] = r.astype(out_dtype)

        _acc_steps(p, acc, pl.program_id(2), nk, finish)

    return pl.pallas_call(
        body, name=name, grid=(M // tm, K // tkk, nk),
        in_specs=[pl.BlockSpec((tm, s * tn), lambda i, kk, c: (i, c)),
                  pl.BlockSpec((s, tkk, tn), lambda i, kk, c: (c // q, kk, c % q))]
        + [pl.BlockSpec((tm, tkk), lambda i, kk, c: (i, kk))] * ne + [ANY] * len(deps),
        out_specs=pl.BlockSpec((tm, tkk), lambda i, kk, c: (i, kk)),
        out_shape=jax.ShapeDtypeStruct((M, K), out_dtype),
        scratch_shapes=[pltpu.VMEM((tm, tkk), F32)],
        compiler_params=_params(("parallel", "parallel", "arbitrary")),
    )(dy, w3, *extras, *deps)


def mm_tn(a, dy, J, name, out_dtype=F32, dep=None):
    M, K = a.shape
    n = dy.shape[1] // J
    tm, tn, tkk = min(MAX_CONTRACT, M), min(1024, n), min(1024, K)
    q, nk = n // tn, M // tm
    deps = () if dep is None else (dep,)

    def body(a_ref, d_ref, *rest):
        o_ref, acc = rest[-2:]
        p = lax.dot_general(a_ref[...], d_ref[...], (((0,), (0,)), ((), ())), preferred_element_type=F32)

        def finish(r):
            o_ref[...] = r.astype(out_dtype)

        _acc_steps(p, acc, pl.program_id(2), nk, finish)

    return pl.pallas_call(
        body, name=name, grid=(K // tkk, J * q, nk),
        in_specs=[pl.BlockSpec((tm, tkk), lambda kk, c, m: (m, kk)),
                  pl.BlockSpec((tm, tn), lambda kk, c, m: (m, c))] + [ANY] * len(deps),
        out_specs=pl.BlockSpec((None, tkk, tn), lambda kk, c, m: (c // q, kk, c % q)),
        out_shape=jax.ShapeDtypeStruct((J, K, n), out_dtype),
        scratch_shapes=[pltpu.VMEM((tkk, tn), F32)],
        compiler_params=_params(("parallel", "parallel", "arbitrary")),
    )(a, dy, *deps)


def _tm(L):
    return min(256, L)


def _row(w, cb=0, tm=None):
    return pl.BlockSpec((tm, w), lambda i: (i, cb))


def _vec(w, cb=0):
    return pl.BlockSpec((1, w), lambda i: (0, cb))


def _accum(ref, val, i):
    @pl.when(i == 0)
    def _():
        ref[...] = val

    @pl.when(i > 0)
    def _():
        ref[...] += val


def _colsum(v):
    return jnp.sum(v, axis=0, keepdims=True)


def _rms(v):
    return lax.rsqrt(jnp.mean(v * v, axis=-1, keepdims=True) + EPS)


def adaln_mod(c_all, w_ada, b_cols):
    B, D = c_all.shape
    n = w_ada.shape[1]
    tn = 512

    def body(c_ref, w_ref, b_ref, o_ref, ca_ref):
        cv = c_ref[...]
        ca = cv * _sigmoid(cv)
        ca_ref[...] = ca
        o_ref[...] = jnp.dot(ca.astype(BF16), w_ref[...].astype(BF16), preferred_element_type=F32) + b_ref[...]

    return pl.pallas_call(
        body, name="adaln_mod", grid=(n // tn,),
        in_specs=[pl.BlockSpec((B, D), lambda j: (0, 0)), pl.BlockSpec((D, tn), lambda j: (0, j)),
                  pl.BlockSpec((1, tn), lambda j: (0, j))],
        out_specs=(pl.BlockSpec((B, tn), lambda j: (0, j)), pl.BlockSpec((B, D), lambda j: (0, 0))),
        out_shape=(jax.ShapeDtypeStruct((B, n), F32), jax.ShapeDtypeStruct((B, D), F32)),
        compiler_params=_params(("arbitrary",)),
    )(c_all, w_ada, b_cols)


def prenorm(x, g, scale, shift, name):
    L, D = x.shape
    tm = _tm(L)

    def body(x_ref, g_ref, sc_ref, sh_ref, u_ref):
        xv = x_ref[...]
        u_ref[...] = (xv * _rms(xv) * g_ref[...] * (1.0 + sc_ref[...]) + sh_ref[...]).astype(BF16)

    return pl.pallas_call(
        body, name=name, grid=(L // tm,),
        in_specs=[_row(D, tm=tm), _vec(D), _vec(D), _vec(D)],
        out_specs=_row(D, tm=tm), out_shape=jax.ShapeDtypeStruct((L, D), BF16),
        compiler_params=_params(("parallel",)),
    )(x, g, scale, shift)


def _shifted_copies(buf, shifted, tm):
    n = HALO + tm - 8
    for r in range(1, 8):
        shifted[r - 1] = buf[pl.ds(r, n), :]


def _window(buf, shifted, off, tm):
    r, base = off % 8, off - off % 8
    if r == 0:
        return buf[pl.ds(base, tm), :]
    return shifted[r - 1, pl.ds(base, tm), :]


def conv_fwd(proj, w_dw, b_dw, ln_g, ln_b):
    L = proj.shape[0]
    C = w_dw.shape[1]
    tm = _tm(L)
    hb = tm // HALO

    def body(a_ref, g_ref, ah_ref, gh_ref, w_ref, b_ref, lg_ref, lb_ref, vs_ref, vc_ref, buf, shifted):
        i = pl.program_id(0)
        halo = ah_ref[...] * _sigmoid(gh_ref[...])
        buf[0:HALO, :] = halo * jnp.where(i > 0, 1.0, 0.0)
        buf[HALO:HALO + tm, :] = a_ref[...] * _sigmoid(g_ref[...])
        _shifted_copies(buf, shifted, tm)
        acc = jnp.zeros((tm, C), F32) + b_ref[...]
        for k in range(CONV_K):
            acc = acc + w_ref[k:k + 1, :] * _window(buf, shifted, HALO - (CONV_K - 1) + k, tm)
        vc_ref[...] = acc
        mu = jnp.mean(acc, axis=-1, keepdims=True)
        d = acc - mu
        var = jnp.mean(d * d, axis=-1, keepdims=True)
        ln = d * lax.rsqrt(var + EPS) * lg_ref[...] + lb_ref[...]
        vs_ref[...] = (ln * _sigmoid(ln)).astype(BF16)

    prev = lambda cb: pl.BlockSpec((HALO, C), lambda i: (jnp.maximum(i * hb - 1, 0), cb))
    return pl.pallas_call(
        body, name="conv_fwd", grid=(L // tm,),
        in_specs=[_row(C, 0, tm), _row(C, 1, tm), prev(0), prev(1),
                  pl.BlockSpec((HALO, C), lambda i: (0, 0)), _vec(C), _vec(C), _vec(C)],
        out_specs=(_row(C, tm=tm), _row(C, tm=tm)),
        out_shape=(jax.ShapeDtypeStruct((L, C), BF16), jax.ShapeDtypeStruct((L, C), F32)),
        scratch_shapes=[pltpu.VMEM((HALO + tm, C), F32), pltpu.VMEM((7, HALO + tm - 8, C), F32)],
        compiler_params=_params(("parallel",)),
    )(proj, proj, proj, proj, w_dw, b_dw, ln_g, ln_b)


def _gelu(v):
    return 0.5 * v * (1.0 + jnp.tanh(math.sqrt(2.0 / math.pi) * (v + 0.044715 * v * v * v)))


def _gelu_grad(v):
    k = math.sqrt(2.0 / math.pi)
    t = jnp.tanh(k * (v + 0.044715 * v * v * v))
    return 0.5 * (1.0 + t) + 0.5 * v * (1.0 - t * t) * k * (1.0 + 3.0 * 0.044715 * v * v)


def s5_param_fn(ar, ai, ldt, br, bi, expand):
    dt = jnp.exp(ldt)
    er = jnp.exp(ar * dt)
    th = ai * dt
    lbr, lbi = er * jnp.cos(th), er * jnp.sin(th)
    nr, ni = lbr - 1.0, lbi
    den = ar * ar + ai * ai
    qr, qi = (nr * ar + ni * ai) / den, (ni * ar - nr * ai) / den
    qre = jnp.dot(expand, qr, precision=HI, preferred_element_type=F32)
    qie = jnp.dot(expand, qi, precision=HI, preferred_element_type=F32)
    return lbr, lbi, qre * br - qie * bi, qre * bi + qie * br


def s5_params(ar, ai, ldt, br2, bi2, expand):
    def body(ar_ref, ai_ref, ld_ref, br_ref, bi_ref, e_ref, o1, o2, o3, o4):
        r = s5_param_fn(ar_ref[...], ai_ref[...], ld_ref[...], br_ref[...], bi_ref[...], e_ref[...])
        o1[...], o2[...], o3[...], o4[...] = r

    s2, s3 = jax.ShapeDtypeStruct(ar.shape, F32), jax.ShapeDtypeStruct(br2.shape, F32)
    return pl.pallas_call(body, name="s5_params", out_shape=(s2, s2, s3, s3), compiler_params=_params())(
        ar, ai, ldt, br2, bi2, expand)


def s5_params_bwd(ar, ai, ldt, br2, bi2, expand, dlr, dli, dbr, dbi):
    def body(ar_ref, ai_ref, ld_ref, br_ref, bi_ref, e_ref, c1, c2, c3, c4, o1, o2, o3, o4, o5):
        e = e_ref[...]
        fn = lambda a, b, c, d, f: s5_param_fn(a, b, c, d, f, e)
        _, vjp = jax.vjp(fn, ar_ref[...], ai_ref[...], ld_ref[...], br_ref[...], bi_ref[...])
        r = vjp((c1[...], c2[...], c3[...], c4[...]))
        o1[...], o2[...], o3[...], o4[...], o5[...] = r

    shapes = tuple(jax.ShapeDtypeStruct(v.shape, F32) for v in (ar, ai, ldt, br2, bi2))
    return pl.pallas_call(body, name="s5_params_bwd", out_shape=shapes, compiler_params=_params())(
        ar, ai, ldt, br2, bi2, expand, dlr, dli, dbr, dbi)


def s5_tables(lr, li):
    C = lr.shape[1]

    def body(lr_ref, li_ref, o_ref):
        row = lax.broadcasted_iota(jnp.int32, (8, C), 0)
        for rev in (0, 1):
            pr = jnp.broadcast_to(lr_ref[...], (8, C))
            pi = jnp.broadcast_to(-li_ref[...] if rev else li_ref[...], (8, C))
            br, bi = pr, pi
            pows = [(pr, pi)]
            for _ in range(7):
                pr, pi = pr * br - pi * bi, pr * bi + pi * br
                pows.append((pr, pi))
            base = 8 * rev
            for s, d in enumerate((1, 2, 4)):
                keep = (row + d <= 7) if rev else (row >= d)
                o_ref[base + 2 * s] = jnp.where(keep, pows[d - 1][0], 0.0)
                o_ref[base + 2 * s + 1] = jnp.where(keep, pows[d - 1][1], 0.0)
            cr, ci = jnp.zeros((8, C), F32), jnp.zeros((8, C), F32)
            for j in range(8):
                e = (8 - j) if rev else (j + 1)
                cr = jnp.where(row == j, pows[e - 1][0], cr)
                ci = jnp.where(row == j, pows[e - 1][1], ci)
            o_ref[base + 6] = cr
            o_ref[base + 7] = ci

    return pl.pallas_call(body, name="s5_tables", out_shape=jax.ShapeDtypeStruct((16, 8, C), F32),
                          compiler_params=_params())(lr, li)


def _tile_steps(xr, xi, tabs, rev):
    for s, d in enumerate((1, 2, 4)):
        tr, ti = tabs[2 * s], tabs[2 * s + 1]
        sh = (8 - d) if rev else d
        sr, si = pltpu.roll(xr, sh, 0), pltpu.roll(xi, sh, 0)
        xr, xi = xr + tr * sr - ti * si, xi + tr * si + ti * sr
    return xr, xi


def _tile_carry(xr, xi, tabs, cr, ci):
    tr, ti = tabs[6], tabs[7]
    return xr + tr * cr - ti * ci, xi + tr * ci + ti * cr


def _hi_lo(a):
    hi = a.astype(BF16)
    return hi, (a - hi.astype(F32)).astype(BF16)


def _lhs3(a):
    hi, lo = _hi_lo(a)
    return jnp.concatenate([hi, lo, hi], axis=1)


def _rhs3(m):
    hi, lo = _hi_lo(m)
    return jnp.concatenate([hi, hi, lo], axis=-2)


def s5_fwd(proj, col0, bdr3, bdi3, cd2, tabs, d_skip):
    L = proj.shape[0]
    nb, cw3, sw = bdr3.shape
    cw = cw3 // 3
    tl = min(S5_ROWS, L)
    cb0 = col0 // cw

    def body(u_ref, bdr_ref, bdi_ref, cd_ref, t_ref, dk_ref, sr_ref, si_ref, yp_ref, yg_ref, car):
        l = pl.program_id(1)

        @pl.when(l == 0)
        def _():
            car[...] = jnp.zeros_like(car)

        u = u_ref[...]
        u3 = _lhs3(u)
        sr_ref[...] = jnp.dot(u3, bdr_ref[...], preferred_element_type=F32)
        si_ref[...] = jnp.dot(u3, bdi_ref[...], preferred_element_type=F32)

        def pair(i, c):
            tabs = [t_ref[j] for j in range(8)]
            r0 = pl.multiple_of(i * 16, 16)
            lo, hi = pl.ds(r0, 8), pl.ds(r0 + 8, 8)
            a = _tile_steps(sr_ref[lo, :], si_ref[lo, :], tabs, False)
            b = _tile_steps(sr_ref[hi, :], si_ref[hi, :], tabs, False)
            ar, ai = _tile_carry(a[0], a[1], tabs, c[0], c[1])
            br, bi = _tile_carry(b[0], b[1], tabs, ar[7:8, :], ai[7:8, :])
            sr_ref[lo, :], si_ref[lo, :] = ar, ai
            sr_ref[hi, :], si_ref[hi, :] = br, bi
            return br[7:8, :], bi[7:8, :]

        c = lax.fori_loop(0, tl // 16, pair, (car[0:1, :], car[1:2, :]))
        car[0:1, :] = c[0]
        car[1:2, :] = c[1]
        s2 = jnp.concatenate([sr_ref[...].astype(BF16), si_ref[...].astype(BF16)], axis=1)
        y = jnp.dot(s2, cd_ref[...], preferred_element_type=F32) + dk_ref[...] * u
        yp_ref[...] = y
        yg_ref[...] = _gelu(y).astype(BF16)

    blk = lambda r, c: pl.BlockSpec((None, r, c), lambda b, l: (b, 0, 0))
    return pl.pallas_call(
        body, name="s5_fwd", grid=(nb, L // tl),
        in_specs=[pl.BlockSpec((tl, cw), lambda b, l: (l, cb0 + b)), blk(cw3, sw), blk(cw3, sw), blk(2 * sw, cw),
                  pl.BlockSpec((8, 8, sw), lambda b, l: (0, 0, b)), pl.BlockSpec((1, cw), lambda b, l: (0, b))],
        out_specs=(pl.BlockSpec((tl, sw), lambda b, l: (l, b)), pl.BlockSpec((tl, sw), lambda b, l: (l, b)),
                   pl.BlockSpec((tl, cw), lambda b, l: (l, b)), pl.BlockSpec((tl, cw), lambda b, l: (l, b))),
        out_shape=(jax.ShapeDtypeStruct((L, nb * sw), F32), jax.ShapeDtypeStruct((L, nb * sw), F32),
                   jax.ShapeDtypeStruct((L, nb * cw), F32), jax.ShapeDtypeStruct((L, nb * cw), BF16)),
        scratch_shapes=[pltpu.VMEM((8, sw), F32)],
        compiler_params=_params(("parallel", "arbitrary")),
    )(proj, bdr3, bdi3, cd2, tabs, d_skip)


def s5_bwd(dyg_a, dyg_b, yp, proj, col0, s_re, s_im, bdt2, cdrt3, cdit3, tabs, d_skip):
    L = proj.shape[0]
    nb, sw2, cw = bdt2.shape
    sw = sw2 // 2
    tl = min(S5_ROWS, L)
    nl = L // tl
    cb0 = col0 // cw
    tb = tl // 8

    def body(da_ref, db_ref, yp_ref, u_ref, sr_ref, si_ref, hr_ref, hi_ref, bdt_ref, cdrt_ref, cdit_ref,
             t_ref, dk_ref, du_ref, ddk_ref, dcr_ref, dci_ref, dbr_ref, dbi_ref, dlr_ref, dli_ref,
             gr, gi, pr, pi, car):
        l = pl.program_id(1)
        first = l == nl - 1

        @pl.when(l == 0)
        def _():
            car[...] = jnp.zeros_like(car)

        u = u_ref[...]
        dy = (da_ref[...].astype(F32) + db_ref[...].astype(F32)) * _gelu_grad(yp_ref[...])
        dy3 = _lhs3(dy)
        gr[...] = jnp.dot(dy3, cdrt_ref[...], preferred_element_type=F32)
        gi[...] = jnp.dot(dy3, cdit_ref[...], preferred_element_type=F32)
        inner = jnp.where(first, 0.0, 1.0)
        pr[0:8, :] = hr_ref[...] * inner
        pi[0:8, :] = hi_ref[...] * inner
        pr[8:8 + tl, :] = sr_ref[...]
        pi[8:8 + tl, :] = si_ref[...]
        row = lax.broadcasted_iota(jnp.int32, (8, sw), 0)

        def pair(j, c):
            tabs = [t_ref[8 + k] for k in range(8)]
            r0 = pl.multiple_of((tb // 2 - 1 - j) * 16, 16)
            lo, hi = pl.ds(r0, 8), pl.ds(r0 + 8, 8)
            b = _tile_steps(gr[hi, :], gi[hi, :], tabs, True)
            a = _tile_steps(gr[lo, :], gi[lo, :], tabs, True)
            br, bi = _tile_carry(b[0], b[1], tabs, c[0], c[1])
            ar, ai = _tile_carry(a[0], a[1], tabs, br[0:1, :], bi[0:1, :])
            gr[lo, :], gi[lo, :] = ar, ai
            gr[hi, :], gi[hi, :] = br, bi
            p0r, p1r, p2r = [pltpu.roll(pr[pl.ds(r0 + 8 * n, 8), :], 1, 0) for n in range(3)]
            p0i, p1i, p2i = [pltpu.roll(pi[pl.ds(r0 + 8 * n, 8), :], 1, 0) for n in range(3)]
            qar, qai = jnp.where(row == 0, p0r, p1r), jnp.where(row == 0, p0i, p1i)
            qbr, qbi = jnp.where(row == 0, p1r, p2r), jnp.where(row == 0, p1i, p2i)
            return (ar[0:1, :], ai[0:1, :], c[2] + (ar * qar + ai * qai) + (br * qbr + bi * qbi),
                    c[3] + (ai * qar - ar * qai) + (bi * qbr - br * qbi))

        z = jnp.zeros((8, sw), F32)
        c = lax.fori_loop(0, tb // 2, pair, (car[0:1, :], car[1:2, :], z, z))
        car[0:1, :] = c[0]
        car[1:2, :] = c[1]
        g_re, g_im = gr[...].astype(BF16), gi[...].astype(BF16)
        g2 = jnp.concatenate([g_re, g_im], axis=1)
        du_ref[...] = (dy * dk_ref[...] + jnp.dot(g2, bdt_ref[...], preferred_element_type=F32)).astype(BF16)
        tn = (((0,), (0,)), ((), ()))
        dyb, ub = dy.astype(BF16), u.astype(BF16)
        _accum(ddk_ref, _colsum(dy * u), l)
        _accum(dcr_ref, lax.dot_general(dyb, sr_ref[...].astype(BF16), tn, preferred_element_type=F32), l)
        _accum(dci_ref, -lax.dot_general(dyb, si_ref[...].astype(BF16), tn, preferred_element_type=F32), l)
        _accum(dbr_ref, lax.dot_general(ub, g_re, tn, preferred_element_type=F32), l)
        _accum(dbi_ref, lax.dot_general(ub, g_im, tn, preferred_element_type=F32), l)
        _accum(dlr_ref, c[2], l)
        _accum(dli_ref, c[3], l)

    rl = lambda l: nl - 1 - l
    cblk = lambda w, off=0: pl.BlockSpec((tl, w), lambda b, l: (rl(l), off + b))
    halo = pl.BlockSpec((8, sw), lambda b, l: (jnp.maximum(rl(l) * tb - 1, 0), b))
    mat = lambda r, c: pl.BlockSpec((None, r, c), lambda b, l: (b, 0, 0))
    return pl.pallas_call(
        body, name="s5_bwd", grid=(nb, nl),
        in_specs=[cblk(cw), cblk(cw), cblk(cw), cblk(cw, cb0), cblk(sw), cblk(sw), halo, halo,
                  mat(2 * sw, cw), mat(3 * cw, sw), mat(3 * cw, sw),
                  pl.BlockSpec((16, 8, sw), lambda b, l: (0, 0, b)), pl.BlockSpec((1, cw), lambda b, l: (0, b))],
        out_specs=(cblk(cw), pl.BlockSpec((1, cw), lambda b, l: (0, b)), mat(cw, sw), mat(cw, sw), mat(cw, sw), mat(cw, sw),
                   pl.BlockSpec((8, sw), lambda b, l: (0, b)), pl.BlockSpec((8, sw), lambda b, l: (0, b))),
        out_shape=(jax.ShapeDtypeStruct((L, nb * cw), BF16), jax.ShapeDtypeStruct((1, nb * cw), F32),
                   jax.ShapeDtypeStruct((nb, cw, sw), F32), jax.ShapeDtypeStruct((nb, cw, sw), F32),
                   jax.ShapeDtypeStruct((nb, cw, sw), F32), jax.ShapeDtypeStruct((nb, cw, sw), F32),
                   jax.ShapeDtypeStruct((8, nb * sw), F32), jax.ShapeDtypeStruct((8, nb * sw), F32)),
        scratch_shapes=[pltpu.VMEM((tl, sw), F32), pltpu.VMEM((tl, sw), F32),
                        pltpu.VMEM((tl + 8, sw), F32), pltpu.VMEM((tl + 8, sw), F32), pltpu.VMEM((8, sw), F32)],
        compiler_params=_params(("parallel", "arbitrary")),
    )(dyg_a, dyg_b, yp, proj, s_re, s_im, s_re, s_im, bdt2, cdrt3, cdit3, tabs, d_skip)


def merge_fwd(proj, col_gc, y_conv, ga, gb):
    L, D = y_conv.shape
    tm = _tm(L)
    h = D // 2
    c0 = col_gc // h

    def body(p0, p1, p2, p3, yc_ref, ga_ref, gb_ref, o_ref):
        gc, gs = (p0, p1), (p2, p3)
        for s in range(2):
            cols = slice(s * h, (s + 1) * h)
            y_ssm = ga_ref[:, cols].astype(F32) * _sigmoid(gb_ref[:, cols].astype(F32))
            o_ref[:, cols] = (_sigmoid(gc[s][...]) * yc_ref[:, cols].astype(F32)
                              + _sigmoid(gs[s][...]) * y_ssm).astype(BF16)

    return pl.pallas_call(
        body, name="merge_fwd", grid=(L // tm,),
        in_specs=[_row(h, c0 + s, tm) for s in range(4)] + [_row(D, tm=tm)] * 3,
        out_specs=_row(D, tm=tm), out_shape=jax.ShapeDtypeStruct((L, D), BF16),
        compiler_params=_params(("parallel",)),
    )(proj, proj, proj, proj, y_conv, ga, gb)


def residual_norm(x, m_out, gate, g, scale, shift):
    L, D = x.shape
    tm = _tm(L)

    def body(x_ref, m_ref, gt_ref, g_ref, sc_ref, sh_ref, h_ref, z_ref):
        h = x_ref[...] + gt_ref[...] * m_ref[...]
        h_ref[...] = h
        z_ref[...] = (h * _rms(h) * g_ref[...] * (1.0 + sc_ref[...]) + sh_ref[...]).astype(BF16)

    return pl.pallas_call(
        body, name="residual_norm", grid=(L // tm,),
        in_specs=[_row(D, tm=tm), _row(D, tm=tm), _vec(D), _vec(D), _vec(D), _vec(D)],
        out_specs=(_row(D, tm=tm), _row(D, tm=tm)),
        out_shape=(jax.ShapeDtypeStruct((L, D), F32), jax.ShapeDtypeStruct((L, D), BF16)),
        compiler_params=_params(("parallel",)),
    )(x, m_out, gate, g, scale, shift)


def loss_bwd(h1, ff, gate2, final_g, target):
    L, D = h1.shape
    tm = _tm(L)

    def body(h_ref, f_ref, gt_ref, g_ref, t_ref, dh_ref, dff_ref, loss_ref, dg_ref, dgt_ref):
        i = pl.program_id(0)
        ffv = f_ref[...]
        h2 = h_ref[...] + gt_ref[...] * ffv
        r = _rms(h2)
        n = h2 * r
        err = n * g_ref[...] - t_ref[...]
        per_tok = jnp.mean(err * err, axis=-1, keepdims=True)
        _accum(loss_ref, 0.5 * jnp.sum(per_tok, axis=0, keepdims=True), i)
        dy = err * (1.0 / D)
        _accum(dg_ref, _colsum(dy * n), i)
        dn = dy * g_ref[...]
        dh2 = r * (dn - n * jnp.mean(dn * n, axis=-1, keepdims=True))
        dh_ref[...] = dh2
        dff_ref[...] = (gt_ref[...] * dh2).astype(BF16)
        _accum(dgt_ref, _colsum(dh2 * ffv), i)

    return pl.pallas_call(
        body, name="loss_bwd", grid=(L // tm,),
        in_specs=[_row(D, tm=tm), _row(D, tm=tm), _vec(D), _vec(D), _row(D, tm=tm)],
        out_specs=(_row(D, tm=tm), _row(D, tm=tm), pl.BlockSpec((1, 1), lambda i: (0, 0)), _vec(D), _vec(D)),
        out_shape=(jax.ShapeDtypeStruct((L, D), F32), jax.ShapeDtypeStruct((L, D), BF16),
                   jax.ShapeDtypeStruct((1, 1), F32), jax.ShapeDtypeStruct((1, D), F32), jax.ShapeDtypeStruct((1, D), F32)),
        compiler_params=_params(("arbitrary",)),
    )(h1, ff, gate2, final_g, target)


def norm_bwd(dz, h, dh_in, g, scale, name, gate=None, m_out=None):
    L, D = h.shape
    tm = _tm(L)
    tail = gate is not None

    def body(*refs):
        dz_ref, h_ref, di_ref, g_ref, sc_ref = refs[:5]
        rest = refs[5:]
        if tail:
            gt_ref, m_ref = rest[:2]
            rest = rest[2:]
        dh_ref, dsc_ref, dsh_ref, dg_ref = rest[:4]
        i = pl.program_id(0)
        hv, dzv = h_ref[...], dz_ref[...].astype(F32)
        r = _rms(hv)
        n = hv * r
        _accum(dsc_ref, _colsum(dzv * n * g_ref[...]), i)
        _accum(dsh_ref, _colsum(dzv), i)
        dzn = dzv * (1.0 + sc_ref[...])
        _accum(dg_ref, _colsum(dzn * n), i)
        dn = dzn * g_ref[...]
        dh = di_ref[...] + r * (dn - n * jnp.mean(dn * n, axis=-1, keepdims=True))
        dh_ref[...] = dh
        if tail:
            dmo_ref, dgt_ref = rest[4:]
            dmo_ref[...] = (gt_ref[...] * dh).astype(BF16)
            _accum(dgt_ref, _colsum(dh * m_ref[...]), i)

    ins = [dz, h, dh_in, g, scale]
    in_specs = [_row(D, tm=tm)] * 3 + [_vec(D)] * 2
    out_specs = [_row(D, tm=tm), _vec(D), _vec(D), _vec(D)]
    out_shape = [jax.ShapeDtypeStruct((L, D), F32)] + [jax.ShapeDtypeStruct((1, D), F32)] * 3
    if tail:
        ins += [gate, m_out]
        in_specs += [_vec(D), _row(D, tm=tm)]
        out_specs += [_row(D, tm=tm), _vec(D)]
        out_shape += [jax.ShapeDtypeStruct((L, D), BF16), jax.ShapeDtypeStruct((1, D), F32)]
    return pl.pallas_call(
        body, name=name, grid=(L // tm,), in_specs=in_specs, out_specs=tuple(out_specs), out_shape=tuple(out_shape),
        compiler_params=_params(("arbitrary",)),
    )(*ins)


def merge_bwd(dmerged, proj, col_gc, y_conv, ga, gb):
    L, D = y_conv.shape
    tm = _tm(L)
    h = D // 2
    c0 = col_gc // h

    def body(dm_ref, p0, p1, p2, p3, yc_ref, ga_ref, gb_ref, dyc_ref, dga_ref, dgb_ref, dg_ref):
        gc, gs = (p0, p1), (p2, p3)
        for s in range(2):
            cols = slice(s * h, (s + 1) * h)
            dm = dm_ref[:, cols].astype(F32)
            sc, ss, sb = _sigmoid(gc[s][...]), _sigmoid(gs[s][...]), _sigmoid(gb_ref[:, cols].astype(F32))
            gav = ga_ref[:, cols].astype(F32)
            dyc_ref[:, cols] = (dm * sc).astype(BF16)
            dg_ref[:, cols] = (dm * yc_ref[:, cols].astype(F32) * sc * (1.0 - sc)).astype(BF16)
            dg_ref[:, D + s * h:D + (s + 1) * h] = (dm * gav * sb * ss * (1.0 - ss)).astype(BF16)
            dys = dm * ss
            dga_ref[:, cols] = (dys * sb).astype(BF16)
            dgb_ref[:, cols] = (dys * gav * sb * (1.0 - sb)).astype(BF16)

    return pl.pallas_call(
        body, name="merge_bwd", grid=(L // tm,),
        in_specs=[_row(D, tm=tm)] + [_row(h, c0 + s, tm) for s in range(4)] + [_row(D, tm=tm)] * 3,
        out_specs=(_row(D, tm=tm), _row(D, tm=tm), _row(D, tm=tm), _row(2 * D, tm=tm)),
        out_shape=(jax.ShapeDtypeStruct((L, D), BF16),) * 3 + (jax.ShapeDtypeStruct((L, 2 * D), BF16),),
        compiler_params=_params(("parallel",)),
    )(dmerged, proj, proj, proj, proj, y_conv, ga, gb)


def conv_ln_bwd(dvs, vc, ln_g, ln_b):
    L, C = vc.shape
    tm = _tm(L)

    def body(d_ref, v_ref, g_ref, b_ref, o_ref, dg_ref, db_ref):
        i = pl.program_id(0)
        v = v_ref[...]
        mu = jnp.mean(v, axis=-1, keepdims=True)
        d = v - mu
        rstd = lax.rsqrt(jnp.mean(d * d, axis=-1, keepdims=True) + EPS)
        xh = d * rstd
        ln = xh * g_ref[...] + b_ref[...]
        sg = _sigmoid(ln)
        dln = d_ref[...].astype(F32) * sg * (1.0 + ln * (1.0 - sg))
        _accum(dg_ref, _colsum(dln * xh), i)
        _accum(db_ref, _colsum(dln), i)
        dxh = dln * g_ref[...]
        o_ref[...] = rstd * (dxh - jnp.mean(dxh, axis=-1, keepdims=True)
                             - xh * jnp.mean(dxh * xh, axis=-1, keepdims=True))

    return pl.pallas_call(
        body, name="conv_ln_bwd", grid=(L // tm,),
        in_specs=[_row(C, tm=tm), _row(C, tm=tm), _vec(C), _vec(C)],
        out_specs=(_row(C, tm=tm), _vec(C), _vec(C)),
        out_shape=(jax.ShapeDtypeStruct((L, C), F32), jax.ShapeDtypeStruct((1, C), F32), jax.ShapeDtypeStruct((1, C), F32)),
        compiler_params=_params(("arbitrary",)),
    )(dvs, vc, ln_g, ln_b)


def conv_bwd(dvc, proj, w_dw):
    L, C = dvc.shape
    tm = _tm(L)
    hb = tm // HALO
    last = L // HALO - 1
    nt = L // tm

    def body(d_ref, dn_ref, a_ref, g_ref, ah_ref, gh_ref, w_ref, o_ref, dw_ref, db_ref, dbuf, vbuf, dsh, vsh, dw8):
        i = pl.program_id(0)
        dcur = d_ref[...]
        dbuf[0:tm, :] = dcur
        dbuf[tm:tm + HALO, :] = dn_ref[...] * jnp.where(i < nt - 1, 1.0, 0.0)
        av, sg = a_ref[...], _sigmoid(g_ref[...])
        vbuf[0:HALO, :] = ah_ref[...] * _sigmoid(gh_ref[...]) * jnp.where(i > 0, 1.0, 0.0)
        vbuf[HALO:HALO + tm, :] = av * sg
        _shifted_copies(dbuf, dsh, tm)
        _shifted_copies(vbuf, vsh, tm)
        dv = jnp.zeros((tm, C), F32)
        for k in range(CONV_K):
            dv = dv + w_ref[k:k + 1, :] * _window(dbuf, dsh, CONV_K - 1 - k, tm)
        o_ref[:, 0:C] = (dv * sg).astype(BF16)
        o_ref[:, C:2 * C] = (dv * av * sg * (1.0 - sg)).astype(BF16)

        @pl.when(i == 0)
        def _():
            dw8[...] = jnp.zeros_like(dw8)

        for k in range(CONV_K):
            prod = dcur * _window(vbuf, vsh, HALO - (CONV_K - 1) + k, tm)
            part = prod[0:8, :]
            for j in range(1, tm // 8):
                part = part + prod[8 * j:8 * j + 8, :]
            dw8[k] += part
        _accum(db_ref, _colsum(dcur), i)

        @pl.when(i == nt - 1)
        def _():
            for k in range(CONV_K):
                dw_ref[k:k + 1, :] = _colsum(dw8[k])
            dw_ref[CONV_K:HALO, :] = jnp.zeros((HALO - CONV_K, C), F32)

    prev = lambda cb: pl.BlockSpec((HALO, C), lambda i: (jnp.maximum(i * hb - 1, 0), cb))
    return pl.pallas_call(
        body, name="conv_bwd", grid=(nt,),
        in_specs=[_row(C, tm=tm), pl.BlockSpec((HALO, C), lambda i: (jnp.minimum((i + 1) * hb, last), 0)),
                  _row(C, 0, tm), _row(C, 1, tm), prev(0), prev(1), pl.BlockSpec((HALO, C), lambda i: (0, 0))],
        out_specs=(_row(2 * C, tm=tm), pl.BlockSpec((HALO, C), lambda i: (0, 0)), _vec(C)),
        out_shape=(jax.ShapeDtypeStruct((L, 2 * C), BF16), jax.ShapeDtypeStruct((HALO, C), F32),
                   jax.ShapeDtypeStruct((1, C), F32)),
        scratch_shapes=[pltpu.VMEM((tm + HALO, C), F32), pltpu.VMEM((HALO + tm, C), F32),
                        pltpu.VMEM((7, HALO + tm - 8, C), F32), pltpu.VMEM((7, HALO + tm - 8, C), F32),
                        pltpu.VMEM((CONV_K, 8, C), F32)],
        compiler_params=_params(("arbitrary",)),
    )(dvc, dvc, proj, proj, proj, proj, w_dw)


def _adamw(w, g, m, v):
    m = ADAM_B1 * m + (1.0 - ADAM_B1) * g
    v = ADAM_B2 * v + (1.0 - ADAM_B2) * (g * g)
    m_hat = m / (1.0 - ADAM_B1 ** ADAM_STEP)
    v_hat = v / (1.0 - ADAM_B2 ** ADAM_STEP)
    delta = -ADAM_LR * (m_hat / (jnp.sqrt(v_hat) + ADAM_EPS) + ADAM_WD * w)
    return delta, m, v


def _tile_rows(R, C):
    tr = 8
    while tr * 2 * C <= 128 * 1024 and R % (tr * 2) == 0:
        tr *= 2
    assert R % tr == 0, (R, C)
    return tr


def sum_devices(parts, name):
    _, R, C = parts.shape
    tr = _tile_rows(R, C)

    def body(p_ref, o_ref):
        s = p_ref[0]
        for j in range(1, NDEV):
            s = s + p_ref[j]
        o_ref[...] = s

    return pl.pallas_call(
        body, name=name, grid=(R // tr,),
        in_specs=[pl.BlockSpec((NDEV, tr, C), lambda i: (0, i, 0))],
        out_specs=pl.BlockSpec((tr, C), lambda i: (i, 0)), out_shape=jax.ShapeDtypeStruct((R, C), F32),
        compiler_params=_params(("parallel",)),
    )(parts)


def adam_update(w, g, m, v, name):
    R, C = w.shape
    tr = _tile_rows(R, C)

    def body(w_ref, g_ref, m_ref, v_ref, d_ref, mo_ref, vo_ref):
        d, mm, vv = _adamw(w_ref[...], g_ref[...], m_ref[...], v_ref[...])
        d_ref[...], mo_ref[...], vo_ref[...] = d, mm, vv

    spec = pl.BlockSpec((tr, C), lambda i: (i, 0))
    return pl.pallas_call(
        body, name=name, grid=(R // tr,), in_specs=[spec] * 4, out_specs=(spec,) * 3,
        out_shape=(jax.ShapeDtypeStruct((R, C), F32),) * 3, compiler_params=_params(("parallel",)),
    )(w, g, m, v)


def adam_many(items, name):
    n = len(items)

    def body(*refs):
        ins, outs = refs[:4 * n], refs[4 * n:]
        for i in range(n):
            w, g, m, v = [ins[4 * i + j][...] for j in range(4)]
            outs[3 * i][...], outs[3 * i + 1][...], outs[3 * i + 2][...] = _adamw(w, g, m, v)

    out_shape = [jax.ShapeDtypeStruct(w.shape, F32) for w, _, _, _ in items for _ in range(3)]
    res = pl.pallas_call(body, name=name, out_shape=tuple(out_shape), compiler_params=_params())(
        *[a for it in items for a in it])
    return [tuple(res[3 * i:3 * i + 3]) for i in range(n)]


def adam_reduce(parts, w, m, v, name):
    R, C = w.shape
    tr = _tile_rows(R, C)

    def body(p_ref, w_ref, m_ref, v_ref, g_ref, d_ref, mo_ref, vo_ref):
        g = p_ref[0].astype(F32)
        for j in range(1, NDEV):
            g = g + p_ref[j].astype(F32)
        g_ref[...] = g
        d, mm, vv = _adamw(w_ref[...], g, m_ref[...], v_ref[...])
        d_ref[...], mo_ref[...], vo_ref[...] = d, mm, vv

    spec = pl.BlockSpec((tr, C), lambda i: (i, 0))
    return pl.pallas_call(
        body, name=name, grid=(R // tr,),
        in_specs=[pl.BlockSpec((NDEV, tr, C), lambda i: (0, i, 0)), spec, spec, spec], out_specs=(spec,) * 4,
        out_shape=(jax.ShapeDtypeStruct((R, C), F32),) * 4, compiler_params=_params(("parallel",)),
    )(parts, w, m, v)


def adam_w_ada(c_act, dmod_cols, w, m, v):
    D, n = w.shape
    tn = 256

    def body(c_ref, dm_ref, w_ref, m_ref, v_ref, g_ref, d_ref, mo_ref, vo_ref):
        g = lax.dot_general(c_ref[...].astype(BF16), dm_ref[...].astype(BF16), (((0,), (0,)), ((), ())),
                            preferred_element_type=F32)
        g_ref[...] = g
        d, mm, vv = _adamw(w_ref[...], g, m_ref[...], v_ref[...])
        d_ref[...], mo_ref[...], vo_ref[...] = d, mm, vv

    spec = pl.BlockSpec((D, tn), lambda j: (0, j))
    return pl.pallas_call(
        body, name="adam_w_ada", grid=(n // tn,),
        in_specs=[pl.BlockSpec((NDEV, D), lambda j: (0, 0)), pl.BlockSpec((NDEV, tn), lambda j: (0, j)), spec, spec, spec],
        out_specs=(spec,) * 4, out_shape=(jax.ShapeDtypeStruct((D, n), F32),) * 4,
        compiler_params=_params(("parallel",)),
    )(c_act, dmod_cols, w, m, v)


def _block_diag(m):
    G, a, b = m.shape
    m4 = m.reshape(G // GB, GB, a, b)
    eye = jnp.eye(GB, dtype=m.dtype)
    return (m4[:, :, :, None, :] * eye[None, :, None, :, None]).reshape(G // GB, GB * a, GB * b)


def _diag_blocks(m, a, b):
    nb = m.shape[0]
    m5 = m.reshape(nb, GB, a, GB, b)
    idx = jnp.arange(GB)
    return m5[:, idx, :, idx, :].transpose(1, 0, 2, 3).reshape(nb * GB, a, b)


def _flat_pad(parts, mult):
    flat = jnp.concatenate([p.reshape(-1) for p in parts])
    pad = (-flat.shape[0]) % mult
    return jnp.pad(flat, (0, pad))


def _split(flat, like):
    out, off = [], 0
    for p in like:
        out.append(flat[off:off + p.size].reshape(p.shape))
        off += p.size
    return out


def kernel(x, c, w_ada, b_ada, norm1_g, w_in, w_dw, b_dw, ln_g, ln_b, w_conv_out, a_re, a_im, log_dt, b_re, b_im, c_re, c_im, d_skip, w_glu_a, w_glu_b, w_out, norm2_g, w_ff1, w_ff2, final_g, loss_target, m_w_ada, m_b_ada, m_norm1_g, m_w_in, m_w_dw, m_b_dw, m_ln_g, m_ln_b, m_w_conv_out, m_a_re, m_a_im, m_log_dt, m_b_re, m_b_im, m_c_re, m_c_im, m_d_skip, m_w_glu_a, m_w_glu_b, m_w_out, m_norm2_g, m_w_ff1, m_w_ff2, m_final_g, v_w_ada, v_b_ada, v_norm1_g, v_w_in, v_w_dw, v_b_dw, v_ln_g, v_ln_b, v_w_conv_out, v_a_re, v_a_im, v_log_dt, v_b_re, v_b_im, v_c_re, v_c_im, v_d_skip, v_w_glu_a, v_w_glu_b, v_w_out, v_norm2_g, v_w_ff1, v_w_ff2, v_final_g):
    W = dict(w_ada=w_ada, b_ada=b_ada, norm1_g=norm1_g, w_in=w_in, w_dw=w_dw, b_dw=b_dw, ln_g=ln_g, ln_b=ln_b,
             w_conv_out=w_conv_out, a_re=a_re, a_im=a_im, log_dt=log_dt, b_re=b_re, b_im=b_im, c_re=c_re, c_im=c_im,
             d_skip=d_skip, w_glu_a=w_glu_a, w_glu_b=w_glu_b, w_out=w_out, norm2_g=norm2_g, w_ff1=w_ff1, w_ff2=w_ff2,
             final_g=final_g)
    Mo = dict(w_ada=m_w_ada, b_ada=m_b_ada, norm1_g=m_norm1_g, w_in=m_w_in, w_dw=m_w_dw, b_dw=m_b_dw, ln_g=m_ln_g,
              ln_b=m_ln_b, w_conv_out=m_w_conv_out, a_re=m_a_re, a_im=m_a_im, log_dt=m_log_dt, b_re=m_b_re, b_im=m_b_im,
              c_re=m_c_re, c_im=m_c_im, d_skip=m_d_skip, w_glu_a=m_w_glu_a, w_glu_b=m_w_glu_b, w_out=m_w_out,
              norm2_g=m_norm2_g, w_ff1=m_w_ff1, w_ff2=m_w_ff2, final_g=m_final_g)
    Vo = dict(w_ada=v_w_ada, b_ada=v_b_ada, norm1_g=v_norm1_g, w_in=v_w_in, w_dw=v_w_dw, b_dw=v_b_dw, ln_g=v_ln_g,
              ln_b=v_ln_b, w_conv_out=v_w_conv_out, a_re=v_a_re, a_im=v_a_im, log_dt=v_log_dt, b_re=v_b_re, b_im=v_b_im,
              c_re=v_c_re, c_im=v_c_im, d_skip=v_d_skip, w_glu_a=v_w_glu_a, w_glu_b=v_w_glu_b, w_out=v_w_out,
              norm2_g=v_norm2_g, w_ff1=v_w_ff1, w_ff2=v_w_ff2, final_g=v_final_g)
    names = list(W)

    me = _me()
    xs, tgt = x[0], loss_target[0]
    L, D = xs.shape
    CW = w_dw.shape[2] * NDEV
    G, P = a_re.shape[1], a_re.shape[2]
    H = b_re.shape[3]
    n_ada = w_ada.shape[2]

    (c_all,) = _exchange([c], "gather_c", True)
    b_cols = lax.dynamic_slice(b_ada, (0, me * n_ada), (1, n_ada))
    mod_cols, c_act = adaln_mod(c_all.reshape(NDEV, D), w_ada[0], b_cols)
    (mod_all,) = _exchange([mod_cols], "gather_mod", True)
    mod = lax.dynamic_slice(mod_all, (0, me, 0), (NDEV, 1, n_ada)).reshape(6, 1, D)
    shift1, scale1, gate1, shift2, scale2, gate2 = [mod[j] for j in range(6)]

    big = ["w_in", "w_conv_out", "w_glu_a", "w_glu_b", "w_out", "w_ff1", "w_ff2"]
    order = ["w_in", "w_dw"] + big[1:]
    shards = {k: W[k][0].astype(BF16) for k in big}
    shards["w_dw"] = jnp.pad(w_dw[0], ((0, HALO - CONV_K), (0, 0)))
    gather_handle = dict(zip(order, gather2_start([shards[k] for k in order], "gather_weights_start", mod_all)))

    def forward(ks, name, after):
        gather_handle.update(zip(ks, gather2_forward([gather_handle[k] for k in ks], name, after)))

    def weight(k, after):
        w = gather2_wait(gather_handle[k], "gather_wait_" + k, after)
        if k in ("w_out", "w_ff2"):
            w = w.reshape(1, w.shape[0] * w.shape[1], w.shape[2])
        elif k in narrow:
            w = w.transpose(1, 0, 2).reshape(1, w.shape[1], NDEV * w.shape[2])
        elif k == "w_dw":
            w = w.transpose(1, 0, 2).reshape(HALO, CW)
        return w

    narrow = ("w_conv_out", "w_glu_a", "w_glu_b")
    scatter_handle = {}

    def scatter(k, g):
        if k in narrow:
            g = g.reshape(g.shape[1], NDEV, -1).transpose(1, 0, 2)
        elif g.shape[0] == 1:
            g = g.reshape(NDEV, -1, g.shape[2])
        (scatter_handle[k],), token = exchange_start([g], "scatter_start_" + k, False)
        return token

    big_out = {}

    def finish_weight(k, after):
        parts = exchange_wait(scatter_handle[k], after, "scatter_wait_" + k, False)
        big_out[k] = adam_reduce(parts, W[k][0], Mo[k][0], Vo[k][0], "adam_" + k)
        return big_out[k][1]

    u = prenorm(xs, norm1_g, scale1, shift1, "prenorm1")

    br2 = b_re[0].transpose(0, 2, 1).reshape(G * H, P)
    bi2 = b_im[0].transpose(0, 2, 1).reshape(G * H, P)
    ldt = log_dt[0].reshape(G, 1)
    expand = jnp.repeat(jnp.eye(G, dtype=F32), H, axis=0)
    lbr, lbi, bbr, bbi = s5_params(a_re[0], a_im[0], ldt, br2, bi2, expand)
    tabs = s5_tables(lbr.reshape(1, G * P), lbi.reshape(1, G * P))
    bdr, bdi = _block_diag(bbr.reshape(G, H, P)), _block_diag(bbi.reshape(G, H, P))
    cdr = _block_diag(c_re[0].transpose(0, 2, 1))
    cdi = _block_diag(c_im[0].transpose(0, 2, 1))
    cd2 = jnp.concatenate([cdr, -cdi], axis=1).astype(BF16)
    bdr3, bdi3 = _rhs3(bdr), _rhs3(bdi)
    bdt2 = jnp.concatenate([bdr.transpose(0, 2, 1), bdi.transpose(0, 2, 1)], axis=1).astype(BF16)
    cdrt3, cdit3 = _rhs3(cdr.transpose(0, 2, 1)), _rhs3(-cdi.transpose(0, 2, 1))

    forward(["w_in"], "gather_forward_in", (u, tabs, bdr3, bdi3, cd2, bdt2, cdrt3, cdit3))
    wg = {"w_in": weight("w_in", u)}
    proj = mm_nn(u, wg["w_in"], "in_proj")
    forward(["w_dw", "w_conv_out", "w_glu_a", "w_glu_b", "w_out"], "gather_forward_mix", proj)
    w_dw_full = weight("w_dw", proj)
    vs, vc = conv_fwd(proj, w_dw_full, b_dw, ln_g, ln_b)
    wg["w_conv_out"] = weight("w_conv_out", vs)
    y_conv = mm_nn(vs, wg["w_conv_out"], "conv_out", out_dtype=BF16)
    s_re, s_im, y_pre, yg = s5_fwd(proj, 2 * CW, bdr3, bdi3, cd2, tabs, d_skip)
    forward(["w_ff1"], "gather_forward_ff1", yg)
    wg["w_glu_a"] = weight("w_glu_a", yg)
    wg["w_glu_b"] = weight("w_glu_b", yg)
    ga = mm_nn(yg, wg["w_glu_a"], "glu_a", out_dtype=BF16)
    gb = mm_nn(yg, wg["w_glu_b"], "glu_b", out_dtype=BF16)
    merged = merge_fwd(proj, 3 * CW, y_conv, ga, gb)
    forward(["w_ff2"], "gather_forward_ff2", merged)
    wg["w_out"] = weight("w_out", merged)
    m_out = mm_nn(merged, wg["w_out"], "out_proj")
    h1, z = residual_norm(xs, m_out, gate1, norm2_g, scale2, shift2)
    wg["w_ff1"] = weight("w_ff1", z)
    act = mm_nn(z, wg["w_ff1"], "ff1", out_dtype=BF16, epi=lambda r: jnp.square(jnp.maximum(r, 0.0)))
    wg["w_ff2"] = weight("w_ff2", act)
    ff = mm_nn(act, wg["w_ff2"], "ff2")

    dh2, dff, loss_part, d_final_g, d_gate2 = loss_bwd(h1, ff, gate2, final_g.reshape(1, D), tgt)
    df = mm_nt(dff, wg["w_ff2"], "ff2_dx", out_dtype=BF16,
               epi=lambda r, a: r * (2.0 * jnp.sqrt(a.astype(F32))), extras=(act,))
    t = scatter("w_ff2", mm_tn(act, dff, 1, "ff2_dw", out_dtype=BF16))
    t = scatter("w_ff1", mm_tn(z, df, NDEV, "ff1_dw", out_dtype=BF16, dep=t))
    dz = mm_nt(df, wg["w_ff1"], "ff1_dx", out_dtype=BF16, dep=t)
    dh1, d_scale2, d_shift2, d_norm2_g, dmo, d_gate1 = norm_bwd(dz, h1, dh2, norm2_g, scale2, "norm2_bwd", gate1, m_out)
    t = scatter("w_out", mm_tn(merged, dmo, 1, "out_dw", out_dtype=BF16))
    dmerged = mm_nt(dmo, wg["w_out"], "out_dx", out_dtype=BF16, dep=t)
    dyc, dga, dgb, dproj_g = merge_bwd(dmerged, proj, 3 * CW, y_conv, ga, gb)
    t = scatter("w_conv_out", mm_tn(vs, dyc, 1, "conv_out_dw", out_dtype=BF16))
    t = scatter("w_glu_a", mm_tn(yg, dga, 1, "glu_a_dw", out_dtype=BF16, dep=t))
    t = scatter("w_glu_b", mm_tn(yg, dgb, 1, "glu_b_dw", out_dtype=BF16, dep=t))
    dvs = mm_nt(dyc, wg["w_conv_out"], "conv_out_dx", out_dtype=BF16, dep=t)
    dyg_a = mm_nt(dga, wg["w_glu_a"], "glu_a_dx", out_dtype=BF16, dep=t)
    dyg_b = mm_nt(dgb, wg["w_glu_b"], "glu_b_dx", out_dtype=BF16, dep=t)
    dvc, d_ln_g, d_ln_b = conv_ln_bwd(dvs, vc, ln_g, ln_b)
    dproj_c, d_w_dw, d_b_dw = conv_bwd(dvc, proj, w_dw_full)
    dproj_s, d_d_skip, dcdr, dcdi, dbdr, dbdi, dlr8, dli8 = s5_bwd(
        dyg_a, dyg_b, y_pre, proj, 2 * CW, s_re, s_im, bdt2, cdrt3, cdit3, tabs, d_skip)
    d_c_re = _diag_blocks(dcdr, H, P)
    d_c_im = _diag_blocks(dcdi, H, P)
    d_bbr = _diag_blocks(dbdr, H, P)
    d_bbi = _diag_blocks(dbdi, H, P)
    dlr = jnp.sum(dlr8, axis=0).reshape(G, P)
    dli = jnp.sum(dli8, axis=0).reshape(G, P)
    early_parts = [d_b_dw, d_ln_g, d_ln_b, dlr, dli, d_bbr, d_bbi, d_c_re, d_c_im, d_d_skip, d_norm2_g, d_final_g,
                   d_w_dw]
    pack_early = _flat_pad(early_parts, PACK).reshape(NDEV, -1, 1024)
    (early_scatter,), done = exchange_start([pack_early], "scatter_small_start", False)
    for k in ("w_ff2", "w_ff1", "w_out"):
        done = finish_weight(k, done)
    early_sum = sum_devices(exchange_wait(early_scatter, done, "scatter_small_wait", False), "sum_small_early")
    (early_gather,), done = exchange_start([early_sum], "gather_small_start", True)
    dproj = jnp.concatenate([dproj_c, dproj_s, dproj_g], axis=1)
    t = scatter("w_in", mm_tn(u, dproj, NDEV, "in_dw", out_dtype=BF16, dep=done))
    du = mm_nt(dproj, wg["w_in"], "in_dx", out_dtype=BF16, dep=t)
    grad_x, d_scale1, d_shift1, d_norm1_g = norm_bwd(du, xs, dh1, norm1_g, scale1, "norm1_bwd")

    dmod = jnp.concatenate([d_shift1, d_scale1, d_gate1, d_shift2, d_scale2, d_gate2], axis=1)
    late_parts = [dmod, d_norm1_g]
    pack_late = _flat_pad(late_parts, PACK).reshape(NDEV, -1, 1024)
    parts_late, dmod_from = _exchange([pack_late, dmod.reshape(NDEV, 1, n_ada)], "scatter_small_late", False)
    (tot_late,) = _exchange([sum_devices(parts_late, "sum_small_late")], "gather_small_late", True)
    tot_early = exchange_wait(early_gather, tot_late, "gather_small_wait", True)
    g_b_ada, g_norm1_g = _split(tot_late.reshape(-1), late_parts)
    (g_b_dw, g_ln_g, g_ln_b, t_lr, t_li, t_bbr, t_bbi, g_c_re_t, g_c_im_t, g_d_skip,
     g_norm2_g, g_final_g, g_w_dw_full) = _split(tot_early.reshape(-1), early_parts)
    g_a_re, g_a_im, g_ldt, g_br2, g_bi2 = s5_params_bwd(
        a_re[0], a_im[0], ldt, br2, bi2, expand, t_lr, t_li, t_bbr.reshape(G * H, P), t_bbi.reshape(G * H, P))
    g_brt, g_bit = g_br2.reshape(G, H, P), g_bi2.reshape(G, H, P)
    dmod_cols = dmod_from.reshape(NDEV, n_ada)

    g2 = {
        "b_ada": g_b_ada, "norm1_g": g_norm1_g, "b_dw": g_b_dw, "ln_g": g_ln_g, "ln_b": g_ln_b,
        "a_re": g_a_re, "a_im": g_a_im, "log_dt": g_ldt.reshape(1, G),
        "b_re": g_brt.transpose(0, 2, 1).reshape(G * P, H), "b_im": g_bit.transpose(0, 2, 1).reshape(G * P, H),
        "c_re": g_c_re_t.reshape(G * H, P), "c_im": g_c_im_t.reshape(G * H, P), "d_skip": g_d_skip,
        "norm2_g": g_norm2_g, "final_g": g_final_g,
        "w_dw": lax.dynamic_slice(g_w_dw_full, (0, me * (CW // NDEV)), (CONV_K, CW // NDEV)),
    }
    small = [k for k in names if k in g2]
    as2d = lambda k, a: a.reshape(g2[k].shape)
    outs = adam_many([(as2d(k, W[k]), g2[k], as2d(k, Mo[k]), as2d(k, Vo[k])) for k in small], "adam_small")
    grads = {k: g2[k].reshape(W[k].shape) for k in small}
    delta = {k: o[0].reshape(W[k].shape) for k, o in zip(small, outs)}
    new_m = {k: o[1].reshape(W[k].shape) for k, o in zip(small, outs)}
    new_v = {k: o[2].reshape(W[k].shape) for k, o in zip(small, outs)}

    g, d, mm, vv = adam_w_ada(c_act, dmod_cols, w_ada[0], m_w_ada[0], v_w_ada[0])
    grads["w_ada"], delta["w_ada"], new_m["w_ada"], new_v["w_ada"] = g[None], d[None], mm[None], vv[None]

    after = d
    for k in ("w_conv_out", "w_glu_a", "w_glu_b", "w_in"):
        after = finish_weight(k, after)
    for k in big:
        g, d, mm, vv = big_out[k]
        grads[k], delta[k], new_m[k], new_v[k] = g[None], d[None], mm[None], vv[None]

    loss = lax.psum(loss_part[0, 0], ("x", "y", "c"))
    return (loss, grad_x[None], *[grads[k] for k in names], *[delta[k] for k in names],
            *[new_m[k] for k in names], *[new_v[k] for k in names])
```

```python
import functools
import math

import jax
import jax.numpy as jnp
from jax import lax
from jax.experimental import pallas as pl
from jax.experimental.pallas import tpu as pltpu

F32 = jnp.float32
BF16 = jnp.bfloat16
NDEV = 8
EPS = 1e-6
ADAM_LR, ADAM_B1, ADAM_B2, ADAM_EPS, ADAM_WD, ADAM_STEP = 0.001, 0.9, 0.999, 1e-08, 0.01, 10
CONV_K = 31
HALO = 32
GROUP = 16
STATE = 64
GB = 8
S5_ROWS = 512
HI = lax.Precision.HIGHEST
MESH = pl.DeviceIdType.MESH
VMEM_LIMIT = 56 * 1024 * 1024
MAX_CONTRACT = 2048
PACK_ROWS = 64
PACK = PACK_ROWS * 1024
ANY = pl.BlockSpec(memory_space=pl.ANY)


def _params(sem=None):
    if sem is None:
        return pltpu.CompilerParams(vmem_limit_bytes=VMEM_LIMIT)
    return pltpu.CompilerParams(dimension_semantics=sem, vmem_limit_bytes=VMEM_LIMIT)


def _sigmoid(v):
    return 1.0 / (1.0 + jnp.exp(-v))


def _me():
    return 4 * lax.axis_index("x") + 2 * lax.axis_index("y") + lax.axis_index("c")


def _peer(k):
    x, y, c = lax.axis_index("x"), lax.axis_index("y"), lax.axis_index("c")
    px = 1 - x if (k >> 2) & 1 else x
    py = 1 - y if (k >> 1) & 1 else y
    pc = 1 - c if k & 1 else c
    return (px, py, pc), 4 * px + 2 * py + pc


def _exchange(arrays, name, gather):
    n = len(arrays)
    out_shape = []
    for a in arrays:
        shp = (NDEV,) + a.shape if gather else a.shape
        out_shape.append(jax.ShapeDtypeStruct(shp, a.dtype))

    def body(*refs):
        ins, outs = refs[:n], refs[n:2 * n]
        send, recv, lsem = refs[2 * n:]
        me = _me()
        local = []
        for a in range(n):
            src = ins[a] if gather else ins[a].at[me]
            cp = pltpu.make_async_copy(src, outs[a].at[me], lsem.at[a])
            cp.start()
            local.append(cp)
        sends = []
        for a in range(n):
            for k in range(1, NDEV):
                dev, pidx = _peer(k)
                src = ins[a] if gather else ins[a].at[pidx]
                cp = pltpu.make_async_remote_copy(
                    src_ref=src, dst_ref=outs[a].at[me], send_sem=send.at[a * (NDEV - 1) + k - 1], recv_sem=recv.at[a * (NDEV - 1) + k - 1],
                    device_id=dev, device_id_type=MESH)
                cp.start()
                sends.append(cp)
        for a in range(n):
            for k in range(1, NDEV):
                dev, pidx = _peer(k)
                src = ins[a] if gather else ins[a].at[pidx]
                pltpu.make_async_remote_copy(
                    src_ref=src, dst_ref=outs[a].at[pidx], send_sem=send.at[a * (NDEV - 1) + k - 1], recv_sem=recv.at[a * (NDEV - 1) + k - 1],
                    device_id=dev, device_id_type=MESH).wait_recv()
        for cp in sends:
            cp.wait_send()
        for cp in local:
            cp.wait()

    return pl.pallas_call(
        body, name=name, out_shape=tuple(out_shape),
        in_specs=[ANY] * n, out_specs=tuple([ANY] * n),
        scratch_shapes=[pltpu.SemaphoreType.DMA((n * (NDEV - 1),)), pltpu.SemaphoreType.DMA((n * (NDEV - 1),)),
                        pltpu.SemaphoreType.DMA((n,))],
    )(*arrays)


HBM = pl.BlockSpec(memory_space=pltpu.HBM)
SEM = pl.BlockSpec(memory_space=pltpu.SEMAPHORE)
EFFECT = pltpu.SideEffectType.DATAFLOW_SIDE_EFFECTING
NPEER = NDEV - 1


def _landing(block_of_me, shape, dtype):
    land = lax.empty((NDEV,) + tuple(shape), dtype)
    start = (_me(),) + (0,) * len(shape)
    return pltpu.with_memory_space_constraint(lax.dynamic_update_slice(land, block_of_me[None], start), pltpu.HBM)


def exchange_start(arrays, name, gather, after=None):
    n = len(arrays)
    me = _me()
    deps = () if after is None else (after,)
    lands = []
    for a in arrays:
        if gather:
            lands.append(_landing(a, a.shape, a.dtype))
        else:
            mine = lax.dynamic_slice(a, (me,) + (0,) * (a.ndim - 1), (1,) + a.shape[1:])[0]
            lands.append(_landing(mine, a.shape[1:], a.dtype))
    srcs = [pltpu.with_memory_space_constraint(a, pltpu.HBM) for a in arrays]

    def body(*refs):
        ins, lnd = refs[:n], refs[n:2 * n]
        outs = refs[2 * n + len(deps):]
        sends, recvs, token = outs[:n], outs[n:2 * n], outs[-1]
        my = _me()
        for a in range(n):
            for k in range(1, NDEV):
                dev, pidx = _peer(k)
                src = ins[a] if gather else ins[a].at[pidx]
                pltpu.make_async_remote_copy(
                    src_ref=src, dst_ref=lnd[a].at[my], send_sem=sends[a].at[k - 1], recv_sem=recvs[a].at[k - 1],
                    device_id=dev, device_id_type=MESH).start()
        token[...] = jnp.zeros_like(token)

    out_shape = ([pltpu.SemaphoreType.DMA((NPEER,))] * (2 * n)
                 + [pltpu.HBM(a.shape, a.dtype) for a in srcs] + [pltpu.HBM(l.shape, l.dtype) for l in lands]
                 + [jax.ShapeDtypeStruct((8, 128), F32)])
    res = pl.pallas_call(
        body, name=name, out_shape=tuple(out_shape),
        in_specs=[HBM] * (2 * n) + [ANY] * len(deps),
        out_specs=tuple([SEM] * (2 * n) + [HBM] * (2 * n) + [pl.BlockSpec(memory_space=pltpu.VMEM)]),
        input_output_aliases={i: 2 * n + i for i in range(2 * n)},
        compiler_params=pltpu.CompilerParams(has_side_effects=EFFECT),
    )(*srcs, *lands, *deps)
    handles = [(res[a], res[n + a], res[2 * n + a], res[3 * n + a]) for a in range(n)]
    return handles, res[-1]


def exchange_wait(handle, after, name, gather):
    send_sem, recv_sem, src, land = handle

    def body(src_ref, land_ref, s_ref, r_ref, after_ref, src_out, land_out):
        for k in range(1, NDEV):
            dev, pidx = _peer(k)
            s = src_ref if gather else src_ref.at[pidx]
            cp = pltpu.make_async_remote_copy(
                src_ref=s, dst_ref=land_ref.at[pidx], send_sem=s_ref.at[k - 1], recv_sem=r_ref.at[k - 1],
                device_id=dev, device_id_type=MESH)
            cp.wait_send()
            cp.wait_recv()

    return pl.pallas_call(
        body, name=name, out_shape=(pltpu.HBM(src.shape, src.dtype), pltpu.HBM(land.shape, land.dtype)),
        in_specs=(HBM, HBM, SEM, SEM, ANY), out_specs=(HBM, HBM), input_output_aliases={0: 0, 1: 1},
        compiler_params=pltpu.CompilerParams(has_side_effects=EFFECT),
    )(src, land, send_sem, recv_sem, after)[1]


ICI_PEERS = (2, 4, 6)
SIBLING = 1


def gather2_start(blocks, name, after):
    m = len(blocks)
    lands = [_landing(b, b.shape, b.dtype) for b in blocks]
    srcs = [pltpu.with_memory_space_constraint(b, pltpu.HBM) for b in blocks]
    n = len(ICI_PEERS)

    def body(*refs):
        src, lnd = refs[:m], refs[m:2 * m]
        outs = refs[2 * m + 1:]
        send, recv_sib, recv_ici = outs[:m], outs[m:2 * m], outs[2 * m:3 * m]
        my = _me()
        for a in range(m):
            dev, _ = _peer(SIBLING)
            pltpu.make_async_remote_copy(src_ref=src[a], dst_ref=lnd[a].at[my], send_sem=send[a].at[0],
                                         recv_sem=recv_sib[a].at[0], device_id=dev, device_id_type=MESH).start()
            for j, k in enumerate(ICI_PEERS):
                dev, _ = _peer(k)
                pltpu.make_async_remote_copy(src_ref=src[a], dst_ref=lnd[a].at[my], send_sem=send[a].at[1 + j],
                                             recv_sem=recv_ici[a].at[j], device_id=dev, device_id_type=MESH).start()

    out_shape = ([pltpu.SemaphoreType.DMA((1 + n,))] * m + [pltpu.SemaphoreType.DMA((1,))] * m
                 + [pltpu.SemaphoreType.DMA((n,))] * m
                 + [pltpu.HBM(s.shape, s.dtype) for s in srcs] + [pltpu.HBM(l.shape, l.dtype) for l in lands])
    res = pl.pallas_call(
        body, name=name, out_shape=tuple(out_shape),
        in_specs=[HBM] * (2 * m) + [ANY], out_specs=tuple([SEM] * (3 * m) + [HBM] * (2 * m)),
        input_output_aliases={i: 3 * m + i for i in range(2 * m)},
        compiler_params=pltpu.CompilerParams(has_side_effects=EFFECT),
    )(*srcs, *lands, after)
    return [tuple(res[g * m + a] for g in range(5)) for a in range(m)]


def gather2_forward(handles, name, after):
    m = len(handles)
    n = len(ICI_PEERS)
    srcs, lands = [h[3] for h in handles], [h[4] for h in handles]
    deps = tuple(after) if isinstance(after, (tuple, list)) else (after,)

    def body(*refs):
        src, lnd, recv_ici = refs[:m], refs[m:2 * m], refs[2 * m:3 * m]
        outs = refs[3 * m + len(deps):]
        fsend, frecv = outs[:m], outs[m:2 * m]
        sib, _ = _peer(SIBLING)
        for a in range(m):
            for j, k in enumerate(ICI_PEERS):
                dev, pidx = _peer(k)
                pltpu.make_async_remote_copy(
                    src_ref=src[a], dst_ref=lnd[a].at[pidx], send_sem=fsend[a].at[j], recv_sem=recv_ici[a].at[j],
                    device_id=dev, device_id_type=MESH).wait_recv()
                pltpu.make_async_remote_copy(
                    src_ref=lnd[a].at[pidx], dst_ref=lnd[a].at[pidx], send_sem=fsend[a].at[j], recv_sem=frecv[a].at[j],
                    device_id=sib, device_id_type=MESH).start()

    out_shape = ([pltpu.SemaphoreType.DMA((n,))] * (2 * m)
                 + [pltpu.HBM(s.shape, s.dtype) for s in srcs] + [pltpu.HBM(l.shape, l.dtype) for l in lands])
    res = pl.pallas_call(
        body, name=name, out_shape=tuple(out_shape),
        in_specs=[HBM] * (2 * m) + [SEM] * m + [ANY] * len(deps),
        out_specs=tuple([SEM] * (2 * m) + [HBM] * (2 * m)),
        input_output_aliases={i: 2 * m + i for i in range(2 * m)},
        compiler_params=pltpu.CompilerParams(has_side_effects=EFFECT),
    )(*srcs, *lands, *[h[2] for h in handles], *deps)
    return [(handles[a][0], handles[a][1], res[a], res[m + a], res[2 * m + a], res[3 * m + a]) for a in range(m)]


def gather2_wait(handle, name, after):
    send, recv_sib, fsend, frecv, src, land = handle

    def body(src_ref, land_ref, send_ref, recv_sib_ref, fsend_ref, frecv_ref, after_ref, src_out, land_out):
        sib, sib_idx = _peer(SIBLING)
        own = pltpu.make_async_remote_copy(src_ref=src_ref, dst_ref=land_ref.at[sib_idx], send_sem=send_ref.at[0],
                                           recv_sem=recv_sib_ref.at[0], device_id=sib, device_id_type=MESH)
        own.wait_send()
        own.wait_recv()
        for j, k in enumerate(ICI_PEERS):
            dev, pidx = _peer(k)
            pltpu.make_async_remote_copy(src_ref=src_ref, dst_ref=land_ref.at[pidx], send_sem=send_ref.at[1 + j],
                                         recv_sem=frecv_ref.at[j], device_id=dev, device_id_type=MESH).wait_send()
            _, fidx = _peer(k ^ SIBLING)
            fwd = pltpu.make_async_remote_copy(src_ref=land_ref.at[pidx], dst_ref=land_ref.at[fidx],
                                               send_sem=fsend_ref.at[j], recv_sem=frecv_ref.at[j],
                                               device_id=sib, device_id_type=MESH)
            fwd.wait_send()
            fwd.wait_recv()

    return pl.pallas_call(
        body, name=name, out_shape=(pltpu.HBM(src.shape, src.dtype), pltpu.HBM(land.shape, land.dtype)),
        in_specs=(HBM, HBM, SEM, SEM, SEM, SEM, ANY), out_specs=(HBM, HBM), input_output_aliases={0: 0, 1: 1},
        compiler_params=pltpu.CompilerParams(has_side_effects=EFFECT),
    )(src, land, send, recv_sib, fsend, frecv, after)[1]


def _acc_steps(p, acc, k, nk, finish):
    if nk == 1:
        finish(p)
        return

    @pl.when(k == 0)
    def _():
        acc[...] = p

    @pl.when(k > 0)
    def _():
        acc[...] += p

    @pl.when(k == nk - 1)
    def _():
        finish(acc[...])


def mm_nn(a, w3, name, out_dtype=F32, epi=None, extras=()):
    M, K = a.shape
    J, _, n = w3.shape
    tm, tn, tk = min(1024, M), min(1024, n), min(2048, K)
    q, nk, ne = n // tn, K // tk, len(extras)

    def body(*refs):
        a_ref, w_ref = refs[:2]
        ex, o_ref, acc = refs[2:2 + ne], refs[2 + ne], refs[-1]
        p = jnp.dot(a_ref[...], w_ref[...], preferred_element_type=F32)

        def finish(r):
            if epi is not None:
                r = epi(r, *[e[...] for e in ex])
            o_ref[...] = r.astype(out_dtype)

        _acc_steps(p, acc, pl.program_id(2), nk, finish)

    return pl.pallas_call(
        body, name=name, grid=(M // tm, J * q, nk),
        in_specs=[pl.BlockSpec((tm, tk), lambda i, j, k: (i, k)),
                  pl.BlockSpec((None, tk, tn), lambda i, j, k: (j // q, k, j % q))]
        + [pl.BlockSpec((tm, tn), lambda i, j, k: (i, j))] * ne,
        out_specs=pl.BlockSpec((tm, tn), lambda i, j, k: (i, j)),
        out_shape=jax.ShapeDtypeStruct((M, J * n), out_dtype),
        scratch_shapes=[pltpu.VMEM((tm, tn), F32)],
        compiler_params=_params(("parallel", "parallel", "arbitrary")),
    )(a, w3, *extras)


def mm_nt(dy, w3, name, out_dtype=F32, epi=None, extras=(), dep=None):
    M, _ = dy.shape
    J, K, n = w3.shape
    tm, tn, tkk = min(1024, M), min(MAX_CONTRACT, n), min(1024, K)
    q, ne = n // tn, len(extras)
    s = 1
    while q == 1 and J % (2 * s) == 0 and 2 * s * tn <= MAX_CONTRACT:
        s *= 2
    nk = (J // s) * q
    deps = () if dep is None else (dep,)

    def body(*refs):
        d_ref, w_ref = refs[:2]
        ex, o_ref, acc = refs[2:2 + ne], refs[-2], refs[-1]
        nt = (((1,), (1,)), ((), ()))
        p = lax.dot_general(d_ref[:, 0:tn], w_ref[0], nt, preferred_element_type=F32)
        for j in range(1, s):
            p = p + lax.dot_general(d_ref[:, j * tn:(j + 1) * tn], w_ref[j], nt, preferred_element_type=F32)

        def finish(r):
            if epi is not None:
                r = epi(r, *[e[...] for e in ex])
            o_ref[...] = r.astype(out_dtype)

        _acc_steps(p, acc, pl.program_id(2), nk, finish)

    return pl.pallas_call(
        body, name=name, grid=(M // tm, K // tkk, nk),
        in_specs=[pl.BlockSpec((tm, s * tn), lambda i, kk, c: (i, c)),
                  pl.BlockSpec((s, tkk, tn), lambda i, kk, c: (c // q, kk, c % q))]
        + [pl.BlockSpec((tm, tkk), lambda i, kk, c: (i, kk))] * ne + [ANY] * len(deps),
        out_specs=pl.BlockSpec((tm, tkk), lambda i, kk, c: (i, kk)),
        out_shape=jax.ShapeDtypeStruct((M, K), out_dtype),
        scratch_shapes=[pltpu.VMEM((tm, tkk), F32)],
        compiler_params=_params(("parallel", "parallel", "arbitrary")),
    )(dy, w3, *extras, *deps)


def mm_tn(a, dy, J, name, out_dtype=F32, dep=None):
    M, K = a.shape
    n = dy.shape[1] // J
    tm, tn, tkk = min(MAX_CONTRACT, M), min(1024, n), min(1024, K)
    q, nk = n // tn, M // tm
    deps = () if dep is None else (dep,)

    def body(a_ref, d_ref, *rest):
        o_ref, acc = rest[-2:]
        p = lax.dot_general(a_ref[...], d_ref[...], (((0,), (0,)), ((), ())), preferred_element_type=F32)

        def finish(r):
            o_ref[...] = r.astype(out_dtype)

        _acc_steps(p, acc, pl.program_id(2), nk, finish)

    return pl.pallas_call(
        body, name=name, grid=(K // tkk, J * q, nk),
        in_specs=[pl.BlockSpec((tm, tkk), lambda kk, c, m: (m, kk)),
                  pl.BlockSpec((tm, tn), lambda kk, c, m: (m, c))] + [ANY] * len(deps),
        out_specs=pl.BlockSpec((None, tkk, tn), lambda kk, c, m: (c // q, kk, c % q)),
        out_shape=jax.ShapeDtypeStruct((J, K, n), out_dtype),
        scratch_shapes=[pltpu.VMEM((tkk, tn), F32)],
        compiler_params=_params(("parallel", "parallel", "arbitrary")),
    )(a, dy, *deps)


def _tm(L):
    return min(256, L)


def _row(w, cb=0, tm=None):
    return pl.BlockSpec((tm, w), lambda i: (i, cb))


def _vec(w, cb=0):
    return pl.BlockSpec((1, w), lambda i: (0, cb))


def _accum(ref, val, i):
    @pl.when(i == 0)
    def _():
        ref[...] = val

    @pl.when(i > 0)
    def _():
        ref[...] += val


def _colsum(v):
    return jnp.sum(v, axis=0, keepdims=True)


def _rms(v):
    return lax.rsqrt(jnp.mean(v * v, axis=-1, keepdims=True) + EPS)


def adaln_mod(c_all, w_ada, b_cols):
    B, D = c_all.shape
    n = w_ada.shape[1]
    tn = 512

    def body(c_ref, w_ref, b_ref, o_ref, ca_ref):
        cv = c_ref[...]
        ca = cv * _sigmoid(cv)
        ca_ref[...] = ca
        o_ref[...] = jnp.dot(ca.astype(BF16), w_ref[...].astype(BF16), preferred_element_type=F32) + b_ref[...]

    return pl.pallas_call(
        body, name="adaln_mod", grid=(n // tn,),
        in_specs=[pl.BlockSpec((B, D), lambda j: (0, 0)), pl.BlockSpec((D, tn), lambda j: (0, j)),
                  pl.BlockSpec((1, tn), lambda j: (0, j))],
        out_specs=(pl.BlockSpec((B, tn), lambda j: (0, j)), pl.BlockSpec((B, D), lambda j: (0, 0))),
        out_shape=(jax.ShapeDtypeStruct((B, n), F32), jax.ShapeDtypeStruct((B, D), F32)),
        compiler_params=_params(("arbitrary",)),
    )(c_all, w_ada, b_cols)


def prenorm(x, g, scale, shift, name):
    L, D = x.shape
    tm = _tm(L)

    def body(x_ref, g_ref, sc_ref, sh_ref, u_ref):
        xv = x_ref[...]
        u_ref[...] = (xv * _rms(xv) * g_ref[...] * (1.0 + sc_ref[...]) + sh_ref[...]).astype(BF16)

    return pl.pallas_call(
        body, name=name, grid=(L // tm,),
        in_specs=[_row(D, tm=tm), _vec(D), _vec(D), _vec(D)],
        out_specs=_row(D, tm=tm), out_shape=jax.ShapeDtypeStruct((L, D), BF16),
        compiler_params=_params(("parallel",)),
    )(x, g, scale, shift)


def _shifted_copies(buf, shifted, tm):
    n = HALO + tm - 8
    for r in range(1, 8):
        shifted[r - 1] = buf[pl.ds(r, n), :]


def _window(buf, shifted, off, tm):
    r, base = off % 8, off - off % 8
    if r == 0:
        return buf[pl.ds(base, tm), :]
    return shifted[r - 1, pl.ds(base, tm), :]


def conv_fwd(proj, w_dw, b_dw, ln_g, ln_b):
    L = proj.shape[0]
    C = w_dw.shape[1]
    tm = _tm(L)
    hb = tm // HALO

    def body(a_ref, g_ref, ah_ref, gh_ref, w_ref, b_ref, lg_ref, lb_ref, vs_ref, vc_ref, buf, shifted):
        i = pl.program_id(0)
        halo = ah_ref[...] * _sigmoid(gh_ref[...])
        buf[0:HALO, :] = halo * jnp.where(i > 0, 1.0, 0.0)
        buf[HALO:HALO + tm, :] = a_ref[...] * _sigmoid(g_ref[...])
        _shifted_copies(buf, shifted, tm)
        acc = jnp.zeros((tm, C), F32) + b_ref[...]
        for k in range(CONV_K):
            acc = acc + w_ref[k:k + 1, :] * _window(buf, shifted, HALO - (CONV_K - 1) + k, tm)
        vc_ref[...] = acc
        mu = jnp.mean(acc, axis=-1, keepdims=True)
        d = acc - mu
        var = jnp.mean(d * d, axis=-1, keepdims=True)
        ln = d * lax.rsqrt(var + EPS) * lg_ref[...] + lb_ref[...]
        vs_ref[...] = (ln * _sigmoid(ln)).astype(BF16)

    prev = lambda cb: pl.BlockSpec((HALO, C), lambda i: (jnp.maximum(i * hb - 1, 0), cb))
    return pl.pallas_call(
        body, name="conv_fwd", grid=(L // tm,),
        in_specs=[_row(C, 0, tm), _row(C, 1, tm), prev(0), prev(1),
                  pl.BlockSpec((HALO, C), lambda i: (0, 0)), _vec(C), _vec(C), _vec(C)],
        out_specs=(_row(C, tm=tm), _row(C, tm=tm)),
        out_shape=(jax.ShapeDtypeStruct((L, C), BF16), jax.ShapeDtypeStruct((L, C), F32)),
        scratch_shapes=[pltpu.VMEM((HALO + tm, C), F32), pltpu.VMEM((7, HALO + tm - 8, C), F32)],
        compiler_params=_params(("parallel",)),
    )(proj, proj, proj, proj, w_dw, b_dw, ln_g, ln_b)


def _gelu(v):
    return 0.5 * v * (1.0 + jnp.tanh(math.sqrt(2.0 / math.pi) * (v + 0.044715 * v * v * v)))


def _gelu_grad(v):
    k = math.sqrt(2.0 / math.pi)
    t = jnp.tanh(k * (v + 0.044715 * v * v * v))
    return 0.5 * (1.0 + t) + 0.5 * v * (1.0 - t * t) * k * (1.0 + 3.0 * 0.044715 * v * v)


def s5_param_fn(ar, ai, ldt, br, bi, expand):
    dt = jnp.exp(ldt)
    er = jnp.exp(ar * dt)
    th = ai * dt
    lbr, lbi = er * jnp.cos(th), er * jnp.sin(th)
    nr, ni = lbr - 1.0, lbi
    den = ar * ar + ai * ai
    qr, qi = (nr * ar + ni * ai) / den, (ni * ar - nr * ai) / den
    qre = jnp.dot(expand, qr, precision=HI, preferred_element_type=F32)
    qie = jnp.dot(expand, qi, precision=HI, preferred_element_type=F32)
    return lbr, lbi, qre * br - qie * bi, qre * bi + qie * br


def s5_params(ar, ai, ldt, br2, bi2, expand):
    def body(ar_ref, ai_ref, ld_ref, br_ref, bi_ref, e_ref, o1, o2, o3, o4):
        r = s5_param_fn(ar_ref[...], ai_ref[...], ld_ref[...], br_ref[...], bi_ref[...], e_ref[...])
        o1[...], o2[...], o3[...], o4[...] = r

    s2, s3 = jax.ShapeDtypeStruct(ar.shape, F32), jax.ShapeDtypeStruct(br2.shape, F32)
    return pl.pallas_call(body, name="s5_params", out_shape=(s2, s2, s3, s3), compiler_params=_params())(
        ar, ai, ldt, br2, bi2, expand)


def s5_params_bwd(ar, ai, ldt, br2, bi2, expand, dlr, dli, dbr, dbi):
    def body(ar_ref, ai_ref, ld_ref, br_ref, bi_ref, e_ref, c1, c2, c3, c4, o1, o2, o3, o4, o5):
        e = e_ref[...]
        fn = lambda a, b, c, d, f: s5_param_fn(a, b, c, d, f, e)
        _, vjp = jax.vjp(fn, ar_ref[...], ai_ref[...], ld_ref[...], br_ref[...], bi_ref[...])
        r = vjp((c1[...], c2[...], c3[...], c4[...]))
        o1[...], o2[...], o3[...], o4[...], o5[...] = r

    shapes = tuple(jax.ShapeDtypeStruct(v.shape, F32) for v in (ar, ai, ldt, br2, bi2))
    return pl.pallas_call(body, name="s5_params_bwd", out_shape=shapes, compiler_params=_params())(
        ar, ai, ldt, br2, bi2, expand, dlr, dli, dbr, dbi)


def s5_tables(lr, li):
    C = lr.shape[1]

    def body(lr_ref, li_ref, o_ref):
        row = lax.broadcasted_iota(jnp.int32, (8, C), 0)
        for rev in (0, 1):
            pr = jnp.broadcast_to(lr_ref[...], (8, C))
            pi = jnp.broadcast_to(-li_ref[...] if rev else li_ref[...], (8, C))
            br, bi = pr, pi
            pows = [(pr, pi)]
            for _ in range(7):
                pr, pi = pr * br - pi * bi, pr * bi + pi * br
                pows.append((pr, pi))
            base = 8 * rev
            for s, d in enumerate((1, 2, 4)):
                keep = (row + d <= 7) if rev else (row >= d)
                o_ref[base + 2 * s] = jnp.where(keep, pows[d - 1][0], 0.0)
                o_ref[base + 2 * s + 1] = jnp.where(keep, pows[d - 1][1], 0.0)
            cr, ci = jnp.zeros((8, C), F32), jnp.zeros((8, C), F32)
            for j in range(8):
                e = (8 - j) if rev else (j + 1)
                cr = jnp.where(row == j, pows[e - 1][0], cr)
                ci = jnp.where(row == j, pows[e - 1][1], ci)
            o_ref[base + 6] = cr
            o_ref[base + 7] = ci

    return pl.pallas_call(body, name="s5_tables", out_shape=jax.ShapeDtypeStruct((16, 8, C), F32),
                          compiler_params=_params())(lr, li)


def _tile_steps(xr, xi, tabs, rev):
    for s, d in enumerate((1, 2, 4)):
        tr, ti = tabs[2 * s], tabs[2 * s + 1]
        sh = (8 - d) if rev else d
        sr, si = pltpu.roll(xr, sh, 0), pltpu.roll(xi, sh, 0)
        xr, xi = xr + tr * sr - ti * si, xi + tr * si + ti * sr
    return xr, xi


def _tile_carry(xr, xi, tabs, cr, ci):
    tr, ti = tabs[6], tabs[7]
    return xr + tr * cr - ti * ci, xi + tr * ci + ti * cr


def _hi_lo(a):
    hi = a.astype(BF16)
    return hi, (a - hi.astype(F32)).astype(BF16)


def _lhs3(a):
    hi, lo = _hi_lo(a)
    return jnp.concatenate([hi, lo, hi], axis=1)


def _rhs3(m):
    hi, lo = _hi_lo(m)
    return jnp.concatenate([hi, hi, lo], axis=-2)


def s5_fwd(proj, col0, bdr3, bdi3, cd2, tabs, d_skip):
    L = proj.shape[0]
    nb, cw3, sw = bdr3.shape
    cw = cw3 // 3
    tl = min(S5_ROWS, L)
    cb0 = col0 // cw

    def body(u_ref, bdr_ref, bdi_ref, cd_ref, t_ref, dk_ref, sr_ref, si_ref, yp_ref, yg_ref, car):
        l = pl.program_id(1)

        @pl.when(l == 0)
        def _():
            car[...] = jnp.zeros_like(car)

        u = u_ref[...]
        u3 = _lhs3(u)
        sr_ref[...] = jnp.dot(u3, bdr_ref[...], preferred_element_type=F32)
        si_ref[...] = jnp.dot(u3, bdi_ref[...], preferred_element_type=F32)

        def pair(i, c):
            tabs = [t_ref[j] for j in range(8)]
            r0 = pl.multiple_of(i * 16, 16)
            lo, hi = pl.ds(r0, 8), pl.ds(r0 + 8, 8)
            a = _tile_steps(sr_ref[lo, :], si_ref[lo, :], tabs, False)
            b = _tile_steps(sr_ref[hi, :], si_ref[hi, :], tabs, False)
            ar, ai = _tile_carry(a[0], a[1], tabs, c[0], c[1])
            br, bi = _tile_carry(b[0], b[1], tabs, ar[7:8, :], ai[7:8, :])
            sr_ref[lo, :], si_ref[lo, :] = ar, ai
            sr_ref[hi, :], si_ref[hi, :] = br, bi
            return br[7:8, :], bi[7:8, :]

        c = lax.fori_loop(0, tl // 16, pair, (car[0:1, :], car[1:2, :]))
        car[0:1, :] = c[0]
        car[1:2, :] = c[1]
        s2 = jnp.concatenate([sr_ref[...].astype(BF16), si_ref[...].astype(BF16)], axis=1)
        y = jnp.dot(s2, cd_ref[...], preferred_element_type=F32) + dk_ref[...] * u
        yp_ref[...] = y
        yg_ref[...] = _gelu(y).astype(BF16)

    blk = lambda r, c: pl.BlockSpec((None, r, c), lambda b, l: (b, 0, 0))
    return pl.pallas_call(
        body, name="s5_fwd", grid=(nb, L // tl),
        in_specs=[pl.BlockSpec((tl, cw), lambda b, l: (l, cb0 + b)), blk(cw3, sw), blk(cw3, sw), blk(2 * sw, cw),
                  pl.BlockSpec((8, 8, sw), lambda b, l: (0, 0, b)), pl.BlockSpec((1, cw), lambda b, l: (0, b))],
        out_specs=(pl.BlockSpec((tl, sw), lambda b, l: (l, b)), pl.BlockSpec((tl, sw), lambda b, l: (l, b)),
                   pl.BlockSpec((tl, cw), lambda b, l: (l, b)), pl.BlockSpec((tl, cw), lambda b, l: (l, b))),
        out_shape=(jax.ShapeDtypeStruct((L, nb * sw), F32), jax.ShapeDtypeStruct((L, nb * sw), F32),
                   jax.ShapeDtypeStruct((L, nb * cw), F32), jax.ShapeDtypeStruct((L, nb * cw), BF16)),
        scratch_shapes=[pltpu.VMEM((8, sw), F32)],
        compiler_params=_params(("parallel", "arbitrary")),
    )(proj, bdr3, bdi3, cd2, tabs, d_skip)


def s5_bwd(dyg_a, dyg_b, yp, proj, col0, s_re, s_im, bdt2, cdrt3, cdit3, tabs, d_skip):
    L = proj.shape[0]
    nb, sw2, cw = bdt2.shape
    sw = sw2 // 2
    tl = min(S5_ROWS, L)
    nl = L // tl
    cb0 = col0 // cw
    tb = tl // 8

    def body(da_ref, db_ref, yp_ref, u_ref, sr_ref, si_ref, hr_ref, hi_ref, bdt_ref, cdrt_ref, cdit_ref,
             t_ref, dk_ref, du_ref, ddk_ref, dcr_ref, dci_ref, dbr_ref, dbi_ref, dlr_ref, dli_ref,
             gr, gi, pr, pi, car):
        l = pl.program_id(1)
        first = l == nl - 1

        @pl.when(l == 0)
        def _():
            car[...] = jnp.zeros_like(car)

        u = u_ref[...]
        dy = (da_ref[...].astype(F32) + db_ref[...].astype(F32)) * _gelu_grad(yp_ref[...])
        dy3 = _lhs3(dy)
        gr[...] = jnp.dot(dy3, cdrt_ref[...], preferred_element_type=F32)
        gi[...] = jnp.dot(dy3, cdit_ref[...], preferred_element_type=F32)
        inner = jnp.where(first, 0.0, 1.0)
        pr[0:8, :] = hr_ref[...] * inner
        pi[0:8, :] = hi_ref[...] * inner
        pr[8:8 + tl, :] = sr_ref[...]
        pi[8:8 + tl, :] = si_ref[...]
        row = lax.broadcasted_iota(jnp.int32, (8, sw), 0)

        def pair(j, c):
            tabs = [t_ref[8 + k] for k in range(8)]
            r0 = pl.multiple_of((tb // 2 - 1 - j) * 16, 16)
            lo, hi = pl.ds(r0, 8), pl.ds(r0 + 8, 8)
            b = _tile_steps(gr[hi, :], gi[hi, :], tabs, True)
            a = _tile_steps(gr[lo, :], gi[lo, :], tabs, True)
            br, bi = _tile_carry(b[0], b[1], tabs, c[0], c[1])
            ar, ai = _tile_carry(a[0], a[1], tabs, br[0:1, :], bi[0:1, :])
            gr[lo, :], gi[lo, :] = ar, ai
            gr[hi, :], gi[hi, :] = br, bi
            p0r, p1r, p2r = [pltpu.roll(pr[pl.ds(r0 + 8 * n, 8), :], 1, 0) for n in range(3)]
            p0i, p1i, p2i = [pltpu.roll(pi[pl.ds(r0 + 8 * n, 8), :], 1, 0) for n in range(3)]
            qar, qai = jnp.where(row == 0, p0r, p1r), jnp.where(row == 0, p0i, p1i)
            qbr, qbi = jnp.where(row == 0, p1r, p2r), jnp.where(row == 0, p1i, p2i)
            return (ar[0:1, :], ai[0:1, :], c[2] + (ar * qar + ai * qai) + (br * qbr + bi * qbi),
                    c[3] + (ai * qar - ar * qai) + (bi * qbr - br * qbi))

        z = jnp.zeros((8, sw), F32)
        c = lax.fori_loop(0, tb // 2, pair, (car[0:1, :], car[1:2, :], z, z))
        car[0:1, :] = c[0]
        car[1:2, :] = c[1]
        g_re, g_im = gr[...].astype(BF16), gi[...].astype(BF16)
        g2 = jnp.concatenate([g_re, g_im], axis=1)
        du_ref[...] = (dy * dk_ref[...] + jnp.dot(g2, bdt_ref[...], preferred_element_type=F32)).astype(BF16)
        tn = (((0,), (0,)), ((), ()))
        dyb, ub = dy.astype(BF16), u.astype(BF16)
        _accum(ddk_ref, _colsum(dy * u), l)
        _accum(dcr_ref, lax.dot_general(dyb, sr_ref[...].astype(BF16), tn, preferred_element_type=F32), l)
        _accum(dci_ref, -lax.dot_general(dyb, si_ref[...].astype(BF16), tn, preferred_element_type=F32), l)
        _accum(dbr_ref, lax.dot_general(ub, g_re, tn, preferred_element_type=F32), l)
        _accum(dbi_ref, lax.dot_general(ub, g_im, tn, preferred_element_type=F32), l)
        _accum(dlr_ref, c[2], l)
        _accum(dli_ref, c[3], l)

    rl = lambda l: nl - 1 - l
    cblk = lambda w, off=0: pl.BlockSpec((tl, w), lambda b, l: (rl(l), off + b))
    halo = pl.BlockSpec((8, sw), lambda b, l: (jnp.maximum(rl(l) * tb - 1, 0), b))
    mat = lambda r, c: pl.BlockSpec((None, r, c), lambda b, l: (b, 0, 0))
    return pl.pallas_call(
        body, name="s5_bwd", grid=(nb, nl),
        in_specs=[cblk(cw), cblk(cw), cblk(cw), cblk(cw, cb0), cblk(sw), cblk(sw), halo, halo,
                  mat(2 * sw, cw), mat(3 * cw, sw), mat(3 * cw, sw),
                  pl.BlockSpec((16, 8, sw), lambda b, l: (0, 0, b)), pl.BlockSpec((1, cw), lambda b, l: (0, b))],
        out_specs=(cblk(cw), pl.BlockSpec((1, cw), lambda b, l: (0, b)), mat(cw, sw), mat(cw, sw), mat(cw, sw), mat(cw, sw),
                   pl.BlockSpec((8, sw), lambda b, l: (0, b)), pl.BlockSpec((8, sw), lambda b, l: (0, b))),
        out_shape=(jax.ShapeDtypeStruct((L, nb * cw), BF16), jax.ShapeDtypeStruct((1, nb * cw), F32),
                   jax.ShapeDtypeStruct((nb, cw, sw), F32), jax.ShapeDtypeStruct((nb, cw, sw), F32),
                   jax.ShapeDtypeStruct((nb, cw, sw), F32), jax.ShapeDtypeStruct((nb, cw, sw), F32),
                   jax.ShapeDtypeStruct((8, nb * sw), F32), jax.ShapeDtypeStruct((8, nb * sw), F32)),
        scratch_shapes=[pltpu.VMEM((tl, sw), F32), pltpu.VMEM((tl, sw), F32),
                        pltpu.VMEM((tl + 8, sw), F32), pltpu.VMEM((tl + 8, sw), F32), pltpu.VMEM((8, sw), F32)],
        compiler_params=_params(("parallel", "arbitrary")),
    )(dyg_a, dyg_b, yp, proj, s_re, s_im, s_re, s_im, bdt2, cdrt3, cdit3, tabs, d_skip)


def merge_fwd(proj, col_gc, y_conv, ga, gb):
    L, D = y_conv.shape
    tm = _tm(L)
    h = D // 2
    c0 = col_gc // h

    def body(p0, p1, p2, p3, yc_ref, ga_ref, gb_ref, o_ref):
        gc, gs = (p0, p1), (p2, p3)
        for s in range(2):
            cols = slice(s * h, (s + 1) * h)
            y_ssm = ga_ref[:, cols].astype(F32) * _sigmoid(gb_ref[:, cols].astype(F32))
            o_ref[:, cols] = (_sigmoid(gc[s][...]) * yc_ref[:, cols].astype(F32)
                              + _sigmoid(gs[s][...]) * y_ssm).astype(BF16)

    return pl.pallas_call(
        body, name="merge_fwd", grid=(L // tm,),
        in_specs=[_row(h, c0 + s, tm) for s in range(4)] + [_row(D, tm=tm)] * 3,
        out_specs=_row(D, tm=tm), out_shape=jax.ShapeDtypeStruct((L, D), BF16),
        compiler_params=_params(("parallel",)),
    )(proj, proj, proj, proj, y_conv, ga, gb)


def residual_norm(x, m_out, gate, g, scale, shift):
    L, D = x.shape
    tm = _tm(L)

    def body(x_ref, m_ref, gt_ref, g_ref, sc_ref, sh_ref, h_ref, z_ref):
        h = x_ref[...] + gt_ref[...] * m_ref[...]
        h_ref[...] = h
        z_ref[...] = (h * _rms(h) * g_ref[...] * (1.0 + sc_ref[...]) + sh_ref[...]).astype(BF16)

    return pl.pallas_call(
        body, name="residual_norm", grid=(L // tm,),
        in_specs=[_row(D, tm=tm), _row(D, tm=tm), _vec(D), _vec(D), _vec(D), _vec(D)],
        out_specs=(_row(D, tm=tm), _row(D, tm=tm)),
        out_shape=(jax.ShapeDtypeStruct((L, D), F32), jax.ShapeDtypeStruct((L, D), BF16)),
        compiler_params=_params(("parallel",)),
    )(x, m_out, gate, g, scale, shift)


def loss_bwd(h1, ff, gate2, final_g, target):
    L, D = h1.shape
    tm = _tm(L)

    def body(h_ref, f_ref, gt_ref, g_ref, t_ref, dh_ref, dff_ref, loss_ref, dg_ref, dgt_ref):
        i = pl.program_id(0)
        ffv = f_ref[...]
        h2 = h_ref[...] + gt_ref[...] * ffv
        r = _rms(h2)
        n = h2 * r
        err = n * g_ref[...] - t_ref[...]
        per_tok = jnp.mean(err * err, axis=-1, keepdims=True)
        _accum(loss_ref, 0.5 * jnp.sum(per_tok, axis=0, keepdims=True), i)
        dy = err * (1.0 / D)
        _accum(dg_ref, _colsum(dy * n), i)
        dn = dy * g_ref[...]
        dh2 = r * (dn - n * jnp.mean(dn * n, axis=-1, keepdims=True))
        dh_ref[...] = dh2
        dff_ref[...] = (gt_ref[...] * dh2).astype(BF16)
        _accum(dgt_ref, _colsum(dh2 * ffv), i)

    return pl.pallas_call(
        body, name="loss_bwd", grid=(L // tm,),
        in_specs=[_row(D, tm=tm), _row(D, tm=tm), _vec(D), _vec(D), _row(D, tm=tm)],
        out_specs=(_row(D, tm=tm), _row(D, tm=tm), pl.BlockSpec((1, 1), lambda i: (0, 0)), _vec(D), _vec(D)),
        out_shape=(jax.ShapeDtypeStruct((L, D), F32), jax.ShapeDtypeStruct((L, D), BF16),
                   jax.ShapeDtypeStruct((1, 1), F32), jax.ShapeDtypeStruct((1, D), F32), jax.ShapeDtypeStruct((1, D), F32)),
        compiler_params=_params(("arbitrary",)),
    )(h1, ff, gate2, final_g, target)


def norm_bwd(dz, h, dh_in, g, scale, name, gate=None, m_out=None):
    L, D = h.shape
    tm = _tm(L)
    tail = gate is not None

    def body(*refs):
        dz_ref, h_ref, di_ref, g_ref, sc_ref = refs[:5]
        rest = refs[5:]
        if tail:
            gt_ref, m_ref = rest[:2]
            rest = rest[2:]
        dh_ref, dsc_ref, dsh_ref, dg_ref = rest[:4]
        i = pl.program_id(0)
        hv, dzv = h_ref[...], dz_ref[...].astype(F32)
        r = _rms(hv)
        n = hv * r
        _accum(dsc_ref, _colsum(dzv * n * g_ref[...]), i)
        _accum(dsh_ref, _colsum(dzv), i)
        dzn = dzv * (1.0 + sc_ref[...])
        _accum(dg_ref, _colsum(dzn * n), i)
        dn = dzn * g_ref[...]
        dh = di_ref[...] + r * (dn - n * jnp.mean(dn * n, axis=-1, keepdims=True))
        dh_ref[...] = dh
        if tail:
            dmo_ref, dgt_ref = rest[4:]
            dmo_ref[...] = (gt_ref[...] * dh).astype(BF16)
            _accum(dgt_ref, _colsum(dh * m_ref[...]), i)

    ins = [dz, h, dh_in, g, scale]
    in_specs = [_row(D, tm=tm)] * 3 + [_vec(D)] * 2
    out_specs = [_row(D, tm=tm), _vec(D), _vec(D), _vec(D)]
    out_shape = [jax.ShapeDtypeStruct((L, D), F32)] + [jax.ShapeDtypeStruct((1, D), F32)] * 3
    if tail:
        ins += [gate, m_out]
        in_specs += [_vec(D), _row(D, tm=tm)]
        out_specs += [_row(D, tm=tm), _vec(D)]
        out_shape += [jax.ShapeDtypeStruct((L, D), BF16), jax.ShapeDtypeStruct((1, D), F32)]
    return pl.pallas_call(
        body, name=name, grid=(L // tm,), in_specs=in_specs, out_specs=tuple(out_specs), out_shape=tuple(out_shape),
        compiler_params=_params(("arbitrary",)),
    )(*ins)


def merge_bwd(dmerged, proj, col_gc, y_conv, ga, gb):
    L, D = y_conv.shape
    tm = _tm(L)
    h = D // 2
    c0 = col_gc // h

    def body(dm_ref, p0, p1, p2, p3, yc_ref, ga_ref, gb_ref, dyc_ref, dga_ref, dgb_ref, dg_ref):
        gc, gs = (p0, p1), (p2, p3)
        for s in range(2):
            cols = slice(s * h, (s + 1) * h)
            dm = dm_ref[:, cols].astype(F32)
            sc, ss, sb = _sigmoid(gc[s][...]), _sigmoid(gs[s][...]), _sigmoid(gb_ref[:, cols].astype(F32))
            gav = ga_ref[:, cols].astype(F32)
            dyc_ref[:, cols] = (dm * sc).astype(BF16)
            dg_ref[:, cols] = (dm * yc_ref[:, cols].astype(F32) * sc * (1.0 - sc)).astype(BF16)
            dg_ref[:, D + s * h:D + (s + 1) * h] = (dm * gav * sb * ss * (1.0 - ss)).astype(BF16)
            dys = dm * ss
            dga_ref[:, cols] = (dys * sb).astype(BF16)
            dgb_ref[:, cols] = (dys * gav * sb * (1.0 - sb)).astype(BF16)

    return pl.pallas_call(
        body, name="merge_bwd", grid=(L // tm,),
        in_specs=[_row(D, tm=tm)] + [_row(h, c0 + s, tm) for s in range(4)] + [_row(D, tm=tm)] * 3,
        out_specs=(_row(D, tm=tm), _row(D, tm=tm), _row(D, tm=tm), _row(2 * D, tm=tm)),
        out_shape=(jax.ShapeDtypeStruct((L, D), BF16),) * 3 + (jax.ShapeDtypeStruct((L, 2 * D), BF16),),
        compiler_params=_params(("parallel",)),
    )(dmerged, proj, proj, proj, proj, y_conv, ga, gb)


def conv_ln_bwd(dvs, vc, ln_g, ln_b):
    L, C = vc.shape
    tm = _tm(L)

    def body(d_ref, v_ref, g_ref, b_ref, o_ref, dg_ref, db_ref):
        i = pl.program_id(0)
        v = v_ref[...]
        mu = jnp.mean(v, axis=-1, keepdims=True)
        d = v - mu
        rstd = lax.rsqrt(jnp.mean(d * d, axis=-1, keepdims=True) + EPS)
        xh = d * rstd
        ln = xh * g_ref[...] + b_ref[...]
        sg = _sigmoid(ln)
        dln = d_ref[...].astype(F32) * sg * (1.0 + ln * (1.0 - sg))
        _accum(dg_ref, _colsum(dln * xh), i)
        _accum(db_ref, _colsum(dln), i)
        dxh = dln * g_ref[...]
        o_ref[...] = rstd * (dxh - jnp.mean(dxh, axis=-1, keepdims=True)
                             - xh * jnp.mean(dxh * xh, axis=-1, keepdims=True))

    return pl.pallas_call(
        body, name="conv_ln_bwd", grid=(L // tm,),
        in_specs=[_row(C, tm=tm), _row(C, tm=tm), _vec(C), _vec(C)],
        out_specs=(_row(C, tm=tm), _vec(C), _vec(C)),
        out_shape=(jax.ShapeDtypeStruct((L, C), F32), jax.ShapeDtypeStruct((1, C), F32), jax.ShapeDtypeStruct((1, C), F32)),
        compiler_params=_params(("arbitrary",)),
    )(dvs, vc, ln_g, ln_b)


def conv_bwd(dvc, proj, w_dw):
    L, C = dvc.shape
    tm = _tm(L)
    hb = tm // HALO
    last = L // HALO - 1
    nt = L // tm

    def body(d_ref, dn_ref, a_ref, g_ref, ah_ref, gh_ref, w_ref, o_ref, dw_ref, db_ref, dbuf, vbuf, dsh, vsh, dw8):
        i = pl.program_id(0)
        dcur = d_ref[...]
        dbuf[0:tm, :] = dcur
        dbuf[tm:tm + HALO, :] = dn_ref[...] * jnp.where(i < nt - 1, 1.0, 0.0)
        av, sg = a_ref[...], _sigmoid(g_ref[...])
        vbuf[0:HALO, :] = ah_ref[...] * _sigmoid(gh_ref[...]) * jnp.where(i > 0, 1.0, 0.0)
        vbuf[HALO:HALO + tm, :] = av * sg
        _shifted_copies(dbuf, dsh, tm)
        _shifted_copies(vbuf, vsh, tm)
        dv = jnp.zeros((tm, C), F32)
        for k in range(CONV_K):
            dv = dv + w_ref[k:k + 1, :] * _window(dbuf, dsh, CONV_K - 1 - k, tm)
        o_ref[:, 0:C] = (dv * sg).astype(BF16)
        o_ref[:, C:2 * C] = (dv * av * sg * (1.0 - sg)).astype(BF16)

        @pl.when(i == 0)
        def _():
            dw8[...] = jnp.zeros_like(dw8)

        for k in range(CONV_K):
            prod = dcur * _window(vbuf, vsh, HALO - (CONV_K - 1) + k, tm)
            part = prod[0:8, :]
            for j in range(1, tm // 8):
                part = part + prod[8 * j:8 * j + 8, :]
            dw8[k] += part
        _accum(db_ref, _colsum(dcur), i)

        @pl.when(i == nt - 1)
        def _():
            for k in range(CONV_K):
                dw_ref[k:k + 1, :] = _colsum(dw8[k])
            dw_ref[CONV_K:HALO, :] = jnp.zeros((HALO - CONV_K, C), F32)

    prev = lambda cb: pl.BlockSpec((HALO, C), lambda i: (jnp.maximum(i * hb - 1, 0), cb))
    return pl.pallas_call(
        body, name="conv_bwd", grid=(nt,),
        in_specs=[_row(C, tm=tm), pl.BlockSpec((HALO, C), lambda i: (jnp.minimum((i + 1) * hb, last), 0)),
                  _row(C, 0, tm), _row(C, 1, tm), prev(0), prev(1), pl.BlockSpec((HALO, C), lambda i: (0, 0))],
        out_specs=(_row(2 * C, tm=tm), pl.BlockSpec((HALO, C), lambda i: (0, 0)), _vec(C)),
        out_shape=(jax.ShapeDtypeStruct((L, 2 * C), BF16), jax.ShapeDtypeStruct((HALO, C), F32),
                   jax.ShapeDtypeStruct((1, C), F32)),
        scratch_shapes=[pltpu.VMEM((tm + HALO, C), F32), pltpu.VMEM((HALO + tm, C), F32),
                        pltpu.VMEM((7, HALO + tm - 8, C), F32), pltpu.VMEM((7, HALO + tm - 8, C), F32),
                        pltpu.VMEM((CONV_K, 8, C), F32)],
        compiler_params=_params(("arbitrary",)),
    )(dvc, dvc, proj, proj, proj, proj, w_dw)


def _adamw(w, g, m, v):
    m = ADAM_B1 * m + (1.0 - ADAM_B1) * g
    v = ADAM_B2 * v + (1.0 - ADAM_B2) * (g * g)
    m_hat = m / (1.0 - ADAM_B1 ** ADAM_STEP)
    v_hat = v / (1.0 - ADAM_B2 ** ADAM_STEP)
    delta = -ADAM_LR * (m_hat / (jnp.sqrt(v_hat) + ADAM_EPS) + ADAM_WD * w)
    return delta, m, v


def _tile_rows(R, C):
    tr = 8
    while tr * 2 * C <= 128 * 1024 and R % (tr * 2) == 0:
        tr *= 2
    assert R % tr == 0, (R, C)
    return tr


def sum_devices(parts, name):
    _, R, C = parts.shape
    tr = _tile_rows(R, C)

    def body(p_ref, o_ref):
        s = p_ref[0]
        for j in range(1, NDEV):
            s = s + p_ref[j]
        o_ref[...] = s

    return pl.pallas_call(
        body, name=name, grid=(R // tr,),
        in_specs=[pl.BlockSpec((NDEV, tr, C), lambda i: (0, i, 0))],
        out_specs=pl.BlockSpec((tr, C), lambda i: (i, 0)), out_shape=jax.ShapeDtypeStruct((R, C), F32),
        compiler_params=_params(("parallel",)),
    )(parts)


def adam_update(w, g, m, v, name):
    R, C = w.shape
    tr = _tile_rows(R, C)

    def body(w_ref, g_ref, m_ref, v_ref, d_ref, mo_ref, vo_ref):
        d, mm, vv = _adamw(w_ref[...], g_ref[...], m_ref[...], v_ref[...])
        d_ref[...], mo_ref[...], vo_ref[...] = d, mm, vv

    spec = pl.BlockSpec((tr, C), lambda i: (i, 0))
    return pl.pallas_call(
        body, name=name, grid=(R // tr,), in_specs=[spec] * 4, out_specs=(spec,) * 3,
        out_shape=(jax.ShapeDtypeStruct((R, C), F32),) * 3, compiler_params=_params(("parallel",)),
    )(w, g, m, v)


def adam_many(items, name):
    n = len(items)

    def body(*refs):
        ins, outs = refs[:4 * n], refs[4 * n:]
        for i in range(n):
            w, g, m, v = [ins[4 * i + j][...] for j in range(4)]
            outs[3 * i][...], outs[3 * i + 1][...], outs[3 * i + 2][...] = _adamw(w, g, m, v)

    out_shape = [jax.ShapeDtypeStruct(w.shape, F32) for w, _, _, _ in items for _ in range(3)]
    res = pl.pallas_call(body, name=name, out_shape=tuple(out_shape), compiler_params=_params())(
        *[a for it in items for a in it])
    return [tuple(res[3 * i:3 * i + 3]) for i in range(n)]


def adam_reduce(parts, w, m, v, name):
    R, C = w.shape
    tr = _tile_rows(R, C)

    def body(p_ref, w_ref, m_ref, v_ref, g_ref, d_ref, mo_ref, vo_ref):
        g = p_ref[0].astype(F32)
        for j in range(1, NDEV):
            g = g + p_ref[j].astype(F32)
        g_ref[...] = g
        d, mm, vv = _adamw(w_ref[...], g, m_ref[...], v_ref[...])
        d_ref[...], mo_ref[...], vo_ref[...] = d, mm, vv

    spec = pl.BlockSpec((tr, C), lambda i: (i, 0))
    return pl.pallas_call(
        body, name=name, grid=(R // tr,),
        in_specs=[pl.BlockSpec((NDEV, tr, C), lambda i: (0, i, 0)), spec, spec, spec], out_specs=(spec,) * 4,
        out_shape=(jax.ShapeDtypeStruct((R, C), F32),) * 4, compiler_params=_params(("parallel",)),
    )(parts, w, m, v)


def adam_w_ada(c_act, dmod_cols, w, m, v):
    D, n = w.shape
    tn = 256

    def body(c_ref, dm_ref, w_ref, m_ref, v_ref, g_ref, d_ref, mo_ref, vo_ref):
        g = lax.dot_general(c_ref[...].astype(BF16), dm_ref[...].astype(BF16), (((0,), (0,)), ((), ())),
                            preferred_element_type=F32)
        g_ref[...] = g
        d, mm, vv = _adamw(w_ref[...], g, m_ref[...], v_ref[...])
        d_ref[...], mo_ref[...], vo_ref[...] = d, mm, vv

    spec = pl.BlockSpec((D, tn), lambda j: (0, j))
    return pl.pallas_call(
        body, name="adam_w_ada", grid=(n // tn,),
        in_specs=[pl.BlockSpec((NDEV, D), lambda j: (0, 0)), pl.BlockSpec((NDEV, tn), lambda j: (0, j)), spec, spec, spec],
        out_specs=(spec,) * 4, out_shape=(jax.ShapeDtypeStruct((D, n), F32),) * 4,
        compiler_params=_params(("parallel",)),
    )(c_act, dmod_cols, w, m, v)


def _block_diag(m):
    G, a, b = m.shape
    m4 = m.reshape(G // GB, GB, a, b)
    eye = jnp.eye(GB, dtype=m.dtype)
    return (m4[:, :, :, None, :] * eye[None, :, None, :, None]).reshape(G // GB, GB * a, GB * b)


def _diag_blocks(m, a, b):
    nb = m.shape[0]
    m5 = m.reshape(nb, GB, a, GB, b)
    idx = jnp.arange(GB)
    return m5[:, idx, :, idx, :].transpose(1, 0, 2, 3).reshape(nb * GB, a, b)


def _flat_pad(parts, mult):
    flat = jnp.concatenate([p.reshape(-1) for p in parts])
    pad = (-flat.shape[0]) % mult
    return jnp.pad(flat, (0, pad))


def _split(flat, like):
    out, off = [], 0
    for p in like:
        out.append(flat[off:off + p.size].reshape(p.shape))
        off += p.size
    return out


def kernel(x, c, w_ada, b_ada, norm1_g, w_in, w_dw, b_dw, ln_g, ln_b, w_conv_out, a_re, a_im, log_dt, b_re, b_im, c_re, c_im, d_skip, w_glu_a, w_glu_b, w_out, norm2_g, w_ff1, w_ff2, final_g, loss_target, m_w_ada, m_b_ada, m_norm1_g, m_w_in, m_w_dw, m_b_dw, m_ln_g, m_ln_b, m_w_conv_out, m_a_re, m_a_im, m_log_dt, m_b_re, m_b_im, m_c_re, m_c_im, m_d_skip, m_w_glu_a, m_w_glu_b, m_w_out, m_norm2_g, m_w_ff1, m_w_ff2, m_final_g, v_w_ada, v_b_ada, v_norm1_g, v_w_in, v_w_dw, v_b_dw, v_ln_g, v_ln_b, v_w_conv_out, v_a_re, v_a_im, v_log_dt, v_b_re, v_b_im, v_c_re, v_c_im, v_d_skip, v_w_glu_a, v_w_glu_b, v_w_out, v_norm2_g, v_w_ff1, v_w_ff2, v_final_g):
    W = dict(w_ada=w_ada, b_ada=b_ada, norm1_g=norm1_g, w_in=w_in, w_dw=w_dw, b_dw=b_dw, ln_g=ln_g, ln_b=ln_b,
             w_conv_out=w_conv_out, a_re=a_re, a_im=a_im, log_dt=log_dt, b_re=b_re, b_im=b_im, c_re=c_re, c_im=c_im,
             d_skip=d_skip, w_glu_a=w_glu_a, w_glu_b=w_glu_b, w_out=w_out, norm2_g=norm2_g, w_ff1=w_ff1, w_ff2=w_ff2,
             final_g=final_g)
    Mo = dict(w_ada=m_w_ada, b_ada=m_b_ada, norm1_g=m_norm1_g, w_in=m_w_in, w_dw=m_w_dw, b_dw=m_b_dw, ln_g=m_ln_g,
              ln_b=m_ln_b, w_conv_out=m_w_conv_out, a_re=m_a_re, a_im=m_a_im, log_dt=m_log_dt, b_re=m_b_re, b_im=m_b_im,
              c_re=m_c_re, c_im=m_c_im, d_skip=m_d_skip, w_glu_a=m_w_glu_a, w_glu_b=m_w_glu_b, w_out=m_w_out,
              norm2_g=m_norm2_g, w_ff1=m_w_ff1, w_ff2=m_w_ff2, final_g=m_final_g)
    Vo = dict(w_ada=v_w_ada, b_ada=v_b_ada, norm1_g=v_norm1_g, w_in=v_w_in, w_dw=v_w_dw, b_dw=v_b_dw, ln_g=v_ln_g,
              ln_b=v_ln_b, w_conv_out=v_w_conv_out, a_re=v_a_re, a_im=v_a_im, log_dt=v_log_dt, b_re=v_b_re, b_im=v_b_im,
              c_re=v_c_re, c_im=v_c_im, d_skip=v_d_skip, w_glu_a=v_w_glu_a, w_glu_b=v_w_glu_b, w_out=v_w_out,
              norm2_g=v_norm2_g, w_ff1=v_w_ff1, w_ff2=v_w_ff2, final_g=v_final_g)
    names = list(W)

    me = _me()
    xs, tgt = x[0], loss_target[0]
    L, D = xs.shape
    CW = w_dw.shape[2] * NDEV
    G, P = a_re.shape[1], a_re.shape[2]
    H = b_re.shape[3]
    n_ada = w_ada.shape[2]

    (c_all,) = _exchange([c], "gather_c", True)
    b_cols = lax.dynamic_slice(b_ada, (0, me * n_ada), (1, n_ada))
    mod_cols, c_act = adaln_mod(c_all.reshape(NDEV, D), w_ada[0], b_cols)
    (mod_all,) = _exchange([mod_cols], "gather_mod", True)
    mod = lax.dynamic_slice(mod_all, (0, me, 0), (NDEV, 1, n_ada)).reshape(6, 1, D)
    shift1, scale1, gate1, shift2, scale2, gate2 = [mod[j] for j in range(6)]

    big = ["w_in", "w_conv_out", "w_glu_a", "w_glu_b", "w_out", "w_ff1", "w_ff2"]
    order = ["w_in", "w_dw"] + big[1:]
    shards = {k: W[k][0].astype(BF16) for k in big}
    shards["w_dw"] = jnp.pad(w_dw[0], ((0, HALO - CONV_K), (0, 0)))
    gather_handle = dict(zip(order, gather2_start([shards[k] for k in order], "gather_weights_start", mod_all)))

    def forward(ks, name, after):
        gather_handle.update(zip(ks, gather2_forward([gather_handle[k] for k in ks], name, after)))

    def weight(k, after):
        w = gather2_wait(gather_handle[k], "gather_wait_" + k, after)
        if k in ("w_out", "w_ff2"):
            w = w.reshape(1, w.shape[0] * w.shape[1], w.shape[2])
        elif k in narrow:
            w = w.transpose(1, 0, 2).reshape(1, w.shape[1], NDEV * w.shape[2])
        elif k == "w_dw":
            w = w.transpose(1, 0, 2).reshape(HALO, CW)
        return w

    narrow = ("w_conv_out", "w_glu_a", "w_glu_b")
    scatter_handle = {}

    def scatter(k, g):
        if k in narrow:
            g = g.reshape(g.shape[1], NDEV, -1).transpose(1, 0, 2)
        elif g.shape[0] == 1:
            g = g.reshape(NDEV, -1, g.shape[2])
        (scatter_handle[k],), token = exchange_start([g], "scatter_start_" + k, False)
        return token

    big_out = {}

    def finish_weight(k, after):
        parts = exchange_wait(scatter_handle[k], after, "scatter_wait_" + k, False)
        big_out[k] = adam_reduce(parts, W[k][0], Mo[k][0], Vo[k][0], "adam_" + k)
        return big_out[k][1]

    u = prenorm(xs, norm1_g, scale1, shift1, "prenorm1")

    br2 = b_re[0].transpose(0, 2, 1).reshape(G * H, P)
    bi2 = b_im[0].transpose(0, 2, 1).reshape(G * H, P)
    ldt = log_dt[0].reshape(G, 1)
    expand = jnp.repeat(jnp.eye(G, dtype=F32), H, axis=0)
    lbr, lbi, bbr, bbi = s5_params(a_re[0], a_im[0], ldt, br2, bi2, expand)
    tabs = s5_tables(lbr.reshape(1, G * P), lbi.reshape(1, G * P))
    bdr, bdi = _block_diag(bbr.reshape(G, H, P)), _block_diag(bbi.reshape(G, H, P))
    cdr = _block_diag(c_re[0].transpose(0, 2, 1))
    cdi = _block_diag(c_im[0].transpose(0, 2, 1))
    cd2 = jnp.concatenate([cdr, -cdi], axis=1).astype(BF16)
    bdr3, bdi3 = _rhs3(bdr), _rhs3(bdi)
    bdt2 = jnp.concatenate([bdr.transpose(0, 2, 1), bdi.transpose(0, 2, 1)], axis=1).astype(BF16)
    cdrt3, cdit3 = _rhs3(cdr.transpose(0, 2, 1)), _rhs3(-cdi.transpose(0, 2, 1))

    forward(["w_in"], "gather_forward_in", (u, tabs, bdr3, bdi3, cd2, bdt2, cdrt3, cdit3))
    wg = {"w_in": weight("w_in", u)}
    proj = mm_nn(u, wg["w_in"], "in_proj")
    forward(["w_dw", "w_conv_out", "w_glu_a", "w_glu_b", "w_out"], "gather_forward_mix", proj)
    w_dw_full = weight("w_dw", proj)
    vs, vc = conv_fwd(proj, w_dw_full, b_dw, ln_g, ln_b)
    wg["w_conv_out"] = weight("w_conv_out", vs)
    y_conv = mm_nn(vs, wg["w_conv_out"], "conv_out", out_dtype=BF16)
    s_re, s_im, y_pre, yg = s5_fwd(proj, 2 * CW, bdr3, bdi3, cd2, tabs, d_skip)
    forward(["w_ff1"], "gather_forward_ff1", yg)
    wg["w_glu_a"] = weight("w_glu_a", yg)
    wg["w_glu_b"] = weight("w_glu_b", yg)
    ga = mm_nn(yg, wg["w_glu_a"], "glu_a", out_dtype=BF16)
    gb = mm_nn(yg, wg["w_glu_b"], "glu_b", out_dtype=BF16)
    merged = merge_fwd(proj, 3 * CW, y_conv, ga, gb)
    forward(["w_ff2"], "gather_forward_ff2", merged)
    wg["w_out"] = weight("w_out", merged)
    m_out = mm_nn(merged, wg["w_out"], "out_proj")
    h1, z = residual_norm(xs, m_out, gate1, norm2_g, scale2, shift2)
    wg["w_ff1"] = weight("w_ff1", z)
    act = mm_nn(z, wg["w_ff1"], "ff1", out_dtype=BF16, epi=lambda r: jnp.square(jnp.maximum(r, 0.0)))
    wg["w_ff2"] = weight("w_ff2", act)
    ff = mm_nn(act, wg["w_ff2"], "ff2")

    dh2, dff, loss_part, d_final_g, d_gate2 = loss_bwd(h1, ff, gate2, final_g.reshape(1, D), tgt)
    df = mm_nt(dff, wg["w_ff2"], "ff2_dx", out_dtype=BF16,
               epi=lambda r, a: r * (2.0 * jnp.sqrt(a.astype(F32))), extras=(act,))
    t = scatter("w_ff2", mm_tn(act, dff, 1, "ff2_dw", out_dtype=BF16))
    t = scatter("w_ff1", mm_tn(z, df, NDEV, "ff1_dw", out_dtype=BF16, dep=t))
    dz = mm_nt(df, wg["w_ff1"], "ff1_dx", out_dtype=BF16, dep=t)
    dh1, d_scale2, d_shift2, d_norm2_g, dmo, d_gate1 = norm_bwd(dz, h1, dh2, norm2_g, scale2, "norm2_bwd", gate1, m_out)
    t = scatter("w_out", mm_tn(merged, dmo, 1, "out_dw", out_dtype=BF16))
    dmerged = mm_nt(dmo, wg["w_out"], "out_dx", out_dtype=BF16, dep=t)
    dyc, dga, dgb, dproj_g = merge_bwd(dmerged, proj, 3 * CW, y_conv, ga, gb)
    t = scatter("w_conv_out", mm_tn(vs, dyc, 1, "conv_out_dw", out_dtype=BF16))
    t = scatter("w_glu_a", mm_tn(yg, dga, 1, "glu_a_dw", out_dtype=BF16, dep=t))
    t = scatter("w_glu_b", mm_tn(yg, dgb, 1, "glu_b_dw", out_dtype=BF16, dep=t))
    dvs = mm_nt(dyc, wg["w_conv_out"], "conv_out_dx", out_dtype=BF16, dep=t)
    dyg_a = mm_nt(dga, wg["w_glu_a"], "glu_a_dx", out_dtype=BF16, dep=t)
    dyg_b = mm_nt(dgb, wg["w_glu_b"], "glu_b_dx", out_dtype=BF16, dep=t)
    dvc, d_ln_g, d_ln_b = conv_ln_bwd(dvs, vc, ln_g, ln_b)
    dproj_c, d_w_dw, d_b_dw = conv_bwd(dvc, proj, w_dw_full)
    dproj_s, d_d_skip, dcdr, dcdi, dbdr, dbdi, dlr8, dli8 = s5_bwd(
        dyg_a, dyg_b, y_pre, proj, 2 * CW, s_re, s_im, bdt2, cdrt3, cdit3, tabs, d_skip)
    d_c_re = _diag_blocks(dcdr, H, P)
    d_c_im = _diag_blocks(dcdi, H, P)
    d_bbr = _diag_blocks(dbdr, H, P)
    d_bbi = _diag_blocks(dbdi, H, P)
    dlr = jnp.sum(dlr8, axis=0).reshape(G, P)
    dli = jnp.sum(dli8, axis=0).reshape(G, P)
    early_parts = [d_b_dw, d_ln_g, d_ln_b, dlr, dli, d_bbr, d_bbi, d_c_re, d_c_im, d_d_skip, d_norm2_g, d_final_g,
                   d_w_dw]
    pack_early = _flat_pad(early_parts, PACK).reshape(NDEV, -1, 1024)
    (early_scatter,), done = exchange_start([pack_early], "scatter_small_start", False)
    for k in ("w_ff2", "w_ff1", "w_out"):
        done = finish_weight(k, done)
    early_sum = sum_devices(exchange_wait(early_scatter, done, "scatter_small_wait", False), "sum_small_early")
    (early_gather,), done = exchange_start([early_sum], "gather_small_start", True)
    dproj = jnp.concatenate([dproj_c, dproj_s, dproj_g], axis=1)
    t = scatter("w_in", mm_tn(u, dproj, NDEV, "in_dw", out_dtype=BF16, dep=done))

    tot_early = exchange_wait(early_gather, t, "gather_small_wait", True)
    (g_b_dw, g_ln_g, g_ln_b, t_lr, t_li, t_bbr, t_bbi, g_c_re_t, g_c_im_t, g_d_skip,
     g_norm2_g, g_final_g, g_w_dw_full) = _split(tot_early.reshape(-1), early_parts)
    g_a_re, g_a_im, g_ldt, g_br2, g_bi2 = s5_params_bwd(
        a_re[0], a_im[0], ldt, br2, bi2, expand, t_lr, t_li, t_bbr.reshape(G * H, P), t_bbi.reshape(G * H, P))
    g_brt, g_bit = g_br2.reshape(G, H, P), g_bi2.reshape(G, H, P)
    g2 = {
        "b_dw": g_b_dw, "ln_g": g_ln_g, "ln_b": g_ln_b,
        "a_re": g_a_re, "a_im": g_a_im, "log_dt": g_ldt.reshape(1, G),
        "b_re": g_brt.transpose(0, 2, 1).reshape(G * P, H), "b_im": g_bit.transpose(0, 2, 1).reshape(G * P, H),
        "c_re": g_c_re_t.reshape(G * H, P), "c_im": g_c_im_t.reshape(G * H, P), "d_skip": g_d_skip,
        "norm2_g": g_norm2_g, "final_g": g_final_g,
        "w_dw": lax.dynamic_slice(g_w_dw_full, (0, me * (CW // NDEV)), (CONV_K, CW // NDEV)),
    }
    as2d = lambda k, a: a.reshape(g2[k].shape)
    grads, delta, new_m, new_v = {}, {}, {}, {}

    def adam_small(ks, name):
        outs = adam_many([(as2d(k, W[k]), g2[k], as2d(k, Mo[k]), as2d(k, Vo[k])) for k in ks], name)
        for k, o in zip(ks, outs):
            grads[k] = g2[k].reshape(W[k].shape)
            delta[k], new_m[k], new_v[k] = [a.reshape(W[k].shape) for a in o]
        return outs[0][0]

    after = adam_small([k for k in names if k in g2], "adam_small_early")
    for k in ("w_conv_out", "w_glu_a", "w_glu_b"):
        after = finish_weight(k, after)
    du = mm_nt(dproj, wg["w_in"], "in_dx", out_dtype=BF16, dep=after)
    grad_x, d_scale1, d_shift1, d_norm1_g = norm_bwd(du, xs, dh1, norm1_g, scale1, "norm1_bwd")

    dmod = jnp.concatenate([d_shift1, d_scale1, d_gate1, d_shift2, d_scale2, d_gate2], axis=1)
    late_parts = [dmod, d_norm1_g]
    pack_late = _flat_pad(late_parts, PACK).reshape(NDEV, -1, 1024)
    parts_late, dmod_from = _exchange([pack_late, dmod.reshape(NDEV, 1, n_ada)], "scatter_small_late", False)
    (tot_late,) = _exchange([sum_devices(parts_late, "sum_small_late")], "gather_small_late", True)
    g2["b_ada"], g2["norm1_g"] = _split(tot_late.reshape(-1), late_parts)
    adam_small(["b_ada", "norm1_g"], "adam_small_late")
    dmod_cols = dmod_from.reshape(NDEV, n_ada)

    g, d, mm, vv = adam_w_ada(c_act, dmod_cols, w_ada[0], m_w_ada[0], v_w_ada[0])
    grads["w_ada"], delta["w_ada"], new_m["w_ada"], new_v["w_ada"] = g[None], d[None], mm[None], vv[None]

    finish_weight("w_in", d)
    for k in big:
        g, d, mm, vv = big_out[k]
        grads[k], delta[k], new_m[k], new_v[k] = g[None], d[None], mm[None], vv[None]

    loss = lax.psum(loss_part[0, 0], ("x", "y", "c"))
    return (loss, grad_x[None], *[grads[k] for k in names], *[delta[k] for k in names],
            *[new_m[k] for k in names], *[new_v[k] for k in names])
```

```python
import functools
import math

import jax
import jax.numpy as jnp
from jax import lax
from jax.experimental import pallas as pl
from jax.experimental.pallas import tpu as pltpu

F32 = jnp.float32
BF16 = jnp.bfloat16
NDEV = 8
EPS = 1e-6
ADAM_LR, ADAM_B1, ADAM_B2, ADAM_EPS, ADAM_WD, ADAM_STEP = 0.001, 0.9, 0.999, 1e-08, 0.01, 10
CONV_K = 31
HALO = 32
GROUP = 16
STATE = 64
GB = 8
S5_ROWS = 512
HI = lax.Precision.HIGHEST
MESH = pl.DeviceIdType.MESH
VMEM_LIMIT = 56 * 1024 * 1024
MAX_CONTRACT = 2048
PACK_ROWS = 64
PACK = PACK_ROWS * 1024
ANY = pl.BlockSpec(memory_space=pl.ANY)


def _params(sem=None):
    if sem is None:
        return pltpu.CompilerParams(vmem_limit_bytes=VMEM_LIMIT)
    return pltpu.CompilerParams(dimension_semantics=sem, vmem_limit_bytes=VMEM_LIMIT)


def _sigmoid(v):
    return 1.0 / (1.0 + jnp.exp(-v))


def _me():
    return 4 * lax.axis_index("x") + 2 * lax.axis_index("y") + lax.axis_index("c")


def _peer(k):
    x, y, c = lax.axis_index("x"), lax.axis_index("y"), lax.axis_index("c")
    px = 1 - x if (k >> 2) & 1 else x
    py = 1 - y if (k >> 1) & 1 else y
    pc = 1 - c if k & 1 else c
    return (px, py, pc), 4 * px + 2 * py + pc


def _exchange(arrays, name, gather):
    n = len(arrays)
    out_shape = []
    for a in arrays:
        shp = (NDEV,) + a.shape if gather else a.shape
        out_shape.append(jax.ShapeDtypeStruct(shp, a.dtype))

    def body(*refs):
        ins, outs = refs[:n], refs[n:2 * n]
        send, recv, lsem = refs[2 * n:]
        me = _me()
        local = []
        for a in range(n):
            src = ins[a] if gather else ins[a].at[me]
            cp = pltpu.make_async_copy(src, outs[a].at[me], lsem.at[a])
            cp.start()
            local.append(cp)
        sends = []
        for a in range(n):
            for k in range(1, NDEV):
                dev, pidx = _peer(k)
                src = ins[a] if gather else ins[a].at[pidx]
                cp = pltpu.make_async_remote_copy(
                    src_ref=src, dst_ref=outs[a].at[me], send_sem=send.at[a * (NDEV - 1) + k - 1], recv_sem=recv.at[a * (NDEV - 1) + k - 1],
                    device_id=dev, device_id_type=MESH)
                cp.start()
                sends.append(cp)
        for a in range(n):
            for k in range(1, NDEV):
                dev, pidx = _peer(k)
                src = ins[a] if gather else ins[a].at[pidx]
                pltpu.make_async_remote_copy(
                    src_ref=src, dst_ref=outs[a].at[pidx], send_sem=send.at[a * (NDEV - 1) + k - 1], recv_sem=recv.at[a * (NDEV - 1) + k - 1],
                    device_id=dev, device_id_type=MESH).wait_recv()
        for cp in sends:
            cp.wait_send()
        for cp in local:
            cp.wait()

    return pl.pallas_call(
        body, name=name, out_shape=tuple(out_shape),
        in_specs=[ANY] * n, out_specs=tuple([ANY] * n),
        scratch_shapes=[pltpu.SemaphoreType.DMA((n * (NDEV - 1),)), pltpu.SemaphoreType.DMA((n * (NDEV - 1),)),
                        pltpu.SemaphoreType.DMA((n,))],
    )(*arrays)


HBM = pl.BlockSpec(memory_space=pltpu.HBM)
SEM = pl.BlockSpec(memory_space=pltpu.SEMAPHORE)
EFFECT = pltpu.SideEffectType.DATAFLOW_SIDE_EFFECTING
NPEER = NDEV - 1


def _landing(block_of_me, shape, dtype):
    land = lax.empty((NDEV,) + tuple(shape), dtype)
    start = (_me(),) + (0,) * len(shape)
    return pltpu.with_memory_space_constraint(lax.dynamic_update_slice(land, block_of_me[None], start), pltpu.HBM)


def exchange_start(arrays, name, gather, after=None):
    n = len(arrays)
    me = _me()
    deps = () if after is None else (after,)
    lands = []
    for a in arrays:
        if gather:
            lands.append(_landing(a, a.shape, a.dtype))
        else:
            mine = lax.dynamic_slice(a, (me,) + (0,) * (a.ndim - 1), (1,) + a.shape[1:])[0]
            lands.append(_landing(mine, a.shape[1:], a.dtype))
    srcs = [pltpu.with_memory_space_constraint(a, pltpu.HBM) for a in arrays]

    def body(*refs):
        ins, lnd = refs[:n], refs[n:2 * n]
        outs = refs[2 * n + len(deps):]
        sends, recvs, token = outs[:n], outs[n:2 * n], outs[-1]
        my = _me()
        for a in range(n):
            for k in range(1, NDEV):
                dev, pidx = _peer(k)
                src = ins[a] if gather else ins[a].at[pidx]
                pltpu.make_async_remote_copy(
                    src_ref=src, dst_ref=lnd[a].at[my], send_sem=sends[a].at[k - 1], recv_sem=recvs[a].at[k - 1],
                    device_id=dev, device_id_type=MESH).start()
        token[...] = jnp.zeros_like(token)

    out_shape = ([pltpu.SemaphoreType.DMA((NPEER,))] * (2 * n)
                 + [pltpu.HBM(a.shape, a.dtype) for a in srcs] + [pltpu.HBM(l.shape, l.dtype) for l in lands]
                 + [jax.ShapeDtypeStruct((8, 128), F32)])
    res = pl.pallas_call(
        body, name=name, out_shape=tuple(out_shape),
        in_specs=[HBM] * (2 * n) + [ANY] * len(deps),
        out_specs=tuple([SEM] * (2 * n) + [HBM] * (2 * n) + [pl.BlockSpec(memory_space=pltpu.VMEM)]),
        input_output_aliases={i: 2 * n + i for i in range(2 * n)},
        compiler_params=pltpu.CompilerParams(has_side_effects=EFFECT),
    )(*srcs, *lands, *deps)
    handles = [(res[a], res[n + a], res[2 * n + a], res[3 * n + a]) for a in range(n)]
    return handles, res[-1]


def exchange_wait(handle, after, name, gather):
    send_sem, recv_sem, src, land = handle

    def body(src_ref, land_ref, s_ref, r_ref, after_ref, src_out, land_out):
        for k in range(1, NDEV):
            dev, pidx = _peer(k)
            s = src_ref if gather else src_ref.at[pidx]
            cp = pltpu.make_async_remote_copy(
                src_ref=s, dst_ref=land_ref.at[pidx], send_sem=s_ref.at[k - 1], recv_sem=r_ref.at[k - 1],
                device_id=dev, device_id_type=MESH)
            cp.wait_send()
            cp.wait_recv()

    return pl.pallas_call(
        body, name=name, out_shape=(pltpu.HBM(src.shape, src.dtype), pltpu.HBM(land.shape, land.dtype)),
        in_specs=(HBM, HBM, SEM, SEM, ANY), out_specs=(HBM, HBM), input_output_aliases={0: 0, 1: 1},
        compiler_params=pltpu.CompilerParams(has_side_effects=EFFECT),
    )(src, land, send_sem, recv_sem, after)[1]


ICI_PEERS = (2, 4, 6)
SIBLING = 1


def gather2_start(blocks, name, after):
    m = len(blocks)
    lands = [_landing(b, b.shape, b.dtype) for b in blocks]
    srcs = [pltpu.with_memory_space_constraint(b, pltpu.HBM) for b in blocks]
    n = len(ICI_PEERS)

    def body(*refs):
        src, lnd = refs[:m], refs[m:2 * m]
        outs = refs[2 * m + 1:]
        send, recv_sib, recv_ici = outs[:m], outs[m:2 * m], outs[2 * m:3 * m]
        my = _me()
        for a in range(m):
            dev, _ = _peer(SIBLING)
            pltpu.make_async_remote_copy(src_ref=src[a], dst_ref=lnd[a].at[my], send_sem=send[a].at[0],
                                         recv_sem=recv_sib[a].at[0], device_id=dev, device_id_type=MESH).start()
            for j, k in enumerate(ICI_PEERS):
                dev, _ = _peer(k)
                pltpu.make_async_remote_copy(src_ref=src[a], dst_ref=lnd[a].at[my], send_sem=send[a].at[1 + j],
                                             recv_sem=recv_ici[a].at[j], device_id=dev, device_id_type=MESH).start()

    out_shape = ([pltpu.SemaphoreType.DMA((1 + n,))] * m + [pltpu.SemaphoreType.DMA((1,))] * m
                 + [pltpu.SemaphoreType.DMA((n,))] * m
                 + [pltpu.HBM(s.shape, s.dtype) for s in srcs] + [pltpu.HBM(l.shape, l.dtype) for l in lands])
    res = pl.pallas_call(
        body, name=name, out_shape=tuple(out_shape),
        in_specs=[HBM] * (2 * m) + [ANY], out_specs=tuple([SEM] * (3 * m) + [HBM] * (2 * m)),
        input_output_aliases={i: 3 * m + i for i in range(2 * m)},
        compiler_params=pltpu.CompilerParams(has_side_effects=EFFECT),
    )(*srcs, *lands, after)
    return [tuple(res[g * m + a] for g in range(5)) for a in range(m)]


def gather2_forward(handles, name, after):
    m = len(handles)
    n = len(ICI_PEERS)
    srcs, lands = [h[3] for h in handles], [h[4] for h in handles]
    deps = tuple(after) if isinstance(after, (tuple, list)) else (after,)

    def body(*refs):
        src, lnd, recv_ici = refs[:m], refs[m:2 * m], refs[2 * m:3 * m]
        outs = refs[3 * m + len(deps):]
        fsend, frecv = outs[:m], outs[m:2 * m]
        sib, _ = _peer(SIBLING)
        for a in range(m):
            for j, k in enumerate(ICI_PEERS):
                dev, pidx = _peer(k)
                pltpu.make_async_remote_copy(
                    src_ref=src[a], dst_ref=lnd[a].at[pidx], send_sem=fsend[a].at[j], recv_sem=recv_ici[a].at[j],
                    device_id=dev, device_id_type=MESH).wait_recv()
                pltpu.make_async_remote_copy(
                    src_ref=lnd[a].at[pidx], dst_ref=lnd[a].at[pidx], send_sem=fsend[a].at[j], recv_sem=frecv[a].at[j],
                    device_id=sib, device_id_type=MESH).start()

    out_shape = ([pltpu.SemaphoreType.DMA((n,))] * (2 * m)
                 + [pltpu.HBM(s.shape, s.dtype) for s in srcs] + [pltpu.HBM(l.shape, l.dtype) for l in lands])
    res = pl.pallas_call(
        body, name=name, out_shape=tuple(out_shape),
        in_specs=[HBM] * (2 * m) + [SEM] * m + [ANY] * len(deps),
        out_specs=tuple([SEM] * (2 * m) + [HBM] * (2 * m)),
        input_output_aliases={i: 2 * m + i for i in range(2 * m)},
        compiler_params=pltpu.CompilerParams(has_side_effects=EFFECT),
    )(*srcs, *lands, *[h[2] for h in handles], *deps)
    return [(handles[a][0], handles[a][1], res[a], res[m + a], res[2 * m + a], res[3 * m + a]) for a in range(m)]


def gather2_wait(handle, name, after):
    send, recv_sib, fsend, frecv, src, land = handle

    def body(src_ref, land_ref, send_ref, recv_sib_ref, fsend_ref, frecv_ref, after_ref, src_out, land_out):
        sib, sib_idx = _peer(SIBLING)
        own = pltpu.make_async_remote_copy(src_ref=src_ref, dst_ref=land_ref.at[sib_idx], send_sem=send_ref.at[0],
                                           recv_sem=recv_sib_ref.at[0], device_id=sib, device_id_type=MESH)
        own.wait_send()
        own.wait_recv()
        for j, k in enumerate(ICI_PEERS):
            dev, pidx = _peer(k)
            pltpu.make_async_remote_copy(src_ref=src_ref, dst_ref=land_ref.at[pidx], send_sem=send_ref.at[1 + j],
                                         recv_sem=frecv_ref.at[j], device_id=dev, device_id_type=MESH).wait_send()
            _, fidx = _peer(k ^ SIBLING)
            fwd = pltpu.make_async_remote_copy(src_ref=land_ref.at[pidx], dst_ref=land_ref.at[fidx],
                                               send_sem=fsend_ref.at[j], recv_sem=frecv_ref.at[j],
                                               device_id=sib, device_id_type=MESH)
            fwd.wait_send()
            fwd.wait_recv()

    return pl.pallas_call(
        body, name=name, out_shape=(pltpu.HBM(src.shape, src.dtype), pltpu.HBM(land.shape, land.dtype)),
        in_specs=(HBM, HBM, SEM, SEM, SEM, SEM, ANY), out_specs=(HBM, HBM), input_output_aliases={0: 0, 1: 1},
        compiler_params=pltpu.CompilerParams(has_side_effects=EFFECT),
    )(src, land, send, recv_sib, fsend, frecv, after)[1]


def _acc_steps(p, acc, k, nk, finish):
    if nk == 1:
        finish(p)
        return

    @pl.when(k == 0)
    def _():
        acc[...] = p

    @pl.when(k > 0)
    def _():
        acc[...] += p

    @pl.when(k == nk - 1)
    def _():
        finish(acc[...])


def _shards_per_step(J, n, tn, width):
    s = 1
    while n == tn and J % (2 * s) == 0 and 2 * s * tn <= width:
        s *= 2
    return s


def mm_nn(a, w3, name, out_dtype=F32, epi=None, extras=()):
    M, K = a.shape
    J, _, n = w3.shape
    tm, tn, tk = min(1024, M), min(1024, n), min(2048, K)
    q, nk, ne = n // tn, K // tk, len(extras)
    s = _shards_per_step(J, n, tn, 1024)

    def body(*refs):
        a_ref, w_ref = refs[:2]
        ex, o_ref, acc = refs[2:2 + ne], refs[2 + ne], refs[-1]
        av = a_ref[...]
        p = jnp.dot(av, w_ref[0], preferred_element_type=F32)
        if s > 1:
            p = jnp.concatenate([p] + [jnp.dot(av, w_ref[j], preferred_element_type=F32) for j in range(1, s)], axis=1)

        def finish(r):
            if epi is not None:
                r = epi(r, *[e[...] for e in ex])
            o_ref[...] = r.astype(out_dtype)

        _acc_steps(p, acc, pl.program_id(2), nk, finish)

    return pl.pallas_call(
        body, name=name, grid=(M // tm, (J // s) * q, nk),
        in_specs=[pl.BlockSpec((tm, tk), lambda i, j, k: (i, k)),
                  pl.BlockSpec((s, tk, tn), lambda i, j, k: (j // q, k, j % q))]
        + [pl.BlockSpec((tm, s * tn), lambda i, j, k: (i, j))] * ne,
        out_specs=pl.BlockSpec((tm, s * tn), lambda i, j, k: (i, j)),
        out_shape=jax.ShapeDtypeStruct((M, J * n), out_dtype),
        scratch_shapes=[pltpu.VMEM((tm, s * tn), F32)],
        compiler_params=_params(("parallel", "parallel", "arbitrary")),
    )(a, w3, *extras)


def mm_nt(dy, w3, name, out_dtype=F32, epi=None, extras=(), dep=None):
    M, _ = dy.shape
    J, K, n = w3.shape
    tm, tn, tkk = min(1024, M), min(MAX_CONTRACT, n), min(1024, K)
    q, ne = n // tn, len(extras)
    s = 1
    while q == 1 and J % (2 * s) == 0 and 2 * s * tn <= MAX_CONTRACT:
        s *= 2
    nk = (J // s) * q
    deps = () if dep is None else (dep,)

    def body(*refs):
        d_ref, w_ref = refs[:2]
        ex, o_ref, acc = refs[2:2 + ne], refs[-2], refs[-1]
        nt = (((1,), (1,)), ((), ()))
        p = lax.dot_general(d_ref[:, 0:tn], w_ref[0], nt, preferred_element_type=F32)
        for j in range(1, s):
            p = p + lax.dot_general(d_ref[:, j * tn:(j + 1) * tn], w_ref[j], nt, preferred_element_type=F32)

        def finish(r):
            if epi is not None:
                r = epi(r, *[e[...] for e in ex])
            o_ref[...] = r.astype(out_dtype)

        _acc_steps(p, acc, pl.program_id(2), nk, finish)

    return pl.pallas_call(
        body, name=name, grid=(M // tm, K // tkk, nk),
        in_specs=[pl.BlockSpec((tm, s * tn), lambda i, kk, c: (i, c)),
                  pl.BlockSpec((s, tkk, tn), lambda i, kk, c: (c // q, kk, c % q))]
        + [pl.BlockSpec((tm, tkk), lambda i, kk, c: (i, kk))] * ne + [ANY] * len(deps),
        out_specs=pl.BlockSpec((tm, tkk), lambda i, kk, c: (i, kk)),
        out_shape=jax.ShapeDtypeStruct((M, K), out_dtype),
        scratch_shapes=[pltpu.VMEM((tm, tkk), F32)],
        compiler_params=_params(("parallel", "parallel", "arbitrary")),
    )(dy, w3, *extras, *deps)


def mm_tn(a, dy, J, name, out_dtype=F32, dep=None):
    M, K = a.shape
    n = dy.shape[1] // J
    tm, tn, tkk = min(MAX_CONTRACT, M), min(1024, n), min(1024, K)
    q, nk = n // tn, M // tm
    s = _shards_per_step(J, n, tn, 1024)
    deps = () if dep is None else (dep,)

    def body(a_ref, d_ref, *rest):
        o_ref, acc = rest[-2:]
        p = lax.dot_general(a_ref[...], d_ref[...], (((0,), (0,)), ((), ())), preferred_element_type=F32)

        def finish(r):
            for j in range(s):
                o_ref[j] = r[:, j * tn:(j + 1) * tn].astype(out_dtype)

        _acc_steps(p, acc, pl.program_id(2), nk, finish)

    return pl.pallas_call(
        body, name=name, grid=(K // tkk, (J // s) * q, nk),
        in_specs=[pl.BlockSpec((tm, tkk), lambda kk, c, m: (m, kk)),
                  pl.BlockSpec((tm, s * tn), lambda kk, c, m: (m, c))] + [ANY] * len(deps),
        out_specs=pl.BlockSpec((s, tkk, tn), lambda kk, c, m: (c // q, kk, c % q)),
        out_shape=jax.ShapeDtypeStruct((J, K, n), out_dtype),
        scratch_shapes=[pltpu.VMEM((tkk, s * tn), F32)],
        compiler_params=_params(("parallel", "parallel", "arbitrary")),
    )(a, dy, *deps)


def _tm(L):
    return min(256, L)


def _row(w, cb=0, tm=None):
    return pl.BlockSpec((tm, w), lambda i: (i, cb))


def _vec(w, cb=0):
    return pl.BlockSpec((1, w), lambda i: (0, cb))


def _accum(ref, val, i):
    @pl.when(i == 0)
    def _():
        ref[...] = val

    @pl.when(i > 0)
    def _():
        ref[...] += val


def _colsum(v):
    return jnp.sum(v, axis=0, keepdims=True)


def _rms(v):
    return lax.rsqrt(jnp.mean(v * v, axis=-1, keepdims=True) + EPS)


def adaln_mod(c_all, w_ada, b_cols):
    B, D = c_all.shape
    n = w_ada.shape[1]
    tn = 512

    def body(c_ref, w_ref, b_ref, o_ref, ca_ref):
        cv = c_ref[...]
        ca = cv * _sigmoid(cv)
        ca_ref[...] = ca
        o_ref[...] = jnp.dot(ca.astype(BF16), w_ref[...].astype(BF16), preferred_element_type=F32) + b_ref[...]

    return pl.pallas_call(
        body, name="adaln_mod", grid=(n // tn,),
        in_specs=[pl.BlockSpec((B, D), lambda j: (0, 0)), pl.BlockSpec((D, tn), lambda j: (0, j)),
                  pl.BlockSpec((1, tn), lambda j: (0, j))],
        out_specs=(pl.BlockSpec((B, tn), lambda j: (0, j)), pl.BlockSpec((B, D), lambda j: (0, 0))),
        out_shape=(jax.ShapeDtypeStruct((B, n), F32), jax.ShapeDtypeStruct((B, D), F32)),
        compiler_params=_params(("arbitrary",)),
    )(c_all, w_ada, b_cols)


def prenorm(x, g, scale, shift, name):
    L, D = x.shape
    tm = _tm(L)

    def body(x_ref, g_ref, sc_ref, sh_ref, u_ref):
        xv = x_ref[...]
        u_ref[...] = (xv * _rms(xv) * g_ref[...] * (1.0 + sc_ref[...]) + sh_ref[...]).astype(BF16)

    return pl.pallas_call(
        body, name=name, grid=(L // tm,),
        in_specs=[_row(D, tm=tm), _vec(D), _vec(D), _vec(D)],
        out_specs=_row(D, tm=tm), out_shape=jax.ShapeDtypeStruct((L, D), BF16),
        compiler_params=_params(("parallel",)),
    )(x, g, scale, shift)


def _shifted_copies(buf, shifted, tm):
    n = HALO + tm - 8
    for r in range(1, 8):
        shifted[r - 1] = buf[pl.ds(r, n), :]


def _window(buf, shifted, off, tm):
    r, base = off % 8, off - off % 8
    if r == 0:
        return buf[pl.ds(base, tm), :]
    return shifted[r - 1, pl.ds(base, tm), :]


def conv_fwd(proj, w_dw, b_dw, ln_g, ln_b):
    L = proj.shape[0]
    C = w_dw.shape[1]
    tm = _tm(L)
    hb = tm // HALO

    def body(a_ref, g_ref, ah_ref, gh_ref, w_ref, b_ref, lg_ref, lb_ref, vs_ref, vc_ref, buf, shifted):
        i = pl.program_id(0)
        halo = ah_ref[...] * _sigmoid(gh_ref[...])
        buf[0:HALO, :] = halo * jnp.where(i > 0, 1.0, 0.0)
        buf[HALO:HALO + tm, :] = a_ref[...] * _sigmoid(g_ref[...])
        _shifted_copies(buf, shifted, tm)
        acc = jnp.zeros((tm, C), F32) + b_ref[...]
        for k in range(CONV_K):
            acc = acc + w_ref[k:k + 1, :] * _window(buf, shifted, HALO - (CONV_K - 1) + k, tm)
        vc_ref[...] = acc
        mu = jnp.mean(acc, axis=-1, keepdims=True)
        d = acc - mu
        var = jnp.mean(d * d, axis=-1, keepdims=True)
        ln = d * lax.rsqrt(var + EPS) * lg_ref[...] + lb_ref[...]
        vs_ref[...] = (ln * _sigmoid(ln)).astype(BF16)

    prev = lambda cb: pl.BlockSpec((HALO, C), lambda i: (jnp.maximum(i * hb - 1, 0), cb))
    return pl.pallas_call(
        body, name="conv_fwd", grid=(L // tm,),
        in_specs=[_row(C, 0, tm), _row(C, 1, tm), prev(0), prev(1),
                  pl.BlockSpec((HALO, C), lambda i: (0, 0)), _vec(C), _vec(C), _vec(C)],
        out_specs=(_row(C, tm=tm), _row(C, tm=tm)),
        out_shape=(jax.ShapeDtypeStruct((L, C), BF16), jax.ShapeDtypeStruct((L, C), F32)),
        scratch_shapes=[pltpu.VMEM((HALO + tm, C), F32), pltpu.VMEM((7, HALO + tm - 8, C), F32)],
        compiler_params=_params(("parallel",)),
    )(proj, proj, proj, proj, w_dw, b_dw, ln_g, ln_b)


def _gelu(v):
    return 0.5 * v * (1.0 + jnp.tanh(math.sqrt(2.0 / math.pi) * (v + 0.044715 * v * v * v)))


def _gelu_grad(v):
    k = math.sqrt(2.0 / math.pi)
    t = jnp.tanh(k * (v + 0.044715 * v * v * v))
    return 0.5 * (1.0 + t) + 0.5 * v * (1.0 - t * t) * k * (1.0 + 3.0 * 0.044715 * v * v)


def s5_param_fn(ar, ai, ldt, br, bi, expand):
    dt = jnp.exp(ldt)
    er = jnp.exp(ar * dt)
    th = ai * dt
    lbr, lbi = er * jnp.cos(th), er * jnp.sin(th)
    nr, ni = lbr - 1.0, lbi
    den = ar * ar + ai * ai
    qr, qi = (nr * ar + ni * ai) / den, (ni * ar - nr * ai) / den
    qre = jnp.dot(expand, qr, precision=HI, preferred_element_type=F32)
    qie = jnp.dot(expand, qi, precision=HI, preferred_element_type=F32)
    return lbr, lbi, qre * br - qie * bi, qre * bi + qie * br


def s5_params(ar, ai, ldt, br2, bi2, expand):
    def body(ar_ref, ai_ref, ld_ref, br_ref, bi_ref, e_ref, o1, o2, o3, o4):
        r = s5_param_fn(ar_ref[...], ai_ref[...], ld_ref[...], br_ref[...], bi_ref[...], e_ref[...])
        o1[...], o2[...], o3[...], o4[...] = r

    s2, s3 = jax.ShapeDtypeStruct(ar.shape, F32), jax.ShapeDtypeStruct(br2.shape, F32)
    return pl.pallas_call(body, name="s5_params", out_shape=(s2, s2, s3, s3), compiler_params=_params())(
        ar, ai, ldt, br2, bi2, expand)


def s5_params_bwd(ar, ai, ldt, br2, bi2, expand, dlr, dli, dbr, dbi):
    def body(ar_ref, ai_ref, ld_ref, br_ref, bi_ref, e_ref, c1, c2, c3, c4, o1, o2, o3, o4, o5):
        e = e_ref[...]
        fn = lambda a, b, c, d, f: s5_param_fn(a, b, c, d, f, e)
        _, vjp = jax.vjp(fn, ar_ref[...], ai_ref[...], ld_ref[...], br_ref[...], bi_ref[...])
        r = vjp((c1[...], c2[...], c3[...], c4[...]))
        o1[...], o2[...], o3[...], o4[...], o5[...] = r

    shapes = tuple(jax.ShapeDtypeStruct(v.shape, F32) for v in (ar, ai, ldt, br2, bi2))
    return pl.pallas_call(body, name="s5_params_bwd", out_shape=shapes, compiler_params=_params())(
        ar, ai, ldt, br2, bi2, expand, dlr, dli, dbr, dbi)


def s5_tables(lr, li):
    C = lr.shape[1]

    def body(lr_ref, li_ref, o_ref):
        row = lax.broadcasted_iota(jnp.int32, (8, C), 0)
        for rev in (0, 1):
            pr = jnp.broadcast_to(lr_ref[...], (8, C))
            pi = jnp.broadcast_to(-li_ref[...] if rev else li_ref[...], (8, C))
            br, bi = pr, pi
            pows = [(pr, pi)]
            for _ in range(7):
                pr, pi = pr * br - pi * bi, pr * bi + pi * br
                pows.append((pr, pi))
            base = 8 * rev
            for s, d in enumerate((1, 2, 4)):
                keep = (row + d <= 7) if rev else (row >= d)
                o_ref[base + 2 * s] = jnp.where(keep, pows[d - 1][0], 0.0)
                o_ref[base + 2 * s + 1] = jnp.where(keep, pows[d - 1][1], 0.0)
            cr, ci = jnp.zeros((8, C), F32), jnp.zeros((8, C), F32)
            for j in range(8):
                e = (8 - j) if rev else (j + 1)
                cr = jnp.where(row == j, pows[e - 1][0], cr)
                ci = jnp.where(row == j, pows[e - 1][1], ci)
            o_ref[base + 6] = cr
            o_ref[base + 7] = ci

    return pl.pallas_call(body, name="s5_tables", out_shape=jax.ShapeDtypeStruct((16, 8, C), F32),
                          compiler_params=_params())(lr, li)


def _tile_steps(xr, xi, tabs, rev):
    for s, d in enumerate((1, 2, 4)):
        tr, ti = tabs[2 * s], tabs[2 * s + 1]
        sh = (8 - d) if rev else d
        sr, si = pltpu.roll(xr, sh, 0), pltpu.roll(xi, sh, 0)
        xr, xi = xr + tr * sr - ti * si, xi + tr * si + ti * sr
    return xr, xi


def _tile_carry(xr, xi, tabs, cr, ci):
    tr, ti = tabs[6], tabs[7]
    return xr + tr * cr - ti * ci, xi + tr * ci + ti * cr


def _hi_lo(a):
    hi = a.astype(BF16)
    return hi, (a - hi.astype(F32)).astype(BF16)


def _lhs3(a):
    hi, lo = _hi_lo(a)
    return jnp.concatenate([hi, lo, hi], axis=1)


def _rhs3(m):
    hi, lo = _hi_lo(m)
    return jnp.concatenate([hi, hi, lo], axis=-2)


def s5_fwd(proj, col0, bdr3, bdi3, cd2, tabs, d_skip):
    L = proj.shape[0]
    nb, cw3, sw = bdr3.shape
    cw = cw3 // 3
    tl = min(S5_ROWS, L)
    cb0 = col0 // cw

    def body(u_ref, bdr_ref, bdi_ref, cd_ref, t_ref, dk_ref, sr_ref, si_ref, yp_ref, yg_ref, car):
        l = pl.program_id(1)

        @pl.when(l == 0)
        def _():
            car[...] = jnp.zeros_like(car)

        u = u_ref[...]
        u3 = _lhs3(u)
        sr_ref[...] = jnp.dot(u3, bdr_ref[...], preferred_element_type=F32)
        si_ref[...] = jnp.dot(u3, bdi_ref[...], preferred_element_type=F32)

        def pair(i, c):
            tabs = [t_ref[j] for j in range(8)]
            r0 = pl.multiple_of(i * 16, 16)
            lo, hi = pl.ds(r0, 8), pl.ds(r0 + 8, 8)
            a = _tile_steps(sr_ref[lo, :], si_ref[lo, :], tabs, False)
            b = _tile_steps(sr_ref[hi, :], si_ref[hi, :], tabs, False)
            ar, ai = _tile_carry(a[0], a[1], tabs, c[0], c[1])
            br, bi = _tile_carry(b[0], b[1], tabs, ar[7:8, :], ai[7:8, :])
            sr_ref[lo, :], si_ref[lo, :] = ar, ai
            sr_ref[hi, :], si_ref[hi, :] = br, bi
            return br[7:8, :], bi[7:8, :]

        c = lax.fori_loop(0, tl // 16, pair, (car[0:1, :], car[1:2, :]))
        car[0:1, :] = c[0]
        car[1:2, :] = c[1]
        s2 = jnp.concatenate([sr_ref[...].astype(BF16), si_ref[...].astype(BF16)], axis=1)
        y = jnp.dot(s2, cd_ref[...], preferred_element_type=F32) + dk_ref[...] * u
        yp_ref[...] = y
        yg_ref[...] = _gelu(y).astype(BF16)

    blk = lambda r, c: pl.BlockSpec((None, r, c), lambda b, l: (b, 0, 0))
    return pl.pallas_call(
        body, name="s5_fwd", grid=(nb, L // tl),
        in_specs=[pl.BlockSpec((tl, cw), lambda b, l: (l, cb0 + b)), blk(cw3, sw), blk(cw3, sw), blk(2 * sw, cw),
                  pl.BlockSpec((8, 8, sw), lambda b, l: (0, 0, b)), pl.BlockSpec((1, cw), lambda b, l: (0, b))],
        out_specs=(pl.BlockSpec((tl, sw), lambda b, l: (l, b)), pl.BlockSpec((tl, sw), lambda b, l: (l, b)),
                   pl.BlockSpec((tl, cw), lambda b, l: (l, b)), pl.BlockSpec((tl, cw), lambda b, l: (l, b))),
        out_shape=(jax.ShapeDtypeStruct((L, nb * sw), F32), jax.ShapeDtypeStruct((L, nb * sw), F32),
                   jax.ShapeDtypeStruct((L, nb * cw), F32), jax.ShapeDtypeStruct((L, nb * cw), BF16)),
        scratch_shapes=[pltpu.VMEM((8, sw), F32)],
        compiler_params=_params(("parallel", "arbitrary")),
    )(proj, bdr3, bdi3, cd2, tabs, d_skip)


def s5_bwd(dyg_a, dyg_b, yp, proj, col0, s_re, s_im, bdt2, cdrt3, cdit3, tabs, d_skip):
    L = proj.shape[0]
    nb, sw2, cw = bdt2.shape
    sw = sw2 // 2
    tl = min(S5_ROWS, L)
    nl = L // tl
    cb0 = col0 // cw
    tb = tl // 8

    def body(da_ref, db_ref, yp_ref, u_ref, sr_ref, si_ref, hr_ref, hi_ref, bdt_ref, cdrt_ref, cdit_ref,
             t_ref, dk_ref, du_ref, ddk_ref, dcr_ref, dci_ref, dbr_ref, dbi_ref, dlr_ref, dli_ref,
             gr, gi, pr, pi, car):
        l = pl.program_id(1)
        first = l == nl - 1

        @pl.when(l == 0)
        def _():
            car[...] = jnp.zeros_like(car)

        u = u_ref[...]
        dy = (da_ref[...].astype(F32) + db_ref[...].astype(F32)) * _gelu_grad(yp_ref[...])
        dy3 = _lhs3(dy)
        gr[...] = jnp.dot(dy3, cdrt_ref[...], preferred_element_type=F32)
        gi[...] = jnp.dot(dy3, cdit_ref[...], preferred_element_type=F32)
        inner = jnp.where(first, 0.0, 1.0)
        pr[0:8, :] = hr_ref[...] * inner
        pi[0:8, :] = hi_ref[...] * inner
        pr[8:8 + tl, :] = sr_ref[...]
        pi[8:8 + tl, :] = si_ref[...]
        row = lax.broadcasted_iota(jnp.int32, (8, sw), 0)

        def pair(j, c):
            tabs = [t_ref[8 + k] for k in range(8)]
            r0 = pl.multiple_of((tb // 2 - 1 - j) * 16, 16)
            lo, hi = pl.ds(r0, 8), pl.ds(r0 + 8, 8)
            b = _tile_steps(gr[hi, :], gi[hi, :], tabs, True)
            a = _tile_steps(gr[lo, :], gi[lo, :], tabs, True)
            br, bi = _tile_carry(b[0], b[1], tabs, c[0], c[1])
            ar, ai = _tile_carry(a[0], a[1], tabs, br[0:1, :], bi[0:1, :])
            gr[lo, :], gi[lo, :] = ar, ai
            gr[hi, :], gi[hi, :] = br, bi
            p0r, p1r, p2r = [pltpu.roll(pr[pl.ds(r0 + 8 * n, 8), :], 1, 0) for n in range(3)]
            p0i, p1i, p2i = [pltpu.roll(pi[pl.ds(r0 + 8 * n, 8), :], 1, 0) for n in range(3)]
            qar, qai = jnp.where(row == 0, p0r, p1r), jnp.where(row == 0, p0i, p1i)
            qbr, qbi = jnp.where(row == 0, p1r, p2r), jnp.where(row == 0, p1i, p2i)
            return (ar[0:1, :], ai[0:1, :], c[2] + (ar * qar + ai * qai) + (br * qbr + bi * qbi),
                    c[3] + (ai * qar - ar * qai) + (bi * qbr - br * qbi))

        z = jnp.zeros((8, sw), F32)
        c = lax.fori_loop(0, tb // 2, pair, (car[0:1, :], car[1:2, :], z, z))
        car[0:1, :] = c[0]
        car[1:2, :] = c[1]
        g_re, g_im = gr[...].astype(BF16), gi[...].astype(BF16)
        g2 = jnp.concatenate([g_re, g_im], axis=1)
        du_ref[...] = (dy * dk_ref[...] + jnp.dot(g2, bdt_ref[...], preferred_element_type=F32)).astype(BF16)
        tn = (((0,), (0,)), ((), ()))
        dyb, ub = dy.astype(BF16), u.astype(BF16)
        _accum(ddk_ref, _colsum(dy * u), l)
        _accum(dcr_ref, lax.dot_general(dyb, sr_ref[...].astype(BF16), tn, preferred_element_type=F32), l)
        _accum(dci_ref, -lax.dot_general(dyb, si_ref[...].astype(BF16), tn, preferred_element_type=F32), l)
        _accum(dbr_ref, lax.dot_general(ub, g_re, tn, preferred_element_type=F32), l)
        _accum(dbi_ref, lax.dot_general(ub, g_im, tn, preferred_element_type=F32), l)
        _accum(dlr_ref, c[2], l)
        _accum(dli_ref, c[3], l)

    rl = lambda l: nl - 1 - l
    cblk = lambda w, off=0: pl.BlockSpec((tl, w), lambda b, l: (rl(l), off + b))
    halo = pl.BlockSpec((8, sw), lambda b, l: (jnp.maximum(rl(l) * tb - 1, 0), b))
    mat = lambda r, c: pl.BlockSpec((None, r, c), lambda b, l: (b, 0, 0))
    return pl.pallas_call(
        body, name="s5_bwd", grid=(nb, nl),
        in_specs=[cblk(cw), cblk(cw), cblk(cw), cblk(cw, cb0), cblk(sw), cblk(sw), halo, halo,
                  mat(2 * sw, cw), mat(3 * cw, sw), mat(3 * cw, sw),
                  pl.BlockSpec((16, 8, sw), lambda b, l: (0, 0, b)), pl.BlockSpec((1, cw), lambda b, l: (0, b))],
        out_specs=(cblk(cw), pl.BlockSpec((1, cw), lambda b, l: (0, b)), mat(cw, sw), mat(cw, sw), mat(cw, sw), mat(cw, sw),
                   pl.BlockSpec((8, sw), lambda b, l: (0, b)), pl.BlockSpec((8, sw), lambda b, l: (0, b))),
        out_shape=(jax.ShapeDtypeStruct((L, nb * cw), BF16), jax.ShapeDtypeStruct((1, nb * cw), F32),
                   jax.ShapeDtypeStruct((nb, cw, sw), F32), jax.ShapeDtypeStruct((nb, cw, sw), F32),
                   jax.ShapeDtypeStruct((nb, cw, sw), F32), jax.ShapeDtypeStruct((nb, cw, sw), F32),
                   jax.ShapeDtypeStruct((8, nb * sw), F32), jax.ShapeDtypeStruct((8, nb * sw), F32)),
        scratch_shapes=[pltpu.VMEM((tl, sw), F32), pltpu.VMEM((tl, sw), F32),
                        pltpu.VMEM((tl + 8, sw), F32), pltpu.VMEM((tl + 8, sw), F32), pltpu.VMEM((8, sw), F32)],
        compiler_params=_params(("parallel", "arbitrary")),
    )(dyg_a, dyg_b, yp, proj, s_re, s_im, s_re, s_im, bdt2, cdrt3, cdit3, tabs, d_skip)


def merge_fwd(proj, col_gc, y_conv, ga, gb):
    L, D = y_conv.shape
    tm = _tm(L)
    h = D // 2
    c0 = col_gc // h

    def body(p0, p1, p2, p3, yc_ref, ga_ref, gb_ref, o_ref):
        gc, gs = (p0, p1), (p2, p3)
        for s in range(2):
            cols = slice(s * h, (s + 1) * h)
            y_ssm = ga_ref[:, cols].astype(F32) * _sigmoid(gb_ref[:, cols].astype(F32))
            o_ref[:, cols] = (_sigmoid(gc[s][...]) * yc_ref[:, cols].astype(F32)
                              + _sigmoid(gs[s][...]) * y_ssm).astype(BF16)

    return pl.pallas_call(
        body, name="merge_fwd", grid=(L // tm,),
        in_specs=[_row(h, c0 + s, tm) for s in range(4)] + [_row(D, tm=tm)] * 3,
        out_specs=_row(D, tm=tm), out_shape=jax.ShapeDtypeStruct((L, D), BF16),
        compiler_params=_params(("parallel",)),
    )(proj, proj, proj, proj, y_conv, ga, gb)


def residual_norm(x, m_out, gate, g, scale, shift):
    L, D = x.shape
    tm = _tm(L)

    def body(x_ref, m_ref, gt_ref, g_ref, sc_ref, sh_ref, h_ref, z_ref):
        h = x_ref[...] + gt_ref[...] * m_ref[...]
        h_ref[...] = h
        z_ref[...] = (h * _rms(h) * g_ref[...] * (1.0 + sc_ref[...]) + sh_ref[...]).astype(BF16)

    return pl.pallas_call(
        body, name="residual_norm", grid=(L // tm,),
        in_specs=[_row(D, tm=tm), _row(D, tm=tm), _vec(D), _vec(D), _vec(D), _vec(D)],
        out_specs=(_row(D, tm=tm), _row(D, tm=tm)),
        out_shape=(jax.ShapeDtypeStruct((L, D), F32), jax.ShapeDtypeStruct((L, D), BF16)),
        compiler_params=_params(("parallel",)),
    )(x, m_out, gate, g, scale, shift)


def loss_bwd(h1, ff, gate2, final_g, target):
    L, D = h1.shape
    tm = _tm(L)

    def body(h_ref, f_ref, gt_ref, g_ref, t_ref, dh_ref, dff_ref, loss_ref, dg_ref, dgt_ref):
        i = pl.program_id(0)
        ffv = f_ref[...]
        h2 = h_ref[...] + gt_ref[...] * ffv
        r = _rms(h2)
        n = h2 * r
        err = n * g_ref[...] - t_ref[...]
        per_tok = jnp.mean(err * err, axis=-1, keepdims=True)
        _accum(loss_ref, 0.5 * jnp.sum(per_tok, axis=0, keepdims=True), i)
        dy = err * (1.0 / D)
        _accum(dg_ref, _colsum(dy * n), i)
        dn = dy * g_ref[...]
        dh2 = r * (dn - n * jnp.mean(dn * n, axis=-1, keepdims=True))
        dh_ref[...] = dh2
        dff_ref[...] = (gt_ref[...] * dh2).astype(BF16)
        _accum(dgt_ref, _colsum(dh2 * ffv), i)

    return pl.pallas_call(
        body, name="loss_bwd", grid=(L // tm,),
        in_specs=[_row(D, tm=tm), _row(D, tm=tm), _vec(D), _vec(D), _row(D, tm=tm)],
        out_specs=(_row(D, tm=tm), _row(D, tm=tm), pl.BlockSpec((1, 1), lambda i: (0, 0)), _vec(D), _vec(D)),
        out_shape=(jax.ShapeDtypeStruct((L, D), F32), jax.ShapeDtypeStruct((L, D), BF16),
                   jax.ShapeDtypeStruct((1, 1), F32), jax.ShapeDtypeStruct((1, D), F32), jax.ShapeDtypeStruct((1, D), F32)),
        compiler_params=_params(("arbitrary",)),
    )(h1, ff, gate2, final_g, target)


def norm_bwd(dz, h, dh_in, g, scale, name, gate=None, m_out=None):
    L, D = h.shape
    tm = _tm(L)
    tail = gate is not None

    def body(*refs):
        dz_ref, h_ref, di_ref, g_ref, sc_ref = refs[:5]
        rest = refs[5:]
        if tail:
            gt_ref, m_ref = rest[:2]
            rest = rest[2:]
        dh_ref, dsc_ref, dsh_ref, dg_ref = rest[:4]
        i = pl.program_id(0)
        hv, dzv = h_ref[...], dz_ref[...].astype(F32)
        r = _rms(hv)
        n = hv * r
        _accum(dsc_ref, _colsum(dzv * n * g_ref[...]), i)
        _accum(dsh_ref, _colsum(dzv), i)
        dzn = dzv * (1.0 + sc_ref[...])
        _accum(dg_ref, _colsum(dzn * n), i)
        dn = dzn * g_ref[...]
        dh = di_ref[...] + r * (dn - n * jnp.mean(dn * n, axis=-1, keepdims=True))
        dh_ref[...] = dh
        if tail:
            dmo_ref, dgt_ref = rest[4:]
            dmo_ref[...] = (gt_ref[...] * dh).astype(BF16)
            _accum(dgt_ref, _colsum(dh * m_ref[...]), i)

    ins = [dz, h, dh_in, g, scale]
    in_specs = [_row(D, tm=tm)] * 3 + [_vec(D)] * 2
    out_specs = [_row(D, tm=tm), _vec(D), _vec(D), _vec(D)]
    out_shape = [jax.ShapeDtypeStruct((L, D), F32)] + [jax.ShapeDtypeStruct((1, D), F32)] * 3
    if tail:
        ins += [gate, m_out]
        in_specs += [_vec(D), _row(D, tm=tm)]
        out_specs += [_row(D, tm=tm), _vec(D)]
        out_shape += [jax.ShapeDtypeStruct((L, D), BF16), jax.ShapeDtypeStruct((1, D), F32)]
    return pl.pallas_call(
        body, name=name, grid=(L // tm,), in_specs=in_specs, out_specs=tuple(out_specs), out_shape=tuple(out_shape),
        compiler_params=_params(("arbitrary",)),
    )(*ins)


def merge_bwd(dmerged, proj, col_gc, y_conv, ga, gb):
    L, D = y_conv.shape
    tm = _tm(L)
    h = D // 2
    c0 = col_gc // h

    def body(dm_ref, p0, p1, p2, p3, yc_ref, ga_ref, gb_ref, dyc_ref, dga_ref, dgb_ref, dg_ref):
        gc, gs = (p0, p1), (p2, p3)
        for s in range(2):
            cols = slice(s * h, (s + 1) * h)
            dm = dm_ref[:, cols].astype(F32)
            sc, ss, sb = _sigmoid(gc[s][...]), _sigmoid(gs[s][...]), _sigmoid(gb_ref[:, cols].astype(F32))
            gav = ga_ref[:, cols].astype(F32)
            dyc_ref[:, cols] = (dm * sc).astype(BF16)
            dg_ref[:, cols] = (dm * yc_ref[:, cols].astype(F32) * sc * (1.0 - sc)).astype(BF16)
            dg_ref[:, D + s * h:D + (s + 1) * h] = (dm * gav * sb * ss * (1.0 - ss)).astype(BF16)
            dys = dm * ss
            dga_ref[:, cols] = (dys * sb).astype(BF16)
            dgb_ref[:, cols] = (dys * gav * sb * (1.0 - sb)).astype(BF16)

    return pl.pallas_call(
        body, name="merge_bwd", grid=(L // tm,),
        in_specs=[_row(D, tm=tm)] + [_row(h, c0 + s, tm) for s in range(4)] + [_row(D, tm=tm)] * 3,
        out_specs=(_row(D, tm=tm), _row(D, tm=tm), _row(D, tm=tm), _row(2 * D, tm=tm)),
        out_shape=(jax.ShapeDtypeStruct((L, D), BF16),) * 3 + (jax.ShapeDtypeStruct((L, 2 * D), BF16),),
        compiler_params=_params(("parallel",)),
    )(dmerged, proj, proj, proj, proj, y_conv, ga, gb)


def conv_ln_bwd(dvs, vc, ln_g, ln_b):
    L, C = vc.shape
    tm = _tm(L)

    def body(d_ref, v_ref, g_ref, b_ref, o_ref, dg_ref, db_ref):
        i = pl.program_id(0)
        v = v_ref[...]
        mu = jnp.mean(v, axis=-1, keepdims=True)
        d = v - mu
        rstd = lax.rsqrt(jnp.mean(d * d, axis=-1, keepdims=True) + EPS)
        xh = d * rstd
        ln = xh * g_ref[...] + b_ref[...]
        sg = _sigmoid(ln)
        dln = d_ref[...].astype(F32) * sg * (1.0 + ln * (1.0 - sg))
        _accum(dg_ref, _colsum(dln * xh), i)
        _accum(db_ref, _colsum(dln), i)
        dxh = dln * g_ref[...]
        o_ref[...] = rstd * (dxh - jnp.mean(dxh, axis=-1, keepdims=True)
                             - xh * jnp.mean(dxh * xh, axis=-1, keepdims=True))

    return pl.pallas_call(
        body, name="conv_ln_bwd", grid=(L // tm,),
        in_specs=[_row(C, tm=tm), _row(C, tm=tm), _vec(C), _vec(C)],
        out_specs=(_row(C, tm=tm), _vec(C), _vec(C)),
        out_shape=(jax.ShapeDtypeStruct((L, C), F32), jax.ShapeDtypeStruct((1, C), F32), jax.ShapeDtypeStruct((1, C), F32)),
        compiler_params=_params(("arbitrary",)),
    )(dvs, vc, ln_g, ln_b)


def conv_bwd(dvc, proj, w_dw, dproj_s, dproj_g):
    L, C = dvc.shape
    ws, wgt = dproj_s.shape[1], dproj_g.shape[1]
    tm = _tm(L)
    hb = tm // HALO
    last = L // HALO - 1
    nt = L // tm

    def body(d_ref, dn_ref, a_ref, g_ref, ah_ref, gh_ref, w_ref, ps_ref, pg_ref, o_ref, dw_ref, db_ref,
             dbuf, vbuf, dsh, vsh, dw8):
        i = pl.program_id(0)
        o_ref[:, 2 * C:2 * C + ws] = ps_ref[...]
        o_ref[:, 2 * C + ws:2 * C + ws + wgt] = pg_ref[...]
        dcur = d_ref[...]
        dbuf[0:tm, :] = dcur
        dbuf[tm:tm + HALO, :] = dn_ref[...] * jnp.where(i < nt - 1, 1.0, 0.0)
        av, sg = a_ref[...], _sigmoid(g_ref[...])
        vbuf[0:HALO, :] = ah_ref[...] * _sigmoid(gh_ref[...]) * jnp.where(i > 0, 1.0, 0.0)
        vbuf[HALO:HALO + tm, :] = av * sg
        _shifted_copies(dbuf, dsh, tm)
        _shifted_copies(vbuf, vsh, tm)
        dv = jnp.zeros((tm, C), F32)
        for k in range(CONV_K):
            dv = dv + w_ref[k:k + 1, :] * _window(dbuf, dsh, CONV_K - 1 - k, tm)
        o_ref[:, 0:C] = (dv * sg).astype(BF16)
        o_ref[:, C:2 * C] = (dv * av * sg * (1.0 - sg)).astype(BF16)

        @pl.when(i == 0)
        def _():
            dw8[...] = jnp.zeros_like(dw8)

        for k in range(CONV_K):
            prod = dcur * _window(vbuf, vsh, HALO - (CONV_K - 1) + k, tm)
            part = prod[0:8, :]
            for j in range(1, tm // 8):
                part = part + prod[8 * j:8 * j + 8, :]
            dw8[k] += part
        _accum(db_ref, _colsum(dcur), i)

        @pl.when(i == nt - 1)
        def _():
            for k in range(CONV_K):
                dw_ref[k:k + 1, :] = _colsum(dw8[k])
            dw_ref[CONV_K:HALO, :] = jnp.zeros((HALO - CONV_K, C), F32)

    prev = lambda cb: pl.BlockSpec((HALO, C), lambda i: (jnp.maximum(i * hb - 1, 0), cb))
    return pl.pallas_call(
        body, name="conv_bwd", grid=(nt,),
        in_specs=[_row(C, tm=tm), pl.BlockSpec((HALO, C), lambda i: (jnp.minimum((i + 1) * hb, last), 0)),
                  _row(C, 0, tm), _row(C, 1, tm), prev(0), prev(1), pl.BlockSpec((HALO, C), lambda i: (0, 0)),
                  _row(ws, tm=tm), _row(wgt, tm=tm)],
        out_specs=(_row(2 * C + ws + wgt, tm=tm), pl.BlockSpec((HALO, C), lambda i: (0, 0)), _vec(C)),
        out_shape=(jax.ShapeDtypeStruct((L, 2 * C + ws + wgt), BF16), jax.ShapeDtypeStruct((HALO, C), F32),
                   jax.ShapeDtypeStruct((1, C), F32)),
        scratch_shapes=[pltpu.VMEM((tm + HALO, C), F32), pltpu.VMEM((HALO + tm, C), F32),
                        pltpu.VMEM((7, HALO + tm - 8, C), F32), pltpu.VMEM((7, HALO + tm - 8, C), F32),
                        pltpu.VMEM((CONV_K, 8, C), F32)],
        compiler_params=_params(("arbitrary",)),
    )(dvc, dvc, proj, proj, proj, proj, w_dw, dproj_s, dproj_g)


def _adamw(w, g, m, v):
    m = ADAM_B1 * m + (1.0 - ADAM_B1) * g
    v = ADAM_B2 * v + (1.0 - ADAM_B2) * (g * g)
    m_hat = m / (1.0 - ADAM_B1 ** ADAM_STEP)
    v_hat = v / (1.0 - ADAM_B2 ** ADAM_STEP)
    delta = -ADAM_LR * (m_hat / (jnp.sqrt(v_hat) + ADAM_EPS) + ADAM_WD * w)
    return delta, m, v


def _tile_rows(R, C):
    tr = 8
    while tr * 2 * C <= 128 * 1024 and R % (tr * 2) == 0:
        tr *= 2
    assert R % tr == 0, (R, C)
    return tr


def sum_devices(parts, name):
    _, R, C = parts.shape
    tr = _tile_rows(R, C)

    def body(p_ref, o_ref):
        s = p_ref[0]
        for j in range(1, NDEV):
            s = s + p_ref[j]
        o_ref[...] = s

    return pl.pallas_call(
        body, name=name, grid=(R // tr,),
        in_specs=[pl.BlockSpec((NDEV, tr, C), lambda i: (0, i, 0))],
        out_specs=pl.BlockSpec((tr, C), lambda i: (i, 0)), out_shape=jax.ShapeDtypeStruct((R, C), F32),
        compiler_params=_params(("parallel",)),
    )(parts)


def adam_update(w, g, m, v, name):
    R, C = w.shape
    tr = _tile_rows(R, C)

    def body(w_ref, g_ref, m_ref, v_ref, d_ref, mo_ref, vo_ref):
        d, mm, vv = _adamw(w_ref[...], g_ref[...], m_ref[...], v_ref[...])
        d_ref[...], mo_ref[...], vo_ref[...] = d, mm, vv

    spec = pl.BlockSpec((tr, C), lambda i: (i, 0))
    return pl.pallas_call(
        body, name=name, grid=(R // tr,), in_specs=[spec] * 4, out_specs=(spec,) * 3,
        out_shape=(jax.ShapeDtypeStruct((R, C), F32),) * 3, compiler_params=_params(("parallel",)),
    )(w, g, m, v)


def adam_many(items, name):
    n = len(items)

    def body(*refs):
        ins, outs = refs[:4 * n], refs[4 * n:]
        for i in range(n):
            w, g, m, v = [ins[4 * i + j][...] for j in range(4)]
            outs[3 * i][...], outs[3 * i + 1][...], outs[3 * i + 2][...] = _adamw(w, g, m, v)

    out_shape = [jax.ShapeDtypeStruct(w.shape, F32) for w, _, _, _ in items for _ in range(3)]
    res = pl.pallas_call(body, name=name, out_shape=tuple(out_shape), compiler_params=_params())(
        *[a for it in items for a in it])
    return [tuple(res[3 * i:3 * i + 3]) for i in range(n)]


def adam_reduce(parts, w, m, v, name):
    R, C = w.shape
    tr = _tile_rows(R, C)

    def body(p_ref, w_ref, m_ref, v_ref, g_ref, d_ref, mo_ref, vo_ref):
        g = p_ref[0].astype(F32)
        for j in range(1, NDEV):
            g = g + p_ref[j].astype(F32)
        g_ref[...] = g
        d, mm, vv = _adamw(w_ref[...], g, m_ref[...], v_ref[...])
        d_ref[...], mo_ref[...], vo_ref[...] = d, mm, vv

    spec = pl.BlockSpec((tr, C), lambda i: (i, 0))
    return pl.pallas_call(
        body, name=name, grid=(R // tr,),
        in_specs=[pl.BlockSpec((NDEV, tr, C), lambda i: (0, i, 0)), spec, spec, spec], out_specs=(spec,) * 4,
        out_shape=(jax.ShapeDtypeStruct((R, C), F32),) * 4, compiler_params=_params(("parallel",)),
    )(parts, w, m, v)


def adam_w_ada(c_act, dmod_cols, w, m, v):
    D, n = w.shape
    tn = 256

    def body(c_ref, dm_ref, w_ref, m_ref, v_ref, g_ref, d_ref, mo_ref, vo_ref):
        g = lax.dot_general(c_ref[...].astype(BF16), dm_ref[...].astype(BF16), (((0,), (0,)), ((), ())),
                            preferred_element_type=F32)
        g_ref[...] = g
        d, mm, vv = _adamw(w_ref[...], g, m_ref[...], v_ref[...])
        d_ref[...], mo_ref[...], vo_ref[...] = d, mm, vv

    spec = pl.BlockSpec((D, tn), lambda j: (0, j))
    return pl.pallas_call(
        body, name="adam_w_ada", grid=(n // tn,),
        in_specs=[pl.BlockSpec((NDEV, D), lambda j: (0, 0)), pl.BlockSpec((NDEV, tn), lambda j: (0, j)), spec, spec, spec],
        out_specs=(spec,) * 4, out_shape=(jax.ShapeDtypeStruct((D, n), F32),) * 4,
        compiler_params=_params(("parallel",)),
    )(c_act, dmod_cols, w, m, v)


def _block_diag(m):
    G, a, b = m.shape
    m4 = m.reshape(G // GB, GB, a, b)
    eye = jnp.eye(GB, dtype=m.dtype)
    return (m4[:, :, :, None, :] * eye[None, :, None, :, None]).reshape(G // GB, GB * a, GB * b)


def _diag_blocks(m, a, b):
    nb = m.shape[0]
    m5 = m.reshape(nb, GB, a, GB, b)
    idx = jnp.arange(GB)
    return m5[:, idx, :, idx, :].transpose(1, 0, 2, 3).reshape(nb * GB, a, b)


def _flat_pad(parts, mult):
    flat = jnp.concatenate([p.reshape(-1) for p in parts])
    pad = (-flat.shape[0]) % mult
    return jnp.pad(flat, (0, pad))


def _split(flat, like):
    out, off = [], 0
    for p in like:
        out.append(flat[off:off + p.size].reshape(p.shape))
        off += p.size
    return out


def kernel(x, c, w_ada, b_ada, norm1_g, w_in, w_dw, b_dw, ln_g, ln_b, w_conv_out, a_re, a_im, log_dt, b_re, b_im, c_re, c_im, d_skip, w_glu_a, w_glu_b, w_out, norm2_g, w_ff1, w_ff2, final_g, loss_target, m_w_ada, m_b_ada, m_norm1_g, m_w_in, m_w_dw, m_b_dw, m_ln_g, m_ln_b, m_w_conv_out, m_a_re, m_a_im, m_log_dt, m_b_re, m_b_im, m_c_re, m_c_im, m_d_skip, m_w_glu_a, m_w_glu_b, m_w_out, m_norm2_g, m_w_ff1, m_w_ff2, m_final_g, v_w_ada, v_b_ada, v_norm1_g, v_w_in, v_w_dw, v_b_dw, v_ln_g, v_ln_b, v_w_conv_out, v_a_re, v_a_im, v_log_dt, v_b_re, v_b_im, v_c_re, v_c_im, v_d_skip, v_w_glu_a, v_w_glu_b, v_w_out, v_norm2_g, v_w_ff1, v_w_ff2, v_final_g):
    W = dict(w_ada=w_ada, b_ada=b_ada, norm1_g=norm1_g, w_in=w_in, w_dw=w_dw, b_dw=b_dw, ln_g=ln_g, ln_b=ln_b,
             w_conv_out=w_conv_out, a_re=a_re, a_im=a_im, log_dt=log_dt, b_re=b_re, b_im=b_im, c_re=c_re, c_im=c_im,
             d_skip=d_skip, w_glu_a=w_glu_a, w_glu_b=w_glu_b, w_out=w_out, norm2_g=norm2_g, w_ff1=w_ff1, w_ff2=w_ff2,
             final_g=final_g)
    Mo = dict(w_ada=m_w_ada, b_ada=m_b_ada, norm1_g=m_norm1_g, w_in=m_w_in, w_dw=m_w_dw, b_dw=m_b_dw, ln_g=m_ln_g,
              ln_b=m_ln_b, w_conv_out=m_w_conv_out, a_re=m_a_re, a_im=m_a_im, log_dt=m_log_dt, b_re=m_b_re, b_im=m_b_im,
              c_re=m_c_re, c_im=m_c_im, d_skip=m_d_skip, w_glu_a=m_w_glu_a, w_glu_b=m_w_glu_b, w_out=m_w_out,
              norm2_g=m_norm2_g, w_ff1=m_w_ff1, w_ff2=m_w_ff2, final_g=m_final_g)
    Vo = dict(w_ada=v_w_ada, b_ada=v_b_ada, norm1_g=v_norm1_g, w_in=v_w_in, w_dw=v_w_dw, b_dw=v_b_dw, ln_g=v_ln_g,
              ln_b=v_ln_b, w_conv_out=v_w_conv_out, a_re=v_a_re, a_im=v_a_im, log_dt=v_log_dt, b_re=v_b_re, b_im=v_b_im,
              c_re=v_c_re, c_im=v_c_im, d_skip=v_d_skip, w_glu_a=v_w_glu_a, w_glu_b=v_w_glu_b, w_out=v_w_out,
              norm2_g=v_norm2_g, w_ff1=v_w_ff1, w_ff2=v_w_ff2, final_g=v_final_g)
    names = list(W)

    me = _me()
    xs, tgt = x[0], loss_target[0]
    L, D = xs.shape
    CW = w_dw.shape[2] * NDEV
    G, P = a_re.shape[1], a_re.shape[2]
    H = b_re.shape[3]
    n_ada = w_ada.shape[2]

    (c_all,) = _exchange([c], "gather_c", True)
    b_cols = lax.dynamic_slice(b_ada, (0, me * n_ada), (1, n_ada))
    mod_cols, c_act = adaln_mod(c_all.reshape(NDEV, D), w_ada[0], b_cols)
    (mod_all,) = _exchange([mod_cols], "gather_mod", True)
    mod = lax.dynamic_slice(mod_all, (0, me, 0), (NDEV, 1, n_ada)).reshape(6, 1, D)
    shift1, scale1, gate1, shift2, scale2, gate2 = [mod[j] for j in range(6)]

    big = ["w_in", "w_conv_out", "w_glu_a", "w_glu_b", "w_out", "w_ff1", "w_ff2"]
    order = ["w_in", "w_dw"] + big[1:]
    shards = {k: W[k][0].astype(BF16) for k in big}
    shards["w_dw"] = jnp.pad(w_dw[0], ((0, HALO - CONV_K), (0, 0)))
    gather_handle = dict(zip(order, gather2_start([shards[k] for k in order], "gather_weights_start", mod_all)))

    def forward(ks, name, after):
        gather_handle.update(zip(ks, gather2_forward([gather_handle[k] for k in ks], name, after)))

    def weight(k, after):
        w = gather2_wait(gather_handle[k], "gather_wait_" + k, after)
        if k in ("w_out", "w_ff2"):
            w = w.reshape(1, w.shape[0] * w.shape[1], w.shape[2])
        elif k == "w_dw":
            w = w.transpose(1, 0, 2).reshape(HALO, CW)
        return w

    scatter_handle = {}

    def scatter(k, g):
        if g.shape[0] == 1:
            g = g.reshape(NDEV, -1, g.shape[2])
        (scatter_handle[k],), token = exchange_start([g], "scatter_start_" + k, False)
        return token

    big_out = {}

    def finish_weight(k, after):
        parts = exchange_wait(scatter_handle[k], after, "scatter_wait_" + k, False)
        big_out[k] = adam_reduce(parts, W[k][0], Mo[k][0], Vo[k][0], "adam_" + k)
        return big_out[k][1]

    u = prenorm(xs, norm1_g, scale1, shift1, "prenorm1")

    br2 = b_re[0].transpose(0, 2, 1).reshape(G * H, P)
    bi2 = b_im[0].transpose(0, 2, 1).reshape(G * H, P)
    ldt = log_dt[0].reshape(G, 1)
    expand = jnp.repeat(jnp.eye(G, dtype=F32), H, axis=0)
    lbr, lbi, bbr, bbi = s5_params(a_re[0], a_im[0], ldt, br2, bi2, expand)
    tabs = s5_tables(lbr.reshape(1, G * P), lbi.reshape(1, G * P))
    bdr, bdi = _block_diag(bbr.reshape(G, H, P)), _block_diag(bbi.reshape(G, H, P))
    cdr = _block_diag(c_re[0].transpose(0, 2, 1))
    cdi = _block_diag(c_im[0].transpose(0, 2, 1))
    cd2 = jnp.concatenate([cdr, -cdi], axis=1).astype(BF16)
    bdr3, bdi3 = _rhs3(bdr), _rhs3(bdi)
    bdt2 = jnp.concatenate([bdr.transpose(0, 2, 1), bdi.transpose(0, 2, 1)], axis=1).astype(BF16)
    cdrt3, cdit3 = _rhs3(cdr.transpose(0, 2, 1)), _rhs3(-cdi.transpose(0, 2, 1))

    forward(["w_in"], "gather_forward_in", (u, tabs, bdr3, bdi3, cd2, bdt2, cdrt3, cdit3))
    wg = {"w_in": weight("w_in", u)}
    proj = mm_nn(u, wg["w_in"], "in_proj")
    forward(["w_dw", "w_conv_out", "w_glu_a", "w_glu_b", "w_out"], "gather_forward_mix", proj)
    w_dw_full = weight("w_dw", proj)
    vs, vc = conv_fwd(proj, w_dw_full, b_dw, ln_g, ln_b)
    wg["w_conv_out"] = weight("w_conv_out", vs)
    y_conv = mm_nn(vs, wg["w_conv_out"], "conv_out", out_dtype=BF16)
    s_re, s_im, y_pre, yg = s5_fwd(proj, 2 * CW, bdr3, bdi3, cd2, tabs, d_skip)
    forward(["w_ff1"], "gather_forward_ff1", yg)
    wg["w_glu_a"] = weight("w_glu_a", yg)
    wg["w_glu_b"] = weight("w_glu_b", yg)
    ga = mm_nn(yg, wg["w_glu_a"], "glu_a", out_dtype=BF16)
    gb = mm_nn(yg, wg["w_glu_b"], "glu_b", out_dtype=BF16)
    merged = merge_fwd(proj, 3 * CW, y_conv, ga, gb)
    forward(["w_ff2"], "gather_forward_ff2", merged)
    wg["w_out"] = weight("w_out", merged)
    m_out = mm_nn(merged, wg["w_out"], "out_proj")
    h1, z = residual_norm(xs, m_out, gate1, norm2_g, scale2, shift2)
    wg["w_ff1"] = weight("w_ff1", z)
    act = mm_nn(z, wg["w_ff1"], "ff1", out_dtype=BF16, epi=lambda r: jnp.square(jnp.maximum(r, 0.0)))
    wg["w_ff2"] = weight("w_ff2", act)
    ff = mm_nn(act, wg["w_ff2"], "ff2")

    dh2, dff, loss_part, d_final_g, d_gate2 = loss_bwd(h1, ff, gate2, final_g.reshape(1, D), tgt)
    df = mm_nt(dff, wg["w_ff2"], "ff2_dx", out_dtype=BF16,
               epi=lambda r, a: r * (2.0 * jnp.sqrt(a.astype(F32))), extras=(act,))
    t = scatter("w_ff2", mm_tn(act, dff, 1, "ff2_dw", out_dtype=BF16))
    t = scatter("w_ff1", mm_tn(z, df, NDEV, "ff1_dw", out_dtype=BF16, dep=t))
    dz = mm_nt(df, wg["w_ff1"], "ff1_dx", out_dtype=BF16, dep=t)
    dh1, d_scale2, d_shift2, d_norm2_g, dmo, d_gate1 = norm_bwd(dz, h1, dh2, norm2_g, scale2, "norm2_bwd", gate1, m_out)
    t = scatter("w_out", mm_tn(merged, dmo, 1, "out_dw", out_dtype=BF16))
    dmerged = mm_nt(dmo, wg["w_out"], "out_dx", out_dtype=BF16, dep=t)
    dyc, dga, dgb, dproj_g = merge_bwd(dmerged, proj, 3 * CW, y_conv, ga, gb)
    t = scatter("w_conv_out", mm_tn(vs, dyc, NDEV, "conv_out_dw", out_dtype=BF16))
    t = scatter("w_glu_a", mm_tn(yg, dga, NDEV, "glu_a_dw", out_dtype=BF16, dep=t))
    t = scatter("w_glu_b", mm_tn(yg, dgb, NDEV, "glu_b_dw", out_dtype=BF16, dep=t))
    dvs = mm_nt(dyc, wg["w_conv_out"], "conv_out_dx", out_dtype=BF16, dep=t)
    dyg_a = mm_nt(dga, wg["w_glu_a"], "glu_a_dx", out_dtype=BF16, dep=t)
    dyg_b = mm_nt(dgb, wg["w_glu_b"], "glu_b_dx", out_dtype=BF16, dep=t)
    dvc, d_ln_g, d_ln_b = conv_ln_bwd(dvs, vc, ln_g, ln_b)
    dproj_s, d_d_skip, dcdr, dcdi, dbdr, dbdi, dlr8, dli8 = s5_bwd(
        dyg_a, dyg_b, y_pre, proj, 2 * CW, s_re, s_im, bdt2, cdrt3, cdit3, tabs, d_skip)
    dproj, d_w_dw, d_b_dw = conv_bwd(dvc, proj, w_dw_full, dproj_s, dproj_g)
    d_c_re = _diag_blocks(dcdr, H, P)
    d_c_im = _diag_blocks(dcdi, H, P)
    d_bbr = _diag_blocks(dbdr, H, P)
    d_bbi = _diag_blocks(dbdi, H, P)
    dlr = jnp.sum(dlr8, axis=0).reshape(G, P)
    dli = jnp.sum(dli8, axis=0).reshape(G, P)
    early_parts = [d_b_dw, d_ln_g, d_ln_b, dlr, dli, d_bbr, d_bbi, d_c_re, d_c_im, d_d_skip, d_norm2_g, d_final_g,
                   d_w_dw]
    pack_early = _flat_pad(early_parts, PACK).reshape(NDEV, -1, 1024)
    (early_scatter,), done = exchange_start([pack_early], "scatter_small_start", False)
    for k in ("w_ff2", "w_ff1", "w_out"):
        done = finish_weight(k, done)
    early_sum = sum_devices(exchange_wait(early_scatter, done, "scatter_small_wait", False), "sum_small_early")
    (early_gather,), done = exchange_start([early_sum], "gather_small_start", True)
    t = scatter("w_in", mm_tn(u, dproj, NDEV, "in_dw", out_dtype=BF16, dep=done))

    tot_early = exchange_wait(early_gather, t, "gather_small_wait", True)
    (g_b_dw, g_ln_g, g_ln_b, t_lr, t_li, t_bbr, t_bbi, g_c_re_t, g_c_im_t, g_d_skip,
     g_norm2_g, g_final_g, g_w_dw_full) = _split(tot_early.reshape(-1), early_parts)
    g_a_re, g_a_im, g_ldt, g_br2, g_bi2 = s5_params_bwd(
        a_re[0], a_im[0], ldt, br2, bi2, expand, t_lr, t_li, t_bbr.reshape(G * H, P), t_bbi.reshape(G * H, P))
    g_brt, g_bit = g_br2.reshape(G, H, P), g_bi2.reshape(G, H, P)
    g2 = {
        "b_dw": g_b_dw, "ln_g": g_ln_g, "ln_b": g_ln_b,
        "a_re": g_a_re, "a_im": g_a_im, "log_dt": g_ldt.reshape(1, G),
        "b_re": g_brt.transpose(0, 2, 1).reshape(G * P, H), "b_im": g_bit.transpose(0, 2, 1).reshape(G * P, H),
        "c_re": g_c_re_t.reshape(G * H, P), "c_im": g_c_im_t.reshape(G * H, P), "d_skip": g_d_skip,
        "norm2_g": g_norm2_g, "final_g": g_final_g,
        "w_dw": lax.dynamic_slice(g_w_dw_full, (0, me * (CW // NDEV)), (CONV_K, CW // NDEV)),
    }
    as2d = lambda k, a: a.reshape(g2[k].shape)
    grads, delta, new_m, new_v = {}, {}, {}, {}

    def adam_small(ks, name):
        outs = adam_many([(as2d(k, W[k]), g2[k], as2d(k, Mo[k]), as2d(k, Vo[k])) for k in ks], name)
        for k, o in zip(ks, outs):
            grads[k] = g2[k].reshape(W[k].shape)
            delta[k], new_m[k], new_v[k] = [a.reshape(W[k].shape) for a in o]
        return outs[0][0]

    after = adam_small([k for k in names if k in g2], "adam_small_early")
    for k in ("w_conv_out", "w_glu_a", "w_glu_b"):
        after = finish_weight(k, after)
    du = mm_nt(dproj, wg["w_in"], "in_dx", out_dtype=BF16, dep=after)
    grad_x, d_scale1, d_shift1, d_norm1_g = norm_bwd(du, xs, dh1, norm1_g, scale1, "norm1_bwd")

    dmod = jnp.concatenate([d_shift1, d_scale1, d_gate1, d_shift2, d_scale2, d_gate2], axis=1)
    late_parts = [dmod, d_norm1_g]
    pack_late = _flat_pad(late_parts, PACK).reshape(NDEV, -1, 1024)
    parts_late, dmod_from = _exchange([pack_late, dmod.reshape(NDEV, 1, n_ada)], "scatter_small_late", False)
    (tot_late,) = _exchange([sum_devices(parts_late, "sum_small_late")], "gather_small_late", True)
    g2["b_ada"], g2["norm1_g"] = _split(tot_late.reshape(-1), late_parts)
    adam_small(["b_ada", "norm1_g"], "adam_small_late")
    dmod_cols = dmod_from.reshape(NDEV, n_ada)

    g, d, mm, vv = adam_w_ada(c_act, dmod_cols, w_ada[0], m_w_ada[0], v_w_ada[0])
    grads["w_ada"], delta["w_ada"], new_m["w_ada"], new_v["w_ada"] = g[None], d[None], mm[None], vv[None]

    finish_weight("w_in", d)
    for k in big:
        g, d, mm, vv = big_out[k]
        grads[k], delta[k], new_m[k], new_v[k] = g[None], d[None], mm[None], vv[None]

    loss = lax.psum(loss_part[0, 0], ("x", "y", "c"))
    return (loss, grad_x[None], *[grads[k] for k in names], *[delta[k] for k in names],
            *[new_m[k] for k in names], *[new_v[k] for k in names])
```

```python
import functools
import math

import jax
import jax.numpy as jnp
from jax import lax
from jax.experimental import pallas as pl
from jax.experimental.pallas import tpu as pltpu

F32 = jnp.float32
BF16 = jnp.bfloat16
NDEV = 8
EPS = 1e-6
ADAM_LR, ADAM_B1, ADAM_B2, ADAM_EPS, ADAM_WD, ADAM_STEP = 0.001, 0.9, 0.999, 1e-08, 0.01, 10
CONV_K = 31
HALO = 32
GROUP = 16
STATE = 64
GB = 8
S5_ROWS = 2048
HI = lax.Precision.HIGHEST
MESH = pl.DeviceIdType.MESH
VMEM_LIMIT = 56 * 1024 * 1024
MAX_CONTRACT = 2048
PACK_ROWS = 64
PACK = PACK_ROWS * 1024
ANY = pl.BlockSpec(memory_space=pl.ANY)


def _params(sem=None):
    if sem is None:
        return pltpu.CompilerParams(vmem_limit_bytes=VMEM_LIMIT)
    return pltpu.CompilerParams(dimension_semantics=sem, vmem_limit_bytes=VMEM_LIMIT)


def _sigmoid(v):
    return 1.0 / (1.0 + jnp.exp(-v))


def _me():
    return 4 * lax.axis_index("x") + 2 * lax.axis_index("y") + lax.axis_index("c")


def _peer(k):
    x, y, c = lax.axis_index("x"), lax.axis_index("y"), lax.axis_index("c")
    px = 1 - x if (k >> 2) & 1 else x
    py = 1 - y if (k >> 1) & 1 else y
    pc = 1 - c if k & 1 else c
    return (px, py, pc), 4 * px + 2 * py + pc


def _exchange(arrays, name, gather):
    n = len(arrays)
    out_shape = []
    for a in arrays:
        shp = (NDEV,) + a.shape if gather else a.shape
        out_shape.append(jax.ShapeDtypeStruct(shp, a.dtype))

    def body(*refs):
        ins, outs = refs[:n], refs[n:2 * n]
        send, recv, lsem = refs[2 * n:]
        me = _me()
        local = []
        for a in range(n):
            src = ins[a] if gather else ins[a].at[me]
            cp = pltpu.make_async_copy(src, outs[a].at[me], lsem.at[a])
            cp.start()
            local.append(cp)
        sends = []
        for a in range(n):
            for k in range(1, NDEV):
                dev, pidx = _peer(k)
                src = ins[a] if gather else ins[a].at[pidx]
                cp = pltpu.make_async_remote_copy(
                    src_ref=src, dst_ref=outs[a].at[me], send_sem=send.at[a * (NDEV - 1) + k - 1], recv_sem=recv.at[a * (NDEV - 1) + k - 1],
                    device_id=dev, device_id_type=MESH)
                cp.start()
                sends.append(cp)
        for a in range(n):
            for k in range(1, NDEV):
                dev, pidx = _peer(k)
                src = ins[a] if gather else ins[a].at[pidx]
                pltpu.make_async_remote_copy(
                    src_ref=src, dst_ref=outs[a].at[pidx], send_sem=send.at[a * (NDEV - 1) + k - 1], recv_sem=recv.at[a * (NDEV - 1) + k - 1],
                    device_id=dev, device_id_type=MESH).wait_recv()
        for cp in sends:
            cp.wait_send()
        for cp in local:
            cp.wait()

    return pl.pallas_call(
        body, name=name, out_shape=tuple(out_shape),
        in_specs=[ANY] * n, out_specs=tuple([ANY] * n),
        scratch_shapes=[pltpu.SemaphoreType.DMA((n * (NDEV - 1),)), pltpu.SemaphoreType.DMA((n * (NDEV - 1),)),
                        pltpu.SemaphoreType.DMA((n,))],
    )(*arrays)


HBM = pl.BlockSpec(memory_space=pltpu.HBM)
SEM = pl.BlockSpec(memory_space=pltpu.SEMAPHORE)
EFFECT = pltpu.SideEffectType.DATAFLOW_SIDE_EFFECTING
NPEER = NDEV - 1


def _landing(block_of_me, shape, dtype):
    land = lax.empty((NDEV,) + tuple(shape), dtype)
    start = (_me(),) + (0,) * len(shape)
    return pltpu.with_memory_space_constraint(lax.dynamic_update_slice(land, block_of_me[None], start), pltpu.HBM)


def exchange_start(arrays, name, gather, after=None):
    n = len(arrays)
    me = _me()
    deps = () if after is None else (after,)
    lands = []
    for a in arrays:
        if gather:
            lands.append(_landing(a, a.shape, a.dtype))
        else:
            mine = lax.dynamic_slice(a, (me,) + (0,) * (a.ndim - 1), (1,) + a.shape[1:])[0]
            lands.append(_landing(mine, a.shape[1:], a.dtype))
    srcs = [pltpu.with_memory_space_constraint(a, pltpu.HBM) for a in arrays]

    def body(*refs):
        ins, lnd = refs[:n], refs[n:2 * n]
        outs = refs[2 * n + len(deps):]
        sends, recvs, token = outs[:n], outs[n:2 * n], outs[-1]
        my = _me()
        for a in range(n):
            for k in range(1, NDEV):
                dev, pidx = _peer(k)
                src = ins[a] if gather else ins[a].at[pidx]
                pltpu.make_async_remote_copy(
                    src_ref=src, dst_ref=lnd[a].at[my], send_sem=sends[a].at[k - 1], recv_sem=recvs[a].at[k - 1],
                    device_id=dev, device_id_type=MESH).start()
        token[...] = jnp.zeros_like(token)

    out_shape = ([pltpu.SemaphoreType.DMA((NPEER,))] * (2 * n)
                 + [pltpu.HBM(a.shape, a.dtype) for a in srcs] + [pltpu.HBM(l.shape, l.dtype) for l in lands]
                 + [jax.ShapeDtypeStruct((8, 128), F32)])
    res = pl.pallas_call(
        body, name=name, out_shape=tuple(out_shape),
        in_specs=[HBM] * (2 * n) + [ANY] * len(deps),
        out_specs=tuple([SEM] * (2 * n) + [HBM] * (2 * n) + [pl.BlockSpec(memory_space=pltpu.VMEM)]),
        input_output_aliases={i: 2 * n + i for i in range(2 * n)},
        compiler_params=pltpu.CompilerParams(has_side_effects=EFFECT),
    )(*srcs, *lands, *deps)
    handles = [(res[a], res[n + a], res[2 * n + a], res[3 * n + a]) for a in range(n)]
    return handles, res[-1]


def exchange_wait(handle, after, name, gather):
    send_sem, recv_sem, src, land = handle

    def body(src_ref, land_ref, s_ref, r_ref, after_ref, src_out, land_out):
        for k in range(1, NDEV):
            dev, pidx = _peer(k)
            s = src_ref if gather else src_ref.at[pidx]
            cp = pltpu.make_async_remote_copy(
                src_ref=s, dst_ref=land_ref.at[pidx], send_sem=s_ref.at[k - 1], recv_sem=r_ref.at[k - 1],
                device_id=dev, device_id_type=MESH)
            cp.wait_send()
            cp.wait_recv()

    return pl.pallas_call(
        body, name=name, out_shape=(pltpu.HBM(src.shape, src.dtype), pltpu.HBM(land.shape, land.dtype)),
        in_specs=(HBM, HBM, SEM, SEM, ANY), out_specs=(HBM, HBM), input_output_aliases={0: 0, 1: 1},
        compiler_params=pltpu.CompilerParams(has_side_effects=EFFECT),
    )(src, land, send_sem, recv_sem, after)[1]


ICI_PEERS = (2, 4, 6)
SIBLING = 1


def gather2_start(blocks, name, after):
    m = len(blocks)
    lands = [_landing(b, b.shape, b.dtype) for b in blocks]
    srcs = [pltpu.with_memory_space_constraint(b, pltpu.HBM) for b in blocks]
    n = len(ICI_PEERS)

    def body(*refs):
        src, lnd = refs[:m], refs[m:2 * m]
        outs = refs[2 * m + 1:]
        send, recv_sib, recv_ici = outs[:m], outs[m:2 * m], outs[2 * m:3 * m]
        my = _me()
        for a in range(m):
            dev, _ = _peer(SIBLING)
            pltpu.make_async_remote_copy(src_ref=src[a], dst_ref=lnd[a].at[my], send_sem=send[a].at[0],
                                         recv_sem=recv_sib[a].at[0], device_id=dev, device_id_type=MESH).start()
            for j, k in enumerate(ICI_PEERS):
                dev, _ = _peer(k)
                pltpu.make_async_remote_copy(src_ref=src[a], dst_ref=lnd[a].at[my], send_sem=send[a].at[1 + j],
                                             recv_sem=recv_ici[a].at[j], device_id=dev, device_id_type=MESH).start()

    out_shape = ([pltpu.SemaphoreType.DMA((1 + n,))] * m + [pltpu.SemaphoreType.DMA((1,))] * m
                 + [pltpu.SemaphoreType.DMA((n,))] * m
                 + [pltpu.HBM(s.shape, s.dtype) for s in srcs] + [pltpu.HBM(l.shape, l.dtype) for l in lands])
    res = pl.pallas_call(
        body, name=name, out_shape=tuple(out_shape),
        in_specs=[HBM] * (2 * m) + [ANY], out_specs=tuple([SEM] * (3 * m) + [HBM] * (2 * m)),
        input_output_aliases={i: 3 * m + i for i in range(2 * m)},
        compiler_params=pltpu.CompilerParams(has_side_effects=EFFECT),
    )(*srcs, *lands, after)
    return [tuple(res[g * m + a] for g in range(5)) for a in range(m)]


def gather2_forward(handles, name, after):
    m = len(handles)
    n = len(ICI_PEERS)
    srcs, lands = [h[3] for h in handles], [h[4] for h in handles]
    deps = tuple(after) if isinstance(after, (tuple, list)) else (after,)

    def body(*refs):
        src, lnd, recv_ici = refs[:m], refs[m:2 * m], refs[2 * m:3 * m]
        outs = refs[3 * m + len(deps):]
        fsend, frecv = outs[:m], outs[m:2 * m]
        sib, _ = _peer(SIBLING)
        for a in range(m):
            for j, k in enumerate(ICI_PEERS):
                dev, pidx = _peer(k)
                pltpu.make_async_remote_copy(
                    src_ref=src[a], dst_ref=lnd[a].at[pidx], send_sem=fsend[a].at[j], recv_sem=recv_ici[a].at[j],
                    device_id=dev, device_id_type=MESH).wait_recv()
                pltpu.make_async_remote_copy(
                    src_ref=lnd[a].at[pidx], dst_ref=lnd[a].at[pidx], send_sem=fsend[a].at[j], recv_sem=frecv[a].at[j],
                    device_id=sib, device_id_type=MESH).start()

    out_shape = ([pltpu.SemaphoreType.DMA((n,))] * (2 * m)
                 + [pltpu.HBM(s.shape, s.dtype) for s in srcs] + [pltpu.HBM(l.shape, l.dtype) for l in lands])
    res = pl.pallas_call(
        body, name=name, out_shape=tuple(out_shape),
        in_specs=[HBM] * (2 * m) + [SEM] * m + [ANY] * len(deps),
        out_specs=tuple([SEM] * (2 * m) + [HBM] * (2 * m)),
        input_output_aliases={i: 2 * m + i for i in range(2 * m)},
        compiler_params=pltpu.CompilerParams(has_side_effects=EFFECT),
    )(*srcs, *lands, *[h[2] for h in handles], *deps)
    return [(handles[a][0], handles[a][1], res[a], res[m + a], res[2 * m + a], res[3 * m + a]) for a in range(m)]


def gather2_wait(handle, name, after):
    send, recv_sib, fsend, frecv, src, land = handle

    def body(src_ref, land_ref, send_ref, recv_sib_ref, fsend_ref, frecv_ref, after_ref, src_out, land_out):
        sib, sib_idx = _peer(SIBLING)
        own = pltpu.make_async_remote_copy(src_ref=src_ref, dst_ref=land_ref.at[sib_idx], send_sem=send_ref.at[0],
                                           recv_sem=recv_sib_ref.at[0], device_id=sib, device_id_type=MESH)
        own.wait_send()
        own.wait_recv()
        for j, k in enumerate(ICI_PEERS):
            dev, pidx = _peer(k)
            pltpu.make_async_remote_copy(src_ref=src_ref, dst_ref=land_ref.at[pidx], send_sem=send_ref.at[1 + j],
                                         recv_sem=frecv_ref.at[j], device_id=dev, device_id_type=MESH).wait_send()
            _, fidx = _peer(k ^ SIBLING)
            fwd = pltpu.make_async_remote_copy(src_ref=land_ref.at[pidx], dst_ref=land_ref.at[fidx],
                                               send_sem=fsend_ref.at[j], recv_sem=frecv_ref.at[j],
                                               device_id=sib, device_id_type=MESH)
            fwd.wait_send()
            fwd.wait_recv()

    return pl.pallas_call(
        body, name=name, out_shape=(pltpu.HBM(src.shape, src.dtype), pltpu.HBM(land.shape, land.dtype)),
        in_specs=(HBM, HBM, SEM, SEM, SEM, SEM, ANY), out_specs=(HBM, HBM), input_output_aliases={0: 0, 1: 1},
        compiler_params=pltpu.CompilerParams(has_side_effects=EFFECT),
    )(src, land, send, recv_sib, fsend, frecv, after)[1]


def _acc_steps(p, acc, k, nk, finish):
    if nk == 1:
        finish(p)
        return

    @pl.when(k == 0)
    def _():
        acc[...] = p

    @pl.when(k > 0)
    def _():
        acc[...] += p

    @pl.when(k == nk - 1)
    def _():
        finish(acc[...])


def _shards_per_step(J, n, tn, width):
    s = 1
    while n == tn and J % (2 * s) == 0 and 2 * s * tn <= width:
        s *= 2
    return s


def mm_nn(a, w3, name, out_dtype=F32, epi=None, extras=()):
    M, K = a.shape
    J, _, n = w3.shape
    tm, tn, tk = min(1024, M), min(1024, n), min(2048, K)
    q, nk, ne = n // tn, K // tk, len(extras)
    s = _shards_per_step(J, n, tn, 1024)

    def body(*refs):
        a_ref, w_ref = refs[:2]
        ex, o_ref, acc = refs[2:2 + ne], refs[2 + ne], refs[-1]
        av = a_ref[...]
        p = jnp.dot(av, w_ref[0], preferred_element_type=F32)
        if s > 1:
            p = jnp.concatenate([p] + [jnp.dot(av, w_ref[j], preferred_element_type=F32) for j in range(1, s)], axis=1)

        def finish(r):
            if epi is not None:
                r = epi(r, *[e[...] for e in ex])
            o_ref[...] = r.astype(out_dtype)

        _acc_steps(p, acc, pl.program_id(2), nk, finish)

    return pl.pallas_call(
        body, name=name, grid=(M // tm, (J // s) * q, nk),
        in_specs=[pl.BlockSpec((tm, tk), lambda i, j, k: (i, k)),
                  pl.BlockSpec((s, tk, tn), lambda i, j, k: (j // q, k, j % q))]
        + [pl.BlockSpec((tm, s * tn), lambda i, j, k: (i, j))] * ne,
        out_specs=pl.BlockSpec((tm, s * tn), lambda i, j, k: (i, j)),
        out_shape=jax.ShapeDtypeStruct((M, J * n), out_dtype),
        scratch_shapes=[pltpu.VMEM((tm, s * tn), F32)],
        compiler_params=_params(("parallel", "parallel", "arbitrary")),
    )(a, w3, *extras)


def mm_nt(dy, w3, name, out_dtype=F32, epi=None, extras=(), dep=None):
    M, _ = dy.shape
    J, K, n = w3.shape
    tm, tn, tkk = min(1024, M), min(MAX_CONTRACT, n), min(1024, K)
    q, ne = n // tn, len(extras)
    s = 1
    while q == 1 and J % (2 * s) == 0 and 2 * s * tn <= MAX_CONTRACT:
        s *= 2
    nk = (J // s) * q
    deps = () if dep is None else (dep,)

    def body(*refs):
        d_ref, w_ref = refs[:2]
        ex, o_ref, acc = refs[2:2 + ne], refs[-2], refs[-1]
        nt = (((1,), (1,)), ((), ()))
        p = lax.dot_general(d_ref[:, 0:tn], w_ref[0], nt, preferred_element_type=F32)
        for j in range(1, s):
            p = p + lax.dot_general(d_ref[:, j * tn:(j + 1) * tn], w_ref[j], nt, preferred_element_type=F32)

        def finish(r):
            if epi is not None:
                r = epi(r, *[e[...] for e in ex])
            o_ref[...] = r.astype(out_dtype)

        _acc_steps(p, acc, pl.program_id(2), nk, finish)

    return pl.pallas_call(
        body, name=name, grid=(M // tm, K // tkk, nk),
        in_specs=[pl.BlockSpec((tm, s * tn), lambda i, kk, c: (i, c)),
                  pl.BlockSpec((s, tkk, tn), lambda i, kk, c: (c // q, kk, c % q))]
        + [pl.BlockSpec((tm, tkk), lambda i, kk, c: (i, kk))] * ne + [ANY] * len(deps),
        out_specs=pl.BlockSpec((tm, tkk), lambda i, kk, c: (i, kk)),
        out_shape=jax.ShapeDtypeStruct((M, K), out_dtype),
        scratch_shapes=[pltpu.VMEM((tm, tkk), F32)],
        compiler_params=_params(("parallel", "parallel", "arbitrary")),
    )(dy, w3, *extras, *deps)


def mm_tn(a, dy, J, name, out_dtype=F32, dep=None):
    M, K = a.shape
    n = dy.shape[1] // J
    tm, tn, tkk = min(MAX_CONTRACT, M), min(1024, n), min(1024, K)
    q, nk = n // tn, M // tm
    s = _shards_per_step(J, n, tn, 1024)
    deps = () if dep is None else (dep,)

    def body(a_ref, d_ref, *rest):
        o_ref, acc = rest[-2:]
        p = lax.dot_general(a_ref[...], d_ref[...], (((0,), (0,)), ((), ())), preferred_element_type=F32)

        def finish(r):
            for j in range(s):
                o_ref[j] = r[:, j * tn:(j + 1) * tn].astype(out_dtype)

        _acc_steps(p, acc, pl.program_id(2), nk, finish)

    return pl.pallas_call(
        body, name=name, grid=(K // tkk, (J // s) * q, nk),
        in_specs=[pl.BlockSpec((tm, tkk), lambda kk, c, m: (m, kk)),
                  pl.BlockSpec((tm, s * tn), lambda kk, c, m: (m, c))] + [ANY] * len(deps),
        out_specs=pl.BlockSpec((s, tkk, tn), lambda kk, c, m: (c // q, kk, c % q)),
        out_shape=jax.ShapeDtypeStruct((J, K, n), out_dtype),
        scratch_shapes=[pltpu.VMEM((tkk, s * tn), F32)],
        compiler_params=_params(("parallel", "parallel", "arbitrary")),
    )(a, dy, *deps)


def _tm(L):
    return min(256, L)


def _row(w, cb=0, tm=None):
    return pl.BlockSpec((tm, w), lambda i: (i, cb))


def _vec(w, cb=0):
    return pl.BlockSpec((1, w), lambda i: (0, cb))


def _accum(ref, val, i):
    @pl.when(i == 0)
    def _():
        ref[...] = val

    @pl.when(i > 0)
    def _():
        ref[...] += val


def _colsum(v):
    return jnp.sum(v, axis=0, keepdims=True)


def _rms(v):
    return lax.rsqrt(jnp.mean(v * v, axis=-1, keepdims=True) + EPS)


def adaln_mod(c_all, w_ada, b_cols):
    B, D = c_all.shape
    n = w_ada.shape[1]
    tn = 512

    def body(c_ref, w_ref, b_ref, o_ref, ca_ref):
        cv = c_ref[...]
        ca = cv * _sigmoid(cv)
        ca_ref[...] = ca
        o_ref[...] = jnp.dot(ca.astype(BF16), w_ref[...].astype(BF16), preferred_element_type=F32) + b_ref[...]

    return pl.pallas_call(
        body, name="adaln_mod", grid=(n // tn,),
        in_specs=[pl.BlockSpec((B, D), lambda j: (0, 0)), pl.BlockSpec((D, tn), lambda j: (0, j)),
                  pl.BlockSpec((1, tn), lambda j: (0, j))],
        out_specs=(pl.BlockSpec((B, tn), lambda j: (0, j)), pl.BlockSpec((B, D), lambda j: (0, 0))),
        out_shape=(jax.ShapeDtypeStruct((B, n), F32), jax.ShapeDtypeStruct((B, D), F32)),
        compiler_params=_params(("arbitrary",)),
    )(c_all, w_ada, b_cols)


def prenorm(x, g, scale, shift, name):
    L, D = x.shape
    tm = _tm(L)

    def body(x_ref, g_ref, sc_ref, sh_ref, u_ref):
        xv = x_ref[...]
        u_ref[...] = (xv * _rms(xv) * g_ref[...] * (1.0 + sc_ref[...]) + sh_ref[...]).astype(BF16)

    return pl.pallas_call(
        body, name=name, grid=(L // tm,),
        in_specs=[_row(D, tm=tm), _vec(D), _vec(D), _vec(D)],
        out_specs=_row(D, tm=tm), out_shape=jax.ShapeDtypeStruct((L, D), BF16),
        compiler_params=_params(("parallel",)),
    )(x, g, scale, shift)


def _shifted_copies(buf, shifted, tm):
    n = HALO + tm - 8
    for r in range(1, 8):
        shifted[r - 1] = buf[pl.ds(r, n), :]


def _window(buf, shifted, off, tm):
    r, base = off % 8, off - off % 8
    if r == 0:
        return buf[pl.ds(base, tm), :]
    return shifted[r - 1, pl.ds(base, tm), :]


def conv_fwd(proj, w_dw, b_dw, ln_g, ln_b):
    L = proj.shape[0]
    C = w_dw.shape[1]
    tm = _tm(L)
    hb = tm // HALO

    def body(a_ref, g_ref, ah_ref, gh_ref, w_ref, b_ref, lg_ref, lb_ref, vs_ref, vc_ref, buf, shifted):
        i = pl.program_id(0)
        halo = ah_ref[...] * _sigmoid(gh_ref[...])
        buf[0:HALO, :] = halo * jnp.where(i > 0, 1.0, 0.0)
        buf[HALO:HALO + tm, :] = a_ref[...] * _sigmoid(g_ref[...])
        _shifted_copies(buf, shifted, tm)
        acc = jnp.zeros((tm, C), F32) + b_ref[...]
        for k in range(CONV_K):
            acc = acc + w_ref[k:k + 1, :] * _window(buf, shifted, HALO - (CONV_K - 1) + k, tm)
        vc_ref[...] = acc
        mu = jnp.mean(acc, axis=-1, keepdims=True)
        d = acc - mu
        var = jnp.mean(d * d, axis=-1, keepdims=True)
        ln = d * lax.rsqrt(var + EPS) * lg_ref[...] + lb_ref[...]
        vs_ref[...] = (ln * _sigmoid(ln)).astype(BF16)

    prev = lambda cb: pl.BlockSpec((HALO, C), lambda i: (jnp.maximum(i * hb - 1, 0), cb))
    return pl.pallas_call(
        body, name="conv_fwd", grid=(L // tm,),
        in_specs=[_row(C, 0, tm), _row(C, 1, tm), prev(0), prev(1),
                  pl.BlockSpec((HALO, C), lambda i: (0, 0)), _vec(C), _vec(C), _vec(C)],
        out_specs=(_row(C, tm=tm), _row(C, tm=tm)),
        out_shape=(jax.ShapeDtypeStruct((L, C), BF16), jax.ShapeDtypeStruct((L, C), F32)),
        scratch_shapes=[pltpu.VMEM((HALO + tm, C), F32), pltpu.VMEM((7, HALO + tm - 8, C), F32)],
        compiler_params=_params(("parallel",)),
    )(proj, proj, proj, proj, w_dw, b_dw, ln_g, ln_b)


def _gelu(v):
    return 0.5 * v * (1.0 + jnp.tanh(math.sqrt(2.0 / math.pi) * (v + 0.044715 * v * v * v)))


def _gelu_grad(v):
    k = math.sqrt(2.0 / math.pi)
    t = jnp.tanh(k * (v + 0.044715 * v * v * v))
    return 0.5 * (1.0 + t) + 0.5 * v * (1.0 - t * t) * k * (1.0 + 3.0 * 0.044715 * v * v)


def s5_param_fn(ar, ai, ldt, br, bi, expand):
    dt = jnp.exp(ldt)
    er = jnp.exp(ar * dt)
    th = ai * dt
    lbr, lbi = er * jnp.cos(th), er * jnp.sin(th)
    nr, ni = lbr - 1.0, lbi
    den = ar * ar + ai * ai
    qr, qi = (nr * ar + ni * ai) / den, (ni * ar - nr * ai) / den
    qre = jnp.dot(expand, qr, precision=HI, preferred_element_type=F32)
    qie = jnp.dot(expand, qi, precision=HI, preferred_element_type=F32)
    return lbr, lbi, qre * br - qie * bi, qre * bi + qie * br


def s5_params(ar, ai, ldt, br2, bi2, expand):
    def body(ar_ref, ai_ref, ld_ref, br_ref, bi_ref, e_ref, o1, o2, o3, o4):
        r = s5_param_fn(ar_ref[...], ai_ref[...], ld_ref[...], br_ref[...], bi_ref[...], e_ref[...])
        o1[...], o2[...], o3[...], o4[...] = r

    s2, s3 = jax.ShapeDtypeStruct(ar.shape, F32), jax.ShapeDtypeStruct(br2.shape, F32)
    return pl.pallas_call(body, name="s5_params", out_shape=(s2, s2, s3, s3), compiler_params=_params())(
        ar, ai, ldt, br2, bi2, expand)


def s5_params_bwd(ar, ai, ldt, br2, bi2, expand, dlr, dli, dbr, dbi):
    def body(ar_ref, ai_ref, ld_ref, br_ref, bi_ref, e_ref, c1, c2, c3, c4, o1, o2, o3, o4, o5):
        e = e_ref[...]
        fn = lambda a, b, c, d, f: s5_param_fn(a, b, c, d, f, e)
        _, vjp = jax.vjp(fn, ar_ref[...], ai_ref[...], ld_ref[...], br_ref[...], bi_ref[...])
        r = vjp((c1[...], c2[...], c3[...], c4[...]))
        o1[...], o2[...], o3[...], o4[...], o5[...] = r

    shapes = tuple(jax.ShapeDtypeStruct(v.shape, F32) for v in (ar, ai, ldt, br2, bi2))
    return pl.pallas_call(body, name="s5_params_bwd", out_shape=shapes, compiler_params=_params())(
        ar, ai, ldt, br2, bi2, expand, dlr, dli, dbr, dbi)


def s5_tables(lr, li):
    C = lr.shape[1]

    def body(lr_ref, li_ref, o_ref):
        row = lax.broadcasted_iota(jnp.int32, (8, C), 0)
        for rev in (0, 1):
            pr = jnp.broadcast_to(lr_ref[...], (8, C))
            pi = jnp.broadcast_to(-li_ref[...] if rev else li_ref[...], (8, C))
            br, bi = pr, pi
            pows = [(pr, pi)]
            for _ in range(7):
                pr, pi = pr * br - pi * bi, pr * bi + pi * br
                pows.append((pr, pi))
            base = 8 * rev
            for s, d in enumerate((1, 2, 4)):
                keep = (row + d <= 7) if rev else (row >= d)
                o_ref[base + 2 * s] = jnp.where(keep, pows[d - 1][0], 0.0)
                o_ref[base + 2 * s + 1] = jnp.where(keep, pows[d - 1][1], 0.0)
            cr, ci = jnp.zeros((8, C), F32), jnp.zeros((8, C), F32)
            for j in range(8):
                e = (8 - j) if rev else (j + 1)
                cr = jnp.where(row == j, pows[e - 1][0], cr)
                ci = jnp.where(row == j, pows[e - 1][1], ci)
            o_ref[base + 6] = cr
            o_ref[base + 7] = ci

    return pl.pallas_call(body, name="s5_tables", out_shape=jax.ShapeDtypeStruct((16, 8, C), F32),
                          compiler_params=_params())(lr, li)


def _tile_steps(xr, xi, tabs, rev):
    for s, d in enumerate((1, 2, 4)):
        tr, ti = tabs[2 * s], tabs[2 * s + 1]
        sh = (8 - d) if rev else d
        sr, si = pltpu.roll(xr, sh, 0), pltpu.roll(xi, sh, 0)
        xr, xi = xr + tr * sr - ti * si, xi + tr * si + ti * sr
    return xr, xi


def _tile_carry(xr, xi, tabs, cr, ci):
    tr, ti = tabs[6], tabs[7]
    return xr + tr * cr - ti * ci, xi + tr * ci + ti * cr


def _hi_lo(a):
    hi = a.astype(BF16)
    return hi, (a - hi.astype(F32)).astype(BF16)


def _lhs3(a):
    hi, lo = _hi_lo(a)
    return jnp.concatenate([hi, lo, hi], axis=1)


def _rhs3(m):
    hi, lo = _hi_lo(m)
    return jnp.concatenate([hi, hi, lo], axis=-2)


def s5_fwd(proj, col0, bdr3, bdi3, cd2, tabs, d_skip):
    L = proj.shape[0]
    nb, cw3, sw = bdr3.shape
    cw = cw3 // 3
    tl = min(S5_ROWS, L)
    cb0 = col0 // cw

    def body(u_ref, bdr_ref, bdi_ref, cd_ref, t_ref, dk_ref, sr_ref, si_ref, yp_ref, yg_ref, car):
        l = pl.program_id(1)

        @pl.when(l == 0)
        def _():
            car[...] = jnp.zeros_like(car)

        u = u_ref[...]
        u3 = _lhs3(u)
        sr_ref[...] = jnp.dot(u3, bdr_ref[...], preferred_element_type=F32)
        si_ref[...] = jnp.dot(u3, bdi_ref[...], preferred_element_type=F32)

        def pair(i, c):
            tabs = [t_ref[j] for j in range(8)]
            r0 = pl.multiple_of(i * 16, 16)
            lo, hi = pl.ds(r0, 8), pl.ds(r0 + 8, 8)
            a = _tile_steps(sr_ref[lo, :], si_ref[lo, :], tabs, False)
            b = _tile_steps(sr_ref[hi, :], si_ref[hi, :], tabs, False)
            ar, ai = _tile_carry(a[0], a[1], tabs, c[0], c[1])
            br, bi = _tile_carry(b[0], b[1], tabs, ar[7:8, :], ai[7:8, :])
            sr_ref[lo, :], si_ref[lo, :] = ar, ai
            sr_ref[hi, :], si_ref[hi, :] = br, bi
            return br[7:8, :], bi[7:8, :]

        c = lax.fori_loop(0, tl // 16, pair, (car[0:1, :], car[1:2, :]))
        car[0:1, :] = c[0]
        car[1:2, :] = c[1]
        s2 = jnp.concatenate([sr_ref[...].astype(BF16), si_ref[...].astype(BF16)], axis=1)
        y = jnp.dot(s2, cd_ref[...], preferred_element_type=F32) + dk_ref[...] * u
        yp_ref[...] = y
        yg_ref[...] = _gelu(y).astype(BF16)

    blk = lambda r, c: pl.BlockSpec((None, r, c), lambda b, l: (b, 0, 0))
    return pl.pallas_call(
        body, name="s5_fwd", grid=(nb, L // tl),
        in_specs=[pl.BlockSpec((tl, cw), lambda b, l: (l, cb0 + b)), blk(cw3, sw), blk(cw3, sw), blk(2 * sw, cw),
                  pl.BlockSpec((8, 8, sw), lambda b, l: (0, 0, b)), pl.BlockSpec((1, cw), lambda b, l: (0, b))],
        out_specs=(pl.BlockSpec((tl, sw), lambda b, l: (l, b)), pl.BlockSpec((tl, sw), lambda b, l: (l, b)),
                   pl.BlockSpec((tl, cw), lambda b, l: (l, b)), pl.BlockSpec((tl, cw), lambda b, l: (l, b))),
        out_shape=(jax.ShapeDtypeStruct((L, nb * sw), F32), jax.ShapeDtypeStruct((L, nb * sw), F32),
                   jax.ShapeDtypeStruct((L, nb * cw), F32), jax.ShapeDtypeStruct((L, nb * cw), BF16)),
        scratch_shapes=[pltpu.VMEM((8, sw), F32)],
        compiler_params=_params(("parallel", "arbitrary")),
    )(proj, bdr3, bdi3, cd2, tabs, d_skip)


def s5_bwd(dyg_a, dyg_b, yp, proj, col0, s_re, s_im, bdt2, cdrt3, cdit3, tabs, d_skip):
    L = proj.shape[0]
    nb, sw2, cw = bdt2.shape
    sw = sw2 // 2
    tl = min(S5_ROWS, L)
    nl = L // tl
    cb0 = col0 // cw
    tb = tl // 8

    def body(da_ref, db_ref, yp_ref, u_ref, sr_ref, si_ref, hr_ref, hi_ref, bdt_ref, cdrt_ref, cdit_ref,
             t_ref, dk_ref, du_ref, ddk_ref, dcr_ref, dci_ref, dbr_ref, dbi_ref, dlr_ref, dli_ref,
             gr, gi, pr, pi, car):
        l = pl.program_id(1)
        first = l == nl - 1

        @pl.when(l == 0)
        def _():
            car[...] = jnp.zeros_like(car)

        u = u_ref[...]
        dy = (da_ref[...].astype(F32) + db_ref[...].astype(F32)) * _gelu_grad(yp_ref[...])
        dy3 = _lhs3(dy)
        gr[...] = jnp.dot(dy3, cdrt_ref[...], preferred_element_type=F32)
        gi[...] = jnp.dot(dy3, cdit_ref[...], preferred_element_type=F32)
        inner = jnp.where(first, 0.0, 1.0)
        pr[0:8, :] = hr_ref[...] * inner
        pi[0:8, :] = hi_ref[...] * inner
        pr[8:8 + tl, :] = sr_ref[...]
        pi[8:8 + tl, :] = si_ref[...]
        row = lax.broadcasted_iota(jnp.int32, (8, sw), 0)

        def pair(j, c):
            tabs = [t_ref[8 + k] for k in range(8)]
            r0 = pl.multiple_of((tb // 2 - 1 - j) * 16, 16)
            lo, hi = pl.ds(r0, 8), pl.ds(r0 + 8, 8)
            b = _tile_steps(gr[hi, :], gi[hi, :], tabs, True)
            a = _tile_steps(gr[lo, :], gi[lo, :], tabs, True)
            br, bi = _tile_carry(b[0], b[1], tabs, c[0], c[1])
            ar, ai = _tile_carry(a[0], a[1], tabs, br[0:1, :], bi[0:1, :])
            gr[lo, :], gi[lo, :] = ar, ai
            gr[hi, :], gi[hi, :] = br, bi
            p0r, p1r, p2r = [pltpu.roll(pr[pl.ds(r0 + 8 * n, 8), :], 1, 0) for n in range(3)]
            p0i, p1i, p2i = [pltpu.roll(pi[pl.ds(r0 + 8 * n, 8), :], 1, 0) for n in range(3)]
            qar, qai = jnp.where(row == 0, p0r, p1r), jnp.where(row == 0, p0i, p1i)
            qbr, qbi = jnp.where(row == 0, p1r, p2r), jnp.where(row == 0, p1i, p2i)
            return (ar[0:1, :], ai[0:1, :], c[2] + (ar * qar + ai * qai) + (br * qbr + bi * qbi),
                    c[3] + (ai * qar - ar * qai) + (bi * qbr - br * qbi))

        z = jnp.zeros((8, sw), F32)
        c = lax.fori_loop(0, tb // 2, pair, (car[0:1, :], car[1:2, :], z, z))
        car[0:1, :] = c[0]
        car[1:2, :] = c[1]
        g_re, g_im = gr[...].astype(BF16), gi[...].astype(BF16)
        g2 = jnp.concatenate([g_re, g_im], axis=1)
        du_ref[...] = (dy * dk_ref[...] + jnp.dot(g2, bdt_ref[...], preferred_element_type=F32)).astype(BF16)
        tn = (((0,), (0,)), ((), ()))
        dyb, ub = dy.astype(BF16), u.astype(BF16)
        _accum(ddk_ref, _colsum(dy * u), l)
        _accum(dcr_ref, lax.dot_general(dyb, sr_ref[...].astype(BF16), tn, preferred_element_type=F32), l)
        _accum(dci_ref, -lax.dot_general(dyb, si_ref[...].astype(BF16), tn, preferred_element_type=F32), l)
        _accum(dbr_ref, lax.dot_general(ub, g_re, tn, preferred_element_type=F32), l)
        _accum(dbi_ref, lax.dot_general(ub, g_im, tn, preferred_element_type=F32), l)
        _accum(dlr_ref, c[2], l)
        _accum(dli_ref, c[3], l)

    rl = lambda l: nl - 1 - l
    cblk = lambda w, off=0: pl.BlockSpec((tl, w), lambda b, l: (rl(l), off + b))
    halo = pl.BlockSpec((8, sw), lambda b, l: (jnp.maximum(rl(l) * tb - 1, 0), b))
    mat = lambda r, c: pl.BlockSpec((None, r, c), lambda b, l: (b, 0, 0))
    return pl.pallas_call(
        body, name="s5_bwd", grid=(nb, nl),
        in_specs=[cblk(cw), cblk(cw), cblk(cw), cblk(cw, cb0), cblk(sw), cblk(sw), halo, halo,
                  mat(2 * sw, cw), mat(3 * cw, sw), mat(3 * cw, sw),
                  pl.BlockSpec((16, 8, sw), lambda b, l: (0, 0, b)), pl.BlockSpec((1, cw), lambda b, l: (0, b))],
        out_specs=(cblk(cw), pl.BlockSpec((1, cw), lambda b, l: (0, b)), mat(cw, sw), mat(cw, sw), mat(cw, sw), mat(cw, sw),
                   pl.BlockSpec((8, sw), lambda b, l: (0, b)), pl.BlockSpec((8, sw), lambda b, l: (0, b))),
        out_shape=(jax.ShapeDtypeStruct((L, nb * cw), BF16), jax.ShapeDtypeStruct((1, nb * cw), F32),
                   jax.ShapeDtypeStruct((nb, cw, sw), F32), jax.ShapeDtypeStruct((nb, cw, sw), F32),
                   jax.ShapeDtypeStruct((nb, cw, sw), F32), jax.ShapeDtypeStruct((nb, cw, sw), F32),
                   jax.ShapeDtypeStruct((8, nb * sw), F32), jax.ShapeDtypeStruct((8, nb * sw), F32)),
        scratch_shapes=[pltpu.VMEM((tl, sw), F32), pltpu.VMEM((tl, sw), F32),
                        pltpu.VMEM((tl + 8, sw), F32), pltpu.VMEM((tl + 8, sw), F32), pltpu.VMEM((8, sw), F32)],
        compiler_params=_params(("parallel", "arbitrary")),
    )(dyg_a, dyg_b, yp, proj, s_re, s_im, s_re, s_im, bdt2, cdrt3, cdit3, tabs, d_skip)


def merge_fwd(proj, col_gc, y_conv, ga, gb):
    L, D = y_conv.shape
    tm = _tm(L)
    h = D // 2
    c0 = col_gc // h

    def body(p0, p1, p2, p3, yc_ref, ga_ref, gb_ref, o_ref):
        gc, gs = (p0, p1), (p2, p3)
        for s in range(2):
            cols = slice(s * h, (s + 1) * h)
            y_ssm = ga_ref[:, cols].astype(F32) * _sigmoid(gb_ref[:, cols].astype(F32))
            o_ref[:, cols] = (_sigmoid(gc[s][...]) * yc_ref[:, cols].astype(F32)
                              + _sigmoid(gs[s][...]) * y_ssm).astype(BF16)

    return pl.pallas_call(
        body, name="merge_fwd", grid=(L // tm,),
        in_specs=[_row(h, c0 + s, tm) for s in range(4)] + [_row(D, tm=tm)] * 3,
        out_specs=_row(D, tm=tm), out_shape=jax.ShapeDtypeStruct((L, D), BF16),
        compiler_params=_params(("parallel",)),
    )(proj, proj, proj, proj, y_conv, ga, gb)


def residual_norm(x, m_out, gate, g, scale, shift):
    L, D = x.shape
    tm = _tm(L)

    def body(x_ref, m_ref, gt_ref, g_ref, sc_ref, sh_ref, h_ref, z_ref):
        h = x_ref[...] + gt_ref[...] * m_ref[...]
        h_ref[...] = h
        z_ref[...] = (h * _rms(h) * g_ref[...] * (1.0 + sc_ref[...]) + sh_ref[...]).astype(BF16)

    return pl.pallas_call(
        body, name="residual_norm", grid=(L // tm,),
        in_specs=[_row(D, tm=tm), _row(D, tm=tm), _vec(D), _vec(D), _vec(D), _vec(D)],
        out_specs=(_row(D, tm=tm), _row(D, tm=tm)),
        out_shape=(jax.ShapeDtypeStruct((L, D), F32), jax.ShapeDtypeStruct((L, D), BF16)),
        compiler_params=_params(("parallel",)),
    )(x, m_out, gate, g, scale, shift)


def loss_bwd(h1, ff, gate2, final_g, target):
    L, D = h1.shape
    tm = _tm(L)

    def body(h_ref, f_ref, gt_ref, g_ref, t_ref, dh_ref, dff_ref, loss_ref, dg_ref, dgt_ref):
        i = pl.program_id(0)
        ffv = f_ref[...]
        h2 = h_ref[...] + gt_ref[...] * ffv
        r = _rms(h2)
        n = h2 * r
        err = n * g_ref[...] - t_ref[...]
        per_tok = jnp.mean(err * err, axis=-1, keepdims=True)
        _accum(loss_ref, 0.5 * jnp.sum(per_tok, axis=0, keepdims=True), i)
        dy = err * (1.0 / D)
        _accum(dg_ref, _colsum(dy * n), i)
        dn = dy * g_ref[...]
        dh2 = r * (dn - n * jnp.mean(dn * n, axis=-1, keepdims=True))
        dh_ref[...] = dh2
        dff_ref[...] = (gt_ref[...] * dh2).astype(BF16)
        _accum(dgt_ref, _colsum(dh2 * ffv), i)

    return pl.pallas_call(
        body, name="loss_bwd", grid=(L // tm,),
        in_specs=[_row(D, tm=tm), _row(D, tm=tm), _vec(D), _vec(D), _row(D, tm=tm)],
        out_specs=(_row(D, tm=tm), _row(D, tm=tm), pl.BlockSpec((1, 1), lambda i: (0, 0)), _vec(D), _vec(D)),
        out_shape=(jax.ShapeDtypeStruct((L, D), F32), jax.ShapeDtypeStruct((L, D), BF16),
                   jax.ShapeDtypeStruct((1, 1), F32), jax.ShapeDtypeStruct((1, D), F32), jax.ShapeDtypeStruct((1, D), F32)),
        compiler_params=_params(("arbitrary",)),
    )(h1, ff, gate2, final_g, target)


def norm_bwd(dz, h, dh_in, g, scale, name, gate=None, m_out=None):
    L, D = h.shape
    tm = _tm(L)
    tail = gate is not None

    def body(*refs):
        dz_ref, h_ref, di_ref, g_ref, sc_ref = refs[:5]
        rest = refs[5:]
        if tail:
            gt_ref, m_ref = rest[:2]
            rest = rest[2:]
        dh_ref, dsc_ref, dsh_ref, dg_ref = rest[:4]
        i = pl.program_id(0)
        hv, dzv = h_ref[...], dz_ref[...].astype(F32)
        r = _rms(hv)
        n = hv * r
        _accum(dsc_ref, _colsum(dzv * n * g_ref[...]), i)
        _accum(dsh_ref, _colsum(dzv), i)
        dzn = dzv * (1.0 + sc_ref[...])
        _accum(dg_ref, _colsum(dzn * n), i)
        dn = dzn * g_ref[...]
        dh = di_ref[...] + r * (dn - n * jnp.mean(dn * n, axis=-1, keepdims=True))
        dh_ref[...] = dh
        if tail:
            dmo_ref, dgt_ref = rest[4:]
            dmo_ref[...] = (gt_ref[...] * dh).astype(BF16)
            _accum(dgt_ref, _colsum(dh * m_ref[...]), i)

    ins = [dz, h, dh_in, g, scale]
    in_specs = [_row(D, tm=tm)] * 3 + [_vec(D)] * 2
    out_specs = [_row(D, tm=tm), _vec(D), _vec(D), _vec(D)]
    out_shape = [jax.ShapeDtypeStruct((L, D), F32)] + [jax.ShapeDtypeStruct((1, D), F32)] * 3
    if tail:
        ins += [gate, m_out]
        in_specs += [_vec(D), _row(D, tm=tm)]
        out_specs += [_row(D, tm=tm), _vec(D)]
        out_shape += [jax.ShapeDtypeStruct((L, D), BF16), jax.ShapeDtypeStruct((1, D), F32)]
    return pl.pallas_call(
        body, name=name, grid=(L // tm,), in_specs=in_specs, out_specs=tuple(out_specs), out_shape=tuple(out_shape),
        compiler_params=_params(("arbitrary",)),
    )(*ins)


def merge_bwd(dmerged, proj, col_gc, y_conv, ga, gb):
    L, D = y_conv.shape
    tm = _tm(L)
    h = D // 2
    c0 = col_gc // h

    def body(dm_ref, p0, p1, p2, p3, yc_ref, ga_ref, gb_ref, dyc_ref, dga_ref, dgb_ref, dg_ref):
        gc, gs = (p0, p1), (p2, p3)
        for s in range(2):
            cols = slice(s * h, (s + 1) * h)
            dm = dm_ref[:, cols].astype(F32)
            sc, ss, sb = _sigmoid(gc[s][...]), _sigmoid(gs[s][...]), _sigmoid(gb_ref[:, cols].astype(F32))
            gav = ga_ref[:, cols].astype(F32)
            dyc_ref[:, cols] = (dm * sc).astype(BF16)
            dg_ref[:, cols] = (dm * yc_ref[:, cols].astype(F32) * sc * (1.0 - sc)).astype(BF16)
            dg_ref[:, D + s * h:D + (s + 1) * h] = (dm * gav * sb * ss * (1.0 - ss)).astype(BF16)
            dys = dm * ss
            dga_ref[:, cols] = (dys * sb).astype(BF16)
            dgb_ref[:, cols] = (dys * gav * sb * (1.0 - sb)).astype(BF16)

    return pl.pallas_call(
        body, name="merge_bwd", grid=(L // tm,),
        in_specs=[_row(D, tm=tm)] + [_row(h, c0 + s, tm) for s in range(4)] + [_row(D, tm=tm)] * 3,
        out_specs=(_row(D, tm=tm), _row(D, tm=tm), _row(D, tm=tm), _row(2 * D, tm=tm)),
        out_shape=(jax.ShapeDtypeStruct((L, D), BF16),) * 3 + (jax.ShapeDtypeStruct((L, 2 * D), BF16),),
        compiler_params=_params(("parallel",)),
    )(dmerged, proj, proj, proj, proj, y_conv, ga, gb)


def conv_ln_bwd(dvs, vc, ln_g, ln_b):
    L, C = vc.shape
    tm = _tm(L)

    def body(d_ref, v_ref, g_ref, b_ref, o_ref, dg_ref, db_ref):
        i = pl.program_id(0)
        v = v_ref[...]
        mu = jnp.mean(v, axis=-1, keepdims=True)
        d = v - mu
        rstd = lax.rsqrt(jnp.mean(d * d, axis=-1, keepdims=True) + EPS)
        xh = d * rstd
        ln = xh * g_ref[...] + b_ref[...]
        sg = _sigmoid(ln)
        dln = d_ref[...].astype(F32) * sg * (1.0 + ln * (1.0 - sg))
        _accum(dg_ref, _colsum(dln * xh), i)
        _accum(db_ref, _colsum(dln), i)
        dxh = dln * g_ref[...]
        o_ref[...] = rstd * (dxh - jnp.mean(dxh, axis=-1, keepdims=True)
                             - xh * jnp.mean(dxh * xh, axis=-1, keepdims=True))

    return pl.pallas_call(
        body, name="conv_ln_bwd", grid=(L // tm,),
        in_specs=[_row(C, tm=tm), _row(C, tm=tm), _vec(C), _vec(C)],
        out_specs=(_row(C, tm=tm), _vec(C), _vec(C)),
        out_shape=(jax.ShapeDtypeStruct((L, C), F32), jax.ShapeDtypeStruct((1, C), F32), jax.ShapeDtypeStruct((1, C), F32)),
        compiler_params=_params(("arbitrary",)),
    )(dvs, vc, ln_g, ln_b)


def conv_bwd(dvc, proj, w_dw, dproj_s, dproj_g):
    L, C = dvc.shape
    ws, wgt = dproj_s.shape[1], dproj_g.shape[1]
    tm = _tm(L)
    hb = tm // HALO
    last = L // HALO - 1
    nt = L // tm

    def body(d_ref, dn_ref, a_ref, g_ref, ah_ref, gh_ref, w_ref, ps_ref, pg_ref, o_ref, dw_ref, db_ref,
             dbuf, vbuf, dsh, vsh, dw8):
        i = pl.program_id(0)
        o_ref[:, 2 * C:2 * C + ws] = ps_ref[...]
        o_ref[:, 2 * C + ws:2 * C + ws + wgt] = pg_ref[...]
        dcur = d_ref[...]
        dbuf[0:tm, :] = dcur
        dbuf[tm:tm + HALO, :] = dn_ref[...] * jnp.where(i < nt - 1, 1.0, 0.0)
        av, sg = a_ref[...], _sigmoid(g_ref[...])
        vbuf[0:HALO, :] = ah_ref[...] * _sigmoid(gh_ref[...]) * jnp.where(i > 0, 1.0, 0.0)
        vbuf[HALO:HALO + tm, :] = av * sg
        _shifted_copies(dbuf, dsh, tm)
        _shifted_copies(vbuf, vsh, tm)
        dv = jnp.zeros((tm, C), F32)
        for k in range(CONV_K):
            dv = dv + w_ref[k:k + 1, :] * _window(dbuf, dsh, CONV_K - 1 - k, tm)
        o_ref[:, 0:C] = (dv * sg).astype(BF16)
        o_ref[:, C:2 * C] = (dv * av * sg * (1.0 - sg)).astype(BF16)

        @pl.when(i == 0)
        def _():
            dw8[...] = jnp.zeros_like(dw8)

        for k in range(CONV_K):
            prod = dcur * _window(vbuf, vsh, HALO - (CONV_K - 1) + k, tm)
            part = prod[0:8, :]
            for j in range(1, tm // 8):
                part = part + prod[8 * j:8 * j + 8, :]
            dw8[k] += part
        _accum(db_ref, _colsum(dcur), i)

        @pl.when(i == nt - 1)
        def _():
            for k in range(CONV_K):
                dw_ref[k:k + 1, :] = _colsum(dw8[k])
            dw_ref[CONV_K:HALO, :] = jnp.zeros((HALO - CONV_K, C), F32)

    prev = lambda cb: pl.BlockSpec((HALO, C), lambda i: (jnp.maximum(i * hb - 1, 0), cb))
    return pl.pallas_call(
        body, name="conv_bwd", grid=(nt,),
        in_specs=[_row(C, tm=tm), pl.BlockSpec((HALO, C), lambda i: (jnp.minimum((i + 1) * hb, last), 0)),
                  _row(C, 0, tm), _row(C, 1, tm), prev(0), prev(1), pl.BlockSpec((HALO, C), lambda i: (0, 0)),
                  _row(ws, tm=tm), _row(wgt, tm=tm)],
        out_specs=(_row(2 * C + ws + wgt, tm=tm), pl.BlockSpec((HALO, C), lambda i: (0, 0)), _vec(C)),
        out_shape=(jax.ShapeDtypeStruct((L, 2 * C + ws + wgt), BF16), jax.ShapeDtypeStruct((HALO, C), F32),
                   jax.ShapeDtypeStruct((1, C), F32)),
        scratch_shapes=[pltpu.VMEM((tm + HALO, C), F32), pltpu.VMEM((HALO + tm, C), F32),
                        pltpu.VMEM((7, HALO + tm - 8, C), F32), pltpu.VMEM((7, HALO + tm - 8, C), F32),
                        pltpu.VMEM((CONV_K, 8, C), F32)],
        compiler_params=_params(("arbitrary",)),
    )(dvc, dvc, proj, proj, proj, proj, w_dw, dproj_s, dproj_g)


def _adamw(w, g, m, v):
    m = ADAM_B1 * m + (1.0 - ADAM_B1) * g
    v = ADAM_B2 * v + (1.0 - ADAM_B2) * (g * g)
    m_hat = m / (1.0 - ADAM_B1 ** ADAM_STEP)
    v_hat = v / (1.0 - ADAM_B2 ** ADAM_STEP)
    delta = -ADAM_LR * (m_hat / (jnp.sqrt(v_hat) + ADAM_EPS) + ADAM_WD * w)
    return delta, m, v


def _tile_rows(R, C):
    tr = 8
    while tr * 2 * C <= 128 * 1024 and R % (tr * 2) == 0:
        tr *= 2
    assert R % tr == 0, (R, C)
    return tr


def sum_devices(parts, name):
    _, R, C = parts.shape
    tr = _tile_rows(R, C)

    def body(p_ref, o_ref):
        s = p_ref[0]
        for j in range(1, NDEV):
            s = s + p_ref[j]
        o_ref[...] = s

    return pl.pallas_call(
        body, name=name, grid=(R // tr,),
        in_specs=[pl.BlockSpec((NDEV, tr, C), lambda i: (0, i, 0))],
        out_specs=pl.BlockSpec((tr, C), lambda i: (i, 0)), out_shape=jax.ShapeDtypeStruct((R, C), F32),
        compiler_params=_params(("parallel",)),
    )(parts)


def adam_update(w, g, m, v, name):
    R, C = w.shape
    tr = _tile_rows(R, C)

    def body(w_ref, g_ref, m_ref, v_ref, d_ref, mo_ref, vo_ref):
        d, mm, vv = _adamw(w_ref[...], g_ref[...], m_ref[...], v_ref[...])
        d_ref[...], mo_ref[...], vo_ref[...] = d, mm, vv

    spec = pl.BlockSpec((tr, C), lambda i: (i, 0))
    return pl.pallas_call(
        body, name=name, grid=(R // tr,), in_specs=[spec] * 4, out_specs=(spec,) * 3,
        out_shape=(jax.ShapeDtypeStruct((R, C), F32),) * 3, compiler_params=_params(("parallel",)),
    )(w, g, m, v)


def adam_many(items, name):
    n = len(items)

    def body(*refs):
        ins, outs = refs[:4 * n], refs[4 * n:]
        for i in range(n):
            w, g, m, v = [ins[4 * i + j][...] for j in range(4)]
            outs[3 * i][...], outs[3 * i + 1][...], outs[3 * i + 2][...] = _adamw(w, g, m, v)

    out_shape = [jax.ShapeDtypeStruct(w.shape, F32) for w, _, _, _ in items for _ in range(3)]
    res = pl.pallas_call(body, name=name, out_shape=tuple(out_shape), compiler_params=_params())(
        *[a for it in items for a in it])
    return [tuple(res[3 * i:3 * i + 3]) for i in range(n)]


def adam_reduce(parts, w, m, v, name):
    R, C = w.shape
    tr = _tile_rows(R, C)

    def body(p_ref, w_ref, m_ref, v_ref, g_ref, d_ref, mo_ref, vo_ref):
        g = p_ref[0].astype(F32)
        for j in range(1, NDEV):
            g = g + p_ref[j].astype(F32)
        g_ref[...] = g
        d, mm, vv = _adamw(w_ref[...], g, m_ref[...], v_ref[...])
        d_ref[...], mo_ref[...], vo_ref[...] = d, mm, vv

    spec = pl.BlockSpec((tr, C), lambda i: (i, 0))
    return pl.pallas_call(
        body, name=name, grid=(R // tr,),
        in_specs=[pl.BlockSpec((NDEV, tr, C), lambda i: (0, i, 0)), spec, spec, spec], out_specs=(spec,) * 4,
        out_shape=(jax.ShapeDtypeStruct((R, C), F32),) * 4, compiler_params=_params(("parallel",)),
    )(parts, w, m, v)


def adam_w_ada(c_act, dmod_cols, w, m, v):
    D, n = w.shape
    tn = 256

    def body(c_ref, dm_ref, w_ref, m_ref, v_ref, g_ref, d_ref, mo_ref, vo_ref):
        g = lax.dot_general(c_ref[...].astype(BF16), dm_ref[...].astype(BF16), (((0,), (0,)), ((), ())),
                            preferred_element_type=F32)
        g_ref[...] = g
        d, mm, vv = _adamw(w_ref[...], g, m_ref[...], v_ref[...])
        d_ref[...], mo_ref[...], vo_ref[...] = d, mm, vv

    spec = pl.BlockSpec((D, tn), lambda j: (0, j))
    return pl.pallas_call(
        body, name="adam_w_ada", grid=(n // tn,),
        in_specs=[pl.BlockSpec((NDEV, D), lambda j: (0, 0)), pl.BlockSpec((NDEV, tn), lambda j: (0, j)), spec, spec, spec],
        out_specs=(spec,) * 4, out_shape=(jax.ShapeDtypeStruct((D, n), F32),) * 4,
        compiler_params=_params(("parallel",)),
    )(c_act, dmod_cols, w, m, v)


def _block_diag(m):
    G, a, b = m.shape
    m4 = m.reshape(G // GB, GB, a, b)
    eye = jnp.eye(GB, dtype=m.dtype)
    return (m4[:, :, :, None, :] * eye[None, :, None, :, None]).reshape(G // GB, GB * a, GB * b)


def _diag_blocks(m, a, b):
    nb = m.shape[0]
    m5 = m.reshape(nb, GB, a, GB, b)
    on_diag = jnp.eye(GB, dtype=bool)[None, :, None, :, None]
    return jnp.where(on_diag, m5, 0.0).sum(axis=3).reshape(nb * GB, a, b)


def _flat_pad(parts, mult):
    flat = jnp.concatenate([p.reshape(-1) for p in parts])
    pad = (-flat.shape[0]) % mult
    return jnp.pad(flat, (0, pad))


def _split(flat, like):
    out, off = [], 0
    for p in like:
        out.append(flat[off:off + p.size].reshape(p.shape))
        off += p.size
    return out


def kernel(x, c, w_ada, b_ada, norm1_g, w_in, w_dw, b_dw, ln_g, ln_b, w_conv_out, a_re, a_im, log_dt, b_re, b_im, c_re, c_im, d_skip, w_glu_a, w_glu_b, w_out, norm2_g, w_ff1, w_ff2, final_g, loss_target, m_w_ada, m_b_ada, m_norm1_g, m_w_in, m_w_dw, m_b_dw, m_ln_g, m_ln_b, m_w_conv_out, m_a_re, m_a_im, m_log_dt, m_b_re, m_b_im, m_c_re, m_c_im, m_d_skip, m_w_glu_a, m_w_glu_b, m_w_out, m_norm2_g, m_w_ff1, m_w_ff2, m_final_g, v_w_ada, v_b_ada, v_norm1_g, v_w_in, v_w_dw, v_b_dw, v_ln_g, v_ln_b, v_w_conv_out, v_a_re, v_a_im, v_log_dt, v_b_re, v_b_im, v_c_re, v_c_im, v_d_skip, v_w_glu_a, v_w_glu_b, v_w_out, v_norm2_g, v_w_ff1, v_w_ff2, v_final_g):
    W = dict(w_ada=w_ada, b_ada=b_ada, norm1_g=norm1_g, w_in=w_in, w_dw=w_dw, b_dw=b_dw, ln_g=ln_g, ln_b=ln_b,
             w_conv_out=w_conv_out, a_re=a_re, a_im=a_im, log_dt=log_dt, b_re=b_re, b_im=b_im, c_re=c_re, c_im=c_im,
             d_skip=d_skip, w_glu_a=w_glu_a, w_glu_b=w_glu_b, w_out=w_out, norm2_g=norm2_g, w_ff1=w_ff1, w_ff2=w_ff2,
             final_g=final_g)
    Mo = dict(w_ada=m_w_ada, b_ada=m_b_ada, norm1_g=m_norm1_g, w_in=m_w_in, w_dw=m_w_dw, b_dw=m_b_dw, ln_g=m_ln_g,
              ln_b=m_ln_b, w_conv_out=m_w_conv_out, a_re=m_a_re, a_im=m_a_im, log_dt=m_log_dt, b_re=m_b_re, b_im=m_b_im,
              c_re=m_c_re, c_im=m_c_im, d_skip=m_d_skip, w_glu_a=m_w_glu_a, w_glu_b=m_w_glu_b, w_out=m_w_out,
              norm2_g=m_norm2_g, w_ff1=m_w_ff1, w_ff2=m_w_ff2, final_g=m_final_g)
    Vo = dict(w_ada=v_w_ada, b_ada=v_b_ada, norm1_g=v_norm1_g, w_in=v_w_in, w_dw=v_w_dw, b_dw=v_b_dw, ln_g=v_ln_g,
              ln_b=v_ln_b, w_conv_out=v_w_conv_out, a_re=v_a_re, a_im=v_a_im, log_dt=v_log_dt, b_re=v_b_re, b_im=v_b_im,
              c_re=v_c_re, c_im=v_c_im, d_skip=v_d_skip, w_glu_a=v_w_glu_a, w_glu_b=v_w_glu_b, w_out=v_w_out,
              norm2_g=v_norm2_g, w_ff1=v_w_ff1, w_ff2=v_w_ff2, final_g=v_final_g)
    names = list(W)

    me = _me()
    xs, tgt = x[0], loss_target[0]
    L, D = xs.shape
    CW = w_dw.shape[2] * NDEV
    G, P = a_re.shape[1], a_re.shape[2]
    H = b_re.shape[3]
    n_ada = w_ada.shape[2]

    (c_all,) = _exchange([c], "gather_c", True)
    b_cols = lax.dynamic_slice(b_ada, (0, me * n_ada), (1, n_ada))
    mod_cols, c_act = adaln_mod(c_all.reshape(NDEV, D), w_ada[0], b_cols)
    (mod_all,) = _exchange([mod_cols], "gather_mod", True)
    mod = lax.dynamic_slice(mod_all, (0, me, 0), (NDEV, 1, n_ada)).reshape(6, 1, D)
    shift1, scale1, gate1, shift2, scale2, gate2 = [mod[j] for j in range(6)]

    big = ["w_in", "w_conv_out", "w_glu_a", "w_glu_b", "w_out", "w_ff1", "w_ff2"]
    order = ["w_in", "w_dw"] + big[1:]
    shards = {k: W[k][0].astype(BF16) for k in big}
    shards["w_dw"] = jnp.pad(w_dw[0], ((0, HALO - CONV_K), (0, 0)))
    gather_handle = dict(zip(order, gather2_start([shards[k] for k in order], "gather_weights_start", mod_all)))

    def forward(ks, name, after):
        gather_handle.update(zip(ks, gather2_forward([gather_handle[k] for k in ks], name, after)))

    def weight(k, after):
        w = gather2_wait(gather_handle[k], "gather_wait_" + k, after)
        if k in ("w_out", "w_ff2"):
            w = w.reshape(1, w.shape[0] * w.shape[1], w.shape[2])
        elif k == "w_dw":
            w = w.transpose(1, 0, 2).reshape(HALO, CW)
        return w

    scatter_handle = {}

    def scatter(k, g):
        if g.shape[0] == 1:
            g = g.reshape(NDEV, -1, g.shape[2])
        (scatter_handle[k],), token = exchange_start([g], "scatter_start_" + k, False)
        return token

    big_out = {}

    def finish_weight(k, after):
        parts = exchange_wait(scatter_handle[k], after, "scatter_wait_" + k, False)
        big_out[k] = adam_reduce(parts, W[k][0], Mo[k][0], Vo[k][0], "adam_" + k)
        return big_out[k][1]

    u = prenorm(xs, norm1_g, scale1, shift1, "prenorm1")

    br2 = b_re[0].transpose(0, 2, 1).reshape(G * H, P)
    bi2 = b_im[0].transpose(0, 2, 1).reshape(G * H, P)
    ldt = log_dt[0].reshape(G, 1)
    expand = jnp.repeat(jnp.eye(G, dtype=F32), H, axis=0)
    lbr, lbi, bbr, bbi = s5_params(a_re[0], a_im[0], ldt, br2, bi2, expand)
    tabs = s5_tables(lbr.reshape(1, G * P), lbi.reshape(1, G * P))
    bdr, bdi = _block_diag(bbr.reshape(G, H, P)), _block_diag(bbi.reshape(G, H, P))
    cdr = _block_diag(c_re[0].transpose(0, 2, 1))
    cdi = _block_diag(c_im[0].transpose(0, 2, 1))
    cd2 = jnp.concatenate([cdr, -cdi], axis=1).astype(BF16)
    bdr3, bdi3 = _rhs3(bdr), _rhs3(bdi)
    bdt2 = jnp.concatenate([bdr.transpose(0, 2, 1), bdi.transpose(0, 2, 1)], axis=1).astype(BF16)
    cdrt3, cdit3 = _rhs3(cdr.transpose(0, 2, 1)), _rhs3(-cdi.transpose(0, 2, 1))

    forward(["w_in"], "gather_forward_in", (u, tabs, bdr3, bdi3, cd2, bdt2, cdrt3, cdit3))
    wg = {"w_in": weight("w_in", u)}
    proj = mm_nn(u, wg["w_in"], "in_proj")
    forward(["w_dw", "w_conv_out", "w_glu_a", "w_glu_b", "w_out"], "gather_forward_mix", proj)
    w_dw_full = weight("w_dw", proj)
    vs, vc = conv_fwd(proj, w_dw_full, b_dw, ln_g, ln_b)
    wg["w_conv_out"] = weight("w_conv_out", vs)
    y_conv = mm_nn(vs, wg["w_conv_out"], "conv_out", out_dtype=BF16)
    s_re, s_im, y_pre, yg = s5_fwd(proj, 2 * CW, bdr3, bdi3, cd2, tabs, d_skip)
    forward(["w_ff1"], "gather_forward_ff1", yg)
    wg["w_glu_a"] = weight("w_glu_a", yg)
    wg["w_glu_b"] = weight("w_glu_b", yg)
    ga = mm_nn(yg, wg["w_glu_a"], "glu_a", out_dtype=BF16)
    gb = mm_nn(yg, wg["w_glu_b"], "glu_b", out_dtype=BF16)
    merged = merge_fwd(proj, 3 * CW, y_conv, ga, gb)
    forward(["w_ff2"], "gather_forward_ff2", merged)
    wg["w_out"] = weight("w_out", merged)
    m_out = mm_nn(merged, wg["w_out"], "out_proj")
    h1, z = residual_norm(xs, m_out, gate1, norm2_g, scale2, shift2)
    wg["w_ff1"] = weight("w_ff1", z)
    act = mm_nn(z, wg["w_ff1"], "ff1", out_dtype=BF16, epi=lambda r: jnp.square(jnp.maximum(r, 0.0)))
    wg["w_ff2"] = weight("w_ff2", act)
    ff = mm_nn(act, wg["w_ff2"], "ff2")

    dh2, dff, loss_part, d_final_g, d_gate2 = loss_bwd(h1, ff, gate2, final_g.reshape(1, D), tgt)
    df = mm_nt(dff, wg["w_ff2"], "ff2_dx", out_dtype=BF16,
               epi=lambda r, a: r * (2.0 * jnp.sqrt(a.astype(F32))), extras=(act,))
    t = scatter("w_ff2", mm_tn(act, dff, 1, "ff2_dw", out_dtype=BF16))
    t = scatter("w_ff1", mm_tn(z, df, NDEV, "ff1_dw", out_dtype=BF16, dep=t))
    dz = mm_nt(df, wg["w_ff1"], "ff1_dx", out_dtype=BF16, dep=t)
    dh1, d_scale2, d_shift2, d_norm2_g, dmo, d_gate1 = norm_bwd(dz, h1, dh2, norm2_g, scale2, "norm2_bwd", gate1, m_out)
    t = scatter("w_out", mm_tn(merged, dmo, 1, "out_dw", out_dtype=BF16))
    dmerged = mm_nt(dmo, wg["w_out"], "out_dx", out_dtype=BF16, dep=t)
    dyc, dga, dgb, dproj_g = merge_bwd(dmerged, proj, 3 * CW, y_conv, ga, gb)
    t = scatter("w_conv_out", mm_tn(vs, dyc, NDEV, "conv_out_dw", out_dtype=BF16))
    t = scatter("w_glu_a", mm_tn(yg, dga, NDEV, "glu_a_dw", out_dtype=BF16, dep=t))
    t = scatter("w_glu_b", mm_tn(yg, dgb, NDEV, "glu_b_dw", out_dtype=BF16, dep=t))
    dvs = mm_nt(dyc, wg["w_conv_out"], "conv_out_dx", out_dtype=BF16, dep=t)
    dyg_a = mm_nt(dga, wg["w_glu_a"], "glu_a_dx", out_dtype=BF16, dep=t)
    dyg_b = mm_nt(dgb, wg["w_glu_b"], "glu_b_dx", out_dtype=BF16, dep=t)
    dvc, d_ln_g, d_ln_b = conv_ln_bwd(dvs, vc, ln_g, ln_b)
    dproj_s, d_d_skip, dcdr, dcdi, dbdr, dbdi, dlr8, dli8 = s5_bwd(
        dyg_a, dyg_b, y_pre, proj, 2 * CW, s_re, s_im, bdt2, cdrt3, cdit3, tabs, d_skip)
    dproj, d_w_dw, d_b_dw = conv_bwd(dvc, proj, w_dw_full, dproj_s, dproj_g)
    d_c_re = _diag_blocks(dcdr, H, P)
    d_c_im = _diag_blocks(dcdi, H, P)
    d_bbr = _diag_blocks(dbdr, H, P)
    d_bbi = _diag_blocks(dbdi, H, P)
    dlr = jnp.sum(dlr8, axis=0).reshape(G, P)
    dli = jnp.sum(dli8, axis=0).reshape(G, P)
    early_parts = [d_b_dw, d_ln_g, d_ln_b, dlr, dli, d_bbr, d_bbi, d_c_re, d_c_im, d_d_skip, d_norm2_g, d_final_g,
                   d_w_dw]
    pack_early = _flat_pad(early_parts, PACK).reshape(NDEV, -1, 1024)
    (early_scatter,), done = exchange_start([pack_early], "scatter_small_start", False)
    for k in ("w_ff2", "w_ff1", "w_out"):
        done = finish_weight(k, done)
    early_sum = sum_devices(exchange_wait(early_scatter, done, "scatter_small_wait", False), "sum_small_early")
    (early_gather,), done = exchange_start([early_sum], "gather_small_start", True)
    t = scatter("w_in", mm_tn(u, dproj, NDEV, "in_dw", out_dtype=BF16, dep=done))

    tot_early = exchange_wait(early_gather, t, "gather_small_wait", True)
    (g_b_dw, g_ln_g, g_ln_b, t_lr, t_li, t_bbr, t_bbi, g_c_re_t, g_c_im_t, g_d_skip,
     g_norm2_g, g_final_g, g_w_dw_full) = _split(tot_early.reshape(-1), early_parts)
    g_a_re, g_a_im, g_ldt, g_br2, g_bi2 = s5_params_bwd(
        a_re[0], a_im[0], ldt, br2, bi2, expand, t_lr, t_li, t_bbr.reshape(G * H, P), t_bbi.reshape(G * H, P))
    g_brt, g_bit = g_br2.reshape(G, H, P), g_bi2.reshape(G, H, P)
    g2 = {
        "b_dw": g_b_dw, "ln_g": g_ln_g, "ln_b": g_ln_b,
        "a_re": g_a_re, "a_im": g_a_im, "log_dt": g_ldt.reshape(1, G),
        "b_re": g_brt.transpose(0, 2, 1).reshape(G * P, H), "b_im": g_bit.transpose(0, 2, 1).reshape(G * P, H),
        "c_re": g_c_re_t.reshape(G * H, P), "c_im": g_c_im_t.reshape(G * H, P), "d_skip": g_d_skip,
        "norm2_g": g_norm2_g, "final_g": g_final_g,
        "w_dw": lax.dynamic_slice(g_w_dw_full, (0, me * (CW // NDEV)), (CONV_K, CW // NDEV)),
    }
    as2d = lambda k, a: a.reshape(g2[k].shape)
    grads, delta, new_m, new_v = {}, {}, {}, {}

    def adam_small(ks, name):
        outs = adam_many([(as2d(k, W[k]), g2[k], as2d(k, Mo[k]), as2d(k, Vo[k])) for k in ks], name)
        for k, o in zip(ks, outs):
            grads[k] = g2[k].reshape(W[k].shape)
            delta[k], new_m[k], new_v[k] = [a.reshape(W[k].shape) for a in o]
        return outs[0][0]

    after = adam_small([k for k in names if k in g2], "adam_small_early")
    for k in ("w_conv_out", "w_glu_a", "w_glu_b"):
        after = finish_weight(k, after)
    du = mm_nt(dproj, wg["w_in"], "in_dx", out_dtype=BF16, dep=after)
    grad_x, d_scale1, d_shift1, d_norm1_g = norm_bwd(du, xs, dh1, norm1_g, scale1, "norm1_bwd")

    dmod = jnp.concatenate([d_shift1, d_scale1, d_gate1, d_shift2, d_scale2, d_gate2], axis=1)
    late_parts = [dmod, d_norm1_g]
    pack_late = _flat_pad(late_parts, PACK).reshape(NDEV, -1, 1024)
    parts_late, dmod_from = _exchange([pack_late, dmod.reshape(NDEV, 1, n_ada)], "scatter_small_late", False)
    (tot_late,) = _exchange([sum_devices(parts_late, "sum_small_late")], "gather_small_late", True)
    g2["b_ada"], g2["norm1_g"] = _split(tot_late.reshape(-1), late_parts)
    adam_small(["b_ada", "norm1_g"], "adam_small_late")
    dmod_cols = dmod_from.reshape(NDEV, n_ada)

    g, d, mm, vv = adam_w_ada(c_act, dmod_cols, w_ada[0], m_w_ada[0], v_w_ada[0])
    grads["w_ada"], delta["w_ada"], new_m["w_ada"], new_v["w_ada"] = g[None], d[None], mm[None], vv[None]

    finish_weight("w_in", d)
    for k in big:
        g, d, mm, vv = big_out[k]
        grads[k], delta[k], new_m[k], new_v[k] = g[None], d[None], mm[None], vv[None]

    loss = lax.psum(loss_part[0, 0], ("x", "y", "c"))
    return (loss, grad_x[None], *[grads[k] for k in names], *[delta[k] for k in names],
            *[new_m[k] for k in names], *[new_v[k] for k in names])
```

```python
import math

import jax
import jax.numpy as jnp
from jax import lax
from jax.experimental import pallas as pl
from jax.experimental.pallas import tpu as pltpu

F32 = jnp.float32
BF16 = jnp.bfloat16
NDEV = 8
EPS = 1e-6
ADAM_LR, ADAM_B1, ADAM_B2, ADAM_EPS, ADAM_WD, ADAM_STEP = 0.001, 0.9, 0.999, 1e-08, 0.01, 10
CONV_K = 31
HALO = 32
GB = 8
S5_ROWS = 2048
HI = lax.Precision.HIGHEST
MESH = pl.DeviceIdType.MESH
VMEM_LIMIT = 56 * 1024 * 1024
MAX_CONTRACT = 4096
PACK_ROWS = 64
PACK = PACK_ROWS * 1024
ANY = pl.BlockSpec(memory_space=pl.ANY)


def _params(sem=None):
    if sem is None:
        return pltpu.CompilerParams(vmem_limit_bytes=VMEM_LIMIT)
    return pltpu.CompilerParams(dimension_semantics=sem, vmem_limit_bytes=VMEM_LIMIT)


def _sigmoid(v):
    return 1.0 / (1.0 + jnp.exp(-v))


def _me():
    return 4 * lax.axis_index("x") + 2 * lax.axis_index("y") + lax.axis_index("c")


def _peer(k):
    x, y, c = lax.axis_index("x"), lax.axis_index("y"), lax.axis_index("c")
    px = 1 - x if (k >> 2) & 1 else x
    py = 1 - y if (k >> 1) & 1 else y
    pc = 1 - c if k & 1 else c
    return (px, py, pc), 4 * px + 2 * py + pc


def _exchange(arrays, name, gather):
    n = len(arrays)
    out_shape = []
    for a in arrays:
        shp = (NDEV,) + a.shape if gather else a.shape
        out_shape.append(jax.ShapeDtypeStruct(shp, a.dtype))

    def body(*refs):
        ins, outs = refs[:n], refs[n:2 * n]
        send, recv, lsem = refs[2 * n:]
        me = _me()
        local = []
        for a in range(n):
            src = ins[a] if gather else ins[a].at[me]
            cp = pltpu.make_async_copy(src, outs[a].at[me], lsem.at[a])
            cp.start()
            local.append(cp)
        sends = []
        for a in range(n):
            for k in range(1, NDEV):
                dev, pidx = _peer(k)
                src = ins[a] if gather else ins[a].at[pidx]
                cp = pltpu.make_async_remote_copy(
                    src_ref=src, dst_ref=outs[a].at[me], send_sem=send.at[a * (NDEV - 1) + k - 1], recv_sem=recv.at[a * (NDEV - 1) + k - 1],
                    device_id=dev, device_id_type=MESH)
                cp.start()
                sends.append(cp)
        for a in range(n):
            for k in range(1, NDEV):
                dev, pidx = _peer(k)
                src = ins[a] if gather else ins[a].at[pidx]
                pltpu.make_async_remote_copy(
                    src_ref=src, dst_ref=outs[a].at[pidx], send_sem=send.at[a * (NDEV - 1) + k - 1], recv_sem=recv.at[a * (NDEV - 1) + k - 1],
                    device_id=dev, device_id_type=MESH).wait_recv()
        for cp in sends:
            cp.wait_send()
        for cp in local:
            cp.wait()

    return pl.pallas_call(
        body, name=name, out_shape=tuple(out_shape),
        in_specs=[ANY] * n, out_specs=tuple([ANY] * n),
        scratch_shapes=[pltpu.SemaphoreType.DMA((n * (NDEV - 1),)), pltpu.SemaphoreType.DMA((n * (NDEV - 1),)),
                        pltpu.SemaphoreType.DMA((n,))],
    )(*arrays)


HBM = pl.BlockSpec(memory_space=pltpu.HBM)
SEM = pl.BlockSpec(memory_space=pltpu.SEMAPHORE)
EFFECT = pltpu.SideEffectType.DATAFLOW_SIDE_EFFECTING
NPEER = NDEV - 1


def _landing(block_of_me, shape, dtype):
    land = lax.empty((NDEV,) + tuple(shape), dtype)
    start = (_me(),) + (0,) * len(shape)
    return pltpu.with_memory_space_constraint(lax.dynamic_update_slice(land, block_of_me[None], start), pltpu.HBM)


def exchange_start(arrays, name, gather, after=None):
    n = len(arrays)
    me = _me()
    deps = () if after is None else (after,)
    lands = []
    for a in arrays:
        if gather:
            lands.append(_landing(a, a.shape, a.dtype))
        else:
            mine = lax.dynamic_slice(a, (me,) + (0,) * (a.ndim - 1), (1,) + a.shape[1:])[0]
            lands.append(_landing(mine, a.shape[1:], a.dtype))
    srcs = [pltpu.with_memory_space_constraint(a, pltpu.HBM) for a in arrays]

    def body(*refs):
        ins, lnd = refs[:n], refs[n:2 * n]
        outs = refs[2 * n + len(deps):]
        sends, recvs, token = outs[:n], outs[n:2 * n], outs[-1]
        my = _me()
        for a in range(n):
            for k in range(1, NDEV):
                dev, pidx = _peer(k)
                src = ins[a] if gather else ins[a].at[pidx]
                pltpu.make_async_remote_copy(
                    src_ref=src, dst_ref=lnd[a].at[my], send_sem=sends[a].at[k - 1], recv_sem=recvs[a].at[k - 1],
                    device_id=dev, device_id_type=MESH).start()
        token[...] = jnp.zeros_like(token)

    out_shape = ([pltpu.SemaphoreType.DMA((NPEER,))] * (2 * n)
                 + [pltpu.HBM(a.shape, a.dtype) for a in srcs] + [pltpu.HBM(l.shape, l.dtype) for l in lands]
                 + [jax.ShapeDtypeStruct((8, 128), F32)])
    res = pl.pallas_call(
        body, name=name, out_shape=tuple(out_shape),
        in_specs=[HBM] * (2 * n) + [ANY] * len(deps),
        out_specs=tuple([SEM] * (2 * n) + [HBM] * (2 * n) + [pl.BlockSpec(memory_space=pltpu.VMEM)]),
        input_output_aliases={i: 2 * n + i for i in range(2 * n)},
        compiler_params=pltpu.CompilerParams(has_side_effects=EFFECT),
    )(*srcs, *lands, *deps)
    handles = [(res[a], res[n + a], res[2 * n + a], res[3 * n + a]) for a in range(n)]
    return handles, res[-1]


def exchange_wait(handle, after, name, gather):
    send_sem, recv_sem, src, land = handle

    def body(src_ref, land_ref, s_ref, r_ref, after_ref, src_out, land_out):
        for k in range(1, NDEV):
            dev, pidx = _peer(k)
            s = src_ref if gather else src_ref.at[pidx]
            cp = pltpu.make_async_remote_copy(
                src_ref=s, dst_ref=land_ref.at[pidx], send_sem=s_ref.at[k - 1], recv_sem=r_ref.at[k - 1],
                device_id=dev, device_id_type=MESH)
            cp.wait_send()
            cp.wait_recv()

    return pl.pallas_call(
        body, name=name, out_shape=(pltpu.HBM(src.shape, src.dtype), pltpu.HBM(land.shape, land.dtype)),
        in_specs=(HBM, HBM, SEM, SEM, ANY), out_specs=(HBM, HBM), input_output_aliases={0: 0, 1: 1},
        compiler_params=pltpu.CompilerParams(has_side_effects=EFFECT),
    )(src, land, send_sem, recv_sem, after)[1]


ICI_PEERS = (2, 4, 6)
SIBLING = 1


def gather2_start(blocks, name, after):
    m = len(blocks)
    lands = [_landing(b, b.shape, b.dtype) for b in blocks]
    srcs = [pltpu.with_memory_space_constraint(b, pltpu.HBM) for b in blocks]
    n = len(ICI_PEERS)

    def body(*refs):
        src, lnd = refs[:m], refs[m:2 * m]
        outs = refs[2 * m + 1:]
        send, recv_sib, recv_ici = outs[:m], outs[m:2 * m], outs[2 * m:3 * m]
        my = _me()
        for a in range(m):
            dev, _ = _peer(SIBLING)
            pltpu.make_async_remote_copy(src_ref=src[a], dst_ref=lnd[a].at[my], send_sem=send[a].at[0],
                                         recv_sem=recv_sib[a].at[0], device_id=dev, device_id_type=MESH).start()
            for j, k in enumerate(ICI_PEERS):
                dev, _ = _peer(k)
                pltpu.make_async_remote_copy(src_ref=src[a], dst_ref=lnd[a].at[my], send_sem=send[a].at[1 + j],
                                             recv_sem=recv_ici[a].at[j], device_id=dev, device_id_type=MESH).start()

    out_shape = ([pltpu.SemaphoreType.DMA((1 + n,))] * m + [pltpu.SemaphoreType.DMA((1,))] * m
                 + [pltpu.SemaphoreType.DMA((n,))] * m
                 + [pltpu.HBM(s.shape, s.dtype) for s in srcs] + [pltpu.HBM(l.shape, l.dtype) for l in lands])
    res = pl.pallas_call(
        body, name=name, out_shape=tuple(out_shape),
        in_specs=[HBM] * (2 * m) + [ANY], out_specs=tuple([SEM] * (3 * m) + [HBM] * (2 * m)),
        input_output_aliases={i: 3 * m + i for i in range(2 * m)},
        compiler_params=pltpu.CompilerParams(has_side_effects=EFFECT),
    )(*srcs, *lands, after)
    return [tuple(res[g * m + a] for g in range(5)) for a in range(m)]


def gather2_forward(handles, name, after):
    m = len(handles)
    n = len(ICI_PEERS)
    srcs, lands = [h[3] for h in handles], [h[4] for h in handles]
    deps = tuple(after) if isinstance(after, (tuple, list)) else (after,)

    def body(*refs):
        src, lnd, recv_ici = refs[:m], refs[m:2 * m], refs[2 * m:3 * m]
        outs = refs[3 * m + len(deps):]
        fsend, frecv = outs[:m], outs[m:2 * m]
        sib, _ = _peer(SIBLING)
        for a in range(m):
            for j, k in enumerate(ICI_PEERS):
                dev, pidx = _peer(k)
                pltpu.make_async_remote_copy(
                    src_ref=src[a], dst_ref=lnd[a].at[pidx], send_sem=fsend[a].at[j], recv_sem=recv_ici[a].at[j],
                    device_id=dev, device_id_type=MESH).wait_recv()
                pltpu.make_async_remote_copy(
                    src_ref=lnd[a].at[pidx], dst_ref=lnd[a].at[pidx], send_sem=fsend[a].at[j], recv_sem=frecv[a].at[j],
                    device_id=sib, device_id_type=MESH).start()

    out_shape = ([pltpu.SemaphoreType.DMA((n,))] * (2 * m)
                 + [pltpu.HBM(s.shape, s.dtype) for s in srcs] + [pltpu.HBM(l.shape, l.dtype) for l in lands])
    res = pl.pallas_call(
        body, name=name, out_shape=tuple(out_shape),
        in_specs=[HBM] * (2 * m) + [SEM] * m + [ANY] * len(deps),
        out_specs=tuple([SEM] * (2 * m) + [HBM] * (2 * m)),
        input_output_aliases={i: 2 * m + i for i in range(2 * m)},
        compiler_params=pltpu.CompilerParams(has_side_effects=EFFECT),
    )(*srcs, *lands, *[h[2] for h in handles], *deps)
    return [(handles[a][0], handles[a][1], res[a], res[m + a], res[2 * m + a], res[3 * m + a]) for a in range(m)]


def gather2_wait(handle, name, after):
    send, recv_sib, fsend, frecv, src, land = handle

    def body(src_ref, land_ref, send_ref, recv_sib_ref, fsend_ref, frecv_ref, after_ref, src_out, land_out):
        sib, sib_idx = _peer(SIBLING)
        own = pltpu.make_async_remote_copy(src_ref=src_ref, dst_ref=land_ref.at[sib_idx], send_sem=send_ref.at[0],
                                           recv_sem=recv_sib_ref.at[0], device_id=sib, device_id_type=MESH)
        own.wait_send()
        own.wait_recv()
        for j, k in enumerate(ICI_PEERS):
            dev, pidx = _peer(k)
            pltpu.make_async_remote_copy(src_ref=src_ref, dst_ref=land_ref.at[pidx], send_sem=send_ref.at[1 + j],
                                         recv_sem=frecv_ref.at[j], device_id=dev, device_id_type=MESH).wait_send()
            _, fidx = _peer(k ^ SIBLING)
            fwd = pltpu.make_async_remote_copy(src_ref=land_ref.at[pidx], dst_ref=land_ref.at[fidx],
                                               send_sem=fsend_ref.at[j], recv_sem=frecv_ref.at[j],
                                               device_id=sib, device_id_type=MESH)
            fwd.wait_send()
            fwd.wait_recv()

    return pl.pallas_call(
        body, name=name, out_shape=(pltpu.HBM(src.shape, src.dtype), pltpu.HBM(land.shape, land.dtype)),
        in_specs=(HBM, HBM, SEM, SEM, SEM, SEM, ANY), out_specs=(HBM, HBM), input_output_aliases={0: 0, 1: 1},
        compiler_params=pltpu.CompilerParams(has_side_effects=EFFECT),
    )(src, land, send, recv_sib, fsend, frecv, after)[1]


def _acc_steps(p, acc, k, nk, finish):
    if nk == 1:
        finish(p)
        return

    @pl.when(k == 0)
    def _():
        acc[...] = p

    @pl.when(k > 0)
    def _():
        acc[...] += p

    @pl.when(k == nk - 1)
    def _():
        finish(acc[...])


def _shards_per_step(J, n, tn, width):
    s = 1
    while n == tn and J % (2 * s) == 0 and 2 * s * tn <= width:
        s *= 2
    return s


def mm_nn(a, w3, name, out_dtype=F32, epi=None, extras=()):
    M, K = a.shape
    J, _, n = w3.shape
    tm, tn, tk = min(1024, M), min(1024, n), min(2048, K)
    q, nk, ne = n // tn, K // tk, len(extras)
    s = _shards_per_step(J, n, tn, 1024)

    def body(*refs):
        a_ref, w_ref = refs[:2]
        ex, o_ref, acc = refs[2:2 + ne], refs[2 + ne], refs[-1]
        av = a_ref[...]
        p = jnp.dot(av, w_ref[0], preferred_element_type=F32)
        if s > 1:
            p = jnp.concatenate([p] + [jnp.dot(av, w_ref[j], preferred_element_type=F32) for j in range(1, s)], axis=1)

        def finish(r):
            if epi is not None:
                r = epi(r, *[e[...] for e in ex])
            o_ref[...] = r.astype(out_dtype)

        _acc_steps(p, acc, pl.program_id(2), nk, finish)

    return pl.pallas_call(
        body, name=name, grid=(M // tm, (J // s) * q, nk),
        in_specs=[pl.BlockSpec((tm, tk), lambda i, j, k: (i, k)),
                  pl.BlockSpec((s, tk, tn), lambda i, j, k: (j // q, k, j % q))]
        + [pl.BlockSpec((tm, s * tn), lambda i, j, k: (i, j))] * ne,
        out_specs=pl.BlockSpec((tm, s * tn), lambda i, j, k: (i, j)),
        out_shape=jax.ShapeDtypeStruct((M, J * n), out_dtype),
        scratch_shapes=[pltpu.VMEM((tm, s * tn), F32)],
        compiler_params=_params(("parallel", "parallel", "arbitrary")),
    )(a, w3, *extras)


def mm_nt(dy, w3, name, out_dtype=F32, epi=None, extras=(), dep=None):
    M, _ = dy.shape
    J, K, n = w3.shape
    tm, tn, tkk = min(1024, M), min(MAX_CONTRACT, n), min(1024, K)
    q, ne = n // tn, len(extras)
    s = 1
    while q == 1 and J % (2 * s) == 0 and 2 * s * tn <= MAX_CONTRACT:
        s *= 2
    nk = (J // s) * q
    deps = () if dep is None else (dep,)

    def body(*refs):
        d_ref, w_ref = refs[:2]
        ex, o_ref, acc = refs[2:2 + ne], refs[-2], refs[-1]
        nt = (((1,), (1,)), ((), ()))
        p = lax.dot_general(d_ref[:, 0:tn], w_ref[0], nt, preferred_element_type=F32)
        for j in range(1, s):
            p = p + lax.dot_general(d_ref[:, j * tn:(j + 1) * tn], w_ref[j], nt, preferred_element_type=F32)

        def finish(r):
            if epi is not None:
                r = epi(r, *[e[...] for e in ex])
            o_ref[...] = r.astype(out_dtype)

        _acc_steps(p, acc, pl.program_id(2), nk, finish)

    return pl.pallas_call(
        body, name=name, grid=(M // tm, K // tkk, nk),
        in_specs=[pl.BlockSpec((tm, s * tn), lambda i, kk, c: (i, c)),
                  pl.BlockSpec((s, tkk, tn), lambda i, kk, c: (c // q, kk, c % q))]
        + [pl.BlockSpec((tm, tkk), lambda i, kk, c: (i, kk))] * ne + [ANY] * len(deps),
        out_specs=pl.BlockSpec((tm, tkk), lambda i, kk, c: (i, kk)),
        out_shape=jax.ShapeDtypeStruct((M, K), out_dtype),
        scratch_shapes=[pltpu.VMEM((tm, tkk), F32)],
        compiler_params=_params(("parallel", "parallel", "arbitrary")),
    )(dy, w3, *extras, *deps)


def mm_tn(a, dy, J, name, out_dtype=F32, dep=None):
    M, K = a.shape
    n = dy.shape[1] // J
    tm, tn, tkk = min(MAX_CONTRACT, M), min(1024, n), min(1024, K)
    q, nk = n // tn, M // tm
    s = _shards_per_step(J, n, tn, 1024)
    deps = () if dep is None else (dep,)

    def body(a_ref, d_ref, *rest):
        o_ref, acc = rest[-2:]
        p = lax.dot_general(a_ref[...], d_ref[...], (((0,), (0,)), ((), ())), preferred_element_type=F32)

        def finish(r):
            for j in range(s):
                o_ref[j] = r[:, j * tn:(j + 1) * tn].astype(out_dtype)

        _acc_steps(p, acc, pl.program_id(2), nk, finish)

    return pl.pallas_call(
        body, name=name, grid=(K // tkk, (J // s) * q, nk),
        in_specs=[pl.BlockSpec((tm, tkk), lambda kk, c, m: (m, kk)),
                  pl.BlockSpec((tm, s * tn), lambda kk, c, m: (m, c))] + [ANY] * len(deps),
        out_specs=pl.BlockSpec((s, tkk, tn), lambda kk, c, m: (c // q, kk, c % q)),
        out_shape=jax.ShapeDtypeStruct((J, K, n), out_dtype),
        scratch_shapes=[pltpu.VMEM((tkk, s * tn), F32)],
        compiler_params=_params(("parallel", "parallel", "arbitrary")),
    )(a, dy, *deps)


def _tm(L):
    return min(256, L)


def _row(w, cb=0, tm=None):
    return pl.BlockSpec((tm, w), lambda i: (i, cb))


def _vec(w, cb=0):
    return pl.BlockSpec((1, w), lambda i: (0, cb))


def _accum(ref, val, i):
    @pl.when(i == 0)
    def _():
        ref[...] = val

    @pl.when(i > 0)
    def _():
        ref[...] += val


def _colsum(v):
    return jnp.sum(v, axis=0, keepdims=True)


def _rms(v):
    return lax.rsqrt(jnp.mean(v * v, axis=-1, keepdims=True) + EPS)


def adaln_mod(c_all, w_ada, b_cols):
    B, D = c_all.shape
    n = w_ada.shape[1]
    tn = 512

    def body(c_ref, w_ref, b_ref, o_ref, ca_ref):
        cv = c_ref[...]
        ca = cv * _sigmoid(cv)
        ca_ref[...] = ca
        o_ref[...] = jnp.dot(ca.astype(BF16), w_ref[...].astype(BF16), preferred_element_type=F32) + b_ref[...]

    return pl.pallas_call(
        body, name="adaln_mod", grid=(n // tn,),
        in_specs=[pl.BlockSpec((B, D), lambda j: (0, 0)), pl.BlockSpec((D, tn), lambda j: (0, j)),
                  pl.BlockSpec((1, tn), lambda j: (0, j))],
        out_specs=(pl.BlockSpec((B, tn), lambda j: (0, j)), pl.BlockSpec((B, D), lambda j: (0, 0))),
        out_shape=(jax.ShapeDtypeStruct((B, n), F32), jax.ShapeDtypeStruct((B, D), F32)),
        compiler_params=_params(("arbitrary",)),
    )(c_all, w_ada, b_cols)


def prenorm(x, g, scale, shift, name):
    L, D = x.shape
    tm = _tm(L)

    def body(x_ref, g_ref, sc_ref, sh_ref, u_ref):
        xv = x_ref[...]
        u_ref[...] = (xv * _rms(xv) * g_ref[...] * (1.0 + sc_ref[...]) + sh_ref[...]).astype(BF16)

    return pl.pallas_call(
        body, name=name, grid=(L // tm,),
        in_specs=[_row(D, tm=tm), _vec(D), _vec(D), _vec(D)],
        out_specs=_row(D, tm=tm), out_shape=jax.ShapeDtypeStruct((L, D), BF16),
        compiler_params=_params(("parallel",)),
    )(x, g, scale, shift)


def _shifted_copies(buf, shifted, tm):
    n = HALO + tm - 8
    for r in range(1, 8):
        shifted[r - 1] = buf[pl.ds(r, n), :]


def _window(buf, shifted, off, tm):
    r, base = off % 8, off - off % 8
    if r == 0:
        return buf[pl.ds(base, tm), :]
    return shifted[r - 1, pl.ds(base, tm), :]


def conv_fwd(proj, w_dw, b_dw, ln_g, ln_b):
    L = proj.shape[0]
    C = w_dw.shape[1]
    tm = _tm(L)
    hb = tm // HALO

    def body(a_ref, g_ref, ah_ref, gh_ref, w_ref, b_ref, lg_ref, lb_ref, vs_ref, vc_ref, buf, shifted):
        i = pl.program_id(0)
        halo = ah_ref[...] * _sigmoid(gh_ref[...])
        buf[0:HALO, :] = halo * jnp.where(i > 0, 1.0, 0.0)
        buf[HALO:HALO + tm, :] = a_ref[...] * _sigmoid(g_ref[...])
        _shifted_copies(buf, shifted, tm)
        acc = jnp.zeros((tm, C), F32) + b_ref[...]
        for k in range(CONV_K):
            acc = acc + w_ref[k:k + 1, :] * _window(buf, shifted, HALO - (CONV_K - 1) + k, tm)
        vc_ref[...] = acc
        mu = jnp.mean(acc, axis=-1, keepdims=True)
        d = acc - mu
        var = jnp.mean(d * d, axis=-1, keepdims=True)
        ln = d * lax.rsqrt(var + EPS) * lg_ref[...] + lb_ref[...]
        vs_ref[...] = (ln * _sigmoid(ln)).astype(BF16)

    prev = lambda cb: pl.BlockSpec((HALO, C), lambda i: (jnp.maximum(i * hb - 1, 0), cb))
    return pl.pallas_call(
        body, name="conv_fwd", grid=(L // tm,),
        in_specs=[_row(C, 0, tm), _row(C, 1, tm), prev(0), prev(1),
                  pl.BlockSpec((HALO, C), lambda i: (0, 0)), _vec(C), _vec(C), _vec(C)],
        out_specs=(_row(C, tm=tm), _row(C, tm=tm)),
        out_shape=(jax.ShapeDtypeStruct((L, C), BF16), jax.ShapeDtypeStruct((L, C), F32)),
        scratch_shapes=[pltpu.VMEM((HALO + tm, C), F32), pltpu.VMEM((7, HALO + tm - 8, C), F32)],
        compiler_params=_params(("parallel",)),
    )(proj, proj, proj, proj, w_dw, b_dw, ln_g, ln_b)


def _gelu(v):
    return 0.5 * v * (1.0 + jnp.tanh(math.sqrt(2.0 / math.pi) * (v + 0.044715 * v * v * v)))


def _gelu_grad(v):
    k = math.sqrt(2.0 / math.pi)
    t = jnp.tanh(k * (v + 0.044715 * v * v * v))
    return 0.5 * (1.0 + t) + 0.5 * v * (1.0 - t * t) * k * (1.0 + 3.0 * 0.044715 * v * v)


def s5_param_fn(ar, ai, ldt, br, bi, expand):
    dt = jnp.exp(ldt)
    er = jnp.exp(ar * dt)
    th = ai * dt
    lbr, lbi = er * jnp.cos(th), er * jnp.sin(th)
    nr, ni = lbr - 1.0, lbi
    den = ar * ar + ai * ai
    qr, qi = (nr * ar + ni * ai) / den, (ni * ar - nr * ai) / den
    qre = jnp.dot(expand, qr, precision=HI, preferred_element_type=F32)
    qie = jnp.dot(expand, qi, precision=HI, preferred_element_type=F32)
    return lbr, lbi, qre * br - qie * bi, qre * bi + qie * br


def s5_params(ar, ai, ldt, br2, bi2, expand):
    def body(ar_ref, ai_ref, ld_ref, br_ref, bi_ref, e_ref, o1, o2, o3, o4):
        r = s5_param_fn(ar_ref[...], ai_ref[...], ld_ref[...], br_ref[...], bi_ref[...], e_ref[...])
        o1[...], o2[...], o3[...], o4[...] = r

    s2, s3 = jax.ShapeDtypeStruct(ar.shape, F32), jax.ShapeDtypeStruct(br2.shape, F32)
    return pl.pallas_call(body, name="s5_params", out_shape=(s2, s2, s3, s3), compiler_params=_params())(
        ar, ai, ldt, br2, bi2, expand)


def s5_params_bwd(ar, ai, ldt, br2, bi2, expand, dlr, dli, dbr, dbi):
    def body(ar_ref, ai_ref, ld_ref, br_ref, bi_ref, e_ref, c1, c2, c3, c4, o1, o2, o3, o4, o5):
        e = e_ref[...]
        fn = lambda a, b, c, d, f: s5_param_fn(a, b, c, d, f, e)
        _, vjp = jax.vjp(fn, ar_ref[...], ai_ref[...], ld_ref[...], br_ref[...], bi_ref[...])
        r = vjp((c1[...], c2[...], c3[...], c4[...]))
        o1[...], o2[...], o3[...], o4[...], o5[...] = r

    shapes = tuple(jax.ShapeDtypeStruct(v.shape, F32) for v in (ar, ai, ldt, br2, bi2))
    return pl.pallas_call(body, name="s5_params_bwd", out_shape=shapes, compiler_params=_params())(
        ar, ai, ldt, br2, bi2, expand, dlr, dli, dbr, dbi)


def s5_tables(lr, li):
    C = lr.shape[1]

    def body(lr_ref, li_ref, o_ref):
        row = lax.broadcasted_iota(jnp.int32, (8, C), 0)
        for rev in (0, 1):
            pr = jnp.broadcast_to(lr_ref[...], (8, C))
            pi = jnp.broadcast_to(-li_ref[...] if rev else li_ref[...], (8, C))
            br, bi = pr, pi
            pows = [(pr, pi)]
            for _ in range(7):
                pr, pi = pr * br - pi * bi, pr * bi + pi * br
                pows.append((pr, pi))
            base = 8 * rev
            for s, d in enumerate((1, 2, 4)):
                keep = (row + d <= 7) if rev else (row >= d)
                o_ref[base + 2 * s] = jnp.where(keep, pows[d - 1][0], 0.0)
                o_ref[base + 2 * s + 1] = jnp.where(keep, pows[d - 1][1], 0.0)
            cr, ci = jnp.zeros((8, C), F32), jnp.zeros((8, C), F32)
            for j in range(8):
                e = (8 - j) if rev else (j + 1)
                cr = jnp.where(row == j, pows[e - 1][0], cr)
                ci = jnp.where(row == j, pows[e - 1][1], ci)
            o_ref[base + 6] = cr
            o_ref[base + 7] = ci

    return pl.pallas_call(body, name="s5_tables", out_shape=jax.ShapeDtypeStruct((16, 8, C), F32),
                          compiler_params=_params())(lr, li)


def _tile_steps(xr, xi, tabs, rev):
    for s, d in enumerate((1, 2, 4)):
        tr, ti = tabs[2 * s], tabs[2 * s + 1]
        sh = (8 - d) if rev else d
        sr, si = pltpu.roll(xr, sh, 0), pltpu.roll(xi, sh, 0)
        xr, xi = xr + tr * sr - ti * si, xi + tr * si + ti * sr
    return xr, xi


def _tile_carry(xr, xi, tabs, cr, ci):
    tr, ti = tabs[6], tabs[7]
    return xr + tr * cr - ti * ci, xi + tr * ci + ti * cr


def _hi_lo(a):
    hi = a.astype(BF16)
    return hi, (a - hi.astype(F32)).astype(BF16)


def _lhs3(a):
    hi, lo = _hi_lo(a)
    return jnp.concatenate([hi, lo, hi], axis=1)


def _rhs3(m):
    hi, lo = _hi_lo(m)
    return jnp.concatenate([hi, hi, lo], axis=-2)


def s5_fwd(proj, col0, bdr3, bdi3, cd2, tabs, d_skip):
    L = proj.shape[0]
    nb, cw3, sw = bdr3.shape
    cw = cw3 // 3
    tl = min(S5_ROWS, L)
    cb0 = col0 // cw

    def body(u_ref, bdr_ref, bdi_ref, cd_ref, t_ref, dk_ref, sr_ref, si_ref, yp_ref, yg_ref, car):
        l = pl.program_id(1)

        @pl.when(l == 0)
        def _():
            car[...] = jnp.zeros_like(car)

        u = u_ref[...]
        u3 = _lhs3(u)
        sr_ref[...] = jnp.dot(u3, bdr_ref[...], preferred_element_type=F32)
        si_ref[...] = jnp.dot(u3, bdi_ref[...], preferred_element_type=F32)

        def pair(i, c):
            tabs = [t_ref[j] for j in range(8)]
            r0 = pl.multiple_of(i * 16, 16)
            lo, hi = pl.ds(r0, 8), pl.ds(r0 + 8, 8)
            a = _tile_steps(sr_ref[lo, :], si_ref[lo, :], tabs, False)
            b = _tile_steps(sr_ref[hi, :], si_ref[hi, :], tabs, False)
            ar, ai = _tile_carry(a[0], a[1], tabs, c[0], c[1])
            br, bi = _tile_carry(b[0], b[1], tabs, ar[7:8, :], ai[7:8, :])
            sr_ref[lo, :], si_ref[lo, :] = ar, ai
            sr_ref[hi, :], si_ref[hi, :] = br, bi
            return br[7:8, :], bi[7:8, :]

        c = lax.fori_loop(0, tl // 16, pair, (car[0:1, :], car[1:2, :]))
        car[0:1, :] = c[0]
        car[1:2, :] = c[1]
        s2 = jnp.concatenate([sr_ref[...].astype(BF16), si_ref[...].astype(BF16)], axis=1)
        y = jnp.dot(s2, cd_ref[...], preferred_element_type=F32) + dk_ref[...] * u
        yp_ref[...] = y
        yg_ref[...] = _gelu(y).astype(BF16)

    blk = lambda r, c: pl.BlockSpec((None, r, c), lambda b, l: (b, 0, 0))
    return pl.pallas_call(
        body, name="s5_fwd", grid=(nb, L // tl),
        in_specs=[pl.BlockSpec((tl, cw), lambda b, l: (l, cb0 + b)), blk(cw3, sw), blk(cw3, sw), blk(2 * sw, cw),
                  pl.BlockSpec((8, 8, sw), lambda b, l: (0, 0, b)), pl.BlockSpec((1, cw), lambda b, l: (0, b))],
        out_specs=(pl.BlockSpec((tl, sw), lambda b, l: (l, b)), pl.BlockSpec((tl, sw), lambda b, l: (l, b)),
                   pl.BlockSpec((tl, cw), lambda b, l: (l, b)), pl.BlockSpec((tl, cw), lambda b, l: (l, b))),
        out_shape=(jax.ShapeDtypeStruct((L, nb * sw), F32), jax.ShapeDtypeStruct((L, nb * sw), F32),
                   jax.ShapeDtypeStruct((L, nb * cw), F32), jax.ShapeDtypeStruct((L, nb * cw), BF16)),
        scratch_shapes=[pltpu.VMEM((8, sw), F32)],
        compiler_params=_params(("parallel", "arbitrary")),
    )(proj, bdr3, bdi3, cd2, tabs, d_skip)


def s5_bwd(dyg_a, dyg_b, yp, proj, col0, s_re, s_im, bdt2, cdrt3, cdit3, tabs, d_skip):
    L = proj.shape[0]
    nb, sw2, cw = bdt2.shape
    sw = sw2 // 2
    tl = min(S5_ROWS, L)
    nl = L // tl
    cb0 = col0 // cw
    tb = tl // 8

    def body(da_ref, db_ref, yp_ref, u_ref, sr_ref, si_ref, hr_ref, hi_ref, bdt_ref, cdrt_ref, cdit_ref,
             t_ref, dk_ref, du_ref, ddk_ref, dcr_ref, dci_ref, dbr_ref, dbi_ref, dlr_ref, dli_ref,
             gr, gi, pr, pi, car):
        l = pl.program_id(1)
        first = l == nl - 1

        @pl.when(l == 0)
        def _():
            car[...] = jnp.zeros_like(car)

        u = u_ref[...]
        dy = (da_ref[...].astype(F32) + db_ref[...].astype(F32)) * _gelu_grad(yp_ref[...])
        dy3 = _lhs3(dy)
        gr[...] = jnp.dot(dy3, cdrt_ref[...], preferred_element_type=F32)
        gi[...] = jnp.dot(dy3, cdit_ref[...], preferred_element_type=F32)
        inner = jnp.where(first, 0.0, 1.0)
        pr[0:8, :] = hr_ref[...] * inner
        pi[0:8, :] = hi_ref[...] * inner
        pr[8:8 + tl, :] = sr_ref[...]
        pi[8:8 + tl, :] = si_ref[...]
        row = lax.broadcasted_iota(jnp.int32, (8, sw), 0)

        def pair(j, c):
            tabs = [t_ref[8 + k] for k in range(8)]
            r0 = pl.multiple_of((tb // 2 - 1 - j) * 16, 16)
            lo, hi = pl.ds(r0, 8), pl.ds(r0 + 8, 8)
            b = _tile_steps(gr[hi, :], gi[hi, :], tabs, True)
            a = _tile_steps(gr[lo, :], gi[lo, :], tabs, True)
            br, bi = _tile_carry(b[0], b[1], tabs, c[0], c[1])
            ar, ai = _tile_carry(a[0], a[1], tabs, br[0:1, :], bi[0:1, :])
            gr[lo, :], gi[lo, :] = ar, ai
            gr[hi, :], gi[hi, :] = br, bi
            p0r, p1r, p2r = [pltpu.roll(pr[pl.ds(r0 + 8 * n, 8), :], 1, 0) for n in range(3)]
            p0i, p1i, p2i = [pltpu.roll(pi[pl.ds(r0 + 8 * n, 8), :], 1, 0) for n in range(3)]
            qar, qai = jnp.where(row == 0, p0r, p1r), jnp.where(row == 0, p0i, p1i)
            qbr, qbi = jnp.where(row == 0, p1r, p2r), jnp.where(row == 0, p1i, p2i)
            return (ar[0:1, :], ai[0:1, :], c[2] + (ar * qar + ai * qai) + (br * qbr + bi * qbi),
                    c[3] + (ai * qar - ar * qai) + (bi * qbr - br * qbi))

        z = jnp.zeros((8, sw), F32)
        c = lax.fori_loop(0, tb // 2, pair, (car[0:1, :], car[1:2, :], z, z))
        car[0:1, :] = c[0]
        car[1:2, :] = c[1]
        g_re, g_im = gr[...].astype(BF16), gi[...].astype(BF16)
        g2 = jnp.concatenate([g_re, g_im], axis=1)
        du_ref[...] = (dy * dk_ref[...] + jnp.dot(g2, bdt_ref[...], preferred_element_type=F32)).astype(BF16)
        tn = (((0,), (0,)), ((), ()))
        dyb, ub = dy.astype(BF16), u.astype(BF16)
        _accum(ddk_ref, _colsum(dy * u), l)
        _accum(dcr_ref, lax.dot_general(dyb, sr_ref[...].astype(BF16), tn, preferred_element_type=F32), l)
        _accum(dci_ref, -lax.dot_general(dyb, si_ref[...].astype(BF16), tn, preferred_element_type=F32), l)
        _accum(dbr_ref, lax.dot_general(ub, g_re, tn, preferred_element_type=F32), l)
        _accum(dbi_ref, lax.dot_general(ub, g_im, tn, preferred_element_type=F32), l)
        _accum(dlr_ref, c[2], l)
        _accum(dli_ref, c[3], l)

    rl = lambda l: nl - 1 - l
    cblk = lambda w, off=0: pl.BlockSpec((tl, w), lambda b, l: (rl(l), off + b))
    halo = pl.BlockSpec((8, sw), lambda b, l: (jnp.maximum(rl(l) * tb - 1, 0), b))
    mat = lambda r, c: pl.BlockSpec((None, r, c), lambda b, l: (b, 0, 0))
    return pl.pallas_call(
        body, name="s5_bwd", grid=(nb, nl),
        in_specs=[cblk(cw), cblk(cw), cblk(cw), cblk(cw, cb0), cblk(sw), cblk(sw), halo, halo,
                  mat(2 * sw, cw), mat(3 * cw, sw), mat(3 * cw, sw),
                  pl.BlockSpec((16, 8, sw), lambda b, l: (0, 0, b)), pl.BlockSpec((1, cw), lambda b, l: (0, b))],
        out_specs=(cblk(cw), pl.BlockSpec((1, cw), lambda b, l: (0, b)), mat(cw, sw), mat(cw, sw), mat(cw, sw), mat(cw, sw),
                   pl.BlockSpec((8, sw), lambda b, l: (0, b)), pl.BlockSpec((8, sw), lambda b, l: (0, b))),
        out_shape=(jax.ShapeDtypeStruct((L, nb * cw), BF16), jax.ShapeDtypeStruct((1, nb * cw), F32),
                   jax.ShapeDtypeStruct((nb, cw, sw), F32), jax.ShapeDtypeStruct((nb, cw, sw), F32),
                   jax.ShapeDtypeStruct((nb, cw, sw), F32), jax.ShapeDtypeStruct((nb, cw, sw), F32),
                   jax.ShapeDtypeStruct((8, nb * sw), F32), jax.ShapeDtypeStruct((8, nb * sw), F32)),
        scratch_shapes=[pltpu.VMEM((tl, sw), F32), pltpu.VMEM((tl, sw), F32),
                        pltpu.VMEM((tl + 8, sw), F32), pltpu.VMEM((tl + 8, sw), F32), pltpu.VMEM((8, sw), F32)],
        compiler_params=_params(("parallel", "arbitrary")),
    )(dyg_a, dyg_b, yp, proj, s_re, s_im, s_re, s_im, bdt2, cdrt3, cdit3, tabs, d_skip)


def merge_fwd(proj, col_gc, y_conv, ga, gb):
    L, D = y_conv.shape
    tm = _tm(L)
    h = D // 2
    c0 = col_gc // h

    def body(p0, p1, p2, p3, yc_ref, ga_ref, gb_ref, o_ref):
        gc, gs = (p0, p1), (p2, p3)
        for s in range(2):
            cols = slice(s * h, (s + 1) * h)
            y_ssm = ga_ref[:, cols].astype(F32) * _sigmoid(gb_ref[:, cols].astype(F32))
            o_ref[:, cols] = (_sigmoid(gc[s][...]) * yc_ref[:, cols].astype(F32)
                              + _sigmoid(gs[s][...]) * y_ssm).astype(BF16)

    return pl.pallas_call(
        body, name="merge_fwd", grid=(L // tm,),
        in_specs=[_row(h, c0 + s, tm) for s in range(4)] + [_row(D, tm=tm)] * 3,
        out_specs=_row(D, tm=tm), out_shape=jax.ShapeDtypeStruct((L, D), BF16),
        compiler_params=_params(("parallel",)),
    )(proj, proj, proj, proj, y_conv, ga, gb)


def residual_norm(x, m_out, gate, g, scale, shift):
    L, D = x.shape
    tm = _tm(L)

    def body(x_ref, m_ref, gt_ref, g_ref, sc_ref, sh_ref, h_ref, z_ref):
        h = x_ref[...] + gt_ref[...] * m_ref[...]
        h_ref[...] = h
        z_ref[...] = (h * _rms(h) * g_ref[...] * (1.0 + sc_ref[...]) + sh_ref[...]).astype(BF16)

    return pl.pallas_call(
        body, name="residual_norm", grid=(L // tm,),
        in_specs=[_row(D, tm=tm), _row(D, tm=tm), _vec(D), _vec(D), _vec(D), _vec(D)],
        out_specs=(_row(D, tm=tm), _row(D, tm=tm)),
        out_shape=(jax.ShapeDtypeStruct((L, D), F32), jax.ShapeDtypeStruct((L, D), BF16)),
        compiler_params=_params(("parallel",)),
    )(x, m_out, gate, g, scale, shift)


def loss_bwd(h1, ff, gate2, final_g, target):
    L, D = h1.shape
    tm = _tm(L)

    def body(h_ref, f_ref, gt_ref, g_ref, t_ref, dh_ref, dff_ref, loss_ref, dg_ref, dgt_ref):
        i = pl.program_id(0)
        ffv = f_ref[...]
        h2 = h_ref[...] + gt_ref[...] * ffv
        r = _rms(h2)
        n = h2 * r
        err = n * g_ref[...] - t_ref[...]
        per_tok = jnp.mean(err * err, axis=-1, keepdims=True)
        _accum(loss_ref, 0.5 * jnp.sum(per_tok, axis=0, keepdims=True), i)
        dy = err * (1.0 / D)
        _accum(dg_ref, _colsum(dy * n), i)
        dn = dy * g_ref[...]
        dh2 = r * (dn - n * jnp.mean(dn * n, axis=-1, keepdims=True))
        dh_ref[...] = dh2
        dff_ref[...] = (gt_ref[...] * dh2).astype(BF16)
        _accum(dgt_ref, _colsum(dh2 * ffv), i)

    return pl.pallas_call(
        body, name="loss_bwd", grid=(L // tm,),
        in_specs=[_row(D, tm=tm), _row(D, tm=tm), _vec(D), _vec(D), _row(D, tm=tm)],
        out_specs=(_row(D, tm=tm), _row(D, tm=tm), pl.BlockSpec((1, 1), lambda i: (0, 0)), _vec(D), _vec(D)),
        out_shape=(jax.ShapeDtypeStruct((L, D), F32), jax.ShapeDtypeStruct((L, D), BF16),
                   jax.ShapeDtypeStruct((1, 1), F32), jax.ShapeDtypeStruct((1, D), F32), jax.ShapeDtypeStruct((1, D), F32)),
        compiler_params=_params(("arbitrary",)),
    )(h1, ff, gate2, final_g, target)


def norm_bwd(dz, h, dh_in, g, scale, name, gate=None, m_out=None):
    L, D = h.shape
    tm = _tm(L)
    tail = gate is not None

    def body(*refs):
        dz_ref, h_ref, di_ref, g_ref, sc_ref = refs[:5]
        rest = refs[5:]
        if tail:
            gt_ref, m_ref = rest[:2]
            rest = rest[2:]
        dh_ref, dsc_ref, dsh_ref, dg_ref = rest[:4]
        i = pl.program_id(0)
        hv, dzv = h_ref[...], dz_ref[...].astype(F32)
        r = _rms(hv)
        n = hv * r
        _accum(dsc_ref, _colsum(dzv * n * g_ref[...]), i)
        _accum(dsh_ref, _colsum(dzv), i)
        dzn = dzv * (1.0 + sc_ref[...])
        _accum(dg_ref, _colsum(dzn * n), i)
        dn = dzn * g_ref[...]
        dh = di_ref[...] + r * (dn - n * jnp.mean(dn * n, axis=-1, keepdims=True))
        dh_ref[...] = dh
        if tail:
            dmo_ref, dgt_ref = rest[4:]
            dmo_ref[...] = (gt_ref[...] * dh).astype(BF16)
            _accum(dgt_ref, _colsum(dh * m_ref[...]), i)

    ins = [dz, h, dh_in, g, scale]
    in_specs = [_row(D, tm=tm)] * 3 + [_vec(D)] * 2
    out_specs = [_row(D, tm=tm), _vec(D), _vec(D), _vec(D)]
    out_shape = [jax.ShapeDtypeStruct((L, D), F32)] + [jax.ShapeDtypeStruct((1, D), F32)] * 3
    if tail:
        ins += [gate, m_out]
        in_specs += [_vec(D), _row(D, tm=tm)]
        out_specs += [_row(D, tm=tm), _vec(D)]
        out_shape += [jax.ShapeDtypeStruct((L, D), BF16), jax.ShapeDtypeStruct((1, D), F32)]
    return pl.pallas_call(
        body, name=name, grid=(L // tm,), in_specs=in_specs, out_specs=tuple(out_specs), out_shape=tuple(out_shape),
        compiler_params=_params(("arbitrary",)),
    )(*ins)


def merge_bwd(dmerged, proj, col_gc, y_conv, ga, gb):
    L, D = y_conv.shape
    tm = _tm(L)
    h = D // 2
    c0 = col_gc // h

    def body(dm_ref, p0, p1, p2, p3, yc_ref, ga_ref, gb_ref, dyc_ref, dga_ref, dgb_ref, dg_ref):
        gc, gs = (p0, p1), (p2, p3)
        for s in range(2):
            cols = slice(s * h, (s + 1) * h)
            dm = dm_ref[:, cols].astype(F32)
            sc, ss, sb = _sigmoid(gc[s][...]), _sigmoid(gs[s][...]), _sigmoid(gb_ref[:, cols].astype(F32))
            gav = ga_ref[:, cols].astype(F32)
            dyc_ref[:, cols] = (dm * sc).astype(BF16)
            dg_ref[:, cols] = (dm * yc_ref[:, cols].astype(F32) * sc * (1.0 - sc)).astype(BF16)
            dg_ref[:, D + s * h:D + (s + 1) * h] = (dm * gav * sb * ss * (1.0 - ss)).astype(BF16)
            dys = dm * ss
            dga_ref[:, cols] = (dys * sb).astype(BF16)
            dgb_ref[:, cols] = (dys * gav * sb * (1.0 - sb)).astype(BF16)

    return pl.pallas_call(
        body, name="merge_bwd", grid=(L // tm,),
        in_specs=[_row(D, tm=tm)] + [_row(h, c0 + s, tm) for s in range(4)] + [_row(D, tm=tm)] * 3,
        out_specs=(_row(D, tm=tm), _row(D, tm=tm), _row(D, tm=tm), _row(2 * D, tm=tm)),
        out_shape=(jax.ShapeDtypeStruct((L, D), BF16),) * 3 + (jax.ShapeDtypeStruct((L, 2 * D), BF16),),
        compiler_params=_params(("parallel",)),
    )(dmerged, proj, proj, proj, proj, y_conv, ga, gb)


def conv_ln_bwd(dvs, vc, ln_g, ln_b):
    L, C = vc.shape
    tm = _tm(L)

    def body(d_ref, v_ref, g_ref, b_ref, o_ref, dg_ref, db_ref):
        i = pl.program_id(0)
        v = v_ref[...]
        mu = jnp.mean(v, axis=-1, keepdims=True)
        d = v - mu
        rstd = lax.rsqrt(jnp.mean(d * d, axis=-1, keepdims=True) + EPS)
        xh = d * rstd
        ln = xh * g_ref[...] + b_ref[...]
        sg = _sigmoid(ln)
        dln = d_ref[...].astype(F32) * sg * (1.0 + ln * (1.0 - sg))
        _accum(dg_ref, _colsum(dln * xh), i)
        _accum(db_ref, _colsum(dln), i)
        dxh = dln * g_ref[...]
        o_ref[...] = rstd * (dxh - jnp.mean(dxh, axis=-1, keepdims=True)
                             - xh * jnp.mean(dxh * xh, axis=-1, keepdims=True))

    return pl.pallas_call(
        body, name="conv_ln_bwd", grid=(L // tm,),
        in_specs=[_row(C, tm=tm), _row(C, tm=tm), _vec(C), _vec(C)],
        out_specs=(_row(C, tm=tm), _vec(C), _vec(C)),
        out_shape=(jax.ShapeDtypeStruct((L, C), F32), jax.ShapeDtypeStruct((1, C), F32), jax.ShapeDtypeStruct((1, C), F32)),
        compiler_params=_params(("arbitrary",)),
    )(dvs, vc, ln_g, ln_b)


def conv_bwd(dvc, proj, w_dw, dproj_s, dproj_g):
    L, C = dvc.shape
    ws, wgt = dproj_s.shape[1], dproj_g.shape[1]
    tm = _tm(L)
    hb = tm // HALO
    last = L // HALO - 1
    nt = L // tm

    def body(d_ref, dn_ref, a_ref, g_ref, ah_ref, gh_ref, w_ref, ps_ref, pg_ref, o_ref, dw_ref, db_ref,
             dbuf, vbuf, dsh, vsh, dw8):
        i = pl.program_id(0)
        o_ref[:, 2 * C:2 * C + ws] = ps_ref[...]
        o_ref[:, 2 * C + ws:2 * C + ws + wgt] = pg_ref[...]
        dcur = d_ref[...]
        dbuf[0:tm, :] = dcur
        dbuf[tm:tm + HALO, :] = dn_ref[...] * jnp.where(i < nt - 1, 1.0, 0.0)
        av, sg = a_ref[...], _sigmoid(g_ref[...])
        vbuf[0:HALO, :] = ah_ref[...] * _sigmoid(gh_ref[...]) * jnp.where(i > 0, 1.0, 0.0)
        vbuf[HALO:HALO + tm, :] = av * sg
        _shifted_copies(dbuf, dsh, tm)
        _shifted_copies(vbuf, vsh, tm)
        dv = jnp.zeros((tm, C), F32)
        for k in range(CONV_K):
            dv = dv + w_ref[k:k + 1, :] * _window(dbuf, dsh, CONV_K - 1 - k, tm)
        o_ref[:, 0:C] = (dv * sg).astype(BF16)
        o_ref[:, C:2 * C] = (dv * av * sg * (1.0 - sg)).astype(BF16)

        @pl.when(i == 0)
        def _():
            dw8[...] = jnp.zeros_like(dw8)

        for k in range(CONV_K):
            prod = dcur * _window(vbuf, vsh, HALO - (CONV_K - 1) + k, tm)
            part = prod[0:8, :]
            for j in range(1, tm // 8):
                part = part + prod[8 * j:8 * j + 8, :]
            dw8[k] += part
        _accum(db_ref, _colsum(dcur), i)

        @pl.when(i == nt - 1)
        def _():
            for k in range(CONV_K):
                dw_ref[k:k + 1, :] = _colsum(dw8[k])
            dw_ref[CONV_K:HALO, :] = jnp.zeros((HALO - CONV_K, C), F32)

    prev = lambda cb: pl.BlockSpec((HALO, C), lambda i: (jnp.maximum(i * hb - 1, 0), cb))
    return pl.pallas_call(
        body, name="conv_bwd", grid=(nt,),
        in_specs=[_row(C, tm=tm), pl.BlockSpec((HALO, C), lambda i: (jnp.minimum((i + 1) * hb, last), 0)),
                  _row(C, 0, tm), _row(C, 1, tm), prev(0), prev(1), pl.BlockSpec((HALO, C), lambda i: (0, 0)),
                  _row(ws, tm=tm), _row(wgt, tm=tm)],
        out_specs=(_row(2 * C + ws + wgt, tm=tm), pl.BlockSpec((HALO, C), lambda i: (0, 0)), _vec(C)),
        out_shape=(jax.ShapeDtypeStruct((L, 2 * C + ws + wgt), BF16), jax.ShapeDtypeStruct((HALO, C), F32),
                   jax.ShapeDtypeStruct((1, C), F32)),
        scratch_shapes=[pltpu.VMEM((tm + HALO, C), F32), pltpu.VMEM((HALO + tm, C), F32),
                        pltpu.VMEM((7, HALO + tm - 8, C), F32), pltpu.VMEM((7, HALO + tm - 8, C), F32),
                        pltpu.VMEM((CONV_K, 8, C), F32)],
        compiler_params=_params(("arbitrary",)),
    )(dvc, dvc, proj, proj, proj, proj, w_dw, dproj_s, dproj_g)


def _adamw(w, g, m, v):
    m = ADAM_B1 * m + (1.0 - ADAM_B1) * g
    v = ADAM_B2 * v + (1.0 - ADAM_B2) * (g * g)
    m_hat = m / (1.0 - ADAM_B1 ** ADAM_STEP)
    v_hat = v / (1.0 - ADAM_B2 ** ADAM_STEP)
    delta = -ADAM_LR * (m_hat / (jnp.sqrt(v_hat) + ADAM_EPS) + ADAM_WD * w)
    return delta, m, v


def _tile_rows(R, C):
    tr = 8
    while tr * 2 * C <= 128 * 1024 and R % (tr * 2) == 0:
        tr *= 2
    assert R % tr == 0, (R, C)
    return tr


def sum_devices(parts, name):
    _, R, C = parts.shape
    tr = _tile_rows(R, C)

    def body(p_ref, o_ref):
        s = p_ref[0]
        for j in range(1, NDEV):
            s = s + p_ref[j]
        o_ref[...] = s

    return pl.pallas_call(
        body, name=name, grid=(R // tr,),
        in_specs=[pl.BlockSpec((NDEV, tr, C), lambda i: (0, i, 0))],
        out_specs=pl.BlockSpec((tr, C), lambda i: (i, 0)), out_shape=jax.ShapeDtypeStruct((R, C), F32),
        compiler_params=_params(("parallel",)),
    )(parts)


def adam_many(items, name):
    n = len(items)

    def body(*refs):
        ins, outs = refs[:4 * n], refs[4 * n:]
        for i in range(n):
            w, g, m, v = [ins[4 * i + j][...] for j in range(4)]
            outs[3 * i][...], outs[3 * i + 1][...], outs[3 * i + 2][...] = _adamw(w, g, m, v)

    out_shape = [jax.ShapeDtypeStruct(w.shape, F32) for w, _, _, _ in items for _ in range(3)]
    res = pl.pallas_call(body, name=name, out_shape=tuple(out_shape), compiler_params=_params())(
        *[a for it in items for a in it])
    return [tuple(res[3 * i:3 * i + 3]) for i in range(n)]


def adam_reduce(parts, w, m, v, name):
    R, C = w.shape
    tr = _tile_rows(R, C)

    def body(p_ref, w_ref, m_ref, v_ref, g_ref, d_ref, mo_ref, vo_ref):
        g = p_ref[0].astype(F32)
        for j in range(1, NDEV):
            g = g + p_ref[j].astype(F32)
        g_ref[...] = g
        d, mm, vv = _adamw(w_ref[...], g, m_ref[...], v_ref[...])
        d_ref[...], mo_ref[...], vo_ref[...] = d, mm, vv

    spec = pl.BlockSpec((tr, C), lambda i: (i, 0))
    return pl.pallas_call(
        body, name=name, grid=(R // tr,),
        in_specs=[pl.BlockSpec((NDEV, tr, C), lambda i: (0, i, 0)), spec, spec, spec], out_specs=(spec,) * 4,
        out_shape=(jax.ShapeDtypeStruct((R, C), F32),) * 4, compiler_params=_params(("parallel",)),
    )(parts, w, m, v)


def adam_w_ada(c_act, dmod_cols, w, m, v):
    D, n = w.shape
    tn = 256

    def body(c_ref, dm_ref, w_ref, m_ref, v_ref, g_ref, d_ref, mo_ref, vo_ref):
        g = lax.dot_general(c_ref[...].astype(BF16), dm_ref[...].astype(BF16), (((0,), (0,)), ((), ())),
                            preferred_element_type=F32)
        g_ref[...] = g
        d, mm, vv = _adamw(w_ref[...], g, m_ref[...], v_ref[...])
        d_ref[...], mo_ref[...], vo_ref[...] = d, mm, vv

    spec = pl.BlockSpec((D, tn), lambda j: (0, j))
    return pl.pallas_call(
        body, name="adam_w_ada", grid=(n // tn,),
        in_specs=[pl.BlockSpec((NDEV, D), lambda j: (0, 0)), pl.BlockSpec((NDEV, tn), lambda j: (0, j)), spec, spec, spec],
        out_specs=(spec,) * 4, out_shape=(jax.ShapeDtypeStruct((D, n), F32),) * 4,
        compiler_params=_params(("parallel",)),
    )(c_act, dmod_cols, w, m, v)


def _block_diag(m):
    G, a, b = m.shape
    m4 = m.reshape(G // GB, GB, a, b)
    eye = jnp.eye(GB, dtype=m.dtype)
    return (m4[:, :, :, None, :] * eye[None, :, None, :, None]).reshape(G // GB, GB * a, GB * b)


def _diag_blocks(m, a, b):
    nb = m.shape[0]
    m5 = m.reshape(nb, GB, a, GB, b)
    on_diag = jnp.eye(GB, dtype=bool)[None, :, None, :, None]
    return jnp.where(on_diag, m5, 0.0).sum(axis=3).reshape(nb * GB, a, b)


def _flat_pad(parts, mult):
    flat = jnp.concatenate([p.reshape(-1) for p in parts])
    pad = (-flat.shape[0]) % mult
    return jnp.pad(flat, (0, pad))


def _split(flat, like):
    out, off = [], 0
    for p in like:
        out.append(flat[off:off + p.size].reshape(p.shape))
        off += p.size
    return out


def kernel(x, c, w_ada, b_ada, norm1_g, w_in, w_dw, b_dw, ln_g, ln_b, w_conv_out, a_re, a_im, log_dt, b_re, b_im, c_re, c_im, d_skip, w_glu_a, w_glu_b, w_out, norm2_g, w_ff1, w_ff2, final_g, loss_target, m_w_ada, m_b_ada, m_norm1_g, m_w_in, m_w_dw, m_b_dw, m_ln_g, m_ln_b, m_w_conv_out, m_a_re, m_a_im, m_log_dt, m_b_re, m_b_im, m_c_re, m_c_im, m_d_skip, m_w_glu_a, m_w_glu_b, m_w_out, m_norm2_g, m_w_ff1, m_w_ff2, m_final_g, v_w_ada, v_b_ada, v_norm1_g, v_w_in, v_w_dw, v_b_dw, v_ln_g, v_ln_b, v_w_conv_out, v_a_re, v_a_im, v_log_dt, v_b_re, v_b_im, v_c_re, v_c_im, v_d_skip, v_w_glu_a, v_w_glu_b, v_w_out, v_norm2_g, v_w_ff1, v_w_ff2, v_final_g):
    W = dict(w_ada=w_ada, b_ada=b_ada, norm1_g=norm1_g, w_in=w_in, w_dw=w_dw, b_dw=b_dw, ln_g=ln_g, ln_b=ln_b,
             w_conv_out=w_conv_out, a_re=a_re, a_im=a_im, log_dt=log_dt, b_re=b_re, b_im=b_im, c_re=c_re, c_im=c_im,
             d_skip=d_skip, w_glu_a=w_glu_a, w_glu_b=w_glu_b, w_out=w_out, norm2_g=norm2_g, w_ff1=w_ff1, w_ff2=w_ff2,
             final_g=final_g)
    Mo = dict(w_ada=m_w_ada, b_ada=m_b_ada, norm1_g=m_norm1_g, w_in=m_w_in, w_dw=m_w_dw, b_dw=m_b_dw, ln_g=m_ln_g,
              ln_b=m_ln_b, w_conv_out=m_w_conv_out, a_re=m_a_re, a_im=m_a_im, log_dt=m_log_dt, b_re=m_b_re, b_im=m_b_im,
              c_re=m_c_re, c_im=m_c_im, d_skip=m_d_skip, w_glu_a=m_w_glu_a, w_glu_b=m_w_glu_b, w_out=m_w_out,
              norm2_g=m_norm2_g, w_ff1=m_w_ff1, w_ff2=m_w_ff2, final_g=m_final_g)
    Vo = dict(w_ada=v_w_ada, b_ada=v_b_ada, norm1_g=v_norm1_g, w_in=v_w_in, w_dw=v_w_dw, b_dw=v_b_dw, ln_g=v_ln_g,
              ln_b=v_ln_b, w_conv_out=v_w_conv_out, a_re=v_a_re, a_im=v_a_im, log_dt=v_log_dt, b_re=v_b_re, b_im=v_b_im,
              c_re=v_c_re, c_im=v_c_im, d_skip=v_d_skip, w_glu_a=v_w_glu_a, w_glu_b=v_w_glu_b, w_out=v_w_out,
              norm2_g=v_norm2_g, w_ff1=v_w_ff1, w_ff2=v_w_ff2, final_g=v_final_g)
    names = list(W)

    me = _me()
    xs, tgt = x[0], loss_target[0]
    L, D = xs.shape
    CW = w_dw.shape[2] * NDEV
    G, P = a_re.shape[1], a_re.shape[2]
    H = b_re.shape[3]
    n_ada = w_ada.shape[2]

    (c_all,) = _exchange([c], "gather_c", True)
    b_cols = lax.dynamic_slice(b_ada, (0, me * n_ada), (1, n_ada))
    mod_cols, c_act = adaln_mod(c_all.reshape(NDEV, D), w_ada[0], b_cols)
    (mod_all,) = _exchange([mod_cols], "gather_mod", True)
    mod = lax.dynamic_slice(mod_all, (0, me, 0), (NDEV, 1, n_ada)).reshape(6, 1, D)
    shift1, scale1, gate1, shift2, scale2, gate2 = [mod[j] for j in range(6)]

    big = ["w_in", "w_conv_out", "w_glu_a", "w_glu_b", "w_out", "w_ff1", "w_ff2"]
    order = ["w_in", "w_dw"] + big[1:]
    shards = {k: W[k][0].astype(BF16) for k in big}
    shards["w_dw"] = jnp.pad(w_dw[0], ((0, HALO - CONV_K), (0, 0)))
    gather_handle = dict(zip(order, gather2_start([shards[k] for k in order], "gather_weights_start", mod_all)))

    def forward(ks, name, after):
        gather_handle.update(zip(ks, gather2_forward([gather_handle[k] for k in ks], name, after)))

    def weight(k, after):
        w = gather2_wait(gather_handle[k], "gather_wait_" + k, after)
        if k in ("w_out", "w_ff2"):
            w = w.reshape(1, w.shape[0] * w.shape[1], w.shape[2])
        elif k == "w_dw":
            w = w.transpose(1, 0, 2).reshape(HALO, CW)
        return w

    scatter_handle = {}

    def scatter(k, g):
        if g.shape[0] == 1:
            g = g.reshape(NDEV, -1, g.shape[2])
        (scatter_handle[k],), token = exchange_start([g], "scatter_start_" + k, False)
        return token

    big_out = {}

    def finish_weight(k, after):
        parts = exchange_wait(scatter_handle[k], after, "scatter_wait_" + k, False)
        big_out[k] = adam_reduce(parts, W[k][0], Mo[k][0], Vo[k][0], "adam_" + k)
        return big_out[k][1]

    u = prenorm(xs, norm1_g, scale1, shift1, "prenorm1")

    br2 = b_re[0].transpose(0, 2, 1).reshape(G * H, P)
    bi2 = b_im[0].transpose(0, 2, 1).reshape(G * H, P)
    ldt = log_dt[0].reshape(G, 1)
    expand = jnp.repeat(jnp.eye(G, dtype=F32), H, axis=0)
    lbr, lbi, bbr, bbi = s5_params(a_re[0], a_im[0], ldt, br2, bi2, expand)
    tabs = s5_tables(lbr.reshape(1, G * P), lbi.reshape(1, G * P))
    bdr, bdi = _block_diag(bbr.reshape(G, H, P)), _block_diag(bbi.reshape(G, H, P))
    cdr = _block_diag(c_re[0].transpose(0, 2, 1))
    cdi = _block_diag(c_im[0].transpose(0, 2, 1))
    cd2 = jnp.concatenate([cdr, -cdi], axis=1).astype(BF16)
    bdr3, bdi3 = _rhs3(bdr), _rhs3(bdi)
    bdt2 = jnp.concatenate([bdr.transpose(0, 2, 1), bdi.transpose(0, 2, 1)], axis=1).astype(BF16)
    cdrt3, cdit3 = _rhs3(cdr.transpose(0, 2, 1)), _rhs3(-cdi.transpose(0, 2, 1))

    forward(["w_in"], "gather_forward_in", (u, tabs, bdr3, bdi3, cd2, bdt2, cdrt3, cdit3))
    wg = {"w_in": weight("w_in", u)}
    proj = mm_nn(u, wg["w_in"], "in_proj")
    forward(["w_dw", "w_conv_out", "w_glu_a", "w_glu_b", "w_out"], "gather_forward_mix", proj)
    w_dw_full = weight("w_dw", proj)
    vs, vc = conv_fwd(proj, w_dw_full, b_dw, ln_g, ln_b)
    wg["w_conv_out"] = weight("w_conv_out", vs)
    y_conv = mm_nn(vs, wg["w_conv_out"], "conv_out", out_dtype=BF16)
    s_re, s_im, y_pre, yg = s5_fwd(proj, 2 * CW, bdr3, bdi3, cd2, tabs, d_skip)
    forward(["w_ff1"], "gather_forward_ff1", yg)
    wg["w_glu_a"] = weight("w_glu_a", yg)
    wg["w_glu_b"] = weight("w_glu_b", yg)
    ga = mm_nn(yg, wg["w_glu_a"], "glu_a", out_dtype=BF16)
    gb = mm_nn(yg, wg["w_glu_b"], "glu_b", out_dtype=BF16)
    merged = merge_fwd(proj, 3 * CW, y_conv, ga, gb)
    forward(["w_ff2"], "gather_forward_ff2", merged)
    wg["w_out"] = weight("w_out", merged)
    m_out = mm_nn(merged, wg["w_out"], "out_proj")
    h1, z = residual_norm(xs, m_out, gate1, norm2_g, scale2, shift2)
    wg["w_ff1"] = weight("w_ff1", z)
    act = mm_nn(z, wg["w_ff1"], "ff1", out_dtype=BF16, epi=lambda r: jnp.square(jnp.maximum(r, 0.0)))
    wg["w_ff2"] = weight("w_ff2", act)
    ff = mm_nn(act, wg["w_ff2"], "ff2")

    dh2, dff, loss_part, d_final_g, d_gate2 = loss_bwd(h1, ff, gate2, final_g.reshape(1, D), tgt)
    df = mm_nt(dff, wg["w_ff2"], "ff2_dx", out_dtype=BF16,
               epi=lambda r, a: r * (2.0 * jnp.sqrt(a.astype(F32))), extras=(act,))
    t = scatter("w_ff2", mm_tn(act, dff, 1, "ff2_dw", out_dtype=BF16))
    t = scatter("w_ff1", mm_tn(z, df, NDEV, "ff1_dw", out_dtype=BF16, dep=t))
    dz = mm_nt(df, wg["w_ff1"], "ff1_dx", out_dtype=BF16, dep=t)
    dh1, d_scale2, d_shift2, d_norm2_g, dmo, d_gate1 = norm_bwd(dz, h1, dh2, norm2_g, scale2, "norm2_bwd", gate1, m_out)
    t = scatter("w_out", mm_tn(merged, dmo, 1, "out_dw", out_dtype=BF16))
    dmerged = mm_nt(dmo, wg["w_out"], "out_dx", out_dtype=BF16, dep=t)
    dyc, dga, dgb, dproj_g = merge_bwd(dmerged, proj, 3 * CW, y_conv, ga, gb)
    t = scatter("w_conv_out", mm_tn(vs, dyc, NDEV, "conv_out_dw", out_dtype=BF16))
    t = scatter("w_glu_a", mm_tn(yg, dga, NDEV, "glu_a_dw", out_dtype=BF16, dep=t))
    t = scatter("w_glu_b", mm_tn(yg, dgb, NDEV, "glu_b_dw", out_dtype=BF16, dep=t))
    dvs = mm_nt(dyc, wg["w_conv_out"], "conv_out_dx", out_dtype=BF16, dep=t)
    dyg_a = mm_nt(dga, wg["w_glu_a"], "glu_a_dx", out_dtype=BF16, dep=t)
    dyg_b = mm_nt(dgb, wg["w_glu_b"], "glu_b_dx", out_dtype=BF16, dep=t)
    dvc, d_ln_g, d_ln_b = conv_ln_bwd(dvs, vc, ln_g, ln_b)
    dproj_s, d_d_skip, dcdr, dcdi, dbdr, dbdi, dlr8, dli8 = s5_bwd(
        dyg_a, dyg_b, y_pre, proj, 2 * CW, s_re, s_im, bdt2, cdrt3, cdit3, tabs, d_skip)
    dproj, d_w_dw, d_b_dw = conv_bwd(dvc, proj, w_dw_full, dproj_s, dproj_g)
    d_c_re = _diag_blocks(dcdr, H, P)
    d_c_im = _diag_blocks(dcdi, H, P)
    d_bbr = _diag_blocks(dbdr, H, P)
    d_bbi = _diag_blocks(dbdi, H, P)
    dlr = jnp.sum(dlr8, axis=0).reshape(G, P)
    dli = jnp.sum(dli8, axis=0).reshape(G, P)
    early_parts = [d_b_dw, d_ln_g, d_ln_b, dlr, dli, d_bbr, d_bbi, d_c_re, d_c_im, d_d_skip, d_norm2_g, d_final_g,
                   d_w_dw]
    pack_early = _flat_pad(early_parts, PACK).reshape(NDEV, -1, 1024)
    (early_scatter,), done = exchange_start([pack_early], "scatter_small_start", False)
    for k in ("w_ff2", "w_ff1", "w_out"):
        done = finish_weight(k, done)
    early_sum = sum_devices(exchange_wait(early_scatter, done, "scatter_small_wait", False), "sum_small_early")
    (early_gather,), done = exchange_start([early_sum], "gather_small_start", True)
    t = scatter("w_in", mm_tn(u, dproj, NDEV, "in_dw", out_dtype=BF16, dep=done))

    tot_early = exchange_wait(early_gather, t, "gather_small_wait", True)
    (g_b_dw, g_ln_g, g_ln_b, t_lr, t_li, t_bbr, t_bbi, g_c_re_t, g_c_im_t, g_d_skip,
     g_norm2_g, g_final_g, g_w_dw_full) = _split(tot_early.reshape(-1), early_parts)
    g_a_re, g_a_im, g_ldt, g_br2, g_bi2 = s5_params_bwd(
        a_re[0], a_im[0], ldt, br2, bi2, expand, t_lr, t_li, t_bbr.reshape(G * H, P), t_bbi.reshape(G * H, P))
    g_brt, g_bit = g_br2.reshape(G, H, P), g_bi2.reshape(G, H, P)
    g2 = {
        "b_dw": g_b_dw, "ln_g": g_ln_g, "ln_b": g_ln_b,
        "a_re": g_a_re, "a_im": g_a_im, "log_dt": g_ldt.reshape(1, G),
        "b_re": g_brt.transpose(0, 2, 1).reshape(G * P, H), "b_im": g_bit.transpose(0, 2, 1).reshape(G * P, H),
        "c_re": g_c_re_t.reshape(G * H, P), "c_im": g_c_im_t.reshape(G * H, P), "d_skip": g_d_skip,
        "norm2_g": g_norm2_g, "final_g": g_final_g,
        "w_dw": lax.dynamic_slice(g_w_dw_full, (0, me * (CW // NDEV)), (CONV_K, CW // NDEV)),
    }
    as2d = lambda k, a: a.reshape(g2[k].shape)
    grads, delta, new_m, new_v = {}, {}, {}, {}

    def adam_small(ks, name):
        outs = adam_many([(as2d(k, W[k]), g2[k], as2d(k, Mo[k]), as2d(k, Vo[k])) for k in ks], name)
        for k, o in zip(ks, outs):
            grads[k] = g2[k].reshape(W[k].shape)
            delta[k], new_m[k], new_v[k] = [a.reshape(W[k].shape) for a in o]
        return outs[0][0]

    after = adam_small([k for k in names if k in g2], "adam_small_early")
    for k in ("w_conv_out", "w_glu_a", "w_glu_b"):
        after = finish_weight(k, after)
    du = mm_nt(dproj, wg["w_in"], "in_dx", out_dtype=BF16, dep=after)
    grad_x, d_scale1, d_shift1, d_norm1_g = norm_bwd(du, xs, dh1, norm1_g, scale1, "norm1_bwd")

    dmod = jnp.concatenate([d_shift1, d_scale1, d_gate1, d_shift2, d_scale2, d_gate2], axis=1)
    late_parts = [dmod, d_norm1_g]
    pack_late = _flat_pad(late_parts, PACK).reshape(NDEV, -1, 1024)
    parts_late, dmod_from = _exchange([pack_late, dmod.reshape(NDEV, 1, n_ada)], "scatter_small_late", False)
    (tot_late,) = _exchange([sum_devices(parts_late, "sum_small_late")], "gather_small_late", True)
    g2["b_ada"], g2["norm1_g"] = _split(tot_late.reshape(-1), late_parts)
    adam_small(["b_ada", "norm1_g"], "adam_small_late")
    dmod_cols = dmod_from.reshape(NDEV, n_ada)

    g, d, mm, vv = adam_w_ada(c_act, dmod_cols, w_ada[0], m_w_ada[0], v_w_ada[0])
    grads["w_ada"], delta["w_ada"], new_m["w_ada"], new_v["w_ada"] = g[None], d[None], mm[None], vv[None]

    finish_weight("w_in", d)
    for k in big:
        g, d, mm, vv = big_out[k]
        grads[k], delta[k], new_m[k], new_v[k] = g[None], d[None], mm[None], vv[None]

    loss = lax.psum(loss_part[0, 0], ("x", "y", "c"))
    return (loss, grad_x[None], *[grads[k] for k in names], *[delta[k] for k in names],
            *[new_m[k] for k in names], *[new_v[k] for k in names])
```

```python
import math

import jax
import jax.numpy as jnp
from jax import lax
from jax.experimental import pallas as pl
from jax.experimental.pallas import tpu as pltpu

F32 = jnp.float32
BF16 = jnp.bfloat16
NDEV = 8
EPS = 1e-6
ADAM_LR, ADAM_B1, ADAM_B2, ADAM_EPS, ADAM_WD, ADAM_STEP = 0.001, 0.9, 0.999, 1e-08, 0.01, 10
CONV_K = 31
HALO = 32
GB = 8
S5_ROWS = 2048
HI = lax.Precision.HIGHEST
MESH = pl.DeviceIdType.MESH
VMEM_LIMIT = 56 * 1024 * 1024
MAX_CONTRACT = 4096
PACK_ROWS = 64
PACK = PACK_ROWS * 1024
ANY = pl.BlockSpec(memory_space=pl.ANY)


def _params(sem=None):
    if sem is None:
        return pltpu.CompilerParams(vmem_limit_bytes=VMEM_LIMIT)
    return pltpu.CompilerParams(dimension_semantics=sem, vmem_limit_bytes=VMEM_LIMIT)


def _sigmoid(v):
    return 1.0 / (1.0 + jnp.exp(-v))


def _me():
    return 4 * lax.axis_index("x") + 2 * lax.axis_index("y") + lax.axis_index("c")


def _peer(k):
    x, y, c = lax.axis_index("x"), lax.axis_index("y"), lax.axis_index("c")
    px = 1 - x if (k >> 2) & 1 else x
    py = 1 - y if (k >> 1) & 1 else y
    pc = 1 - c if k & 1 else c
    return (px, py, pc), 4 * px + 2 * py + pc


def _exchange(arrays, name, gather):
    n = len(arrays)
    out_shape = []
    for a in arrays:
        shp = (NDEV,) + a.shape if gather else a.shape
        out_shape.append(jax.ShapeDtypeStruct(shp, a.dtype))

    def body(*refs):
        ins, outs = refs[:n], refs[n:2 * n]
        send, recv, lsem = refs[2 * n:]
        me = _me()
        local = []
        for a in range(n):
            src = ins[a] if gather else ins[a].at[me]
            cp = pltpu.make_async_copy(src, outs[a].at[me], lsem.at[a])
            cp.start()
            local.append(cp)
        sends = []
        for a in range(n):
            for k in range(1, NDEV):
                dev, pidx = _peer(k)
                src = ins[a] if gather else ins[a].at[pidx]
                cp = pltpu.make_async_remote_copy(
                    src_ref=src, dst_ref=outs[a].at[me], send_sem=send.at[a * (NDEV - 1) + k - 1], recv_sem=recv.at[a * (NDEV - 1) + k - 1],
                    device_id=dev, device_id_type=MESH)
                cp.start()
                sends.append(cp)
        for a in range(n):
            for k in range(1, NDEV):
                dev, pidx = _peer(k)
                src = ins[a] if gather else ins[a].at[pidx]
                pltpu.make_async_remote_copy(
                    src_ref=src, dst_ref=outs[a].at[pidx], send_sem=send.at[a * (NDEV - 1) + k - 1], recv_sem=recv.at[a * (NDEV - 1) + k - 1],
                    device_id=dev, device_id_type=MESH).wait_recv()
        for cp in sends:
            cp.wait_send()
        for cp in local:
            cp.wait()

    return pl.pallas_call(
        body, name=name, out_shape=tuple(out_shape),
        in_specs=[ANY] * n, out_specs=tuple([ANY] * n),
        scratch_shapes=[pltpu.SemaphoreType.DMA((n * (NDEV - 1),)), pltpu.SemaphoreType.DMA((n * (NDEV - 1),)),
                        pltpu.SemaphoreType.DMA((n,))],
    )(*arrays)


HBM = pl.BlockSpec(memory_space=pltpu.HBM)
SEM = pl.BlockSpec(memory_space=pltpu.SEMAPHORE)
EFFECT = pltpu.SideEffectType.DATAFLOW_SIDE_EFFECTING
NPEER = NDEV - 1


def _landing(block_of_me, shape, dtype):
    land = lax.empty((NDEV,) + tuple(shape), dtype)
    start = (_me(),) + (0,) * len(shape)
    return pltpu.with_memory_space_constraint(lax.dynamic_update_slice(land, block_of_me[None], start), pltpu.HBM)


def exchange_start(arrays, name, gather, after=None):
    n = len(arrays)
    me = _me()
    deps = () if after is None else (after,)
    lands = []
    for a in arrays:
        if gather:
            lands.append(_landing(a, a.shape, a.dtype))
        else:
            mine = lax.dynamic_slice(a, (me,) + (0,) * (a.ndim - 1), (1,) + a.shape[1:])[0]
            lands.append(_landing(mine, a.shape[1:], a.dtype))
    srcs = [pltpu.with_memory_space_constraint(a, pltpu.HBM) for a in arrays]

    def body(*refs):
        ins, lnd = refs[:n], refs[n:2 * n]
        outs = refs[2 * n + len(deps):]
        sends, recvs, token = outs[:n], outs[n:2 * n], outs[-1]
        my = _me()
        for a in range(n):
            for k in range(1, NDEV):
                dev, pidx = _peer(k)
                src = ins[a] if gather else ins[a].at[pidx]
                pltpu.make_async_remote_copy(
                    src_ref=src, dst_ref=lnd[a].at[my], send_sem=sends[a].at[k - 1], recv_sem=recvs[a].at[k - 1],
                    device_id=dev, device_id_type=MESH).start()
        token[...] = jnp.zeros_like(token)

    out_shape = ([pltpu.SemaphoreType.DMA((NPEER,))] * (2 * n)
                 + [pltpu.HBM(a.shape, a.dtype) for a in srcs] + [pltpu.HBM(l.shape, l.dtype) for l in lands]
                 + [jax.ShapeDtypeStruct((8, 128), F32)])
    res = pl.pallas_call(
        body, name=name, out_shape=tuple(out_shape),
        in_specs=[HBM] * (2 * n) + [ANY] * len(deps),
        out_specs=tuple([SEM] * (2 * n) + [HBM] * (2 * n) + [pl.BlockSpec(memory_space=pltpu.VMEM)]),
        input_output_aliases={i: 2 * n + i for i in range(2 * n)},
        compiler_params=pltpu.CompilerParams(has_side_effects=EFFECT),
    )(*srcs, *lands, *deps)
    handles = [(res[a], res[n + a], res[2 * n + a], res[3 * n + a]) for a in range(n)]
    return handles, res[-1]


def exchange_wait(handle, after, name, gather):
    send_sem, recv_sem, src, land = handle

    def body(src_ref, land_ref, s_ref, r_ref, after_ref, src_out, land_out):
        for k in range(1, NDEV):
            dev, pidx = _peer(k)
            s = src_ref if gather else src_ref.at[pidx]
            cp = pltpu.make_async_remote_copy(
                src_ref=s, dst_ref=land_ref.at[pidx], send_sem=s_ref.at[k - 1], recv_sem=r_ref.at[k - 1],
                device_id=dev, device_id_type=MESH)
            cp.wait_send()
            cp.wait_recv()

    return pl.pallas_call(
        body, name=name, out_shape=(pltpu.HBM(src.shape, src.dtype), pltpu.HBM(land.shape, land.dtype)),
        in_specs=(HBM, HBM, SEM, SEM, ANY), out_specs=(HBM, HBM), input_output_aliases={0: 0, 1: 1},
        compiler_params=pltpu.CompilerParams(has_side_effects=EFFECT),
    )(src, land, send_sem, recv_sem, after)[1]


ICI_PEERS = (2, 4, 6)
SIBLING = 1


def gather2_start(blocks, name, after):
    m = len(blocks)
    lands = [_landing(b, b.shape, b.dtype) for b in blocks]
    srcs = [pltpu.with_memory_space_constraint(b, pltpu.HBM) for b in blocks]
    n = len(ICI_PEERS)

    def body(*refs):
        src, lnd = refs[:m], refs[m:2 * m]
        outs = refs[2 * m + 1:]
        send, recv_sib, recv_ici = outs[:m], outs[m:2 * m], outs[2 * m:3 * m]
        my = _me()
        for a in range(m):
            dev, _ = _peer(SIBLING)
            pltpu.make_async_remote_copy(src_ref=src[a], dst_ref=lnd[a].at[my], send_sem=send[a].at[0],
                                         recv_sem=recv_sib[a].at[0], device_id=dev, device_id_type=MESH).start()
            for j, k in enumerate(ICI_PEERS):
                dev, _ = _peer(k)
                pltpu.make_async_remote_copy(src_ref=src[a], dst_ref=lnd[a].at[my], send_sem=send[a].at[1 + j],
                                             recv_sem=recv_ici[a].at[j], device_id=dev, device_id_type=MESH).start()

    out_shape = ([pltpu.SemaphoreType.DMA((1 + n,))] * m + [pltpu.SemaphoreType.DMA((1,))] * m
                 + [pltpu.SemaphoreType.DMA((n,))] * m
                 + [pltpu.HBM(s.shape, s.dtype) for s in srcs] + [pltpu.HBM(l.shape, l.dtype) for l in lands])
    res = pl.pallas_call(
        body, name=name, out_shape=tuple(out_shape),
        in_specs=[HBM] * (2 * m) + [ANY], out_specs=tuple([SEM] * (3 * m) + [HBM] * (2 * m)),
        input_output_aliases={i: 3 * m + i for i in range(2 * m)},
        compiler_params=pltpu.CompilerParams(has_side_effects=EFFECT),
    )(*srcs, *lands, after)
    return [tuple(res[g * m + a] for g in range(5)) for a in range(m)]


def gather2_forward(handles, name, after):
    m = len(handles)
    n = len(ICI_PEERS)
    srcs, lands = [h[3] for h in handles], [h[4] for h in handles]
    deps = tuple(after) if isinstance(after, (tuple, list)) else (after,)

    def body(*refs):
        src, lnd, recv_ici = refs[:m], refs[m:2 * m], refs[2 * m:3 * m]
        outs = refs[3 * m + len(deps):]
        fsend, frecv = outs[:m], outs[m:2 * m]
        sib, _ = _peer(SIBLING)
        for a in range(m):
            for j, k in enumerate(ICI_PEERS):
                dev, pidx = _peer(k)
                pltpu.make_async_remote_copy(
                    src_ref=src[a], dst_ref=lnd[a].at[pidx], send_sem=fsend[a].at[j], recv_sem=recv_ici[a].at[j],
                    device_id=dev, device_id_type=MESH).wait_recv()
                pltpu.make_async_remote_copy(
                    src_ref=lnd[a].at[pidx], dst_ref=lnd[a].at[pidx], send_sem=fsend[a].at[j], recv_sem=frecv[a].at[j],
                    device_id=sib, device_id_type=MESH).start()

    out_shape = ([pltpu.SemaphoreType.DMA((n,))] * (2 * m)
                 + [pltpu.HBM(s.shape, s.dtype) for s in srcs] + [pltpu.HBM(l.shape, l.dtype) for l in lands])
    res = pl.pallas_call(
        body, name=name, out_shape=tuple(out_shape),
        in_specs=[HBM] * (2 * m) + [SEM] * m + [ANY] * len(deps),
        out_specs=tuple([SEM] * (2 * m) + [HBM] * (2 * m)),
        input_output_aliases={i: 2 * m + i for i in range(2 * m)},
        compiler_params=pltpu.CompilerParams(has_side_effects=EFFECT),
    )(*srcs, *lands, *[h[2] for h in handles], *deps)
    return [(handles[a][0], handles[a][1], res[a], res[m + a], res[2 * m + a], res[3 * m + a]) for a in range(m)]


def gather2_wait(handle, name, after):
    send, recv_sib, fsend, frecv, src, land = handle

    def body(src_ref, land_ref, send_ref, recv_sib_ref, fsend_ref, frecv_ref, after_ref, src_out, land_out):
        sib, sib_idx = _peer(SIBLING)
        own = pltpu.make_async_remote_copy(src_ref=src_ref, dst_ref=land_ref.at[sib_idx], send_sem=send_ref.at[0],
                                           recv_sem=recv_sib_ref.at[0], device_id=sib, device_id_type=MESH)
        own.wait_send()
        own.wait_recv()
        for j, k in enumerate(ICI_PEERS):
            dev, pidx = _peer(k)
            pltpu.make_async_remote_copy(src_ref=src_ref, dst_ref=land_ref.at[pidx], send_sem=send_ref.at[1 + j],
                                         recv_sem=frecv_ref.at[j], device_id=dev, device_id_type=MESH).wait_send()
            _, fidx = _peer(k ^ SIBLING)
            fwd = pltpu.make_async_remote_copy(src_ref=land_ref.at[pidx], dst_ref=land_ref.at[fidx],
                                               send_sem=fsend_ref.at[j], recv_sem=frecv_ref.at[j],
                                               device_id=sib, device_id_type=MESH)
            fwd.wait_send()
            fwd.wait_recv()

    return pl.pallas_call(
        body, name=name, out_shape=(pltpu.HBM(src.shape, src.dtype), pltpu.HBM(land.shape, land.dtype)),
        in_specs=(HBM, HBM, SEM, SEM, SEM, SEM, ANY), out_specs=(HBM, HBM), input_output_aliases={0: 0, 1: 1},
        compiler_params=pltpu.CompilerParams(has_side_effects=EFFECT),
    )(src, land, send, recv_sib, fsend, frecv, after)[1]


def _acc_steps(p, acc, k, nk, finish):
    if nk == 1:
        finish(p)
        return

    @pl.when(k == 0)
    def _():
        acc[...] = p

    @pl.when(k > 0)
    def _():
        acc[...] += p

    @pl.when(k == nk - 1)
    def _():
        finish(acc[...])


def _shards_per_step(J, n, tn, width):
    s = 1
    while n == tn and J % (2 * s) == 0 and 2 * s * tn <= width:
        s *= 2
    return s


def mm_nn(a, w3, name, out_dtype=F32, epi=None, extras=()):
    M, K = a.shape
    J, _, n = w3.shape
    tm, tn, tk = min(1024, M), min(1024, n), min(2048, K)
    q, nk, ne = n // tn, K // tk, len(extras)
    s = _shards_per_step(J, n, tn, 1024)

    def body(*refs):
        a_ref, w_ref = refs[:2]
        ex, o_ref, acc = refs[2:2 + ne], refs[2 + ne], refs[-1]
        av = a_ref[...]
        p = jnp.dot(av, w_ref[0], preferred_element_type=F32)
        if s > 1:
            p = jnp.concatenate([p] + [jnp.dot(av, w_ref[j], preferred_element_type=F32) for j in range(1, s)], axis=1)

        def finish(r):
            if epi is not None:
                r = epi(r, *[e[...] for e in ex])
            o_ref[...] = r.astype(out_dtype)

        _acc_steps(p, acc, pl.program_id(2), nk, finish)

    return pl.pallas_call(
        body, name=name, grid=(M // tm, (J // s) * q, nk),
        in_specs=[pl.BlockSpec((tm, tk), lambda i, j, k: (i, k)),
                  pl.BlockSpec((s, tk, tn), lambda i, j, k: (j // q, k, j % q))]
        + [pl.BlockSpec((tm, s * tn), lambda i, j, k: (i, j))] * ne,
        out_specs=pl.BlockSpec((tm, s * tn), lambda i, j, k: (i, j)),
        out_shape=jax.ShapeDtypeStruct((M, J * n), out_dtype),
        scratch_shapes=[pltpu.VMEM((tm, s * tn), F32)],
        compiler_params=_params(("parallel", "parallel", "arbitrary")),
    )(a, w3, *extras)


def mm_nt(dy, w3, name, out_dtype=F32, epi=None, extras=(), dep=None):
    M, _ = dy.shape
    J, K, n = w3.shape
    tm, tn, tkk = min(1024, M), min(MAX_CONTRACT, n), min(1024, K)
    q, ne = n // tn, len(extras)
    s = 1
    while q == 1 and J % (2 * s) == 0 and 2 * s * tn <= MAX_CONTRACT:
        s *= 2
    nk = (J // s) * q
    deps = () if dep is None else (dep,)

    def body(*refs):
        d_ref, w_ref = refs[:2]
        ex, o_ref, acc = refs[2:2 + ne], refs[-2], refs[-1]
        nt = (((1,), (1,)), ((), ()))
        p = lax.dot_general(d_ref[:, 0:tn], w_ref[0], nt, preferred_element_type=F32)
        for j in range(1, s):
            p = p + lax.dot_general(d_ref[:, j * tn:(j + 1) * tn], w_ref[j], nt, preferred_element_type=F32)

        def finish(r):
            if epi is not None:
                r = epi(r, *[e[...] for e in ex])
            o_ref[...] = r.astype(out_dtype)

        _acc_steps(p, acc, pl.program_id(2), nk, finish)

    return pl.pallas_call(
        body, name=name, grid=(M // tm, K // tkk, nk),
        in_specs=[pl.BlockSpec((tm, s * tn), lambda i, kk, c: (i, c)),
                  pl.BlockSpec((s, tkk, tn), lambda i, kk, c: (c // q, kk, c % q))]
        + [pl.BlockSpec((tm, tkk), lambda i, kk, c: (i, kk))] * ne + [ANY] * len(deps),
        out_specs=pl.BlockSpec((tm, tkk), lambda i, kk, c: (i, kk)),
        out_shape=jax.ShapeDtypeStruct((M, K), out_dtype),
        scratch_shapes=[pltpu.VMEM((tm, tkk), F32)],
        compiler_params=_params(("parallel", "parallel", "arbitrary")),
    )(dy, w3, *extras, *deps)


def mm_tn(a, dy, J, name, out_dtype=F32, dep=None):
    M, K = a.shape
    n = dy.shape[1] // J
    tm, tn, tkk = min(MAX_CONTRACT, M), min(1024, n), min(1024, K)
    q, nk = n // tn, M // tm
    s = _shards_per_step(J, n, tn, 1024)
    deps = () if dep is None else (dep,)

    def body(a_ref, d_ref, *rest):
        o_ref, acc = rest[-2:]
        p = lax.dot_general(a_ref[...], d_ref[...], (((0,), (0,)), ((), ())), preferred_element_type=F32)

        def finish(r):
            for j in range(s):
                o_ref[j] = r[:, j * tn:(j + 1) * tn].astype(out_dtype)

        _acc_steps(p, acc, pl.program_id(2), nk, finish)

    return pl.pallas_call(
        body, name=name, grid=(K // tkk, (J // s) * q, nk),
        in_specs=[pl.BlockSpec((tm, tkk), lambda kk, c, m: (m, kk)),
                  pl.BlockSpec((tm, s * tn), lambda kk, c, m: (m, c))] + [ANY] * len(deps),
        out_specs=pl.BlockSpec((s, tkk, tn), lambda kk, c, m: (c // q, kk, c % q)),
        out_shape=jax.ShapeDtypeStruct((J, K, n), out_dtype),
        scratch_shapes=[pltpu.VMEM((tkk, s * tn), F32)],
        compiler_params=_params(("parallel", "parallel", "arbitrary")),
    )(a, dy, *deps)


def _tm(L):
    return min(256, L)


def _row(w, cb=0, tm=None):
    return pl.BlockSpec((tm, w), lambda i: (i, cb))


def _vec(w, cb=0):
    return pl.BlockSpec((1, w), lambda i: (0, cb))


def _accum(ref, val, i):
    @pl.when(i == 0)
    def _():
        ref[...] = val

    @pl.when(i > 0)
    def _():
        ref[...] += val


def _colsum(v):
    return jnp.sum(v, axis=0, keepdims=True)


def _rms(v):
    return lax.rsqrt(jnp.mean(v * v, axis=-1, keepdims=True) + EPS)


def adaln_mod(c_all, w_ada, b_cols):
    B, D = c_all.shape
    n = w_ada.shape[1]
    tn = 512

    def body(c_ref, w_ref, b_ref, o_ref, ca_ref):
        cv = c_ref[...]
        ca = cv * _sigmoid(cv)
        ca_ref[...] = ca
        o_ref[...] = jnp.dot(ca.astype(BF16), w_ref[...].astype(BF16), preferred_element_type=F32) + b_ref[...]

    return pl.pallas_call(
        body, name="adaln_mod", grid=(n // tn,),
        in_specs=[pl.BlockSpec((B, D), lambda j: (0, 0)), pl.BlockSpec((D, tn), lambda j: (0, j)),
                  pl.BlockSpec((1, tn), lambda j: (0, j))],
        out_specs=(pl.BlockSpec((B, tn), lambda j: (0, j)), pl.BlockSpec((B, D), lambda j: (0, 0))),
        out_shape=(jax.ShapeDtypeStruct((B, n), F32), jax.ShapeDtypeStruct((B, D), F32)),
        compiler_params=_params(("arbitrary",)),
    )(c_all, w_ada, b_cols)


def prenorm(x, g, scale, shift, name):
    L, D = x.shape
    tm = _tm(L)

    def body(x_ref, g_ref, sc_ref, sh_ref, u_ref):
        xv = x_ref[...]
        u_ref[...] = (xv * _rms(xv) * g_ref[...] * (1.0 + sc_ref[...]) + sh_ref[...]).astype(BF16)

    return pl.pallas_call(
        body, name=name, grid=(L // tm,),
        in_specs=[_row(D, tm=tm), _vec(D), _vec(D), _vec(D)],
        out_specs=_row(D, tm=tm), out_shape=jax.ShapeDtypeStruct((L, D), BF16),
        compiler_params=_params(("parallel",)),
    )(x, g, scale, shift)


def _shifted_copies(buf, shifted, tm):
    n = HALO + tm - 8
    for r in range(1, 8):
        shifted[r - 1] = buf[pl.ds(r, n), :]


def _window(buf, shifted, off, tm, r0=0, cols=slice(None)):
    r, base = off % 8, off - off % 8
    if r == 0:
        return buf[pl.ds(base + r0, tm), cols]
    return shifted[r - 1, pl.ds(base + r0, tm), cols]


CONV_CHUNK = (64, 256)


def conv_fwd(proj, w_dw, b_dw, ln_g, ln_b):
    L = proj.shape[0]
    C = w_dw.shape[1]
    tm = _tm(L)
    hb = tm // HALO

    def body(a_ref, g_ref, ah_ref, gh_ref, w_ref, b_ref, lg_ref, lb_ref, vs_ref, vc_ref, buf, shifted):
        i = pl.program_id(0)
        halo = ah_ref[...] * _sigmoid(gh_ref[...])
        buf[0:HALO, :] = halo * jnp.where(i > 0, 1.0, 0.0)
        buf[HALO:HALO + tm, :] = a_ref[...] * _sigmoid(g_ref[...])
        _shifted_copies(buf, shifted, tm)
        acc = jnp.zeros((tm, C), F32) + b_ref[...]
        for k in range(CONV_K):
            acc = acc + w_ref[k:k + 1, :] * _window(buf, shifted, HALO - (CONV_K - 1) + k, tm)
        vc_ref[...] = acc
        mu = jnp.mean(acc, axis=-1, keepdims=True)
        d = acc - mu
        var = jnp.mean(d * d, axis=-1, keepdims=True)
        ln = d * lax.rsqrt(var + EPS) * lg_ref[...] + lb_ref[...]
        vs_ref[...] = (ln * _sigmoid(ln)).astype(BF16)

    prev = lambda cb: pl.BlockSpec((HALO, C), lambda i: (jnp.maximum(i * hb - 1, 0), cb))
    return pl.pallas_call(
        body, name="conv_fwd", grid=(L // tm,),
        in_specs=[_row(C, 0, tm), _row(C, 1, tm), prev(0), prev(1),
                  pl.BlockSpec((HALO, C), lambda i: (0, 0)), _vec(C), _vec(C), _vec(C)],
        out_specs=(_row(C, tm=tm), _row(C, tm=tm)),
        out_shape=(jax.ShapeDtypeStruct((L, C), BF16), jax.ShapeDtypeStruct((L, C), F32)),
        scratch_shapes=[pltpu.VMEM((HALO + tm, C), F32), pltpu.VMEM((7, HALO + tm - 8, C), F32)],
        compiler_params=_params(("parallel",)),
    )(proj, proj, proj, proj, w_dw, b_dw, ln_g, ln_b)


def _gelu(v):
    return 0.5 * v * (1.0 + jnp.tanh(math.sqrt(2.0 / math.pi) * (v + 0.044715 * v * v * v)))


def _gelu_grad(v):
    k = math.sqrt(2.0 / math.pi)
    t = jnp.tanh(k * (v + 0.044715 * v * v * v))
    return 0.5 * (1.0 + t) + 0.5 * v * (1.0 - t * t) * k * (1.0 + 3.0 * 0.044715 * v * v)


def s5_param_fn(ar, ai, ldt, br, bi, expand):
    dt = jnp.exp(ldt)
    er = jnp.exp(ar * dt)
    th = ai * dt
    lbr, lbi = er * jnp.cos(th), er * jnp.sin(th)
    nr, ni = lbr - 1.0, lbi
    den = ar * ar + ai * ai
    qr, qi = (nr * ar + ni * ai) / den, (ni * ar - nr * ai) / den
    qre = jnp.dot(expand, qr, precision=HI, preferred_element_type=F32)
    qie = jnp.dot(expand, qi, precision=HI, preferred_element_type=F32)
    return lbr, lbi, qre * br - qie * bi, qre * bi + qie * br


def s5_params(ar, ai, ldt, br2, bi2, expand):
    def body(ar_ref, ai_ref, ld_ref, br_ref, bi_ref, e_ref, o1, o2, o3, o4):
        r = s5_param_fn(ar_ref[...], ai_ref[...], ld_ref[...], br_ref[...], bi_ref[...], e_ref[...])
        o1[...], o2[...], o3[...], o4[...] = r

    s2, s3 = jax.ShapeDtypeStruct(ar.shape, F32), jax.ShapeDtypeStruct(br2.shape, F32)
    return pl.pallas_call(body, name="s5_params", out_shape=(s2, s2, s3, s3), compiler_params=_params())(
        ar, ai, ldt, br2, bi2, expand)


def s5_params_bwd(ar, ai, ldt, br2, bi2, expand, dlr, dli, dbr, dbi):
    def body(ar_ref, ai_ref, ld_ref, br_ref, bi_ref, e_ref, c1, c2, c3, c4, o1, o2, o3, o4, o5):
        e = e_ref[...]
        fn = lambda a, b, c, d, f: s5_param_fn(a, b, c, d, f, e)
        _, vjp = jax.vjp(fn, ar_ref[...], ai_ref[...], ld_ref[...], br_ref[...], bi_ref[...])
        r = vjp((c1[...], c2[...], c3[...], c4[...]))
        o1[...], o2[...], o3[...], o4[...], o5[...] = r

    shapes = tuple(jax.ShapeDtypeStruct(v.shape, F32) for v in (ar, ai, ldt, br2, bi2))
    return pl.pallas_call(body, name="s5_params_bwd", out_shape=shapes, compiler_params=_params())(
        ar, ai, ldt, br2, bi2, expand, dlr, dli, dbr, dbi)


def s5_tables(lr, li):
    C = lr.shape[1]

    def body(lr_ref, li_ref, o_ref):
        row = lax.broadcasted_iota(jnp.int32, (8, C), 0)
        for rev in (0, 1):
            pr = jnp.broadcast_to(lr_ref[...], (8, C))
            pi = jnp.broadcast_to(-li_ref[...] if rev else li_ref[...], (8, C))
            br, bi = pr, pi
            pows = [(pr, pi)]
            for _ in range(7):
                pr, pi = pr * br - pi * bi, pr * bi + pi * br
                pows.append((pr, pi))
            base = 8 * rev
            for s, d in enumerate((1, 2, 4)):
                keep = (row + d <= 7) if rev else (row >= d)
                o_ref[base + 2 * s] = jnp.where(keep, pows[d - 1][0], 0.0)
                o_ref[base + 2 * s + 1] = jnp.where(keep, pows[d - 1][1], 0.0)
            cr, ci = jnp.zeros((8, C), F32), jnp.zeros((8, C), F32)
            for j in range(8):
                e = (8 - j) if rev else (j + 1)
                cr = jnp.where(row == j, pows[e - 1][0], cr)
                ci = jnp.where(row == j, pows[e - 1][1], ci)
            o_ref[base + 6] = cr
            o_ref[base + 7] = ci

    return pl.pallas_call(body, name="s5_tables", out_shape=jax.ShapeDtypeStruct((16, 8, C), F32),
                          compiler_params=_params())(lr, li)


def _tile_steps(xr, xi, tabs, rev):
    for s, d in enumerate((1, 2, 4)):
        tr, ti = tabs[2 * s], tabs[2 * s + 1]
        sh = (8 - d) if rev else d
        sr, si = pltpu.roll(xr, sh, 0), pltpu.roll(xi, sh, 0)
        xr, xi = xr + tr * sr - ti * si, xi + tr * si + ti * sr
    return xr, xi


def _tile_carry(xr, xi, tabs, cr, ci):
    tr, ti = tabs[6], tabs[7]
    return xr + tr * cr - ti * ci, xi + tr * ci + ti * cr


def _hi_lo(a):
    hi = a.astype(BF16)
    return hi, (a - hi.astype(F32)).astype(BF16)


def _lhs3(a):
    hi, lo = _hi_lo(a)
    return jnp.concatenate([hi, lo, hi], axis=1)


def _rhs3(m):
    hi, lo = _hi_lo(m)
    return jnp.concatenate([hi, hi, lo], axis=-2)


def s5_fwd(proj, col0, bdr3, bdi3, cd2, tabs, d_skip):
    L = proj.shape[0]
    nb, cw3, sw = bdr3.shape
    cw = cw3 // 3
    tl = min(S5_ROWS, L)
    cb0 = col0 // cw

    def body(u_ref, bdr_ref, bdi_ref, cd_ref, t_ref, dk_ref, sr_ref, si_ref, yp_ref, yg_ref, car):
        l = pl.program_id(1)

        @pl.when(l == 0)
        def _():
            car[...] = jnp.zeros_like(car)

        u = u_ref[...]
        u3 = _lhs3(u)
        sr_ref[...] = jnp.dot(u3, bdr_ref[...], preferred_element_type=F32)
        si_ref[...] = jnp.dot(u3, bdi_ref[...], preferred_element_type=F32)

        def pair(i, c):
            tabs = [t_ref[j] for j in range(8)]
            r0 = pl.multiple_of(i * 16, 16)
            lo, hi = pl.ds(r0, 8), pl.ds(r0 + 8, 8)
            a = _tile_steps(sr_ref[lo, :], si_ref[lo, :], tabs, False)
            b = _tile_steps(sr_ref[hi, :], si_ref[hi, :], tabs, False)
            ar, ai = _tile_carry(a[0], a[1], tabs, c[0], c[1])
            br, bi = _tile_carry(b[0], b[1], tabs, ar[7:8, :], ai[7:8, :])
            sr_ref[lo, :], si_ref[lo, :] = ar, ai
            sr_ref[hi, :], si_ref[hi, :] = br, bi
            return br[7:8, :], bi[7:8, :]

        c = lax.fori_loop(0, tl // 16, pair, (car[0:1, :], car[1:2, :]))
        car[0:1, :] = c[0]
        car[1:2, :] = c[1]
        s2 = jnp.concatenate([sr_ref[...].astype(BF16), si_ref[...].astype(BF16)], axis=1)
        y = jnp.dot(s2, cd_ref[...], preferred_element_type=F32) + dk_ref[...] * u
        yp_ref[...] = y
        yg_ref[...] = _gelu(y).astype(BF16)

    blk = lambda r, c: pl.BlockSpec((None, r, c), lambda b, l: (b, 0, 0))
    return pl.pallas_call(
        body, name="s5_fwd", grid=(nb, L // tl),
        in_specs=[pl.BlockSpec((tl, cw), lambda b, l: (l, cb0 + b)), blk(cw3, sw), blk(cw3, sw), blk(2 * sw, cw),
                  pl.BlockSpec((8, 8, sw), lambda b, l: (0, 0, b)), pl.BlockSpec((1, cw), lambda b, l: (0, b))],
        out_specs=(pl.BlockSpec((tl, sw), lambda b, l: (l, b)), pl.BlockSpec((tl, sw), lambda b, l: (l, b)),
                   pl.BlockSpec((tl, cw), lambda b, l: (l, b)), pl.BlockSpec((tl, cw), lambda b, l: (l, b))),
        out_shape=(jax.ShapeDtypeStruct((L, nb * sw), F32), jax.ShapeDtypeStruct((L, nb * sw), F32),
                   jax.ShapeDtypeStruct((L, nb * cw), F32), jax.ShapeDtypeStruct((L, nb * cw), BF16)),
        scratch_shapes=[pltpu.VMEM((8, sw), F32)],
        compiler_params=_params(("parallel", "arbitrary")),
    )(proj, bdr3, bdi3, cd2, tabs, d_skip)


def s5_bwd(dyg_a, dyg_b, yp, proj, col0, s_re, s_im, bdt2, cdrt3, cdit3, tabs, d_skip):
    L = proj.shape[0]
    nb, sw2, cw = bdt2.shape
    sw = sw2 // 2
    tl = min(S5_ROWS, L)
    nl = L // tl
    cb0 = col0 // cw
    tb = tl // 8

    def body(da_ref, db_ref, yp_ref, u_ref, sr_ref, si_ref, hr_ref, hi_ref, bdt_ref, cdrt_ref, cdit_ref,
             t_ref, dk_ref, du_ref, ddk_ref, dcr_ref, dci_ref, dbr_ref, dbi_ref, dlr_ref, dli_ref,
             gr, gi, pr, pi, car):
        l = pl.program_id(1)
        first = l == nl - 1

        @pl.when(l == 0)
        def _():
            car[...] = jnp.zeros_like(car)

        u = u_ref[...]
        dy = (da_ref[...].astype(F32) + db_ref[...].astype(F32)) * _gelu_grad(yp_ref[...])
        dy3 = _lhs3(dy)
        gr[...] = jnp.dot(dy3, cdrt_ref[...], preferred_element_type=F32)
        gi[...] = jnp.dot(dy3, cdit_ref[...], preferred_element_type=F32)
        inner = jnp.where(first, 0.0, 1.0)
        pr[0:8, :] = hr_ref[...] * inner
        pi[0:8, :] = hi_ref[...] * inner
        pr[8:8 + tl, :] = sr_ref[...]
        pi[8:8 + tl, :] = si_ref[...]
        row = lax.broadcasted_iota(jnp.int32, (8, sw), 0)

        def pair(j, c):
            tabs = [t_ref[8 + k] for k in range(8)]
            r0 = pl.multiple_of((tb // 2 - 1 - j) * 16, 16)
            lo, hi = pl.ds(r0, 8), pl.ds(r0 + 8, 8)
            b = _tile_steps(gr[hi, :], gi[hi, :], tabs, True)
            a = _tile_steps(gr[lo, :], gi[lo, :], tabs, True)
            br, bi = _tile_carry(b[0], b[1], tabs, c[0], c[1])
            ar, ai = _tile_carry(a[0], a[1], tabs, br[0:1, :], bi[0:1, :])
            gr[lo, :], gi[lo, :] = ar, ai
            gr[hi, :], gi[hi, :] = br, bi
            p0r, p1r, p2r = [pltpu.roll(pr[pl.ds(r0 + 8 * n, 8), :], 1, 0) for n in range(3)]
            p0i, p1i, p2i = [pltpu.roll(pi[pl.ds(r0 + 8 * n, 8), :], 1, 0) for n in range(3)]
            qar, qai = jnp.where(row == 0, p0r, p1r), jnp.where(row == 0, p0i, p1i)
            qbr, qbi = jnp.where(row == 0, p1r, p2r), jnp.where(row == 0, p1i, p2i)
            return (ar[0:1, :], ai[0:1, :], c[2] + (ar * qar + ai * qai) + (br * qbr + bi * qbi),
                    c[3] + (ai * qar - ar * qai) + (bi * qbr - br * qbi))

        z = jnp.zeros((8, sw), F32)
        c = lax.fori_loop(0, tb // 2, pair, (car[0:1, :], car[1:2, :], z, z))
        car[0:1, :] = c[0]
        car[1:2, :] = c[1]
        g_re, g_im = gr[...].astype(BF16), gi[...].astype(BF16)
        g2 = jnp.concatenate([g_re, g_im], axis=1)
        du_ref[...] = (dy * dk_ref[...] + jnp.dot(g2, bdt_ref[...], preferred_element_type=F32)).astype(BF16)
        tn = (((0,), (0,)), ((), ()))
        dyb, ub = dy.astype(BF16), u.astype(BF16)
        _accum(ddk_ref, _colsum(dy * u), l)
        _accum(dcr_ref, lax.dot_general(dyb, sr_ref[...].astype(BF16), tn, preferred_element_type=F32), l)
        _accum(dci_ref, -lax.dot_general(dyb, si_ref[...].astype(BF16), tn, preferred_element_type=F32), l)
        _accum(dbr_ref, lax.dot_general(ub, g_re, tn, preferred_element_type=F32), l)
        _accum(dbi_ref, lax.dot_general(ub, g_im, tn, preferred_element_type=F32), l)
        _accum(dlr_ref, c[2], l)
        _accum(dli_ref, c[3], l)

    rl = lambda l: nl - 1 - l
    cblk = lambda w, off=0: pl.BlockSpec((tl, w), lambda b, l: (rl(l), off + b))
    halo = pl.BlockSpec((8, sw), lambda b, l: (jnp.maximum(rl(l) * tb - 1, 0), b))
    mat = lambda r, c: pl.BlockSpec((None, r, c), lambda b, l: (b, 0, 0))
    return pl.pallas_call(
        body, name="s5_bwd", grid=(nb, nl),
        in_specs=[cblk(cw), cblk(cw), cblk(cw), cblk(cw, cb0), cblk(sw), cblk(sw), halo, halo,
                  mat(2 * sw, cw), mat(3 * cw, sw), mat(3 * cw, sw),
                  pl.BlockSpec((16, 8, sw), lambda b, l: (0, 0, b)), pl.BlockSpec((1, cw), lambda b, l: (0, b))],
        out_specs=(cblk(cw), pl.BlockSpec((1, cw), lambda b, l: (0, b)), mat(cw, sw), mat(cw, sw), mat(cw, sw), mat(cw, sw),
                   pl.BlockSpec((8, sw), lambda b, l: (0, b)), pl.BlockSpec((8, sw), lambda b, l: (0, b))),
        out_shape=(jax.ShapeDtypeStruct((L, nb * cw), BF16), jax.ShapeDtypeStruct((1, nb * cw), F32),
                   jax.ShapeDtypeStruct((nb, cw, sw), F32), jax.ShapeDtypeStruct((nb, cw, sw), F32),
                   jax.ShapeDtypeStruct((nb, cw, sw), F32), jax.ShapeDtypeStruct((nb, cw, sw), F32),
                   jax.ShapeDtypeStruct((8, nb * sw), F32), jax.ShapeDtypeStruct((8, nb * sw), F32)),
        scratch_shapes=[pltpu.VMEM((tl, sw), F32), pltpu.VMEM((tl, sw), F32),
                        pltpu.VMEM((tl + 8, sw), F32), pltpu.VMEM((tl + 8, sw), F32), pltpu.VMEM((8, sw), F32)],
        compiler_params=_params(("parallel", "arbitrary")),
    )(dyg_a, dyg_b, yp, proj, s_re, s_im, s_re, s_im, bdt2, cdrt3, cdit3, tabs, d_skip)


def merge_fwd(proj, col_gc, y_conv, ga, gb):
    L, D = y_conv.shape
    tm = _tm(L)
    h = D // 2
    c0 = col_gc // h

    def body(p0, p1, p2, p3, yc_ref, ga_ref, gb_ref, o_ref):
        gc, gs = (p0, p1), (p2, p3)
        for s in range(2):
            cols = slice(s * h, (s + 1) * h)
            y_ssm = ga_ref[:, cols].astype(F32) * _sigmoid(gb_ref[:, cols].astype(F32))
            o_ref[:, cols] = (_sigmoid(gc[s][...]) * yc_ref[:, cols].astype(F32)
                              + _sigmoid(gs[s][...]) * y_ssm).astype(BF16)

    return pl.pallas_call(
        body, name="merge_fwd", grid=(L // tm,),
        in_specs=[_row(h, c0 + s, tm) for s in range(4)] + [_row(D, tm=tm)] * 3,
        out_specs=_row(D, tm=tm), out_shape=jax.ShapeDtypeStruct((L, D), BF16),
        compiler_params=_params(("parallel",)),
    )(proj, proj, proj, proj, y_conv, ga, gb)


def residual_norm(x, m_out, gate, g, scale, shift):
    L, D = x.shape
    tm = _tm(L)

    def body(x_ref, m_ref, gt_ref, g_ref, sc_ref, sh_ref, h_ref, z_ref):
        h = x_ref[...] + gt_ref[...] * m_ref[...]
        h_ref[...] = h
        z_ref[...] = (h * _rms(h) * g_ref[...] * (1.0 + sc_ref[...]) + sh_ref[...]).astype(BF16)

    return pl.pallas_call(
        body, name="residual_norm", grid=(L // tm,),
        in_specs=[_row(D, tm=tm), _row(D, tm=tm), _vec(D), _vec(D), _vec(D), _vec(D)],
        out_specs=(_row(D, tm=tm), _row(D, tm=tm)),
        out_shape=(jax.ShapeDtypeStruct((L, D), F32), jax.ShapeDtypeStruct((L, D), BF16)),
        compiler_params=_params(("parallel",)),
    )(x, m_out, gate, g, scale, shift)


def loss_bwd(h1, ff, gate2, final_g, target):
    L, D = h1.shape
    tm = _tm(L)

    def body(h_ref, f_ref, gt_ref, g_ref, t_ref, dh_ref, dff_ref, loss_ref, dg_ref, dgt_ref):
        i = pl.program_id(0)
        ffv = f_ref[...]
        h2 = h_ref[...] + gt_ref[...] * ffv
        r = _rms(h2)
        n = h2 * r
        err = n * g_ref[...] - t_ref[...]
        per_tok = jnp.mean(err * err, axis=-1, keepdims=True)
        _accum(loss_ref, 0.5 * jnp.sum(per_tok, axis=0, keepdims=True), i)
        dy = err * (1.0 / D)
        _accum(dg_ref, _colsum(dy * n), i)
        dn = dy * g_ref[...]
        dh2 = r * (dn - n * jnp.mean(dn * n, axis=-1, keepdims=True))
        dh_ref[...] = dh2
        dff_ref[...] = (gt_ref[...] * dh2).astype(BF16)
        _accum(dgt_ref, _colsum(dh2 * ffv), i)

    return pl.pallas_call(
        body, name="loss_bwd", grid=(L // tm,),
        in_specs=[_row(D, tm=tm), _row(D, tm=tm), _vec(D), _vec(D), _row(D, tm=tm)],
        out_specs=(_row(D, tm=tm), _row(D, tm=tm), pl.BlockSpec((1, 1), lambda i: (0, 0)), _vec(D), _vec(D)),
        out_shape=(jax.ShapeDtypeStruct((L, D), F32), jax.ShapeDtypeStruct((L, D), BF16),
                   jax.ShapeDtypeStruct((1, 1), F32), jax.ShapeDtypeStruct((1, D), F32), jax.ShapeDtypeStruct((1, D), F32)),
        compiler_params=_params(("arbitrary",)),
    )(h1, ff, gate2, final_g, target)


def norm_bwd(dz, h, dh_in, g, scale, name, gate=None, m_out=None):
    L, D = h.shape
    tm = _tm(L)
    tail = gate is not None

    def body(*refs):
        dz_ref, h_ref, di_ref, g_ref, sc_ref = refs[:5]
        rest = refs[5:]
        if tail:
            gt_ref, m_ref = rest[:2]
            rest = rest[2:]
        dh_ref, dsc_ref, dsh_ref, dg_ref = rest[:4]
        i = pl.program_id(0)
        hv, dzv = h_ref[...], dz_ref[...].astype(F32)
        r = _rms(hv)
        n = hv * r
        _accum(dsc_ref, _colsum(dzv * n * g_ref[...]), i)
        _accum(dsh_ref, _colsum(dzv), i)
        dzn = dzv * (1.0 + sc_ref[...])
        _accum(dg_ref, _colsum(dzn * n), i)
        dn = dzn * g_ref[...]
        dh = di_ref[...] + r * (dn - n * jnp.mean(dn * n, axis=-1, keepdims=True))
        dh_ref[...] = dh
        if tail:
            dmo_ref, dgt_ref = rest[4:]
            dmo_ref[...] = (gt_ref[...] * dh).astype(BF16)
            _accum(dgt_ref, _colsum(dh * m_ref[...]), i)

    ins = [dz, h, dh_in, g, scale]
    in_specs = [_row(D, tm=tm)] * 3 + [_vec(D)] * 2
    out_specs = [_row(D, tm=tm), _vec(D), _vec(D), _vec(D)]
    out_shape = [jax.ShapeDtypeStruct((L, D), F32)] + [jax.ShapeDtypeStruct((1, D), F32)] * 3
    if tail:
        ins += [gate, m_out]
        in_specs += [_vec(D), _row(D, tm=tm)]
        out_specs += [_row(D, tm=tm), _vec(D)]
        out_shape += [jax.ShapeDtypeStruct((L, D), BF16), jax.ShapeDtypeStruct((1, D), F32)]
    return pl.pallas_call(
        body, name=name, grid=(L // tm,), in_specs=in_specs, out_specs=tuple(out_specs), out_shape=tuple(out_shape),
        compiler_params=_params(("arbitrary",)),
    )(*ins)


def merge_bwd(dmerged, proj, col_gc, y_conv, ga, gb):
    L, D = y_conv.shape
    tm = _tm(L)
    h = D // 2
    c0 = col_gc // h

    def body(dm_ref, p0, p1, p2, p3, yc_ref, ga_ref, gb_ref, dyc_ref, dga_ref, dgb_ref, dg_ref):
        gc, gs = (p0, p1), (p2, p3)
        for s in range(2):
            cols = slice(s * h, (s + 1) * h)
            dm = dm_ref[:, cols].astype(F32)
            sc, ss, sb = _sigmoid(gc[s][...]), _sigmoid(gs[s][...]), _sigmoid(gb_ref[:, cols].astype(F32))
            gav = ga_ref[:, cols].astype(F32)
            dyc_ref[:, cols] = (dm * sc).astype(BF16)
            dg_ref[:, cols] = (dm * yc_ref[:, cols].astype(F32) * sc * (1.0 - sc)).astype(BF16)
            dg_ref[:, D + s * h:D + (s + 1) * h] = (dm * gav * sb * ss * (1.0 - ss)).astype(BF16)
            dys = dm * ss
            dga_ref[:, cols] = (dys * sb).astype(BF16)
            dgb_ref[:, cols] = (dys * gav * sb * (1.0 - sb)).astype(BF16)

    return pl.pallas_call(
        body, name="merge_bwd", grid=(L // tm,),
        in_specs=[_row(D, tm=tm)] + [_row(h, c0 + s, tm) for s in range(4)] + [_row(D, tm=tm)] * 3,
        out_specs=(_row(D, tm=tm), _row(D, tm=tm), _row(D, tm=tm), _row(2 * D, tm=tm)),
        out_shape=(jax.ShapeDtypeStruct((L, D), BF16),) * 3 + (jax.ShapeDtypeStruct((L, 2 * D), BF16),),
        compiler_params=_params(("parallel",)),
    )(dmerged, proj, proj, proj, proj, y_conv, ga, gb)


def conv_ln_bwd(dvs, vc, ln_g, ln_b):
    L, C = vc.shape
    tm = _tm(L)

    def body(d_ref, v_ref, g_ref, b_ref, o_ref, dg_ref, db_ref):
        i = pl.program_id(0)
        v = v_ref[...]
        mu = jnp.mean(v, axis=-1, keepdims=True)
        d = v - mu
        rstd = lax.rsqrt(jnp.mean(d * d, axis=-1, keepdims=True) + EPS)
        xh = d * rstd
        ln = xh * g_ref[...] + b_ref[...]
        sg = _sigmoid(ln)
        dln = d_ref[...].astype(F32) * sg * (1.0 + ln * (1.0 - sg))
        _accum(dg_ref, _colsum(dln * xh), i)
        _accum(db_ref, _colsum(dln), i)
        dxh = dln * g_ref[...]
        o_ref[...] = rstd * (dxh - jnp.mean(dxh, axis=-1, keepdims=True)
                             - xh * jnp.mean(dxh * xh, axis=-1, keepdims=True))

    return pl.pallas_call(
        body, name="conv_ln_bwd", grid=(L // tm,),
        in_specs=[_row(C, tm=tm), _row(C, tm=tm), _vec(C), _vec(C)],
        out_specs=(_row(C, tm=tm), _vec(C), _vec(C)),
        out_shape=(jax.ShapeDtypeStruct((L, C), F32), jax.ShapeDtypeStruct((1, C), F32), jax.ShapeDtypeStruct((1, C), F32)),
        compiler_params=_params(("arbitrary",)),
    )(dvs, vc, ln_g, ln_b)


def conv_bwd(dvc, proj, w_dw, dproj_s, dproj_g):
    L, C = dvc.shape
    ws, wgt = dproj_s.shape[1], dproj_g.shape[1]
    tm = _tm(L)
    hb = tm // HALO
    last = L // HALO - 1
    nt = L // tm

    def body(d_ref, dn_ref, a_ref, g_ref, ah_ref, gh_ref, w_ref, ps_ref, pg_ref, o_ref, dw_ref, db_ref,
             dbuf, vbuf, dsh, vsh, dw8):
        i = pl.program_id(0)
        o_ref[:, 2 * C:2 * C + ws] = ps_ref[...]
        o_ref[:, 2 * C + ws:2 * C + ws + wgt] = pg_ref[...]
        dcur = d_ref[...]
        dbuf[0:tm, :] = dcur
        dbuf[tm:tm + HALO, :] = dn_ref[...] * jnp.where(i < nt - 1, 1.0, 0.0)
        av, sg = a_ref[...], _sigmoid(g_ref[...])
        vbuf[0:HALO, :] = ah_ref[...] * _sigmoid(gh_ref[...]) * jnp.where(i > 0, 1.0, 0.0)
        vbuf[HALO:HALO + tm, :] = av * sg
        _shifted_copies(dbuf, dsh, tm)
        _shifted_copies(vbuf, vsh, tm)
        @pl.when(i == 0)
        def _():
            dw8[...] = jnp.zeros_like(dw8)

        rc, lc = min(CONV_CHUNK[0], tm), min(CONV_CHUNK[1], C)
        for c0 in range(0, C, lc):
            cols = slice(c0, c0 + lc)
            for r0 in range(0, tm, rc):
                rows = slice(r0, r0 + rc)
                dv = jnp.zeros((rc, lc), F32)
                for k in range(CONV_K):
                    dv = dv + w_ref[k:k + 1, cols] * _window(dbuf, dsh, CONV_K - 1 - k, rc, r0, cols)
                a_c, s_c = a_ref[rows, cols], _sigmoid(g_ref[rows, cols])
                o_ref[rows, c0:c0 + lc] = (dv * s_c).astype(BF16)
                o_ref[rows, C + c0:C + c0 + lc] = (dv * a_c * s_c * (1.0 - s_c)).astype(BF16)
            for k in range(CONV_K):
                part = jnp.zeros((8, lc), F32)
                for r0 in range(0, tm, rc):
                    prod = d_ref[r0:r0 + rc, cols] * _window(vbuf, vsh, HALO - (CONV_K - 1) + k, rc, r0, cols)
                    for j in range(rc // 8):
                        part = part + prod[8 * j:8 * j + 8, :]
                dw8[k, :, cols] += part
        _accum(db_ref, _colsum(dcur), i)

        @pl.when(i == nt - 1)
        def _():
            for k in range(CONV_K):
                dw_ref[k:k + 1, :] = _colsum(dw8[k])
            dw_ref[CONV_K:HALO, :] = jnp.zeros((HALO - CONV_K, C), F32)

    prev = lambda cb: pl.BlockSpec((HALO, C), lambda i: (jnp.maximum(i * hb - 1, 0), cb))
    return pl.pallas_call(
        body, name="conv_bwd", grid=(nt,),
        in_specs=[_row(C, tm=tm), pl.BlockSpec((HALO, C), lambda i: (jnp.minimum((i + 1) * hb, last), 0)),
                  _row(C, 0, tm), _row(C, 1, tm), prev(0), prev(1), pl.BlockSpec((HALO, C), lambda i: (0, 0)),
                  _row(ws, tm=tm), _row(wgt, tm=tm)],
        out_specs=(_row(2 * C + ws + wgt, tm=tm), pl.BlockSpec((HALO, C), lambda i: (0, 0)), _vec(C)),
        out_shape=(jax.ShapeDtypeStruct((L, 2 * C + ws + wgt), BF16), jax.ShapeDtypeStruct((HALO, C), F32),
                   jax.ShapeDtypeStruct((1, C), F32)),
        scratch_shapes=[pltpu.VMEM((tm + HALO, C), F32), pltpu.VMEM((HALO + tm, C), F32),
                        pltpu.VMEM((7, HALO + tm - 8, C), F32), pltpu.VMEM((7, HALO + tm - 8, C), F32),
                        pltpu.VMEM((CONV_K, 8, C), F32)],
        compiler_params=_params(("arbitrary",)),
    )(dvc, dvc, proj, proj, proj, proj, w_dw, dproj_s, dproj_g)


def _adamw(w, g, m, v):
    m = ADAM_B1 * m + (1.0 - ADAM_B1) * g
    v = ADAM_B2 * v + (1.0 - ADAM_B2) * (g * g)
    m_hat = m / (1.0 - ADAM_B1 ** ADAM_STEP)
    v_hat = v / (1.0 - ADAM_B2 ** ADAM_STEP)
    delta = -ADAM_LR * (m_hat / (jnp.sqrt(v_hat) + ADAM_EPS) + ADAM_WD * w)
    return delta, m, v


def _tile_rows(R, C):
    tr = 8
    while tr * 2 * C <= 128 * 1024 and R % (tr * 2) == 0:
        tr *= 2
    assert R % tr == 0, (R, C)
    return tr


def sum_devices(parts, name):
    _, R, C = parts.shape
    tr = _tile_rows(R, C)

    def body(p_ref, o_ref):
        s = p_ref[0]
        for j in range(1, NDEV):
            s = s + p_ref[j]
        o_ref[...] = s

    return pl.pallas_call(
        body, name=name, grid=(R // tr,),
        in_specs=[pl.BlockSpec((NDEV, tr, C), lambda i: (0, i, 0))],
        out_specs=pl.BlockSpec((tr, C), lambda i: (i, 0)), out_shape=jax.ShapeDtypeStruct((R, C), F32),
        compiler_params=_params(("parallel",)),
    )(parts)


def adam_many(items, name):
    n = len(items)

    def body(*refs):
        ins, outs = refs[:4 * n], refs[4 * n:]
        for i in range(n):
            w, g, m, v = [ins[4 * i + j][...] for j in range(4)]
            outs[3 * i][...], outs[3 * i + 1][...], outs[3 * i + 2][...] = _adamw(w, g, m, v)

    out_shape = [jax.ShapeDtypeStruct(w.shape, F32) for w, _, _, _ in items for _ in range(3)]
    res = pl.pallas_call(body, name=name, out_shape=tuple(out_shape), compiler_params=_params())(
        *[a for it in items for a in it])
    return [tuple(res[3 * i:3 * i + 3]) for i in range(n)]


def adam_reduce(parts, w, m, v, name):
    R, C = w.shape
    tr = _tile_rows(R, C)

    def body(p_ref, w_ref, m_ref, v_ref, g_ref, d_ref, mo_ref, vo_ref):
        g = p_ref[0].astype(F32)
        for j in range(1, NDEV):
            g = g + p_ref[j].astype(F32)
        g_ref[...] = g
        d, mm, vv = _adamw(w_ref[...], g, m_ref[...], v_ref[...])
        d_ref[...], mo_ref[...], vo_ref[...] = d, mm, vv

    spec = pl.BlockSpec((tr, C), lambda i: (i, 0))
    return pl.pallas_call(
        body, name=name, grid=(R // tr,),
        in_specs=[pl.BlockSpec((NDEV, tr, C), lambda i: (0, i, 0)), spec, spec, spec], out_specs=(spec,) * 4,
        out_shape=(jax.ShapeDtypeStruct((R, C), F32),) * 4, compiler_params=_params(("parallel",)),
    )(parts, w, m, v)


def adam_w_ada(c_act, dmod_cols, w, m, v):
    D, n = w.shape
    tn = 256

    def body(c_ref, dm_ref, w_ref, m_ref, v_ref, g_ref, d_ref, mo_ref, vo_ref):
        g = lax.dot_general(c_ref[...].astype(BF16), dm_ref[...].astype(BF16), (((0,), (0,)), ((), ())),
                            preferred_element_type=F32)
        g_ref[...] = g
        d, mm, vv = _adamw(w_ref[...], g, m_ref[...], v_ref[...])
        d_ref[...], mo_ref[...], vo_ref[...] = d, mm, vv

    spec = pl.BlockSpec((D, tn), lambda j: (0, j))
    return pl.pallas_call(
        body, name="adam_w_ada", grid=(n // tn,),
        in_specs=[pl.BlockSpec((NDEV, D), lambda j: (0, 0)), pl.BlockSpec((NDEV, tn), lambda j: (0, j)), spec, spec, spec],
        out_specs=(spec,) * 4, out_shape=(jax.ShapeDtypeStruct((D, n), F32),) * 4,
        compiler_params=_params(("parallel",)),
    )(c_act, dmod_cols, w, m, v)


def _block_diag(m):
    G, a, b = m.shape
    m4 = m.reshape(G // GB, GB, a, b)
    eye = jnp.eye(GB, dtype=m.dtype)
    return (m4[:, :, :, None, :] * eye[None, :, None, :, None]).reshape(G // GB, GB * a, GB * b)


def _diag_blocks(m, a, b):
    nb = m.shape[0]
    m5 = m.reshape(nb, GB, a, GB, b)
    on_diag = jnp.eye(GB, dtype=bool)[None, :, None, :, None]
    return jnp.where(on_diag, m5, 0.0).sum(axis=3).reshape(nb * GB, a, b)


def _flat_pad(parts, mult):
    flat = jnp.concatenate([p.reshape(-1) for p in parts])
    pad = (-flat.shape[0]) % mult
    return jnp.pad(flat, (0, pad))


def _split(flat, like):
    out, off = [], 0
    for p in like:
        out.append(flat[off:off + p.size].reshape(p.shape))
        off += p.size
    return out


def kernel(x, c, w_ada, b_ada, norm1_g, w_in, w_dw, b_dw, ln_g, ln_b, w_conv_out, a_re, a_im, log_dt, b_re, b_im, c_re, c_im, d_skip, w_glu_a, w_glu_b, w_out, norm2_g, w_ff1, w_ff2, final_g, loss_target, m_w_ada, m_b_ada, m_norm1_g, m_w_in, m_w_dw, m_b_dw, m_ln_g, m_ln_b, m_w_conv_out, m_a_re, m_a_im, m_log_dt, m_b_re, m_b_im, m_c_re, m_c_im, m_d_skip, m_w_glu_a, m_w_glu_b, m_w_out, m_norm2_g, m_w_ff1, m_w_ff2, m_final_g, v_w_ada, v_b_ada, v_norm1_g, v_w_in, v_w_dw, v_b_dw, v_ln_g, v_ln_b, v_w_conv_out, v_a_re, v_a_im, v_log_dt, v_b_re, v_b_im, v_c_re, v_c_im, v_d_skip, v_w_glu_a, v_w_glu_b, v_w_out, v_norm2_g, v_w_ff1, v_w_ff2, v_final_g):
    W = dict(w_ada=w_ada, b_ada=b_ada, norm1_g=norm1_g, w_in=w_in, w_dw=w_dw, b_dw=b_dw, ln_g=ln_g, ln_b=ln_b,
             w_conv_out=w_conv_out, a_re=a_re, a_im=a_im, log_dt=log_dt, b_re=b_re, b_im=b_im, c_re=c_re, c_im=c_im,
             d_skip=d_skip, w_glu_a=w_glu_a, w_glu_b=w_glu_b, w_out=w_out, norm2_g=norm2_g, w_ff1=w_ff1, w_ff2=w_ff2,
             final_g=final_g)
    Mo = dict(w_ada=m_w_ada, b_ada=m_b_ada, norm1_g=m_norm1_g, w_in=m_w_in, w_dw=m_w_dw, b_dw=m_b_dw, ln_g=m_ln_g,
              ln_b=m_ln_b, w_conv_out=m_w_conv_out, a_re=m_a_re, a_im=m_a_im, log_dt=m_log_dt, b_re=m_b_re, b_im=m_b_im,
              c_re=m_c_re, c_im=m_c_im, d_skip=m_d_skip, w_glu_a=m_w_glu_a, w_glu_b=m_w_glu_b, w_out=m_w_out,
              norm2_g=m_norm2_g, w_ff1=m_w_ff1, w_ff2=m_w_ff2, final_g=m_final_g)
    Vo = dict(w_ada=v_w_ada, b_ada=v_b_ada, norm1_g=v_norm1_g, w_in=v_w_in, w_dw=v_w_dw, b_dw=v_b_dw, ln_g=v_ln_g,
              ln_b=v_ln_b, w_conv_out=v_w_conv_out, a_re=v_a_re, a_im=v_a_im, log_dt=v_log_dt, b_re=v_b_re, b_im=v_b_im,
              c_re=v_c_re, c_im=v_c_im, d_skip=v_d_skip, w_glu_a=v_w_glu_a, w_glu_b=v_w_glu_b, w_out=v_w_out,
              norm2_g=v_norm2_g, w_ff1=v_w_ff1, w_ff2=v_w_ff2, final_g=v_final_g)
    names = list(W)

    me = _me()
    xs, tgt = x[0], loss_target[0]
    L, D = xs.shape
    CW = w_dw.shape[2] * NDEV
    G, P = a_re.shape[1], a_re.shape[2]
    H = b_re.shape[3]
    n_ada = w_ada.shape[2]

    (c_all,) = _exchange([c], "gather_c", True)
    b_cols = lax.dynamic_slice(b_ada, (0, me * n_ada), (1, n_ada))
    mod_cols, c_act = adaln_mod(c_all.reshape(NDEV, D), w_ada[0], b_cols)
    (mod_all,) = _exchange([mod_cols], "gather_mod", True)
    mod = lax.dynamic_slice(mod_all, (0, me, 0), (NDEV, 1, n_ada)).reshape(6, 1, D)
    shift1, scale1, gate1, shift2, scale2, gate2 = [mod[j] for j in range(6)]

    big = ["w_in", "w_conv_out", "w_glu_a", "w_glu_b", "w_out", "w_ff1", "w_ff2"]
    order = ["w_in", "w_dw"] + big[1:]
    shards = {k: W[k][0].astype(BF16) for k in big}
    shards["w_dw"] = jnp.pad(w_dw[0], ((0, HALO - CONV_K), (0, 0)))
    gather_handle = dict(zip(order, gather2_start([shards[k] for k in order], "gather_weights_start", mod_all)))

    def forward(ks, name, after):
        gather_handle.update(zip(ks, gather2_forward([gather_handle[k] for k in ks], name, after)))

    def weight(k, after):
        w = gather2_wait(gather_handle[k], "gather_wait_" + k, after)
        if k in ("w_out", "w_ff2"):
            w = w.reshape(1, w.shape[0] * w.shape[1], w.shape[2])
        elif k == "w_dw":
            w = w.transpose(1, 0, 2).reshape(HALO, CW)
        return w

    scatter_handle = {}

    def scatter(k, g):
        if g.shape[0] == 1:
            g = g.reshape(NDEV, -1, g.shape[2])
        (scatter_handle[k],), token = exchange_start([g], "scatter_start_" + k, False)
        return token

    big_out = {}

    def finish_weight(k, after):
        parts = exchange_wait(scatter_handle[k], after, "scatter_wait_" + k, False)
        big_out[k] = adam_reduce(parts, W[k][0], Mo[k][0], Vo[k][0], "adam_" + k)
        return big_out[k][1]

    u = prenorm(xs, norm1_g, scale1, shift1, "prenorm1")

    br2 = b_re[0].transpose(0, 2, 1).reshape(G * H, P)
    bi2 = b_im[0].transpose(0, 2, 1).reshape(G * H, P)
    ldt = log_dt[0].reshape(G, 1)
    expand = jnp.repeat(jnp.eye(G, dtype=F32), H, axis=0)
    lbr, lbi, bbr, bbi = s5_params(a_re[0], a_im[0], ldt, br2, bi2, expand)
    tabs = s5_tables(lbr.reshape(1, G * P), lbi.reshape(1, G * P))
    bdr, bdi = _block_diag(bbr.reshape(G, H, P)), _block_diag(bbi.reshape(G, H, P))
    cdr = _block_diag(c_re[0].transpose(0, 2, 1))
    cdi = _block_diag(c_im[0].transpose(0, 2, 1))
    cd2 = jnp.concatenate([cdr, -cdi], axis=1).astype(BF16)
    bdr3, bdi3 = _rhs3(bdr), _rhs3(bdi)
    bdt2 = jnp.concatenate([bdr.transpose(0, 2, 1), bdi.transpose(0, 2, 1)], axis=1).astype(BF16)
    cdrt3, cdit3 = _rhs3(cdr.transpose(0, 2, 1)), _rhs3(-cdi.transpose(0, 2, 1))

    forward(["w_in"], "gather_forward_in", (u, tabs, bdr3, bdi3, cd2, bdt2, cdrt3, cdit3))
    wg = {"w_in": weight("w_in", u)}
    proj = mm_nn(u, wg["w_in"], "in_proj")
    forward(["w_dw", "w_conv_out", "w_glu_a", "w_glu_b", "w_out"], "gather_forward_mix", proj)
    w_dw_full = weight("w_dw", proj)
    vs, vc = conv_fwd(proj, w_dw_full, b_dw, ln_g, ln_b)
    wg["w_conv_out"] = weight("w_conv_out", vs)
    y_conv = mm_nn(vs, wg["w_conv_out"], "conv_out", out_dtype=BF16)
    s_re, s_im, y_pre, yg = s5_fwd(proj, 2 * CW, bdr3, bdi3, cd2, tabs, d_skip)
    forward(["w_ff1"], "gather_forward_ff1", yg)
    wg["w_glu_a"] = weight("w_glu_a", yg)
    wg["w_glu_b"] = weight("w_glu_b", yg)
    ga = mm_nn(yg, wg["w_glu_a"], "glu_a", out_dtype=BF16)
    gb = mm_nn(yg, wg["w_glu_b"], "glu_b", out_dtype=BF16)
    merged = merge_fwd(proj, 3 * CW, y_conv, ga, gb)
    forward(["w_ff2"], "gather_forward_ff2", merged)
    wg["w_out"] = weight("w_out", merged)
    m_out = mm_nn(merged, wg["w_out"], "out_proj")
    h1, z = residual_norm(xs, m_out, gate1, norm2_g, scale2, shift2)
    wg["w_ff1"] = weight("w_ff1", z)
    act = mm_nn(z, wg["w_ff1"], "ff1", out_dtype=BF16, epi=lambda r: jnp.square(jnp.maximum(r, 0.0)))
    wg["w_ff2"] = weight("w_ff2", act)
    ff = mm_nn(act, wg["w_ff2"], "ff2")

    dh2, dff, loss_part, d_final_g, d_gate2 = loss_bwd(h1, ff, gate2, final_g.reshape(1, D), tgt)
    df = mm_nt(dff, wg["w_ff2"], "ff2_dx", out_dtype=BF16,
               epi=lambda r, a: r * (2.0 * jnp.sqrt(a.astype(F32))), extras=(act,))
    t = scatter("w_ff2", mm_tn(act, dff, 1, "ff2_dw", out_dtype=BF16))
    t = scatter("w_ff1", mm_tn(z, df, NDEV, "ff1_dw", out_dtype=BF16, dep=t))
    dz = mm_nt(df, wg["w_ff1"], "ff1_dx", out_dtype=BF16, dep=t)
    dh1, d_scale2, d_shift2, d_norm2_g, dmo, d_gate1 = norm_bwd(dz, h1, dh2, norm2_g, scale2, "norm2_bwd", gate1, m_out)
    t = scatter("w_out", mm_tn(merged, dmo, 1, "out_dw", out_dtype=BF16))
    dmerged = mm_nt(dmo, wg["w_out"], "out_dx", out_dtype=BF16, dep=t)
    dyc, dga, dgb, dproj_g = merge_bwd(dmerged, proj, 3 * CW, y_conv, ga, gb)
    t = scatter("w_conv_out", mm_tn(vs, dyc, NDEV, "conv_out_dw", out_dtype=BF16))
    t = scatter("w_glu_a", mm_tn(yg, dga, NDEV, "glu_a_dw", out_dtype=BF16, dep=t))
    t = scatter("w_glu_b", mm_tn(yg, dgb, NDEV, "glu_b_dw", out_dtype=BF16, dep=t))
    dvs = mm_nt(dyc, wg["w_conv_out"], "conv_out_dx", out_dtype=BF16, dep=t)
    dyg_a = mm_nt(dga, wg["w_glu_a"], "glu_a_dx", out_dtype=BF16, dep=t)
    dyg_b = mm_nt(dgb, wg["w_glu_b"], "glu_b_dx", out_dtype=BF16, dep=t)
    dvc, d_ln_g, d_ln_b = conv_ln_bwd(dvs, vc, ln_g, ln_b)
    dproj_s, d_d_skip, dcdr, dcdi, dbdr, dbdi, dlr8, dli8 = s5_bwd(
        dyg_a, dyg_b, y_pre, proj, 2 * CW, s_re, s_im, bdt2, cdrt3, cdit3, tabs, d_skip)
    dproj, d_w_dw, d_b_dw = conv_bwd(dvc, proj, w_dw_full, dproj_s, dproj_g)
    d_c_re = _diag_blocks(dcdr, H, P)
    d_c_im = _diag_blocks(dcdi, H, P)
    d_bbr = _diag_blocks(dbdr, H, P)
    d_bbi = _diag_blocks(dbdi, H, P)
    dlr = jnp.sum(dlr8, axis=0).reshape(G, P)
    dli = jnp.sum(dli8, axis=0).reshape(G, P)
    early_parts = [d_b_dw, d_ln_g, d_ln_b, dlr, dli, d_bbr, d_bbi, d_c_re, d_c_im, d_d_skip, d_norm2_g, d_final_g,
                   d_w_dw]
    pack_early = _flat_pad(early_parts, PACK).reshape(NDEV, -1, 1024)
    (early_scatter,), done = exchange_start([pack_early], "scatter_small_start", False)
    for k in ("w_ff2", "w_ff1", "w_out"):
        done = finish_weight(k, done)
    early_sum = sum_devices(exchange_wait(early_scatter, done, "scatter_small_wait", False), "sum_small_early")
    (early_gather,), done = exchange_start([early_sum], "gather_small_start", True)
    t = scatter("w_in", mm_tn(u, dproj, NDEV, "in_dw", out_dtype=BF16, dep=done))

    tot_early = exchange_wait(early_gather, t, "gather_small_wait", True)
    (g_b_dw, g_ln_g, g_ln_b, t_lr, t_li, t_bbr, t_bbi, g_c_re_t, g_c_im_t, g_d_skip,
     g_norm2_g, g_final_g, g_w_dw_full) = _split(tot_early.reshape(-1), early_parts)
    g_a_re, g_a_im, g_ldt, g_br2, g_bi2 = s5_params_bwd(
        a_re[0], a_im[0], ldt, br2, bi2, expand, t_lr, t_li, t_bbr.reshape(G * H, P), t_bbi.reshape(G * H, P))
    g_brt, g_bit = g_br2.reshape(G, H, P), g_bi2.reshape(G, H, P)
    g2 = {
        "b_dw": g_b_dw, "ln_g": g_ln_g, "ln_b": g_ln_b,
        "a_re": g_a_re, "a_im": g_a_im, "log_dt": g_ldt.reshape(1, G),
        "b_re": g_brt.transpose(0, 2, 1).reshape(G * P, H), "b_im": g_bit.transpose(0, 2, 1).reshape(G * P, H),
        "c_re": g_c_re_t.reshape(G * H, P), "c_im": g_c_im_t.reshape(G * H, P), "d_skip": g_d_skip,
        "norm2_g": g_norm2_g, "final_g": g_final_g,
        "w_dw": lax.dynamic_slice(g_w_dw_full, (0, me * (CW // NDEV)), (CONV_K, CW // NDEV)),
    }
    as2d = lambda k, a: a.reshape(g2[k].shape)
    grads, delta, new_m, new_v = {}, {}, {}, {}

    def adam_small(ks, name):
        outs = adam_many([(as2d(k, W[k]), g2[k], as2d(k, Mo[k]), as2d(k, Vo[k])) for k in ks], name)
        for k, o in zip(ks, outs):
            grads[k] = g2[k].reshape(W[k].shape)
            delta[k], new_m[k], new_v[k] = [a.reshape(W[k].shape) for a in o]
        return outs[0][0]

    after = adam_small([k for k in names if k in g2], "adam_small_early")
    for k in ("w_conv_out", "w_glu_a", "w_glu_b"):
        after = finish_weight(k, after)
    du = mm_nt(dproj, wg["w_in"], "in_dx", out_dtype=BF16, dep=after)
    grad_x, d_scale1, d_shift1, d_norm1_g = norm_bwd(du, xs, dh1, norm1_g, scale1, "norm1_bwd")

    dmod = jnp.concatenate([d_shift1, d_scale1, d_gate1, d_shift2, d_scale2, d_gate2], axis=1)
    late_parts = [dmod, d_norm1_g]
    pack_late = _flat_pad(late_parts, PACK).reshape(NDEV, -1, 1024)
    parts_late, dmod_from = _exchange([pack_late, dmod.reshape(NDEV, 1, n_ada)], "scatter_small_late", False)
    (tot_late,) = _exchange([sum_devices(parts_late, "sum_small_late")], "gather_small_late", True)
    g2["b_ada"], g2["norm1_g"] = _split(tot_late.reshape(-1), late_parts)
    adam_small(["b_ada", "norm1_g"], "adam_small_late")
    dmod_cols = dmod_from.reshape(NDEV, n_ada)

    g, d, mm, vv = adam_w_ada(c_act, dmod_cols, w_ada[0], m_w_ada[0], v_w_ada[0])
    grads["w_ada"], delta["w_ada"], new_m["w_ada"], new_v["w_ada"] = g[None], d[None], mm[None], vv[None]

    finish_weight("w_in", d)
    for k in big:
        g, d, mm, vv = big_out[k]
        grads[k], delta[k], new_m[k], new_v[k] = g[None], d[None], mm[None], vv[None]

    loss = lax.psum(loss_part[0, 0], ("x", "y", "c"))
    return (loss, grad_x[None], *[grads[k] for k in names], *[delta[k] for k in names],
            *[new_m[k] for k in names], *[new_v[k] for k in names])
```
